```python
import jax
import jax.numpy as jnp
from jax import lax
import numpy as np

D_MODEL = 1024
BATCH = 16
SEQ = 256
DEPTH = 2
DEC_BATCH = 2
DEC_SEQ = 4096
PAST_LEN = 256

GRID_W = 64
HEAD_DIM = 64
A_HEADS = 6
A_KV = 2
A_GRP = A_HEADS // A_KV
C_HEADS = 6
C_KV = 2
C_GRP = C_HEADS // C_KV
B_GROUPS = 4
B_GROUP_CH = 64
B_WIDTH = B_GROUPS * B_GROUP_CH
A_Q = A_HEADS * HEAD_DIM
A_KVW = A_KV * HEAD_DIM
C_Q = C_HEADS * HEAD_DIM
C_KVW = C_KV * HEAD_DIM
N_BRANCH = 3
WINDOW = 128
BLOCK = 128
CHUNK = 128
N_EXPERTS = 16
EXPERT_FF = 1024
CAP_FACTOR = 2
ROPE_THETA = 10000.0
ROPE_FREQS = HEAD_DIM // 4
EPS = 1e-6
SPLIT_SIZES = (A_Q, A_KVW, A_KVW, B_WIDTH, B_WIDTH, C_Q, C_KVW, C_KVW, N_BRANCH * D_MODEL)
SPLIT_IDX = tuple(int(i) for i in np.cumsum(SPLIT_SIZES)[:-1])
IN_WIDTH = sum(SPLIT_SIZES)

kernel_name = 'hybrid_diffusion_prefix_step'


def rmsnorm(x, g):
    xf = x.astype(jnp.float32)
    y = xf * lax.rsqrt(jnp.mean(xf * xf, axis=-1, keepdims=True) + EPS)
    return (y * g.astype(jnp.float32)).astype(x.dtype)


def axial_angles(n_tok):
    rows = n_tok // GRID_W
    r = jnp.repeat(jnp.arange(rows, dtype=jnp.float32), GRID_W)
    col = jnp.tile(jnp.arange(GRID_W, dtype=jnp.float32), rows)
    freq = ROPE_THETA ** (-jnp.arange(ROPE_FREQS, dtype=jnp.float32) / ROPE_FREQS)
    return r[:, None] * freq, col[:, None] * freq


def rope_1d(x, ang):
    x1, x2 = x[..., :ROPE_FREQS], x[..., ROPE_FREQS:]
    cos, sin = jnp.cos(ang), jnp.sin(ang)
    return jnp.concatenate([x1 * cos - x2 * sin, x1 * sin + x2 * cos], axis=-1)


def axial_rope(x, ang_r, ang_c):
    shp = (ang_r.shape[0],) + (1,) * (x.ndim - 3) + (ROPE_FREQS,)
    xf = x.astype(jnp.float32)
    half = HEAD_DIM // 2
    out = jnp.concatenate([rope_1d(xf[..., :half], ang_r.reshape(shp)),
                           rope_1d(xf[..., half:], ang_c.reshape(shp))], axis=-1)
    return out.astype(x.dtype)


def softmax_with_sink(s, sink):
    if sink is None:
        return jax.nn.softmax(s, axis=-1)
    sk = sink.astype(jnp.float32)[:, :, None, None]
    m = jnp.maximum(jnp.max(s, axis=-1, keepdims=True), sk)
    e = jnp.exp(s - m)
    return e / (jnp.sum(e, axis=-1, keepdims=True) + jnp.exp(sk - m))


def block_attention(q, k, v, sink):
    b, s, kv, g, hd = q.shape
    nb = s // BLOCK
    qb = jnp.moveaxis(q.reshape(b, nb, BLOCK, kv, g, hd), 1, 0)

    def one_block(qi):
        sc = jnp.einsum('bqkgd,btkd->bkgqt', qi, k).astype(jnp.float32) * (HEAD_DIM ** -0.5)
        p = softmax_with_sink(sc, sink).astype(v.dtype)
        return jnp.einsum('bkgqt,btkd->bqkgd', p, v)

    out = lax.map(one_block, qb)
    return jnp.moveaxis(out, 0, 1).reshape(b, s, kv, g, hd)


def windowed_attention(q, k, v, ck, cv, sink):
    b, s, kv, g, hd = q.shape
    nb = s // BLOCK
    pad = ((0, 0), (BLOCK, BLOCK), (0, 0), (0, 0))
    kp = jnp.pad(k, pad).reshape(b, nb + 2, BLOCK, kv, hd)
    vp = jnp.pad(v, pad).reshape(b, nb + 2, BLOCK, kv, hd)
    kb = jnp.concatenate([kp[:, :-2], kp[:, 1:-1], kp[:, 2:]], axis=2)
    vb = jnp.concatenate([vp[:, :-2], vp[:, 1:-1], vp[:, 2:]], axis=2)
    qb = q.reshape(b, nb, BLOCK, kv, g, hd)
    scale = HEAD_DIM ** -0.5
    s_loc = jnp.einsum('bnqkgd,bnskd->bnkgqs', qb, kb).astype(jnp.float32) * scale
    qi = jnp.arange(BLOCK)[:, None]
    si = jnp.arange(3 * BLOCK)[None, :]
    band = jnp.abs(si - BLOCK - qi) <= WINDOW
    kpos = jnp.arange(nb)[:, None] * BLOCK - BLOCK + jnp.arange(3 * BLOCK)[None, :]
    inside = (kpos >= 0) & (kpos < s)
    mask = band[None] & inside[:, None, :]
    s_loc = jnp.where(mask[None, :, None, None], s_loc, -jnp.inf)
    s_ctx = jnp.einsum('bnqkgd,btkd->bnkgqt', qb, ck).astype(jnp.float32) * scale
    p = softmax_with_sink(jnp.concatenate([s_loc, s_ctx], axis=-1), sink).astype(v.dtype)
    o = (jnp.einsum('bnkgqs,bnskd->bnqkgd', p[..., :3 * BLOCK], vb)
         + jnp.einsum('bnkgqt,btkd->bnqkgd', p[..., 3 * BLOCK:], cv))
    return o.reshape(b, s, kv, g, hd)


def chunk_gmlp(u, v, v_norm_g, w_s, b_s):
    b, n, _ = u.shape
    nc = n // CHUNK
    u = jax.nn.gelu(u).reshape(b, nc, CHUNK, B_GROUPS, B_GROUP_CH)
    v = rmsnorm(jax.nn.gelu(v).reshape(b, nc, CHUNK, B_GROUPS, B_GROUP_CH), v_norm_g.reshape(B_GROUPS, B_GROUP_CH))
    sv = jnp.einsum('gpq,bnqgc->bnpgc', w_s, v) + b_s.T[:, :, None]
    return (u * sv).reshape(b, n, B_WIDTH)


def expert_choice_moe(h, w_router, b_router, w_gate, w_up, w_down):
    b, n, d = h.shape
    cap = CAP_FACTOR * n // N_EXPERTS
    aff = jax.nn.softmax((h @ w_router + b_router).astype(jnp.float32), axis=-1)
    vals, idx = lax.top_k(jnp.swapaxes(aff, 1, 2), cap)
    xg = jax.vmap(lambda xb, ib: xb[ib])(h, idx)
    hh = jax.nn.silu(jnp.einsum('becd,edf->becf', xg, w_gate)) * jnp.einsum('becd,edf->becf', xg, w_up)
    out = jnp.einsum('becf,efd->becd', hh, w_down) * vals[..., None].astype(h.dtype)
    return jax.vmap(lambda ib, ob: jnp.zeros((n, d), ob.dtype).at[ib.reshape(-1)].add(ob.reshape(-1, d)))(idx, out)


def trunk_layer(x, cond, P, l, ctx):
    b, s, _ = x.shape
    mod = (jax.nn.silu(cond) @ P['w_mod'][l] + P['b_mod'][l])[:, None, :]
    sh1, sc1, g1, sh2, sc2, g2 = jnp.split(mod, 6, axis=-1)
    h = rmsnorm(x, P['norm1_g'][l]) * (1 + sc1) + sh1
    aq, ak, av, bu, bv, cq, ck, cv, gl = jnp.split(h @ P['w_in'][l], SPLIT_IDX, axis=-1)
    aq = rmsnorm(aq.reshape(b, s, A_KV, A_GRP, HEAD_DIM), P['a_q_norm'][l])
    ak = rmsnorm(ak.reshape(b, s, A_KV, HEAD_DIM), P['a_k_norm'][l])
    av = av.reshape(b, s, A_KV, HEAD_DIM)
    cq = rmsnorm(cq.reshape(b, s, C_KV, C_GRP, HEAD_DIM), P['c_q_norm'][l])
    ck = rmsnorm(ck.reshape(b, s, C_KV, HEAD_DIM), P['c_k_norm'][l])
    cv = cv.reshape(b, s, C_KV, HEAD_DIM)
    sink = P['a_sink'][l].reshape(A_KV, A_GRP)
    ob = chunk_gmlp(bu, bv, P['b_v_norm'][l], P['b_ws'][l], P['b_bs'][l])
    if ctx is None:
        oa = block_attention(aq, ak, av, sink)
        oc = block_attention(cq, ck, cv, None)
        new = (ak, av, ck, cv)
    else:
        cak, cav, cck, ccv = ctx
        ang_r, ang_c = axial_angles(s)
        oa = windowed_attention(axial_rope(aq, ang_r, ang_c), axial_rope(ak, ang_r, ang_c), av, cak, cav, sink)
        oc = block_attention(axial_rope(cq, ang_r, ang_c),
                             jnp.concatenate([axial_rope(ck, ang_r, ang_c), cck], axis=1),
                             jnp.concatenate([cv, ccv], axis=1), None)
        new = None
    gates = jax.nn.sigmoid(gl.astype(jnp.float32)).astype(x.dtype).reshape(b, s, N_BRANCH, D_MODEL)
    merged = (gates[:, :, 0] * (oa.reshape(b, s, A_Q) @ P['w_a_o'][l])
              + gates[:, :, 1] * (ob @ P['w_b_o'][l])
              + gates[:, :, 2] * (oc.reshape(b, s, C_Q) @ P['w_c_o'][l]))
    x = x + g1 * (merged @ P['w_out'][l])
    h2 = rmsnorm(x, P['norm2_g'][l]) * (1 + sc2) + sh2
    x = x + g2 * expert_choice_moe(h2, P['w_router'][l], P['b_router'][l], P['w_gate'][l], P['w_up'][l], P['w_down'][l])
    return x, new


def setup_inputs(seed: int = 0) -> dict:
    key = jax.random.key(seed)
    ks = iter(jax.random.split(key, 40))

    def nrm(shape, scale=1.0):
        return jax.random.normal(next(ks), shape, jnp.float32) * scale

    def gain(shape):
        return 1.0 + nrm(shape, 0.02)

    D, E, F = D_MODEL, N_EXPERTS, EXPERT_FF
    return {
        'x_prompt': nrm((BATCH, SEQ, D)),
        'x_sample': nrm((DEC_BATCH, DEC_SEQ, D)),
        'cache_a_k': nrm((DEC_BATCH, DEPTH, PAST_LEN, A_KV, HEAD_DIM)),
        'cache_a_v': nrm((DEC_BATCH, DEPTH, PAST_LEN, A_KV, HEAD_DIM)),
        'cache_c_k': nrm((DEC_BATCH, DEPTH, PAST_LEN, C_KV, HEAD_DIM)),
        'cache_c_v': nrm((DEC_BATCH, DEPTH, PAST_LEN, C_KV, HEAD_DIM)),
        'c': nrm((DEC_BATCH, D)),
        'c_ctx': nrm((D,)),
        'norm1_g': gain((DEPTH, D)),
        'w_mod': nrm((DEPTH, D, 6 * D), 0.5 * D ** -0.5),
        'b_mod': nrm((DEPTH, 6 * D), 0.01),
        'w_in': nrm((DEPTH, D, IN_WIDTH), D ** -0.5),
        'a_q_norm': gain((DEPTH, HEAD_DIM)),
        'a_k_norm': gain((DEPTH, HEAD_DIM)),
        'a_sink': nrm((DEPTH, A_HEADS), 0.5),
        'b_v_norm': gain((DEPTH, B_WIDTH)),
        'b_ws': nrm((DEPTH, B_GROUPS, CHUNK, CHUNK), CHUNK ** -0.5),
        'b_bs': gain((DEPTH, B_GROUPS, CHUNK)),
        'c_q_norm': gain((DEPTH, HEAD_DIM)),
        'c_k_norm': gain((DEPTH, HEAD_DIM)),
        'w_a_o': nrm((DEPTH, A_Q, D), A_Q ** -0.5),
        'w_b_o': nrm((DEPTH, B_WIDTH, D), B_WIDTH ** -0.5),
        'w_c_o': nrm((DEPTH, C_Q, D), C_Q ** -0.5),
        'w_out': nrm((DEPTH, D, D), D ** -0.5),
        'norm2_g': gain((DEPTH, D)),
        'w_router': nrm((DEPTH, D, E), D ** -0.5),
        'b_router': nrm((DEPTH, E), 0.01),
        'w_gate': nrm((DEPTH, E, D, F), D ** -0.5),
        'w_up': nrm((DEPTH, E, D, F), D ** -0.5),
        'w_down': nrm((DEPTH, E, F, D), F ** -0.5),
    }


def reference(x_prompt, x_sample, cache_a_k, cache_a_v, cache_c_k, cache_c_v, c, c_ctx,
              norm1_g, w_mod, b_mod, w_in, a_q_norm, a_k_norm, a_sink, b_v_norm, b_ws, b_bs,
              c_q_norm, c_k_norm, w_a_o, w_b_o, w_c_o, w_out, norm2_g, w_router, b_router,
              w_gate, w_up, w_down):
    P = dict(norm1_g=norm1_g, w_mod=w_mod, b_mod=b_mod, w_in=w_in, a_q_norm=a_q_norm, a_k_norm=a_k_norm,
             a_sink=a_sink, b_v_norm=b_v_norm, b_ws=b_ws, b_bs=b_bs, c_q_norm=c_q_norm, c_k_norm=c_k_norm,
             w_a_o=w_a_o, w_b_o=w_b_o, w_c_o=w_c_o, w_out=w_out, norm2_g=norm2_g, w_router=w_router,
             b_router=b_router, w_gate=w_gate, w_up=w_up, w_down=w_down)
    y_prompt = x_prompt
    a_ks, a_vs, c_ks, c_vs = [], [], [], []
    for l in range(DEPTH):
        y_prompt, (ak, av, ck, cv) = trunk_layer(y_prompt, c_ctx[None, :], P, l, None)
        a_ks.append(ak)
        a_vs.append(av)
        c_ks.append(ck)
        c_vs.append(cv)
    y_sample = x_sample
    for l in range(DEPTH):
        y_sample, _ = trunk_layer(y_sample, c, P, l,
                                  (cache_a_k[:, l], cache_a_v[:, l], cache_c_k[:, l], cache_c_v[:, l]))
    new_a_k = jnp.stack(a_ks, axis=1)
    new_a_v = jnp.stack(a_vs, axis=1)
    new_c_k = jnp.stack(c_ks, axis=1)
    new_c_v = jnp.stack(c_vs, axis=1)
    return (y_prompt, y_sample, new_a_k, new_a_v, new_c_k, new_c_v)
```

```python
import functools

import jax
import jax.numpy as jnp
from jax import lax
from jax.experimental import pallas as pl
from jax.experimental.pallas import tpu as pltpu

F32 = jnp.float32
BF16 = jnp.bfloat16
I32 = jnp.int32

D_MODEL = 1024
BATCH = 16
SEQ = 256
DEPTH = 2
DEC_BATCH = 2
DEC_SEQ = 4096
PAST_LEN = 256
GRID_W = 64
HEAD_DIM = 64
N_HEADS = 6
N_KV = 2
N_GRP = N_HEADS // N_KV
B_GROUPS = 4
B_GROUP_CH = 64
B_WIDTH = B_GROUPS * B_GROUP_CH
Q_W = N_HEADS * HEAD_DIM
KV_W = N_KV * HEAD_DIM
QK_W = Q_W + KV_W
N_BRANCH = 3
WINDOW = 128
BLOCK = 128
CHUNK = 128
N_EXPERTS = 16
EXPERT_FF = 1024
CAP_FACTOR = 2
ROPE_THETA = 10000.0
ROPE_FREQS = HEAD_DIM // 4
EPS = 1e-6
IN_WIDTH = 2 * (QK_W + KV_W) + 2 * B_WIDTH + N_BRANCH * D_MODEL

T_CTX = BATCH * SEQ
T_LAT = DEC_BATCH * DEC_SEQ
T_ALL = T_CTX + T_LAT
N_REQ = 1 + DEC_BATCH
CAP_CTX = CAP_FACTOR * SEQ // N_EXPERTS
CAP_LAT = CAP_FACTOR * DEC_SEQ // N_EXPERTS
ROWS_PER_EXPERT = BATCH * CAP_CTX + DEC_BATCH * CAP_LAT

LANE = 128
ROW_TILE = 512
FFN_ROW_TILE = 512
VMEM_LIMIT = 56 * 1024 * 1024
NEG_BIG = -1e30

OFF_A = 0
OFF_AV = OFF_A + QK_W
OFF_BU = OFF_AV + KV_W
OFF_BV = OFF_BU + B_WIDTH
OFF_C = OFF_BV + B_WIDTH
OFF_CV = OFF_C + QK_W
OFF_G = OFF_CV + KV_W


def _params(sem, vmem=VMEM_LIMIT):
    return pltpu.CompilerParams(dimension_semantics=sem, vmem_limit_bytes=vmem)


def _sigmoid(x):
    return 1.0 / (1.0 + jnp.exp(-x))


def _gelu_tanh(x):
    return 0.5 * x * (1.0 + jnp.tanh(0.7978845608028654 * (x + 0.044715 * (x * x * x))))


def _split_bf16(x):
    hi = x.astype(BF16)
    lo = (x - hi.astype(F32)).astype(BF16)
    return hi, lo


def _mod_kernel(c_ref, w_ref, b_ref, o_ref):
    c = c_ref[...]
    s_hi, s_lo = _split_bf16(c * _sigmoid(c))
    w_hi, w_lo = _split_bf16(w_ref[0])
    acc = jnp.dot(s_hi, w_hi, preferred_element_type=F32)
    acc += jnp.dot(s_lo, w_hi, preferred_element_type=F32)
    acc += jnp.dot(s_hi, w_lo, preferred_element_type=F32)
    o_ref[0] = acc + b_ref[0]


def _modulation(cond8, w_mod, b_mod):
    n_col = 6 * D_MODEL // D_MODEL
    return pl.pallas_call(
        _mod_kernel,
        grid=(DEPTH, n_col),
        in_specs=[
            pl.BlockSpec((8, D_MODEL), lambda l, j: (0, 0)),
            pl.BlockSpec((1, D_MODEL, D_MODEL), lambda l, j: (l, 0, j)),
            pl.BlockSpec((1, 1, D_MODEL), lambda l, j: (l, 0, j)),
        ],
        out_specs=pl.BlockSpec((1, 8, D_MODEL), lambda l, j: (l, 0, j)),
        out_shape=jax.ShapeDtypeStruct((DEPTH, 8, 6 * D_MODEL), F32),
        compiler_params=_params(("arbitrary", "arbitrary")),
        name="modulation",
    )(cond8, w_mod, b_mod.reshape(DEPTH, 1, 6 * D_MODEL))


def _group_sumsq(y, bd_ref):
    return jnp.dot((y * y).astype(BF16), bd_ref[...], preferred_element_type=F32)


def _in_kernel(x_ref, sc_ref, sh_ref, n1_ref, w_ref, cs_ref, sn_ref, ga_ref, gc_ref, gbv_ref, bd_qk_ref, bd_b_ref,
               qa_ref, ka_ref, va_ref, bu_ref, bv_ref, qc_ref, kc_ref, vc_ref, gt_ref):
    x = x_ref[...]
    ms = jnp.mean(x * x, axis=-1, keepdims=True)
    h = x * lax.rsqrt(ms + EPS) * n1_ref[...]
    h = h * (1.0 + sc_ref[...]) + sh_ref[...]
    hb = h.astype(BF16)
    tm = x.shape[0]

    def proj(c0, width):
        return jnp.dot(hb, w_ref[:, c0:c0 + width], preferred_element_type=F32)

    cs = jnp.concatenate([cs_ref[...]] * (QK_W // LANE), axis=1)
    sn = jnp.concatenate([sn_ref[...]] * (QK_W // LANE), axis=1)
    lane = lax.broadcasted_iota(I32, (tm, QK_W), 1)
    first_half = (lane & ROPE_FREQS) == 0

    def qk_post(y, gain_ref):
        yn = y * lax.rsqrt(_group_sumsq(y, bd_qk_ref) * (1.0 / HEAD_DIM) + EPS) * gain_ref[...]
        partner = jnp.where(first_half, pltpu.roll(yn, QK_W - ROPE_FREQS, 1), pltpu.roll(yn, ROPE_FREQS, 1))
        return yn * cs + partner * sn

    ya = qk_post(proj(OFF_A, QK_W), ga_ref)
    qa_ref[...] = ya[:, :Q_W].astype(BF16)
    ka_ref[...] = ya[:, Q_W:]
    va_ref[...] = proj(OFF_AV, KV_W)

    bu_ref[...] = _gelu_tanh(proj(OFF_BU, B_WIDTH)).astype(BF16)
    gv = _gelu_tanh(proj(OFF_BV, B_WIDTH))
    gvn = gv * lax.rsqrt(_group_sumsq(gv, bd_b_ref) * (1.0 / B_GROUP_CH) + EPS) * gbv_ref[...]
    bv_ref[...] = gvn.astype(BF16)

    yc = qk_post(proj(OFF_C, QK_W), gc_ref)
    qc_ref[...] = yc[:, :Q_W].astype(BF16)
    kc_ref[...] = yc[:, Q_W:]
    vc_ref[...] = proj(OFF_CV, KV_W)

    gate_chunk = 512
    for j in range(N_BRANCH * D_MODEL // gate_chunk):
        g = proj(OFF_G + j * gate_chunk, gate_chunk)
        gt_ref[:, j * gate_chunk:(j + 1) * gate_chunk] = _sigmoid(g).astype(BF16)


def _req_of_tile(i):
    return i // (T_CTX // ROW_TILE)


def _input_projection(x, sc1, sh1, n1, w_in_b, cs, sn, gain_a, gain_c, gain_bv, bd_qk, bd_b):
    tm = ROW_TILE
    row = lambda w: pl.BlockSpec((tm, w), lambda i: (i, 0))
    full = lambda a: pl.BlockSpec(a.shape, lambda i: (0,) * a.ndim)
    modspec = pl.BlockSpec((None, 1, D_MODEL), lambda i: (_req_of_tile(i), 0, 0))
    outs = [(Q_W, BF16), (KV_W, F32), (KV_W, F32), (B_WIDTH, BF16), (B_WIDTH, BF16),
            (Q_W, BF16), (KV_W, F32), (KV_W, F32), (N_BRANCH * D_MODEL, BF16)]
    return pl.pallas_call(
        _in_kernel,
        grid=(T_ALL // tm,),
        in_specs=[row(D_MODEL), modspec, modspec, full(n1), full(w_in_b), row(LANE), row(LANE),
                  full(gain_a), full(gain_c), full(gain_bv), full(bd_qk), full(bd_b)],
        out_specs=[row(w) for w, _ in outs],
        out_shape=[jax.ShapeDtypeStruct((T_ALL, w), dt) for w, dt in outs],
        compiler_params=_params(("arbitrary",)),
        name="input_projection",
    )(x, sc1, sh1, n1, w_in_b, cs, sn, gain_a, gain_c, gain_bv, bd_qk, bd_b)


def _stack_group_queries(q_ref, kv):
    return jnp.concatenate(
        [q_ref[0, :, (kv * N_GRP + g) * HEAD_DIM:(kv * N_GRP + g + 1) * HEAD_DIM] for g in range(N_GRP)], axis=0)


def _sink_column(sink_ref, kv, tq):
    return jnp.concatenate([jnp.full((tq, 1), sink_ref[kv * N_GRP + g], F32) for g in range(N_GRP)], axis=0)


def _qk(q, k):
    return lax.dot_general(q, k, (((1,), (1,)), ((), ())), preferred_element_type=F32)


def _store_heads(o_ref, o, kv, tq):
    for g in range(N_GRP):
        hd = kv * N_GRP + g
        o_ref[0, :, hd * HEAD_DIM:(hd + 1) * HEAD_DIM] = o[g * tq:(g + 1) * tq].astype(o_ref.dtype)


def _dense_attn_kernel(*refs, tq, n_keys, key_chunk, has_sink):
    if has_sink:
        q_ref, k_ref, v_ref, sink_ref, o_ref = refs
    else:
        q_ref, k_ref, v_ref, o_ref = refs
    rows = N_GRP * tq
    n_full = n_keys // key_chunk
    rem = n_keys - n_full * key_chunk
    for kv in range(N_KV):
        q = _stack_group_queries(q_ref, kv)
        lo, hi = kv * HEAD_DIM, (kv + 1) * HEAD_DIM

        def step(carry, kc, vc):
            m, l, acc = carry
            s = _qk(q, kc)
            m_new = jnp.maximum(m, jnp.max(s, axis=-1, keepdims=True))
            alpha = jnp.exp(m - m_new)
            p = jnp.exp(s - m_new)
            l = l * alpha + jnp.sum(p, axis=-1, keepdims=True)
            acc = acc * alpha + jnp.dot(p.astype(BF16), vc, preferred_element_type=F32)
            return m_new, l, acc

        if has_sink:
            m0 = _sink_column(sink_ref, kv, tq)
            l0 = jnp.ones((rows, 1), F32)
        else:
            m0 = jnp.full((rows, 1), NEG_BIG, F32)
            l0 = jnp.zeros((rows, 1), F32)
        carry = (m0, l0, jnp.zeros((rows, HEAD_DIM), F32))

        def body(c, carry):
            start = pl.multiple_of(c * key_chunk, key_chunk)
            return step(carry, k_ref[0, pl.ds(start, key_chunk), lo:hi], v_ref[0, pl.ds(start, key_chunk), lo:hi])

        if n_full == 1:
            carry = step(carry, k_ref[0, 0:key_chunk, lo:hi], v_ref[0, 0:key_chunk, lo:hi])
        else:
            carry = lax.fori_loop(0, n_full, body, carry)
        if rem:
            carry = step(carry, k_ref[0, n_full * key_chunk:n_keys, lo:hi], v_ref[0, n_full * key_chunk:n_keys, lo:hi])
        _, l, acc = carry
        _store_heads(o_ref, acc / l, kv, tq)


def _dense_attention(q, k, v, sink, *, tq, key_chunk):
    b, s, _ = q.shape
    n_keys = k.shape[1]
    in_specs = [pl.BlockSpec((1, tq, Q_W), lambda i, j: (i, j, 0)),
                pl.BlockSpec((1, n_keys, KV_W), lambda i, j: (i, 0, 0)),
                pl.BlockSpec((1, n_keys, KV_W), lambda i, j: (i, 0, 0))]
    args = [q, k, v]
    if sink is not None:
        in_specs.append(pl.BlockSpec(memory_space=pltpu.SMEM))
        args.append(sink)
    return pl.pallas_call(
        functools.partial(_dense_attn_kernel, tq=tq, n_keys=n_keys, key_chunk=key_chunk, has_sink=sink is not None),
        grid=(b, s // tq),
        in_specs=in_specs,
        out_specs=pl.BlockSpec((1, tq, Q_W), lambda i, j: (i, j, 0)),
        out_shape=jax.ShapeDtypeStruct((b, s, Q_W), BF16),
        compiler_params=_params(("arbitrary", "arbitrary")),
        name="dense_attention",
    )(*args)


def _window_attn_kernel(q_ref, kp_ref, kc_ref, kn_ref, vp_ref, vc_ref, vn_ref, ck_ref, cv_ref, sink_ref, o_ref, *, seq):
    i = pl.program_id(1)
    rows = N_GRP * BLOCK
    r = lax.broadcasted_iota(I32, (rows, 3 * BLOCK), 0)
    si = lax.broadcasted_iota(I32, (rows, 3 * BLOCK), 1)
    qpos = i * BLOCK + (r & (BLOCK - 1))
    kpos = (i - 1) * BLOCK + si
    dist = kpos - qpos
    valid = (jnp.abs(dist) <= WINDOW) & (kpos >= 0) & (kpos < seq)
    for kv in range(N_KV):
        lo, hi = kv * HEAD_DIM, (kv + 1) * HEAD_DIM
        q = _stack_group_queries(q_ref, kv)
        k_loc = jnp.concatenate([kp_ref[0, :, lo:hi], kc_ref[0, :, lo:hi], kn_ref[0, :, lo:hi]], axis=0)
        v_loc = jnp.concatenate([vp_ref[0, :, lo:hi], vc_ref[0, :, lo:hi], vn_ref[0, :, lo:hi]], axis=0)
        s_loc = jnp.where(valid, _qk(q, k_loc), NEG_BIG)
        s_ctx = _qk(q, ck_ref[0, :, lo:hi])
        sk = _sink_column(sink_ref, kv, BLOCK)
        m = jnp.maximum(jnp.maximum(jnp.max(s_loc, axis=-1, keepdims=True), jnp.max(s_ctx, axis=-1, keepdims=True)), sk)
        e_loc = jnp.where(valid, jnp.exp(s_loc - m), 0.0)
        e_ctx = jnp.exp(s_ctx - m)
        den = jnp.sum(e_loc, axis=-1, keepdims=True) + jnp.sum(e_ctx, axis=-1, keepdims=True) + jnp.exp(sk - m)
        o = (jnp.dot(e_loc.astype(BF16), v_loc, preferred_element_type=F32)
             + jnp.dot(e_ctx.astype(BF16), cv_ref[0, :, lo:hi], preferred_element_type=F32))
        _store_heads(o_ref, o / den, kv, BLOCK)


def _window_attention(q, k, v, ck, cv, sink):
    b, s, _ = q.shape
    nb = s // BLOCK
    blk = lambda f: pl.BlockSpec((1, BLOCK, KV_W), lambda i, j: (i, f(j), 0))
    prev, cur, nxt = (lambda j: jnp.maximum(j - 1, 0)), (lambda j: j), (lambda j: jnp.minimum(j + 1, nb - 1))
    ctx = pl.BlockSpec((1, PAST_LEN, KV_W), lambda i, j: (i, 0, 0))
    return pl.pallas_call(
        functools.partial(_window_attn_kernel, seq=s),
        grid=(b, nb),
        in_specs=[pl.BlockSpec((1, BLOCK, Q_W), lambda i, j: (i, j, 0)),
                  blk(prev), blk(cur), blk(nxt), blk(prev), blk(cur), blk(nxt), ctx, ctx,
                  pl.BlockSpec(memory_space=pltpu.SMEM)],
        out_specs=pl.BlockSpec((1, BLOCK, Q_W), lambda i, j: (i, j, 0)),
        out_shape=jax.ShapeDtypeStruct((b, s, Q_W), BF16),
        compiler_params=_params(("arbitrary", "arbitrary")),
        name="window_attention",
    )(q, k, k, k, v, v, v, ck, cv, sink)


def _merge_kernel(x_ref, oa_ref, bu_ref, bv_ref, oc_ref, gt_ref, wa_ref, wb_ref, wc_ref, wo_ref, ws_ref, bs_ref,
                  g1_ref, sc2_ref, sh2_ref, n2_ref, wr_ref, br_ref, x1_ref, h2_ref, afft_ref):
    tm = x_ref.shape[0]
    group = lax.broadcasted_iota(I32, (CHUNK, B_WIDTH), 1) // B_GROUP_CH
    obs = []
    for c in range(tm // CHUNK):
        v = bv_ref[c * CHUNK:(c + 1) * CHUNK, :]
        sv = jnp.zeros((CHUNK, B_WIDTH), F32)
        for g in range(B_GROUPS):
            sv = jnp.where(group == g, jnp.dot(ws_ref[g], v, preferred_element_type=F32), sv)
        u = bu_ref[c * CHUNK:(c + 1) * CHUNK, :].astype(F32)
        obs.append((u * (sv + bs_ref[...])).astype(BF16))
    ob = jnp.concatenate(obs, axis=0)

    merged = gt_ref[:, 0:D_MODEL].astype(F32) * jnp.dot(oa_ref[...], wa_ref[...], preferred_element_type=F32)
    merged += gt_ref[:, D_MODEL:2 * D_MODEL].astype(F32) * jnp.dot(ob, wb_ref[...], preferred_element_type=F32)
    merged += gt_ref[:, 2 * D_MODEL:3 * D_MODEL].astype(F32) * jnp.dot(oc_ref[...], wc_ref[...],
                                                                      preferred_element_type=F32)
    y = jnp.dot(merged.astype(BF16), wo_ref[...], preferred_element_type=F32)
    x1 = x_ref[...] + g1_ref[...] * y
    x1_ref[...] = x1

    ms = jnp.mean(x1 * x1, axis=-1, keepdims=True)
    h2 = x1 * lax.rsqrt(ms + EPS) * n2_ref[...]
    h2 = h2 * (1.0 + sc2_ref[...]) + sh2_ref[...]
    h2_ref[...] = h2

    logits = jnp.dot(h2.astype(BF16), wr_ref[...], preferred_element_type=F32) + br_ref[...]
    e = jnp.exp(logits - jnp.max(logits, axis=-1, keepdims=True))
    aff = e / jnp.sum(e, axis=-1, keepdims=True)
    afft_ref[...] = aff.T[:N_EXPERTS, :]


def _merge(x, oa, bu, bv, oc, gt, wa, wb, wc, wo, ws, bs, g1, sc2, sh2, n2, wr, br):
    tm = ROW_TILE
    row = lambda w: pl.BlockSpec((tm, w), lambda i: (i, 0))
    full = lambda a: pl.BlockSpec(a.shape, lambda i: (0,) * a.ndim)
    modspec = pl.BlockSpec((None, 1, D_MODEL), lambda i: (_req_of_tile(i), 0, 0))
    return pl.pallas_call(
        _merge_kernel,
        grid=(T_ALL // tm,),
        in_specs=[row(D_MODEL), row(Q_W), row(B_WIDTH), row(B_WIDTH), row(Q_W), row(N_BRANCH * D_MODEL),
                  full(wa), full(wb), full(wc), full(wo), full(ws), full(bs),
                  modspec, modspec, modspec, full(n2), full(wr), full(br)],
        out_specs=[row(D_MODEL), row(D_MODEL), pl.BlockSpec((N_EXPERTS, tm), lambda i: (0, i))],
        out_shape=[jax.ShapeDtypeStruct((T_ALL, D_MODEL), F32), jax.ShapeDtypeStruct((T_ALL, D_MODEL), F32),
                   jax.ShapeDtypeStruct((N_EXPERTS, T_ALL), F32)],
        compiler_params=_params(("arbitrary",)),
        name="merge_router",
    )(x, oa, bu, bv, oc, gt, wa, wb, wc, wo, ws, bs, g1, sc2, sh2, n2, wr, br)


def _select_kernel(afft_ref, idx_ref, val_ref, possel_ref, *, n, cap, row_chunk):
    a = afft_ref[...]
    tok = lax.broadcasted_iota(I32, (N_EXPERTS, n), 1)

    def count(ones):
        return jnp.sum(ones, axis=1, keepdims=True)

    def at_least(word):
        return jnp.where(a >= pltpu.bitcast(word, F32), 1, 0)

    thr = jnp.zeros((N_EXPERTS, 1), I32)
    for bit in range(30, -1, -1):
        cand = thr | (1 << bit)
        thr = jnp.where(count(at_least(cand)) >= cap, cand, thr)
    above = at_least(thr + 1)
    tied = at_least(thr) - above
    need = cap - count(above)
    last = jnp.zeros((N_EXPERTS, 1), I32)
    for bit in range(n.bit_length() - 2, -1, -1):
        cand = last | (1 << bit)
        last = jnp.where(count(jnp.where(tok < cand, tied, 0)) < need, cand, last)
    sel = above + jnp.where(tok <= last, tied, 0)

    blk = min(n, 256)
    tri = jnp.where(lax.broadcasted_iota(I32, (blk, blk), 0) <= lax.broadcasted_iota(I32, (blk, blk), 1),
                    1.0, 0.0).astype(BF16)
    sel_f = sel.astype(F32)
    offset = jnp.zeros((N_EXPERTS, 1), F32)
    for j in range(n // blk):
        s_blk = sel_f[:, j * blk:(j + 1) * blk]
        incl = jnp.dot(s_blk.astype(BF16), tri, preferred_element_type=F32)
        pos = (incl - s_blk + offset).astype(I32)
        possel_ref[:, j * blk:(j + 1) * blk] = jnp.where(sel[:, j * blk:(j + 1) * blk] > 0, pos, -1)
        offset = offset + incl[:, blk - 1:blk]

    tok_row = lax.broadcasted_iota(I32, (1, n), 1)

    def per_expert(e, _):
        pe = possel_ref[pl.ds(e, 1), :]
        ae = afft_ref[pl.ds(e, 1), :]

        def per_chunk(c, _):
            r0 = pl.multiple_of(c * row_chunk, row_chunk)
            slot = lax.broadcasted_iota(I32, (row_chunk, 1), 0) + r0
            hit = pe == slot
            idx_ref[0, e, pl.ds(r0, row_chunk), :] = jnp.sum(jnp.where(hit, tok_row, 0), axis=1, keepdims=True)
            val_ref[0, e, pl.ds(r0, row_chunk), :] = jnp.sum(jnp.where(hit, ae, 0.0), axis=1, keepdims=True)
            return 0

        return lax.fori_loop(0, cap // row_chunk, per_chunk, 0)

    lax.fori_loop(0, N_EXPERTS, per_expert, 0)


def _select(afft, n_req, n, cap):
    row_chunk = min(cap, 64)
    return pl.pallas_call(
        functools.partial(_select_kernel, n=n, cap=cap, row_chunk=row_chunk),
        grid=(n_req,),
        in_specs=[pl.BlockSpec((N_EXPERTS, n), lambda b: (0, b))],
        out_specs=[pl.BlockSpec((1, N_EXPERTS, cap, 1), lambda b: (b, 0, 0, 0))] * 2,
        out_shape=[jax.ShapeDtypeStruct((n_req, N_EXPERTS, cap, 1), I32),
                   jax.ShapeDtypeStruct((n_req, N_EXPERTS, cap, 1), F32)],
        scratch_shapes=[pltpu.VMEM((N_EXPERTS, n), I32)],
        compiler_params=_params(("arbitrary",)),
        name="expert_select",
    )(afft)


def _gather_kernel(idx_ref, src_ref, out_ref, *, cap):
    base = (pl.program_id(0) * N_EXPERTS + pl.program_id(1)) * cap

    def body(r, _):
        row = idx_ref[base + r]
        out_ref[0, pl.ds(r, 1), :] = src_ref[0, pl.ds(row, 1), :]
        return 0

    lax.fori_loop(0, cap, body, 0, unroll=8)


def _gather(idx_flat, h2, cap):
    n_req, n, _ = h2.shape
    return pl.pallas_call(
        functools.partial(_gather_kernel, cap=cap),
        grid_spec=pltpu.PrefetchScalarGridSpec(
            num_scalar_prefetch=1,
            grid=(n_req, N_EXPERTS),
            in_specs=[pl.BlockSpec((1, n, D_MODEL), lambda b, e, idx: (b, 0, 0))],
            out_specs=pl.BlockSpec((1, cap, D_MODEL), lambda b, e, idx: (e, b, 0)),
        ),
        out_shape=jax.ShapeDtypeStruct((N_EXPERTS, n_req * cap, D_MODEL), F32),
        compiler_params=_params(("arbitrary", "arbitrary")),
        name="expert_gather",
    )(idx_flat, h2)


def _ffn_kernel(x_ref, v_ref, wg_ref, wu_ref, wd_ref, o_ref, wg_b, wu_b, wd_b):
    @pl.when(pl.program_id(1) == 0)
    def _():
        wg_b[...] = wg_ref[0].astype(BF16)
        wu_b[...] = wu_ref[0].astype(BF16)
        wd_b[...] = wd_ref[0].astype(BF16)

    x = x_ref[0].astype(BF16)
    g = jnp.dot(x, wg_b[...], preferred_element_type=F32)
    u = jnp.dot(x, wu_b[...], preferred_element_type=F32)
    hh = (g * _sigmoid(g)) * u
    y = jnp.dot(hh.astype(BF16), wd_b[...], preferred_element_type=F32)
    o_ref[0] = y * v_ref[0]


def _expert_ffn(xg, vals, w_gate, w_up, w_down, layer):
    _, rows, _ = xg.shape
    tr = FFN_ROW_TILE
    wspec = lambda k, n: pl.BlockSpec((None, 1, k, n), lambda e, j: (layer, e, 0, 0))
    return pl.pallas_call(
        _ffn_kernel,
        grid=(N_EXPERTS, rows // tr),
        in_specs=[pl.BlockSpec((1, tr, D_MODEL), lambda e, j: (e, j, 0)),
                  pl.BlockSpec((1, tr, 1), lambda e, j: (e, j, 0)),
                  wspec(D_MODEL, EXPERT_FF), wspec(D_MODEL, EXPERT_FF), wspec(EXPERT_FF, D_MODEL)],
        out_specs=pl.BlockSpec((1, tr, D_MODEL), lambda e, j: (e, j, 0)),
        out_shape=jax.ShapeDtypeStruct((N_EXPERTS, rows, D_MODEL), F32),
        scratch_shapes=[pltpu.VMEM((D_MODEL, EXPERT_FF), BF16), pltpu.VMEM((D_MODEL, EXPERT_FF), BF16),
                        pltpu.VMEM((EXPERT_FF, D_MODEL), BF16)],
        compiler_params=_params(("arbitrary", "arbitrary")),
        name="expert_ffn",
    )(xg, vals, w_gate, w_up, w_down)


def _scatter_kernel(idx_ref, y_ref, x1_ref, g2_ref, out_ref, *, cap):
    e = pl.program_id(2)

    @pl.when(e == 0)
    def _():
        out_ref[...] = jnp.zeros_like(out_ref)

    base = (pl.program_id(0) * N_EXPERTS + e) * cap

    def body(r, _):
        row = idx_ref[base + r]
        out_ref[0, pl.ds(row, 1), :] = out_ref[0, pl.ds(row, 1), :] + y_ref[0, pl.ds(r, 1), :]
        return 0

    lax.fori_loop(0, cap, body, 0, unroll=4)

    @pl.when(e == N_EXPERTS - 1)
    def _():
        out_ref[0] = x1_ref[0] + g2_ref[...] * out_ref[0]


def _scatter(idx_flat, yg, row0, x1, g2, req0, cap, n_split):
    n_req, n, _ = x1.shape
    w = D_MODEL // n_split
    blk0 = row0 // cap
    return pl.pallas_call(
        functools.partial(_scatter_kernel, cap=cap),
        grid_spec=pltpu.PrefetchScalarGridSpec(
            num_scalar_prefetch=1,
            grid=(n_req, n_split, N_EXPERTS),
            in_specs=[pl.BlockSpec((1, cap, w), lambda b, h, e, idx: (e, blk0 + b, h)),
                      pl.BlockSpec((1, n, w), lambda b, h, e, idx: (b, 0, h)),
                      pl.BlockSpec((None, 1, w), lambda b, h, e, idx: (req0(b), 0, h))],
            out_specs=pl.BlockSpec((1, n, w), lambda b, h, e, idx: (b, 0, h)),
        ),
        out_shape=jax.ShapeDtypeStruct((n_req, n, D_MODEL), F32),
        compiler_params=_params(("arbitrary", "arbitrary", "arbitrary")),
        name="expert_scatter",
    )(idx_flat, yg, x1, g2)


def _rope_tables():
    pos = jnp.arange(DEC_SEQ)
    r = (pos // GRID_W).astype(F32)
    col = (pos % GRID_W).astype(F32)
    freq = ROPE_THETA ** (-jnp.arange(ROPE_FREQS, dtype=F32) / ROPE_FREQS)
    ang_r, ang_c = r[:, None] * freq, col[:, None] * freq
    cos = jnp.concatenate([jnp.cos(ang_r)] * 2 + [jnp.cos(ang_c)] * 2, axis=-1)
    sin = jnp.concatenate([-jnp.sin(ang_r), jnp.sin(ang_r), -jnp.sin(ang_c), jnp.sin(ang_c)], axis=-1)
    cos = jnp.tile(cos, (DEC_BATCH, LANE // HEAD_DIM))
    sin = jnp.tile(sin, (DEC_BATCH, LANE // HEAD_DIM))
    cs = jnp.concatenate([jnp.ones((T_CTX, LANE), F32), cos], axis=0)
    sn = jnp.concatenate([jnp.zeros((T_CTX, LANE), F32), sin], axis=0)
    return cs, sn


def _block_diag_ones(width, group):
    g = jnp.arange(width) // group
    return (g[:, None] == g[None, :]).astype(BF16)


def _qk_gain(q_norm, k_norm):
    q = jnp.tile(q_norm, N_HEADS) * (HEAD_DIM ** -0.5)
    return jnp.concatenate([q, jnp.tile(k_norm, N_KV)])[None, :]


def kernel(x_prompt, x_sample, cache_a_k, cache_a_v, cache_c_k, cache_c_v, c, c_ctx, norm1_g, w_mod, b_mod, w_in,
           a_q_norm, a_k_norm, a_sink, b_v_norm, b_ws, b_bs, c_q_norm, c_k_norm, w_a_o, w_b_o, w_c_o, w_out, norm2_g,
           w_router, b_router, w_gate, w_up, w_down):
    cond8 = jnp.concatenate([c_ctx[None, :], c, jnp.zeros((8 - N_REQ, D_MODEL), F32)], axis=0)
    mods = _modulation(cond8, w_mod, b_mod)

    cs, sn = _rope_tables()
    bd_qk = _block_diag_ones(QK_W, HEAD_DIM)
    bd_b = _block_diag_ones(B_WIDTH, B_GROUP_CH)
    w_in_b = w_in.astype(BF16)
    wa_b, wb_b, wc_b, wo_b = w_a_o.astype(BF16), w_b_o.astype(BF16), w_c_o.astype(BF16), w_out.astype(BF16)
    ws_b = b_ws.astype(BF16)
    wr_pad = jnp.pad(w_router, ((0, 0), (0, 0), (0, LANE - N_EXPERTS))).astype(BF16)
    br_pad = jnp.pad(b_router, ((0, 0), (0, LANE - N_EXPERTS)), constant_values=NEG_BIG)

    x = jnp.concatenate([x_prompt.reshape(T_CTX, D_MODEL), x_sample.reshape(T_LAT, D_MODEL)], axis=0)
    new_kv = [[], [], [], []]
    for l in range(DEPTH):
        sh1, sc1, g1, sh2, sc2, g2 = [mods[l, :N_REQ, i * D_MODEL:(i + 1) * D_MODEL].reshape(N_REQ, 1, D_MODEL)
                                      for i in range(6)]
        qa, ka, va, bu, bv, qc, kc, vc, gt = _input_projection(
            x, sc1, sh1, norm1_g[l][None, :], w_in_b[l], cs, sn,
            _qk_gain(a_q_norm[l], a_k_norm[l]), _qk_gain(c_q_norm[l], c_k_norm[l]), b_v_norm[l][None, :], bd_qk, bd_b)
        for lst, arr in zip(new_kv, (ka, va, kc, vc)):
            lst.append(arr[:T_CTX].reshape(BATCH, SEQ, N_KV, HEAD_DIM))

        ctx3 = lambda a: a[:T_CTX].reshape(BATCH, SEQ, a.shape[-1])
        lat3 = lambda a: a[T_CTX:].reshape(DEC_BATCH, DEC_SEQ, a.shape[-1])
        sink = a_sink[l]
        oa_ctx = _dense_attention(ctx3(qa), ctx3(ka).astype(BF16), ctx3(va).astype(BF16), sink, tq=SEQ, key_chunk=SEQ)
        oc_ctx = _dense_attention(ctx3(qc), ctx3(kc).astype(BF16), ctx3(vc).astype(BF16), None, tq=SEQ, key_chunk=SEQ)
        flat_cache = lambda a: a[:, l].reshape(DEC_BATCH, PAST_LEN, KV_W).astype(BF16)
        oa_lat = _window_attention(lat3(qa), lat3(ka).astype(BF16), lat3(va).astype(BF16),
                                   flat_cache(cache_a_k), flat_cache(cache_a_v), sink)
        kc_all = jnp.concatenate([lat3(kc).astype(BF16), flat_cache(cache_c_k)], axis=1)
        vc_all = jnp.concatenate([lat3(vc).astype(BF16), flat_cache(cache_c_v)], axis=1)
        oc_lat = _dense_attention(lat3(qc), kc_all, vc_all, None, tq=256, key_chunk=512)
        oa = jnp.concatenate([oa_ctx.reshape(T_CTX, Q_W), oa_lat.reshape(T_LAT, Q_W)], axis=0)
        oc = jnp.concatenate([oc_ctx.reshape(T_CTX, Q_W), oc_lat.reshape(T_LAT, Q_W)], axis=0)

        bs_full = jnp.repeat(b_bs[l].T, B_GROUP_CH, axis=1)
        x1, h2, afft = _merge(x, oa, bu, bv, oc, gt, wa_b[l], wb_b[l], wc_b[l], wo_b[l], ws_b[l], bs_full,
                              g1, sc2, sh2, norm2_g[l][None, :], wr_pad[l], br_pad[l][None, :])

        idx_c, val_c = _select(afft[:, :T_CTX], BATCH, SEQ, CAP_CTX)
        idx_l, val_l = _select(afft[:, T_CTX:], DEC_BATCH, DEC_SEQ, CAP_LAT)
        idx_c, idx_l = idx_c.reshape(-1), idx_l.reshape(-1)
        xg = jnp.concatenate([_gather(idx_c, h2[:T_CTX].reshape(BATCH, SEQ, D_MODEL), CAP_CTX),
                              _gather(idx_l, h2[T_CTX:].reshape(DEC_BATCH, DEC_SEQ, D_MODEL), CAP_LAT)], axis=1)
        per_expert = lambda v, n_req, cap: v.reshape(n_req, N_EXPERTS, cap).transpose(1, 0, 2).reshape(
            N_EXPERTS, n_req * cap)
        vals = jnp.concatenate([per_expert(val_c, BATCH, CAP_CTX), per_expert(val_l, DEC_BATCH, CAP_LAT)], axis=1)
        yg = _expert_ffn(xg, vals[:, :, None], w_gate, w_up, w_down, l)

        y_ctx = _scatter(idx_c, yg, 0, x1[:T_CTX].reshape(BATCH, SEQ, D_MODEL), g2, lambda b: 0, CAP_CTX, 1)
        y_lat = _scatter(idx_l, yg, BATCH * CAP_CTX, x1[T_CTX:].reshape(DEC_BATCH, DEC_SEQ, D_MODEL), g2,
                         lambda b: b + 1, CAP_LAT, 2)
        x = jnp.concatenate([y_ctx.reshape(T_CTX, D_MODEL), y_lat.reshape(T_LAT, D_MODEL)], axis=0)

    y_prompt = x[:T_CTX].reshape(BATCH, SEQ, D_MODEL)
    y_sample = x[T_CTX:].reshape(DEC_BATCH, DEC_SEQ, D_MODEL)
    return (y_prompt, y_sample) + tuple(jnp.stack(lst, axis=1) for lst in new_kv)
```

```python
import functools

import jax
import jax.numpy as jnp
from jax import lax
from jax.experimental import pallas as pl
from jax.experimental.pallas import tpu as pltpu

F32 = jnp.float32
BF16 = jnp.bfloat16
I32 = jnp.int32

D_MODEL = 1024
BATCH = 16
SEQ = 256
DEPTH = 2
DEC_BATCH = 2
DEC_SEQ = 4096
PAST_LEN = 256
GRID_W = 64
HEAD_DIM = 64
N_HEADS = 6
N_KV = 2
N_GRP = N_HEADS // N_KV
B_GROUPS = 4
B_GROUP_CH = 64
B_WIDTH = B_GROUPS * B_GROUP_CH
Q_W = N_HEADS * HEAD_DIM
KV_W = N_KV * HEAD_DIM
QK_W = Q_W + KV_W
N_BRANCH = 3
WINDOW = 128
BLOCK = 128
CHUNK = 128
N_EXPERTS = 16
EXPERT_FF = 1024
CAP_FACTOR = 2
ROPE_THETA = 10000.0
ROPE_FREQS = HEAD_DIM // 4
EPS = 1e-6
IN_WIDTH = 2 * (QK_W + KV_W) + 2 * B_WIDTH + N_BRANCH * D_MODEL

T_CTX = BATCH * SEQ
T_LAT = DEC_BATCH * DEC_SEQ
T_ALL = T_CTX + T_LAT
N_REQ = 1 + DEC_BATCH
CAP_CTX = CAP_FACTOR * SEQ // N_EXPERTS
CAP_LAT = CAP_FACTOR * DEC_SEQ // N_EXPERTS
ROWS_PER_EXPERT = BATCH * CAP_CTX + DEC_BATCH * CAP_LAT

LANE = 128
ROW_TILE = 512
N_CTX_TILES = T_CTX // ROW_TILE
FFN_ROW_TILE = 512
VMEM_LIMIT = 56 * 1024 * 1024
NEG_BIG = -1e30
LOG2_E = 1.4426950408889634

OFF_A = 0
OFF_AV = OFF_A + QK_W
OFF_BU = OFF_AV + KV_W
OFF_BV = OFF_BU + B_WIDTH
OFF_C = OFF_BV + B_WIDTH
OFF_CV = OFF_C + QK_W
OFF_G = OFF_CV + KV_W


def _params(sem, vmem=VMEM_LIMIT):
    return pltpu.CompilerParams(dimension_semantics=sem, vmem_limit_bytes=vmem)


def _sigmoid(x):
    return 1.0 / (1.0 + jnp.exp(-x))


def _gelu_tanh(x):
    return 0.5 * x * (1.0 + jnp.tanh(0.7978845608028654 * (x + 0.044715 * (x * x * x))))


def _split_bf16(x):
    hi = x.astype(BF16)
    lo = (x - hi.astype(F32)).astype(BF16)
    return hi, lo


def _mod_kernel(c_ref, w_ref, b_ref, o_ref):
    c = c_ref[...]
    s_hi, s_lo = _split_bf16(c * _sigmoid(c))
    w_hi, w_lo = _split_bf16(w_ref[0])
    acc = jnp.dot(s_hi, w_hi, preferred_element_type=F32)
    acc += jnp.dot(s_lo, w_hi, preferred_element_type=F32)
    acc += jnp.dot(s_hi, w_lo, preferred_element_type=F32)
    o_ref[0] = acc + b_ref[0]


def _modulation(cond8, w_mod, b_mod):
    n_col = 6 * D_MODEL // D_MODEL
    return pl.pallas_call(
        _mod_kernel,
        grid=(DEPTH, n_col),
        in_specs=[
            pl.BlockSpec((8, D_MODEL), lambda l, j: (0, 0)),
            pl.BlockSpec((1, D_MODEL, D_MODEL), lambda l, j: (l, 0, j)),
            pl.BlockSpec((1, 1, D_MODEL), lambda l, j: (l, 0, j)),
        ],
        out_specs=pl.BlockSpec((1, 8, D_MODEL), lambda l, j: (l, 0, j)),
        out_shape=jax.ShapeDtypeStruct((DEPTH, 8, 6 * D_MODEL), F32),
        compiler_params=_params(("arbitrary", "arbitrary")),
        name="modulation",
    )(cond8, w_mod, b_mod.reshape(DEPTH, 1, 6 * D_MODEL))


def _group_sumsq(y, bd_ref):
    return jnp.dot((y * y).astype(BF16), bd_ref[...], preferred_element_type=F32)


def _pick_pass(i, ctx_ref, lat_ref):
    return jnp.where(i < N_CTX_TILES, ctx_ref[...], lat_ref[...])


def _in_kernel(xc_ref, xl_ref, sc_ref, sh_ref, n1_ref, w_ref, cs_ref, sn_ref, ga_ref, gc_ref, gbv_ref, bd_qk_ref,
               bd_b_ref, qa_ref, ka_ref, va_ref, bu_ref, bv_ref, qc_ref, kc_ref, vc_ref, gt_ref):
    x = _pick_pass(pl.program_id(0), xc_ref, xl_ref)
    ms = jnp.mean(x * x, axis=-1, keepdims=True)
    h = x * lax.rsqrt(ms + EPS) * n1_ref[...]
    h = h * (1.0 + sc_ref[...]) + sh_ref[...]
    hb = h.astype(BF16)
    tm = x.shape[0]

    def proj(c0, width):
        return jnp.dot(hb, w_ref[:, c0:c0 + width], preferred_element_type=F32)

    cs = jnp.concatenate([cs_ref[...]] * (QK_W // LANE), axis=1)
    sn = jnp.concatenate([sn_ref[...]] * (QK_W // LANE), axis=1)
    lane = lax.broadcasted_iota(I32, (tm, QK_W), 1)
    first_half = (lane & ROPE_FREQS) == 0

    def qk_post(y, gain_ref):
        yn = y * lax.rsqrt(_group_sumsq(y, bd_qk_ref) * (1.0 / HEAD_DIM) + EPS) * gain_ref[...]
        partner = jnp.where(first_half, pltpu.roll(yn, QK_W - ROPE_FREQS, 1), pltpu.roll(yn, ROPE_FREQS, 1))
        return yn * cs + partner * sn

    ya = qk_post(proj(OFF_A, QK_W), ga_ref)
    qa_ref[...] = ya[:, :Q_W].astype(BF16)
    ka_ref[...] = ya[:, Q_W:]
    va_ref[...] = proj(OFF_AV, KV_W)

    bu_ref[...] = _gelu_tanh(proj(OFF_BU, B_WIDTH)).astype(BF16)
    gv = _gelu_tanh(proj(OFF_BV, B_WIDTH))
    gvn = gv * lax.rsqrt(_group_sumsq(gv, bd_b_ref) * (1.0 / B_GROUP_CH) + EPS) * gbv_ref[...]
    bv_ref[...] = gvn.astype(BF16)

    yc = qk_post(proj(OFF_C, QK_W), gc_ref)
    qc_ref[...] = yc[:, :Q_W].astype(BF16)
    kc_ref[...] = yc[:, Q_W:]
    vc_ref[...] = proj(OFF_CV, KV_W)

    gate_chunk = 512
    for j in range(N_BRANCH * D_MODEL // gate_chunk):
        g = proj(OFF_G + j * gate_chunk, gate_chunk)
        gt_ref[:, j * gate_chunk:(j + 1) * gate_chunk] = _sigmoid(g).astype(BF16)


def _req_of_tile(i):
    return i // N_CTX_TILES


def _ctx_rows(w):
    return pl.BlockSpec((ROW_TILE, w), lambda i: (jnp.minimum(i, N_CTX_TILES - 1), 0))


def _lat_rows(w):
    return pl.BlockSpec((ROW_TILE, w), lambda i: (jnp.maximum(i - N_CTX_TILES, 0), 0))


def _input_projection(x_ctx, x_lat, sc1, sh1, n1, w_in_b, cs, sn, gain_a, gain_c, gain_bv, bd_qk, bd_b):
    tm = ROW_TILE
    row = lambda w: pl.BlockSpec((tm, w), lambda i: (i, 0))
    full = lambda a: pl.BlockSpec(a.shape, lambda i: (0,) * a.ndim)
    modspec = pl.BlockSpec((None, 1, D_MODEL), lambda i: (_req_of_tile(i), 0, 0))
    outs = [(Q_W, BF16), (KV_W, F32), (KV_W, F32), (B_WIDTH, BF16), (B_WIDTH, BF16),
            (Q_W, BF16), (KV_W, F32), (KV_W, F32), (N_BRANCH * D_MODEL, BF16)]
    return pl.pallas_call(
        _in_kernel,
        grid=(T_ALL // tm,),
        in_specs=[_ctx_rows(D_MODEL), _lat_rows(D_MODEL), modspec, modspec, full(n1), full(w_in_b), row(LANE), row(LANE),
                  full(gain_a), full(gain_c), full(gain_bv), full(bd_qk), full(bd_b)],
        out_specs=[row(w) for w, _ in outs],
        out_shape=[jax.ShapeDtypeStruct((T_ALL, w), dt) for w, dt in outs],
        compiler_params=_params(("arbitrary",)),
        name="input_projection",
    )(x_ctx, x_lat, sc1, sh1, n1, w_in_b, cs, sn, gain_a, gain_c, gain_bv, bd_qk, bd_b)


def _stack_group_queries(q_ref, kv):
    return jnp.concatenate(
        [q_ref[0, :, (kv * N_GRP + g) * HEAD_DIM:(kv * N_GRP + g + 1) * HEAD_DIM] for g in range(N_GRP)], axis=0)


def _sink_column(sink_ref, kv, tq):
    return jnp.concatenate([jnp.full((tq, 1), sink_ref[kv * N_GRP + g], F32) for g in range(N_GRP)], axis=0)


def _qk(q, k):
    return lax.dot_general(q, k, (((1,), (1,)), ((), ())), preferred_element_type=F32)


def _store_heads(o_ref, o, kv, tq):
    for g in range(N_GRP):
        hd = kv * N_GRP + g
        o_ref[0, :, hd * HEAD_DIM:(hd + 1) * HEAD_DIM] = o[g * tq:(g + 1) * tq].astype(o_ref.dtype)


def _dense_attn_kernel(*refs, tq, n_keys, key_chunk, has_sink):
    if has_sink:
        q_ref, k_ref, vt_ref, sink_ref, o_ref, qt_scr, ot_scr = refs
    else:
        q_ref, k_ref, vt_ref, o_ref, qt_scr, ot_scr = refs
    n_full = n_keys // key_chunk
    rem = n_keys - n_full * key_chunk
    width = N_GRP * tq
    for j in range(Q_W // LANE):
        qt_scr[j * LANE:(j + 1) * LANE, :] = q_ref[0, :, j * LANE:(j + 1) * LANE].astype(F32).T.astype(BF16)
    for kv in range(N_KV):
        heads = [kv * N_GRP + g for g in range(N_GRP)]
        qt = jnp.concatenate([qt_scr[h * HEAD_DIM:(h + 1) * HEAD_DIM, :] for h in heads], axis=1)

        def step(carry, c0, size):
            m, l, acc = carry
            s = jnp.dot(k_ref[0, kv, pl.ds(c0, size), :], qt, preferred_element_type=F32)
            m_new = jnp.maximum(m, jnp.max(s, axis=0, keepdims=True))
            alpha = jnp.exp2(m - m_new)
            p = jnp.exp2(s - m_new)
            l = l * alpha + jnp.sum(p, axis=0, keepdims=True)
            acc = acc * alpha + jnp.dot(vt_ref[0, kv, :, pl.ds(c0, size)], p.astype(BF16),
                                        preferred_element_type=F32)
            return m_new, l, acc

        if has_sink:
            m0 = jnp.concatenate([jnp.full((1, tq), sink_ref[h] * LOG2_E, F32) for h in heads], axis=1)
            l0 = jnp.ones((1, width), F32)
        else:
            m0 = jnp.full((1, width), NEG_BIG, F32)
            l0 = jnp.zeros((1, width), F32)
        carry = (m0, l0, jnp.zeros((HEAD_DIM, width), F32))
        if n_full == 1:
            carry = step(carry, 0, key_chunk)
        else:
            carry = lax.fori_loop(0, n_full, lambda c, cr: step(cr, pl.multiple_of(c * key_chunk, key_chunk), key_chunk),
                                  carry)
        if rem:
            carry = step(carry, n_full * key_chunk, rem)
        _, l, acc = carry
        o = acc / l
        for g, h in enumerate(heads):
            ot_scr[h * HEAD_DIM:(h + 1) * HEAD_DIM, :] = o[:, g * tq:(g + 1) * tq]
    for j in range(Q_W // LANE):
        o_ref[0, :, j * LANE:(j + 1) * LANE] = ot_scr[j * LANE:(j + 1) * LANE, :].T.astype(o_ref.dtype)


def _dense_attention(q, k, v, sink, *, n_req, q_off, tq, key_chunk):
    s = q.shape[1]
    n_keys = k.shape[1]
    k4 = k.reshape(n_req, n_keys, N_KV, HEAD_DIM).transpose(0, 2, 1, 3)
    vt4 = v.reshape(n_req, n_keys, N_KV, HEAD_DIM).transpose(0, 2, 3, 1)
    in_specs = [pl.BlockSpec((1, tq, Q_W), lambda i, j: (q_off + i, j, 0)),
                pl.BlockSpec((1, N_KV, n_keys, HEAD_DIM), lambda i, j: (i, 0, 0, 0)),
                pl.BlockSpec((1, N_KV, HEAD_DIM, n_keys), lambda i, j: (i, 0, 0, 0))]
    args = [q, k4, vt4]
    if sink is not None:
        in_specs.append(pl.BlockSpec(memory_space=pltpu.SMEM))
        args.append(sink)
    return pl.pallas_call(
        functools.partial(_dense_attn_kernel, tq=tq, n_keys=n_keys, key_chunk=key_chunk, has_sink=sink is not None),
        grid=(n_req, s // tq),
        in_specs=in_specs,
        out_specs=pl.BlockSpec((1, tq, Q_W), lambda i, j: (i, j, 0)),
        out_shape=jax.ShapeDtypeStruct((n_req, s, Q_W), BF16),
        scratch_shapes=[pltpu.VMEM((Q_W, tq), BF16), pltpu.VMEM((Q_W, tq), F32)],
        compiler_params=_params(("arbitrary", "arbitrary")),
        name="dense_attention",
    )(*args)


def _window_attn_kernel(q_ref, kp_ref, kc_ref, kn_ref, vp_ref, vc_ref, vn_ref, ck_ref, cv_ref, sink_ref, o_ref, *, seq):
    i = pl.program_id(1)
    rows = N_GRP * BLOCK
    r = lax.broadcasted_iota(I32, (rows, 3 * BLOCK), 0)
    si = lax.broadcasted_iota(I32, (rows, 3 * BLOCK), 1)
    qpos = i * BLOCK + (r & (BLOCK - 1))
    kpos = (i - 1) * BLOCK + si
    dist = kpos - qpos
    valid = (jnp.abs(dist) <= WINDOW) & (kpos >= 0) & (kpos < seq)
    for kv in range(N_KV):
        lo, hi = kv * HEAD_DIM, (kv + 1) * HEAD_DIM
        q = _stack_group_queries(q_ref, kv)
        k_loc = jnp.concatenate([kp_ref[0, :, lo:hi], kc_ref[0, :, lo:hi], kn_ref[0, :, lo:hi]], axis=0)
        v_loc = jnp.concatenate([vp_ref[0, :, lo:hi], vc_ref[0, :, lo:hi], vn_ref[0, :, lo:hi]], axis=0)
        s_loc = jnp.where(valid, _qk(q, k_loc), NEG_BIG)
        s_ctx = _qk(q, ck_ref[0, :, lo:hi])
        sk = _sink_column(sink_ref, kv, BLOCK) * LOG2_E
        m = jnp.maximum(jnp.maximum(jnp.max(s_loc, axis=-1, keepdims=True), jnp.max(s_ctx, axis=-1, keepdims=True)), sk)
        e_loc = jnp.where(valid, jnp.exp2(s_loc - m), 0.0)
        e_ctx = jnp.exp2(s_ctx - m)
        den = jnp.sum(e_loc, axis=-1, keepdims=True) + jnp.sum(e_ctx, axis=-1, keepdims=True) + jnp.exp2(sk - m)
        o = (jnp.dot(e_loc.astype(BF16), v_loc, preferred_element_type=F32)
             + jnp.dot(e_ctx.astype(BF16), cv_ref[0, :, lo:hi], preferred_element_type=F32))
        _store_heads(o_ref, o / den, kv, BLOCK)


def _window_attention(q, k, v, ck, cv, sink, *, n_req, off):
    b, s = n_req, q.shape[1]
    nb = s // BLOCK
    blk = lambda f: pl.BlockSpec((1, BLOCK, KV_W), lambda i, j: (off + i, f(j), 0))
    prev, cur, nxt = (lambda j: jnp.maximum(j - 1, 0)), (lambda j: j), (lambda j: jnp.minimum(j + 1, nb - 1))
    ctx = pl.BlockSpec((1, PAST_LEN, KV_W), lambda i, j: (i, 0, 0))
    return pl.pallas_call(
        functools.partial(_window_attn_kernel, seq=s),
        grid=(b, nb),
        in_specs=[pl.BlockSpec((1, BLOCK, Q_W), lambda i, j: (off + i, j, 0)),
                  blk(prev), blk(cur), blk(nxt), blk(prev), blk(cur), blk(nxt), ctx, ctx,
                  pl.BlockSpec(memory_space=pltpu.SMEM)],
        out_specs=pl.BlockSpec((1, BLOCK, Q_W), lambda i, j: (i, j, 0)),
        out_shape=jax.ShapeDtypeStruct((b, s, Q_W), BF16),
        compiler_params=_params(("arbitrary", "arbitrary")),
        name="window_attention",
    )(q, k, k, k, v, v, v, ck, cv, sink)


def _pack_halves(x):
    half = x.shape[1] // 2
    return pltpu.pack_elementwise([x[:, :half], x[:, half:]], packed_dtype=BF16)


def _unpack_halves(words):
    return tuple(pltpu.unpack_elementwise(words, index=i, packed_dtype=BF16, unpacked_dtype=F32).astype(BF16)
                 for i in range(2))


def _merge_kernel(xc_ref, xl_ref, oac_ref, oal_ref, bu_ref, bv_ref, occ_ref, ocl_ref, gt_ref, wa_ref, wb_ref, wc_ref,
                  wo_ref, ws_ref, bs_ref, g1_ref, sc2_ref, sh2_ref, n2_ref, wr_ref, br_ref, x1_ref, h2p_ref, afft_ref):
    i = pl.program_id(0)
    tm = xc_ref.shape[0]
    group = lax.broadcasted_iota(I32, (CHUNK, B_WIDTH), 1) // B_GROUP_CH
    obs = []
    for c in range(tm // CHUNK):
        v = bv_ref[c * CHUNK:(c + 1) * CHUNK, :]
        sv = jnp.zeros((CHUNK, B_WIDTH), F32)
        for g in range(B_GROUPS):
            sv = jnp.where(group == g, jnp.dot(ws_ref[g], v, preferred_element_type=F32), sv)
        u = bu_ref[c * CHUNK:(c + 1) * CHUNK, :].astype(F32)
        obs.append((u * (sv + bs_ref[...])).astype(BF16))
    ob = jnp.concatenate(obs, axis=0)

    oa = _pick_pass(i, oac_ref, oal_ref)
    oc = _pick_pass(i, occ_ref, ocl_ref)
    merged = gt_ref[:, 0:D_MODEL].astype(F32) * jnp.dot(oa, wa_ref[...], preferred_element_type=F32)
    merged += gt_ref[:, D_MODEL:2 * D_MODEL].astype(F32) * jnp.dot(ob, wb_ref[...], preferred_element_type=F32)
    merged += gt_ref[:, 2 * D_MODEL:3 * D_MODEL].astype(F32) * jnp.dot(oc, wc_ref[...], preferred_element_type=F32)
    y = jnp.dot(merged.astype(BF16), wo_ref[...], preferred_element_type=F32)
    x1 = _pick_pass(i, xc_ref, xl_ref) + g1_ref[...] * y
    x1_ref[...] = x1

    ms = jnp.mean(x1 * x1, axis=-1, keepdims=True)
    h2 = x1 * lax.rsqrt(ms + EPS) * n2_ref[...]
    h2 = h2 * (1.0 + sc2_ref[...]) + sh2_ref[...]
    h2p_ref[...] = _pack_halves(h2)

    logits = jnp.dot(h2.astype(BF16), wr_ref[...], preferred_element_type=F32) + br_ref[...]
    e = jnp.exp(logits - jnp.max(logits, axis=-1, keepdims=True))
    aff = e / jnp.sum(e, axis=-1, keepdims=True)
    afft_ref[...] = aff.T[:N_EXPERTS, :]


def _merge(x_ctx, x_lat, oa_ctx, oa_lat, bu, bv, oc_ctx, oc_lat, gt, wa, wb, wc, wo, ws, bs, g1, sc2, sh2, n2, wr, br):
    tm = ROW_TILE
    row = lambda w: pl.BlockSpec((tm, w), lambda i: (i, 0))
    full = lambda a: pl.BlockSpec(a.shape, lambda i: (0,) * a.ndim)
    modspec = pl.BlockSpec((None, 1, D_MODEL), lambda i: (_req_of_tile(i), 0, 0))
    return pl.pallas_call(
        _merge_kernel,
        grid=(T_ALL // tm,),
        in_specs=[_ctx_rows(D_MODEL), _lat_rows(D_MODEL), _ctx_rows(Q_W), _lat_rows(Q_W), row(B_WIDTH), row(B_WIDTH),
                  _ctx_rows(Q_W), _lat_rows(Q_W), row(N_BRANCH * D_MODEL),
                  full(wa), full(wb), full(wc), full(wo), full(ws), full(bs),
                  modspec, modspec, modspec, full(n2), full(wr), full(br)],
        out_specs=[row(D_MODEL), row(D_MODEL // 2), pl.BlockSpec((N_EXPERTS, tm), lambda i: (0, i))],
        out_shape=[jax.ShapeDtypeStruct((T_ALL, D_MODEL), F32), jax.ShapeDtypeStruct((T_ALL, D_MODEL // 2), jnp.uint32),
                   jax.ShapeDtypeStruct((N_EXPERTS, T_ALL), F32)],
        compiler_params=_params(("arbitrary",)),
        name="merge_router",
    )(x_ctx, x_lat, oa_ctx, oa_lat, bu, bv, oc_ctx, oc_lat, gt, wa, wb, wc, wo, ws, bs, g1, sc2, sh2, n2, wr, br)


def _select_kernel(aff_ref, idx_ref, val_ref, possel_ref, *, n, cap, row_chunk):
    a = aff_ref[...]
    rows = a.shape[0]
    tok = lax.broadcasted_iota(I32, (rows, n), 1)

    def count(ones):
        return jnp.sum(ones, axis=1, keepdims=True)

    def at_least(word):
        return jnp.where(a >= pltpu.bitcast(word, F32), 1, 0)

    thr = jnp.zeros((rows, 1), I32)
    for bit in range(30, -1, -1):
        cand = thr | (1 << bit)
        thr = jnp.where(count(at_least(cand)) >= cap, cand, thr)
    above = at_least(thr + 1)
    tied = at_least(thr) - above
    need = cap - count(above)
    last = jnp.zeros((rows, 1), I32)
    for bit in range(n.bit_length() - 2, -1, -1):
        cand = last | (1 << bit)
        last = jnp.where(count(jnp.where(tok < cand, tied, 0)) < need, cand, last)
    sel = above + jnp.where(tok <= last, tied, 0)

    blk = min(n, 256)
    tri = jnp.where(lax.broadcasted_iota(I32, (blk, blk), 0) <= lax.broadcasted_iota(I32, (blk, blk), 1),
                    1.0, 0.0).astype(BF16)
    sel_f = sel.astype(F32)
    offset = jnp.zeros((rows, 1), F32)
    for j in range(n // blk):
        s_blk = sel_f[:, j * blk:(j + 1) * blk]
        incl = jnp.dot(s_blk.astype(BF16), tri, preferred_element_type=F32)
        pos = (incl - s_blk + offset).astype(I32)
        possel_ref[:, j * blk:(j + 1) * blk] = jnp.where(sel[:, j * blk:(j + 1) * blk] > 0, pos, -1)
        offset = offset + incl[:, blk - 1:blk]

    tok_row = lax.broadcasted_iota(I32, (1, n), 1)

    def per_row(e, _):
        pe = possel_ref[pl.ds(e, 1), :]
        ae = aff_ref[pl.ds(e, 1), :]

        def per_chunk(c, _):
            r0 = pl.multiple_of(c * row_chunk, row_chunk)
            slot = lax.broadcasted_iota(I32, (row_chunk, 1), 0) + r0
            hit = pe == slot
            idx_ref[e, pl.ds(r0, row_chunk), :] = jnp.sum(jnp.where(hit, tok_row, 0), axis=1, keepdims=True)
            val_ref[e, pl.ds(r0, row_chunk), :] = jnp.sum(jnp.where(hit, ae, 0.0), axis=1, keepdims=True)
            return 0

        return lax.fori_loop(0, cap // row_chunk, per_chunk, 0)

    lax.fori_loop(0, rows, per_row, 0)


def _select(aff_rows, rows_per_step, cap):
    r, n = aff_rows.shape
    row_chunk = min(cap, 64)
    return pl.pallas_call(
        functools.partial(_select_kernel, n=n, cap=cap, row_chunk=row_chunk),
        grid=(r // rows_per_step,),
        in_specs=[pl.BlockSpec((rows_per_step, n), lambda s: (s, 0))],
        out_specs=[pl.BlockSpec((rows_per_step, cap, 1), lambda s: (s, 0, 0))] * 2,
        out_shape=[jax.ShapeDtypeStruct((r, cap, 1), I32), jax.ShapeDtypeStruct((r, cap, 1), F32)],
        scratch_shapes=[pltpu.VMEM((rows_per_step, n), I32)],
        compiler_params=_params(("arbitrary",)),
        name="expert_select",
    )(aff_rows)


CTX_SLOTS = N_EXPERTS * CAP_CTX
SLOT_GROUP = 8


def _ctx_slot_onehot(idx_ref, slots_on_rows):
    idx = idx_ref[...].reshape(CTX_SLOTS, 1)
    if slots_on_rows:
        hit = idx == lax.broadcasted_iota(I32, (CTX_SLOTS, SEQ), 1)
    else:
        idx_lane = jnp.broadcast_to(idx.astype(F32), (CTX_SLOTS, LANE)).T[0:1, :]
        hit = idx_lane == lax.broadcasted_iota(I32, (SEQ, CTX_SLOTS), 0).astype(F32)
    return jnp.where(hit, 1.0, 0.0).astype(BF16)


def _gather_ctx_kernel(idx_ref, h_ref, out_ref):
    onehot = _ctx_slot_onehot(idx_ref, True)
    lo, hi = _unpack_halves(h_ref[...])
    g_lo = jnp.dot(onehot, lo, preferred_element_type=F32)
    g_hi = jnp.dot(onehot, hi, preferred_element_type=F32)
    packed = pltpu.pack_elementwise([g_lo, g_hi], packed_dtype=BF16)
    out_ref[...] = packed.reshape(N_EXPERTS, CAP_CTX, D_MODEL // 2)


def _gather_ctx(idx_c, h2p):
    return pl.pallas_call(
        _gather_ctx_kernel,
        grid=(BATCH,),
        in_specs=[pl.BlockSpec((N_EXPERTS, CAP_CTX, 1), lambda b: (b, 0, 0)),
                  pl.BlockSpec((SEQ, D_MODEL // 2), lambda b: (b, 0))],
        out_specs=pl.BlockSpec((N_EXPERTS, CAP_CTX, D_MODEL // 2), lambda b: (0, b, 0)),
        out_shape=jax.ShapeDtypeStruct((N_EXPERTS, BATCH * CAP_CTX, D_MODEL // 2), jnp.uint32),
        compiler_params=_params(("arbitrary",)),
        name="gather_ctx",
    )(idx_c, h2p)


def _gather_lat_kernel(idx_ref, src_ref, out_ref):
    base = (pl.program_id(0) * N_EXPERTS + pl.program_id(1)) * CAP_LAT

    def body(it, _):
        r0 = pl.multiple_of(it * SLOT_GROUP, SLOT_GROUP)
        picked = [src_ref[0, pl.ds(idx_ref[base + r0 + k], 1), :] for k in range(SLOT_GROUP)]
        for k in range(SLOT_GROUP):
            out_ref[0, pl.ds(r0 + k, 1), :] = picked[k]
        return 0

    lax.fori_loop(0, CAP_LAT // SLOT_GROUP, body, 0)


def _gather_lat(idx_flat, h2p3, off):
    return pl.pallas_call(
        _gather_lat_kernel,
        grid_spec=pltpu.PrefetchScalarGridSpec(
            num_scalar_prefetch=1,
            grid=(DEC_BATCH, N_EXPERTS),
            in_specs=[pl.BlockSpec((1, DEC_SEQ, D_MODEL // 2), lambda b, e, idx: (off + b, 0, 0))],
            out_specs=pl.BlockSpec((1, CAP_LAT, D_MODEL // 2), lambda b, e, idx: (e, b, 0)),
        ),
        out_shape=jax.ShapeDtypeStruct((N_EXPERTS, DEC_BATCH * CAP_LAT, D_MODEL // 2), jnp.uint32),
        compiler_params=_params(("arbitrary", "arbitrary")),
        name="gather_lat",
    )(idx_flat, h2p3)


N_CTX_FFN_TILES = BATCH * CAP_CTX // FFN_ROW_TILE


def _ffn_kernel(xc_ref, xl_ref, vc_ref, vl_ref, wg_ref, wu_ref, wd_ref, o_ref, wg_b, wu_b, wd_b):
    j = pl.program_id(1)

    @pl.when(j == 0)
    def _():
        wg_b[...] = wg_ref[0].astype(BF16)
        wu_b[...] = wu_ref[0].astype(BF16)
        wd_b[...] = wd_ref[0].astype(BF16)

    is_ctx = j < N_CTX_FFN_TILES
    x = jnp.where(is_ctx, jnp.concatenate(_unpack_halves(xc_ref[0]), axis=1),
                  jnp.concatenate(_unpack_halves(xl_ref[0]), axis=1))
    g = jnp.dot(x, wg_b[...], preferred_element_type=F32)
    u = jnp.dot(x, wu_b[...], preferred_element_type=F32)
    hh = (g * _sigmoid(g)) * u
    y = jnp.dot(hh.astype(BF16), wd_b[...], preferred_element_type=F32)
    o_ref[0] = y * jnp.where(is_ctx, vc_ref[0], vl_ref[0])


def _expert_ffn(xg_ctx, xg_lat, val_ctx, val_lat, w_gate, w_up, w_down, layer):
    tr = FFN_ROW_TILE
    n_tiles = ROWS_PER_EXPERT // tr
    def wspec(k, n, tiles_held):
        ahead = lambda e, j: jnp.minimum(e + jnp.where(j >= tiles_held, 1, 0), N_EXPERTS - 1)
        return pl.BlockSpec((None, 1, k, n), lambda e, j: (layer, ahead(e, j), 0, 0))

    ctx_tile = lambda j: jnp.minimum(j, N_CTX_FFN_TILES - 1)
    lat_tile = lambda j: jnp.maximum(j - N_CTX_FFN_TILES, 0)
    return pl.pallas_call(
        _ffn_kernel,
        grid=(N_EXPERTS, n_tiles),
        in_specs=[pl.BlockSpec((1, tr, D_MODEL // 2), lambda e, j: (e, ctx_tile(j), 0)),
                  pl.BlockSpec((1, tr, D_MODEL // 2), lambda e, j: (e, lat_tile(j), 0)),
                  pl.BlockSpec((1, tr, 1), lambda e, j: (e, ctx_tile(j), 0)),
                  pl.BlockSpec((1, tr, 1), lambda e, j: (e, lat_tile(j), 0)),
                  wspec(D_MODEL, EXPERT_FF, 1), wspec(D_MODEL, EXPERT_FF, n_tiles - 1),
                  wspec(EXPERT_FF, D_MODEL, n_tiles)],
        out_specs=pl.BlockSpec((1, tr, D_MODEL), lambda e, j: (e, j, 0)),
        out_shape=jax.ShapeDtypeStruct((N_EXPERTS, ROWS_PER_EXPERT, D_MODEL), F32),
        scratch_shapes=[pltpu.VMEM((D_MODEL, EXPERT_FF), BF16), pltpu.VMEM((D_MODEL, EXPERT_FF), BF16),
                        pltpu.VMEM((EXPERT_FF, D_MODEL), BF16)],
        compiler_params=_params(("arbitrary", "arbitrary")),
        name="expert_ffn",
    )(xg_ctx, xg_lat, val_ctx, val_lat, w_gate, w_up, w_down)


def _scatter_ctx_kernel(idx_ref, y_ref, x1_ref, g2_ref, out_ref):
    onehot = _ctx_slot_onehot(idx_ref, False)
    y_hi, y_lo = _split_bf16(y_ref[...].reshape(CTX_SLOTS, D_MODEL))
    moe = jnp.dot(onehot, y_hi, preferred_element_type=F32) + jnp.dot(onehot, y_lo, preferred_element_type=F32)
    out_ref[...] = x1_ref[...] + g2_ref[...] * moe


def _scatter_ctx(idx_c, yg, x1, g2):
    return pl.pallas_call(
        _scatter_ctx_kernel,
        grid=(BATCH,),
        in_specs=[pl.BlockSpec((N_EXPERTS, CAP_CTX, 1), lambda b: (b, 0, 0)),
                  pl.BlockSpec((N_EXPERTS, CAP_CTX, D_MODEL), lambda b: (0, b, 0)),
                  pl.BlockSpec((SEQ, D_MODEL), lambda b: (b, 0)),
                  pl.BlockSpec((None, 1, D_MODEL), lambda b: (0, 0, 0))],
        out_specs=pl.BlockSpec((SEQ, D_MODEL), lambda b: (b, 0)),
        out_shape=jax.ShapeDtypeStruct((T_CTX, D_MODEL), F32),
        compiler_params=_params(("arbitrary",)),
        name="scatter_ctx",
    )(idx_c, yg, x1, g2)


def _scatter_lat_kernel(idx_ref, y_ref, x1_ref, g2_ref, out_ref):
    e = pl.program_id(2)

    @pl.when(e == 0)
    def _():
        out_ref[...] = jnp.zeros_like(out_ref)

    base = (pl.program_id(0) * N_EXPERTS + e) * CAP_LAT

    def body(it, _):
        r0 = pl.multiple_of(it * SLOT_GROUP, SLOT_GROUP)
        rows = [idx_ref[base + r0 + k] for k in range(SLOT_GROUP)]
        old = [out_ref[0, pl.ds(rows[k], 1), :] for k in range(SLOT_GROUP)]
        y = y_ref[0, pl.ds(r0, SLOT_GROUP), :]
        for k in range(SLOT_GROUP):
            out_ref[0, pl.ds(rows[k], 1), :] = old[k] + y[k:k + 1, :]
        return 0

    lax.fori_loop(0, CAP_LAT // SLOT_GROUP, body, 0)

    @pl.when(e == N_EXPERTS - 1)
    def _():
        out_ref[0] = x1_ref[0] + g2_ref[...] * out_ref[0]


def _scatter_lat(idx_flat, yg, x1_3, g2, off, n_split):
    w = D_MODEL // n_split
    blk0 = BATCH * CAP_CTX // CAP_LAT
    return pl.pallas_call(
        _scatter_lat_kernel,
        grid_spec=pltpu.PrefetchScalarGridSpec(
            num_scalar_prefetch=1,
            grid=(DEC_BATCH, n_split, N_EXPERTS),
            in_specs=[pl.BlockSpec((1, CAP_LAT, w), lambda b, h, e, idx: (e, blk0 + b, h)),
                      pl.BlockSpec((1, DEC_SEQ, w), lambda b, h, e, idx: (off + b, 0, h)),
                      pl.BlockSpec((None, 1, w), lambda b, h, e, idx: (off + b, 0, h))],
            out_specs=pl.BlockSpec((1, DEC_SEQ, w), lambda b, h, e, idx: (b, 0, h)),
        ),
        out_shape=jax.ShapeDtypeStruct((DEC_BATCH, DEC_SEQ, D_MODEL), F32),
        compiler_params=_params(("arbitrary", "arbitrary", "arbitrary")),
        name="scatter_lat",
    )(idx_flat, yg, x1_3, g2)


def _rope_tables():
    pos = jnp.arange(DEC_SEQ)
    r = (pos // GRID_W).astype(F32)
    col = (pos % GRID_W).astype(F32)
    freq = ROPE_THETA ** (-jnp.arange(ROPE_FREQS, dtype=F32) / ROPE_FREQS)
    ang_r, ang_c = r[:, None] * freq, col[:, None] * freq
    cos = jnp.concatenate([jnp.cos(ang_r)] * 2 + [jnp.cos(ang_c)] * 2, axis=-1)
    sin = jnp.concatenate([-jnp.sin(ang_r), jnp.sin(ang_r), -jnp.sin(ang_c), jnp.sin(ang_c)], axis=-1)
    cos = jnp.tile(cos, (DEC_BATCH, LANE // HEAD_DIM))
    sin = jnp.tile(sin, (DEC_BATCH, LANE // HEAD_DIM))
    cs = jnp.concatenate([jnp.ones((T_CTX, LANE), F32), cos], axis=0)
    sn = jnp.concatenate([jnp.zeros((T_CTX, LANE), F32), sin], axis=0)
    return cs, sn


def _block_diag_ones(width, group):
    g = jnp.arange(width) // group
    return (g[:, None] == g[None, :]).astype(BF16)


def _qk_gain(q_norm, k_norm):
    q = jnp.tile(q_norm, N_HEADS) * (HEAD_DIM ** -0.5 * LOG2_E)
    return jnp.concatenate([q, jnp.tile(k_norm, N_KV)])[None, :]


def kernel(x_prompt, x_sample, cache_a_k, cache_a_v, cache_c_k, cache_c_v, c, c_ctx, norm1_g, w_mod, b_mod, w_in,
           a_q_norm, a_k_norm, a_sink, b_v_norm, b_ws, b_bs, c_q_norm, c_k_norm, w_a_o, w_b_o, w_c_o, w_out, norm2_g,
           w_router, b_router, w_gate, w_up, w_down):
    cond8 = jnp.concatenate([c_ctx[None, :], c, jnp.zeros((8 - N_REQ, D_MODEL), F32)], axis=0)
    mods = _modulation(cond8, w_mod, b_mod)

    cs, sn = _rope_tables()
    bd_qk = _block_diag_ones(QK_W, HEAD_DIM)
    bd_b = _block_diag_ones(B_WIDTH, B_GROUP_CH)
    w_in_b = w_in.astype(BF16)
    wa_b, wb_b, wc_b, wo_b = w_a_o.astype(BF16), w_b_o.astype(BF16), w_c_o.astype(BF16), w_out.astype(BF16)
    ws_b = b_ws.astype(BF16)
    wr_pad = jnp.pad(w_router, ((0, 0), (0, 0), (0, LANE - N_EXPERTS))).astype(BF16)
    br_pad = jnp.pad(b_router, ((0, 0), (0, LANE - N_EXPERTS)), constant_values=NEG_BIG)

    by_seq = lambda a: a.reshape(T_ALL // SEQ, SEQ, a.shape[-1])
    by_dec = lambda a: a.reshape(T_ALL // DEC_SEQ, DEC_SEQ, a.shape[-1])
    lat_off = T_CTX // DEC_SEQ

    x_ctx = x_prompt.reshape(T_CTX, D_MODEL)
    x_lat = x_sample.reshape(T_LAT, D_MODEL)
    new_kv = [[], [], [], []]
    for l in range(DEPTH):
        sh1, sc1, g1, sh2, sc2, g2 = [mods[l, :N_REQ, i * D_MODEL:(i + 1) * D_MODEL].reshape(N_REQ, 1, D_MODEL)
                                      for i in range(6)]
        qa, ka, va, bu, bv, qc, kc, vc, gt = _input_projection(
            x_ctx, x_lat, sc1, sh1, norm1_g[l][None, :], w_in_b[l], cs, sn,
            _qk_gain(a_q_norm[l], a_k_norm[l]), _qk_gain(c_q_norm[l], c_k_norm[l]), b_v_norm[l][None, :], bd_qk, bd_b)
        for lst, arr in zip(new_kv, (ka, va, kc, vc)):
            lst.append(arr[:T_CTX].reshape(BATCH, SEQ, N_KV, HEAD_DIM))

        ka_b, va_b, kc_b, vc_b = (a.astype(BF16) for a in (ka, va, kc, vc))
        sink = a_sink[l]
        oa_ctx = _dense_attention(by_seq(qa), by_seq(ka_b)[:BATCH], by_seq(va_b)[:BATCH], sink,
                                  n_req=BATCH, q_off=0, tq=SEQ, key_chunk=SEQ)
        oc_ctx = _dense_attention(by_seq(qc), by_seq(kc_b)[:BATCH], by_seq(vc_b)[:BATCH], None,
                                  n_req=BATCH, q_off=0, tq=SEQ, key_chunk=SEQ)
        flat_cache = lambda a: a[:, l].reshape(DEC_BATCH, PAST_LEN, KV_W).astype(BF16)
        oa_lat = _window_attention(by_dec(qa), by_dec(ka_b), by_dec(va_b), flat_cache(cache_a_k),
                                   flat_cache(cache_a_v), sink, n_req=DEC_BATCH, off=lat_off)
        kc_all = jnp.concatenate([by_dec(kc_b)[lat_off:], flat_cache(cache_c_k)], axis=1)
        vc_all = jnp.concatenate([by_dec(vc_b)[lat_off:], flat_cache(cache_c_v)], axis=1)
        oc_lat = _dense_attention(by_dec(qc), kc_all, vc_all, None,
                                  n_req=DEC_BATCH, q_off=lat_off, tq=1024, key_chunk=512)

        bs_full = jnp.repeat(b_bs[l].T, B_GROUP_CH, axis=1)
        x1, h2p, afft = _merge(x_ctx, x_lat, oa_ctx.reshape(T_CTX, Q_W), oa_lat.reshape(T_LAT, Q_W), bu, bv,
                               oc_ctx.reshape(T_CTX, Q_W), oc_lat.reshape(T_LAT, Q_W), gt,
                               wa_b[l], wb_b[l], wc_b[l], wo_b[l], ws_b[l], bs_full,
                               g1, sc2, sh2, norm2_g[l][None, :], wr_pad[l], br_pad[l][None, :])

        aff_rows = lambda a, n_req, n: a.reshape(N_EXPERTS, n_req, n).transpose(1, 0, 2).reshape(n_req * N_EXPERTS, n)
        idx_c, val_c = _select(aff_rows(afft[:, :T_CTX], BATCH, SEQ), BATCH * N_EXPERTS, CAP_CTX)
        idx_l, val_l = _select(aff_rows(afft[:, T_CTX:], DEC_BATCH, DEC_SEQ), N_EXPERTS, CAP_LAT)
        idx_l_flat = idx_l.reshape(-1)
        xg_ctx = _gather_ctx(idx_c, h2p)
        xg_lat = _gather_lat(idx_l_flat, by_dec(h2p), lat_off)
        per_expert = lambda v, n_req, cap: v.reshape(n_req, N_EXPERTS, cap).transpose(1, 0, 2).reshape(
            N_EXPERTS, n_req * cap, 1)
        yg = _expert_ffn(xg_ctx, xg_lat, per_expert(val_c, BATCH, CAP_CTX), per_expert(val_l, DEC_BATCH, CAP_LAT),
                         w_gate, w_up, w_down, l)

        x_ctx = _scatter_ctx(idx_c, yg, x1, g2)
        x_lat = _scatter_lat(idx_l_flat, yg, by_dec(x1), g2, lat_off, 2).reshape(T_LAT, D_MODEL)

    y_prompt = x_ctx.reshape(BATCH, SEQ, D_MODEL)
    y_sample = x_lat.reshape(DEC_BATCH, DEC_SEQ, D_MODEL)
    return (y_prompt, y_sample) + tuple(jnp.stack(lst, axis=1) for lst in new_kv)
```

```python
import functools

import jax
import numpy as np
import jax.numpy as jnp
from jax import lax
from jax.experimental import pallas as pl
from jax.experimental.pallas import tpu as pltpu

F32 = jnp.float32
BF16 = jnp.bfloat16
I32 = jnp.int32

D_MODEL = 1024
BATCH = 16
SEQ = 256
DEPTH = 2
DEC_BATCH = 2
DEC_SEQ = 4096
PAST_LEN = 256
GRID_W = 64
HEAD_DIM = 64
N_HEADS = 6
N_KV = 2
N_GRP = N_HEADS // N_KV
B_GROUPS = 4
B_GROUP_CH = 64
B_WIDTH = B_GROUPS * B_GROUP_CH
Q_W = N_HEADS * HEAD_DIM
KV_W = N_KV * HEAD_DIM
QK_W = Q_W + KV_W
N_BRANCH = 3
WINDOW = 128
BLOCK = 128
CHUNK = 128
N_EXPERTS = 16
EXPERT_FF = 1024
CAP_FACTOR = 2
ROPE_THETA = 10000.0
ROPE_FREQS = HEAD_DIM // 4
EPS = 1e-6
IN_WIDTH = 2 * (QK_W + KV_W) + 2 * B_WIDTH + N_BRANCH * D_MODEL

T_CTX = BATCH * SEQ
T_LAT = DEC_BATCH * DEC_SEQ
T_ALL = T_CTX + T_LAT
N_REQ = 1 + DEC_BATCH
CAP_CTX = CAP_FACTOR * SEQ // N_EXPERTS
CAP_LAT = CAP_FACTOR * DEC_SEQ // N_EXPERTS
ROWS_PER_EXPERT = BATCH * CAP_CTX + DEC_BATCH * CAP_LAT

LANE = 128
ROW_TILE = 512
N_CTX_TILES = T_CTX // ROW_TILE
FFN_ROW_TILE = 512
VMEM_LIMIT = 56 * 1024 * 1024
NEG_BIG = -1e30
LOG2_E = 1.4426950408889634

OFF_A = 0
OFF_AV = OFF_A + QK_W
OFF_BU = OFF_AV + KV_W
OFF_BV = OFF_BU + B_WIDTH
OFF_C = OFF_BV + B_WIDTH
OFF_CV = OFF_C + QK_W
OFF_G = OFF_CV + KV_W


def _params(sem, vmem=VMEM_LIMIT):
    return pltpu.CompilerParams(dimension_semantics=sem, vmem_limit_bytes=vmem)


def _sigmoid(x):
    return 1.0 / (1.0 + jnp.exp(-x))


def _gelu_tanh(x):
    return 0.5 * x * (1.0 + jnp.tanh(0.7978845608028654 * (x + 0.044715 * (x * x * x))))


def _split_bf16(x):
    hi = x.astype(BF16)
    lo = (x - hi.astype(F32)).astype(BF16)
    return hi, lo


def _mod_kernel(c_ref, w_ref, b_ref, o_ref):
    c = c_ref[...]
    s_hi, s_lo = _split_bf16(c * _sigmoid(c))
    w_hi, w_lo = _split_bf16(w_ref[0])
    acc = jnp.dot(s_hi, w_hi, preferred_element_type=F32)
    acc += jnp.dot(s_lo, w_hi, preferred_element_type=F32)
    acc += jnp.dot(s_hi, w_lo, preferred_element_type=F32)
    o_ref[0] = acc + b_ref[0]


def _modulation(cond8, w_mod, b_mod):
    n_col = 6 * D_MODEL // D_MODEL
    return pl.pallas_call(
        _mod_kernel,
        grid=(DEPTH, n_col),
        in_specs=[
            pl.BlockSpec((8, D_MODEL), lambda l, j: (0, 0)),
            pl.BlockSpec((1, D_MODEL, D_MODEL), lambda l, j: (l, 0, j)),
            pl.BlockSpec((1, 1, D_MODEL), lambda l, j: (l, 0, j)),
        ],
        out_specs=pl.BlockSpec((1, 8, D_MODEL), lambda l, j: (l, 0, j)),
        out_shape=jax.ShapeDtypeStruct((DEPTH, 8, 6 * D_MODEL), F32),
        compiler_params=_params(("arbitrary", "arbitrary")),
        name="modulation",
    )(cond8, w_mod, b_mod.reshape(DEPTH, 1, 6 * D_MODEL))


def _group_sumsq(y, bd_ref):
    return jnp.dot((y * y).astype(BF16), bd_ref[...], preferred_element_type=F32)


def _pick_pass(i, ctx_ref, lat_ref):
    return jnp.where(i < N_CTX_TILES, ctx_ref[...], lat_ref[...])


def _in_kernel(xc_ref, xl_ref, sc_ref, sh_ref, n1_ref, w_ref, cs_ref, sn_ref, ga_ref, gc_ref, gbv_ref, bd_qk_ref,
               bd_b_ref, qa_ref, ka_ref, va_ref, bu_ref, bv_ref, qc_ref, kc_ref, vc_ref, gt_ref):
    x = _pick_pass(pl.program_id(0), xc_ref, xl_ref)
    ms = jnp.mean(x * x, axis=-1, keepdims=True)
    h = x * lax.rsqrt(ms + EPS) * n1_ref[...]
    h = h * (1.0 + sc_ref[...]) + sh_ref[...]
    hb = h.astype(BF16)
    tm = x.shape[0]

    def proj(c0, width):
        return jnp.dot(hb, w_ref[:, c0:c0 + width], preferred_element_type=F32)

    cs = jnp.concatenate([cs_ref[...]] * (QK_W // LANE), axis=1)
    sn = jnp.concatenate([sn_ref[...]] * (QK_W // LANE), axis=1)
    lane = lax.broadcasted_iota(I32, (tm, QK_W), 1)
    first_half = (lane & ROPE_FREQS) == 0

    def qk_post(y, gain_ref):
        yn = y * lax.rsqrt(_group_sumsq(y, bd_qk_ref) * (1.0 / HEAD_DIM) + EPS) * gain_ref[...]
        partner = jnp.where(first_half, pltpu.roll(yn, QK_W - ROPE_FREQS, 1), pltpu.roll(yn, ROPE_FREQS, 1))
        return yn * cs + partner * sn

    ya = qk_post(proj(OFF_A, QK_W), ga_ref)
    qa_ref[...] = ya[:, :Q_W].astype(BF16)
    ka_ref[...] = ya[:, Q_W:]
    va_ref[...] = proj(OFF_AV, KV_W)

    bu_ref[...] = _gelu_tanh(proj(OFF_BU, B_WIDTH)).astype(BF16)
    gv = _gelu_tanh(proj(OFF_BV, B_WIDTH))
    gvn = gv * lax.rsqrt(_group_sumsq(gv, bd_b_ref) * (1.0 / B_GROUP_CH) + EPS) * gbv_ref[...]
    bv_ref[...] = gvn.astype(BF16)

    yc = qk_post(proj(OFF_C, QK_W), gc_ref)
    qc_ref[...] = yc[:, :Q_W].astype(BF16)
    kc_ref[...] = yc[:, Q_W:]
    vc_ref[...] = proj(OFF_CV, KV_W)

    gate_chunk = 512
    for j in range(N_BRANCH * D_MODEL // gate_chunk):
        g = proj(OFF_G + j * gate_chunk, gate_chunk)
        gt_ref[:, j * gate_chunk:(j + 1) * gate_chunk] = _sigmoid(g).astype(BF16)


def _req_of_tile(i):
    return i // N_CTX_TILES


def _ctx_rows(w):
    return pl.BlockSpec((ROW_TILE, w), lambda i: (jnp.minimum(i, N_CTX_TILES - 1), 0))


def _lat_rows(w):
    return pl.BlockSpec((ROW_TILE, w), lambda i: (jnp.maximum(i - N_CTX_TILES, 0), 0))


def _input_projection(x_ctx, x_lat, sc1, sh1, n1, w_in_b, cs, sn, gain_a, gain_c, gain_bv, bd_qk, bd_b):
    tm = ROW_TILE
    row = lambda w: pl.BlockSpec((tm, w), lambda i: (i, 0))
    full = lambda a: pl.BlockSpec(a.shape, lambda i: (0,) * a.ndim)
    modspec = pl.BlockSpec((None, 1, D_MODEL), lambda i: (_req_of_tile(i), 0, 0))
    rope = pl.BlockSpec((tm, LANE), lambda i: (_rope_tile(i), 0))
    outs = [(Q_W, BF16), (KV_W, F32), (KV_W, F32), (B_WIDTH, BF16), (B_WIDTH, BF16),
            (Q_W, BF16), (KV_W, F32), (KV_W, F32), (N_BRANCH * D_MODEL, BF16)]
    return pl.pallas_call(
        _in_kernel,
        grid=(T_ALL // tm,),
        in_specs=[_ctx_rows(D_MODEL), _lat_rows(D_MODEL), modspec, modspec, full(n1), full(w_in_b), rope, rope,
                  full(gain_a), full(gain_c), full(gain_bv), full(bd_qk), full(bd_b)],
        out_specs=[row(w) for w, _ in outs],
        out_shape=[jax.ShapeDtypeStruct((T_ALL, w), dt) for w, dt in outs],
        compiler_params=_params(("arbitrary",)),
        name="input_projection",
    )(x_ctx, x_lat, sc1, sh1, n1, w_in_b, cs, sn, gain_a, gain_c, gain_bv, bd_qk, bd_b)


def _attention_tile(q_ref, sources, sink_ref, o_ref, qt_scr, ot_scr, *, tq, key_chunk):
    width = N_GRP * tq
    for j in range(Q_W // LANE):
        qt_scr[j * LANE:(j + 1) * LANE, :] = q_ref[0, :, j * LANE:(j + 1) * LANE].astype(F32).T.astype(BF16)
    for kv in range(N_KV):
        lo, hi = kv * HEAD_DIM, (kv + 1) * HEAD_DIM
        heads = [kv * N_GRP + g for g in range(N_GRP)]
        qt = jnp.concatenate([qt_scr[h * HEAD_DIM:(h + 1) * HEAD_DIM, :] for h in heads], axis=1)

        def step(carry, kref, vref, c0, size, bias):
            m, l, acc = carry
            s = jnp.dot(kref[0, pl.ds(c0, size), lo:hi], qt, preferred_element_type=F32)
            if bias is not None:
                s = s + jnp.concatenate([bias] * N_GRP, axis=1)
            vt = vref[0, pl.ds(c0, size), :].astype(F32).T[lo:hi, :].astype(BF16)
            m_new = jnp.maximum(m, jnp.max(s, axis=0, keepdims=True))
            alpha = jnp.exp2(m - m_new)
            p = jnp.exp2(s - m_new)
            l = l * alpha + jnp.sum(p, axis=0, keepdims=True)
            acc = acc * alpha + jnp.dot(vt, p.astype(BF16), preferred_element_type=F32)
            return m_new, l, acc

        if sink_ref is not None:
            m0 = jnp.concatenate([jnp.full((1, tq), sink_ref[h] * LOG2_E, F32) for h in heads], axis=1)
            l0 = jnp.ones((1, width), F32)
        else:
            m0 = jnp.full((1, width), NEG_BIG, F32)
            l0 = jnp.zeros((1, width), F32)
        carry = (m0, l0, jnp.zeros((HEAD_DIM, width), F32))
        for kref, vref, bias in sources:
            n_rows = kref.shape[1]
            n_full = n_rows // key_chunk
            if bias is not None:
                carry = step(carry, kref, vref, 0, n_rows, bias)
                continue
            if n_full > 1:
                carry = lax.fori_loop(
                    0, n_full,
                    lambda c, cr: step(cr, kref, vref, pl.multiple_of(c * key_chunk, key_chunk), key_chunk, None), carry)
            elif n_full == 1:
                carry = step(carry, kref, vref, 0, key_chunk, None)
            if n_rows - n_full * key_chunk:
                carry = step(carry, kref, vref, n_full * key_chunk, n_rows - n_full * key_chunk, None)
        _, l, acc = carry
        o = acc / l
        for g, h in enumerate(heads):
            ot_scr[h * HEAD_DIM:(h + 1) * HEAD_DIM, :] = o[:, g * tq:(g + 1) * tq]
    for j in range(Q_W // LANE):
        o_ref[0, :, j * LANE:(j + 1) * LANE] = ot_scr[j * LANE:(j + 1) * LANE, :].T.astype(o_ref.dtype)


def _dense_attn_kernel(*refs, tq, key_chunk, has_extra, has_sink):
    refs = list(refs)
    q_ref, k_ref, v_ref = refs[:3]
    del refs[:3]
    sources = [(k_ref, v_ref, None)]
    if has_extra:
        sources.append((refs.pop(0), refs.pop(0), None))
    sink_ref = refs.pop(0) if has_sink else None
    o_ref, qt_scr, ot_scr = refs
    _attention_tile(q_ref, sources, sink_ref, o_ref, qt_scr, ot_scr, tq=tq, key_chunk=key_chunk)


def _attention_scratch(tq):
    return [pltpu.VMEM((Q_W, tq), BF16), pltpu.VMEM((Q_W, tq), F32)]


def _dense_attention(q, k, v, extra, sink, *, n_req, off, tq, key_chunk):
    s = q.shape[1]
    kv_spec = pl.BlockSpec((1, s, KV_W), lambda i, j: (off + i, 0, 0))
    in_specs = [pl.BlockSpec((1, tq, Q_W), lambda i, j: (off + i, j, 0)), kv_spec, kv_spec]
    args = [q, k, v]
    if extra is not None:
        in_specs += [pl.BlockSpec((1, extra[0].shape[1], KV_W), lambda i, j: (i, 0, 0))] * 2
        args += list(extra)
    if sink is not None:
        in_specs.append(pl.BlockSpec(memory_space=pltpu.SMEM))
        args.append(sink)
    return pl.pallas_call(
        functools.partial(_dense_attn_kernel, tq=tq, key_chunk=key_chunk, has_extra=extra is not None,
                          has_sink=sink is not None),
        grid=(n_req, s // tq),
        in_specs=in_specs,
        out_specs=pl.BlockSpec((1, tq, Q_W), lambda i, j: (i, j, 0)),
        out_shape=jax.ShapeDtypeStruct((n_req, s, Q_W), BF16),
        scratch_shapes=_attention_scratch(tq),
        compiler_params=_params(("arbitrary", "arbitrary")),
        name="dense_attention",
    )(*args)


WINDOW_TQ = 512


def _window_attn_kernel(q_ref, kp_ref, kc_ref, kn_ref, vp_ref, vc_ref, vn_ref, ck_ref, cv_ref, bp_ref, bc_ref, bn_ref,
                        sink_ref, o_ref, qt_scr, ot_scr, *, seq):
    q_pos0 = pl.program_id(1) * WINDOW_TQ
    prev_bias = bp_ref[...] + jnp.where(q_pos0 >= BLOCK, 0.0, NEG_BIG)
    next_bias = bn_ref[...] + jnp.where(q_pos0 + WINDOW_TQ < seq, 0.0, NEG_BIG)
    sources = [(kp_ref, vp_ref, prev_bias), (kc_ref, vc_ref, bc_ref[...]), (kn_ref, vn_ref, next_bias),
               (ck_ref, cv_ref, None)]
    _attention_tile(q_ref, sources, sink_ref, o_ref, qt_scr, ot_scr, tq=WINDOW_TQ, key_chunk=WINDOW_TQ)


def _band_bias(first_key, n_keys):
    d = (first_key + np.arange(n_keys))[:, None] - np.arange(WINDOW_TQ)[None, :]
    return np.where(np.abs(d) <= WINDOW, 0.0, NEG_BIG).astype(np.float32)


def _window_attention(q, k, v, ck, cv, sink, *, n_req, off):
    b, s = n_req, q.shape[1]
    nb = s // BLOCK
    per_tile = WINDOW_TQ // BLOCK
    edge = lambda f: pl.BlockSpec((1, BLOCK, KV_W), lambda i, j: (off + i, f(j), 0))
    prev = lambda j: jnp.maximum(j * per_tile - 1, 0)
    nxt = lambda j: jnp.minimum((j + 1) * per_tile, nb - 1)
    cur = pl.BlockSpec((1, WINDOW_TQ, KV_W), lambda i, j: (off + i, j, 0))
    ctx = pl.BlockSpec((1, PAST_LEN, KV_W), lambda i, j: (i, 0, 0))
    biases = [_band_bias(-BLOCK, BLOCK), _band_bias(0, WINDOW_TQ), _band_bias(WINDOW_TQ, BLOCK)]
    table = lambda a: pl.BlockSpec(a.shape, lambda i, j: (0, 0))
    return pl.pallas_call(
        functools.partial(_window_attn_kernel, seq=s),
        grid=(b, s // WINDOW_TQ),
        in_specs=[pl.BlockSpec((1, WINDOW_TQ, Q_W), lambda i, j: (off + i, j, 0)),
                  edge(prev), cur, edge(nxt), edge(prev), cur, edge(nxt), ctx, ctx,
                  table(biases[0]), table(biases[1]), table(biases[2]),
                  pl.BlockSpec(memory_space=pltpu.SMEM)],
        out_specs=pl.BlockSpec((1, WINDOW_TQ, Q_W), lambda i, j: (i, j, 0)),
        out_shape=jax.ShapeDtypeStruct((b, s, Q_W), BF16),
        scratch_shapes=_attention_scratch(WINDOW_TQ),
        compiler_params=_params(("arbitrary", "arbitrary")),
        name="window_attention",
    )(q, k, k, k, v, v, v, ck, cv, *biases, sink)


def _pack_halves(x):
    half = x.shape[1] // 2
    return pltpu.pack_elementwise([x[:, :half], x[:, half:]], packed_dtype=BF16)


def _unpack_halves(words):
    return tuple(pltpu.unpack_elementwise(words, index=i, packed_dtype=BF16, unpacked_dtype=F32).astype(BF16)
                 for i in range(2))


def _merge_kernel(xc_ref, xl_ref, oac_ref, oal_ref, bu_ref, bv_ref, occ_ref, ocl_ref, gt_ref, wa_ref, wb_ref, wc_ref,
                  wo_ref, ws_ref, bs_ref, g1_ref, sc2_ref, sh2_ref, n2_ref, wr_ref, br_ref, x1_ref, h2p_ref, afft_ref):
    i = pl.program_id(0)
    tm = xc_ref.shape[0]
    group = lax.broadcasted_iota(I32, (CHUNK, B_WIDTH), 1) // B_GROUP_CH
    obs = []
    for c in range(tm // CHUNK):
        v = bv_ref[c * CHUNK:(c + 1) * CHUNK, :]
        sv = jnp.zeros((CHUNK, B_WIDTH), F32)
        for g in range(B_GROUPS):
            sv = jnp.where(group == g, jnp.dot(ws_ref[g], v, preferred_element_type=F32), sv)
        u = bu_ref[c * CHUNK:(c + 1) * CHUNK, :].astype(F32)
        obs.append((u * (sv + bs_ref[...])).astype(BF16))
    ob = jnp.concatenate(obs, axis=0)

    oa = _pick_pass(i, oac_ref, oal_ref)
    oc = _pick_pass(i, occ_ref, ocl_ref)
    merged = gt_ref[:, 0:D_MODEL].astype(F32) * jnp.dot(oa, wa_ref[...], preferred_element_type=F32)
    merged += gt_ref[:, D_MODEL:2 * D_MODEL].astype(F32) * jnp.dot(ob, wb_ref[...], preferred_element_type=F32)
    merged += gt_ref[:, 2 * D_MODEL:3 * D_MODEL].astype(F32) * jnp.dot(oc, wc_ref[...], preferred_element_type=F32)
    y = jnp.dot(merged.astype(BF16), wo_ref[...], preferred_element_type=F32)
    x1 = _pick_pass(i, xc_ref, xl_ref) + g1_ref[...] * y
    x1_ref[...] = x1

    ms = jnp.mean(x1 * x1, axis=-1, keepdims=True)
    h2 = x1 * lax.rsqrt(ms + EPS) * n2_ref[...]
    h2 = h2 * (1.0 + sc2_ref[...]) + sh2_ref[...]
    h2p_ref[...] = _pack_halves(h2)

    logits = jnp.dot(h2.astype(BF16), wr_ref[...], preferred_element_type=F32) + br_ref[...]
    e = jnp.exp(logits - jnp.max(logits, axis=-1, keepdims=True))
    aff = e / jnp.sum(e, axis=-1, keepdims=True)
    afft_ref[...] = aff.T[:N_EXPERTS, :]


def _merge(x_ctx, x_lat, oa_ctx, oa_lat, bu, bv, oc_ctx, oc_lat, gt, wa, wb, wc, wo, ws, bs, g1, sc2, sh2, n2, wr, br):
    tm = ROW_TILE
    row = lambda w: pl.BlockSpec((tm, w), lambda i: (i, 0))
    full = lambda a: pl.BlockSpec(a.shape, lambda i: (0,) * a.ndim)
    modspec = pl.BlockSpec((None, 1, D_MODEL), lambda i: (_req_of_tile(i), 0, 0))
    return pl.pallas_call(
        _merge_kernel,
        grid=(T_ALL // tm,),
        in_specs=[_ctx_rows(D_MODEL), _lat_rows(D_MODEL), _ctx_rows(Q_W), _lat_rows(Q_W), row(B_WIDTH), row(B_WIDTH),
                  _ctx_rows(Q_W), _lat_rows(Q_W), row(N_BRANCH * D_MODEL),
                  full(wa), full(wb), full(wc), full(wo), full(ws), full(bs),
                  modspec, modspec, modspec, full(n2), full(wr), full(br)],
        out_specs=[row(D_MODEL), row(D_MODEL // 2), pl.BlockSpec((N_EXPERTS, tm), lambda i: (0, i))],
        out_shape=[jax.ShapeDtypeStruct((T_ALL, D_MODEL), F32), jax.ShapeDtypeStruct((T_ALL, D_MODEL // 2), jnp.uint32),
                   jax.ShapeDtypeStruct((N_EXPERTS, T_ALL), F32)],
        compiler_params=_params(("arbitrary",)),
        name="merge_router",
    )(x_ctx, x_lat, oa_ctx, oa_lat, bu, bv, oc_ctx, oc_lat, gt, wa, wb, wc, wo, ws, bs, g1, sc2, sh2, n2, wr, br)


def _select_kernel(aff_ref, idx_ref, val_ref, possel_ref, *, n, cap, row_chunk):
    a = aff_ref[...]
    rows = a.shape[0]
    tok = lax.broadcasted_iota(I32, (rows, n), 1)

    def count(ones):
        return jnp.sum(ones, axis=1, keepdims=True)

    def at_least(word):
        return jnp.where(a >= pltpu.bitcast(word, F32), 1, 0)

    thr = jnp.zeros((rows, 1), I32)
    for bit in range(30, -1, -1):
        cand = thr | (1 << bit)
        thr = jnp.where(count(at_least(cand)) >= cap, cand, thr)
    above = at_least(thr + 1)
    tied = at_least(thr) - above
    need = cap - count(above)
    last = jnp.zeros((rows, 1), I32)
    for bit in range(n.bit_length() - 2, -1, -1):
        cand = last | (1 << bit)
        last = jnp.where(count(jnp.where(tok < cand, tied, 0)) < need, cand, last)
    sel = above + jnp.where(tok <= last, tied, 0)

    blk = min(n, 256)
    tri = jnp.where(lax.broadcasted_iota(I32, (blk, blk), 0) <= lax.broadcasted_iota(I32, (blk, blk), 1),
                    1.0, 0.0).astype(BF16)
    sel_f = sel.astype(F32)
    offset = jnp.zeros((rows, 1), F32)
    for j in range(n // blk):
        s_blk = sel_f[:, j * blk:(j + 1) * blk]
        incl = jnp.dot(s_blk.astype(BF16), tri, preferred_element_type=F32)
        pos = (incl - s_blk + offset).astype(I32)
        possel_ref[:, j * blk:(j + 1) * blk] = jnp.where(sel[:, j * blk:(j + 1) * blk] > 0, pos, -1)
        offset = offset + incl[:, blk - 1:blk]

    tok_row = lax.broadcasted_iota(I32, (1, n), 1)

    def per_row(e, _):
        pe = possel_ref[pl.ds(e, 1), :]
        ae = aff_ref[pl.ds(e, 1), :]

        def per_chunk(c, _):
            r0 = pl.multiple_of(c * row_chunk, row_chunk)
            slot = lax.broadcasted_iota(I32, (row_chunk, 1), 0) + r0
            hit = pe == slot
            idx_ref[e, pl.ds(r0, row_chunk), :] = jnp.sum(jnp.where(hit, tok_row, 0), axis=1, keepdims=True)
            val_ref[e, pl.ds(r0, row_chunk), :] = jnp.sum(jnp.where(hit, ae, 0.0), axis=1, keepdims=True)
            return 0

        return lax.fori_loop(0, cap // row_chunk, per_chunk, 0)

    lax.fori_loop(0, rows, per_row, 0)


def _select(aff_rows, rows_per_step, cap):
    r, n = aff_rows.shape
    row_chunk = min(cap, 64)
    return pl.pallas_call(
        functools.partial(_select_kernel, n=n, cap=cap, row_chunk=row_chunk),
        grid=(r // rows_per_step,),
        in_specs=[pl.BlockSpec((rows_per_step, n), lambda s: (s, 0))],
        out_specs=[pl.BlockSpec((rows_per_step, cap, 1), lambda s: (s, 0, 0))] * 2,
        out_shape=[jax.ShapeDtypeStruct((r, cap, 1), I32), jax.ShapeDtypeStruct((r, cap, 1), F32)],
        scratch_shapes=[pltpu.VMEM((rows_per_step, n), I32)],
        compiler_params=_params(("arbitrary",)),
        name="expert_select",
    )(aff_rows)


CTX_SLOTS = N_EXPERTS * CAP_CTX
SLOT_GROUP = 8


def _ctx_slot_onehot(idx_ref, slots_on_rows):
    idx = idx_ref[...].reshape(CTX_SLOTS, 1)
    if slots_on_rows:
        hit = idx == lax.broadcasted_iota(I32, (CTX_SLOTS, SEQ), 1)
    else:
        idx_lane = jnp.broadcast_to(idx.astype(F32), (CTX_SLOTS, LANE)).T[0:1, :]
        hit = idx_lane == lax.broadcasted_iota(I32, (SEQ, CTX_SLOTS), 0).astype(F32)
    return jnp.where(hit, 1.0, 0.0).astype(BF16)


def _gather_ctx_kernel(idx_ref, h_ref, out_ref):
    onehot = _ctx_slot_onehot(idx_ref, True)
    lo, hi = _unpack_halves(h_ref[...])
    g_lo = jnp.dot(onehot, lo, preferred_element_type=F32)
    g_hi = jnp.dot(onehot, hi, preferred_element_type=F32)
    packed = pltpu.pack_elementwise([g_lo, g_hi], packed_dtype=BF16)
    out_ref[...] = packed.reshape(N_EXPERTS, CAP_CTX, D_MODEL // 2)


def _gather_ctx(idx_c, h2p):
    return pl.pallas_call(
        _gather_ctx_kernel,
        grid=(BATCH,),
        in_specs=[pl.BlockSpec((N_EXPERTS, CAP_CTX, 1), lambda b: (b, 0, 0)),
                  pl.BlockSpec((SEQ, D_MODEL // 2), lambda b: (b, 0))],
        out_specs=pl.BlockSpec((N_EXPERTS, CAP_CTX, D_MODEL // 2), lambda b: (0, b, 0)),
        out_shape=jax.ShapeDtypeStruct((N_EXPERTS, BATCH * CAP_CTX, D_MODEL // 2), jnp.uint32),
        compiler_params=_params(("arbitrary",)),
        name="gather_ctx",
    )(idx_c, h2p)


def _gather_lat_kernel(idx_ref, src_ref, out_ref):
    base = (pl.program_id(0) * N_EXPERTS + pl.program_id(1)) * CAP_LAT

    def body(it, _):
        r0 = pl.multiple_of(it * SLOT_GROUP, SLOT_GROUP)
        picked = [src_ref[0, pl.ds(idx_ref[base + r0 + k], 1), :] for k in range(SLOT_GROUP)]
        for k in range(SLOT_GROUP):
            out_ref[0, pl.ds(r0 + k, 1), :] = picked[k]
        return 0

    lax.fori_loop(0, CAP_LAT // SLOT_GROUP, body, 0)


def _gather_lat(idx_flat, h2p3, off):
    return pl.pallas_call(
        _gather_lat_kernel,
        grid_spec=pltpu.PrefetchScalarGridSpec(
            num_scalar_prefetch=1,
            grid=(DEC_BATCH, N_EXPERTS),
            in_specs=[pl.BlockSpec((1, DEC_SEQ, D_MODEL // 2), lambda b, e, idx: (off + b, 0, 0))],
            out_specs=pl.BlockSpec((1, CAP_LAT, D_MODEL // 2), lambda b, e, idx: (e, b, 0)),
        ),
        out_shape=jax.ShapeDtypeStruct((N_EXPERTS, DEC_BATCH * CAP_LAT, D_MODEL // 2), jnp.uint32),
        compiler_params=_params(("arbitrary", "arbitrary")),
        name="gather_lat",
    )(idx_flat, h2p3)


N_CTX_FFN_TILES = BATCH * CAP_CTX // FFN_ROW_TILE


def _ffn_kernel(xc_ref, xl_ref, vc_ref, vl_ref, wg_ref, wu_ref, wd_ref, o_ref, wg_b, wu_b, wd_b):
    j = pl.program_id(1)

    @pl.when(j == 0)
    def _():
        wg_b[...] = wg_ref[0].astype(BF16)
        wu_b[...] = wu_ref[0].astype(BF16)
        wd_b[...] = wd_ref[0].astype(BF16)

    is_ctx = j < N_CTX_FFN_TILES
    x = jnp.where(is_ctx, jnp.concatenate(_unpack_halves(xc_ref[0]), axis=1),
                  jnp.concatenate(_unpack_halves(xl_ref[0]), axis=1))
    g = jnp.dot(x, wg_b[...], preferred_element_type=F32)
    u = jnp.dot(x, wu_b[...], preferred_element_type=F32)
    hh = (g * _sigmoid(g)) * u
    y = jnp.dot(hh.astype(BF16), wd_b[...], preferred_element_type=F32)
    o_ref[0] = y * jnp.where(is_ctx, vc_ref[0], vl_ref[0])


def _expert_ffn(xg_ctx, xg_lat, val_ctx, val_lat, w_gate, w_up, w_down, layer):
    tr = FFN_ROW_TILE
    n_tiles = ROWS_PER_EXPERT // tr
    def wspec(k, n, tiles_held):
        ahead = lambda e, j: jnp.minimum(e + jnp.where(j >= tiles_held, 1, 0), N_EXPERTS - 1)
        return pl.BlockSpec((None, 1, k, n), lambda e, j: (layer, ahead(e, j), 0, 0))

    ctx_tile = lambda j: jnp.minimum(j, N_CTX_FFN_TILES - 1)
    lat_tile = lambda j: jnp.maximum(j - N_CTX_FFN_TILES, 0)
    return pl.pallas_call(
        _ffn_kernel,
        grid=(N_EXPERTS, n_tiles),
        in_specs=[pl.BlockSpec((1, tr, D_MODEL // 2), lambda e, j: (e, ctx_tile(j), 0)),
                  pl.BlockSpec((1, tr, D_MODEL // 2), lambda e, j: (e, lat_tile(j), 0)),
                  pl.BlockSpec((1, tr, 1), lambda e, j: (e, ctx_tile(j), 0)),
                  pl.BlockSpec((1, tr, 1), lambda e, j: (e, lat_tile(j), 0)),
                  wspec(D_MODEL, EXPERT_FF, 1), wspec(D_MODEL, EXPERT_FF, n_tiles - 1),
                  wspec(EXPERT_FF, D_MODEL, n_tiles)],
        out_specs=pl.BlockSpec((1, tr, D_MODEL), lambda e, j: (e, j, 0)),
        out_shape=jax.ShapeDtypeStruct((N_EXPERTS, ROWS_PER_EXPERT, D_MODEL), F32),
        scratch_shapes=[pltpu.VMEM((D_MODEL, EXPERT_FF), BF16), pltpu.VMEM((D_MODEL, EXPERT_FF), BF16),
                        pltpu.VMEM((EXPERT_FF, D_MODEL), BF16)],
        compiler_params=_params(("arbitrary", "arbitrary")),
        name="expert_ffn",
    )(xg_ctx, xg_lat, val_ctx, val_lat, w_gate, w_up, w_down)


def _scatter_ctx_kernel(idx_ref, y_ref, x1_ref, g2_ref, out_ref):
    onehot = _ctx_slot_onehot(idx_ref, False)
    y_hi, y_lo = _split_bf16(y_ref[...].reshape(CTX_SLOTS, D_MODEL))
    moe = jnp.dot(onehot, y_hi, preferred_element_type=F32) + jnp.dot(onehot, y_lo, preferred_element_type=F32)
    out_ref[...] = x1_ref[...] + g2_ref[...] * moe


def _scatter_ctx(idx_c, yg, x1, g2):
    return pl.pallas_call(
        _scatter_ctx_kernel,
        grid=(BATCH,),
        in_specs=[pl.BlockSpec((N_EXPERTS, CAP_CTX, 1), lambda b: (b, 0, 0)),
                  pl.BlockSpec((N_EXPERTS, CAP_CTX, D_MODEL), lambda b: (0, b, 0)),
                  pl.BlockSpec((SEQ, D_MODEL), lambda b: (b, 0)),
                  pl.BlockSpec((None, 1, D_MODEL), lambda b: (0, 0, 0))],
        out_specs=pl.BlockSpec((SEQ, D_MODEL), lambda b: (b, 0)),
        out_shape=jax.ShapeDtypeStruct((T_CTX, D_MODEL), F32),
        compiler_params=_params(("arbitrary",)),
        name="scatter_ctx",
    )(idx_c, yg, x1, g2)


def _scatter_lat_kernel(idx_ref, y_ref, x1_ref, g2_ref, out_ref):
    e = pl.program_id(2)

    @pl.when(e == 0)
    def _():
        out_ref[...] = jnp.zeros_like(out_ref)

    base = (pl.program_id(0) * N_EXPERTS + e) * CAP_LAT

    def body(it, _):
        r0 = pl.multiple_of(it * SLOT_GROUP, SLOT_GROUP)
        rows = [idx_ref[base + r0 + k] for k in range(SLOT_GROUP)]
        old = [out_ref[0, pl.ds(rows[k], 1), :] for k in range(SLOT_GROUP)]
        y = y_ref[0, pl.ds(r0, SLOT_GROUP), :]
        for k in range(SLOT_GROUP):
            out_ref[0, pl.ds(rows[k], 1), :] = old[k] + y[k:k + 1, :]
        return 0

    lax.fori_loop(0, CAP_LAT // SLOT_GROUP, body, 0)

    @pl.when(e == N_EXPERTS - 1)
    def _():
        out_ref[0] = x1_ref[0] + g2_ref[...] * out_ref[0]


def _scatter_lat(idx_flat, yg, x1_3, g2, off, n_split):
    w = D_MODEL // n_split
    blk0 = BATCH * CAP_CTX // CAP_LAT
    return pl.pallas_call(
        _scatter_lat_kernel,
        grid_spec=pltpu.PrefetchScalarGridSpec(
            num_scalar_prefetch=1,
            grid=(DEC_BATCH, n_split, N_EXPERTS),
            in_specs=[pl.BlockSpec((1, CAP_LAT, w), lambda b, h, e, idx: (e, blk0 + b, h)),
                      pl.BlockSpec((1, DEC_SEQ, w), lambda b, h, e, idx: (off + b, 0, h)),
                      pl.BlockSpec((None, 1, w), lambda b, h, e, idx: (off + b, 0, h))],
            out_specs=pl.BlockSpec((1, DEC_SEQ, w), lambda b, h, e, idx: (b, 0, h)),
        ),
        out_shape=jax.ShapeDtypeStruct((DEC_BATCH, DEC_SEQ, D_MODEL), F32),
        compiler_params=_params(("arbitrary", "arbitrary", "arbitrary")),
        name="scatter_lat",
    )(idx_flat, yg, x1_3, g2)


def _rope_tables():
    pos = np.arange(DEC_SEQ)
    freq = (np.float32(ROPE_THETA) ** (-np.arange(ROPE_FREQS, dtype=np.float32) / np.float32(ROPE_FREQS)))
    ang_r = (pos // GRID_W).astype(np.float32)[:, None] * freq.astype(np.float32)
    ang_c = (pos % GRID_W).astype(np.float32)[:, None] * freq.astype(np.float32)
    cos = np.concatenate([np.cos(ang_r)] * 2 + [np.cos(ang_c)] * 2, axis=-1)
    sin = np.concatenate([-np.sin(ang_r), np.sin(ang_r), -np.sin(ang_c), np.sin(ang_c)], axis=-1)
    reps = LANE // HEAD_DIM
    cs = np.concatenate([np.ones((ROW_TILE, LANE)), np.tile(cos, (1, reps))], axis=0).astype(np.float32)
    sn = np.concatenate([np.zeros((ROW_TILE, LANE)), np.tile(sin, (1, reps))], axis=0).astype(np.float32)
    return cs, sn


def _rope_tile(i):
    lat = jnp.maximum(i - N_CTX_TILES, 0) % (DEC_SEQ // ROW_TILE)
    return jnp.where(i < N_CTX_TILES, 0, 1 + lat)


def _block_diag_ones(width, group):
    g = jnp.arange(width) // group
    return (g[:, None] == g[None, :]).astype(BF16)


def _qk_gain(q_norm, k_norm):
    q = jnp.tile(q_norm, N_HEADS) * (HEAD_DIM ** -0.5 * LOG2_E)
    return jnp.concatenate([q, jnp.tile(k_norm, N_KV)])[None, :]


def kernel(x_prompt, x_sample, cache_a_k, cache_a_v, cache_c_k, cache_c_v, c, c_ctx, norm1_g, w_mod, b_mod, w_in,
           a_q_norm, a_k_norm, a_sink, b_v_norm, b_ws, b_bs, c_q_norm, c_k_norm, w_a_o, w_b_o, w_c_o, w_out, norm2_g,
           w_router, b_router, w_gate, w_up, w_down):
    cond8 = jnp.concatenate([c_ctx[None, :], c, jnp.zeros((8 - N_REQ, D_MODEL), F32)], axis=0)
    mods = _modulation(cond8, w_mod, b_mod)

    cs, sn = _rope_tables()
    bd_qk = _block_diag_ones(QK_W, HEAD_DIM)
    bd_b = _block_diag_ones(B_WIDTH, B_GROUP_CH)
    w_in_b = w_in.astype(BF16)
    wa_b, wb_b, wc_b, wo_b = w_a_o.astype(BF16), w_b_o.astype(BF16), w_c_o.astype(BF16), w_out.astype(BF16)
    ws_b = b_ws.astype(BF16)
    wr_pad = jnp.pad(w_router, ((0, 0), (0, 0), (0, LANE - N_EXPERTS))).astype(BF16)
    br_pad = jnp.pad(b_router, ((0, 0), (0, LANE - N_EXPERTS)), constant_values=NEG_BIG)

    by_seq = lambda a: a.reshape(T_ALL // SEQ, SEQ, a.shape[-1])
    by_dec = lambda a: a.reshape(T_ALL // DEC_SEQ, DEC_SEQ, a.shape[-1])
    lat_off = T_CTX // DEC_SEQ

    x_ctx = x_prompt.reshape(T_CTX, D_MODEL)
    x_lat = x_sample.reshape(T_LAT, D_MODEL)
    new_kv = [[], [], [], []]
    for l in range(DEPTH):
        sh1, sc1, g1, sh2, sc2, g2 = [mods[l, :N_REQ, i * D_MODEL:(i + 1) * D_MODEL].reshape(N_REQ, 1, D_MODEL)
                                      for i in range(6)]
        qa, ka, va, bu, bv, qc, kc, vc, gt = _input_projection(
            x_ctx, x_lat, sc1, sh1, norm1_g[l][None, :], w_in_b[l], cs, sn,
            _qk_gain(a_q_norm[l], a_k_norm[l]), _qk_gain(c_q_norm[l], c_k_norm[l]), b_v_norm[l][None, :], bd_qk, bd_b)
        for lst, arr in zip(new_kv, (ka, va, kc, vc)):
            lst.append(arr[:T_CTX].reshape(BATCH, SEQ, N_KV, HEAD_DIM))

        ka_b, va_b, kc_b, vc_b = (a.astype(BF16) for a in (ka, va, kc, vc))
        sink = a_sink[l]
        oa_ctx = _dense_attention(by_seq(qa), by_seq(ka_b), by_seq(va_b), None, sink,
                                  n_req=BATCH, off=0, tq=SEQ, key_chunk=SEQ)
        oc_ctx = _dense_attention(by_seq(qc), by_seq(kc_b), by_seq(vc_b), None, None,
                                  n_req=BATCH, off=0, tq=SEQ, key_chunk=SEQ)
        flat_cache = lambda a: a[:, l].reshape(DEC_BATCH, PAST_LEN, KV_W).astype(BF16)
        oa_lat = _window_attention(by_dec(qa), by_dec(ka_b), by_dec(va_b), flat_cache(cache_a_k),
                                   flat_cache(cache_a_v), sink, n_req=DEC_BATCH, off=lat_off)
        oc_lat = _dense_attention(by_dec(qc), by_dec(kc_b), by_dec(vc_b), (flat_cache(cache_c_k), flat_cache(cache_c_v)),
                                  None, n_req=DEC_BATCH, off=lat_off, tq=1024, key_chunk=512)

        bs_full = jnp.repeat(b_bs[l].T, B_GROUP_CH, axis=1)
        x1, h2p, afft = _merge(x_ctx, x_lat, oa_ctx.reshape(T_CTX, Q_W), oa_lat.reshape(T_LAT, Q_W), bu, bv,
                               oc_ctx.reshape(T_CTX, Q_W), oc_lat.reshape(T_LAT, Q_W), gt,
                               wa_b[l], wb_b[l], wc_b[l], wo_b[l], ws_b[l], bs_full,
                               g1, sc2, sh2, norm2_g[l][None, :], wr_pad[l], br_pad[l][None, :])

        aff_rows = lambda a, n_req, n: a.reshape(N_EXPERTS, n_req, n).transpose(1, 0, 2).reshape(n_req * N_EXPERTS, n)
        idx_c, val_c = _select(aff_rows(afft[:, :T_CTX], BATCH, SEQ), BATCH * N_EXPERTS, CAP_CTX)
        idx_l, val_l = _select(aff_rows(afft[:, T_CTX:], DEC_BATCH, DEC_SEQ), N_EXPERTS, CAP_LAT)
        idx_l_flat = idx_l.reshape(-1)
        xg_ctx = _gather_ctx(idx_c, h2p)
        xg_lat = _gather_lat(idx_l_flat, by_dec(h2p), lat_off)
        per_expert = lambda v, n_req, cap: v.reshape(n_req, N_EXPERTS, cap).transpose(1, 0, 2).reshape(
            N_EXPERTS, n_req * cap, 1)
        yg = _expert_ffn(xg_ctx, xg_lat, per_expert(val_c, BATCH, CAP_CTX), per_expert(val_l, DEC_BATCH, CAP_LAT),
                         w_gate, w_up, w_down, l)

        x_ctx = _scatter_ctx(idx_c, yg, x1, g2)
        x_lat = _scatter_lat(idx_l_flat, yg, by_dec(x1), g2, lat_off, 2).reshape(T_LAT, D_MODEL)

    y_prompt = x_ctx.reshape(BATCH, SEQ, D_MODEL)
    y_sample = x_lat.reshape(DEC_BATCH, DEC_SEQ, D_MODEL)
    return (y_prompt, y_sample) + tuple(jnp.stack(lst, axis=1) for lst in new_kv)
```

```python
import functools

import jax
import numpy as np
import jax.numpy as jnp
from jax import lax
from jax.experimental import pallas as pl
from jax.experimental.pallas import tpu as pltpu

F32 = jnp.float32
BF16 = jnp.bfloat16
I32 = jnp.int32

D_MODEL = 1024
BATCH = 16
SEQ = 256
DEPTH = 2
DEC_BATCH = 2
DEC_SEQ = 4096
PAST_LEN = 256
GRID_W = 64
HEAD_DIM = 64
N_HEADS = 6
N_KV = 2
N_GRP = N_HEADS // N_KV
B_GROUPS = 4
B_GROUP_CH = 64
B_WIDTH = B_GROUPS * B_GROUP_CH
Q_W = N_HEADS * HEAD_DIM
KV_W = N_KV * HEAD_DIM
QK_W = Q_W + KV_W
N_BRANCH = 3
WINDOW = 128
BLOCK = 128
CHUNK = 128
N_EXPERTS = 16
EXPERT_FF = 1024
CAP_FACTOR = 2
ROPE_THETA = 10000.0
ROPE_FREQS = HEAD_DIM // 4
EPS = 1e-6
IN_WIDTH = 2 * (QK_W + KV_W) + 2 * B_WIDTH + N_BRANCH * D_MODEL

T_CTX = BATCH * SEQ
T_LAT = DEC_BATCH * DEC_SEQ
T_ALL = T_CTX + T_LAT
N_REQ = 1 + DEC_BATCH
CAP_CTX = CAP_FACTOR * SEQ // N_EXPERTS
CAP_LAT = CAP_FACTOR * DEC_SEQ // N_EXPERTS
ROWS_PER_EXPERT = BATCH * CAP_CTX + DEC_BATCH * CAP_LAT

LANE = 128
ROW_TILE = 512
N_CTX_TILES = T_CTX // ROW_TILE
FFN_ROW_TILE = 512
VMEM_LIMIT = 56 * 1024 * 1024
NEG_BIG = -1e30
LOG2_E = 1.4426950408889634

OFF_A = 0
OFF_AV = OFF_A + QK_W
OFF_BU = OFF_AV + KV_W
OFF_BV = OFF_BU + B_WIDTH
OFF_C = OFF_BV + B_WIDTH
OFF_CV = OFF_C + QK_W
OFF_G = OFF_CV + KV_W


def _params(sem, vmem=VMEM_LIMIT):
    return pltpu.CompilerParams(dimension_semantics=sem, vmem_limit_bytes=vmem)


def _sigmoid(x):
    return 1.0 / (1.0 + jnp.exp(-x))


def _gelu_tanh(x):
    return 0.5 * x * (1.0 + jnp.tanh(0.7978845608028654 * (x + 0.044715 * (x * x * x))))


def _split_bf16(x):
    hi = x.astype(BF16)
    lo = (x - hi.astype(F32)).astype(BF16)
    return hi, lo


def _mod_kernel(c_ref, w_ref, b_ref, o_ref):
    c = c_ref[...]
    s_hi, s_lo = _split_bf16(c * _sigmoid(c))
    w_hi, w_lo = _split_bf16(w_ref[0])
    acc = jnp.dot(s_hi, w_hi, preferred_element_type=F32)
    acc += jnp.dot(s_lo, w_hi, preferred_element_type=F32)
    acc += jnp.dot(s_hi, w_lo, preferred_element_type=F32)
    o_ref[0] = acc + b_ref[0]


def _modulation(cond8, w_mod, b_mod):
    n_col = 6 * D_MODEL // D_MODEL
    return pl.pallas_call(
        _mod_kernel,
        grid=(DEPTH, n_col),
        in_specs=[
            pl.BlockSpec((8, D_MODEL), lambda l, j: (0, 0)),
            pl.BlockSpec((1, D_MODEL, D_MODEL), lambda l, j: (l, 0, j)),
            pl.BlockSpec((1, 1, D_MODEL), lambda l, j: (l, 0, j)),
        ],
        out_specs=pl.BlockSpec((1, 8, D_MODEL), lambda l, j: (l, 0, j)),
        out_shape=jax.ShapeDtypeStruct((DEPTH, 8, 6 * D_MODEL), F32),
        compiler_params=_params(("arbitrary", "arbitrary")),
        name="modulation",
    )(cond8, w_mod, b_mod.reshape(DEPTH, 1, 6 * D_MODEL))


def _group_sumsq(y, bd_ref):
    return jnp.dot((y * y).astype(BF16), bd_ref[...], preferred_element_type=F32)


def _pick_pass(i, ctx_ref, lat_ref):
    return jnp.where(i < N_CTX_TILES, ctx_ref[...], lat_ref[...])


def _in_kernel(xc_ref, xl_ref, sc_ref, sh_ref, n1_ref, w_ref, cs_ref, sn_ref, ga_ref, gc_ref, gbv_ref, bd_qk_ref,
               bd_b_ref, qa_ref, ka_ref, va_ref, bu_ref, bv_ref, qc_ref, kc_ref, vc_ref, gt_ref):
    x = _pick_pass(pl.program_id(0), xc_ref, xl_ref)
    ms = jnp.mean(x * x, axis=-1, keepdims=True)
    h = x * lax.rsqrt(ms + EPS) * n1_ref[...]
    h = h * (1.0 + sc_ref[...]) + sh_ref[...]
    hb = h.astype(BF16)
    tm = x.shape[0]

    def proj(c0, width):
        return jnp.dot(hb, w_ref[:, c0:c0 + width], preferred_element_type=F32)

    cs = jnp.concatenate([cs_ref[...]] * (QK_W // LANE), axis=1)
    sn = jnp.concatenate([sn_ref[...]] * (QK_W // LANE), axis=1)
    lane = lax.broadcasted_iota(I32, (tm, QK_W), 1)
    first_half = (lane & ROPE_FREQS) == 0

    def qk_post(y, gain_ref):
        yn = y * lax.rsqrt(_group_sumsq(y, bd_qk_ref) * (1.0 / HEAD_DIM) + EPS) * gain_ref[...]
        partner = jnp.where(first_half, pltpu.roll(yn, QK_W - ROPE_FREQS, 1), pltpu.roll(yn, ROPE_FREQS, 1))
        return yn * cs + partner * sn

    ya = qk_post(proj(OFF_A, QK_W), ga_ref)
    qa_ref[...] = ya[:, :Q_W].astype(BF16)
    ka_ref[...] = ya[:, Q_W:]
    va_ref[...] = proj(OFF_AV, KV_W)

    bu_ref[...] = _gelu_tanh(proj(OFF_BU, B_WIDTH)).astype(BF16)
    gv = _gelu_tanh(proj(OFF_BV, B_WIDTH))
    gvn = gv * lax.rsqrt(_group_sumsq(gv, bd_b_ref) * (1.0 / B_GROUP_CH) + EPS) * gbv_ref[...]
    bv_ref[...] = gvn.astype(BF16)

    yc = qk_post(proj(OFF_C, QK_W), gc_ref)
    qc_ref[...] = yc[:, :Q_W].astype(BF16)
    kc_ref[...] = yc[:, Q_W:]
    vc_ref[...] = proj(OFF_CV, KV_W)

    gate_chunk = 512
    for j in range(N_BRANCH * D_MODEL // gate_chunk):
        g = proj(OFF_G + j * gate_chunk, gate_chunk)
        gt_ref[:, j * gate_chunk:(j + 1) * gate_chunk] = _sigmoid(g).astype(BF16)


def _req_of_tile(i):
    return i // N_CTX_TILES


def _ctx_rows(w):
    return pl.BlockSpec((ROW_TILE, w), lambda i: (jnp.minimum(i, N_CTX_TILES - 1), 0))


def _lat_rows(w):
    return pl.BlockSpec((ROW_TILE, w), lambda i: (jnp.maximum(i - N_CTX_TILES, 0), 0))


def _input_projection(x_ctx, x_lat, sc1, sh1, n1, w_in_b, cs, sn, gain_a, gain_c, gain_bv, bd_qk, bd_b):
    tm = ROW_TILE
    row = lambda w: pl.BlockSpec((tm, w), lambda i: (i, 0))
    full = lambda a: pl.BlockSpec(a.shape, lambda i: (0,) * a.ndim)
    modspec = pl.BlockSpec((None, 1, D_MODEL), lambda i: (_req_of_tile(i), 0, 0))
    rope = pl.BlockSpec((tm, LANE), lambda i: (_rope_tile(i), 0))
    outs = [(Q_W, BF16), (KV_W, F32), (KV_W, F32), (B_WIDTH, BF16), (B_WIDTH, BF16),
            (Q_W, BF16), (KV_W, F32), (KV_W, F32), (N_BRANCH * D_MODEL, BF16)]
    return pl.pallas_call(
        _in_kernel,
        grid=(T_ALL // tm,),
        in_specs=[_ctx_rows(D_MODEL), _lat_rows(D_MODEL), modspec, modspec, full(n1), full(w_in_b), rope, rope,
                  full(gain_a), full(gain_c), full(gain_bv), full(bd_qk), full(bd_b)],
        out_specs=[row(w) for w, _ in outs],
        out_shape=[jax.ShapeDtypeStruct((T_ALL, w), dt) for w, dt in outs],
        compiler_params=_params(("arbitrary",)),
        name="input_projection",
    )(x_ctx, x_lat, sc1, sh1, n1, w_in_b, cs, sn, gain_a, gain_c, gain_bv, bd_qk, bd_b)


def _attention_tile(q_ref, sources, sink_ref, o_ref, qt_scr, ot_scr, *, tq, key_chunk):
    width = N_GRP * tq
    for j in range(Q_W // LANE):
        qt_scr[j * LANE:(j + 1) * LANE, :] = q_ref[0, :, j * LANE:(j + 1) * LANE].astype(F32).T.astype(BF16)
    for kv in range(N_KV):
        lo, hi = kv * HEAD_DIM, (kv + 1) * HEAD_DIM
        heads = [kv * N_GRP + g for g in range(N_GRP)]
        qt = jnp.concatenate([qt_scr[h * HEAD_DIM:(h + 1) * HEAD_DIM, :] for h in heads], axis=1)

        def step(carry, kref, vref, c0, size, bias):
            m, l, acc = carry
            s = jnp.dot(kref[0, pl.ds(c0, size), lo:hi], qt, preferred_element_type=F32)
            if bias is not None:
                s = s + jnp.concatenate([bias] * N_GRP, axis=1)
            vt = vref[0, pl.ds(c0, size), :].astype(F32).T[lo:hi, :].astype(BF16)
            m_new = jnp.maximum(m, jnp.max(s, axis=0, keepdims=True))
            alpha = jnp.exp2(m - m_new)
            p = jnp.exp2(s - m_new)
            l = l * alpha + jnp.sum(p, axis=0, keepdims=True)
            acc = acc * alpha + jnp.dot(vt, p.astype(BF16), preferred_element_type=F32)
            return m_new, l, acc

        if sink_ref is not None:
            m0 = jnp.concatenate([jnp.full((1, tq), sink_ref[h] * LOG2_E, F32) for h in heads], axis=1)
            l0 = jnp.ones((1, width), F32)
        else:
            m0 = jnp.full((1, width), NEG_BIG, F32)
            l0 = jnp.zeros((1, width), F32)
        carry = (m0, l0, jnp.zeros((HEAD_DIM, width), F32))
        for kref, vref, bias in sources:
            n_rows = kref.shape[1]
            n_full = n_rows // key_chunk
            if bias is not None:
                carry = step(carry, kref, vref, 0, n_rows, bias)
                continue
            if n_full > 1:
                carry = lax.fori_loop(
                    0, n_full,
                    lambda c, cr: step(cr, kref, vref, pl.multiple_of(c * key_chunk, key_chunk), key_chunk, None), carry)
            elif n_full == 1:
                carry = step(carry, kref, vref, 0, key_chunk, None)
            if n_rows - n_full * key_chunk:
                carry = step(carry, kref, vref, n_full * key_chunk, n_rows - n_full * key_chunk, None)
        _, l, acc = carry
        o = acc / l
        for g, h in enumerate(heads):
            ot_scr[h * HEAD_DIM:(h + 1) * HEAD_DIM, :] = o[:, g * tq:(g + 1) * tq]
    for j in range(Q_W // LANE):
        o_ref[0, :, j * LANE:(j + 1) * LANE] = ot_scr[j * LANE:(j + 1) * LANE, :].T.astype(o_ref.dtype)


def _dense_attn_kernel(*refs, tq, key_chunk, has_extra, has_sink):
    refs = list(refs)
    q_ref, k_ref, v_ref = refs[:3]
    del refs[:3]
    sources = [(k_ref, v_ref, None)]
    if has_extra:
        sources.append((refs.pop(0), refs.pop(0), None))
    sink_ref = refs.pop(0) if has_sink else None
    o_ref, qt_scr, ot_scr = refs
    _attention_tile(q_ref, sources, sink_ref, o_ref, qt_scr, ot_scr, tq=tq, key_chunk=key_chunk)


def _attention_scratch(tq):
    return [pltpu.VMEM((Q_W, tq), BF16), pltpu.VMEM((Q_W, tq), F32)]


def _dense_attention(q, k, v, extra, sink, *, n_req, off, tq, key_chunk):
    s = q.shape[1]
    kv_spec = pl.BlockSpec((1, s, KV_W), lambda i, j: (off + i, 0, 0))
    in_specs = [pl.BlockSpec((1, tq, Q_W), lambda i, j: (off + i, j, 0)), kv_spec, kv_spec]
    args = [q, k, v]
    if extra is not None:
        in_specs += [pl.BlockSpec((1, extra[0].shape[1], KV_W), lambda i, j: (i, 0, 0))] * 2
        args += list(extra)
    if sink is not None:
        in_specs.append(pl.BlockSpec(memory_space=pltpu.SMEM))
        args.append(sink)
    return pl.pallas_call(
        functools.partial(_dense_attn_kernel, tq=tq, key_chunk=key_chunk, has_extra=extra is not None,
                          has_sink=sink is not None),
        grid=(n_req, s // tq),
        in_specs=in_specs,
        out_specs=pl.BlockSpec((1, tq, Q_W), lambda i, j: (i, j, 0)),
        out_shape=jax.ShapeDtypeStruct((n_req, s, Q_W), BF16),
        scratch_shapes=_attention_scratch(tq),
        compiler_params=_params(("arbitrary", "arbitrary")),
        name="dense_attention",
    )(*args)


WINDOW_TQ = 512


def _window_attn_kernel(q_ref, kp_ref, kc_ref, kn_ref, vp_ref, vc_ref, vn_ref, ck_ref, cv_ref, bp_ref, bc_ref, bn_ref,
                        sink_ref, o_ref, qt_scr, ot_scr, *, seq):
    q_pos0 = pl.program_id(1) * WINDOW_TQ
    prev_bias = bp_ref[...] + jnp.where(q_pos0 >= BLOCK, 0.0, NEG_BIG)
    next_bias = bn_ref[...] + jnp.where(q_pos0 + WINDOW_TQ < seq, 0.0, NEG_BIG)
    sources = [(kp_ref, vp_ref, prev_bias), (kc_ref, vc_ref, bc_ref[...]), (kn_ref, vn_ref, next_bias),
               (ck_ref, cv_ref, None)]
    _attention_tile(q_ref, sources, sink_ref, o_ref, qt_scr, ot_scr, tq=WINDOW_TQ, key_chunk=WINDOW_TQ)


def _band_bias(first_key, n_keys):
    d = (first_key + np.arange(n_keys))[:, None] - np.arange(WINDOW_TQ)[None, :]
    return np.where(np.abs(d) <= WINDOW, 0.0, NEG_BIG).astype(np.float32)


def _window_attention(q, k, v, ck, cv, sink, *, n_req, off):
    b, s = n_req, q.shape[1]
    nb = s // BLOCK
    per_tile = WINDOW_TQ // BLOCK
    edge = lambda f: pl.BlockSpec((1, BLOCK, KV_W), lambda i, j: (off + i, f(j), 0))
    prev = lambda j: jnp.maximum(j * per_tile - 1, 0)
    nxt = lambda j: jnp.minimum((j + 1) * per_tile, nb - 1)
    cur = pl.BlockSpec((1, WINDOW_TQ, KV_W), lambda i, j: (off + i, j, 0))
    ctx = pl.BlockSpec((1, PAST_LEN, KV_W), lambda i, j: (i, 0, 0))
    biases = [_band_bias(-BLOCK, BLOCK), _band_bias(0, WINDOW_TQ), _band_bias(WINDOW_TQ, BLOCK)]
    table = lambda a: pl.BlockSpec(a.shape, lambda i, j: (0, 0))
    return pl.pallas_call(
        functools.partial(_window_attn_kernel, seq=s),
        grid=(b, s // WINDOW_TQ),
        in_specs=[pl.BlockSpec((1, WINDOW_TQ, Q_W), lambda i, j: (off + i, j, 0)),
                  edge(prev), cur, edge(nxt), edge(prev), cur, edge(nxt), ctx, ctx,
                  table(biases[0]), table(biases[1]), table(biases[2]),
                  pl.BlockSpec(memory_space=pltpu.SMEM)],
        out_specs=pl.BlockSpec((1, WINDOW_TQ, Q_W), lambda i, j: (i, j, 0)),
        out_shape=jax.ShapeDtypeStruct((b, s, Q_W), BF16),
        scratch_shapes=_attention_scratch(WINDOW_TQ),
        compiler_params=_params(("arbitrary", "arbitrary")),
        name="window_attention",
    )(q, k, k, k, v, v, v, ck, cv, *biases, sink)


def _pack_halves(x):
    half = x.shape[1] // 2
    return pltpu.pack_elementwise([x[:, :half], x[:, half:]], packed_dtype=BF16)


def _unpack_halves(words):
    return tuple(pltpu.unpack_elementwise(words, index=i, packed_dtype=BF16, unpacked_dtype=F32).astype(BF16)
                 for i in range(2))


def _merge_kernel(xc_ref, xl_ref, oac_ref, oal_ref, bu_ref, bv_ref, occ_ref, ocl_ref, gt_ref, wa_ref, wb_ref, wc_ref,
                  wo_ref, ws_ref, bs_ref, g1_ref, sc2_ref, sh2_ref, n2_ref, wr_ref, br_ref, x1_ref, h2p_ref, afft_ref):
    i = pl.program_id(0)
    tm = xc_ref.shape[0]
    group = lax.broadcasted_iota(I32, (CHUNK, B_WIDTH), 1) // B_GROUP_CH
    obs = []
    for c in range(tm // CHUNK):
        v = bv_ref[c * CHUNK:(c + 1) * CHUNK, :]
        sv = jnp.zeros((CHUNK, B_WIDTH), F32)
        for g in range(B_GROUPS):
            sv = jnp.where(group == g, jnp.dot(ws_ref[g], v, preferred_element_type=F32), sv)
        u = bu_ref[c * CHUNK:(c + 1) * CHUNK, :].astype(F32)
        obs.append((u * (sv + bs_ref[...])).astype(BF16))
    ob = jnp.concatenate(obs, axis=0)

    oa = _pick_pass(i, oac_ref, oal_ref)
    oc = _pick_pass(i, occ_ref, ocl_ref)
    merged = gt_ref[:, 0:D_MODEL].astype(F32) * jnp.dot(oa, wa_ref[...], preferred_element_type=F32)
    merged += gt_ref[:, D_MODEL:2 * D_MODEL].astype(F32) * jnp.dot(ob, wb_ref[...], preferred_element_type=F32)
    merged += gt_ref[:, 2 * D_MODEL:3 * D_MODEL].astype(F32) * jnp.dot(oc, wc_ref[...], preferred_element_type=F32)
    y = jnp.dot(merged.astype(BF16), wo_ref[...], preferred_element_type=F32)
    x1 = _pick_pass(i, xc_ref, xl_ref) + g1_ref[...] * y
    x1_ref[...] = x1

    ms = jnp.mean(x1 * x1, axis=-1, keepdims=True)
    h2 = x1 * lax.rsqrt(ms + EPS) * n2_ref[...]
    h2 = h2 * (1.0 + sc2_ref[...]) + sh2_ref[...]
    h2p_ref[...] = _pack_halves(h2)

    logits = jnp.dot(h2.astype(BF16), wr_ref[...], preferred_element_type=F32) + br_ref[...]
    e = jnp.exp(logits - jnp.max(logits, axis=-1, keepdims=True))
    aff = e / jnp.sum(e, axis=-1, keepdims=True)
    afft_ref[...] = aff.T[:N_EXPERTS, :]


def _merge(x_ctx, x_lat, oa_ctx, oa_lat, bu, bv, oc_ctx, oc_lat, gt, wa, wb, wc, wo, ws, bs, g1, sc2, sh2, n2, wr, br):
    tm = ROW_TILE
    row = lambda w: pl.BlockSpec((tm, w), lambda i: (i, 0))
    full = lambda a: pl.BlockSpec(a.shape, lambda i: (0,) * a.ndim)
    modspec = pl.BlockSpec((None, 1, D_MODEL), lambda i: (_req_of_tile(i), 0, 0))
    return pl.pallas_call(
        _merge_kernel,
        grid=(T_ALL // tm,),
        in_specs=[_ctx_rows(D_MODEL), _lat_rows(D_MODEL), _ctx_rows(Q_W), _lat_rows(Q_W), row(B_WIDTH), row(B_WIDTH),
                  _ctx_rows(Q_W), _lat_rows(Q_W), row(N_BRANCH * D_MODEL),
                  full(wa), full(wb), full(wc), full(wo), full(ws), full(bs),
                  modspec, modspec, modspec, full(n2), full(wr), full(br)],
        out_specs=[row(D_MODEL), row(D_MODEL // 2), pl.BlockSpec((N_EXPERTS, tm), lambda i: (0, i))],
        out_shape=[jax.ShapeDtypeStruct((T_ALL, D_MODEL), F32), jax.ShapeDtypeStruct((T_ALL, D_MODEL // 2), jnp.uint32),
                   jax.ShapeDtypeStruct((N_EXPERTS, T_ALL), F32)],
        compiler_params=_params(("arbitrary",)),
        name="merge_router",
    )(x_ctx, x_lat, oa_ctx, oa_lat, bu, bv, oc_ctx, oc_lat, gt, wa, wb, wc, wo, ws, bs, g1, sc2, sh2, n2, wr, br)


def _select_kernel(aff_ref, idx_ref, val_ref, possel_ref, idx_scr, val_scr, *, n, cap, row_chunk):
    a = aff_ref[...]
    rows = a.shape[0]
    tok = lax.broadcasted_iota(I32, (rows, n), 1)

    def count(ones):
        return jnp.sum(ones, axis=1, keepdims=True)

    def at_least(word):
        return jnp.where(a >= pltpu.bitcast(word, F32), 1, 0)

    thr = jnp.zeros((rows, 1), I32)
    for bit in range(30, -1, -1):
        cand = thr | (1 << bit)
        thr = jnp.where(count(at_least(cand)) >= cap, cand, thr)
    above = at_least(thr + 1)
    tied = at_least(thr) - above
    need = cap - count(above)
    last = jnp.zeros((rows, 1), I32)
    for bit in range(n.bit_length() - 2, -1, -1):
        cand = last | (1 << bit)
        last = jnp.where(count(jnp.where(tok < cand, tied, 0)) < need, cand, last)
    sel = above + jnp.where(tok <= last, tied, 0)

    blk = min(n, 256)
    tri = jnp.where(lax.broadcasted_iota(I32, (blk, blk), 0) <= lax.broadcasted_iota(I32, (blk, blk), 1),
                    1.0, 0.0).astype(BF16)
    sel_f = sel.astype(F32)
    offset = jnp.zeros((rows, 1), F32)
    for j in range(n // blk):
        s_blk = sel_f[:, j * blk:(j + 1) * blk]
        incl = jnp.dot(s_blk.astype(BF16), tri, preferred_element_type=F32)
        pos = (incl - s_blk + offset).astype(I32)
        possel_ref[:, j * blk:(j + 1) * blk] = jnp.where(sel[:, j * blk:(j + 1) * blk] > 0, pos, -1)
        offset = offset + incl[:, blk - 1:blk]

    tb = min(n, TOKEN_BLOCK)
    n_blk = n // tb

    def fold_lanes(x):
        acc = x[:, :LANE]
        for k in range(1, tb // LANE):
            acc = acc + x[:, k * LANE:(k + 1) * LANE]
        return acc

    def match(e, slot, t0):
        hit = possel_ref[pl.ds(e, 1), pl.ds(t0, tb)] == slot
        tok = t0 + lax.broadcasted_iota(I32, (1, tb), 1)
        return (fold_lanes(jnp.where(hit, tok, 0)),
                fold_lanes(jnp.where(hit, aff_ref[pl.ds(e, 1), pl.ds(t0, tb)], 0.0)))

    def per_row(e, _):
        ends, run = [], 0
        for j in range(n_blk - 1):
            run = run + jnp.sum(jnp.where(possel_ref[pl.ds(e, 1), j * tb:(j + 1) * tb] >= 0, 1, 0))
            ends.append(run)

        def per_chunk(c, _):
            r0 = pl.multiple_of(c * row_chunk, row_chunk)
            slot = lax.broadcasted_iota(I32, (row_chunk, 1), 0) + r0
            if n_blk == 1:
                idx, val = match(e, slot, 0)
            else:
                first = sum(jnp.where(end <= r0, 1, 0) for end in ends)
                last = 1 + sum(jnp.where(end < r0 + row_chunk, 1, 0) for end in ends)

                def per_block(j, acc):
                    i, v = match(e, slot, pl.multiple_of(j * tb, tb))
                    return acc[0] + i, acc[1] + v

                idx, val = lax.fori_loop(first, last, per_block,
                                         (jnp.zeros((row_chunk, LANE), I32), jnp.zeros((row_chunk, LANE), F32)))
            idx_scr[pl.ds(r0, row_chunk), :] = idx
            val_scr[pl.ds(r0, row_chunk), :] = val
            return 0

        lax.fori_loop(0, cap // row_chunk, per_chunk, 0)
        idx_ref[e] = jnp.sum(idx_scr[...], axis=1, keepdims=True)
        val_ref[e] = jnp.sum(val_scr[...], axis=1, keepdims=True)
        return 0

    def per_small_row(e, _):
        idx, val = match(e, lax.broadcasted_iota(I32, (cap, 1), 0), 0)
        idx_ref[e] = jnp.sum(idx, axis=1, keepdims=True)
        val_ref[e] = jnp.sum(val, axis=1, keepdims=True)
        return 0

    if n_blk == 1 and cap == row_chunk:
        lax.fori_loop(0, rows, per_small_row, 0, unroll=4)
    else:
        lax.fori_loop(0, rows, per_row, 0)


def _select(aff_rows, rows_per_step, cap):
    r, n = aff_rows.shape
    row_chunk = min(cap, 64)
    return pl.pallas_call(
        functools.partial(_select_kernel, n=n, cap=cap, row_chunk=row_chunk),
        grid=(r // rows_per_step,),
        in_specs=[pl.BlockSpec((rows_per_step, n), lambda s: (s, 0))],
        out_specs=[pl.BlockSpec((rows_per_step, cap, 1), lambda s: (s, 0, 0))] * 2,
        out_shape=[jax.ShapeDtypeStruct((r, cap, 1), I32), jax.ShapeDtypeStruct((r, cap, 1), F32)],
        scratch_shapes=[pltpu.VMEM((rows_per_step, n), I32), pltpu.VMEM((cap, LANE), I32), pltpu.VMEM((cap, LANE), F32)],
        compiler_params=_params(("arbitrary",)),
        name="expert_select",
    )(aff_rows)


CTX_SLOTS = N_EXPERTS * CAP_CTX
TOKEN_BLOCK = 512
SLOT_GROUP = 8


def _ctx_slot_onehot(idx_ref, slots_on_rows):
    idx = idx_ref[...].reshape(CTX_SLOTS, 1)
    if slots_on_rows:
        hit = idx == lax.broadcasted_iota(I32, (CTX_SLOTS, SEQ), 1)
    else:
        idx_lane = jnp.broadcast_to(idx.astype(F32), (CTX_SLOTS, LANE)).T[0:1, :]
        hit = idx_lane == lax.broadcasted_iota(I32, (SEQ, CTX_SLOTS), 0).astype(F32)
    return jnp.where(hit, 1.0, 0.0).astype(BF16)


def _gather_ctx_kernel(idx_ref, h_ref, out_ref):
    onehot = _ctx_slot_onehot(idx_ref, True)
    lo, hi = _unpack_halves(h_ref[...])
    g_lo = jnp.dot(onehot, lo, preferred_element_type=F32)
    g_hi = jnp.dot(onehot, hi, preferred_element_type=F32)
    packed = pltpu.pack_elementwise([g_lo, g_hi], packed_dtype=BF16)
    out_ref[...] = packed.reshape(N_EXPERTS, CAP_CTX, D_MODEL // 2)


def _gather_ctx(idx_c, h2p):
    return pl.pallas_call(
        _gather_ctx_kernel,
        grid=(BATCH,),
        in_specs=[pl.BlockSpec((N_EXPERTS, CAP_CTX, 1), lambda b: (b, 0, 0)),
                  pl.BlockSpec((SEQ, D_MODEL // 2), lambda b: (b, 0))],
        out_specs=pl.BlockSpec((N_EXPERTS, CAP_CTX, D_MODEL // 2), lambda b: (0, b, 0)),
        out_shape=jax.ShapeDtypeStruct((N_EXPERTS, BATCH * CAP_CTX, D_MODEL // 2), jnp.uint32),
        compiler_params=_params(("arbitrary",)),
        name="gather_ctx",
    )(idx_c, h2p)


def _gather_lat_kernel(idx_ref, src_ref, out_ref):
    base = (pl.program_id(0) * N_EXPERTS + pl.program_id(1)) * CAP_LAT

    def body(it, _):
        r0 = pl.multiple_of(it * SLOT_GROUP, SLOT_GROUP)
        picked = [src_ref[0, pl.ds(idx_ref[base + r0 + k], 1), :] for k in range(SLOT_GROUP)]
        for k in range(SLOT_GROUP):
            out_ref[0, pl.ds(r0 + k, 1), :] = picked[k]
        return 0

    lax.fori_loop(0, CAP_LAT // SLOT_GROUP, body, 0)


def _gather_lat(idx_flat, h2p3, off):
    return pl.pallas_call(
        _gather_lat_kernel,
        grid_spec=pltpu.PrefetchScalarGridSpec(
            num_scalar_prefetch=1,
            grid=(DEC_BATCH, N_EXPERTS),
            in_specs=[pl.BlockSpec((1, DEC_SEQ, D_MODEL // 2), lambda b, e, idx: (off + b, 0, 0))],
            out_specs=pl.BlockSpec((1, CAP_LAT, D_MODEL // 2), lambda b, e, idx: (e, b, 0)),
        ),
        out_shape=jax.ShapeDtypeStruct((N_EXPERTS, DEC_BATCH * CAP_LAT, D_MODEL // 2), jnp.uint32),
        compiler_params=_params(("arbitrary", "arbitrary")),
        name="gather_lat",
    )(idx_flat, h2p3)


N_CTX_FFN_TILES = BATCH * CAP_CTX // FFN_ROW_TILE


def _ffn_kernel(xc_ref, xl_ref, vc_ref, vl_ref, wg_ref, wu_ref, wd_ref, o_ref, wg_b, wu_b, wd_b):
    j = pl.program_id(1)

    @pl.when(j == 0)
    def _():
        wg_b[...] = wg_ref[0].astype(BF16)
        wu_b[...] = wu_ref[0].astype(BF16)
        wd_b[...] = wd_ref[0].astype(BF16)

    is_ctx = j < N_CTX_FFN_TILES
    x = jnp.where(is_ctx, jnp.concatenate(_unpack_halves(xc_ref[0]), axis=1),
                  jnp.concatenate(_unpack_halves(xl_ref[0]), axis=1))
    g = jnp.dot(x, wg_b[...], preferred_element_type=F32)
    u = jnp.dot(x, wu_b[...], preferred_element_type=F32)
    hh = (g * _sigmoid(g)) * u
    y = jnp.dot(hh.astype(BF16), wd_b[...], preferred_element_type=F32)
    o_ref[0] = y * jnp.where(is_ctx, vc_ref[0], vl_ref[0])


def _expert_ffn(xg_ctx, xg_lat, val_ctx, val_lat, w_gate, w_up, w_down, layer):
    tr = FFN_ROW_TILE
    n_tiles = ROWS_PER_EXPERT // tr
    def wspec(k, n, tiles_held):
        ahead = lambda e, j: jnp.minimum(e + jnp.where(j >= tiles_held, 1, 0), N_EXPERTS - 1)
        return pl.BlockSpec((None, 1, k, n), lambda e, j: (layer, ahead(e, j), 0, 0))

    ctx_tile = lambda j: jnp.minimum(j, N_CTX_FFN_TILES - 1)
    lat_tile = lambda j: jnp.maximum(j - N_CTX_FFN_TILES, 0)
    return pl.pallas_call(
        _ffn_kernel,
        grid=(N_EXPERTS, n_tiles),
        in_specs=[pl.BlockSpec((1, tr, D_MODEL // 2), lambda e, j: (e, ctx_tile(j), 0)),
                  pl.BlockSpec((1, tr, D_MODEL // 2), lambda e, j: (e, lat_tile(j), 0)),
                  pl.BlockSpec((1, tr, 1), lambda e, j: (e, ctx_tile(j), 0)),
                  pl.BlockSpec((1, tr, 1), lambda e, j: (e, lat_tile(j), 0)),
                  wspec(D_MODEL, EXPERT_FF, 1), wspec(D_MODEL, EXPERT_FF, n_tiles - 1),
                  wspec(EXPERT_FF, D_MODEL, n_tiles)],
        out_specs=pl.BlockSpec((1, tr, D_MODEL), lambda e, j: (e, j, 0)),
        out_shape=jax.ShapeDtypeStruct((N_EXPERTS, ROWS_PER_EXPERT, D_MODEL), F32),
        scratch_shapes=[pltpu.VMEM((D_MODEL, EXPERT_FF), BF16), pltpu.VMEM((D_MODEL, EXPERT_FF), BF16),
                        pltpu.VMEM((EXPERT_FF, D_MODEL), BF16)],
        compiler_params=_params(("arbitrary", "arbitrary")),
        name="expert_ffn",
    )(xg_ctx, xg_lat, val_ctx, val_lat, w_gate, w_up, w_down)


def _scatter_ctx_kernel(idx_ref, y_ref, x1_ref, g2_ref, out_ref):
    onehot = _ctx_slot_onehot(idx_ref, False)
    y_hi, y_lo = _split_bf16(y_ref[...].reshape(CTX_SLOTS, D_MODEL))
    moe = jnp.dot(onehot, y_hi, preferred_element_type=F32) + jnp.dot(onehot, y_lo, preferred_element_type=F32)
    out_ref[...] = x1_ref[...] + g2_ref[...] * moe


def _scatter_ctx(idx_c, yg, x1, g2):
    return pl.pallas_call(
        _scatter_ctx_kernel,
        grid=(BATCH,),
        in_specs=[pl.BlockSpec((N_EXPERTS, CAP_CTX, 1), lambda b: (b, 0, 0)),
                  pl.BlockSpec((N_EXPERTS, CAP_CTX, D_MODEL), lambda b: (0, b, 0)),
                  pl.BlockSpec((SEQ, D_MODEL), lambda b: (b, 0)),
                  pl.BlockSpec((None, 1, D_MODEL), lambda b: (0, 0, 0))],
        out_specs=pl.BlockSpec((SEQ, D_MODEL), lambda b: (b, 0)),
        out_shape=jax.ShapeDtypeStruct((T_CTX, D_MODEL), F32),
        compiler_params=_params(("arbitrary",)),
        name="scatter_ctx",
    )(idx_c, yg, x1, g2)


def _scatter_lat_kernel(idx_ref, y_ref, x1_ref, g2_ref, out_ref):
    e = pl.program_id(2)

    @pl.when(e == 0)
    def _():
        out_ref[...] = jnp.zeros_like(out_ref)

    base = (pl.program_id(0) * N_EXPERTS + e) * CAP_LAT

    def body(it, _):
        r0 = pl.multiple_of(it * SLOT_GROUP, SLOT_GROUP)
        rows = [idx_ref[base + r0 + k] for k in range(SLOT_GROUP)]
        old = [out_ref[0, pl.ds(rows[k], 1), :] for k in range(SLOT_GROUP)]
        y = y_ref[0, pl.ds(r0, SLOT_GROUP), :]
        for k in range(SLOT_GROUP):
            out_ref[0, pl.ds(rows[k], 1), :] = old[k] + y[k:k + 1, :]
        return 0

    lax.fori_loop(0, CAP_LAT // SLOT_GROUP, body, 0)

    @pl.when(e == N_EXPERTS - 1)
    def _():
        out_ref[0] = x1_ref[0] + g2_ref[...] * out_ref[0]


def _scatter_lat(idx_flat, yg, x1_3, g2, off, n_split):
    w = D_MODEL // n_split
    blk0 = BATCH * CAP_CTX // CAP_LAT
    return pl.pallas_call(
        _scatter_lat_kernel,
        grid_spec=pltpu.PrefetchScalarGridSpec(
            num_scalar_prefetch=1,
            grid=(DEC_BATCH, n_split, N_EXPERTS),
            in_specs=[pl.BlockSpec((1, CAP_LAT, w), lambda b, h, e, idx: (e, blk0 + b, h)),
                      pl.BlockSpec((1, DEC_SEQ, w), lambda b, h, e, idx: (off + b, 0, h),
                                   pipeline_mode=pl.Buffered(1)),
                      pl.BlockSpec((None, 1, w), lambda b, h, e, idx: (off + b, 0, h))],
            out_specs=pl.BlockSpec((1, DEC_SEQ, w), lambda b, h, e, idx: (b, 0, h), pipeline_mode=pl.Buffered(1)),
        ),
        out_shape=jax.ShapeDtypeStruct((DEC_BATCH, DEC_SEQ, D_MODEL), F32),
        compiler_params=_params(("arbitrary", "arbitrary", "arbitrary")),
        name="scatter_lat",
    )(idx_flat, yg, x1_3, g2)


def _rope_tables():
    pos = np.arange(DEC_SEQ)
    freq = (np.float32(ROPE_THETA) ** (-np.arange(ROPE_FREQS, dtype=np.float32) / np.float32(ROPE_FREQS)))
    ang_r = (pos // GRID_W).astype(np.float32)[:, None] * freq.astype(np.float32)
    ang_c = (pos % GRID_W).astype(np.float32)[:, None] * freq.astype(np.float32)
    cos = np.concatenate([np.cos(ang_r)] * 2 + [np.cos(ang_c)] * 2, axis=-1)
    sin = np.concatenate([-np.sin(ang_r), np.sin(ang_r), -np.sin(ang_c), np.sin(ang_c)], axis=-1)
    reps = LANE // HEAD_DIM
    cs = np.concatenate([np.ones((ROW_TILE, LANE)), np.tile(cos, (1, reps))], axis=0).astype(np.float32)
    sn = np.concatenate([np.zeros((ROW_TILE, LANE)), np.tile(sin, (1, reps))], axis=0).astype(np.float32)
    return cs, sn


def _rope_tile(i):
    lat = jnp.maximum(i - N_CTX_TILES, 0) % (DEC_SEQ // ROW_TILE)
    return jnp.where(i < N_CTX_TILES, 0, 1 + lat)


def _block_diag_ones(width, group):
    g = jnp.arange(width) // group
    return (g[:, None] == g[None, :]).astype(BF16)


def _qk_gain(q_norm, k_norm):
    q = jnp.tile(q_norm, N_HEADS) * (HEAD_DIM ** -0.5 * LOG2_E)
    return jnp.concatenate([q, jnp.tile(k_norm, N_KV)])[None, :]


def kernel(x_prompt, x_sample, cache_a_k, cache_a_v, cache_c_k, cache_c_v, c, c_ctx, norm1_g, w_mod, b_mod, w_in,
           a_q_norm, a_k_norm, a_sink, b_v_norm, b_ws, b_bs, c_q_norm, c_k_norm, w_a_o, w_b_o, w_c_o, w_out, norm2_g,
           w_router, b_router, w_gate, w_up, w_down):
    cond8 = jnp.concatenate([c_ctx[None, :], c, jnp.zeros((8 - N_REQ, D_MODEL), F32)], axis=0)
    mods = _modulation(cond8, w_mod, b_mod)

    cs, sn = _rope_tables()
    bd_qk = _block_diag_ones(QK_W, HEAD_DIM)
    bd_b = _block_diag_ones(B_WIDTH, B_GROUP_CH)
    w_in_b = w_in.astype(BF16)
    wa_b, wb_b, wc_b, wo_b = w_a_o.astype(BF16), w_b_o.astype(BF16), w_c_o.astype(BF16), w_out.astype(BF16)
    ws_b = b_ws.astype(BF16)
    wr_pad = jnp.pad(w_router, ((0, 0), (0, 0), (0, LANE - N_EXPERTS))).astype(BF16)
    br_pad = jnp.pad(b_router, ((0, 0), (0, LANE - N_EXPERTS)), constant_values=NEG_BIG)

    by_seq = lambda a: a.reshape(T_ALL // SEQ, SEQ, a.shape[-1])
    by_dec = lambda a: a.reshape(T_ALL // DEC_SEQ, DEC_SEQ, a.shape[-1])
    lat_off = T_CTX // DEC_SEQ

    x_ctx = x_prompt.reshape(T_CTX, D_MODEL)
    x_lat = x_sample.reshape(T_LAT, D_MODEL)
    new_kv = [[], [], [], []]
    for l in range(DEPTH):
        sh1, sc1, g1, sh2, sc2, g2 = [mods[l, :N_REQ, i * D_MODEL:(i + 1) * D_MODEL].reshape(N_REQ, 1, D_MODEL)
                                      for i in range(6)]
        qa, ka, va, bu, bv, qc, kc, vc, gt = _input_projection(
            x_ctx, x_lat, sc1, sh1, norm1_g[l][None, :], w_in_b[l], cs, sn,
            _qk_gain(a_q_norm[l], a_k_norm[l]), _qk_gain(c_q_norm[l], c_k_norm[l]), b_v_norm[l][None, :], bd_qk, bd_b)
        for lst, arr in zip(new_kv, (ka, va, kc, vc)):
            lst.append(arr[:T_CTX].reshape(BATCH, SEQ, N_KV, HEAD_DIM))

        ka_b, va_b, kc_b, vc_b = (a.astype(BF16) for a in (ka, va, kc, vc))
        sink = a_sink[l]
        oa_ctx = _dense_attention(by_seq(qa), by_seq(ka_b), by_seq(va_b), None, sink,
                                  n_req=BATCH, off=0, tq=SEQ, key_chunk=SEQ)
        oc_ctx = _dense_attention(by_seq(qc), by_seq(kc_b), by_seq(vc_b), None, None,
                                  n_req=BATCH, off=0, tq=SEQ, key_chunk=SEQ)
        flat_cache = lambda a: a[:, l].reshape(DEC_BATCH, PAST_LEN, KV_W).astype(BF16)
        oa_lat = _window_attention(by_dec(qa), by_dec(ka_b), by_dec(va_b), flat_cache(cache_a_k),
                                   flat_cache(cache_a_v), sink, n_req=DEC_BATCH, off=lat_off)
        oc_lat = _dense_attention(by_dec(qc), by_dec(kc_b), by_dec(vc_b), (flat_cache(cache_c_k), flat_cache(cache_c_v)),
                                  None, n_req=DEC_BATCH, off=lat_off, tq=1024, key_chunk=512)

        bs_full = jnp.repeat(b_bs[l].T, B_GROUP_CH, axis=1)
        x1, h2p, afft = _merge(x_ctx, x_lat, oa_ctx.reshape(T_CTX, Q_W), oa_lat.reshape(T_LAT, Q_W), bu, bv,
                               oc_ctx.reshape(T_CTX, Q_W), oc_lat.reshape(T_LAT, Q_W), gt,
                               wa_b[l], wb_b[l], wc_b[l], wo_b[l], ws_b[l], bs_full,
                               g1, sc2, sh2, norm2_g[l][None, :], wr_pad[l], br_pad[l][None, :])

        aff_rows = lambda a, n_req, n: a.reshape(N_EXPERTS, n_req, n).transpose(1, 0, 2).reshape(n_req * N_EXPERTS, n)
        idx_c, val_c = _select(aff_rows(afft[:, :T_CTX], BATCH, SEQ), BATCH * N_EXPERTS, CAP_CTX)
        idx_l, val_l = _select(aff_rows(afft[:, T_CTX:], DEC_BATCH, DEC_SEQ), N_EXPERTS, CAP_LAT)
        idx_l_flat = idx_l.reshape(-1)
        xg_ctx = _gather_ctx(idx_c, h2p)
        xg_lat = _gather_lat(idx_l_flat, by_dec(h2p), lat_off)
        per_expert = lambda v, n_req, cap: v.reshape(n_req, N_EXPERTS, cap).transpose(1, 0, 2).reshape(
            N_EXPERTS, n_req * cap, 1)
        yg = _expert_ffn(xg_ctx, xg_lat, per_expert(val_c, BATCH, CAP_CTX), per_expert(val_l, DEC_BATCH, CAP_LAT),
                         w_gate, w_up, w_down, l)

        x_ctx = _scatter_ctx(idx_c, yg, x1, g2)
        x_lat = _scatter_lat(idx_l_flat, yg, by_dec(x1), g2, lat_off, 1).reshape(T_LAT, D_MODEL)

    y_prompt = x_ctx.reshape(BATCH, SEQ, D_MODEL)
    y_sample = x_lat.reshape(DEC_BATCH, DEC_SEQ, D_MODEL)
    return (y_prompt, y_sample) + tuple(jnp.stack(lst, axis=1) for lst in new_kv)
```

```python
import functools

import jax
import numpy as np
import jax.numpy as jnp
from jax import lax
from jax.experimental import pallas as pl
from jax.experimental.pallas import tpu as pltpu

F32 = jnp.float32
BF16 = jnp.bfloat16
I32 = jnp.int32

D_MODEL = 1024
BATCH = 16
SEQ = 256
DEPTH = 2
DEC_BATCH = 2
DEC_SEQ = 4096
PAST_LEN = 256
GRID_W = 64
HEAD_DIM = 64
N_HEADS = 6
N_KV = 2
N_GRP = N_HEADS // N_KV
B_GROUPS = 4
B_GROUP_CH = 64
B_WIDTH = B_GROUPS * B_GROUP_CH
Q_W = N_HEADS * HEAD_DIM
KV_W = N_KV * HEAD_DIM
QK_W = Q_W + KV_W
N_BRANCH = 3
WINDOW = 128
BLOCK = 128
CHUNK = 128
N_EXPERTS = 16
EXPERT_FF = 1024
CAP_FACTOR = 2
ROPE_THETA = 10000.0
ROPE_FREQS = HEAD_DIM // 4
EPS = 1e-6
IN_WIDTH = 2 * (QK_W + KV_W) + 2 * B_WIDTH + N_BRANCH * D_MODEL

T_CTX = BATCH * SEQ
T_LAT = DEC_BATCH * DEC_SEQ
T_ALL = T_CTX + T_LAT
N_REQ = 1 + DEC_BATCH
CAP_CTX = CAP_FACTOR * SEQ // N_EXPERTS
CAP_LAT = CAP_FACTOR * DEC_SEQ // N_EXPERTS
ROWS_PER_EXPERT = BATCH * CAP_CTX + DEC_BATCH * CAP_LAT

LANE = 128
ROW_TILE = 512
N_CTX_TILES = T_CTX // ROW_TILE
FFN_ROW_TILE = 512
VMEM_LIMIT = 56 * 1024 * 1024
NEG_BIG = -1e30
LOG2_E = 1.4426950408889634

OFF_A = 0
OFF_AV = OFF_A + QK_W
OFF_BU = OFF_AV + KV_W
OFF_BV = OFF_BU + B_WIDTH
OFF_C = OFF_BV + B_WIDTH
OFF_CV = OFF_C + QK_W
OFF_G = OFF_CV + KV_W


def _params(sem, vmem=VMEM_LIMIT):
    return pltpu.CompilerParams(dimension_semantics=sem, vmem_limit_bytes=vmem)


def _sigmoid(x):
    return 1.0 / (1.0 + jnp.exp(-x))


def _gelu_tanh(x):
    return 0.5 * x * (1.0 + jnp.tanh(0.7978845608028654 * (x + 0.044715 * (x * x * x))))


def _split_bf16(x):
    hi = x.astype(BF16)
    lo = (x - hi.astype(F32)).astype(BF16)
    return hi, lo


def _mod_kernel(c_ref, w_ref, b_ref, o_ref):
    c = c_ref[...]
    s_hi, s_lo = _split_bf16(c * _sigmoid(c))
    w_hi, w_lo = _split_bf16(w_ref[0])
    acc = jnp.dot(s_hi, w_hi, preferred_element_type=F32)
    acc += jnp.dot(s_lo, w_hi, preferred_element_type=F32)
    acc += jnp.dot(s_hi, w_lo, preferred_element_type=F32)
    o_ref[0] = acc + b_ref[0]


def _modulation(cond8, w_mod, b_mod):
    n_col = 6 * D_MODEL // D_MODEL
    return pl.pallas_call(
        _mod_kernel,
        grid=(DEPTH, n_col),
        in_specs=[
            pl.BlockSpec((8, D_MODEL), lambda l, j: (0, 0)),
            pl.BlockSpec((1, D_MODEL, D_MODEL), lambda l, j: (l, 0, j)),
            pl.BlockSpec((1, 1, D_MODEL), lambda l, j: (l, 0, j)),
        ],
        out_specs=pl.BlockSpec((1, 8, D_MODEL), lambda l, j: (l, 0, j)),
        out_shape=jax.ShapeDtypeStruct((DEPTH, 8, 6 * D_MODEL), F32),
        compiler_params=_params(("arbitrary", "arbitrary")),
        name="modulation",
    )(cond8, w_mod, b_mod.reshape(DEPTH, 1, 6 * D_MODEL))


def _group_sumsq(y):
    y2 = y * y
    first = (lax.broadcasted_iota(I32, (y.shape[0], LANE), 1) & (LANE // 2)) == 0
    out = []
    for j in range(y.shape[1] // LANE):
        blk = y2[:, j * LANE:(j + 1) * LANE]
        lo = jnp.sum(jnp.where(first, blk, 0.0), axis=1, keepdims=True)
        hi = jnp.sum(jnp.where(first, 0.0, blk), axis=1, keepdims=True)
        out.append(jnp.where(first, lo, hi))
    return jnp.concatenate(out, axis=1)


def _pick_pass(i, ctx_ref, lat_ref):
    return jnp.where(i < N_CTX_TILES, ctx_ref[...], lat_ref[...])


def _in_kernel(xc_ref, xl_ref, sc_ref, sh_ref, n1_ref, w_ref, cs_ref, sn_ref, ga_ref, gc_ref, gbv_ref,
               qa_ref, ka_ref, va_ref, nka_ref, nva_ref, bu_ref, bv_ref, qc_ref, kc_ref, vc_ref, nkc_ref, nvc_ref, gt_ref):
    is_ctx = pl.program_id(0) < N_CTX_TILES
    x = _pick_pass(pl.program_id(0), xc_ref, xl_ref)
    ms = jnp.mean(x * x, axis=-1, keepdims=True)
    h = x * lax.rsqrt(ms + EPS) * n1_ref[...]
    h = h * (1.0 + sc_ref[...]) + sh_ref[...]
    hb = h.astype(BF16)
    tm = x.shape[0]

    def proj(c0, width):
        return jnp.dot(hb, w_ref[:, c0:c0 + width], preferred_element_type=F32)

    cs = jnp.concatenate([cs_ref[...]] * (QK_W // LANE), axis=1)
    sn = jnp.concatenate([sn_ref[...]] * (QK_W // LANE), axis=1)
    lane = lax.broadcasted_iota(I32, (tm, QK_W), 1)
    first_half = (lane & ROPE_FREQS) == 0

    def qk_post(y, gain_ref):
        yn = y * lax.rsqrt(_group_sumsq(y) * (1.0 / HEAD_DIM) + EPS) * gain_ref[...]
        partner = jnp.where(first_half, pltpu.roll(yn, QK_W - ROPE_FREQS, 1), pltpu.roll(yn, ROPE_FREQS, 1))
        return yn * cs + partner * sn

    def mixer(off_qk, off_v, gain_ref, q_ref, k_ref, v_ref, nk_ref, nv_ref):
        y = qk_post(proj(off_qk, QK_W), gain_ref)
        v = proj(off_v, KV_W)
        q_ref[...] = y[:, :Q_W].astype(BF16)
        k_ref[...] = y[:, Q_W:].astype(BF16)
        v_ref[...] = v.astype(BF16)

        @pl.when(is_ctx)
        def _():
            nk_ref[...] = y[:, Q_W:]
            nv_ref[...] = v

    mixer(OFF_A, OFF_AV, ga_ref, qa_ref, ka_ref, va_ref, nka_ref, nva_ref)

    bu_ref[...] = _gelu_tanh(proj(OFF_BU, B_WIDTH)).astype(BF16)
    gv = _gelu_tanh(proj(OFF_BV, B_WIDTH))
    gvn = gv * lax.rsqrt(_group_sumsq(gv) * (1.0 / B_GROUP_CH) + EPS) * gbv_ref[...]
    bv_ref[...] = gvn.astype(BF16)

    mixer(OFF_C, OFF_CV, gc_ref, qc_ref, kc_ref, vc_ref, nkc_ref, nvc_ref)

    gate_chunk = 512
    for j in range(N_BRANCH * D_MODEL // gate_chunk):
        g = proj(OFF_G + j * gate_chunk, gate_chunk)
        gt_ref[:, j * gate_chunk:(j + 1) * gate_chunk] = _sigmoid(g).astype(BF16)


def _req_of_tile(i):
    return i // N_CTX_TILES


def _ctx_rows(w):
    return pl.BlockSpec((ROW_TILE, w), lambda i: (jnp.minimum(i, N_CTX_TILES - 1), 0))


def _lat_rows(w):
    return pl.BlockSpec((ROW_TILE, w), lambda i: (jnp.maximum(i - N_CTX_TILES, 0), 0))


MOD_SH1, MOD_SC1, MOD_G1, MOD_SH2, MOD_SC2, MOD_G2 = range(6)


def _mod_spec(layer, chunk, req):
    return pl.BlockSpec((None, None, 1, D_MODEL), lambda *g: (layer, req(*g), 0, chunk))


def _input_projection(x_ctx, x_lat, mods, layer, n1, w_in_b, cs, sn, gain_a, gain_c, gain_bv):
    tm = ROW_TILE
    row = lambda w: pl.BlockSpec((tm, w), lambda i: (i, 0))
    full = lambda a: pl.BlockSpec(a.shape, lambda i: (0,) * a.ndim)
    rope = pl.BlockSpec((tm, LANE), lambda i: (_rope_tile(i), 0))
    mixer_outs = [(Q_W, BF16, T_ALL), (KV_W, BF16, T_ALL), (KV_W, BF16, T_ALL), (KV_W, F32, T_CTX), (KV_W, F32, T_CTX)]
    outs = mixer_outs + [(B_WIDTH, BF16, T_ALL), (B_WIDTH, BF16, T_ALL)] + mixer_outs + [(N_BRANCH * D_MODEL, BF16, T_ALL)]
    return pl.pallas_call(
        _in_kernel,
        grid=(T_ALL // tm,),
        in_specs=[_ctx_rows(D_MODEL), _lat_rows(D_MODEL), _mod_spec(layer, MOD_SC1, _req_of_tile),
                  _mod_spec(layer, MOD_SH1, _req_of_tile), full(n1), full(w_in_b), rope, rope,
                  full(gain_a), full(gain_c), full(gain_bv)],
        out_specs=[row(w) if rows == T_ALL else _ctx_rows(w) for w, _, rows in outs],
        out_shape=[jax.ShapeDtypeStruct((rows, w), dt) for w, dt, rows in outs],
        compiler_params=_params(("arbitrary",)),
        name="input_projection",
    )(x_ctx, x_lat, mods, mods, n1, w_in_b, cs, sn, gain_a, gain_c, gain_bv)


def _attention_tile(q_ref, sources, sink_ref, o_ref, qt_scr, ot_scr, *, tq, key_chunk):
    width = N_GRP * tq
    for j in range(Q_W // LANE):
        qt_scr[j * LANE:(j + 1) * LANE, :] = q_ref[0, :, j * LANE:(j + 1) * LANE].astype(F32).T.astype(BF16)
    for kv in range(N_KV):
        lo, hi = kv * HEAD_DIM, (kv + 1) * HEAD_DIM
        heads = [kv * N_GRP + g for g in range(N_GRP)]
        qt = jnp.concatenate([qt_scr[h * HEAD_DIM:(h + 1) * HEAD_DIM, :] for h in heads], axis=1)

        def step(carry, kref, vref, c0, size, bias):
            m, acc = carry
            s = jnp.dot(kref[0, pl.ds(c0, size), lo:hi], qt, preferred_element_type=F32)
            if bias is not None:
                s = s + jnp.concatenate([bias] * N_GRP, axis=1)
            vt = vref[0, pl.ds(c0, size), :].astype(F32).T[lo:hi, :].astype(BF16)
            vt = jnp.concatenate([vt, jnp.ones((DEN_ROWS, size), BF16)], axis=0)
            m_new = jnp.maximum(m, jnp.max(s, axis=0, keepdims=True))
            p = jnp.exp2(s - m_new).astype(BF16)
            acc = acc * jnp.exp2(m - m_new) + jnp.dot(vt, p, preferred_element_type=F32)
            return m_new, acc

        if sink_ref is not None:
            m0 = jnp.concatenate([jnp.full((1, tq), sink_ref[h] * LOG2_E, F32) for h in heads], axis=1)
            den0 = jnp.ones((DEN_ROWS, width), F32)
        else:
            m0 = jnp.full((1, width), NEG_BIG, F32)
            den0 = jnp.zeros((DEN_ROWS, width), F32)
        carry = (m0, jnp.concatenate([jnp.zeros((HEAD_DIM, width), F32), den0], axis=0))
        for kref, vref, bias in sources:
            n_rows = kref.shape[1]
            n_full = n_rows // key_chunk
            if bias is not None:
                carry = step(carry, kref, vref, 0, n_rows, bias)
                continue
            if n_full > 1:
                carry = lax.fori_loop(
                    0, n_full,
                    lambda c, cr: step(cr, kref, vref, pl.multiple_of(c * key_chunk, key_chunk), key_chunk, None), carry)
            elif n_full == 1:
                carry = step(carry, kref, vref, 0, key_chunk, None)
            if n_rows - n_full * key_chunk:
                carry = step(carry, kref, vref, n_full * key_chunk, n_rows - n_full * key_chunk, None)
        _, acc = carry
        o = acc[:HEAD_DIM] / acc[HEAD_DIM:HEAD_DIM + 1]
        for g, h in enumerate(heads):
            ot_scr[h * HEAD_DIM:(h + 1) * HEAD_DIM, :] = o[:, g * tq:(g + 1) * tq]
    for j in range(Q_W // LANE):
        o_ref[0, :, j * LANE:(j + 1) * LANE] = ot_scr[j * LANE:(j + 1) * LANE, :].T.astype(o_ref.dtype)


def _dense_attn_kernel(*refs, tq, key_chunk, has_extra, has_sink):
    refs = list(refs)
    q_ref, k_ref, v_ref = refs[:3]
    del refs[:3]
    sources = [(k_ref, v_ref, None)]
    if has_extra:
        sources.append((refs.pop(0), refs.pop(0), None))
    sink_ref = refs.pop(0) if has_sink else None
    o_ref, qt_scr, ot_scr = refs
    _attention_tile(q_ref, sources, sink_ref, o_ref, qt_scr, ot_scr, tq=tq, key_chunk=key_chunk)


def _attention_scratch(tq):
    return [pltpu.VMEM((Q_W, tq), BF16), pltpu.VMEM((Q_W, tq), F32)]


def _dense_attention(q, k, v, extra, sink, *, n_req, off, tq, key_chunk):
    s = q.shape[1]
    kv_spec = pl.BlockSpec((1, s, KV_W), lambda i, j: (off + i, 0, 0))
    in_specs = [pl.BlockSpec((1, tq, Q_W), lambda i, j: (off + i, j, 0)), kv_spec, kv_spec]
    args = [q, k, v]
    if extra is not None:
        in_specs += [pl.BlockSpec((1, extra[0].shape[1], KV_W), lambda i, j: (i, 0, 0))] * 2
        args += list(extra)
    if sink is not None:
        in_specs.append(pl.BlockSpec(memory_space=pltpu.SMEM))
        args.append(sink)
    return pl.pallas_call(
        functools.partial(_dense_attn_kernel, tq=tq, key_chunk=key_chunk, has_extra=extra is not None,
                          has_sink=sink is not None),
        grid=(n_req, s // tq),
        in_specs=in_specs,
        out_specs=pl.BlockSpec((1, tq, Q_W), lambda i, j: (i, j, 0)),
        out_shape=jax.ShapeDtypeStruct((n_req, s, Q_W), BF16),
        scratch_shapes=_attention_scratch(tq),
        compiler_params=_params(("arbitrary", "arbitrary")),
        name="dense_attention",
    )(*args)


DEN_ROWS = 16
WINDOW_TQ = 512


def _window_attn_kernel(q_ref, kp_ref, kc_ref, kn_ref, vp_ref, vc_ref, vn_ref, ck_ref, cv_ref, bp_ref, bc_ref, bn_ref,
                        sink_ref, o_ref, qt_scr, ot_scr, *, seq):
    q_pos0 = pl.program_id(1) * WINDOW_TQ
    prev_bias = bp_ref[...] + jnp.where(q_pos0 >= BLOCK, 0.0, NEG_BIG)
    next_bias = bn_ref[...] + jnp.where(q_pos0 + WINDOW_TQ < seq, 0.0, NEG_BIG)
    sources = [(kp_ref, vp_ref, prev_bias), (kc_ref, vc_ref, bc_ref[...]), (kn_ref, vn_ref, next_bias),
               (ck_ref, cv_ref, None)]
    _attention_tile(q_ref, sources, sink_ref, o_ref, qt_scr, ot_scr, tq=WINDOW_TQ, key_chunk=WINDOW_TQ)


def _band_bias(first_key, n_keys):
    d = (first_key + np.arange(n_keys))[:, None] - np.arange(WINDOW_TQ)[None, :]
    return np.where(np.abs(d) <= WINDOW, 0.0, NEG_BIG).astype(np.float32)


def _window_attention(q, k, v, ck, cv, sink, *, n_req, off):
    b, s = n_req, q.shape[1]
    nb = s // BLOCK
    per_tile = WINDOW_TQ // BLOCK
    edge = lambda f: pl.BlockSpec((1, BLOCK, KV_W), lambda i, j: (off + i, f(j), 0))
    prev = lambda j: jnp.maximum(j * per_tile - 1, 0)
    nxt = lambda j: jnp.minimum((j + 1) * per_tile, nb - 1)
    cur = pl.BlockSpec((1, WINDOW_TQ, KV_W), lambda i, j: (off + i, j, 0))
    ctx = pl.BlockSpec((1, PAST_LEN, KV_W), lambda i, j: (i, 0, 0))
    biases = [_band_bias(-BLOCK, BLOCK), _band_bias(0, WINDOW_TQ), _band_bias(WINDOW_TQ, BLOCK)]
    table = lambda a: pl.BlockSpec(a.shape, lambda i, j: (0, 0))
    return pl.pallas_call(
        functools.partial(_window_attn_kernel, seq=s),
        grid=(b, s // WINDOW_TQ),
        in_specs=[pl.BlockSpec((1, WINDOW_TQ, Q_W), lambda i, j: (off + i, j, 0)),
                  edge(prev), cur, edge(nxt), edge(prev), cur, edge(nxt), ctx, ctx,
                  table(biases[0]), table(biases[1]), table(biases[2]),
                  pl.BlockSpec(memory_space=pltpu.SMEM)],
        out_specs=pl.BlockSpec((1, WINDOW_TQ, Q_W), lambda i, j: (i, j, 0)),
        out_shape=jax.ShapeDtypeStruct((b, s, Q_W), BF16),
        scratch_shapes=_attention_scratch(WINDOW_TQ),
        compiler_params=_params(("arbitrary", "arbitrary")),
        name="window_attention",
    )(q, k, k, k, v, v, v, ck, cv, *biases, sink)


def _pack_halves(x):
    half = x.shape[1] // 2
    return pltpu.pack_elementwise([x[:, :half], x[:, half:]], packed_dtype=BF16)


def _unpack_halves(words):
    return tuple(pltpu.unpack_elementwise(words, index=i, packed_dtype=BF16, unpacked_dtype=F32).astype(BF16)
                 for i in range(2))


def _merge_kernel(xc_ref, xl_ref, oac_ref, oal_ref, bu_ref, bv_ref, occ_ref, ocl_ref, gt_ref, wa_ref, wb_ref, wc_ref,
                  wo_ref, ws_ref, bs_ref, g1_ref, sc2_ref, sh2_ref, n2_ref, wr_ref, br_ref, x1_ref, h2p_ref, afft_ref):
    i = pl.program_id(0)
    tm = xc_ref.shape[0]
    group = lax.broadcasted_iota(I32, (CHUNK, B_WIDTH), 1) // B_GROUP_CH
    obs = []
    for c in range(tm // CHUNK):
        v = bv_ref[c * CHUNK:(c + 1) * CHUNK, :]
        sv = jnp.zeros((CHUNK, B_WIDTH), F32)
        for g in range(B_GROUPS):
            sv = jnp.where(group == g, jnp.dot(ws_ref[g], v, preferred_element_type=F32), sv)
        u = bu_ref[c * CHUNK:(c + 1) * CHUNK, :].astype(F32)
        obs.append((u * (sv + bs_ref[...])).astype(BF16))
    ob = jnp.concatenate(obs, axis=0)

    oa = _pick_pass(i, oac_ref, oal_ref)
    oc = _pick_pass(i, occ_ref, ocl_ref)
    merged = gt_ref[:, 0:D_MODEL].astype(F32) * jnp.dot(oa, wa_ref[...], preferred_element_type=F32)
    merged += gt_ref[:, D_MODEL:2 * D_MODEL].astype(F32) * jnp.dot(ob, wb_ref[...], preferred_element_type=F32)
    merged += gt_ref[:, 2 * D_MODEL:3 * D_MODEL].astype(F32) * jnp.dot(oc, wc_ref[...], preferred_element_type=F32)
    y = jnp.dot(merged.astype(BF16), wo_ref[...], preferred_element_type=F32)
    x1 = _pick_pass(i, xc_ref, xl_ref) + g1_ref[...] * y
    x1_ref[...] = x1

    ms = jnp.mean(x1 * x1, axis=-1, keepdims=True)
    h2 = x1 * lax.rsqrt(ms + EPS) * n2_ref[...]
    h2 = h2 * (1.0 + sc2_ref[...]) + sh2_ref[...]
    h2p_ref[...] = _pack_halves(h2)

    logits = jnp.dot(h2.astype(BF16), wr_ref[...], preferred_element_type=F32) + br_ref[...]
    e = jnp.exp(logits - jnp.max(logits, axis=-1, keepdims=True))
    aff = e / jnp.sum(e, axis=-1, keepdims=True)
    afft_ref[...] = aff.T[:N_EXPERTS, :]


def _merge(x_ctx, x_lat, oa_ctx, oa_lat, bu, bv, oc_ctx, oc_lat, gt, wa, wb, wc, wo, ws, bs, mods, layer, n2, wr, br):
    tm = ROW_TILE
    row = lambda w: pl.BlockSpec((tm, w), lambda i: (i, 0))
    full = lambda a: pl.BlockSpec(a.shape, lambda i: (0,) * a.ndim)
    mod = lambda chunk: _mod_spec(layer, chunk, _req_of_tile)
    return pl.pallas_call(
        _merge_kernel,
        grid=(T_ALL // tm,),
        in_specs=[_ctx_rows(D_MODEL), _lat_rows(D_MODEL), _ctx_rows(Q_W), _lat_rows(Q_W), row(B_WIDTH), row(B_WIDTH),
                  _ctx_rows(Q_W), _lat_rows(Q_W), row(N_BRANCH * D_MODEL),
                  full(wa), full(wb), full(wc), full(wo), full(ws), full(bs),
                  mod(MOD_G1), mod(MOD_SC2), mod(MOD_SH2), full(n2), full(wr), full(br)],
        out_specs=[row(D_MODEL), row(D_MODEL // 2), pl.BlockSpec((N_EXPERTS, tm), lambda i: (0, i))],
        out_shape=[jax.ShapeDtypeStruct((T_ALL, D_MODEL), F32), jax.ShapeDtypeStruct((T_ALL, D_MODEL // 2), jnp.uint32),
                   jax.ShapeDtypeStruct((N_EXPERTS, T_ALL), F32)],
        compiler_params=_params(("arbitrary",)),
        name="merge_router",
    )(x_ctx, x_lat, oa_ctx, oa_lat, bu, bv, oc_ctx, oc_lat, gt, wa, wb, wc, wo, ws, bs, mods, mods, mods, n2, wr, br)


def _select_kernel(aff_ref, idx_ref, val_ref, *rest, n, cap, row_chunk):
    idx_row_ref = rest[0] if len(rest) == 4 else None
    possel_ref, idx_scr, val_scr = rest[-3:]
    a = aff_ref[...]
    rows = a.shape[0]
    tok = lax.broadcasted_iota(I32, (rows, n), 1)

    def count(ones):
        return jnp.sum(ones, axis=1, keepdims=True)

    def at_least(word):
        return jnp.where(a >= pltpu.bitcast(word, F32), 1, 0)

    thr = jnp.zeros((rows, 1), I32)
    for bit in range(30, -1, -1):
        cand = thr | (1 << bit)
        thr = jnp.where(count(at_least(cand)) >= cap, cand, thr)
    above = at_least(thr + 1)
    tied = at_least(thr) - above
    need = cap - count(above)
    last = jnp.zeros((rows, 1), I32)
    for bit in range(n.bit_length() - 2, -1, -1):
        cand = last | (1 << bit)
        last = jnp.where(count(jnp.where(tok < cand, tied, 0)) < need, cand, last)
    sel = above + jnp.where(tok <= last, tied, 0)

    blk = min(n, 256)
    tri = jnp.where(lax.broadcasted_iota(I32, (blk, blk), 0) <= lax.broadcasted_iota(I32, (blk, blk), 1),
                    1.0, 0.0).astype(BF16)
    sel_f = sel.astype(F32)
    offset = jnp.zeros((rows, 1), F32)
    for j in range(n // blk):
        s_blk = sel_f[:, j * blk:(j + 1) * blk]
        incl = jnp.dot(s_blk.astype(BF16), tri, preferred_element_type=F32)
        pos = (incl - s_blk + offset).astype(I32)
        possel_ref[:, j * blk:(j + 1) * blk] = jnp.where(sel[:, j * blk:(j + 1) * blk] > 0, pos, -1)
        offset = offset + incl[:, blk - 1:blk]

    tb = min(n, TOKEN_BLOCK)
    n_blk = n // tb

    def fold_lanes(x):
        acc = x[:, :LANE]
        for k in range(1, tb // LANE):
            acc = acc + x[:, k * LANE:(k + 1) * LANE]
        return acc

    def match(e, slot, t0):
        hit = possel_ref[pl.ds(e, 1), pl.ds(t0, tb)] == slot
        tok = t0 + lax.broadcasted_iota(I32, (1, tb), 1)
        return (fold_lanes(jnp.where(hit, tok, 0)),
                fold_lanes(jnp.where(hit, aff_ref[pl.ds(e, 1), pl.ds(t0, tb)], 0.0)))

    def per_row(e, _):
        ends, run = [], 0
        for j in range(n_blk - 1):
            run = run + jnp.sum(jnp.where(possel_ref[pl.ds(e, 1), j * tb:(j + 1) * tb] >= 0, 1, 0))
            ends.append(run)

        def per_chunk(c, _):
            r0 = pl.multiple_of(c * row_chunk, row_chunk)
            slot = lax.broadcasted_iota(I32, (row_chunk, 1), 0) + r0
            if n_blk == 1:
                idx, val = match(e, slot, 0)
            else:
                first = sum(jnp.where(end <= r0, 1, 0) for end in ends)
                last = 1 + sum(jnp.where(end < r0 + row_chunk, 1, 0) for end in ends)

                def per_block(j, acc):
                    i, v = match(e, slot, pl.multiple_of(j * tb, tb))
                    return acc[0] + i, acc[1] + v

                idx, val = lax.fori_loop(first, last, per_block,
                                         (jnp.zeros((row_chunk, LANE), I32), jnp.zeros((row_chunk, LANE), F32)))
            idx_scr[pl.ds(r0, row_chunk), :] = idx
            val_scr[pl.ds(r0, row_chunk), :] = val
            return 0

        lax.fori_loop(0, cap // row_chunk, per_chunk, 0)
        idx = jnp.sum(idx_scr[...], axis=1, keepdims=True)
        idx_ref[e] = idx
        val_ref[e] = jnp.sum(val_scr[...], axis=1, keepdims=True)
        if idx_row_ref is not None:
            idx_row_ref[pl.ds(e, 1), :] = jnp.broadcast_to(idx.astype(F32), (cap, LANE)).T[0:1, :].astype(I32)
        return 0

    def per_small_row(e, _):
        idx, val = match(e, lax.broadcasted_iota(I32, (cap, 1), 0), 0)
        idx_ref[e] = jnp.sum(idx, axis=1, keepdims=True)
        val_ref[e] = jnp.sum(val, axis=1, keepdims=True)
        return 0

    if n_blk == 1 and cap == row_chunk:
        lax.fori_loop(0, rows, per_small_row, 0, unroll=4)
    else:
        lax.fori_loop(0, rows, per_row, 0)


def _select(aff_rows, rows_per_step, cap):
    r, n = aff_rows.shape
    row_chunk = min(cap, 64)
    out_specs = [pl.BlockSpec((rows_per_step, cap, 1), lambda s: (s, 0, 0))] * 2
    out_shape = [jax.ShapeDtypeStruct((r, cap, 1), I32), jax.ShapeDtypeStruct((r, cap, 1), F32)]
    if cap % LANE == 0:
        out_specs.append(pl.BlockSpec((rows_per_step, cap), lambda s: (s, 0)))
        out_shape.append(jax.ShapeDtypeStruct((r, cap), I32))
    return pl.pallas_call(
        functools.partial(_select_kernel, n=n, cap=cap, row_chunk=row_chunk),
        grid=(r // rows_per_step,),
        in_specs=[pl.BlockSpec((rows_per_step, n), lambda s: (s, 0))],
        out_specs=out_specs,
        out_shape=out_shape,
        scratch_shapes=[pltpu.VMEM((rows_per_step, n), I32), pltpu.VMEM((cap, LANE), I32), pltpu.VMEM((cap, LANE), F32)],
        compiler_params=_params(("arbitrary",)),
        name="expert_select",
    )(aff_rows)


CTX_SLOTS = N_EXPERTS * CAP_CTX
TOKEN_BLOCK = 512
SLOT_GROUP = 8


def _ctx_slot_onehot(idx_ref, slots_on_rows):
    idx = idx_ref[...].reshape(CTX_SLOTS, 1)
    if slots_on_rows:
        hit = idx == lax.broadcasted_iota(I32, (CTX_SLOTS, SEQ), 1)
    else:
        idx_lane = jnp.broadcast_to(idx.astype(F32), (CTX_SLOTS, LANE)).T[0:1, :]
        hit = idx_lane == lax.broadcasted_iota(I32, (SEQ, CTX_SLOTS), 0).astype(F32)
    return jnp.where(hit, 1.0, 0.0).astype(BF16)


def _gather_ctx_kernel(idx_ref, h_ref, out_ref):
    onehot = _ctx_slot_onehot(idx_ref, True)
    lo, hi = _unpack_halves(h_ref[...])
    g_lo = jnp.dot(onehot, lo, preferred_element_type=F32)
    g_hi = jnp.dot(onehot, hi, preferred_element_type=F32)
    packed = pltpu.pack_elementwise([g_lo, g_hi], packed_dtype=BF16)
    out_ref[...] = packed.reshape(N_EXPERTS, CAP_CTX, D_MODEL // 2)


def _gather_ctx(idx_c, h2p):
    return pl.pallas_call(
        _gather_ctx_kernel,
        grid=(BATCH,),
        in_specs=[pl.BlockSpec((N_EXPERTS, CAP_CTX, 1), lambda b: (b, 0, 0)),
                  pl.BlockSpec((SEQ, D_MODEL // 2), lambda b: (b, 0))],
        out_specs=pl.BlockSpec((N_EXPERTS, CAP_CTX, D_MODEL // 2), lambda b: (0, b, 0)),
        out_shape=jax.ShapeDtypeStruct((N_EXPERTS, BATCH * CAP_CTX, D_MODEL // 2), jnp.uint32),
        compiler_params=_params(("arbitrary",)),
        name="gather_ctx",
    )(idx_c, h2p)


def _gather_lat_kernel(idx_ref, src_ref, out_ref):
    base = (pl.program_id(0) * N_EXPERTS + pl.program_id(1)) * CAP_LAT

    def body(it, _):
        r0 = pl.multiple_of(it * SLOT_GROUP, SLOT_GROUP)
        picked = [src_ref[0, pl.ds(idx_ref[base + r0 + k], 1), :] for k in range(SLOT_GROUP)]
        for k in range(SLOT_GROUP):
            out_ref[0, pl.ds(r0 + k, 1), :] = picked[k]
        return 0

    lax.fori_loop(0, CAP_LAT // SLOT_GROUP, body, 0)


def _gather_lat(idx_flat, h2p3, off):
    return pl.pallas_call(
        _gather_lat_kernel,
        grid_spec=pltpu.PrefetchScalarGridSpec(
            num_scalar_prefetch=1,
            grid=(DEC_BATCH, N_EXPERTS),
            in_specs=[pl.BlockSpec((1, DEC_SEQ, D_MODEL // 2), lambda b, e, idx: (off + b, 0, 0))],
            out_specs=pl.BlockSpec((1, CAP_LAT, D_MODEL // 2), lambda b, e, idx: (e, b, 0)),
        ),
        out_shape=jax.ShapeDtypeStruct((N_EXPERTS, DEC_BATCH * CAP_LAT, D_MODEL // 2), jnp.uint32),
        compiler_params=_params(("arbitrary", "arbitrary")),
        name="gather_lat",
    )(idx_flat, h2p3)


N_CTX_FFN_TILES = BATCH * CAP_CTX // FFN_ROW_TILE


def _ffn_kernel(xc_ref, xl_ref, vc_ref, vl_ref, wg_ref, wu_ref, wd_ref, o_ref, wg_b, wu_b, wd_b):
    j = pl.program_id(1)

    @pl.when(j == 0)
    def _():
        wg_b[...] = wg_ref[0].astype(BF16)
        wu_b[...] = wu_ref[0].astype(BF16)
        wd_b[...] = wd_ref[0].astype(BF16)

    is_ctx = j < N_CTX_FFN_TILES
    x = jnp.where(is_ctx, jnp.concatenate(_unpack_halves(xc_ref[0]), axis=1),
                  jnp.concatenate(_unpack_halves(xl_ref[0]), axis=1))
    g = jnp.dot(x, wg_b[...], preferred_element_type=F32)
    u = jnp.dot(x, wu_b[...], preferred_element_type=F32)
    hh = (g * _sigmoid(g)) * u
    y = jnp.dot(hh.astype(BF16), wd_b[...], preferred_element_type=F32)
    o_ref[0] = y * jnp.where(is_ctx, vc_ref[...].reshape(FFN_ROW_TILE, 1), vl_ref[...])


def _expert_ffn(xg_ctx, xg_lat, val_ctx, val_lat, w_gate, w_up, w_down, layer):
    tr = FFN_ROW_TILE
    assert tr == BATCH * CAP_CTX == CAP_LAT
    n_tiles = ROWS_PER_EXPERT // tr
    def wspec(k, n, tiles_held):
        ahead = lambda e, j: jnp.minimum(e + jnp.where(j >= tiles_held, 1, 0), N_EXPERTS - 1)
        return pl.BlockSpec((None, 1, k, n), lambda e, j: (layer, ahead(e, j), 0, 0))

    ctx_tile = lambda j: jnp.minimum(j, N_CTX_FFN_TILES - 1)
    lat_tile = lambda j: jnp.maximum(j - N_CTX_FFN_TILES, 0)
    return pl.pallas_call(
        _ffn_kernel,
        grid=(N_EXPERTS, n_tiles),
        in_specs=[pl.BlockSpec((1, tr, D_MODEL // 2), lambda e, j: (e, ctx_tile(j), 0)),
                  pl.BlockSpec((1, tr, D_MODEL // 2), lambda e, j: (e, lat_tile(j), 0)),
                  pl.BlockSpec((BATCH, None, CAP_CTX, 1), lambda e, j: (0, e, 0, 0)),
                  pl.BlockSpec((None, None, CAP_LAT, 1), lambda e, j: (lat_tile(j), e, 0, 0)),
                  wspec(D_MODEL, EXPERT_FF, 1), wspec(D_MODEL, EXPERT_FF, n_tiles - 1),
                  wspec(EXPERT_FF, D_MODEL, n_tiles)],
        out_specs=pl.BlockSpec((1, tr, D_MODEL), lambda e, j: (e, j, 0)),
        out_shape=jax.ShapeDtypeStruct((N_EXPERTS, ROWS_PER_EXPERT, D_MODEL), F32),
        scratch_shapes=[pltpu.VMEM((D_MODEL, EXPERT_FF), BF16), pltpu.VMEM((D_MODEL, EXPERT_FF), BF16),
                        pltpu.VMEM((EXPERT_FF, D_MODEL), BF16)],
        compiler_params=_params(("arbitrary", "arbitrary")),
        name="expert_ffn",
    )(xg_ctx, xg_lat, val_ctx, val_lat, w_gate, w_up, w_down)


def _scatter_ctx_kernel(idx_ref, y_ref, x1_ref, g2_ref, out_ref):
    onehot = _ctx_slot_onehot(idx_ref, False)
    y_hi, y_lo = _split_bf16(y_ref[...].reshape(CTX_SLOTS, D_MODEL))
    moe = jnp.dot(onehot, y_hi, preferred_element_type=F32) + jnp.dot(onehot, y_lo, preferred_element_type=F32)
    out_ref[...] = x1_ref[...] + g2_ref[...] * moe


def _scatter_ctx(idx_c, yg, x1, mods, layer):
    return pl.pallas_call(
        _scatter_ctx_kernel,
        grid=(BATCH,),
        in_specs=[pl.BlockSpec((N_EXPERTS, CAP_CTX, 1), lambda b: (b, 0, 0)),
                  pl.BlockSpec((N_EXPERTS, CAP_CTX, D_MODEL), lambda b: (0, b, 0)),
                  pl.BlockSpec((SEQ, D_MODEL), lambda b: (b, 0)),
                  _mod_spec(layer, MOD_G2, lambda b: 0)],
        out_specs=pl.BlockSpec((SEQ, D_MODEL), lambda b: (b, 0)),
        out_shape=jax.ShapeDtypeStruct((T_CTX, D_MODEL), F32),
        compiler_params=_params(("arbitrary",)),
        name="scatter_ctx",
    )(idx_c, yg, x1, mods)


def _scatter_lat_kernel(idx_ref, y_ref, x1_ref, g2_ref, out_ref):
    e = pl.program_id(2)

    @pl.when(e == 0)
    def _():
        out_ref[...] = jnp.zeros_like(out_ref)

    base = (pl.program_id(0) * N_EXPERTS + e) * CAP_LAT

    def body(it, _):
        r0 = pl.multiple_of(it * SLOT_GROUP, SLOT_GROUP)
        rows = [idx_ref[base + r0 + k] for k in range(SLOT_GROUP)]
        old = [out_ref[0, pl.ds(rows[k], 1), :] for k in range(SLOT_GROUP)]
        y = y_ref[0, pl.ds(r0, SLOT_GROUP), :]
        for k in range(SLOT_GROUP):
            out_ref[0, pl.ds(rows[k], 1), :] = old[k] + y[k:k + 1, :]
        return 0

    lax.fori_loop(0, CAP_LAT // SLOT_GROUP, body, 0)

    @pl.when(e == N_EXPERTS - 1)
    def _():
        out_ref[0] = x1_ref[0] + g2_ref[...] * out_ref[0]


def _scatter_lat(idx_flat, yg, x1_3, mods, layer, off):
    blk0 = BATCH * CAP_CTX // CAP_LAT
    return pl.pallas_call(
        _scatter_lat_kernel,
        grid_spec=pltpu.PrefetchScalarGridSpec(
            num_scalar_prefetch=1,
            grid=(DEC_BATCH, 1, N_EXPERTS),
            in_specs=[pl.BlockSpec((1, CAP_LAT, D_MODEL), lambda b, h, e, idx: (e, blk0 + b, 0)),
                      pl.BlockSpec((1, DEC_SEQ, D_MODEL), lambda b, h, e, idx: (off + b, 0, 0),
                                   pipeline_mode=pl.Buffered(1)),
                      _mod_spec(layer, MOD_G2, lambda b, h, e, idx: off + b)],
            out_specs=pl.BlockSpec((1, DEC_SEQ, D_MODEL), lambda b, h, e, idx: (b, 0, 0),
                                   pipeline_mode=pl.Buffered(1)),
        ),
        out_shape=jax.ShapeDtypeStruct((DEC_BATCH, DEC_SEQ, D_MODEL), F32),
        compiler_params=_params(("arbitrary", "arbitrary", "arbitrary")),
        name="scatter_lat",
    )(idx_flat, yg, x1_3, mods)


def _rope_tables():
    pos = np.arange(DEC_SEQ)
    freq = (np.float32(ROPE_THETA) ** (-np.arange(ROPE_FREQS, dtype=np.float32) / np.float32(ROPE_FREQS)))
    ang_r = (pos // GRID_W).astype(np.float32)[:, None] * freq.astype(np.float32)
    ang_c = (pos % GRID_W).astype(np.float32)[:, None] * freq.astype(np.float32)
    cos = np.concatenate([np.cos(ang_r)] * 2 + [np.cos(ang_c)] * 2, axis=-1)
    sin = np.concatenate([-np.sin(ang_r), np.sin(ang_r), -np.sin(ang_c), np.sin(ang_c)], axis=-1)
    reps = LANE // HEAD_DIM
    cs = np.concatenate([np.ones((ROW_TILE, LANE)), np.tile(cos, (1, reps))], axis=0).astype(np.float32)
    sn = np.concatenate([np.zeros((ROW_TILE, LANE)), np.tile(sin, (1, reps))], axis=0).astype(np.float32)
    return cs, sn


def _rope_tile(i):
    lat = jnp.maximum(i - N_CTX_TILES, 0) % (DEC_SEQ // ROW_TILE)
    return jnp.where(i < N_CTX_TILES, 0, 1 + lat)


def _qk_gain(q_norm, k_norm):
    q = jnp.tile(q_norm, N_HEADS) * (HEAD_DIM ** -0.5 * LOG2_E)
    return jnp.concatenate([q, jnp.tile(k_norm, N_KV)])[None, :]


def kernel(x_prompt, x_sample, cache_a_k, cache_a_v, cache_c_k, cache_c_v, c, c_ctx, norm1_g, w_mod, b_mod, w_in,
           a_q_norm, a_k_norm, a_sink, b_v_norm, b_ws, b_bs, c_q_norm, c_k_norm, w_a_o, w_b_o, w_c_o, w_out, norm2_g,
           w_router, b_router, w_gate, w_up, w_down):
    cond8 = jnp.concatenate([c_ctx[None, :], c, jnp.zeros((8 - N_REQ, D_MODEL), F32)], axis=0)
    mods = _modulation(cond8, w_mod, b_mod).reshape(DEPTH, 8, 1, 6 * D_MODEL)

    cs, sn = _rope_tables()
    w_in_b = w_in.astype(BF16)
    wa_b, wb_b, wc_b, wo_b = w_a_o.astype(BF16), w_b_o.astype(BF16), w_c_o.astype(BF16), w_out.astype(BF16)
    ws_b = b_ws.astype(BF16)
    wr_pad = jnp.pad(w_router, ((0, 0), (0, 0), (0, LANE - N_EXPERTS))).astype(BF16)
    br_pad = jnp.pad(b_router, ((0, 0), (0, LANE - N_EXPERTS)), constant_values=NEG_BIG)

    by_seq = lambda a: a.reshape(T_ALL // SEQ, SEQ, a.shape[-1])
    by_dec = lambda a: a.reshape(T_ALL // DEC_SEQ, DEC_SEQ, a.shape[-1])
    lat_off = T_CTX // DEC_SEQ

    caches = [a.reshape(DEC_BATCH, DEPTH, PAST_LEN, KV_W).astype(BF16)
              for a in (cache_a_k, cache_a_v, cache_c_k, cache_c_v)]

    x_ctx = x_prompt.reshape(T_CTX, D_MODEL)
    x_lat = x_sample.reshape(T_LAT, D_MODEL)
    new_kv = [[], [], [], []]
    for l in range(DEPTH):
        qa, ka_b, va_b, nka, nva, bu, bv, qc, kc_b, vc_b, nkc, nvc, gt = _input_projection(
            x_ctx, x_lat, mods, l, norm1_g[l][None, :], w_in_b[l], cs, sn,
            _qk_gain(a_q_norm[l], a_k_norm[l]), _qk_gain(c_q_norm[l], c_k_norm[l]), b_v_norm[l][None, :])
        for lst, arr in zip(new_kv, (nka, nva, nkc, nvc)):
            lst.append(arr.reshape(BATCH, SEQ, N_KV, HEAD_DIM))

        sink = a_sink[l]
        oa_ctx = _dense_attention(by_seq(qa), by_seq(ka_b), by_seq(va_b), None, sink,
                                  n_req=BATCH, off=0, tq=SEQ, key_chunk=SEQ)
        oc_ctx = _dense_attention(by_seq(qc), by_seq(kc_b), by_seq(vc_b), None, None,
                                  n_req=BATCH, off=0, tq=SEQ, key_chunk=SEQ)
        cak, cav, cck, ccv = (a[:, l] for a in caches)
        oa_lat = _window_attention(by_dec(qa), by_dec(ka_b), by_dec(va_b), cak, cav, sink,
                                   n_req=DEC_BATCH, off=lat_off)
        oc_lat = _dense_attention(by_dec(qc), by_dec(kc_b), by_dec(vc_b), (cck, ccv), None,
                                  n_req=DEC_BATCH, off=lat_off, tq=1024, key_chunk=512)

        bs_full = jnp.repeat(b_bs[l].T, B_GROUP_CH, axis=1)
        x1, h2p, afft = _merge(x_ctx, x_lat, oa_ctx.reshape(T_CTX, Q_W), oa_lat.reshape(T_LAT, Q_W), bu, bv,
                               oc_ctx.reshape(T_CTX, Q_W), oc_lat.reshape(T_LAT, Q_W), gt,
                               wa_b[l], wb_b[l], wc_b[l], wo_b[l], ws_b[l], bs_full,
                               mods, l, norm2_g[l][None, :], wr_pad[l], br_pad[l][None, :])

        aff_rows = lambda a, n_req, n: a.reshape(N_EXPERTS, n_req, n).transpose(1, 0, 2).reshape(n_req * N_EXPERTS, n)
        idx_c, val_c = _select(aff_rows(afft[:, :T_CTX], BATCH, SEQ), BATCH * N_EXPERTS, CAP_CTX)
        _, val_l, idx_l_rows = _select(aff_rows(afft[:, T_CTX:], DEC_BATCH, DEC_SEQ), N_EXPERTS, CAP_LAT)
        idx_l_flat = idx_l_rows.reshape(-1)
        xg_ctx = _gather_ctx(idx_c, h2p)
        xg_lat = _gather_lat(idx_l_flat, by_dec(h2p), lat_off)
        yg = _expert_ffn(xg_ctx, xg_lat, val_c.reshape(BATCH, N_EXPERTS, CAP_CTX, 1),
                         val_l.reshape(DEC_BATCH, N_EXPERTS, CAP_LAT, 1), w_gate, w_up, w_down, l)

        x_ctx = _scatter_ctx(idx_c, yg, x1, mods, l)
        x_lat = _scatter_lat(idx_l_flat, yg, by_dec(x1), mods, l, lat_off).reshape(T_LAT, D_MODEL)

    y_prompt = x_ctx.reshape(BATCH, SEQ, D_MODEL)
    y_sample = x_lat.reshape(DEC_BATCH, DEC_SEQ, D_MODEL)
    return (y_prompt, y_sample) + tuple(jnp.stack(lst, axis=1) for lst in new_kv)
```

```python
import functools

import jax
import numpy as np
import jax.numpy as jnp
from jax import lax
from jax.experimental import pallas as pl
from jax.experimental.pallas import tpu as pltpu

F32 = jnp.float32
BF16 = jnp.bfloat16
I32 = jnp.int32

D_MODEL = 1024
BATCH = 16
SEQ = 256
DEPTH = 2
DEC_BATCH = 2
DEC_SEQ = 4096
PAST_LEN = 256
GRID_W = 64
HEAD_DIM = 64
N_HEADS = 6
N_KV = 2
N_GRP = N_HEADS // N_KV
B_GROUPS = 4
B_GROUP_CH = 64
B_WIDTH = B_GROUPS * B_GROUP_CH
Q_W = N_HEADS * HEAD_DIM
KV_W = N_KV * HEAD_DIM
QK_W = Q_W + KV_W
N_BRANCH = 3
WINDOW = 128
BLOCK = 128
CHUNK = 128
N_EXPERTS = 16
EXPERT_FF = 1024
CAP_FACTOR = 2
ROPE_THETA = 10000.0
ROPE_FREQS = HEAD_DIM // 4
EPS = 1e-6
IN_WIDTH = 2 * (QK_W + KV_W) + 2 * B_WIDTH + N_BRANCH * D_MODEL

T_CTX = BATCH * SEQ
T_LAT = DEC_BATCH * DEC_SEQ
T_ALL = T_CTX + T_LAT
N_REQ = 1 + DEC_BATCH
CAP_CTX = CAP_FACTOR * SEQ // N_EXPERTS
CAP_LAT = CAP_FACTOR * DEC_SEQ // N_EXPERTS
ROWS_PER_EXPERT = BATCH * CAP_CTX + DEC_BATCH * CAP_LAT

LANE = 128
ROW_TILE = 512
N_CTX_TILES = T_CTX // ROW_TILE
FFN_ROW_TILE = 512
VMEM_LIMIT = 56 * 1024 * 1024
NEG_BIG = -1e30
LOG2_E = 1.4426950408889634

OFF_A = 0
OFF_AV = OFF_A + QK_W
OFF_BU = OFF_AV + KV_W
OFF_BV = OFF_BU + B_WIDTH
OFF_C = OFF_BV + B_WIDTH
OFF_CV = OFF_C + QK_W
OFF_G = OFF_CV + KV_W


def _params(sem, vmem=VMEM_LIMIT):
    return pltpu.CompilerParams(dimension_semantics=sem, vmem_limit_bytes=vmem)


def _sigmoid(x):
    return 1.0 / (1.0 + jnp.exp(-x))


def _gelu_tanh(x):
    return 0.5 * x * (1.0 + jnp.tanh(0.7978845608028654 * (x + 0.044715 * (x * x * x))))


def _split_bf16(x):
    hi = x.astype(BF16)
    lo = (x - hi.astype(F32)).astype(BF16)
    return hi, lo


def _mod_kernel(c_ref, w_ref, b_ref, o_ref):
    c = c_ref[...]
    s_hi, s_lo = _split_bf16(c * _sigmoid(c))
    w_hi, w_lo = _split_bf16(w_ref[0])
    acc = jnp.dot(s_hi, w_hi, preferred_element_type=F32)
    acc += jnp.dot(s_lo, w_hi, preferred_element_type=F32)
    acc += jnp.dot(s_hi, w_lo, preferred_element_type=F32)
    o_ref[0] = acc + b_ref[0]


def _modulation(cond8, w_mod, b_mod):
    n_col = 6 * D_MODEL // D_MODEL
    return pl.pallas_call(
        _mod_kernel,
        grid=(DEPTH, n_col),
        in_specs=[
            pl.BlockSpec((8, D_MODEL), lambda l, j: (0, 0)),
            pl.BlockSpec((1, D_MODEL, D_MODEL), lambda l, j: (l, 0, j)),
            pl.BlockSpec((1, 1, D_MODEL), lambda l, j: (l, 0, j)),
        ],
        out_specs=pl.BlockSpec((1, 8, D_MODEL), lambda l, j: (l, 0, j)),
        out_shape=jax.ShapeDtypeStruct((DEPTH, 8, 6 * D_MODEL), F32),
        compiler_params=_params(("arbitrary", "arbitrary")),
        name="modulation",
    )(cond8, w_mod, b_mod.reshape(DEPTH, 1, 6 * D_MODEL))


def _group_sumsq(y, bd_ref):
    return jnp.dot((y * y).astype(BF16), bd_ref[...], preferred_element_type=F32)


def _pick_pass(i, ctx_ref, lat_ref):
    return jnp.where(i < N_CTX_TILES, ctx_ref[...], lat_ref[...])


def _in_kernel(xc_ref, xl_ref, sc_ref, sh_ref, n1_ref, w_ref, cs_ref, sn_ref, ga_ref, gc_ref, gbv_ref, bd_qk_ref,
               bd_b_ref, qa_ref, ka_ref, va_ref, nka_ref, nva_ref, bu_ref, bv_ref, qc_ref, kc_ref, vc_ref, nkc_ref,
               nvc_ref, gt_ref):
    is_ctx = pl.program_id(0) < N_CTX_TILES
    x = _pick_pass(pl.program_id(0), xc_ref, xl_ref)
    ms = jnp.mean(x * x, axis=-1, keepdims=True)
    h = x * lax.rsqrt(ms + EPS) * n1_ref[...]
    h = h * (1.0 + sc_ref[...]) + sh_ref[...]
    hb = h.astype(BF16)
    tm = x.shape[0]

    def proj(c0, width):
        return jnp.dot(hb, w_ref[:, c0:c0 + width], preferred_element_type=F32)

    cs = jnp.concatenate([cs_ref[...]] * (QK_W // LANE), axis=1)
    sn = jnp.concatenate([sn_ref[...]] * (QK_W // LANE), axis=1)
    lane = lax.broadcasted_iota(I32, (tm, QK_W), 1)
    first_half = (lane & ROPE_FREQS) == 0

    def qk_post(y, gain_ref):
        yn = y * lax.rsqrt(_group_sumsq(y, bd_qk_ref) * (1.0 / HEAD_DIM) + EPS) * gain_ref[...]
        partner = jnp.where(first_half, pltpu.roll(yn, QK_W - ROPE_FREQS, 1), pltpu.roll(yn, ROPE_FREQS, 1))
        return yn * cs + partner * sn

    def mixer(off_qk, off_v, gain_ref, q_ref, k_ref, v_ref, nk_ref, nv_ref):
        y = qk_post(proj(off_qk, QK_W), gain_ref)
        v = proj(off_v, KV_W)
        q_ref[...] = y[:, :Q_W].astype(BF16)
        k_ref[...] = y[:, Q_W:].astype(BF16)
        v_ref[...] = v.astype(BF16)

        @pl.when(is_ctx)
        def _():
            nk_ref[...] = y[:, Q_W:]
            nv_ref[...] = v

    mixer(OFF_A, OFF_AV, ga_ref, qa_ref, ka_ref, va_ref, nka_ref, nva_ref)

    bu_ref[...] = _gelu_tanh(proj(OFF_BU, B_WIDTH)).astype(BF16)
    gv = _gelu_tanh(proj(OFF_BV, B_WIDTH))
    gvn = gv * lax.rsqrt(_group_sumsq(gv, bd_b_ref) * (1.0 / B_GROUP_CH) + EPS) * gbv_ref[...]
    bv_ref[...] = gvn.astype(BF16)

    mixer(OFF_C, OFF_CV, gc_ref, qc_ref, kc_ref, vc_ref, nkc_ref, nvc_ref)

    gate_chunk = 512
    for j in range(N_BRANCH * D_MODEL // gate_chunk):
        g = proj(OFF_G + j * gate_chunk, gate_chunk)
        gt_ref[:, j * gate_chunk:(j + 1) * gate_chunk] = _sigmoid(g).astype(BF16)


def _req_of_tile(i):
    return i // N_CTX_TILES


def _ctx_rows(w):
    return pl.BlockSpec((ROW_TILE, w), lambda i: (jnp.minimum(i, N_CTX_TILES - 1), 0))


def _lat_rows(w):
    return pl.BlockSpec((ROW_TILE, w), lambda i: (jnp.maximum(i - N_CTX_TILES, 0), 0))


MOD_SH1, MOD_SC1, MOD_G1, MOD_SH2, MOD_SC2, MOD_G2 = range(6)


def _mod_spec(layer, chunk, req):
    return pl.BlockSpec((None, None, 1, D_MODEL), lambda *g: (layer, req(*g), 0, chunk))


def _block_diag_ones(width, group):
    g = np.arange(width) // group
    return (g[:, None] == g[None, :]).astype(np.float32)


def _input_projection(x_ctx, x_lat, mods, layer, n1, w_in_b, cs, sn, gain_a, gain_c, gain_bv):
    bd_qk = jnp.asarray(_block_diag_ones(QK_W, HEAD_DIM), BF16)
    bd_b = jnp.asarray(_block_diag_ones(B_WIDTH, B_GROUP_CH), BF16)
    tm = ROW_TILE
    row = lambda w: pl.BlockSpec((tm, w), lambda i: (i, 0))
    full = lambda a: pl.BlockSpec(a.shape, lambda i: (0,) * a.ndim)
    rope = pl.BlockSpec((tm, LANE), lambda i: (_rope_tile(i), 0))
    mixer_outs = [(Q_W, BF16, T_ALL), (KV_W, BF16, T_ALL), (KV_W, BF16, T_ALL), (KV_W, F32, T_CTX), (KV_W, F32, T_CTX)]
    outs = mixer_outs + [(B_WIDTH, BF16, T_ALL), (B_WIDTH, BF16, T_ALL)] + mixer_outs + [(N_BRANCH * D_MODEL, BF16, T_ALL)]
    return pl.pallas_call(
        _in_kernel,
        grid=(T_ALL // tm,),
        in_specs=[_ctx_rows(D_MODEL), _lat_rows(D_MODEL), _mod_spec(layer, MOD_SC1, _req_of_tile),
                  _mod_spec(layer, MOD_SH1, _req_of_tile), full(n1), full(w_in_b), rope, rope,
                  full(gain_a), full(gain_c), full(gain_bv), full(bd_qk), full(bd_b)],
        out_specs=[row(w) if rows == T_ALL else _ctx_rows(w) for w, _, rows in outs],
        out_shape=[jax.ShapeDtypeStruct((rows, w), dt) for w, dt, rows in outs],
        compiler_params=_params(("arbitrary",)),
        name="input_projection",
    )(x_ctx, x_lat, mods, mods, n1, w_in_b, cs, sn, gain_a, gain_c, gain_bv, bd_qk, bd_b)


def _attention_tile(q_ref, sources, sink_ref, o_ref, qt_scr, ot_scr, *, tq, key_chunk):
    width = N_GRP * tq
    for j in range(Q_W // LANE):
        qt_scr[j * LANE:(j + 1) * LANE, :] = q_ref[0, :, j * LANE:(j + 1) * LANE].astype(F32).T.astype(BF16)
    for kv in range(N_KV):
        lo, hi = kv * HEAD_DIM, (kv + 1) * HEAD_DIM
        heads = [kv * N_GRP + g for g in range(N_GRP)]
        qt = jnp.concatenate([qt_scr[h * HEAD_DIM:(h + 1) * HEAD_DIM, :] for h in heads], axis=1)

        def step(carry, kref, vref, c0, size, bias):
            m, acc = carry
            s = jnp.dot(kref[0, pl.ds(c0, size), lo:hi], qt, preferred_element_type=F32)
            if bias is not None:
                s = s + jnp.concatenate([bias] * N_GRP, axis=1)
            vt = vref[0, pl.ds(c0, size), :].astype(F32).T[lo:hi, :].astype(BF16)
            vt = jnp.concatenate([vt, jnp.ones((DEN_ROWS, size), BF16)], axis=0)
            m_new = jnp.maximum(m, jnp.max(s, axis=0, keepdims=True))
            p = jnp.exp2(s - m_new).astype(BF16)
            acc = acc * jnp.exp2(m - m_new) + jnp.dot(vt, p, preferred_element_type=F32)
            return m_new, acc

        if sink_ref is not None:
            m0 = jnp.concatenate([jnp.full((1, tq), sink_ref[h] * LOG2_E, F32) for h in heads], axis=1)
            den0 = jnp.ones((DEN_ROWS, width), F32)
        else:
            m0 = jnp.full((1, width), NEG_BIG, F32)
            den0 = jnp.zeros((DEN_ROWS, width), F32)
        carry = (m0, jnp.concatenate([jnp.zeros((HEAD_DIM, width), F32), den0], axis=0))
        for kref, vref, bias in sources:
            n_rows = kref.shape[1]
            n_full = n_rows // key_chunk
            if bias is not None:
                carry = step(carry, kref, vref, 0, n_rows, bias)
                continue
            if n_full > 1:
                carry = lax.fori_loop(
                    0, n_full,
                    lambda c, cr: step(cr, kref, vref, pl.multiple_of(c * key_chunk, key_chunk), key_chunk, None), carry)
            elif n_full == 1:
                carry = step(carry, kref, vref, 0, key_chunk, None)
            if n_rows - n_full * key_chunk:
                carry = step(carry, kref, vref, n_full * key_chunk, n_rows - n_full * key_chunk, None)
        _, acc = carry
        o = acc[:HEAD_DIM] / acc[HEAD_DIM:HEAD_DIM + 1]
        for g, h in enumerate(heads):
            ot_scr[h * HEAD_DIM:(h + 1) * HEAD_DIM, :] = o[:, g * tq:(g + 1) * tq]
    for j in range(Q_W // LANE):
        o_ref[0, :, j * LANE:(j + 1) * LANE] = ot_scr[j * LANE:(j + 1) * LANE, :].T.astype(o_ref.dtype)


def _dense_attn_kernel(*refs, tq, key_chunk, has_extra, has_sink):
    refs = list(refs)
    q_ref, k_ref, v_ref = refs[:3]
    del refs[:3]
    sources = [(k_ref, v_ref, None)]
    if has_extra:
        sources.append((refs.pop(0), refs.pop(0), None))
    sink_ref = refs.pop(0) if has_sink else None
    o_ref, qt_scr, ot_scr = refs
    _attention_tile(q_ref, sources, sink_ref, o_ref, qt_scr, ot_scr, tq=tq, key_chunk=key_chunk)


def _attention_scratch(tq):
    return [pltpu.VMEM((Q_W, tq), BF16), pltpu.VMEM((Q_W, tq), F32)]


def _dense_attention(q, k, v, extra, sink, *, n_req, off, tq, key_chunk):
    s = q.shape[1]
    kv_spec = pl.BlockSpec((1, s, KV_W), lambda i, j: (off + i, 0, 0))
    in_specs = [pl.BlockSpec((1, tq, Q_W), lambda i, j: (off + i, j, 0)), kv_spec, kv_spec]
    args = [q, k, v]
    if extra is not None:
        in_specs += [pl.BlockSpec((1, extra[0].shape[1], KV_W), lambda i, j: (i, 0, 0))] * 2
        args += list(extra)
    if sink is not None:
        in_specs.append(pl.BlockSpec(memory_space=pltpu.SMEM))
        args.append(sink)
    return pl.pallas_call(
        functools.partial(_dense_attn_kernel, tq=tq, key_chunk=key_chunk, has_extra=extra is not None,
                          has_sink=sink is not None),
        grid=(n_req, s // tq),
        in_specs=in_specs,
        out_specs=pl.BlockSpec((1, tq, Q_W), lambda i, j: (i, j, 0)),
        out_shape=jax.ShapeDtypeStruct((n_req, s, Q_W), BF16),
        scratch_shapes=_attention_scratch(tq),
        compiler_params=_params(("arbitrary", "arbitrary")),
        name="dense_attention",
    )(*args)


DEN_ROWS = 16
WINDOW_TQ = 512


def _window_attn_kernel(q_ref, kp_ref, kc_ref, kn_ref, vp_ref, vc_ref, vn_ref, ck_ref, cv_ref, bp_ref, bc_ref, bn_ref,
                        sink_ref, o_ref, qt_scr, ot_scr, *, seq):
    q_pos0 = pl.program_id(1) * WINDOW_TQ
    prev_bias = bp_ref[...] + jnp.where(q_pos0 >= BLOCK, 0.0, NEG_BIG)
    next_bias = bn_ref[...] + jnp.where(q_pos0 + WINDOW_TQ < seq, 0.0, NEG_BIG)
    sources = [(kp_ref, vp_ref, prev_bias), (kc_ref, vc_ref, bc_ref[...]), (kn_ref, vn_ref, next_bias),
               (ck_ref, cv_ref, None)]
    _attention_tile(q_ref, sources, sink_ref, o_ref, qt_scr, ot_scr, tq=WINDOW_TQ, key_chunk=WINDOW_TQ)


def _band_bias(first_key, n_keys):
    d = (first_key + np.arange(n_keys))[:, None] - np.arange(WINDOW_TQ)[None, :]
    return np.where(np.abs(d) <= WINDOW, 0.0, NEG_BIG).astype(np.float32)


def _window_attention(q, k, v, ck, cv, sink, *, n_req, off):
    b, s = n_req, q.shape[1]
    nb = s // BLOCK
    per_tile = WINDOW_TQ // BLOCK
    edge = lambda f: pl.BlockSpec((1, BLOCK, KV_W), lambda i, j: (off + i, f(j), 0))
    prev = lambda j: jnp.maximum(j * per_tile - 1, 0)
    nxt = lambda j: jnp.minimum((j + 1) * per_tile, nb - 1)
    cur = pl.BlockSpec((1, WINDOW_TQ, KV_W), lambda i, j: (off + i, j, 0))
    ctx = pl.BlockSpec((1, PAST_LEN, KV_W), lambda i, j: (i, 0, 0))
    biases = [_band_bias(-BLOCK, BLOCK), _band_bias(0, WINDOW_TQ), _band_bias(WINDOW_TQ, BLOCK)]
    table = lambda a: pl.BlockSpec(a.shape, lambda i, j: (0, 0))
    return pl.pallas_call(
        functools.partial(_window_attn_kernel, seq=s),
        grid=(b, s // WINDOW_TQ),
        in_specs=[pl.BlockSpec((1, WINDOW_TQ, Q_W), lambda i, j: (off + i, j, 0)),
                  edge(prev), cur, edge(nxt), edge(prev), cur, edge(nxt), ctx, ctx,
                  table(biases[0]), table(biases[1]), table(biases[2]),
                  pl.BlockSpec(memory_space=pltpu.SMEM)],
        out_specs=pl.BlockSpec((1, WINDOW_TQ, Q_W), lambda i, j: (i, j, 0)),
        out_shape=jax.ShapeDtypeStruct((b, s, Q_W), BF16),
        scratch_shapes=_attention_scratch(WINDOW_TQ),
        compiler_params=_params(("arbitrary", "arbitrary")),
        name="window_attention",
    )(q, k, k, k, v, v, v, ck, cv, *biases, sink)


def _pack_halves(x):
    half = x.shape[1] // 2
    return pltpu.pack_elementwise([x[:, :half], x[:, half:]], packed_dtype=BF16)


def _unpack_halves(words):
    return tuple(pltpu.unpack_elementwise(words, index=i, packed_dtype=BF16, unpacked_dtype=F32).astype(BF16)
                 for i in range(2))


def _merge_kernel(xc_ref, xl_ref, oac_ref, oal_ref, bu_ref, bv_ref, occ_ref, ocl_ref, gt_ref, wa_ref, wb_ref, wc_ref,
                  wo_ref, ws_ref, bs_ref, g1_ref, sc2_ref, sh2_ref, n2_ref, wr_ref, br_ref, x1_ref, h2p_ref, afft_ref):
    i = pl.program_id(0)
    tm = xc_ref.shape[0]
    group = lax.broadcasted_iota(I32, (CHUNK, B_WIDTH), 1) // B_GROUP_CH
    obs = []
    for c in range(tm // CHUNK):
        v = bv_ref[c * CHUNK:(c + 1) * CHUNK, :]
        sv = jnp.zeros((CHUNK, B_WIDTH), F32)
        for g in range(B_GROUPS):
            sv = jnp.where(group == g, jnp.dot(ws_ref[g], v, preferred_element_type=F32), sv)
        u = bu_ref[c * CHUNK:(c + 1) * CHUNK, :].astype(F32)
        obs.append((u * (sv + bs_ref[...])).astype(BF16))
    ob = jnp.concatenate(obs, axis=0)

    oa = _pick_pass(i, oac_ref, oal_ref)
    oc = _pick_pass(i, occ_ref, ocl_ref)
    merged = gt_ref[:, 0:D_MODEL].astype(F32) * jnp.dot(oa, wa_ref[...], preferred_element_type=F32)
    merged += gt_ref[:, D_MODEL:2 * D_MODEL].astype(F32) * jnp.dot(ob, wb_ref[...], preferred_element_type=F32)
    merged += gt_ref[:, 2 * D_MODEL:3 * D_MODEL].astype(F32) * jnp.dot(oc, wc_ref[...], preferred_element_type=F32)
    y = jnp.dot(merged.astype(BF16), wo_ref[...], preferred_element_type=F32)
    x1 = _pick_pass(i, xc_ref, xl_ref) + g1_ref[...] * y
    x1_ref[...] = x1

    ms = jnp.mean(x1 * x1, axis=-1, keepdims=True)
    h2 = x1 * lax.rsqrt(ms + EPS) * n2_ref[...]
    h2 = h2 * (1.0 + sc2_ref[...]) + sh2_ref[...]
    h2p_ref[...] = _pack_halves(h2)

    logits = jnp.dot(h2.astype(BF16), wr_ref[...], preferred_element_type=F32) + br_ref[...]
    e = jnp.exp(logits - jnp.max(logits, axis=-1, keepdims=True))
    aff = e / jnp.sum(e, axis=-1, keepdims=True)
    afft_ref[...] = aff.T[:N_EXPERTS, :]


def _merge(x_ctx, x_lat, oa_ctx, oa_lat, bu, bv, oc_ctx, oc_lat, gt, wa, wb, wc, wo, ws, bs, mods, layer, n2, wr, br):
    tm = ROW_TILE
    row = lambda w: pl.BlockSpec((tm, w), lambda i: (i, 0))
    full = lambda a: pl.BlockSpec(a.shape, lambda i: (0,) * a.ndim)
    mod = lambda chunk: _mod_spec(layer, chunk, _req_of_tile)
    return pl.pallas_call(
        _merge_kernel,
        grid=(T_ALL // tm,),
        in_specs=[_ctx_rows(D_MODEL), _lat_rows(D_MODEL), _ctx_rows(Q_W), _lat_rows(Q_W), row(B_WIDTH), row(B_WIDTH),
                  _ctx_rows(Q_W), _lat_rows(Q_W), row(N_BRANCH * D_MODEL),
                  full(wa), full(wb), full(wc), full(wo), full(ws), full(bs),
                  mod(MOD_G1), mod(MOD_SC2), mod(MOD_SH2), full(n2), full(wr), full(br)],
        out_specs=[row(D_MODEL), row(D_MODEL // 2), pl.BlockSpec((N_EXPERTS, tm), lambda i: (0, i))],
        out_shape=[jax.ShapeDtypeStruct((T_ALL, D_MODEL), F32), jax.ShapeDtypeStruct((T_ALL, D_MODEL // 2), jnp.uint32),
                   jax.ShapeDtypeStruct((N_EXPERTS, T_ALL), F32)],
        compiler_params=_params(("arbitrary",)),
        name="merge_router",
    )(x_ctx, x_lat, oa_ctx, oa_lat, bu, bv, oc_ctx, oc_lat, gt, wa, wb, wc, wo, ws, bs, mods, mods, mods, n2, wr, br)


def _select_kernel(aff_ref, idx_ref, val_ref, *rest, n, cap, row_chunk):
    idx_row_ref = rest[0] if len(rest) == 4 else None
    possel_ref, idx_scr, val_scr = rest[-3:]
    a = aff_ref[...]
    rows = a.shape[0]
    tok = lax.broadcasted_iota(I32, (rows, n), 1)

    def count(ones):
        return jnp.sum(ones, axis=1, keepdims=True)

    def at_least(word):
        return jnp.where(a >= pltpu.bitcast(word, F32), 1, 0)

    thr = jnp.zeros((rows, 1), I32)
    for bit in range(30, -1, -1):
        cand = thr | (1 << bit)
        thr = jnp.where(count(at_least(cand)) >= cap, cand, thr)
    above = at_least(thr + 1)
    tied = at_least(thr) - above
    need = cap - count(above)
    last = jnp.zeros((rows, 1), I32)
    for bit in range(n.bit_length() - 2, -1, -1):
        cand = last | (1 << bit)
        last = jnp.where(count(jnp.where(tok < cand, tied, 0)) < need, cand, last)
    sel = above + jnp.where(tok <= last, tied, 0)

    blk = min(n, 256)
    tri = jnp.where(lax.broadcasted_iota(I32, (blk, blk), 0) <= lax.broadcasted_iota(I32, (blk, blk), 1),
                    1.0, 0.0).astype(BF16)
    sel_f = sel.astype(F32)
    offset = jnp.zeros((rows, 1), F32)
    for j in range(n // blk):
        s_blk = sel_f[:, j * blk:(j + 1) * blk]
        incl = jnp.dot(s_blk.astype(BF16), tri, preferred_element_type=F32)
        pos = (incl - s_blk + offset).astype(I32)
        possel_ref[:, j * blk:(j + 1) * blk] = jnp.where(sel[:, j * blk:(j + 1) * blk] > 0, pos, -1)
        offset = offset + incl[:, blk - 1:blk]

    tb = min(n, TOKEN_BLOCK)
    n_blk = n // tb

    def fold_lanes(x):
        acc = x[:, :LANE]
        for k in range(1, tb // LANE):
            acc = acc + x[:, k * LANE:(k + 1) * LANE]
        return acc

    def match(e, slot, t0):
        hit = possel_ref[pl.ds(e, 1), pl.ds(t0, tb)] == slot
        tok = t0 + lax.broadcasted_iota(I32, (1, tb), 1)
        return (fold_lanes(jnp.where(hit, tok, 0)),
                fold_lanes(jnp.where(hit, aff_ref[pl.ds(e, 1), pl.ds(t0, tb)], 0.0)))

    def per_row(e, _):
        ends, run = [], 0
        for j in range(n_blk - 1):
            run = run + jnp.sum(jnp.where(possel_ref[pl.ds(e, 1), j * tb:(j + 1) * tb] >= 0, 1, 0))
            ends.append(run)

        def per_chunk(c, _):
            r0 = pl.multiple_of(c * row_chunk, row_chunk)
            slot = lax.broadcasted_iota(I32, (row_chunk, 1), 0) + r0
            if n_blk == 1:
                idx, val = match(e, slot, 0)
            else:
                first = sum(jnp.where(end <= r0, 1, 0) for end in ends)
                last = 1 + sum(jnp.where(end < r0 + row_chunk, 1, 0) for end in ends)

                def per_block(j, acc):
                    i, v = match(e, slot, pl.multiple_of(j * tb, tb))
                    return acc[0] + i, acc[1] + v

                idx, val = lax.fori_loop(first, last, per_block,
                                         (jnp.zeros((row_chunk, LANE), I32), jnp.zeros((row_chunk, LANE), F32)))
            idx_scr[pl.ds(r0, row_chunk), :] = idx
            val_scr[pl.ds(r0, row_chunk), :] = val
            return 0

        lax.fori_loop(0, cap // row_chunk, per_chunk, 0)
        idx = jnp.sum(idx_scr[...], axis=1, keepdims=True)
        idx_ref[e] = idx
        val_ref[e] = jnp.sum(val_scr[...], axis=1, keepdims=True)
        if idx_row_ref is not None:
            idx_row_ref[pl.ds(e, 1), :] = jnp.broadcast_to(idx.astype(F32), (cap, LANE)).T[0:1, :].astype(I32)
        return 0

    def per_small_row(e, _):
        idx, val = match(e, lax.broadcasted_iota(I32, (cap, 1), 0), 0)
        idx_ref[e] = jnp.sum(idx, axis=1, keepdims=True)
        val_ref[e] = jnp.sum(val, axis=1, keepdims=True)
        return 0

    if n_blk == 1 and cap == row_chunk:
        lax.fori_loop(0, rows, per_small_row, 0, unroll=4)
    else:
        lax.fori_loop(0, rows, per_row, 0)


def _select(aff_rows, rows_per_step, cap):
    r, n = aff_rows.shape
    row_chunk = min(cap, 64)
    out_specs = [pl.BlockSpec((rows_per_step, cap, 1), lambda s: (s, 0, 0))] * 2
    out_shape = [jax.ShapeDtypeStruct((r, cap, 1), I32), jax.ShapeDtypeStruct((r, cap, 1), F32)]
    if cap % LANE == 0:
        out_specs.append(pl.BlockSpec((rows_per_step, cap), lambda s: (s, 0)))
        out_shape.append(jax.ShapeDtypeStruct((r, cap), I32))
    return pl.pallas_call(
        functools.partial(_select_kernel, n=n, cap=cap, row_chunk=row_chunk),
        grid=(r // rows_per_step,),
        in_specs=[pl.BlockSpec((rows_per_step, n), lambda s: (s, 0))],
        out_specs=out_specs,
        out_shape=out_shape,
        scratch_shapes=[pltpu.VMEM((rows_per_step, n), I32), pltpu.VMEM((cap, LANE), I32), pltpu.VMEM((cap, LANE), F32)],
        compiler_params=_params(("arbitrary",)),
        name="expert_select",
    )(aff_rows)


CTX_SLOTS = N_EXPERTS * CAP_CTX
TOKEN_BLOCK = 512
SLOT_GROUP = 8


def _ctx_slot_onehot(idx_ref, slots_on_rows):
    idx = idx_ref[...].reshape(CTX_SLOTS, 1)
    if slots_on_rows:
        hit = idx == lax.broadcasted_iota(I32, (CTX_SLOTS, SEQ), 1)
    else:
        idx_lane = jnp.broadcast_to(idx.astype(F32), (CTX_SLOTS, LANE)).T[0:1, :]
        hit = idx_lane == lax.broadcasted_iota(I32, (SEQ, CTX_SLOTS), 0).astype(F32)
    return jnp.where(hit, 1.0, 0.0).astype(BF16)


def _gather_ctx_kernel(idx_ref, h_ref, out_ref):
    onehot = _ctx_slot_onehot(idx_ref, True)
    lo, hi = _unpack_halves(h_ref[...])
    g_lo = jnp.dot(onehot, lo, preferred_element_type=F32)
    g_hi = jnp.dot(onehot, hi, preferred_element_type=F32)
    packed = pltpu.pack_elementwise([g_lo, g_hi], packed_dtype=BF16)
    out_ref[...] = packed.reshape(N_EXPERTS, CAP_CTX, D_MODEL // 2)


def _gather_ctx(idx_c, h2p):
    return pl.pallas_call(
        _gather_ctx_kernel,
        grid=(BATCH,),
        in_specs=[pl.BlockSpec((N_EXPERTS, CAP_CTX, 1), lambda b: (b, 0, 0)),
                  pl.BlockSpec((SEQ, D_MODEL // 2), lambda b: (b, 0))],
        out_specs=pl.BlockSpec((N_EXPERTS, CAP_CTX, D_MODEL // 2), lambda b: (0, b, 0)),
        out_shape=jax.ShapeDtypeStruct((N_EXPERTS, BATCH * CAP_CTX, D_MODEL // 2), jnp.uint32),
        compiler_params=_params(("arbitrary",)),
        name="gather_ctx",
    )(idx_c, h2p)


def _gather_lat_kernel(idx_ref, src_ref, out_ref):
    base = (pl.program_id(0) * N_EXPERTS + pl.program_id(1)) * CAP_LAT

    def body(it, _):
        r0 = pl.multiple_of(it * SLOT_GROUP, SLOT_GROUP)
        picked = [src_ref[0, pl.ds(idx_ref[base + r0 + k], 1), :] for k in range(SLOT_GROUP)]
        for k in range(SLOT_GROUP):
            out_ref[0, pl.ds(r0 + k, 1), :] = picked[k]
        return 0

    lax.fori_loop(0, CAP_LAT // SLOT_GROUP, body, 0)


def _gather_lat(idx_flat, h2p3, off):
    return pl.pallas_call(
        _gather_lat_kernel,
        grid_spec=pltpu.PrefetchScalarGridSpec(
            num_scalar_prefetch=1,
            grid=(DEC_BATCH, N_EXPERTS),
            in_specs=[pl.BlockSpec((1, DEC_SEQ, D_MODEL // 2), lambda b, e, idx: (off + b, 0, 0))],
            out_specs=pl.BlockSpec((1, CAP_LAT, D_MODEL // 2), lambda b, e, idx: (e, b, 0)),
        ),
        out_shape=jax.ShapeDtypeStruct((N_EXPERTS, DEC_BATCH * CAP_LAT, D_MODEL // 2), jnp.uint32),
        compiler_params=_params(("arbitrary", "arbitrary")),
        name="gather_lat",
    )(idx_flat, h2p3)


N_CTX_FFN_TILES = BATCH * CAP_CTX // FFN_ROW_TILE


def _ffn_kernel(xc_ref, xl_ref, vc_ref, vl_ref, wg_ref, wu_ref, wd_ref, o_ref, wg_b, wu_b, wd_b):
    j = pl.program_id(1)

    @pl.when(j == 0)
    def _():
        wg_b[...] = wg_ref[0].astype(BF16)
        wu_b[...] = wu_ref[0].astype(BF16)
        wd_b[...] = wd_ref[0].astype(BF16)

    is_ctx = j < N_CTX_FFN_TILES
    x = jnp.where(is_ctx, jnp.concatenate(_unpack_halves(xc_ref[0]), axis=1),
                  jnp.concatenate(_unpack_halves(xl_ref[0]), axis=1))
    g = jnp.dot(x, wg_b[...], preferred_element_type=F32)
    u = jnp.dot(x, wu_b[...], preferred_element_type=F32)
    hh = (g * _sigmoid(g)) * u
    y = jnp.dot(hh.astype(BF16), wd_b[...], preferred_element_type=F32)
    o_ref[0] = y * jnp.where(is_ctx, vc_ref[...].reshape(FFN_ROW_TILE, 1), vl_ref[...])


def _expert_ffn(xg_ctx, xg_lat, val_ctx, val_lat, w_gate, w_up, w_down, layer):
    tr = FFN_ROW_TILE
    assert tr == BATCH * CAP_CTX == CAP_LAT
    n_tiles = ROWS_PER_EXPERT // tr
    def wspec(k, n, tiles_held):
        ahead = lambda e, j: jnp.minimum(e + jnp.where(j >= tiles_held, 1, 0), N_EXPERTS - 1)
        return pl.BlockSpec((None, 1, k, n), lambda e, j: (layer, ahead(e, j), 0, 0))

    ctx_tile = lambda j: jnp.minimum(j, N_CTX_FFN_TILES - 1)
    lat_tile = lambda j: jnp.maximum(j - N_CTX_FFN_TILES, 0)
    return pl.pallas_call(
        _ffn_kernel,
        grid=(N_EXPERTS, n_tiles),
        in_specs=[pl.BlockSpec((1, tr, D_MODEL // 2), lambda e, j: (e, ctx_tile(j), 0)),
                  pl.BlockSpec((1, tr, D_MODEL // 2), lambda e, j: (e, lat_tile(j), 0)),
                  pl.BlockSpec((BATCH, None, CAP_CTX, 1), lambda e, j: (0, e, 0, 0)),
                  pl.BlockSpec((None, None, CAP_LAT, 1), lambda e, j: (lat_tile(j), e, 0, 0)),
                  wspec(D_MODEL, EXPERT_FF, 1), wspec(D_MODEL, EXPERT_FF, n_tiles - 1),
                  wspec(EXPERT_FF, D_MODEL, n_tiles)],
        out_specs=pl.BlockSpec((1, tr, D_MODEL), lambda e, j: (e, j, 0)),
        out_shape=jax.ShapeDtypeStruct((N_EXPERTS, ROWS_PER_EXPERT, D_MODEL), F32),
        scratch_shapes=[pltpu.VMEM((D_MODEL, EXPERT_FF), BF16), pltpu.VMEM((D_MODEL, EXPERT_FF), BF16),
                        pltpu.VMEM((EXPERT_FF, D_MODEL), BF16)],
        compiler_params=_params(("arbitrary", "arbitrary")),
        name="expert_ffn",
    )(xg_ctx, xg_lat, val_ctx, val_lat, w_gate, w_up, w_down)


def _scatter_ctx_kernel(idx_ref, y_ref, x1_ref, g2_ref, out_ref):
    onehot = _ctx_slot_onehot(idx_ref, False)
    y_hi, y_lo = _split_bf16(y_ref[...].reshape(CTX_SLOTS, D_MODEL))
    moe = jnp.dot(onehot, y_hi, preferred_element_type=F32) + jnp.dot(onehot, y_lo, preferred_element_type=F32)
    out_ref[...] = x1_ref[...] + g2_ref[...] * moe


def _scatter_ctx(idx_c, yg, x1, mods, layer):
    return pl.pallas_call(
        _scatter_ctx_kernel,
        grid=(BATCH,),
        in_specs=[pl.BlockSpec((N_EXPERTS, CAP_CTX, 1), lambda b: (b, 0, 0)),
                  pl.BlockSpec((N_EXPERTS, CAP_CTX, D_MODEL), lambda b: (0, b, 0)),
                  pl.BlockSpec((SEQ, D_MODEL), lambda b: (b, 0)),
                  _mod_spec(layer, MOD_G2, lambda b: 0)],
        out_specs=pl.BlockSpec((SEQ, D_MODEL), lambda b: (b, 0)),
        out_shape=jax.ShapeDtypeStruct((T_CTX, D_MODEL), F32),
        compiler_params=_params(("arbitrary",)),
        name="scatter_ctx",
    )(idx_c, yg, x1, mods)


def _scatter_lat_kernel(idx_ref, y_ref, x1_ref, g2_ref, out_ref):
    e = pl.program_id(2)

    @pl.when(e == 0)
    def _():
        out_ref[...] = jnp.zeros_like(out_ref)

    base = (pl.program_id(0) * N_EXPERTS + e) * CAP_LAT

    def body(it, _):
        r0 = pl.multiple_of(it * SLOT_GROUP, SLOT_GROUP)
        rows = [idx_ref[base + r0 + k] for k in range(SLOT_GROUP)]
        old = [out_ref[0, pl.ds(rows[k], 1), :] for k in range(SLOT_GROUP)]
        y = y_ref[0, pl.ds(r0, SLOT_GROUP), :]
        for k in range(SLOT_GROUP):
            out_ref[0, pl.ds(rows[k], 1), :] = old[k] + y[k:k + 1, :]
        return 0

    lax.fori_loop(0, CAP_LAT // SLOT_GROUP, body, 0)

    @pl.when(e == N_EXPERTS - 1)
    def _():
        out_ref[0] = x1_ref[0] + g2_ref[...] * out_ref[0]


def _scatter_lat(idx_flat, yg, x1_3, mods, layer, off):
    blk0 = BATCH * CAP_CTX // CAP_LAT
    return pl.pallas_call(
        _scatter_lat_kernel,
        grid_spec=pltpu.PrefetchScalarGridSpec(
            num_scalar_prefetch=1,
            grid=(DEC_BATCH, 1, N_EXPERTS),
            in_specs=[pl.BlockSpec((1, CAP_LAT, D_MODEL), lambda b, h, e, idx: (e, blk0 + b, 0)),
                      pl.BlockSpec((1, DEC_SEQ, D_MODEL), lambda b, h, e, idx: (off + b, 0, 0),
                                   pipeline_mode=pl.Buffered(1)),
                      _mod_spec(layer, MOD_G2, lambda b, h, e, idx: off + b)],
            out_specs=pl.BlockSpec((1, DEC_SEQ, D_MODEL), lambda b, h, e, idx: (b, 0, 0),
                                   pipeline_mode=pl.Buffered(1)),
        ),
        out_shape=jax.ShapeDtypeStruct((DEC_BATCH, DEC_SEQ, D_MODEL), F32),
        compiler_params=_params(("arbitrary", "arbitrary", "arbitrary")),
        name="scatter_lat",
    )(idx_flat, yg, x1_3, mods)


def _rope_tables():
    pos = np.arange(DEC_SEQ)
    freq = (np.float32(ROPE_THETA) ** (-np.arange(ROPE_FREQS, dtype=np.float32) / np.float32(ROPE_FREQS)))
    ang_r = (pos // GRID_W).astype(np.float32)[:, None] * freq.astype(np.float32)
    ang_c = (pos % GRID_W).astype(np.float32)[:, None] * freq.astype(np.float32)
    cos = np.concatenate([np.cos(ang_r)] * 2 + [np.cos(ang_c)] * 2, axis=-1)
    sin = np.concatenate([-np.sin(ang_r), np.sin(ang_r), -np.sin(ang_c), np.sin(ang_c)], axis=-1)
    reps = LANE // HEAD_DIM
    cs = np.concatenate([np.ones((ROW_TILE, LANE)), np.tile(cos, (1, reps))], axis=0).astype(np.float32)
    sn = np.concatenate([np.zeros((ROW_TILE, LANE)), np.tile(sin, (1, reps))], axis=0).astype(np.float32)
    return cs, sn


def _rope_tile(i):
    lat = jnp.maximum(i - N_CTX_TILES, 0) % (DEC_SEQ // ROW_TILE)
    return jnp.where(i < N_CTX_TILES, 0, 1 + lat)


def _qk_gain(q_norm, k_norm):
    q = jnp.tile(q_norm, N_HEADS) * (HEAD_DIM ** -0.5 * LOG2_E)
    return jnp.concatenate([q, jnp.tile(k_norm, N_KV)])[None, :]


def kernel(x_prompt, x_sample, cache_a_k, cache_a_v, cache_c_k, cache_c_v, c, c_ctx, norm1_g, w_mod, b_mod, w_in,
           a_q_norm, a_k_norm, a_sink, b_v_norm, b_ws, b_bs, c_q_norm, c_k_norm, w_a_o, w_b_o, w_c_o, w_out, norm2_g,
           w_router, b_router, w_gate, w_up, w_down):
    cond8 = jnp.concatenate([c_ctx[None, :], c, jnp.zeros((8 - N_REQ, D_MODEL), F32)], axis=0)
    mods = _modulation(cond8, w_mod, b_mod).reshape(DEPTH, 8, 1, 6 * D_MODEL)

    cs, sn = _rope_tables()
    w_in_b = w_in.astype(BF16)
    wa_b, wb_b, wc_b, wo_b = w_a_o.astype(BF16), w_b_o.astype(BF16), w_c_o.astype(BF16), w_out.astype(BF16)
    ws_b = b_ws.astype(BF16)
    wr_pad = jnp.pad(w_router, ((0, 0), (0, 0), (0, LANE - N_EXPERTS))).astype(BF16)
    br_pad = jnp.pad(b_router, ((0, 0), (0, LANE - N_EXPERTS)), constant_values=NEG_BIG)

    by_seq = lambda a: a.reshape(T_ALL // SEQ, SEQ, a.shape[-1])
    by_dec = lambda a: a.reshape(T_ALL // DEC_SEQ, DEC_SEQ, a.shape[-1])
    lat_off = T_CTX // DEC_SEQ

    caches = [a.reshape(DEC_BATCH, DEPTH, PAST_LEN, KV_W).astype(BF16)
              for a in (cache_a_k, cache_a_v, cache_c_k, cache_c_v)]

    x_ctx = x_prompt.reshape(T_CTX, D_MODEL)
    x_lat = x_sample.reshape(T_LAT, D_MODEL)
    new_kv = [[], [], [], []]
    for l in range(DEPTH):
        qa, ka_b, va_b, nka, nva, bu, bv, qc, kc_b, vc_b, nkc, nvc, gt = _input_projection(
            x_ctx, x_lat, mods, l, norm1_g[l][None, :], w_in_b[l], cs, sn,
            _qk_gain(a_q_norm[l], a_k_norm[l]), _qk_gain(c_q_norm[l], c_k_norm[l]), b_v_norm[l][None, :])
        for lst, arr in zip(new_kv, (nka, nva, nkc, nvc)):
            lst.append(arr.reshape(BATCH, SEQ, N_KV, HEAD_DIM))

        sink = a_sink[l]
        oa_ctx = _dense_attention(by_seq(qa), by_seq(ka_b), by_seq(va_b), None, sink,
                                  n_req=BATCH, off=0, tq=SEQ, key_chunk=SEQ)
        oc_ctx = _dense_attention(by_seq(qc), by_seq(kc_b), by_seq(vc_b), None, None,
                                  n_req=BATCH, off=0, tq=SEQ, key_chunk=SEQ)
        cak, cav, cck, ccv = (a[:, l] for a in caches)
        oa_lat = _window_attention(by_dec(qa), by_dec(ka_b), by_dec(va_b), cak, cav, sink,
                                   n_req=DEC_BATCH, off=lat_off)
        oc_lat = _dense_attention(by_dec(qc), by_dec(kc_b), by_dec(vc_b), (cck, ccv), None,
                                  n_req=DEC_BATCH, off=lat_off, tq=1024, key_chunk=512)

        bs_full = jnp.repeat(b_bs[l].T, B_GROUP_CH, axis=1)
        x1, h2p, afft = _merge(x_ctx, x_lat, oa_ctx.reshape(T_CTX, Q_W), oa_lat.reshape(T_LAT, Q_W), bu, bv,
                               oc_ctx.reshape(T_CTX, Q_W), oc_lat.reshape(T_LAT, Q_W), gt,
                               wa_b[l], wb_b[l], wc_b[l], wo_b[l], ws_b[l], bs_full,
                               mods, l, norm2_g[l][None, :], wr_pad[l], br_pad[l][None, :])

        aff_rows = lambda a, n_req, n: a.reshape(N_EXPERTS, n_req, n).transpose(1, 0, 2).reshape(n_req * N_EXPERTS, n)
        idx_c, val_c = _select(aff_rows(afft[:, :T_CTX], BATCH, SEQ), BATCH * N_EXPERTS, CAP_CTX)
        _, val_l, idx_l_rows = _select(aff_rows(afft[:, T_CTX:], DEC_BATCH, DEC_SEQ), N_EXPERTS, CAP_LAT)
        idx_l_flat = idx_l_rows.reshape(-1)
        xg_ctx = _gather_ctx(idx_c, h2p)
        xg_lat = _gather_lat(idx_l_flat, by_dec(h2p), lat_off)
        yg = _expert_ffn(xg_ctx, xg_lat, val_c.reshape(BATCH, N_EXPERTS, CAP_CTX, 1),
                         val_l.reshape(DEC_BATCH, N_EXPERTS, CAP_LAT, 1), w_gate, w_up, w_down, l)

        x_ctx = _scatter_ctx(idx_c, yg, x1, mods, l)
        x_lat = _scatter_lat(idx_l_flat, yg, by_dec(x1), mods, l, lat_off).reshape(T_LAT, D_MODEL)

    y_prompt = x_ctx.reshape(BATCH, SEQ, D_MODEL)
    y_sample = x_lat.reshape(DEC_BATCH, DEC_SEQ, D_MODEL)
    return (y_prompt, y_sample) + tuple(jnp.stack(lst, axis=1) for lst in new_kv)
```

```python
import functools

import jax
import numpy as np
import jax.numpy as jnp
from jax import lax
from jax.experimental import pallas as pl
from jax.experimental.pallas import tpu as pltpu

F32 = jnp.float32
BF16 = jnp.bfloat16
I32 = jnp.int32

D_MODEL = 1024
BATCH = 16
SEQ = 256
DEPTH = 2
DEC_BATCH = 2
DEC_SEQ = 4096
PAST_LEN = 256
GRID_W = 64
HEAD_DIM = 64
N_HEADS = 6
N_KV = 2
N_GRP = N_HEADS // N_KV
B_GROUPS = 4
B_GROUP_CH = 64
B_WIDTH = B_GROUPS * B_GROUP_CH
Q_W = N_HEADS * HEAD_DIM
KV_W = N_KV * HEAD_DIM
QK_W = Q_W + KV_W
N_BRANCH = 3
WINDOW = 128
BLOCK = 128
CHUNK = 128
N_EXPERTS = 16
EXPERT_FF = 1024
CAP_FACTOR = 2
ROPE_THETA = 10000.0
ROPE_FREQS = HEAD_DIM // 4
EPS = 1e-6
IN_WIDTH = 2 * (QK_W + KV_W) + 2 * B_WIDTH + N_BRANCH * D_MODEL

T_CTX = BATCH * SEQ
T_LAT = DEC_BATCH * DEC_SEQ
T_ALL = T_CTX + T_LAT
N_REQ = 1 + DEC_BATCH
CAP_CTX = CAP_FACTOR * SEQ // N_EXPERTS
CAP_LAT = CAP_FACTOR * DEC_SEQ // N_EXPERTS
ROWS_PER_EXPERT = BATCH * CAP_CTX + DEC_BATCH * CAP_LAT

LANE = 128
ROW_TILE = 512
N_CTX_TILES = T_CTX // ROW_TILE
FFN_ROW_TILE = 512
VMEM_LIMIT = 56 * 1024 * 1024
NEG_BIG = -1e30
LOG2_E = 1.4426950408889634

OFF_A = 0
OFF_AV = OFF_A + QK_W
OFF_BU = OFF_AV + KV_W
OFF_BV = OFF_BU + B_WIDTH
OFF_C = OFF_BV + B_WIDTH
OFF_CV = OFF_C + QK_W
OFF_G = OFF_CV + KV_W


def _params(sem, vmem=VMEM_LIMIT):
    return pltpu.CompilerParams(dimension_semantics=sem, vmem_limit_bytes=vmem)


def _sigmoid(x):
    return 1.0 / (1.0 + jnp.exp(-x))


def _gelu_tanh(x):
    return 0.5 * x * (1.0 + jnp.tanh(0.7978845608028654 * (x + 0.044715 * (x * x * x))))


def _split_bf16(x):
    hi = x.astype(BF16)
    lo = (x - hi.astype(F32)).astype(BF16)
    return hi, lo


def _mod_kernel(c_ref, w_ref, b_ref, o_ref):
    c = c_ref[...]
    s_hi, s_lo = _split_bf16(c * _sigmoid(c))
    w_hi, w_lo = _split_bf16(w_ref[0])
    acc = jnp.dot(s_hi, w_hi, preferred_element_type=F32)
    acc += jnp.dot(s_lo, w_hi, preferred_element_type=F32)
    acc += jnp.dot(s_hi, w_lo, preferred_element_type=F32)
    o_ref[0] = acc + b_ref[0]


def _modulation(cond8, w_mod, b_mod):
    n_col = 6 * D_MODEL // D_MODEL
    return pl.pallas_call(
        _mod_kernel,
        grid=(DEPTH, n_col),
        in_specs=[
            pl.BlockSpec((8, D_MODEL), lambda l, j: (0, 0)),
            pl.BlockSpec((1, D_MODEL, D_MODEL), lambda l, j: (l, 0, j)),
            pl.BlockSpec((1, 1, D_MODEL), lambda l, j: (l, 0, j)),
        ],
        out_specs=pl.BlockSpec((1, 8, D_MODEL), lambda l, j: (l, 0, j)),
        out_shape=jax.ShapeDtypeStruct((DEPTH, 8, 6 * D_MODEL), F32),
        compiler_params=_params(("arbitrary", "arbitrary")),
        name="modulation",
    )(cond8, w_mod, b_mod.reshape(DEPTH, 1, 6 * D_MODEL))


def _group_sumsq(y, bd_ref):
    return jnp.dot((y * y).astype(BF16), bd_ref[...], preferred_element_type=F32)


def _pick_pass(i, ctx_ref, lat_ref):
    return jnp.where(i < N_CTX_TILES, ctx_ref[...], lat_ref[...])


def _in_kernel(xc_ref, xl_ref, sc_ref, sh_ref, n1_ref, w_ref, cs_ref, sn_ref, ga_ref, gc_ref, gbv_ref, bd_qk_ref,
               bd_b_ref, qa_ref, ka_ref, va_ref, nka_ref, nva_ref, bu_ref, bv_ref, qc_ref, kc_ref, vc_ref, nkc_ref,
               nvc_ref, gt_ref):
    x = _pick_pass(pl.program_id(0), xc_ref, xl_ref)
    ms = jnp.mean(x * x, axis=-1, keepdims=True)
    h = x * lax.rsqrt(ms + EPS) * n1_ref[...]
    h = h * (1.0 + sc_ref[...]) + sh_ref[...]
    hb = h.astype(BF16)
    tm = x.shape[0]

    def proj(c0, width):
        return jnp.dot(hb, w_ref[:, c0:c0 + width], preferred_element_type=F32)

    cs = jnp.concatenate([cs_ref[...]] * (QK_W // LANE), axis=1)
    sn = jnp.concatenate([sn_ref[...]] * (QK_W // LANE), axis=1)
    lane = lax.broadcasted_iota(I32, (tm, QK_W), 1)
    first_half = (lane & ROPE_FREQS) == 0

    def qk_post(y, gain_ref):
        yn = y * lax.rsqrt(_group_sumsq(y, bd_qk_ref) * (1.0 / HEAD_DIM) + EPS) * gain_ref[...]
        partner = jnp.where(first_half, pltpu.roll(yn, QK_W - ROPE_FREQS, 1), pltpu.roll(yn, ROPE_FREQS, 1))
        return yn * cs + partner * sn

    def mixer(off_qk, off_v, gain_ref, q_ref, k_ref, v_ref, nk_ref, nv_ref):
        y = qk_post(proj(off_qk, QK_W), gain_ref)
        v = proj(off_v, KV_W)
        q_ref[...] = y[:, :Q_W].astype(BF16)
        k_ref[...] = y[:, Q_W:].astype(BF16)
        v_ref[...] = v.astype(BF16)

        nk_ref[...] = y[:, Q_W:]
        nv_ref[...] = v

    mixer(OFF_A, OFF_AV, ga_ref, qa_ref, ka_ref, va_ref, nka_ref, nva_ref)

    bu_ref[...] = _gelu_tanh(proj(OFF_BU, B_WIDTH)).astype(BF16)
    gv = _gelu_tanh(proj(OFF_BV, B_WIDTH))
    gvn = gv * lax.rsqrt(_group_sumsq(gv, bd_b_ref) * (1.0 / B_GROUP_CH) + EPS) * gbv_ref[...]
    bv_ref[...] = gvn.astype(BF16)

    mixer(OFF_C, OFF_CV, gc_ref, qc_ref, kc_ref, vc_ref, nkc_ref, nvc_ref)

    gate_chunk = 512
    for j in range(N_BRANCH * D_MODEL // gate_chunk):
        g = proj(OFF_G + j * gate_chunk, gate_chunk)
        gt_ref[:, j * gate_chunk:(j + 1) * gate_chunk] = _sigmoid(g).astype(BF16)


def _req_of_tile(i):
    return i // N_CTX_TILES


def _ctx_rows(w):
    return pl.BlockSpec((ROW_TILE, w), lambda i: (jnp.minimum(i, N_CTX_TILES - 1), 0))


def _lat_rows(w):
    return pl.BlockSpec((ROW_TILE, w), lambda i: (jnp.maximum(i - N_CTX_TILES, 0), 0))


MOD_SH1, MOD_SC1, MOD_G1, MOD_SH2, MOD_SC2, MOD_G2 = range(6)


def _mod_spec(layer, chunk, req):
    return pl.BlockSpec((None, None, 1, D_MODEL), lambda *g: (layer, req(*g), 0, chunk))


def _block_diag_ones(width, group):
    g = np.arange(width) // group
    return (g[:, None] == g[None, :]).astype(np.float32)


def _input_projection(x_ctx, x_lat, mods, layer, n1, w_in_b, cs, sn, gain_a, gain_c, gain_bv):
    bd_qk = jnp.asarray(_block_diag_ones(QK_W, HEAD_DIM), BF16)
    bd_b = jnp.asarray(_block_diag_ones(B_WIDTH, B_GROUP_CH), BF16)
    tm = ROW_TILE
    row = lambda w: pl.BlockSpec((tm, w), lambda i: (i, 0))
    full = lambda a: pl.BlockSpec(a.shape, lambda i: (0,) * a.ndim)
    rope = pl.BlockSpec((tm, LANE), lambda i: (_rope_tile(i), 0))
    cache_rows = T_CTX + tm
    spare = lambda w: pl.BlockSpec((tm, w), lambda i: (jnp.minimum(i, N_CTX_TILES), 0))
    mixer_outs = [(Q_W, BF16, T_ALL), (KV_W, BF16, T_ALL), (KV_W, BF16, T_ALL), (KV_W, F32, cache_rows),
                  (KV_W, F32, cache_rows)]
    outs = mixer_outs + [(B_WIDTH, BF16, T_ALL), (B_WIDTH, BF16, T_ALL)] + mixer_outs + [(N_BRANCH * D_MODEL, BF16, T_ALL)]
    return pl.pallas_call(
        _in_kernel,
        grid=(T_ALL // tm,),
        in_specs=[_ctx_rows(D_MODEL), _lat_rows(D_MODEL), _mod_spec(layer, MOD_SC1, _req_of_tile),
                  _mod_spec(layer, MOD_SH1, _req_of_tile), full(n1), full(w_in_b), rope, rope,
                  full(gain_a), full(gain_c), full(gain_bv), full(bd_qk), full(bd_b)],
        out_specs=[row(w) if rows == T_ALL else spare(w) for w, _, rows in outs],
        out_shape=[jax.ShapeDtypeStruct((rows, w), dt) for w, dt, rows in outs],
        compiler_params=_params(("arbitrary",)),
        name="input_projection",
    )(x_ctx, x_lat, mods, mods, n1, w_in_b, cs, sn, gain_a, gain_c, gain_bv, bd_qk, bd_b)


def _attention_tile(q_ref, sources, sink_ref, o_ref, qt_scr, ot_scr, *, tq, key_chunk):
    width = N_GRP * tq
    for j in range(Q_W // LANE):
        qt_scr[j * LANE:(j + 1) * LANE, :] = q_ref[0, :, j * LANE:(j + 1) * LANE].astype(F32).T.astype(BF16)
    for kv in range(N_KV):
        lo, hi = kv * HEAD_DIM, (kv + 1) * HEAD_DIM
        heads = [kv * N_GRP + g for g in range(N_GRP)]
        qt = jnp.concatenate([qt_scr[h * HEAD_DIM:(h + 1) * HEAD_DIM, :] for h in heads], axis=1)

        def step(carry, kref, vref, c0, size, bias):
            m, acc = carry
            s = jnp.dot(kref[0, pl.ds(c0, size), lo:hi], qt, preferred_element_type=F32)
            if bias is not None:
                s = s + jnp.concatenate([bias] * N_GRP, axis=1)
            vt = vref[0, pl.ds(c0, size), :].astype(F32).T[lo:hi, :].astype(BF16)
            vt = jnp.concatenate([vt, jnp.ones((DEN_ROWS, size), BF16)], axis=0)
            m_new = jnp.maximum(m, jnp.max(s, axis=0, keepdims=True))
            p = jnp.exp2(s - m_new).astype(BF16)
            acc = acc * jnp.exp2(m - m_new) + jnp.dot(vt, p, preferred_element_type=F32)
            return m_new, acc

        if sink_ref is not None:
            m0 = jnp.concatenate([jnp.full((1, tq), sink_ref[h] * LOG2_E, F32) for h in heads], axis=1)
            den0 = jnp.ones((DEN_ROWS, width), F32)
        else:
            m0 = jnp.full((1, width), NEG_BIG, F32)
            den0 = jnp.zeros((DEN_ROWS, width), F32)
        carry = (m0, jnp.concatenate([jnp.zeros((HEAD_DIM, width), F32), den0], axis=0))
        for kref, vref, bias in sources:
            n_rows = kref.shape[1]
            n_full = n_rows // key_chunk
            if bias is not None:
                carry = step(carry, kref, vref, 0, n_rows, bias)
                continue
            if n_full > 1:
                carry = lax.fori_loop(
                    0, n_full,
                    lambda c, cr: step(cr, kref, vref, pl.multiple_of(c * key_chunk, key_chunk), key_chunk, None), carry)
            elif n_full == 1:
                carry = step(carry, kref, vref, 0, key_chunk, None)
            if n_rows - n_full * key_chunk:
                carry = step(carry, kref, vref, n_full * key_chunk, n_rows - n_full * key_chunk, None)
        _, acc = carry
        o = acc[:HEAD_DIM] / acc[HEAD_DIM:HEAD_DIM + 1]
        for g, h in enumerate(heads):
            ot_scr[h * HEAD_DIM:(h + 1) * HEAD_DIM, :] = o[:, g * tq:(g + 1) * tq]
    for j in range(Q_W // LANE):
        o_ref[0, :, j * LANE:(j + 1) * LANE] = ot_scr[j * LANE:(j + 1) * LANE, :].T.astype(o_ref.dtype)


def _dense_attn_kernel(*refs, tq, key_chunk, has_extra, has_sink):
    refs = list(refs)
    q_ref, k_ref, v_ref = refs[:3]
    del refs[:3]
    sources = [(k_ref, v_ref, None)]
    if has_extra:
        sources.append((refs.pop(0), refs.pop(0), None))
    sink_ref = refs.pop(0) if has_sink else None
    o_ref, qt_scr, ot_scr = refs
    _attention_tile(q_ref, sources, sink_ref, o_ref, qt_scr, ot_scr, tq=tq, key_chunk=key_chunk)


def _attention_scratch(tq):
    return [pltpu.VMEM((Q_W, tq), BF16), pltpu.VMEM((Q_W, tq), F32)]


def _dense_attention(q, k, v, extra, sink, *, n_req, off, tq, key_chunk):
    s = q.shape[1]
    kv_spec = pl.BlockSpec((1, s, KV_W), lambda i, j: (off + i, 0, 0))
    in_specs = [pl.BlockSpec((1, tq, Q_W), lambda i, j: (off + i, j, 0)), kv_spec, kv_spec]
    args = [q, k, v]
    if extra is not None:
        in_specs += [pl.BlockSpec((1, extra[0].shape[1], KV_W), lambda i, j: (i, 0, 0))] * 2
        args += list(extra)
    if sink is not None:
        in_specs.append(pl.BlockSpec(memory_space=pltpu.SMEM))
        args.append(sink)
    return pl.pallas_call(
        functools.partial(_dense_attn_kernel, tq=tq, key_chunk=key_chunk, has_extra=extra is not None,
                          has_sink=sink is not None),
        grid=(n_req, s // tq),
        in_specs=in_specs,
        out_specs=pl.BlockSpec((1, tq, Q_W), lambda i, j: (i, j, 0)),
        out_shape=jax.ShapeDtypeStruct((n_req, s, Q_W), BF16),
        scratch_shapes=_attention_scratch(tq),
        compiler_params=_params(("arbitrary", "arbitrary")),
        name="dense_attention",
    )(*args)


DEN_ROWS = 16
WINDOW_TQ = 512


def _window_attn_kernel(q_ref, kp_ref, kc_ref, kn_ref, vp_ref, vc_ref, vn_ref, ck_ref, cv_ref, bp_ref, bc_ref, bn_ref,
                        sink_ref, o_ref, qt_scr, ot_scr, *, seq):
    q_pos0 = pl.program_id(1) * WINDOW_TQ
    prev_bias = bp_ref[...] + jnp.where(q_pos0 >= BLOCK, 0.0, NEG_BIG)
    next_bias = bn_ref[...] + jnp.where(q_pos0 + WINDOW_TQ < seq, 0.0, NEG_BIG)
    sources = [(kp_ref, vp_ref, prev_bias), (kc_ref, vc_ref, bc_ref[...]), (kn_ref, vn_ref, next_bias),
               (ck_ref, cv_ref, None)]
    _attention_tile(q_ref, sources, sink_ref, o_ref, qt_scr, ot_scr, tq=WINDOW_TQ, key_chunk=WINDOW_TQ)


def _band_bias(first_key, n_keys):
    d = (first_key + np.arange(n_keys))[:, None] - np.arange(WINDOW_TQ)[None, :]
    return np.where(np.abs(d) <= WINDOW, 0.0, NEG_BIG).astype(np.float32)


def _window_attention(q, k, v, ck, cv, sink, *, n_req, off):
    b, s = n_req, q.shape[1]
    nb = s // BLOCK
    per_tile = WINDOW_TQ // BLOCK
    edge = lambda f: pl.BlockSpec((1, BLOCK, KV_W), lambda i, j: (off + i, f(j), 0))
    prev = lambda j: jnp.maximum(j * per_tile - 1, 0)
    nxt = lambda j: jnp.minimum((j + 1) * per_tile, nb - 1)
    cur = pl.BlockSpec((1, WINDOW_TQ, KV_W), lambda i, j: (off + i, j, 0))
    ctx = pl.BlockSpec((1, PAST_LEN, KV_W), lambda i, j: (i, 0, 0))
    biases = [_band_bias(-BLOCK, BLOCK), _band_bias(0, WINDOW_TQ), _band_bias(WINDOW_TQ, BLOCK)]
    table = lambda a: pl.BlockSpec(a.shape, lambda i, j: (0, 0))
    return pl.pallas_call(
        functools.partial(_window_attn_kernel, seq=s),
        grid=(b, s // WINDOW_TQ),
        in_specs=[pl.BlockSpec((1, WINDOW_TQ, Q_W), lambda i, j: (off + i, j, 0)),
                  edge(prev), cur, edge(nxt), edge(prev), cur, edge(nxt), ctx, ctx,
                  table(biases[0]), table(biases[1]), table(biases[2]),
                  pl.BlockSpec(memory_space=pltpu.SMEM)],
        out_specs=pl.BlockSpec((1, WINDOW_TQ, Q_W), lambda i, j: (i, j, 0)),
        out_shape=jax.ShapeDtypeStruct((b, s, Q_W), BF16),
        scratch_shapes=_attention_scratch(WINDOW_TQ),
        compiler_params=_params(("arbitrary", "arbitrary")),
        name="window_attention",
    )(q, k, k, k, v, v, v, ck, cv, *biases, sink)


def _pack_halves(x):
    half = x.shape[1] // 2
    return pltpu.pack_elementwise([x[:, :half], x[:, half:]], packed_dtype=BF16)


def _unpack_halves(words):
    return tuple(pltpu.unpack_elementwise(words, index=i, packed_dtype=BF16, unpacked_dtype=F32).astype(BF16)
                 for i in range(2))


def _merge_kernel(xc_ref, xl_ref, oac_ref, oal_ref, bu_ref, bv_ref, occ_ref, ocl_ref, gt_ref, wa_ref, wb_ref, wc_ref,
                  wo_ref, ws_ref, bs_ref, g1_ref, sc2_ref, sh2_ref, n2_ref, wr_ref, br_ref, x1_ref, h2p_ref, afft_ref):
    i = pl.program_id(0)
    tm = xc_ref.shape[0]
    group = lax.broadcasted_iota(I32, (CHUNK, B_WIDTH), 1) // B_GROUP_CH
    obs = []
    for c in range(tm // CHUNK):
        v = bv_ref[c * CHUNK:(c + 1) * CHUNK, :]
        sv = jnp.zeros((CHUNK, B_WIDTH), F32)
        for g in range(B_GROUPS):
            sv = jnp.where(group == g, jnp.dot(ws_ref[g], v, preferred_element_type=F32), sv)
        u = bu_ref[c * CHUNK:(c + 1) * CHUNK, :].astype(F32)
        obs.append((u * (sv + bs_ref[...])).astype(BF16))
    ob = jnp.concatenate(obs, axis=0)

    oa = _pick_pass(i, oac_ref, oal_ref)
    oc = _pick_pass(i, occ_ref, ocl_ref)
    merged = gt_ref[:, 0:D_MODEL].astype(F32) * jnp.dot(oa, wa_ref[...], preferred_element_type=F32)
    merged += gt_ref[:, D_MODEL:2 * D_MODEL].astype(F32) * jnp.dot(ob, wb_ref[...], preferred_element_type=F32)
    merged += gt_ref[:, 2 * D_MODEL:3 * D_MODEL].astype(F32) * jnp.dot(oc, wc_ref[...], preferred_element_type=F32)
    y = jnp.dot(merged.astype(BF16), wo_ref[...], preferred_element_type=F32)
    x1 = _pick_pass(i, xc_ref, xl_ref) + g1_ref[...] * y
    x1_ref[...] = x1

    ms = jnp.mean(x1 * x1, axis=-1, keepdims=True)
    h2 = x1 * lax.rsqrt(ms + EPS) * n2_ref[...]
    h2 = h2 * (1.0 + sc2_ref[...]) + sh2_ref[...]
    h2p_ref[...] = _pack_halves(h2)

    logits = jnp.dot(h2.astype(BF16), wr_ref[...], preferred_element_type=F32) + br_ref[...]
    e = jnp.exp(logits - jnp.max(logits, axis=-1, keepdims=True))
    aff = e / jnp.sum(e, axis=-1, keepdims=True)
    afft_ref[...] = aff.T[:N_EXPERTS, :]


def _merge(x_ctx, x_lat, oa_ctx, oa_lat, bu, bv, oc_ctx, oc_lat, gt, wa, wb, wc, wo, ws, bs, mods, layer, n2, wr, br):
    tm = ROW_TILE
    row = lambda w: pl.BlockSpec((tm, w), lambda i: (i, 0))
    full = lambda a: pl.BlockSpec(a.shape, lambda i: (0,) * a.ndim)
    mod = lambda chunk: _mod_spec(layer, chunk, _req_of_tile)
    return pl.pallas_call(
        _merge_kernel,
        grid=(T_ALL // tm,),
        in_specs=[_ctx_rows(D_MODEL), _lat_rows(D_MODEL), _ctx_rows(Q_W), _lat_rows(Q_W), row(B_WIDTH), row(B_WIDTH),
                  _ctx_rows(Q_W), _lat_rows(Q_W), row(N_BRANCH * D_MODEL),
                  full(wa), full(wb), full(wc), full(wo), full(ws), full(bs),
                  mod(MOD_G1), mod(MOD_SC2), mod(MOD_SH2), full(n2), full(wr), full(br)],
        out_specs=[row(D_MODEL), row(D_MODEL // 2), pl.BlockSpec((N_EXPERTS, tm), lambda i: (0, i))],
        out_shape=[jax.ShapeDtypeStruct((T_ALL, D_MODEL), F32), jax.ShapeDtypeStruct((T_ALL, D_MODEL // 2), jnp.uint32),
                   jax.ShapeDtypeStruct((N_EXPERTS, T_ALL), F32)],
        compiler_params=_params(("arbitrary",)),
        name="merge_router",
    )(x_ctx, x_lat, oa_ctx, oa_lat, bu, bv, oc_ctx, oc_lat, gt, wa, wb, wc, wo, ws, bs, mods, mods, mods, n2, wr, br)


def _select_kernel(aff_ref, idx_ref, val_ref, *rest, n, cap, row_chunk):
    idx_row_ref = rest[0] if len(rest) == 4 else None
    possel_ref, idx_scr, val_scr = rest[-3:]
    a = aff_ref[...]
    rows = a.shape[0]
    tok = lax.broadcasted_iota(I32, (rows, n), 1)

    def count(ones):
        return jnp.sum(ones, axis=1, keepdims=True)

    def at_least(word):
        return jnp.where(a >= pltpu.bitcast(word, F32), 1, 0)

    thr = jnp.zeros((rows, 1), I32)
    for bit in range(30, -1, -1):
        cand = thr | (1 << bit)
        thr = jnp.where(count(at_least(cand)) >= cap, cand, thr)
    above = at_least(thr + 1)
    tied = at_least(thr) - above
    need = cap - count(above)
    last = jnp.zeros((rows, 1), I32)
    for bit in range(n.bit_length() - 2, -1, -1):
        cand = last | (1 << bit)
        last = jnp.where(count(jnp.where(tok < cand, tied, 0)) < need, cand, last)
    sel = above + jnp.where(tok <= last, tied, 0)

    blk = min(n, 256)
    tri = jnp.where(lax.broadcasted_iota(I32, (blk, blk), 0) <= lax.broadcasted_iota(I32, (blk, blk), 1),
                    1.0, 0.0).astype(BF16)
    sel_f = sel.astype(F32)
    offset = jnp.zeros((rows, 1), F32)
    for j in range(n // blk):
        s_blk = sel_f[:, j * blk:(j + 1) * blk]
        incl = jnp.dot(s_blk.astype(BF16), tri, preferred_element_type=F32)
        pos = (incl - s_blk + offset).astype(I32)
        possel_ref[:, j * blk:(j + 1) * blk] = jnp.where(sel[:, j * blk:(j + 1) * blk] > 0, pos, -1)
        offset = offset + incl[:, blk - 1:blk]

    tb = min(n, TOKEN_BLOCK)
    n_blk = n // tb

    def fold_lanes(x):
        acc = x[:, :LANE]
        for k in range(1, tb // LANE):
            acc = acc + x[:, k * LANE:(k + 1) * LANE]
        return acc

    def match(e, slot, t0):
        hit = possel_ref[pl.ds(e, 1), pl.ds(t0, tb)] == slot
        tok = t0 + lax.broadcasted_iota(I32, (1, tb), 1)
        return (fold_lanes(jnp.where(hit, tok, 0)),
                fold_lanes(jnp.where(hit, aff_ref[pl.ds(e, 1), pl.ds(t0, tb)], 0.0)))

    def per_row(e, _):
        ends, run = [], 0
        for j in range(n_blk - 1):
            run = run + jnp.sum(jnp.where(possel_ref[pl.ds(e, 1), j * tb:(j + 1) * tb] >= 0, 1, 0))
            ends.append(run)

        def per_chunk(c, _):
            r0 = pl.multiple_of(c * row_chunk, row_chunk)
            slot = lax.broadcasted_iota(I32, (row_chunk, 1), 0) + r0
            if n_blk == 1:
                idx, val = match(e, slot, 0)
            else:
                first = sum(jnp.where(end <= r0, 1, 0) for end in ends)
                last = 1 + sum(jnp.where(end < r0 + row_chunk, 1, 0) for end in ends)

                def per_block(j, acc):
                    i, v = match(e, slot, pl.multiple_of(j * tb, tb))
                    return acc[0] + i, acc[1] + v

                idx, val = lax.fori_loop(first, last, per_block,
                                         (jnp.zeros((row_chunk, LANE), I32), jnp.zeros((row_chunk, LANE), F32)))
            idx_scr[pl.ds(r0, row_chunk), :] = idx
            val_scr[pl.ds(r0, row_chunk), :] = val
            return 0

        lax.fori_loop(0, cap // row_chunk, per_chunk, 0)
        idx = jnp.sum(idx_scr[...], axis=1, keepdims=True)
        idx_ref[e] = idx
        val_ref[e] = jnp.sum(val_scr[...], axis=1, keepdims=True)
        if idx_row_ref is not None:
            idx_row_ref[pl.ds(e, 1), :] = jnp.broadcast_to(idx.astype(F32), (cap, LANE)).T[0:1, :].astype(I32)
        return 0

    def per_small_row(e, _):
        idx, val = match(e, lax.broadcasted_iota(I32, (cap, 1), 0), 0)
        idx_ref[e] = jnp.sum(idx, axis=1, keepdims=True)
        val_ref[e] = jnp.sum(val, axis=1, keepdims=True)
        return 0

    if n_blk == 1 and cap == row_chunk:
        lax.fori_loop(0, rows, per_small_row, 0, unroll=4)
    else:
        lax.fori_loop(0, rows, per_row, 0)


def _select(aff_rows, rows_per_step, cap):
    r, n = aff_rows.shape
    row_chunk = min(cap, 64)
    out_specs = [pl.BlockSpec((rows_per_step, cap, 1), lambda s: (s, 0, 0))] * 2
    out_shape = [jax.ShapeDtypeStruct((r, cap, 1), I32), jax.ShapeDtypeStruct((r, cap, 1), F32)]
    if cap % LANE == 0:
        out_specs.append(pl.BlockSpec((rows_per_step, cap), lambda s: (s, 0)))
        out_shape.append(jax.ShapeDtypeStruct((r, cap), I32))
    return pl.pallas_call(
        functools.partial(_select_kernel, n=n, cap=cap, row_chunk=row_chunk),
        grid=(r // rows_per_step,),
        in_specs=[pl.BlockSpec((rows_per_step, n), lambda s: (s, 0))],
        out_specs=out_specs,
        out_shape=out_shape,
        scratch_shapes=[pltpu.VMEM((rows_per_step, n), I32), pltpu.VMEM((cap, LANE), I32), pltpu.VMEM((cap, LANE), F32)],
        compiler_params=_params(("arbitrary",)),
        name="expert_select",
    )(aff_rows)


CTX_SLOTS = N_EXPERTS * CAP_CTX
TOKEN_BLOCK = 512
SLOT_GROUP = 8


def _ctx_slot_onehot(idx_ref, slots_on_rows):
    idx = idx_ref[...].reshape(CTX_SLOTS, 1)
    if slots_on_rows:
        hit = idx == lax.broadcasted_iota(I32, (CTX_SLOTS, SEQ), 1)
    else:
        idx_lane = jnp.broadcast_to(idx.astype(F32), (CTX_SLOTS, LANE)).T[0:1, :]
        hit = idx_lane == lax.broadcasted_iota(I32, (SEQ, CTX_SLOTS), 0).astype(F32)
    return jnp.where(hit, 1.0, 0.0).astype(BF16)


def _gather_ctx_kernel(idx_ref, h_ref, out_ref):
    onehot = _ctx_slot_onehot(idx_ref, True)
    lo, hi = _unpack_halves(h_ref[...])
    g_lo = jnp.dot(onehot, lo, preferred_element_type=F32)
    g_hi = jnp.dot(onehot, hi, preferred_element_type=F32)
    packed = pltpu.pack_elementwise([g_lo, g_hi], packed_dtype=BF16)
    out_ref[...] = packed.reshape(N_EXPERTS, CAP_CTX, D_MODEL // 2)


def _gather_ctx(idx_c, h2p):
    return pl.pallas_call(
        _gather_ctx_kernel,
        grid=(BATCH,),
        in_specs=[pl.BlockSpec((N_EXPERTS, CAP_CTX, 1), lambda b: (b, 0, 0)),
                  pl.BlockSpec((SEQ, D_MODEL // 2), lambda b: (b, 0))],
        out_specs=pl.BlockSpec((N_EXPERTS, CAP_CTX, D_MODEL // 2), lambda b: (0, b, 0)),
        out_shape=jax.ShapeDtypeStruct((N_EXPERTS, BATCH * CAP_CTX, D_MODEL // 2), jnp.uint32),
        compiler_params=_params(("arbitrary",)),
        name="gather_ctx",
    )(idx_c, h2p)


def _gather_lat_kernel(idx_ref, src_ref, out_ref):
    base = (pl.program_id(0) * N_EXPERTS + pl.program_id(1)) * CAP_LAT

    def body(it, _):
        r0 = pl.multiple_of(it * SLOT_GROUP, SLOT_GROUP)
        picked = [src_ref[0, pl.ds(idx_ref[base + r0 + k], 1), :] for k in range(SLOT_GROUP)]
        for k in range(SLOT_GROUP):
            out_ref[0, pl.ds(r0 + k, 1), :] = picked[k]
        return 0

    lax.fori_loop(0, CAP_LAT // SLOT_GROUP, body, 0)


def _gather_lat(idx_flat, h2p3, off):
    return pl.pallas_call(
        _gather_lat_kernel,
        grid_spec=pltpu.PrefetchScalarGridSpec(
            num_scalar_prefetch=1,
            grid=(DEC_BATCH, N_EXPERTS),
            in_specs=[pl.BlockSpec((1, DEC_SEQ, D_MODEL // 2), lambda b, e, idx: (off + b, 0, 0))],
            out_specs=pl.BlockSpec((1, CAP_LAT, D_MODEL // 2), lambda b, e, idx: (e, b, 0)),
        ),
        out_shape=jax.ShapeDtypeStruct((N_EXPERTS, DEC_BATCH * CAP_LAT, D_MODEL // 2), jnp.uint32),
        compiler_params=_params(("arbitrary", "arbitrary")),
        name="gather_lat",
    )(idx_flat, h2p3)


N_CTX_FFN_TILES = BATCH * CAP_CTX // FFN_ROW_TILE


def _ffn_kernel(xc_ref, xl_ref, vc_ref, vl_ref, wg_ref, wu_ref, wd_ref, o_ref, wg_b, wu_b, wd_b):
    j = pl.program_id(1)

    @pl.when(j == 0)
    def _():
        wg_b[...] = wg_ref[0].astype(BF16)
        wu_b[...] = wu_ref[0].astype(BF16)
        wd_b[...] = wd_ref[0].astype(BF16)

    is_ctx = j < N_CTX_FFN_TILES
    x = jnp.where(is_ctx, jnp.concatenate(_unpack_halves(xc_ref[0]), axis=1),
                  jnp.concatenate(_unpack_halves(xl_ref[0]), axis=1))
    g = jnp.dot(x, wg_b[...], preferred_element_type=F32)
    u = jnp.dot(x, wu_b[...], preferred_element_type=F32)
    hh = (g * _sigmoid(g)) * u
    y = jnp.dot(hh.astype(BF16), wd_b[...], preferred_element_type=F32)
    o_ref[0] = y * jnp.where(is_ctx, vc_ref[...].reshape(FFN_ROW_TILE, 1), vl_ref[...])


def _expert_ffn(xg_ctx, xg_lat, val_ctx, val_lat, w_gate, w_up, w_down, layer):
    tr = FFN_ROW_TILE
    assert tr == BATCH * CAP_CTX == CAP_LAT
    n_tiles = ROWS_PER_EXPERT // tr
    def wspec(k, n, tiles_held):
        ahead = lambda e, j: jnp.minimum(e + jnp.where(j >= tiles_held, 1, 0), N_EXPERTS - 1)
        return pl.BlockSpec((None, 1, k, n), lambda e, j: (layer, ahead(e, j), 0, 0))

    ctx_tile = lambda j: jnp.minimum(j, N_CTX_FFN_TILES - 1)
    lat_tile = lambda j: jnp.maximum(j - N_CTX_FFN_TILES, 0)
    return pl.pallas_call(
        _ffn_kernel,
        grid=(N_EXPERTS, n_tiles),
        in_specs=[pl.BlockSpec((1, tr, D_MODEL // 2), lambda e, j: (e, ctx_tile(j), 0)),
                  pl.BlockSpec((1, tr, D_MODEL // 2), lambda e, j: (e, lat_tile(j), 0)),
                  pl.BlockSpec((BATCH, None, CAP_CTX, 1), lambda e, j: (0, e, 0, 0)),
                  pl.BlockSpec((None, None, CAP_LAT, 1), lambda e, j: (lat_tile(j), e, 0, 0)),
                  wspec(D_MODEL, EXPERT_FF, 1), wspec(D_MODEL, EXPERT_FF, n_tiles - 1),
                  wspec(EXPERT_FF, D_MODEL, n_tiles)],
        out_specs=pl.BlockSpec((1, tr, D_MODEL), lambda e, j: (e, j, 0)),
        out_shape=jax.ShapeDtypeStruct((N_EXPERTS, ROWS_PER_EXPERT, D_MODEL), F32),
        scratch_shapes=[pltpu.VMEM((D_MODEL, EXPERT_FF), BF16), pltpu.VMEM((D_MODEL, EXPERT_FF), BF16),
                        pltpu.VMEM((EXPERT_FF, D_MODEL), BF16)],
        compiler_params=_params(("arbitrary", "arbitrary")),
        name="expert_ffn",
    )(xg_ctx, xg_lat, val_ctx, val_lat, w_gate, w_up, w_down)


def _scatter_ctx_kernel(idx_ref, y_ref, x1_ref, g2_ref, out_ref):
    onehot = _ctx_slot_onehot(idx_ref, False)
    y_hi, y_lo = _split_bf16(y_ref[...].reshape(CTX_SLOTS, D_MODEL))
    moe = jnp.dot(onehot, y_hi, preferred_element_type=F32) + jnp.dot(onehot, y_lo, preferred_element_type=F32)
    out_ref[...] = x1_ref[...] + g2_ref[...] * moe


def _scatter_ctx(idx_c, yg, x1, mods, layer):
    return pl.pallas_call(
        _scatter_ctx_kernel,
        grid=(BATCH,),
        in_specs=[pl.BlockSpec((N_EXPERTS, CAP_CTX, 1), lambda b: (b, 0, 0)),
                  pl.BlockSpec((N_EXPERTS, CAP_CTX, D_MODEL), lambda b: (0, b, 0)),
                  pl.BlockSpec((SEQ, D_MODEL), lambda b: (b, 0)),
                  _mod_spec(layer, MOD_G2, lambda b: 0)],
        out_specs=pl.BlockSpec((SEQ, D_MODEL), lambda b: (b, 0)),
        out_shape=jax.ShapeDtypeStruct((T_CTX, D_MODEL), F32),
        compiler_params=_params(("arbitrary",)),
        name="scatter_ctx",
    )(idx_c, yg, x1, mods)


def _scatter_lat_kernel(idx_ref, y_ref, x1_ref, g2_ref, out_ref):
    e = pl.program_id(2)

    @pl.when(e == 0)
    def _():
        out_ref[...] = jnp.zeros_like(out_ref)

    base = (pl.program_id(0) * N_EXPERTS + e) * CAP_LAT

    def body(it, _):
        r0 = pl.multiple_of(it * SLOT_GROUP, SLOT_GROUP)
        rows = [idx_ref[base + r0 + k] for k in range(SLOT_GROUP)]
        old = [out_ref[0, pl.ds(rows[k], 1), :] for k in range(SLOT_GROUP)]
        y = y_ref[0, pl.ds(r0, SLOT_GROUP), :]
        for k in range(SLOT_GROUP):
            out_ref[0, pl.ds(rows[k], 1), :] = old[k] + y[k:k + 1, :]
        return 0

    lax.fori_loop(0, CAP_LAT // SLOT_GROUP, body, 0)

    @pl.when(e == N_EXPERTS - 1)
    def _():
        out_ref[0] = x1_ref[0] + g2_ref[...] * out_ref[0]


def _scatter_lat(idx_flat, yg, x1_3, mods, layer, off):
    blk0 = BATCH * CAP_CTX // CAP_LAT
    return pl.pallas_call(
        _scatter_lat_kernel,
        grid_spec=pltpu.PrefetchScalarGridSpec(
            num_scalar_prefetch=1,
            grid=(DEC_BATCH, 1, N_EXPERTS),
            in_specs=[pl.BlockSpec((1, CAP_LAT, D_MODEL), lambda b, h, e, idx: (e, blk0 + b, 0)),
                      pl.BlockSpec((1, DEC_SEQ, D_MODEL), lambda b, h, e, idx: (off + b, 0, 0),
                                   pipeline_mode=pl.Buffered(1)),
                      _mod_spec(layer, MOD_G2, lambda b, h, e, idx: off + b)],
            out_specs=pl.BlockSpec((1, DEC_SEQ, D_MODEL), lambda b, h, e, idx: (b, 0, 0),
                                   pipeline_mode=pl.Buffered(1)),
        ),
        out_shape=jax.ShapeDtypeStruct((DEC_BATCH, DEC_SEQ, D_MODEL), F32),
        compiler_params=_params(("arbitrary", "arbitrary", "arbitrary")),
        name="scatter_lat",
    )(idx_flat, yg, x1_3, mods)


def _rope_tables():
    pos = np.arange(DEC_SEQ)
    freq = (np.float32(ROPE_THETA) ** (-np.arange(ROPE_FREQS, dtype=np.float32) / np.float32(ROPE_FREQS)))
    ang_r = (pos // GRID_W).astype(np.float32)[:, None] * freq.astype(np.float32)
    ang_c = (pos % GRID_W).astype(np.float32)[:, None] * freq.astype(np.float32)
    cos = np.concatenate([np.cos(ang_r)] * 2 + [np.cos(ang_c)] * 2, axis=-1)
    sin = np.concatenate([-np.sin(ang_r), np.sin(ang_r), -np.sin(ang_c), np.sin(ang_c)], axis=-1)
    reps = LANE // HEAD_DIM
    cs = np.concatenate([np.ones((ROW_TILE, LANE)), np.tile(cos, (1, reps))], axis=0).astype(np.float32)
    sn = np.concatenate([np.zeros((ROW_TILE, LANE)), np.tile(sin, (1, reps))], axis=0).astype(np.float32)
    return cs, sn


def _rope_tile(i):
    lat = jnp.maximum(i - N_CTX_TILES, 0) % (DEC_SEQ // ROW_TILE)
    return jnp.where(i < N_CTX_TILES, 0, 1 + lat)


def _qk_gain(q_norm, k_norm):
    q = jnp.tile(q_norm, N_HEADS) * (HEAD_DIM ** -0.5 * LOG2_E)
    return jnp.concatenate([q, jnp.tile(k_norm, N_KV)])[None, :]


def kernel(x_prompt, x_sample, cache_a_k, cache_a_v, cache_c_k, cache_c_v, c, c_ctx, norm1_g, w_mod, b_mod, w_in,
           a_q_norm, a_k_norm, a_sink, b_v_norm, b_ws, b_bs, c_q_norm, c_k_norm, w_a_o, w_b_o, w_c_o, w_out, norm2_g,
           w_router, b_router, w_gate, w_up, w_down):
    cond8 = jnp.concatenate([c_ctx[None, :], c, jnp.zeros((8 - N_REQ, D_MODEL), F32)], axis=0)
    mods = _modulation(cond8, w_mod, b_mod).reshape(DEPTH, 8, 1, 6 * D_MODEL)

    cs, sn = _rope_tables()
    w_in_b = w_in.astype(BF16)
    wa_b, wb_b, wc_b, wo_b = w_a_o.astype(BF16), w_b_o.astype(BF16), w_c_o.astype(BF16), w_out.astype(BF16)
    ws_b = b_ws.astype(BF16)
    wr_pad = jnp.pad(w_router, ((0, 0), (0, 0), (0, LANE - N_EXPERTS))).astype(BF16)
    br_pad = jnp.pad(b_router, ((0, 0), (0, LANE - N_EXPERTS)), constant_values=NEG_BIG)

    by_seq = lambda a: a.reshape(T_ALL // SEQ, SEQ, a.shape[-1])
    by_dec = lambda a: a.reshape(T_ALL // DEC_SEQ, DEC_SEQ, a.shape[-1])
    lat_off = T_CTX // DEC_SEQ

    caches = [a.reshape(DEC_BATCH, DEPTH, PAST_LEN, KV_W).astype(BF16)
              for a in (cache_a_k, cache_a_v, cache_c_k, cache_c_v)]

    x_ctx = x_prompt.reshape(T_CTX, D_MODEL)
    x_lat = x_sample.reshape(T_LAT, D_MODEL)
    new_kv = [[], [], [], []]
    for l in range(DEPTH):
        qa, ka_b, va_b, nka, nva, bu, bv, qc, kc_b, vc_b, nkc, nvc, gt = _input_projection(
            x_ctx, x_lat, mods, l, norm1_g[l][None, :], w_in_b[l], cs, sn,
            _qk_gain(a_q_norm[l], a_k_norm[l]), _qk_gain(c_q_norm[l], c_k_norm[l]), b_v_norm[l][None, :])
        for lst, arr in zip(new_kv, (nka, nva, nkc, nvc)):
            lst.append(arr[:T_CTX].reshape(BATCH, SEQ, N_KV, HEAD_DIM))

        sink = a_sink[l]
        oa_ctx = _dense_attention(by_seq(qa), by_seq(ka_b), by_seq(va_b), None, sink,
                                  n_req=BATCH, off=0, tq=SEQ, key_chunk=SEQ)
        oc_ctx = _dense_attention(by_seq(qc), by_seq(kc_b), by_seq(vc_b), None, None,
                                  n_req=BATCH, off=0, tq=SEQ, key_chunk=SEQ)
        cak, cav, cck, ccv = (a[:, l] for a in caches)
        oa_lat = _window_attention(by_dec(qa), by_dec(ka_b), by_dec(va_b), cak, cav, sink,
                                   n_req=DEC_BATCH, off=lat_off)
        oc_lat = _dense_attention(by_dec(qc), by_dec(kc_b), by_dec(vc_b), (cck, ccv), None,
                                  n_req=DEC_BATCH, off=lat_off, tq=1024, key_chunk=512)

        bs_full = jnp.repeat(b_bs[l].T, B_GROUP_CH, axis=1)
        x1, h2p, afft = _merge(x_ctx, x_lat, oa_ctx.reshape(T_CTX, Q_W), oa_lat.reshape(T_LAT, Q_W), bu, bv,
                               oc_ctx.reshape(T_CTX, Q_W), oc_lat.reshape(T_LAT, Q_W), gt,
                               wa_b[l], wb_b[l], wc_b[l], wo_b[l], ws_b[l], bs_full,
                               mods, l, norm2_g[l][None, :], wr_pad[l], br_pad[l][None, :])

        aff_rows = lambda a, n_req, n: a.reshape(N_EXPERTS, n_req, n).transpose(1, 0, 2).reshape(n_req * N_EXPERTS, n)
        idx_c, val_c = _select(aff_rows(afft[:, :T_CTX], BATCH, SEQ), BATCH * N_EXPERTS, CAP_CTX)
        _, val_l, idx_l_rows = _select(aff_rows(afft[:, T_CTX:], DEC_BATCH, DEC_SEQ), N_EXPERTS, CAP_LAT)
        idx_l_flat = idx_l_rows.reshape(-1)
        xg_ctx = _gather_ctx(idx_c, h2p)
        xg_lat = _gather_lat(idx_l_flat, by_dec(h2p), lat_off)
        yg = _expert_ffn(xg_ctx, xg_lat, val_c.reshape(BATCH, N_EXPERTS, CAP_CTX, 1),
                         val_l.reshape(DEC_BATCH, N_EXPERTS, CAP_LAT, 1), w_gate, w_up, w_down, l)

        x_ctx = _scatter_ctx(idx_c, yg, x1, mods, l)
        x_lat = _scatter_lat(idx_l_flat, yg, by_dec(x1), mods, l, lat_off).reshape(T_LAT, D_MODEL)

    y_prompt = x_ctx.reshape(BATCH, SEQ, D_MODEL)
    y_sample = x_lat.reshape(DEC_BATCH, DEC_SEQ, D_MODEL)
    return (y_prompt, y_sample) + tuple(jnp.stack(lst, axis=1) for lst in new_kv)
```

```python
import functools

import jax
import numpy as np
import jax.numpy as jnp
from jax import lax
from jax.experimental import pallas as pl
from jax.experimental.pallas import tpu as pltpu

F32 = jnp.float32
BF16 = jnp.bfloat16
I32 = jnp.int32

D_MODEL = 1024
BATCH = 16
SEQ = 256
DEPTH = 2
DEC_BATCH = 2
DEC_SEQ = 4096
PAST_LEN = 256
GRID_W = 64
HEAD_DIM = 64
N_HEADS = 6
N_KV = 2
N_GRP = N_HEADS // N_KV
B_GROUPS = 4
B_GROUP_CH = 64
B_WIDTH = B_GROUPS * B_GROUP_CH
Q_W = N_HEADS * HEAD_DIM
KV_W = N_KV * HEAD_DIM
QK_W = Q_W + KV_W
N_BRANCH = 3
WINDOW = 128
BLOCK = 128
CHUNK = 128
N_EXPERTS = 16
EXPERT_FF = 1024
CAP_FACTOR = 2
ROPE_THETA = 10000.0
ROPE_FREQS = HEAD_DIM // 4
EPS = 1e-6
IN_WIDTH = 2 * (QK_W + KV_W) + 2 * B_WIDTH + N_BRANCH * D_MODEL

T_CTX = BATCH * SEQ
T_LAT = DEC_BATCH * DEC_SEQ
T_ALL = T_CTX + T_LAT
N_REQ = 1 + DEC_BATCH
CAP_CTX = CAP_FACTOR * SEQ // N_EXPERTS
CAP_LAT = CAP_FACTOR * DEC_SEQ // N_EXPERTS
ROWS_PER_EXPERT = BATCH * CAP_CTX + DEC_BATCH * CAP_LAT

LANE = 128
ROW_TILE = 512
N_CTX_TILES = T_CTX // ROW_TILE
FFN_ROW_TILE = 512
VMEM_LIMIT = 56 * 1024 * 1024
NEG_BIG = -1e30
LOG2_E = 1.4426950408889634

OFF_A = 0
OFF_AV = OFF_A + QK_W
OFF_BU = OFF_AV + KV_W
OFF_BV = OFF_BU + B_WIDTH
OFF_C = OFF_BV + B_WIDTH
OFF_CV = OFF_C + QK_W
OFF_G = OFF_CV + KV_W


def _params(sem, vmem=VMEM_LIMIT):
    return pltpu.CompilerParams(dimension_semantics=sem, vmem_limit_bytes=vmem)


def _sigmoid(x):
    return 1.0 / (1.0 + jnp.exp(-x))


def _gelu_tanh(x):
    return 0.5 * x * (1.0 + jnp.tanh(0.7978845608028654 * (x + 0.044715 * (x * x * x))))


def _split_bf16(x):
    hi = x.astype(BF16)
    lo = (x - hi.astype(F32)).astype(BF16)
    return hi, lo


def _mod_kernel(c_ref, w_ref, b_ref, o_ref):
    c = c_ref[...]
    s_hi, s_lo = _split_bf16(c * _sigmoid(c))
    w_hi, w_lo = _split_bf16(w_ref[0])
    acc = jnp.dot(s_hi, w_hi, preferred_element_type=F32)
    acc += jnp.dot(s_lo, w_hi, preferred_element_type=F32)
    acc += jnp.dot(s_hi, w_lo, preferred_element_type=F32)
    o_ref[0] = acc + b_ref[0]


def _modulation(cond8, w_mod, b_mod):
    n_col = 6 * D_MODEL // D_MODEL
    return pl.pallas_call(
        _mod_kernel,
        grid=(DEPTH, n_col),
        in_specs=[
            pl.BlockSpec((8, D_MODEL), lambda l, j: (0, 0)),
            pl.BlockSpec((1, D_MODEL, D_MODEL), lambda l, j: (l, 0, j)),
            pl.BlockSpec((1, 1, D_MODEL), lambda l, j: (l, 0, j)),
        ],
        out_specs=pl.BlockSpec((1, 8, D_MODEL), lambda l, j: (l, 0, j)),
        out_shape=jax.ShapeDtypeStruct((DEPTH, 8, 6 * D_MODEL), F32),
        compiler_params=_params(("arbitrary", "arbitrary")),
        name="modulation",
    )(cond8, w_mod, b_mod.reshape(DEPTH, 1, 6 * D_MODEL))


def _group_sumsq(y, bd_ref):
    return jnp.dot((y * y).astype(BF16), bd_ref[...], preferred_element_type=F32)


def _pick_pass(i, ctx_ref, lat_ref):
    return jnp.where(i < N_CTX_TILES, ctx_ref[...], lat_ref[...])


def _in_kernel(xc_ref, xl_ref, sc_ref, sh_ref, n1_ref, w_ref, cs_ref, sn_ref, ga_ref, gc_ref, gbv_ref, bd_qk_ref,
               bd_b_ref, qa_ref, ka_ref, va_ref, nka_ref, nva_ref, bu_ref, bv_ref, qc_ref, kc_ref, vc_ref, nkc_ref,
               nvc_ref, gt_ref):
    x = _pick_pass(pl.program_id(0), xc_ref, xl_ref)
    ms = jnp.mean(x * x, axis=-1, keepdims=True)
    h = x * lax.rsqrt(ms + EPS) * n1_ref[...]
    h = h * (1.0 + sc_ref[...]) + sh_ref[...]
    hb = h.astype(BF16)
    tm = x.shape[0]

    def proj(c0, width):
        return jnp.dot(hb, w_ref[:, c0:c0 + width], preferred_element_type=F32)

    cs = jnp.concatenate([cs_ref[...]] * (QK_W // LANE), axis=1)
    sn = jnp.concatenate([sn_ref[...]] * (QK_W // LANE), axis=1)
    lane = lax.broadcasted_iota(I32, (tm, QK_W), 1)
    first_half = (lane & ROPE_FREQS) == 0

    def qk_post(y, gain_ref):
        yn = y * lax.rsqrt(_group_sumsq(y, bd_qk_ref) * (1.0 / HEAD_DIM) + EPS) * gain_ref[...]
        partner = jnp.where(first_half, pltpu.roll(yn, QK_W - ROPE_FREQS, 1), pltpu.roll(yn, ROPE_FREQS, 1))
        return yn * cs + partner * sn

    def mixer(off_qk, off_v, gain_ref, q_ref, k_ref, v_ref, nk_ref, nv_ref):
        y = qk_post(proj(off_qk, QK_W), gain_ref)
        v = proj(off_v, KV_W)
        q_ref[...] = y[:, :Q_W].astype(BF16)
        k_ref[...] = y[:, Q_W:].astype(BF16)
        v_ref[...] = v.astype(BF16)

        nk_ref[...] = y[:, Q_W:]
        nv_ref[...] = v

    mixer(OFF_A, OFF_AV, ga_ref, qa_ref, ka_ref, va_ref, nka_ref, nva_ref)

    bu_ref[...] = _gelu_tanh(proj(OFF_BU, B_WIDTH)).astype(BF16)
    gv = _gelu_tanh(proj(OFF_BV, B_WIDTH))
    gvn = gv * lax.rsqrt(_group_sumsq(gv, bd_b_ref) * (1.0 / B_GROUP_CH) + EPS) * gbv_ref[...]
    bv_ref[...] = gvn.astype(BF16)

    mixer(OFF_C, OFF_CV, gc_ref, qc_ref, kc_ref, vc_ref, nkc_ref, nvc_ref)

    gate_chunk = 512
    for j in range(N_BRANCH * D_MODEL // gate_chunk):
        g = proj(OFF_G + j * gate_chunk, gate_chunk)
        gt_ref[:, j * gate_chunk:(j + 1) * gate_chunk] = _sigmoid(g).astype(BF16)


def _req_of_tile(i):
    return i // N_CTX_TILES


def _ctx_rows(w):
    return pl.BlockSpec((ROW_TILE, w), lambda i: (jnp.minimum(i, N_CTX_TILES - 1), 0))


def _lat_rows(w):
    return pl.BlockSpec((ROW_TILE, w), lambda i: (jnp.maximum(i - N_CTX_TILES, 0), 0))


MOD_SH1, MOD_SC1, MOD_G1, MOD_SH2, MOD_SC2, MOD_G2 = range(6)


def _mod_spec(layer, chunk, req):
    return pl.BlockSpec((None, None, 1, D_MODEL), lambda *g: (layer, req(*g), 0, chunk))


def _block_diag_ones(width, group):
    g = np.arange(width) // group
    return (g[:, None] == g[None, :]).astype(np.float32)


def _input_projection(x_ctx, x_lat, mods, layer, n1, w_in_b, cs, sn, gain_a, gain_c, gain_bv):
    bd_qk = jnp.asarray(_block_diag_ones(QK_W, HEAD_DIM), BF16)
    bd_b = jnp.asarray(_block_diag_ones(B_WIDTH, B_GROUP_CH), BF16)
    tm = ROW_TILE
    row = lambda w: pl.BlockSpec((tm, w), lambda i: (i, 0))
    full = lambda a: pl.BlockSpec(a.shape, lambda i: (0,) * a.ndim)
    rope = pl.BlockSpec((tm, LANE), lambda i: (_rope_tile(i), 0))
    cache_rows = T_CTX + tm
    spare = lambda w: pl.BlockSpec((tm, w), lambda i: (jnp.minimum(i, N_CTX_TILES), 0))
    mixer_outs = [(Q_W, BF16, T_ALL), (KV_W, BF16, T_ALL), (KV_W, BF16, T_ALL), (KV_W, F32, cache_rows),
                  (KV_W, F32, cache_rows)]
    outs = mixer_outs + [(B_WIDTH, BF16, T_ALL), (B_WIDTH, BF16, T_ALL)] + mixer_outs + [(N_BRANCH * D_MODEL, BF16, T_ALL)]
    return pl.pallas_call(
        _in_kernel,
        grid=(T_ALL // tm,),
        in_specs=[_ctx_rows(D_MODEL), _lat_rows(D_MODEL), _mod_spec(layer, MOD_SC1, _req_of_tile),
                  _mod_spec(layer, MOD_SH1, _req_of_tile), full(n1), full(w_in_b), rope, rope,
                  full(gain_a), full(gain_c), full(gain_bv), full(bd_qk), full(bd_b)],
        out_specs=[row(w) if rows == T_ALL else spare(w) for w, _, rows in outs],
        out_shape=[jax.ShapeDtypeStruct((rows, w), dt) for w, dt, rows in outs],
        compiler_params=_params(("arbitrary",)),
        name="input_projection",
    )(x_ctx, x_lat, mods, mods, n1, w_in_b, cs, sn, gain_a, gain_c, gain_bv, bd_qk, bd_b)


def _attention_tile(q_ref, sources, sink_ref, o_ref, qt_scr, ot_scr, *, tq, key_chunk):
    width = N_GRP * tq
    for j in range(Q_W // LANE):
        qt_scr[j * LANE:(j + 1) * LANE, :] = q_ref[0, :, j * LANE:(j + 1) * LANE].astype(F32).T.astype(BF16)
    for kv in range(N_KV):
        lo, hi = kv * HEAD_DIM, (kv + 1) * HEAD_DIM
        heads = [kv * N_GRP + g for g in range(N_GRP)]
        qt = jnp.concatenate([qt_scr[h * HEAD_DIM:(h + 1) * HEAD_DIM, :] for h in heads], axis=1)

        def step(carry, kref, vref, c0, size, bias):
            m, acc = carry
            s = jnp.dot(kref[0, pl.ds(c0, size), lo:hi], qt, preferred_element_type=F32)
            if bias is not None:
                s = s + jnp.concatenate([bias] * N_GRP, axis=1)
            vt = vref[0, pl.ds(c0, size), :].astype(F32).T[lo:hi, :].astype(BF16)
            vt = jnp.concatenate([vt, jnp.ones((DEN_ROWS, size), BF16)], axis=0)
            m_new = jnp.maximum(m, jnp.max(s, axis=0, keepdims=True))
            p = jnp.exp2(s - m_new).astype(BF16)
            acc = acc * jnp.exp2(m - m_new) + jnp.dot(vt, p, preferred_element_type=F32)
            return m_new, acc

        if sink_ref is not None:
            m0 = jnp.concatenate([jnp.full((1, tq), sink_ref[h] * LOG2_E, F32) for h in heads], axis=1)
            den0 = jnp.ones((DEN_ROWS, width), F32)
        else:
            m0 = jnp.full((1, width), NEG_BIG, F32)
            den0 = jnp.zeros((DEN_ROWS, width), F32)
        carry = (m0, jnp.concatenate([jnp.zeros((HEAD_DIM, width), F32), den0], axis=0))
        for kref, vref, bias in sources:
            n_rows = kref.shape[1]
            n_full = n_rows // key_chunk
            if bias is not None:
                carry = step(carry, kref, vref, 0, n_rows, bias)
                continue
            if n_full > 1:
                carry = lax.fori_loop(
                    0, n_full,
                    lambda c, cr: step(cr, kref, vref, pl.multiple_of(c * key_chunk, key_chunk), key_chunk, None), carry)
            elif n_full == 1:
                carry = step(carry, kref, vref, 0, key_chunk, None)
            if n_rows - n_full * key_chunk:
                carry = step(carry, kref, vref, n_full * key_chunk, n_rows - n_full * key_chunk, None)
        _, acc = carry
        o = acc[:HEAD_DIM] / acc[HEAD_DIM:HEAD_DIM + 1]
        for g, h in enumerate(heads):
            ot_scr[h * HEAD_DIM:(h + 1) * HEAD_DIM, :] = o[:, g * tq:(g + 1) * tq]
    for j in range(Q_W // LANE):
        o_ref[0, :, j * LANE:(j + 1) * LANE] = ot_scr[j * LANE:(j + 1) * LANE, :].T.astype(o_ref.dtype)


def _dense_attn_kernel(*refs, tq, key_chunk, has_extra, has_sink):
    refs = list(refs)
    q_ref, k_ref, v_ref = refs[:3]
    del refs[:3]
    sources = [(k_ref, v_ref, None)]
    if has_extra:
        sources.append((refs.pop(0), refs.pop(0), None))
    sink_ref = refs.pop(0) if has_sink else None
    o_ref, qt_scr, ot_scr = refs
    _attention_tile(q_ref, sources, sink_ref, o_ref, qt_scr, ot_scr, tq=tq, key_chunk=key_chunk)


def _attention_scratch(tq):
    return [pltpu.VMEM((Q_W, tq), BF16), pltpu.VMEM((Q_W, tq), F32)]


def _dense_attention(q, k, v, extra, sink, *, n_req, off, tq, key_chunk):
    s = q.shape[1]
    kv_spec = pl.BlockSpec((1, s, KV_W), lambda i, j: (off + i, 0, 0))
    in_specs = [pl.BlockSpec((1, tq, Q_W), lambda i, j: (off + i, j, 0)), kv_spec, kv_spec]
    args = [q, k, v]
    if extra is not None:
        in_specs += [pl.BlockSpec((1, extra[0].shape[1], KV_W), lambda i, j: (i, 0, 0))] * 2
        args += list(extra)
    if sink is not None:
        in_specs.append(pl.BlockSpec(memory_space=pltpu.SMEM))
        args.append(sink)
    return pl.pallas_call(
        functools.partial(_dense_attn_kernel, tq=tq, key_chunk=key_chunk, has_extra=extra is not None,
                          has_sink=sink is not None),
        grid=(n_req, s // tq),
        in_specs=in_specs,
        out_specs=pl.BlockSpec((1, tq, Q_W), lambda i, j: (i, j, 0)),
        out_shape=jax.ShapeDtypeStruct((n_req, s, Q_W), BF16),
        scratch_shapes=_attention_scratch(tq),
        compiler_params=_params(("arbitrary", "arbitrary")),
        name="dense_attention",
    )(*args)


DEN_ROWS = 16
WINDOW_TQ = 512


def _window_attn_kernel(q_ref, kp_ref, kc_ref, kn_ref, vp_ref, vc_ref, vn_ref, ck_ref, cv_ref, bp_ref, bc_ref, bn_ref,
                        sink_ref, o_ref, qt_scr, ot_scr, *, seq):
    q_pos0 = pl.program_id(1) * WINDOW_TQ
    prev_bias = bp_ref[...] + jnp.where(q_pos0 >= BLOCK, 0.0, NEG_BIG)
    next_bias = bn_ref[...] + jnp.where(q_pos0 + WINDOW_TQ < seq, 0.0, NEG_BIG)
    sources = [(kp_ref, vp_ref, prev_bias), (kc_ref, vc_ref, bc_ref[...]), (kn_ref, vn_ref, next_bias),
               (ck_ref, cv_ref, None)]
    _attention_tile(q_ref, sources, sink_ref, o_ref, qt_scr, ot_scr, tq=WINDOW_TQ, key_chunk=WINDOW_TQ)


def _band_bias(first_key, n_keys):
    d = (first_key + np.arange(n_keys))[:, None] - np.arange(WINDOW_TQ)[None, :]
    return np.where(np.abs(d) <= WINDOW, 0.0, NEG_BIG).astype(np.float32)


def _window_attention(q, k, v, ck, cv, sink, *, n_req, off):
    b, s = n_req, q.shape[1]
    nb = s // BLOCK
    per_tile = WINDOW_TQ // BLOCK
    edge = lambda f: pl.BlockSpec((1, BLOCK, KV_W), lambda i, j: (off + i, f(j), 0))
    prev = lambda j: jnp.maximum(j * per_tile - 1, 0)
    nxt = lambda j: jnp.minimum((j + 1) * per_tile, nb - 1)
    cur = pl.BlockSpec((1, WINDOW_TQ, KV_W), lambda i, j: (off + i, j, 0))
    ctx = pl.BlockSpec((1, PAST_LEN, KV_W), lambda i, j: (i, 0, 0))
    biases = [_band_bias(-BLOCK, BLOCK), _band_bias(0, WINDOW_TQ), _band_bias(WINDOW_TQ, BLOCK)]
    table = lambda a: pl.BlockSpec(a.shape, lambda i, j: (0, 0))
    return pl.pallas_call(
        functools.partial(_window_attn_kernel, seq=s),
        grid=(b, s // WINDOW_TQ),
        in_specs=[pl.BlockSpec((1, WINDOW_TQ, Q_W), lambda i, j: (off + i, j, 0)),
                  edge(prev), cur, edge(nxt), edge(prev), cur, edge(nxt), ctx, ctx,
                  table(biases[0]), table(biases[1]), table(biases[2]),
                  pl.BlockSpec(memory_space=pltpu.SMEM)],
        out_specs=pl.BlockSpec((1, WINDOW_TQ, Q_W), lambda i, j: (i, j, 0)),
        out_shape=jax.ShapeDtypeStruct((b, s, Q_W), BF16),
        scratch_shapes=_attention_scratch(WINDOW_TQ),
        compiler_params=_params(("arbitrary", "arbitrary")),
        name="window_attention",
    )(q, k, k, k, v, v, v, ck, cv, *biases, sink)


def _pack_halves(x):
    half = x.shape[1] // 2
    return pltpu.pack_elementwise([x[:, :half], x[:, half:]], packed_dtype=BF16)


def _unpack_halves(words):
    return tuple(pltpu.unpack_elementwise(words, index=i, packed_dtype=BF16, unpacked_dtype=F32).astype(BF16)
                 for i in range(2))


def _merge_kernel(xc_ref, xl_ref, oac_ref, oal_ref, bu_ref, bv_ref, occ_ref, ocl_ref, gt_ref, wa_ref, wb_ref, wc_ref,
                  wo_ref, ws_ref, bs_ref, g1_ref, sc2_ref, sh2_ref, n2_ref, wr_ref, br_ref, x1_ref, h2p_ref, afft_ref):
    i = pl.program_id(0)
    tm = xc_ref.shape[0]
    group = lax.broadcasted_iota(I32, (CHUNK, B_WIDTH), 1) // B_GROUP_CH
    obs = []
    for c in range(tm // CHUNK):
        v = bv_ref[c * CHUNK:(c + 1) * CHUNK, :]
        sv = jnp.zeros((CHUNK, B_WIDTH), F32)
        for g in range(B_GROUPS):
            sv = jnp.where(group == g, jnp.dot(ws_ref[g], v, preferred_element_type=F32), sv)
        u = bu_ref[c * CHUNK:(c + 1) * CHUNK, :].astype(F32)
        obs.append((u * (sv + bs_ref[...])).astype(BF16))
    ob = jnp.concatenate(obs, axis=0)

    oa = _pick_pass(i, oac_ref, oal_ref)
    oc = _pick_pass(i, occ_ref, ocl_ref)
    merged = gt_ref[:, 0:D_MODEL].astype(F32) * jnp.dot(oa, wa_ref[...], preferred_element_type=F32)
    merged += gt_ref[:, D_MODEL:2 * D_MODEL].astype(F32) * jnp.dot(ob, wb_ref[...], preferred_element_type=F32)
    merged += gt_ref[:, 2 * D_MODEL:3 * D_MODEL].astype(F32) * jnp.dot(oc, wc_ref[...], preferred_element_type=F32)
    y = jnp.dot(merged.astype(BF16), wo_ref[...], preferred_element_type=F32)
    x1 = _pick_pass(i, xc_ref, xl_ref) + g1_ref[...] * y
    x1_ref[...] = x1

    ms = jnp.mean(x1 * x1, axis=-1, keepdims=True)
    h2 = x1 * lax.rsqrt(ms + EPS) * n2_ref[...]
    h2 = h2 * (1.0 + sc2_ref[...]) + sh2_ref[...]
    h2p_ref[...] = _pack_halves(h2)

    logits = jnp.dot(h2.astype(BF16), wr_ref[...], preferred_element_type=F32) + br_ref[...]
    e = jnp.exp(logits - jnp.max(logits, axis=-1, keepdims=True))
    aff = e / jnp.sum(e, axis=-1, keepdims=True)
    afft_ref[...] = aff.T[:N_EXPERTS, :]


def _merge(x_ctx, x_lat, oa_ctx, oa_lat, bu, bv, oc_ctx, oc_lat, gt, wa, wb, wc, wo, ws, bs, mods, layer, n2, wr, br):
    tm = ROW_TILE
    row = lambda w: pl.BlockSpec((tm, w), lambda i: (i, 0))
    full = lambda a: pl.BlockSpec(a.shape, lambda i: (0,) * a.ndim)
    mod = lambda chunk: _mod_spec(layer, chunk, _req_of_tile)
    return pl.pallas_call(
        _merge_kernel,
        grid=(T_ALL // tm,),
        in_specs=[_ctx_rows(D_MODEL), _lat_rows(D_MODEL), _ctx_rows(Q_W), _lat_rows(Q_W), row(B_WIDTH), row(B_WIDTH),
                  _ctx_rows(Q_W), _lat_rows(Q_W), row(N_BRANCH * D_MODEL),
                  full(wa), full(wb), full(wc), full(wo), full(ws), full(bs),
                  mod(MOD_G1), mod(MOD_SC2), mod(MOD_SH2), full(n2), full(wr), full(br)],
        out_specs=[row(D_MODEL), row(D_MODEL // 2), pl.BlockSpec((N_EXPERTS, tm), lambda i: (0, i))],
        out_shape=[jax.ShapeDtypeStruct((T_ALL, D_MODEL), F32), jax.ShapeDtypeStruct((T_ALL, D_MODEL // 2), jnp.uint32),
                   jax.ShapeDtypeStruct((N_EXPERTS, T_ALL), F32)],
        compiler_params=_params(("arbitrary",)),
        name="merge_router",
    )(x_ctx, x_lat, oa_ctx, oa_lat, bu, bv, oc_ctx, oc_lat, gt, wa, wb, wc, wo, ws, bs, mods, mods, mods, n2, wr, br)


def _select_kernel(aff_ref, idx_ref, val_ref, *rest, n, cap, row_chunk):
    idx_row_ref = rest[0] if len(rest) == 4 else None
    possel_ref, idx_scr, val_scr = rest[-3:]
    a = aff_ref[...]
    rows = a.shape[0]
    tok = lax.broadcasted_iota(I32, (rows, n), 1)

    def count(ones):
        return jnp.sum(ones, axis=1, keepdims=True)

    def at_least(word):
        return jnp.where(a >= pltpu.bitcast(word, F32), 1, 0)

    thr = jnp.zeros((rows, 1), I32)
    for bit in range(30, -1, -1):
        cand = thr | (1 << bit)
        thr = jnp.where(count(at_least(cand)) >= cap, cand, thr)
    above = at_least(thr + 1)
    tied = at_least(thr) - above
    need = cap - count(above)
    last = jnp.zeros((rows, 1), I32)
    for bit in range(n.bit_length() - 2, -1, -1):
        cand = last | (1 << bit)
        last = jnp.where(count(jnp.where(tok < cand, tied, 0)) < need, cand, last)
    sel = above + jnp.where(tok <= last, tied, 0)

    blk = min(n, 256)
    tri = jnp.where(lax.broadcasted_iota(I32, (blk, blk), 0) <= lax.broadcasted_iota(I32, (blk, blk), 1),
                    1.0, 0.0).astype(BF16)
    sel_f = sel.astype(F32)
    offset = jnp.zeros((rows, 1), F32)
    for j in range(n // blk):
        s_blk = sel_f[:, j * blk:(j + 1) * blk]
        incl = jnp.dot(s_blk.astype(BF16), tri, preferred_element_type=F32)
        pos = (incl - s_blk + offset).astype(I32)
        possel_ref[:, j * blk:(j + 1) * blk] = jnp.where(sel[:, j * blk:(j + 1) * blk] > 0, pos, -1)
        offset = offset + incl[:, blk - 1:blk]

    tb = min(n, TOKEN_BLOCK)
    n_blk = n // tb

    def fold_lanes(x):
        acc = x[:, :LANE]
        for k in range(1, tb // LANE):
            acc = acc + x[:, k * LANE:(k + 1) * LANE]
        return acc

    def match(e, slot, t0):
        hit = possel_ref[pl.ds(e, 1), pl.ds(t0, tb)] == slot
        tok = t0 + lax.broadcasted_iota(I32, (1, tb), 1)
        return (fold_lanes(jnp.where(hit, tok, 0)),
                fold_lanes(jnp.where(hit, aff_ref[pl.ds(e, 1), pl.ds(t0, tb)], 0.0)))

    def per_row(e, _):
        ends, run = [], 0
        for j in range(n_blk - 1):
            run = run + jnp.sum(jnp.where(possel_ref[pl.ds(e, 1), j * tb:(j + 1) * tb] >= 0, 1, 0))
            ends.append(run)

        def per_chunk(c, _):
            r0 = pl.multiple_of(c * row_chunk, row_chunk)
            slot = lax.broadcasted_iota(I32, (row_chunk, 1), 0) + r0
            if n_blk == 1:
                idx, val = match(e, slot, 0)
            else:
                first = sum(jnp.where(end <= r0, 1, 0) for end in ends)
                last = 1 + sum(jnp.where(end < r0 + row_chunk, 1, 0) for end in ends)

                def per_block(j, acc):
                    i, v = match(e, slot, pl.multiple_of(j * tb, tb))
                    return acc[0] + i, acc[1] + v

                idx, val = lax.fori_loop(first, last, per_block,
                                         (jnp.zeros((row_chunk, LANE), I32), jnp.zeros((row_chunk, LANE), F32)))
            idx_scr[pl.ds(r0, row_chunk), :] = idx
            val_scr[pl.ds(r0, row_chunk), :] = val
            return 0

        lax.fori_loop(0, cap // row_chunk, per_chunk, 0)
        idx = jnp.sum(idx_scr[...], axis=1, keepdims=True)
        idx_ref[e] = idx
        val_ref[e] = jnp.sum(val_scr[...], axis=1, keepdims=True)
        if idx_row_ref is not None:
            idx_row_ref[pl.ds(e, 1), :] = jnp.broadcast_to(idx.astype(F32), (cap, LANE)).T[0:1, :].astype(I32)
        return 0

    def per_small_row(e, _):
        idx, val = match(e, lax.broadcasted_iota(I32, (cap, 1), 0), 0)
        idx_ref[e] = jnp.sum(idx, axis=1, keepdims=True)
        val_ref[e] = jnp.sum(val, axis=1, keepdims=True)
        return 0

    if n_blk == 1 and cap == row_chunk:
        lax.fori_loop(0, rows, per_small_row, 0, unroll=4)
    else:
        lax.fori_loop(0, rows, per_row, 0)


def _select(aff_rows, rows_per_step, cap):
    r, n = aff_rows.shape
    row_chunk = min(cap, 64)
    out_specs = [pl.BlockSpec((rows_per_step, cap, 1), lambda s: (s, 0, 0))] * 2
    out_shape = [jax.ShapeDtypeStruct((r, cap, 1), I32), jax.ShapeDtypeStruct((r, cap, 1), F32)]
    if cap % LANE == 0:
        out_specs.append(pl.BlockSpec((rows_per_step, cap), lambda s: (s, 0)))
        out_shape.append(jax.ShapeDtypeStruct((r, cap), I32))
    return pl.pallas_call(
        functools.partial(_select_kernel, n=n, cap=cap, row_chunk=row_chunk),
        grid=(r // rows_per_step,),
        in_specs=[pl.BlockSpec((rows_per_step, n), lambda s: (s, 0))],
        out_specs=out_specs,
        out_shape=out_shape,
        scratch_shapes=[pltpu.VMEM((rows_per_step, n), I32), pltpu.VMEM((cap, LANE), I32), pltpu.VMEM((cap, LANE), F32)],
        compiler_params=_params(("arbitrary",)),
        name="expert_select",
    )(aff_rows)


CTX_SLOTS = N_EXPERTS * CAP_CTX
TOKEN_BLOCK = 512
SLOT_GROUP = 16


def _ctx_slot_onehot(idx_ref, slots_on_rows):
    idx = idx_ref[...].reshape(CTX_SLOTS, 1)
    if slots_on_rows:
        hit = idx == lax.broadcasted_iota(I32, (CTX_SLOTS, SEQ), 1)
    else:
        idx_lane = jnp.broadcast_to(idx.astype(F32), (CTX_SLOTS, LANE)).T[0:1, :]
        hit = idx_lane == lax.broadcasted_iota(I32, (SEQ, CTX_SLOTS), 0).astype(F32)
    return jnp.where(hit, 1.0, 0.0).astype(BF16)


def _gather_ctx_kernel(idx_ref, h_ref, out_ref):
    onehot = _ctx_slot_onehot(idx_ref, True)
    lo, hi = _unpack_halves(h_ref[...])
    g_lo = jnp.dot(onehot, lo, preferred_element_type=F32)
    g_hi = jnp.dot(onehot, hi, preferred_element_type=F32)
    packed = pltpu.pack_elementwise([g_lo, g_hi], packed_dtype=BF16)
    out_ref[...] = packed.reshape(N_EXPERTS, CAP_CTX, D_MODEL // 2)


def _gather_ctx(idx_c, h2p):
    return pl.pallas_call(
        _gather_ctx_kernel,
        grid=(BATCH,),
        in_specs=[pl.BlockSpec((N_EXPERTS, CAP_CTX, 1), lambda b: (b, 0, 0)),
                  pl.BlockSpec((SEQ, D_MODEL // 2), lambda b: (b, 0))],
        out_specs=pl.BlockSpec((N_EXPERTS, CAP_CTX, D_MODEL // 2), lambda b: (0, b, 0)),
        out_shape=jax.ShapeDtypeStruct((N_EXPERTS, BATCH * CAP_CTX, D_MODEL // 2), jnp.uint32),
        compiler_params=_params(("arbitrary",)),
        name="gather_ctx",
    )(idx_c, h2p)


def _gather_lat_kernel(idx_ref, src_ref, out_ref):
    base = (pl.program_id(0) * N_EXPERTS + pl.program_id(1)) * CAP_LAT

    def body(it, _):
        r0 = pl.multiple_of(it * SLOT_GROUP, SLOT_GROUP)
        picked = [src_ref[0, pl.ds(idx_ref[base + r0 + k], 1), :] for k in range(SLOT_GROUP)]
        dst = out_ref.at[0, pl.ds(r0, SLOT_GROUP)]
        for k in range(SLOT_GROUP):
            dst[k:k + 1, :] = picked[k]
        return 0

    lax.fori_loop(0, CAP_LAT // SLOT_GROUP, body, 0)


def _gather_lat(idx_flat, h2p3, off):
    return pl.pallas_call(
        _gather_lat_kernel,
        grid_spec=pltpu.PrefetchScalarGridSpec(
            num_scalar_prefetch=1,
            grid=(DEC_BATCH, N_EXPERTS),
            in_specs=[pl.BlockSpec((1, DEC_SEQ, D_MODEL // 2), lambda b, e, idx: (off + b, 0, 0))],
            out_specs=pl.BlockSpec((1, CAP_LAT, D_MODEL // 2), lambda b, e, idx: (e, b, 0)),
        ),
        out_shape=jax.ShapeDtypeStruct((N_EXPERTS, DEC_BATCH * CAP_LAT, D_MODEL // 2), jnp.uint32),
        compiler_params=_params(("arbitrary", "arbitrary")),
        name="gather_lat",
    )(idx_flat, h2p3)


N_CTX_FFN_TILES = BATCH * CAP_CTX // FFN_ROW_TILE


def _ffn_kernel(xc_ref, xl_ref, vc_ref, vl_ref, g2_ref, wg_ref, wu_ref, wd_ref, o_ref, wg_b, wu_b, wd_b):
    j = pl.program_id(1)

    @pl.when(j == 0)
    def _():
        wg_b[...] = wg_ref[0].astype(BF16)
        wu_b[...] = wu_ref[0].astype(BF16)
        wd_b[...] = wd_ref[0].astype(BF16)

    is_ctx = j < N_CTX_FFN_TILES
    x = jnp.where(is_ctx, jnp.concatenate(_unpack_halves(xc_ref[0]), axis=1),
                  jnp.concatenate(_unpack_halves(xl_ref[0]), axis=1))
    g = jnp.dot(x, wg_b[...], preferred_element_type=F32)
    u = jnp.dot(x, wu_b[...], preferred_element_type=F32)
    hh = (g * _sigmoid(g)) * u
    y = jnp.dot(hh.astype(BF16), wd_b[...], preferred_element_type=F32)
    o_ref[0] = (y * jnp.where(is_ctx, vc_ref[...].reshape(FFN_ROW_TILE, 1), vl_ref[...])) * g2_ref[...]


def _expert_ffn(xg_ctx, xg_lat, val_ctx, val_lat, mods, w_gate, w_up, w_down, layer):
    tr = FFN_ROW_TILE
    assert tr == BATCH * CAP_CTX == CAP_LAT
    n_tiles = ROWS_PER_EXPERT // tr
    def wspec(k, n, tiles_held):
        ahead = lambda e, j: jnp.minimum(e + jnp.where(j >= tiles_held, 1, 0), N_EXPERTS - 1)
        return pl.BlockSpec((None, 1, k, n), lambda e, j: (layer, ahead(e, j), 0, 0))

    ctx_tile = lambda j: jnp.minimum(j, N_CTX_FFN_TILES - 1)
    lat_tile = lambda j: jnp.maximum(j - N_CTX_FFN_TILES, 0)
    return pl.pallas_call(
        _ffn_kernel,
        grid=(N_EXPERTS, n_tiles),
        in_specs=[pl.BlockSpec((1, tr, D_MODEL // 2), lambda e, j: (e, ctx_tile(j), 0)),
                  pl.BlockSpec((1, tr, D_MODEL // 2), lambda e, j: (e, lat_tile(j), 0)),
                  pl.BlockSpec((BATCH, None, CAP_CTX, 1), lambda e, j: (0, e, 0, 0)),
                  pl.BlockSpec((None, None, CAP_LAT, 1), lambda e, j: (lat_tile(j), e, 0, 0)),
                  _mod_spec(layer, MOD_G2, lambda e, j: j),
                  wspec(D_MODEL, EXPERT_FF, 1), wspec(D_MODEL, EXPERT_FF, n_tiles - 1),
                  wspec(EXPERT_FF, D_MODEL, n_tiles)],
        out_specs=pl.BlockSpec((1, tr, D_MODEL), lambda e, j: (e, j, 0)),
        out_shape=jax.ShapeDtypeStruct((N_EXPERTS, ROWS_PER_EXPERT, D_MODEL), F32),
        scratch_shapes=[pltpu.VMEM((D_MODEL, EXPERT_FF), BF16), pltpu.VMEM((D_MODEL, EXPERT_FF), BF16),
                        pltpu.VMEM((EXPERT_FF, D_MODEL), BF16)],
        compiler_params=_params(("arbitrary", "arbitrary")),
        name="expert_ffn",
    )(xg_ctx, xg_lat, val_ctx, val_lat, mods, w_gate, w_up, w_down)


def _scatter_ctx_kernel(idx_ref, y_ref, x1_ref, out_ref):
    onehot = _ctx_slot_onehot(idx_ref, False)
    y_hi, y_lo = _split_bf16(y_ref[...].reshape(CTX_SLOTS, D_MODEL))
    moe = jnp.dot(onehot, y_hi, preferred_element_type=F32) + jnp.dot(onehot, y_lo, preferred_element_type=F32)
    out_ref[...] = x1_ref[...] + moe


def _scatter_ctx(idx_c, yg, x1):
    return pl.pallas_call(
        _scatter_ctx_kernel,
        grid=(BATCH,),
        in_specs=[pl.BlockSpec((N_EXPERTS, CAP_CTX, 1), lambda b: (b, 0, 0)),
                  pl.BlockSpec((N_EXPERTS, CAP_CTX, D_MODEL), lambda b: (0, b, 0)),
                  pl.BlockSpec((SEQ, D_MODEL), lambda b: (b, 0))],
        out_specs=pl.BlockSpec((SEQ, D_MODEL), lambda b: (b, 0)),
        out_shape=jax.ShapeDtypeStruct((T_CTX, D_MODEL), F32),
        compiler_params=_params(("arbitrary",)),
        name="scatter_ctx",
    )(idx_c, yg, x1)


def _scatter_lat_kernel(idx_ref, y_ref, x1_ref, out_ref):
    e = pl.program_id(2)

    @pl.when(e == 0)
    def _():
        out_ref[...] = x1_ref[...]

    base = (pl.program_id(0) * N_EXPERTS + e) * CAP_LAT

    def body(it, _):
        r0 = pl.multiple_of(it * SLOT_GROUP, SLOT_GROUP)
        rows = [idx_ref[base + r0 + k] for k in range(SLOT_GROUP)]
        old = [out_ref[0, pl.ds(rows[k], 1), :] for k in range(SLOT_GROUP)]
        y = y_ref[0, pl.ds(r0, SLOT_GROUP), :]
        for k in range(SLOT_GROUP):
            out_ref[0, pl.ds(rows[k], 1), :] = old[k] + y[k:k + 1, :]
        return 0

    lax.fori_loop(0, CAP_LAT // SLOT_GROUP, body, 0)


def _scatter_lat(idx_flat, yg, x1_3, off):
    blk0 = BATCH * CAP_CTX // CAP_LAT
    return pl.pallas_call(
        _scatter_lat_kernel,
        grid_spec=pltpu.PrefetchScalarGridSpec(
            num_scalar_prefetch=1,
            grid=(DEC_BATCH, 1, N_EXPERTS),
            in_specs=[pl.BlockSpec((1, CAP_LAT, D_MODEL), lambda b, h, e, idx: (e, blk0 + b, 0)),
                      pl.BlockSpec((1, DEC_SEQ, D_MODEL), lambda b, h, e, idx: (off + b, 0, 0),
                                   pipeline_mode=pl.Buffered(1))],
            out_specs=pl.BlockSpec((1, DEC_SEQ, D_MODEL), lambda b, h, e, idx: (b, 0, 0),
                                   pipeline_mode=pl.Buffered(1)),
        ),
        out_shape=jax.ShapeDtypeStruct((DEC_BATCH, DEC_SEQ, D_MODEL), F32),
        compiler_params=_params(("arbitrary", "arbitrary", "arbitrary")),
        name="scatter_lat",
    )(idx_flat, yg, x1_3)


def _rope_tables():
    pos = np.arange(DEC_SEQ)
    freq = (np.float32(ROPE_THETA) ** (-np.arange(ROPE_FREQS, dtype=np.float32) / np.float32(ROPE_FREQS)))
    ang_r = (pos // GRID_W).astype(np.float32)[:, None] * freq.astype(np.float32)
    ang_c = (pos % GRID_W).astype(np.float32)[:, None] * freq.astype(np.float32)
    cos = np.concatenate([np.cos(ang_r)] * 2 + [np.cos(ang_c)] * 2, axis=-1)
    sin = np.concatenate([-np.sin(ang_r), np.sin(ang_r), -np.sin(ang_c), np.sin(ang_c)], axis=-1)
    reps = LANE // HEAD_DIM
    cs = np.concatenate([np.ones((ROW_TILE, LANE)), np.tile(cos, (1, reps))], axis=0).astype(np.float32)
    sn = np.concatenate([np.zeros((ROW_TILE, LANE)), np.tile(sin, (1, reps))], axis=0).astype(np.float32)
    return cs, sn


def _rope_tile(i):
    lat = jnp.maximum(i - N_CTX_TILES, 0) % (DEC_SEQ // ROW_TILE)
    return jnp.where(i < N_CTX_TILES, 0, 1 + lat)


def _qk_gain(q_norm, k_norm):
    q = jnp.tile(q_norm, N_HEADS) * (HEAD_DIM ** -0.5 * LOG2_E)
    return jnp.concatenate([q, jnp.tile(k_norm, N_KV)])[None, :]


def kernel(x_prompt, x_sample, cache_a_k, cache_a_v, cache_c_k, cache_c_v, c, c_ctx, norm1_g, w_mod, b_mod, w_in,
           a_q_norm, a_k_norm, a_sink, b_v_norm, b_ws, b_bs, c_q_norm, c_k_norm, w_a_o, w_b_o, w_c_o, w_out, norm2_g,
           w_router, b_router, w_gate, w_up, w_down):
    cond8 = jnp.concatenate([c_ctx[None, :], c, jnp.zeros((8 - N_REQ, D_MODEL), F32)], axis=0)
    mods = _modulation(cond8, w_mod, b_mod).reshape(DEPTH, 8, 1, 6 * D_MODEL)

    cs, sn = _rope_tables()
    w_in_b = w_in.astype(BF16)
    wa_b, wb_b, wc_b, wo_b = w_a_o.astype(BF16), w_b_o.astype(BF16), w_c_o.astype(BF16), w_out.astype(BF16)
    ws_b = b_ws.astype(BF16)
    wr_pad = jnp.pad(w_router, ((0, 0), (0, 0), (0, LANE - N_EXPERTS))).astype(BF16)
    br_pad = jnp.pad(b_router, ((0, 0), (0, LANE - N_EXPERTS)), constant_values=NEG_BIG)

    by_seq = lambda a: a.reshape(T_ALL // SEQ, SEQ, a.shape[-1])
    by_dec = lambda a: a.reshape(T_ALL // DEC_SEQ, DEC_SEQ, a.shape[-1])
    lat_off = T_CTX // DEC_SEQ

    caches = [a.reshape(DEC_BATCH, DEPTH, PAST_LEN, KV_W).astype(BF16)
              for a in (cache_a_k, cache_a_v, cache_c_k, cache_c_v)]

    x_ctx = x_prompt.reshape(T_CTX, D_MODEL)
    x_lat = x_sample.reshape(T_LAT, D_MODEL)
    new_kv = [[], [], [], []]
    for l in range(DEPTH):
        qa, ka_b, va_b, nka, nva, bu, bv, qc, kc_b, vc_b, nkc, nvc, gt = _input_projection(
            x_ctx, x_lat, mods, l, norm1_g[l][None, :], w_in_b[l], cs, sn,
            _qk_gain(a_q_norm[l], a_k_norm[l]), _qk_gain(c_q_norm[l], c_k_norm[l]), b_v_norm[l][None, :])
        for lst, arr in zip(new_kv, (nka, nva, nkc, nvc)):
            lst.append(arr[:T_CTX].reshape(BATCH, SEQ, N_KV, HEAD_DIM))

        sink = a_sink[l]
        oa_ctx = _dense_attention(by_seq(qa), by_seq(ka_b), by_seq(va_b), None, sink,
                                  n_req=BATCH, off=0, tq=SEQ, key_chunk=SEQ)
        oc_ctx = _dense_attention(by_seq(qc), by_seq(kc_b), by_seq(vc_b), None, None,
                                  n_req=BATCH, off=0, tq=SEQ, key_chunk=SEQ)
        cak, cav, cck, ccv = (a[:, l] for a in caches)
        oa_lat = _window_attention(by_dec(qa), by_dec(ka_b), by_dec(va_b), cak, cav, sink,
                                   n_req=DEC_BATCH, off=lat_off)
        oc_lat = _dense_attention(by_dec(qc), by_dec(kc_b), by_dec(vc_b), (cck, ccv), None,
                                  n_req=DEC_BATCH, off=lat_off, tq=1024, key_chunk=512)

        bs_full = jnp.repeat(b_bs[l].T, B_GROUP_CH, axis=1)
        x1, h2p, afft = _merge(x_ctx, x_lat, oa_ctx.reshape(T_CTX, Q_W), oa_lat.reshape(T_LAT, Q_W), bu, bv,
                               oc_ctx.reshape(T_CTX, Q_W), oc_lat.reshape(T_LAT, Q_W), gt,
                               wa_b[l], wb_b[l], wc_b[l], wo_b[l], ws_b[l], bs_full,
                               mods, l, norm2_g[l][None, :], wr_pad[l], br_pad[l][None, :])

        aff_rows = lambda a, n_req, n: a.reshape(N_EXPERTS, n_req, n).transpose(1, 0, 2).reshape(n_req * N_EXPERTS, n)
        idx_c, val_c = _select(aff_rows(afft[:, :T_CTX], BATCH, SEQ), BATCH * N_EXPERTS, CAP_CTX)
        _, val_l, idx_l_rows = _select(aff_rows(afft[:, T_CTX:], DEC_BATCH, DEC_SEQ), N_EXPERTS, CAP_LAT)
        idx_l_flat = idx_l_rows.reshape(-1)
        xg_ctx = _gather_ctx(idx_c, h2p)
        xg_lat = _gather_lat(idx_l_flat, by_dec(h2p), lat_off)
        yg = _expert_ffn(xg_ctx, xg_lat, val_c.reshape(BATCH, N_EXPERTS, CAP_CTX, 1),
                         val_l.reshape(DEC_BATCH, N_EXPERTS, CAP_LAT, 1), mods, w_gate, w_up, w_down, l)

        x_ctx = _scatter_ctx(idx_c, yg, x1)
        x_lat = _scatter_lat(idx_l_flat, yg, by_dec(x1), lat_off).reshape(T_LAT, D_MODEL)

    y_prompt = x_ctx.reshape(BATCH, SEQ, D_MODEL)
    y_sample = x_lat.reshape(DEC_BATCH, DEC_SEQ, D_MODEL)
    return (y_prompt, y_sample) + tuple(jnp.stack(lst, axis=1) for lst in new_kv)
```

```python
import functools

import jax
import numpy as np
import jax.numpy as jnp
from jax import lax
from jax.experimental import pallas as pl
from jax.experimental.pallas import tpu as pltpu

F32 = jnp.float32
BF16 = jnp.bfloat16
I32 = jnp.int32

D_MODEL = 1024
BATCH = 16
SEQ = 256
DEPTH = 2
DEC_BATCH = 2
DEC_SEQ = 4096
PAST_LEN = 256
GRID_W = 64
HEAD_DIM = 64
N_HEADS = 6
N_KV = 2
N_GRP = N_HEADS // N_KV
B_GROUPS = 4
B_GROUP_CH = 64
B_WIDTH = B_GROUPS * B_GROUP_CH
Q_W = N_HEADS * HEAD_DIM
KV_W = N_KV * HEAD_DIM
QK_W = Q_W + KV_W
N_BRANCH = 3
WINDOW = 128
BLOCK = 128
CHUNK = 128
N_EXPERTS = 16
EXPERT_FF = 1024
CAP_FACTOR = 2
ROPE_THETA = 10000.0
ROPE_FREQS = HEAD_DIM // 4
EPS = 1e-6
IN_WIDTH = 2 * (QK_W + KV_W) + 2 * B_WIDTH + N_BRANCH * D_MODEL

T_CTX = BATCH * SEQ
T_LAT = DEC_BATCH * DEC_SEQ
T_ALL = T_CTX + T_LAT
N_REQ = 1 + DEC_BATCH
CAP_CTX = CAP_FACTOR * SEQ // N_EXPERTS
CAP_LAT = CAP_FACTOR * DEC_SEQ // N_EXPERTS
ROWS_PER_EXPERT = BATCH * CAP_CTX + DEC_BATCH * CAP_LAT

LANE = 128
ROW_TILE = 512
N_CTX_TILES = T_CTX // ROW_TILE
FFN_ROW_TILE = 512
VMEM_LIMIT = 56 * 1024 * 1024
NEG_BIG = -1e30
LOG2_E = 1.4426950408889634

OFF_A = 0
OFF_AV = OFF_A + QK_W
OFF_BU = OFF_AV + KV_W
OFF_BV = OFF_BU + B_WIDTH
OFF_C = OFF_BV + B_WIDTH
OFF_CV = OFF_C + QK_W
OFF_G = OFF_CV + KV_W


def _params(sem, vmem=VMEM_LIMIT):
    return pltpu.CompilerParams(dimension_semantics=sem, vmem_limit_bytes=vmem)


def _sigmoid(x):
    return 1.0 / (1.0 + jnp.exp(-x))


def _gelu_tanh(x):
    return 0.5 * x * (1.0 + jnp.tanh(0.7978845608028654 * (x + 0.044715 * (x * x * x))))


def _split_bf16(x):
    hi = x.astype(BF16)
    lo = (x - hi.astype(F32)).astype(BF16)
    return hi, lo


def _mod_kernel(c_ref, w_ref, b_ref, o_ref):
    c = c_ref[...]
    s_hi, s_lo = _split_bf16(c * _sigmoid(c))
    w_hi, w_lo = _split_bf16(w_ref[0])
    acc = jnp.dot(s_hi, w_hi, preferred_element_type=F32)
    acc += jnp.dot(s_lo, w_hi, preferred_element_type=F32)
    acc += jnp.dot(s_hi, w_lo, preferred_element_type=F32)
    o_ref[0] = acc + b_ref[0]


def _modulation(cond8, w_mod, b_mod):
    n_col = 6 * D_MODEL // D_MODEL
    return pl.pallas_call(
        _mod_kernel,
        grid=(DEPTH, n_col),
        in_specs=[
            pl.BlockSpec((8, D_MODEL), lambda l, j: (0, 0)),
            pl.BlockSpec((1, D_MODEL, D_MODEL), lambda l, j: (l, 0, j)),
            pl.BlockSpec((1, 1, D_MODEL), lambda l, j: (l, 0, j)),
        ],
        out_specs=pl.BlockSpec((1, 8, D_MODEL), lambda l, j: (l, 0, j)),
        out_shape=jax.ShapeDtypeStruct((DEPTH, 8, 6 * D_MODEL), F32),
        compiler_params=_params(("arbitrary", "arbitrary")),
        name="modulation",
    )(cond8, w_mod, b_mod.reshape(DEPTH, 1, 6 * D_MODEL))


def _group_sumsq(y, bd_ref):
    return jnp.dot((y * y).astype(BF16), bd_ref[...], preferred_element_type=F32)


def _pick_pass(i, ctx_ref, lat_ref):
    return jnp.where(i < N_CTX_TILES, ctx_ref[...], lat_ref[...])


def _in_kernel(xc_ref, xl_ref, sc_ref, sh_ref, n1_ref, w_ref, cs_ref, sn_ref, ga_ref, gc_ref, gbv_ref, bd_qk_ref,
               bd_b_ref, qa_ref, ka_ref, va_ref, nka_ref, nva_ref, bu_ref, bv_ref, qc_ref, kc_ref, vc_ref, nkc_ref,
               nvc_ref, gt_ref):
    x = _pick_pass(pl.program_id(0), xc_ref, xl_ref)
    ms = jnp.mean(x * x, axis=-1, keepdims=True)
    h = x * lax.rsqrt(ms + EPS) * n1_ref[...]
    h = h * (1.0 + sc_ref[...]) + sh_ref[...]
    hb = h.astype(BF16)
    tm = x.shape[0]

    def proj(c0, width):
        return jnp.dot(hb, w_ref[:, c0:c0 + width], preferred_element_type=F32)

    cs = jnp.concatenate([cs_ref[...]] * (QK_W // LANE), axis=1)
    sn = jnp.concatenate([sn_ref[...]] * (QK_W // LANE), axis=1)
    lane = lax.broadcasted_iota(I32, (tm, QK_W), 1)
    first_half = (lane & ROPE_FREQS) == 0

    def qk_post(y, gain_ref):
        yn = y * lax.rsqrt(_group_sumsq(y, bd_qk_ref) * (1.0 / HEAD_DIM) + EPS) * gain_ref[...]
        partner = jnp.where(first_half, pltpu.roll(yn, QK_W - ROPE_FREQS, 1), pltpu.roll(yn, ROPE_FREQS, 1))
        return yn * cs + partner * sn

    def mixer(off_qk, off_v, gain_ref, q_ref, k_ref, v_ref, nk_ref, nv_ref):
        y = qk_post(proj(off_qk, QK_W), gain_ref)
        v = proj(off_v, KV_W)
        q_ref[...] = y[:, :Q_W].astype(BF16)
        k_ref[...] = y[:, Q_W:].astype(BF16)
        v_ref[...] = v.astype(BF16)

        nk_ref[...] = y[:, Q_W:]
        nv_ref[...] = v

    mixer(OFF_A, OFF_AV, ga_ref, qa_ref, ka_ref, va_ref, nka_ref, nva_ref)

    bu_ref[...] = _gelu_tanh(proj(OFF_BU, B_WIDTH)).astype(BF16)
    gv = _gelu_tanh(proj(OFF_BV, B_WIDTH))
    gvn = gv * lax.rsqrt(_group_sumsq(gv, bd_b_ref) * (1.0 / B_GROUP_CH) + EPS) * gbv_ref[...]
    bv_ref[...] = gvn.astype(BF16)

    mixer(OFF_C, OFF_CV, gc_ref, qc_ref, kc_ref, vc_ref, nkc_ref, nvc_ref)

    gate_chunk = 512
    for j in range(N_BRANCH * D_MODEL // gate_chunk):
        g = proj(OFF_G + j * gate_chunk, gate_chunk)
        gt_ref[:, j * gate_chunk:(j + 1) * gate_chunk] = _sigmoid(g).astype(BF16)


def _req_of_tile(i):
    return i // N_CTX_TILES


def _ctx_rows(w):
    return pl.BlockSpec((ROW_TILE, w), lambda i: (jnp.minimum(i, N_CTX_TILES - 1), 0))


def _lat_rows(w):
    return pl.BlockSpec((ROW_TILE, w), lambda i: (jnp.maximum(i - N_CTX_TILES, 0), 0))


MOD_SH1, MOD_SC1, MOD_G1, MOD_SH2, MOD_SC2, MOD_G2 = range(6)


def _mod_spec(layer, chunk, req):
    return pl.BlockSpec((None, None, 1, D_MODEL), lambda *g: (layer, req(*g), 0, chunk))


def _block_diag_ones(width, group):
    g = np.arange(width) // group
    return (g[:, None] == g[None, :]).astype(np.float32)


def _input_projection(x_ctx, x_lat, mods, layer, n1, w_in_b, cs, sn, gain_a, gain_c, gain_bv):
    bd_qk = jnp.asarray(_block_diag_ones(QK_W, HEAD_DIM), BF16)
    bd_b = jnp.asarray(_block_diag_ones(B_WIDTH, B_GROUP_CH), BF16)
    tm = ROW_TILE
    row = lambda w: pl.BlockSpec((tm, w), lambda i: (i, 0))
    full = lambda a: pl.BlockSpec(a.shape, lambda i: (0,) * a.ndim)
    rope = pl.BlockSpec((tm, LANE), lambda i: (_rope_tile(i), 0))
    cache_rows = T_CTX + tm
    spare = lambda w: pl.BlockSpec((tm, w), lambda i: (jnp.minimum(i, N_CTX_TILES), 0))
    mixer_outs = [(Q_W, BF16, T_ALL), (KV_W, BF16, T_ALL), (KV_W, BF16, T_ALL), (KV_W, F32, cache_rows),
                  (KV_W, F32, cache_rows)]
    outs = mixer_outs + [(B_WIDTH, BF16, T_ALL), (B_WIDTH, BF16, T_ALL)] + mixer_outs + [(N_BRANCH * D_MODEL, BF16, T_ALL)]
    return pl.pallas_call(
        _in_kernel,
        grid=(T_ALL // tm,),
        in_specs=[_ctx_rows(D_MODEL), _lat_rows(D_MODEL), _mod_spec(layer, MOD_SC1, _req_of_tile),
                  _mod_spec(layer, MOD_SH1, _req_of_tile), full(n1), full(w_in_b), rope, rope,
                  full(gain_a), full(gain_c), full(gain_bv), full(bd_qk), full(bd_b)],
        out_specs=[row(w) if rows == T_ALL else spare(w) for w, _, rows in outs],
        out_shape=[jax.ShapeDtypeStruct((rows, w), dt) for w, dt, rows in outs],
        compiler_params=_params(("arbitrary",)),
        name="input_projection",
    )(x_ctx, x_lat, mods, mods, n1, w_in_b, cs, sn, gain_a, gain_c, gain_bv, bd_qk, bd_b)


def _attention_tile(q_ref, sources, sink_ref, o_ref, qt_scr, ot_scr, *, tq, key_chunk):
    width = N_GRP * tq
    for j in range(Q_W // LANE):
        qt_scr[j * LANE:(j + 1) * LANE, :] = q_ref[0, :, j * LANE:(j + 1) * LANE].astype(F32).T.astype(BF16)
    for kv in range(N_KV):
        lo, hi = kv * HEAD_DIM, (kv + 1) * HEAD_DIM
        heads = [kv * N_GRP + g for g in range(N_GRP)]
        qt = jnp.concatenate([qt_scr[h * HEAD_DIM:(h + 1) * HEAD_DIM, :] for h in heads], axis=1)

        def step(carry, kref, vref, c0, size, bias):
            m, acc = carry
            s = jnp.dot(kref[0, pl.ds(c0, size), lo:hi], qt, preferred_element_type=F32)
            if bias is not None:
                s = s + jnp.concatenate([bias] * N_GRP, axis=1)
            vt = vref[0, pl.ds(c0, size), :].astype(F32).T[lo:hi, :].astype(BF16)
            vt = jnp.concatenate([vt, jnp.ones((DEN_ROWS, size), BF16)], axis=0)
            m_new = jnp.maximum(m, jnp.max(s, axis=0, keepdims=True))
            p = jnp.exp2(s - m_new).astype(BF16)
            acc = acc * jnp.exp2(m - m_new) + jnp.dot(vt, p, preferred_element_type=F32)
            return m_new, acc

        if sink_ref is not None:
            m0 = jnp.concatenate([jnp.full((1, tq), sink_ref[h] * LOG2_E, F32) for h in heads], axis=1)
            den0 = jnp.ones((DEN_ROWS, width), F32)
        else:
            m0 = jnp.full((1, width), NEG_BIG, F32)
            den0 = jnp.zeros((DEN_ROWS, width), F32)
        carry = (m0, jnp.concatenate([jnp.zeros((HEAD_DIM, width), F32), den0], axis=0))
        for kref, vref, bias in sources:
            n_rows = kref.shape[1]
            n_full = n_rows // key_chunk
            if bias is not None:
                carry = step(carry, kref, vref, 0, n_rows, bias)
                continue
            if n_full > 1:
                carry = lax.fori_loop(
                    0, n_full,
                    lambda c, cr: step(cr, kref, vref, pl.multiple_of(c * key_chunk, key_chunk), key_chunk, None), carry)
            elif n_full == 1:
                carry = step(carry, kref, vref, 0, key_chunk, None)
            if n_rows - n_full * key_chunk:
                carry = step(carry, kref, vref, n_full * key_chunk, n_rows - n_full * key_chunk, None)
        _, acc = carry
        o = acc[:HEAD_DIM] / acc[HEAD_DIM:HEAD_DIM + 1]
        for g, h in enumerate(heads):
            ot_scr[h * HEAD_DIM:(h + 1) * HEAD_DIM, :] = o[:, g * tq:(g + 1) * tq]
    for j in range(Q_W // LANE):
        o_ref[0, :, j * LANE:(j + 1) * LANE] = ot_scr[j * LANE:(j + 1) * LANE, :].T.astype(o_ref.dtype)


def _dense_attn_kernel(*refs, tq, key_chunk, has_extra, has_sink):
    refs = list(refs)
    q_ref, k_ref, v_ref = refs[:3]
    del refs[:3]
    sources = [(k_ref, v_ref, None)]
    if has_extra:
        sources.append((refs.pop(0), refs.pop(0), None))
    sink_ref = refs.pop(0) if has_sink else None
    o_ref, qt_scr, ot_scr = refs
    _attention_tile(q_ref, sources, sink_ref, o_ref, qt_scr, ot_scr, tq=tq, key_chunk=key_chunk)


def _attention_scratch(tq):
    return [pltpu.VMEM((Q_W, tq), BF16), pltpu.VMEM((Q_W, tq), F32)]


def _dense_attention(q, k, v, extra, sink, *, n_req, off, tq, key_chunk):
    s = q.shape[1]
    kv_spec = pl.BlockSpec((1, s, KV_W), lambda i, j: (off + i, 0, 0))
    in_specs = [pl.BlockSpec((1, tq, Q_W), lambda i, j: (off + i, j, 0)), kv_spec, kv_spec]
    args = [q, k, v]
    if extra is not None:
        in_specs += [pl.BlockSpec((1, extra[0].shape[1], KV_W), lambda i, j: (i, 0, 0))] * 2
        args += list(extra)
    if sink is not None:
        in_specs.append(pl.BlockSpec(memory_space=pltpu.SMEM))
        args.append(sink)
    return pl.pallas_call(
        functools.partial(_dense_attn_kernel, tq=tq, key_chunk=key_chunk, has_extra=extra is not None,
                          has_sink=sink is not None),
        grid=(n_req, s // tq),
        in_specs=in_specs,
        out_specs=pl.BlockSpec((1, tq, Q_W), lambda i, j: (i, j, 0)),
        out_shape=jax.ShapeDtypeStruct((n_req, s, Q_W), BF16),
        scratch_shapes=_attention_scratch(tq),
        compiler_params=_params(("arbitrary", "arbitrary")),
        name="dense_attention",
    )(*args)


def _ctx_attn_kernel(qa_ref, ka_ref, va_ref, qc_ref, kc_ref, vc_ref, sink_ref, oa_ref, oc_ref, qta, ota, qtc, otc):
    _attention_tile(qa_ref, [(ka_ref, va_ref, None)], sink_ref, oa_ref, qta, ota, tq=SEQ, key_chunk=SEQ)
    _attention_tile(qc_ref, [(kc_ref, vc_ref, None)], None, oc_ref, qtc, otc, tq=SEQ, key_chunk=SEQ)


def _context_attention(qa, ka, va, qc, kc, vc, sink):
    q_spec = pl.BlockSpec((1, SEQ, Q_W), lambda i: (i, 0, 0))
    kv_spec = pl.BlockSpec((1, SEQ, KV_W), lambda i: (i, 0, 0))
    return pl.pallas_call(
        _ctx_attn_kernel,
        grid=(BATCH,),
        in_specs=[q_spec, kv_spec, kv_spec, q_spec, kv_spec, kv_spec, pl.BlockSpec(memory_space=pltpu.SMEM)],
        out_specs=[q_spec, q_spec],
        out_shape=[jax.ShapeDtypeStruct((BATCH, SEQ, Q_W), BF16)] * 2,
        scratch_shapes=_attention_scratch(SEQ) * 2,
        compiler_params=_params(("arbitrary",)),
        name="context_attention",
    )(qa, ka, va, qc, kc, vc, sink)


DEN_ROWS = 16
WINDOW_TQ = 512


def _window_attn_kernel(q_ref, kp_ref, kc_ref, kn_ref, vp_ref, vc_ref, vn_ref, ck_ref, cv_ref, bp_ref, bc_ref, bn_ref,
                        sink_ref, o_ref, qt_scr, ot_scr, *, seq):
    q_pos0 = pl.program_id(1) * WINDOW_TQ
    prev_bias = bp_ref[...] + jnp.where(q_pos0 >= BLOCK, 0.0, NEG_BIG)
    next_bias = bn_ref[...] + jnp.where(q_pos0 + WINDOW_TQ < seq, 0.0, NEG_BIG)
    sources = [(kp_ref, vp_ref, prev_bias), (kc_ref, vc_ref, bc_ref[...]), (kn_ref, vn_ref, next_bias),
               (ck_ref, cv_ref, None)]
    _attention_tile(q_ref, sources, sink_ref, o_ref, qt_scr, ot_scr, tq=WINDOW_TQ, key_chunk=WINDOW_TQ)


def _band_bias(first_key, n_keys):
    d = (first_key + np.arange(n_keys))[:, None] - np.arange(WINDOW_TQ)[None, :]
    return np.where(np.abs(d) <= WINDOW, 0.0, NEG_BIG).astype(np.float32)


def _window_attention(q, k, v, ck, cv, sink, *, n_req, off):
    b, s = n_req, q.shape[1]
    nb = s // BLOCK
    per_tile = WINDOW_TQ // BLOCK
    edge = lambda f: pl.BlockSpec((1, BLOCK, KV_W), lambda i, j: (off + i, f(j), 0))
    prev = lambda j: jnp.maximum(j * per_tile - 1, 0)
    nxt = lambda j: jnp.minimum((j + 1) * per_tile, nb - 1)
    cur = pl.BlockSpec((1, WINDOW_TQ, KV_W), lambda i, j: (off + i, j, 0))
    ctx = pl.BlockSpec((1, PAST_LEN, KV_W), lambda i, j: (i, 0, 0))
    biases = [_band_bias(-BLOCK, BLOCK), _band_bias(0, WINDOW_TQ), _band_bias(WINDOW_TQ, BLOCK)]
    table = lambda a: pl.BlockSpec(a.shape, lambda i, j: (0, 0))
    return pl.pallas_call(
        functools.partial(_window_attn_kernel, seq=s),
        grid=(b, s // WINDOW_TQ),
        in_specs=[pl.BlockSpec((1, WINDOW_TQ, Q_W), lambda i, j: (off + i, j, 0)),
                  edge(prev), cur, edge(nxt), edge(prev), cur, edge(nxt), ctx, ctx,
                  table(biases[0]), table(biases[1]), table(biases[2]),
                  pl.BlockSpec(memory_space=pltpu.SMEM)],
        out_specs=pl.BlockSpec((1, WINDOW_TQ, Q_W), lambda i, j: (i, j, 0)),
        out_shape=jax.ShapeDtypeStruct((b, s, Q_W), BF16),
        scratch_shapes=_attention_scratch(WINDOW_TQ),
        compiler_params=_params(("arbitrary", "arbitrary")),
        name="window_attention",
    )(q, k, k, k, v, v, v, ck, cv, *biases, sink)


def _pack_halves(x):
    half = x.shape[1] // 2
    return pltpu.pack_elementwise([x[:, :half], x[:, half:]], packed_dtype=BF16)


def _unpack_halves(words):
    return tuple(pltpu.unpack_elementwise(words, index=i, packed_dtype=BF16, unpacked_dtype=F32).astype(BF16)
                 for i in range(2))


def _merge_kernel(xc_ref, xl_ref, oac_ref, oal_ref, bu_ref, bv_ref, occ_ref, ocl_ref, gt_ref, wa_ref, wb_ref, wc_ref,
                  wo_ref, ws_ref, bs_ref, g1_ref, sc2_ref, sh2_ref, n2_ref, wr_ref, br_ref, x1_ref, h2p_ref, afft_ref):
    i = pl.program_id(0)
    tm = xc_ref.shape[0]
    group = lax.broadcasted_iota(I32, (CHUNK, B_WIDTH), 1) // B_GROUP_CH
    obs = []
    for c in range(tm // CHUNK):
        v = bv_ref[c * CHUNK:(c + 1) * CHUNK, :]
        sv = jnp.zeros((CHUNK, B_WIDTH), F32)
        for g in range(B_GROUPS):
            sv = jnp.where(group == g, jnp.dot(ws_ref[g], v, preferred_element_type=F32), sv)
        u = bu_ref[c * CHUNK:(c + 1) * CHUNK, :].astype(F32)
        obs.append((u * (sv + bs_ref[...])).astype(BF16))
    ob = jnp.concatenate(obs, axis=0)

    oa = _pick_pass(i, oac_ref, oal_ref)
    oc = _pick_pass(i, occ_ref, ocl_ref)
    merged = gt_ref[:, 0:D_MODEL].astype(F32) * jnp.dot(oa, wa_ref[...], preferred_element_type=F32)
    merged += gt_ref[:, D_MODEL:2 * D_MODEL].astype(F32) * jnp.dot(ob, wb_ref[...], preferred_element_type=F32)
    merged += gt_ref[:, 2 * D_MODEL:3 * D_MODEL].astype(F32) * jnp.dot(oc, wc_ref[...], preferred_element_type=F32)
    y = jnp.dot(merged.astype(BF16), wo_ref[...], preferred_element_type=F32)
    x1 = _pick_pass(i, xc_ref, xl_ref) + g1_ref[...] * y
    x1_ref[...] = x1

    ms = jnp.mean(x1 * x1, axis=-1, keepdims=True)
    h2 = x1 * lax.rsqrt(ms + EPS) * n2_ref[...]
    h2 = h2 * (1.0 + sc2_ref[...]) + sh2_ref[...]
    h2p_ref[...] = _pack_halves(h2)

    logits = jnp.dot(h2.astype(BF16), wr_ref[...], preferred_element_type=F32) + br_ref[...]
    e = jnp.exp(logits - jnp.max(logits, axis=-1, keepdims=True))
    aff = e / jnp.sum(e, axis=-1, keepdims=True)
    afft_ref[...] = aff.T[:N_EXPERTS, :]


def _merge(x_ctx, x_lat, oa_ctx, oa_lat, bu, bv, oc_ctx, oc_lat, gt, wa, wb, wc, wo, ws, bs, mods, layer, n2, wr, br):
    tm = ROW_TILE
    row = lambda w: pl.BlockSpec((tm, w), lambda i: (i, 0))
    full = lambda a: pl.BlockSpec(a.shape, lambda i: (0,) * a.ndim)
    mod = lambda chunk: _mod_spec(layer, chunk, _req_of_tile)
    return pl.pallas_call(
        _merge_kernel,
        grid=(T_ALL // tm,),
        in_specs=[_ctx_rows(D_MODEL), _lat_rows(D_MODEL), _ctx_rows(Q_W), _lat_rows(Q_W), row(B_WIDTH), row(B_WIDTH),
                  _ctx_rows(Q_W), _lat_rows(Q_W), row(N_BRANCH * D_MODEL),
                  full(wa), full(wb), full(wc), full(wo), full(ws), full(bs),
                  mod(MOD_G1), mod(MOD_SC2), mod(MOD_SH2), full(n2), full(wr), full(br)],
        out_specs=[row(D_MODEL), row(D_MODEL // 2), pl.BlockSpec((N_EXPERTS, tm), lambda i: (0, i))],
        out_shape=[jax.ShapeDtypeStruct((T_ALL, D_MODEL), F32), jax.ShapeDtypeStruct((T_ALL, D_MODEL // 2), jnp.uint32),
                   jax.ShapeDtypeStruct((N_EXPERTS, T_ALL), F32)],
        compiler_params=_params(("arbitrary",)),
        name="merge_router",
    )(x_ctx, x_lat, oa_ctx, oa_lat, bu, bv, oc_ctx, oc_lat, gt, wa, wb, wc, wo, ws, bs, mods, mods, mods, n2, wr, br)


def _select_kernel(aff_ref, idx_ref, val_ref, *rest, n, cap, row_chunk):
    idx_row_ref = rest[0] if len(rest) == 4 else None
    possel_ref, idx_scr, val_scr = rest[-3:]
    a = aff_ref[...]
    rows = a.shape[0]
    tok = lax.broadcasted_iota(I32, (rows, n), 1)

    def count(ones):
        return jnp.sum(ones, axis=1, keepdims=True)

    def at_least(word):
        return jnp.where(a >= pltpu.bitcast(word, F32), 1, 0)

    thr = jnp.zeros((rows, 1), I32)
    for bit in range(30, -1, -1):
        cand = thr | (1 << bit)
        thr = jnp.where(count(at_least(cand)) >= cap, cand, thr)
    above = at_least(thr + 1)
    tied = at_least(thr) - above
    need = cap - count(above)
    last = jnp.zeros((rows, 1), I32)
    for bit in range(n.bit_length() - 2, -1, -1):
        cand = last | (1 << bit)
        last = jnp.where(count(jnp.where(tok < cand, tied, 0)) < need, cand, last)
    sel = above + jnp.where(tok <= last, tied, 0)

    blk = min(n, 256)
    tri = jnp.where(lax.broadcasted_iota(I32, (blk, blk), 0) <= lax.broadcasted_iota(I32, (blk, blk), 1),
                    1.0, 0.0).astype(BF16)
    sel_f = sel.astype(F32)
    offset = jnp.zeros((rows, 1), F32)
    for j in range(n // blk):
        s_blk = sel_f[:, j * blk:(j + 1) * blk]
        incl = jnp.dot(s_blk.astype(BF16), tri, preferred_element_type=F32)
        pos = (incl - s_blk + offset).astype(I32)
        possel_ref[:, j * blk:(j + 1) * blk] = jnp.where(sel[:, j * blk:(j + 1) * blk] > 0, pos, -1)
        offset = offset + incl[:, blk - 1:blk]

    tb = min(n, TOKEN_BLOCK)
    n_blk = n // tb

    def fold_lanes(x):
        acc = x[:, :LANE]
        for k in range(1, tb // LANE):
            acc = acc + x[:, k * LANE:(k + 1) * LANE]
        return acc

    def match(e, slot, t0):
        hit = possel_ref[pl.ds(e, 1), pl.ds(t0, tb)] == slot
        tok = t0 + lax.broadcasted_iota(I32, (1, tb), 1)
        return (fold_lanes(jnp.where(hit, tok, 0)),
                fold_lanes(jnp.where(hit, aff_ref[pl.ds(e, 1), pl.ds(t0, tb)], 0.0)))

    def per_row(e, _):
        ends, run = [], 0
        for j in range(n_blk - 1):
            run = run + jnp.sum(jnp.where(possel_ref[pl.ds(e, 1), j * tb:(j + 1) * tb] >= 0, 1, 0))
            ends.append(run)

        def per_chunk(c, _):
            r0 = pl.multiple_of(c * row_chunk, row_chunk)
            slot = lax.broadcasted_iota(I32, (row_chunk, 1), 0) + r0
            if n_blk == 1:
                idx, val = match(e, slot, 0)
            else:
                first = sum(jnp.where(end <= r0, 1, 0) for end in ends)
                last = 1 + sum(jnp.where(end < r0 + row_chunk, 1, 0) for end in ends)

                def per_block(j, acc):
                    i, v = match(e, slot, pl.multiple_of(j * tb, tb))
                    return acc[0] + i, acc[1] + v

                idx, val = lax.fori_loop(first, last, per_block,
                                         (jnp.zeros((row_chunk, LANE), I32), jnp.zeros((row_chunk, LANE), F32)))
            idx_scr[pl.ds(r0, row_chunk), :] = idx
            val_scr[pl.ds(r0, row_chunk), :] = val
            return 0

        lax.fori_loop(0, cap // row_chunk, per_chunk, 0)
        idx = jnp.sum(idx_scr[...], axis=1, keepdims=True)
        idx_ref[e] = idx
        val_ref[e] = jnp.sum(val_scr[...], axis=1, keepdims=True)
        if idx_row_ref is not None:
            idx_row_ref[pl.ds(e, 1), :] = jnp.broadcast_to(idx.astype(F32), (cap, LANE)).T[0:1, :].astype(I32)
        return 0

    def per_small_row(e, _):
        idx, val = match(e, lax.broadcasted_iota(I32, (cap, 1), 0), 0)
        idx_ref[e] = jnp.sum(idx, axis=1, keepdims=True)
        val_ref[e] = jnp.sum(val, axis=1, keepdims=True)
        return 0

    if n_blk == 1 and cap == row_chunk:
        lax.fori_loop(0, rows, per_small_row, 0, unroll=4)
    else:
        lax.fori_loop(0, rows, per_row, 0)


def _select(aff_rows, rows_per_step, cap):
    r, n = aff_rows.shape
    row_chunk = min(cap, 64)
    out_specs = [pl.BlockSpec((rows_per_step, cap, 1), lambda s: (s, 0, 0))] * 2
    out_shape = [jax.ShapeDtypeStruct((r, cap, 1), I32), jax.ShapeDtypeStruct((r, cap, 1), F32)]
    if cap % LANE == 0:
        out_specs.append(pl.BlockSpec((rows_per_step, cap), lambda s: (s, 0)))
        out_shape.append(jax.ShapeDtypeStruct((r, cap), I32))
    return pl.pallas_call(
        functools.partial(_select_kernel, n=n, cap=cap, row_chunk=row_chunk),
        grid=(r // rows_per_step,),
        in_specs=[pl.BlockSpec((rows_per_step, n), lambda s: (s, 0))],
        out_specs=out_specs,
        out_shape=out_shape,
        scratch_shapes=[pltpu.VMEM((rows_per_step, n), I32), pltpu.VMEM((cap, LANE), I32), pltpu.VMEM((cap, LANE), F32)],
        compiler_params=_params(("arbitrary",)),
        name="expert_select",
    )(aff_rows)


CTX_SLOTS = N_EXPERTS * CAP_CTX
TOKEN_BLOCK = 512
SLOT_GROUP = 16


CTX_PER_STEP = 2


def _ctx_slot_onehot(idx, slots_on_rows):
    idx = idx.reshape(CTX_SLOTS, 1)
    if slots_on_rows:
        hit = idx == lax.broadcasted_iota(I32, (CTX_SLOTS, SEQ), 1)
    else:
        idx_lane = jnp.broadcast_to(idx.astype(F32), (CTX_SLOTS, LANE)).T[0:1, :]
        hit = idx_lane == lax.broadcasted_iota(I32, (SEQ, CTX_SLOTS), 0).astype(F32)
    return jnp.where(hit, 1.0, 0.0).astype(BF16)


def _gather_ctx_kernel(idx_ref, h_ref, out_ref):
    for r in range(CTX_PER_STEP):
        onehot = _ctx_slot_onehot(idx_ref[r * N_EXPERTS:(r + 1) * N_EXPERTS], True)
        lo, hi = _unpack_halves(h_ref[r * SEQ:(r + 1) * SEQ, :])
        g_lo = jnp.dot(onehot, lo, preferred_element_type=F32)
        g_hi = jnp.dot(onehot, hi, preferred_element_type=F32)
        packed = pltpu.pack_elementwise([g_lo, g_hi], packed_dtype=BF16)
        out_ref[:, r * CAP_CTX:(r + 1) * CAP_CTX, :] = packed.reshape(N_EXPERTS, CAP_CTX, D_MODEL // 2)


def _gather_ctx(idx_c, h2p):
    n = CTX_PER_STEP
    return pl.pallas_call(
        _gather_ctx_kernel,
        grid=(BATCH // n,),
        in_specs=[pl.BlockSpec((n * N_EXPERTS, CAP_CTX, 1), lambda b: (b, 0, 0)),
                  pl.BlockSpec((n * SEQ, D_MODEL // 2), lambda b: (b, 0))],
        out_specs=pl.BlockSpec((N_EXPERTS, n * CAP_CTX, D_MODEL // 2), lambda b: (0, b, 0)),
        out_shape=jax.ShapeDtypeStruct((N_EXPERTS, BATCH * CAP_CTX, D_MODEL // 2), jnp.uint32),
        compiler_params=_params(("arbitrary",)),
        name="gather_ctx",
    )(idx_c, h2p)


def _gather_lat_kernel(idx_ref, src_ref, out_ref):
    base = (pl.program_id(0) * N_EXPERTS + pl.program_id(1)) * CAP_LAT

    def body(it, _):
        r0 = pl.multiple_of(it * SLOT_GROUP, SLOT_GROUP)
        picked = [src_ref[0, pl.ds(idx_ref[base + r0 + k], 1), :] for k in range(SLOT_GROUP)]
        dst = out_ref.at[0, pl.ds(r0, SLOT_GROUP)]
        for k in range(SLOT_GROUP):
            dst[k:k + 1, :] = picked[k]
        return 0

    lax.fori_loop(0, CAP_LAT // SLOT_GROUP, body, 0)


def _gather_lat(idx_flat, h2p3, off):
    return pl.pallas_call(
        _gather_lat_kernel,
        grid_spec=pltpu.PrefetchScalarGridSpec(
            num_scalar_prefetch=1,
            grid=(DEC_BATCH, N_EXPERTS),
            in_specs=[pl.BlockSpec((1, DEC_SEQ, D_MODEL // 2), lambda b, e, idx: (off + b, 0, 0))],
            out_specs=pl.BlockSpec((1, CAP_LAT, D_MODEL // 2), lambda b, e, idx: (e, b, 0)),
        ),
        out_shape=jax.ShapeDtypeStruct((N_EXPERTS, DEC_BATCH * CAP_LAT, D_MODEL // 2), jnp.uint32),
        compiler_params=_params(("arbitrary", "arbitrary")),
        name="gather_lat",
    )(idx_flat, h2p3)


N_CTX_FFN_TILES = BATCH * CAP_CTX // FFN_ROW_TILE


def _ffn_kernel(xc_ref, xl_ref, vc_ref, vl_ref, g2_ref, wg_ref, wu_ref, wd_ref, o_ref, wg_b, wu_b, wd_b):
    j = pl.program_id(1)

    @pl.when(j == 0)
    def _():
        wg_b[...] = wg_ref[0].astype(BF16)
        wu_b[...] = wu_ref[0].astype(BF16)
        wd_b[...] = wd_ref[0].astype(BF16)

    is_ctx = j < N_CTX_FFN_TILES
    x = jnp.where(is_ctx, jnp.concatenate(_unpack_halves(xc_ref[0]), axis=1),
                  jnp.concatenate(_unpack_halves(xl_ref[0]), axis=1))
    g = jnp.dot(x, wg_b[...], preferred_element_type=F32)
    u = jnp.dot(x, wu_b[...], preferred_element_type=F32)
    hh = (g * _sigmoid(g)) * u
    y = jnp.dot(hh.astype(BF16), wd_b[...], preferred_element_type=F32)
    o_ref[0] = (y * jnp.where(is_ctx, vc_ref[...].reshape(FFN_ROW_TILE, 1), vl_ref[...])) * g2_ref[...]


def _expert_ffn(xg_ctx, xg_lat, val_ctx, val_lat, mods, w_gate, w_up, w_down, layer):
    tr = FFN_ROW_TILE
    assert tr == BATCH * CAP_CTX == CAP_LAT
    n_tiles = ROWS_PER_EXPERT // tr
    def wspec(k, n, tiles_held):
        ahead = lambda e, j: jnp.minimum(e + jnp.where(j >= tiles_held, 1, 0), N_EXPERTS - 1)
        return pl.BlockSpec((None, 1, k, n), lambda e, j: (layer, ahead(e, j), 0, 0))

    ctx_tile = lambda j: jnp.minimum(j, N_CTX_FFN_TILES - 1)
    lat_tile = lambda j: jnp.maximum(j - N_CTX_FFN_TILES, 0)
    return pl.pallas_call(
        _ffn_kernel,
        grid=(N_EXPERTS, n_tiles),
        in_specs=[pl.BlockSpec((1, tr, D_MODEL // 2), lambda e, j: (e, ctx_tile(j), 0)),
                  pl.BlockSpec((1, tr, D_MODEL // 2), lambda e, j: (e, lat_tile(j), 0)),
                  pl.BlockSpec((BATCH, None, CAP_CTX, 1), lambda e, j: (0, e, 0, 0)),
                  pl.BlockSpec((None, None, CAP_LAT, 1), lambda e, j: (lat_tile(j), e, 0, 0)),
                  _mod_spec(layer, MOD_G2, lambda e, j: j),
                  wspec(D_MODEL, EXPERT_FF, 1), wspec(D_MODEL, EXPERT_FF, n_tiles - 1),
                  wspec(EXPERT_FF, D_MODEL, n_tiles)],
        out_specs=pl.BlockSpec((1, tr, D_MODEL), lambda e, j: (e, j, 0)),
        out_shape=jax.ShapeDtypeStruct((N_EXPERTS, ROWS_PER_EXPERT, D_MODEL), F32),
        scratch_shapes=[pltpu.VMEM((D_MODEL, EXPERT_FF), BF16), pltpu.VMEM((D_MODEL, EXPERT_FF), BF16),
                        pltpu.VMEM((EXPERT_FF, D_MODEL), BF16)],
        compiler_params=_params(("arbitrary", "arbitrary")),
        name="expert_ffn",
    )(xg_ctx, xg_lat, val_ctx, val_lat, mods, w_gate, w_up, w_down)


def _scatter_ctx_kernel(idx_ref, y_ref, x1_ref, out_ref):
    for r in range(CTX_PER_STEP):
        onehot = _ctx_slot_onehot(idx_ref[r * N_EXPERTS:(r + 1) * N_EXPERTS], False)
        y_hi, y_lo = _split_bf16(y_ref[:, r * CAP_CTX:(r + 1) * CAP_CTX, :].reshape(CTX_SLOTS, D_MODEL))
        moe = jnp.dot(onehot, y_hi, preferred_element_type=F32) + jnp.dot(onehot, y_lo, preferred_element_type=F32)
        out_ref[r * SEQ:(r + 1) * SEQ, :] = x1_ref[r * SEQ:(r + 1) * SEQ, :] + moe


def _scatter_ctx(idx_c, yg, x1):
    n = CTX_PER_STEP
    return pl.pallas_call(
        _scatter_ctx_kernel,
        grid=(BATCH // n,),
        in_specs=[pl.BlockSpec((n * N_EXPERTS, CAP_CTX, 1), lambda b: (b, 0, 0)),
                  pl.BlockSpec((N_EXPERTS, n * CAP_CTX, D_MODEL), lambda b: (0, b, 0)),
                  pl.BlockSpec((n * SEQ, D_MODEL), lambda b: (b, 0))],
        out_specs=pl.BlockSpec((n * SEQ, D_MODEL), lambda b: (b, 0)),
        out_shape=jax.ShapeDtypeStruct((T_CTX, D_MODEL), F32),
        compiler_params=_params(("arbitrary",)),
        name="scatter_ctx",
    )(idx_c, yg, x1)


def _scatter_lat_kernel(idx_ref, y_ref, x1_ref, out_ref):
    e = pl.program_id(2)

    @pl.when(e == 0)
    def _():
        out_ref[...] = x1_ref[...]

    base = (pl.program_id(0) * N_EXPERTS + e) * CAP_LAT

    def body(it, _):
        r0 = pl.multiple_of(it * SLOT_GROUP, SLOT_GROUP)
        rows = [idx_ref[base + r0 + k] for k in range(SLOT_GROUP)]
        old = [out_ref[0, pl.ds(rows[k], 1), :] for k in range(SLOT_GROUP)]
        y = y_ref[0, pl.ds(r0, SLOT_GROUP), :]
        for k in range(SLOT_GROUP):
            out_ref[0, pl.ds(rows[k], 1), :] = old[k] + y[k:k + 1, :]
        return 0

    lax.fori_loop(0, CAP_LAT // SLOT_GROUP, body, 0)


def _scatter_lat(idx_flat, yg, x1_3, off):
    blk0 = BATCH * CAP_CTX // CAP_LAT
    return pl.pallas_call(
        _scatter_lat_kernel,
        grid_spec=pltpu.PrefetchScalarGridSpec(
            num_scalar_prefetch=1,
            grid=(DEC_BATCH, 1, N_EXPERTS),
            in_specs=[pl.BlockSpec((1, CAP_LAT, D_MODEL), lambda b, h, e, idx: (e, blk0 + b, 0)),
                      pl.BlockSpec((1, DEC_SEQ, D_MODEL), lambda b, h, e, idx: (off + b, 0, 0),
                                   pipeline_mode=pl.Buffered(1))],
            out_specs=pl.BlockSpec((1, DEC_SEQ, D_MODEL), lambda b, h, e, idx: (b, 0, 0),
                                   pipeline_mode=pl.Buffered(1)),
        ),
        out_shape=jax.ShapeDtypeStruct((DEC_BATCH, DEC_SEQ, D_MODEL), F32),
        compiler_params=_params(("arbitrary", "arbitrary", "arbitrary")),
        name="scatter_lat",
    )(idx_flat, yg, x1_3)


def _rope_tables():
    pos = np.arange(DEC_SEQ)
    freq = (np.float32(ROPE_THETA) ** (-np.arange(ROPE_FREQS, dtype=np.float32) / np.float32(ROPE_FREQS)))
    ang_r = (pos // GRID_W).astype(np.float32)[:, None] * freq.astype(np.float32)
    ang_c = (pos % GRID_W).astype(np.float32)[:, None] * freq.astype(np.float32)
    cos = np.concatenate([np.cos(ang_r)] * 2 + [np.cos(ang_c)] * 2, axis=-1)
    sin = np.concatenate([-np.sin(ang_r), np.sin(ang_r), -np.sin(ang_c), np.sin(ang_c)], axis=-1)
    reps = LANE // HEAD_DIM
    cs = np.concatenate([np.ones((ROW_TILE, LANE)), np.tile(cos, (1, reps))], axis=0).astype(np.float32)
    sn = np.concatenate([np.zeros((ROW_TILE, LANE)), np.tile(sin, (1, reps))], axis=0).astype(np.float32)
    return cs, sn


def _rope_tile(i):
    lat = jnp.maximum(i - N_CTX_TILES, 0) % (DEC_SEQ // ROW_TILE)
    return jnp.where(i < N_CTX_TILES, 0, 1 + lat)


def _qk_gain(q_norm, k_norm):
    q = jnp.tile(q_norm, N_HEADS) * (HEAD_DIM ** -0.5 * LOG2_E)
    return jnp.concatenate([q, jnp.tile(k_norm, N_KV)])[None, :]


def kernel(x_prompt, x_sample, cache_a_k, cache_a_v, cache_c_k, cache_c_v, c, c_ctx, norm1_g, w_mod, b_mod, w_in,
           a_q_norm, a_k_norm, a_sink, b_v_norm, b_ws, b_bs, c_q_norm, c_k_norm, w_a_o, w_b_o, w_c_o, w_out, norm2_g,
           w_router, b_router, w_gate, w_up, w_down):
    cond8 = jnp.concatenate([c_ctx[None, :], c, jnp.zeros((8 - N_REQ, D_MODEL), F32)], axis=0)
    mods = _modulation(cond8, w_mod, b_mod).reshape(DEPTH, 8, 1, 6 * D_MODEL)

    cs, sn = _rope_tables()
    w_in_b = w_in.astype(BF16)
    wa_b, wb_b, wc_b, wo_b = w_a_o.astype(BF16), w_b_o.astype(BF16), w_c_o.astype(BF16), w_out.astype(BF16)
    ws_b = b_ws.astype(BF16)
    wr_pad = jnp.pad(w_router, ((0, 0), (0, 0), (0, LANE - N_EXPERTS))).astype(BF16)
    br_pad = jnp.pad(b_router, ((0, 0), (0, LANE - N_EXPERTS)), constant_values=NEG_BIG)

    by_seq = lambda a: a.reshape(T_ALL // SEQ, SEQ, a.shape[-1])
    by_dec = lambda a: a.reshape(T_ALL // DEC_SEQ, DEC_SEQ, a.shape[-1])
    lat_off = T_CTX // DEC_SEQ

    caches = [a.reshape(DEC_BATCH, DEPTH, PAST_LEN, KV_W).astype(BF16)
              for a in (cache_a_k, cache_a_v, cache_c_k, cache_c_v)]

    x_ctx = x_prompt.reshape(T_CTX, D_MODEL)
    x_lat = x_sample.reshape(T_LAT, D_MODEL)
    new_kv = [[], [], [], []]
    for l in range(DEPTH):
        qa, ka_b, va_b, nka, nva, bu, bv, qc, kc_b, vc_b, nkc, nvc, gt = _input_projection(
            x_ctx, x_lat, mods, l, norm1_g[l][None, :], w_in_b[l], cs, sn,
            _qk_gain(a_q_norm[l], a_k_norm[l]), _qk_gain(c_q_norm[l], c_k_norm[l]), b_v_norm[l][None, :])
        for lst, arr in zip(new_kv, (nka, nva, nkc, nvc)):
            lst.append(arr[:T_CTX].reshape(BATCH, SEQ, N_KV, HEAD_DIM))

        sink = a_sink[l]
        oa_ctx, oc_ctx = _context_attention(by_seq(qa), by_seq(ka_b), by_seq(va_b),
                                            by_seq(qc), by_seq(kc_b), by_seq(vc_b), sink)
        cak, cav, cck, ccv = (a[:, l] for a in caches)
        oa_lat = _window_attention(by_dec(qa), by_dec(ka_b), by_dec(va_b), cak, cav, sink,
                                   n_req=DEC_BATCH, off=lat_off)
        oc_lat = _dense_attention(by_dec(qc), by_dec(kc_b), by_dec(vc_b), (cck, ccv), None,
                                  n_req=DEC_BATCH, off=lat_off, tq=1024, key_chunk=1024)

        bs_full = jnp.repeat(b_bs[l].T, B_GROUP_CH, axis=1)
        x1, h2p, afft = _merge(x_ctx, x_lat, oa_ctx.reshape(T_CTX, Q_W), oa_lat.reshape(T_LAT, Q_W), bu, bv,
                               oc_ctx.reshape(T_CTX, Q_W), oc_lat.reshape(T_LAT, Q_W), gt,
                               wa_b[l], wb_b[l], wc_b[l], wo_b[l], ws_b[l], bs_full,
                               mods, l, norm2_g[l][None, :], wr_pad[l], br_pad[l][None, :])

        aff_rows = lambda a, n_req, n: a.reshape(N_EXPERTS, n_req, n).transpose(1, 0, 2).reshape(n_req * N_EXPERTS, n)
        idx_c, val_c = _select(aff_rows(afft[:, :T_CTX], BATCH, SEQ), BATCH * N_EXPERTS, CAP_CTX)
        _, val_l, idx_l_rows = _select(aff_rows(afft[:, T_CTX:], DEC_BATCH, DEC_SEQ), N_EXPERTS, CAP_LAT)
        idx_l_flat = idx_l_rows.reshape(-1)
        xg_ctx = _gather_ctx(idx_c, h2p)
        xg_lat = _gather_lat(idx_l_flat, by_dec(h2p), lat_off)
        yg = _expert_ffn(xg_ctx, xg_lat, val_c.reshape(BATCH, N_EXPERTS, CAP_CTX, 1),
                         val_l.reshape(DEC_BATCH, N_EXPERTS, CAP_LAT, 1), mods, w_gate, w_up, w_down, l)

        x_ctx = _scatter_ctx(idx_c, yg, x1)
        x_lat = _scatter_lat(idx_l_flat, yg, by_dec(x1), lat_off).reshape(T_LAT, D_MODEL)

    y_prompt = x_ctx.reshape(BATCH, SEQ, D_MODEL)
    y_sample = x_lat.reshape(DEC_BATCH, DEC_SEQ, D_MODEL)
    return (y_prompt, y_sample) + tuple(jnp.stack(lst, axis=1) for lst in new_kv)
```

```python
import functools

import jax
import numpy as np
import jax.numpy as jnp
from jax import lax
from jax.experimental import pallas as pl
from jax.experimental.pallas import tpu as pltpu

F32 = jnp.float32
BF16 = jnp.bfloat16
I32 = jnp.int32

D_MODEL = 1024
BATCH = 16
SEQ = 256
DEPTH = 2
DEC_BATCH = 2
DEC_SEQ = 4096
PAST_LEN = 256
GRID_W = 64
HEAD_DIM = 64
N_HEADS = 6
N_KV = 2
N_GRP = N_HEADS // N_KV
B_GROUPS = 4
B_GROUP_CH = 64
B_WIDTH = B_GROUPS * B_GROUP_CH
Q_W = N_HEADS * HEAD_DIM
KV_W = N_KV * HEAD_DIM
QK_W = Q_W + KV_W
N_BRANCH = 3
WINDOW = 128
BLOCK = 128
CHUNK = 128
N_EXPERTS = 16
EXPERT_FF = 1024
CAP_FACTOR = 2
ROPE_THETA = 10000.0
ROPE_FREQS = HEAD_DIM // 4
EPS = 1e-6
IN_WIDTH = 2 * (QK_W + KV_W) + 2 * B_WIDTH + N_BRANCH * D_MODEL

T_CTX = BATCH * SEQ
T_LAT = DEC_BATCH * DEC_SEQ
T_ALL = T_CTX + T_LAT
N_REQ = 1 + DEC_BATCH
CAP_CTX = CAP_FACTOR * SEQ // N_EXPERTS
CAP_LAT = CAP_FACTOR * DEC_SEQ // N_EXPERTS
ROWS_PER_EXPERT = BATCH * CAP_CTX + DEC_BATCH * CAP_LAT

LANE = 128
ROW_TILE = 512
N_CTX_TILES = T_CTX // ROW_TILE
FFN_ROW_TILE = 512
VMEM_LIMIT = 56 * 1024 * 1024
NEG_BIG = -1e30
LOG2_E = 1.4426950408889634

OFF_A = 0
OFF_AV = OFF_A + QK_W
OFF_BU = OFF_AV + KV_W
OFF_BV = OFF_BU + B_WIDTH
OFF_C = OFF_BV + B_WIDTH
OFF_CV = OFF_C + QK_W
OFF_G = OFF_CV + KV_W


def _params(sem, vmem=VMEM_LIMIT):
    return pltpu.CompilerParams(dimension_semantics=sem, vmem_limit_bytes=vmem)


def _sigmoid(x):
    return 1.0 / (1.0 + jnp.exp(-x))


def _gelu_tanh(x):
    return 0.5 * x * (1.0 + jnp.tanh(0.7978845608028654 * (x + 0.044715 * (x * x * x))))


def _split_bf16(x):
    hi = x.astype(BF16)
    lo = (x - hi.astype(F32)).astype(BF16)
    return hi, lo


def _mod_kernel(c_ref, w_ref, b_ref, o_ref):
    c = c_ref[...]
    s_hi, s_lo = _split_bf16(c * _sigmoid(c))
    w_hi, w_lo = _split_bf16(w_ref[0])
    acc = jnp.dot(s_hi, w_hi, preferred_element_type=F32)
    acc += jnp.dot(s_lo, w_hi, preferred_element_type=F32)
    acc += jnp.dot(s_hi, w_lo, preferred_element_type=F32)
    o_ref[0] = acc + b_ref[0]


def _modulation(cond8, w_mod, b_mod):
    n_col = 6 * D_MODEL // D_MODEL
    return pl.pallas_call(
        _mod_kernel,
        grid=(DEPTH, n_col),
        in_specs=[
            pl.BlockSpec((8, D_MODEL), lambda l, j: (0, 0)),
            pl.BlockSpec((1, D_MODEL, D_MODEL), lambda l, j: (l, 0, j)),
            pl.BlockSpec((1, 1, D_MODEL), lambda l, j: (l, 0, j)),
        ],
        out_specs=pl.BlockSpec((1, 8, D_MODEL), lambda l, j: (l, 0, j)),
        out_shape=jax.ShapeDtypeStruct((DEPTH, 8, 6 * D_MODEL), F32),
        compiler_params=_params(("arbitrary", "arbitrary")),
        name="modulation",
    )(cond8, w_mod, b_mod.reshape(DEPTH, 1, 6 * D_MODEL))


def _group_sumsq(y, bd_ref):
    return jnp.dot((y * y).astype(BF16), bd_ref[...], preferred_element_type=F32)


def _pick_pass(i, ctx_ref, lat_ref):
    return jnp.where(i < N_CTX_TILES, ctx_ref[...], lat_ref[...])


def _in_kernel(xc_ref, xl_ref, sc_ref, sh_ref, n1_ref, w_ref, cs_ref, sn_ref, ga_ref, gc_ref, gbv_ref, bd_qk_ref,
               bd_b_ref, qa_ref, ka_ref, va_ref, nka_ref, nva_ref, bu_ref, bv_ref, qc_ref, kc_ref, vc_ref, nkc_ref,
               nvc_ref, gt_ref):
    x = _pick_pass(pl.program_id(0), xc_ref, xl_ref)
    ms = jnp.mean(x * x, axis=-1, keepdims=True)
    h = x * lax.rsqrt(ms + EPS) * n1_ref[...]
    h = h * (1.0 + sc_ref[...]) + sh_ref[...]
    hb = h.astype(BF16)
    tm = x.shape[0]

    def proj(c0, width):
        return jnp.dot(hb, w_ref[:, c0:c0 + width], preferred_element_type=F32)

    cs = jnp.concatenate([cs_ref[...]] * (QK_W // LANE), axis=1)
    sn = jnp.concatenate([sn_ref[...]] * (QK_W // LANE), axis=1)
    lane = lax.broadcasted_iota(I32, (tm, QK_W), 1)
    first_half = (lane & ROPE_FREQS) == 0

    def qk_post(y, gain_ref):
        yn = y * lax.rsqrt(_group_sumsq(y, bd_qk_ref) * (1.0 / HEAD_DIM) + EPS) * gain_ref[...]
        partner = jnp.where(first_half, pltpu.roll(yn, QK_W - ROPE_FREQS, 1), pltpu.roll(yn, ROPE_FREQS, 1))
        return yn * cs + partner * sn

    def mixer(off_qk, off_v, gain_ref, q_ref, k_ref, v_ref, nk_ref, nv_ref):
        y = qk_post(proj(off_qk, QK_W), gain_ref)
        v = proj(off_v, KV_W)
        q_ref[...] = y[:, :Q_W].astype(BF16)
        k_ref[...] = y[:, Q_W:].astype(BF16)
        v_ref[...] = v.astype(BF16)

        nk_ref[...] = y[:, Q_W:]
        nv_ref[...] = v

    mixer(OFF_A, OFF_AV, ga_ref, qa_ref, ka_ref, va_ref, nka_ref, nva_ref)

    bu_ref[...] = _gelu_tanh(proj(OFF_BU, B_WIDTH)).astype(BF16)
    gv = _gelu_tanh(proj(OFF_BV, B_WIDTH))
    gvn = gv * lax.rsqrt(_group_sumsq(gv, bd_b_ref) * (1.0 / B_GROUP_CH) + EPS) * gbv_ref[...]
    bv_ref[...] = gvn.astype(BF16)

    mixer(OFF_C, OFF_CV, gc_ref, qc_ref, kc_ref, vc_ref, nkc_ref, nvc_ref)

    gate_chunk = 512
    for j in range(N_BRANCH * D_MODEL // gate_chunk):
        g = proj(OFF_G + j * gate_chunk, gate_chunk)
        gt_ref[:, j * gate_chunk:(j + 1) * gate_chunk] = _sigmoid(g).astype(BF16)


def _req_of_tile(i):
    return i // N_CTX_TILES


def _ctx_rows(w):
    return pl.BlockSpec((ROW_TILE, w), lambda i: (jnp.minimum(i, N_CTX_TILES - 1), 0))


def _lat_rows(w):
    return pl.BlockSpec((ROW_TILE, w), lambda i: (jnp.maximum(i - N_CTX_TILES, 0), 0))


MOD_SH1, MOD_SC1, MOD_G1, MOD_SH2, MOD_SC2, MOD_G2 = range(6)


def _layer_spec(stacked, layer):
    rest = stacked.shape[1:]
    return pl.BlockSpec((None,) + rest, lambda *g: (layer,) + (0,) * len(rest))


def _mod_spec(layer, chunk, req):
    return pl.BlockSpec((None, None, 1, D_MODEL), lambda *g: (layer, req(*g), 0, chunk))


def _block_diag_ones(width, group):
    g = np.arange(width) // group
    return (g[:, None] == g[None, :]).astype(np.float32)


def _input_projection(x_ctx, x_lat, mods, layer, n1, w_in_b, cs, sn, gain_a, gain_c, gain_bv):
    bd_qk = jnp.asarray(_block_diag_ones(QK_W, HEAD_DIM), BF16)
    bd_b = jnp.asarray(_block_diag_ones(B_WIDTH, B_GROUP_CH), BF16)
    tm = ROW_TILE
    row = lambda w: pl.BlockSpec((tm, w), lambda i: (i, 0))
    full = lambda a: pl.BlockSpec(a.shape, lambda i: (0,) * a.ndim)
    rope = pl.BlockSpec((tm, LANE), lambda i: (_rope_tile(i), 0))
    cache_rows = T_CTX + tm
    spare = lambda w: pl.BlockSpec((tm, w), lambda i: (jnp.minimum(i, N_CTX_TILES), 0))
    mixer_outs = [(Q_W, BF16, T_ALL), (KV_W, BF16, T_ALL), (KV_W, BF16, T_ALL), (KV_W, F32, cache_rows),
                  (KV_W, F32, cache_rows)]
    outs = mixer_outs + [(B_WIDTH, BF16, T_ALL), (B_WIDTH, BF16, T_ALL)] + mixer_outs + [(N_BRANCH * D_MODEL, BF16, T_ALL)]
    return pl.pallas_call(
        _in_kernel,
        grid=(T_ALL // tm,),
        in_specs=[_ctx_rows(D_MODEL), _lat_rows(D_MODEL), _mod_spec(layer, MOD_SC1, _req_of_tile),
                  _mod_spec(layer, MOD_SH1, _req_of_tile), full(n1), _layer_spec(w_in_b, layer), rope, rope,
                  full(gain_a), full(gain_c), full(gain_bv), full(bd_qk), full(bd_b)],
        out_specs=[row(w) if rows == T_ALL else spare(w) for w, _, rows in outs],
        out_shape=[jax.ShapeDtypeStruct((rows, w), dt) for w, dt, rows in outs],
        compiler_params=_params(("arbitrary",)),
        name="input_projection",
    )(x_ctx, x_lat, mods, mods, n1, w_in_b, cs, sn, gain_a, gain_c, gain_bv, bd_qk, bd_b)


def _attention_tile(q_ref, sources, sink_ref, o_ref, qt_scr, ot_scr, *, tq, key_chunk):
    width = N_GRP * tq
    for j in range(Q_W // LANE):
        qt_scr[j * LANE:(j + 1) * LANE, :] = q_ref[0, :, j * LANE:(j + 1) * LANE].astype(F32).T.astype(BF16)
    for kv in range(N_KV):
        lo, hi = kv * HEAD_DIM, (kv + 1) * HEAD_DIM
        heads = [kv * N_GRP + g for g in range(N_GRP)]
        qt = jnp.concatenate([qt_scr[h * HEAD_DIM:(h + 1) * HEAD_DIM, :] for h in heads], axis=1)

        def step(carry, kref, vref, c0, size, bias):
            m, acc = carry
            s = jnp.dot(kref[0, pl.ds(c0, size), lo:hi], qt, preferred_element_type=F32)
            if bias is not None:
                s = s + jnp.concatenate([bias] * N_GRP, axis=1)
            vt = vref[0, pl.ds(c0, size), :].astype(F32).T[lo:hi, :].astype(BF16)
            vt = jnp.concatenate([vt, jnp.ones((DEN_ROWS, size), BF16)], axis=0)
            m_new = jnp.maximum(m, jnp.max(s, axis=0, keepdims=True))
            p = jnp.exp2(s - m_new).astype(BF16)
            acc = acc * jnp.exp2(m - m_new) + jnp.dot(vt, p, preferred_element_type=F32)
            return m_new, acc

        if sink_ref is not None:
            m0 = jnp.concatenate([jnp.full((1, tq), sink_ref[h] * LOG2_E, F32) for h in heads], axis=1)
            den0 = jnp.ones((DEN_ROWS, width), F32)
        else:
            m0 = jnp.full((1, width), NEG_BIG, F32)
            den0 = jnp.zeros((DEN_ROWS, width), F32)
        carry = (m0, jnp.concatenate([jnp.zeros((HEAD_DIM, width), F32), den0], axis=0))
        for kref, vref, bias in sources:
            n_rows = kref.shape[1]
            n_full = n_rows // key_chunk
            if bias is not None:
                carry = step(carry, kref, vref, 0, n_rows, bias)
                continue
            if n_full > 1:
                carry = lax.fori_loop(
                    0, n_full,
                    lambda c, cr: step(cr, kref, vref, pl.multiple_of(c * key_chunk, key_chunk), key_chunk, None), carry)
            elif n_full == 1:
                carry = step(carry, kref, vref, 0, key_chunk, None)
            if n_rows - n_full * key_chunk:
                carry = step(carry, kref, vref, n_full * key_chunk, n_rows - n_full * key_chunk, None)
        _, acc = carry
        o = acc[:HEAD_DIM] / acc[HEAD_DIM:HEAD_DIM + 1]
        for g, h in enumerate(heads):
            ot_scr[h * HEAD_DIM:(h + 1) * HEAD_DIM, :] = o[:, g * tq:(g + 1) * tq]
    for j in range(Q_W // LANE):
        o_ref[0, :, j * LANE:(j + 1) * LANE] = ot_scr[j * LANE:(j + 1) * LANE, :].T.astype(o_ref.dtype)


def _dense_attn_kernel(*refs, tq, key_chunk, has_extra, has_sink):
    refs = list(refs)
    q_ref, k_ref, v_ref = refs[:3]
    del refs[:3]
    sources = [(k_ref, v_ref, None)]
    if has_extra:
        sources.append((refs.pop(0), refs.pop(0), None))
    sink_ref = refs.pop(0) if has_sink else None
    o_ref, qt_scr, ot_scr = refs
    _attention_tile(q_ref, sources, sink_ref, o_ref, qt_scr, ot_scr, tq=tq, key_chunk=key_chunk)


def _attention_scratch(tq):
    return [pltpu.VMEM((Q_W, tq), BF16), pltpu.VMEM((Q_W, tq), F32)]


def _dense_attention(q, k, v, extra, sink, *, n_req, off, tq, key_chunk):
    s = q.shape[1]
    kv_spec = pl.BlockSpec((1, s, KV_W), lambda i, j: (off + i, 0, 0))
    in_specs = [pl.BlockSpec((1, tq, Q_W), lambda i, j: (off + i, j, 0)), kv_spec, kv_spec]
    args = [q, k, v]
    if extra is not None:
        in_specs += [pl.BlockSpec((1, extra[0].shape[1], KV_W), lambda i, j: (i, 0, 0))] * 2
        args += list(extra)
    if sink is not None:
        in_specs.append(pl.BlockSpec(memory_space=pltpu.SMEM))
        args.append(sink)
    return pl.pallas_call(
        functools.partial(_dense_attn_kernel, tq=tq, key_chunk=key_chunk, has_extra=extra is not None,
                          has_sink=sink is not None),
        grid=(n_req, s // tq),
        in_specs=in_specs,
        out_specs=pl.BlockSpec((1, tq, Q_W), lambda i, j: (i, j, 0)),
        out_shape=jax.ShapeDtypeStruct((n_req, s, Q_W), BF16),
        scratch_shapes=_attention_scratch(tq),
        compiler_params=_params(("arbitrary", "arbitrary")),
        name="dense_attention",
    )(*args)


def _ctx_attn_kernel(qa_ref, ka_ref, va_ref, qc_ref, kc_ref, vc_ref, sink_ref, oa_ref, oc_ref, qta, ota, qtc, otc):
    _attention_tile(qa_ref, [(ka_ref, va_ref, None)], sink_ref, oa_ref, qta, ota, tq=SEQ, key_chunk=SEQ)
    _attention_tile(qc_ref, [(kc_ref, vc_ref, None)], None, oc_ref, qtc, otc, tq=SEQ, key_chunk=SEQ)


def _context_attention(qa, ka, va, qc, kc, vc, sink):
    q_spec = pl.BlockSpec((1, SEQ, Q_W), lambda i: (i, 0, 0))
    kv_spec = pl.BlockSpec((1, SEQ, KV_W), lambda i: (i, 0, 0))
    return pl.pallas_call(
        _ctx_attn_kernel,
        grid=(BATCH,),
        in_specs=[q_spec, kv_spec, kv_spec, q_spec, kv_spec, kv_spec, pl.BlockSpec(memory_space=pltpu.SMEM)],
        out_specs=[q_spec, q_spec],
        out_shape=[jax.ShapeDtypeStruct((BATCH, SEQ, Q_W), BF16)] * 2,
        scratch_shapes=_attention_scratch(SEQ) * 2,
        compiler_params=_params(("arbitrary",)),
        name="context_attention",
    )(qa, ka, va, qc, kc, vc, sink)


DEN_ROWS = 16
WINDOW_TQ = 512


def _window_attn_kernel(q_ref, kp_ref, kc_ref, kn_ref, vp_ref, vc_ref, vn_ref, ck_ref, cv_ref, bp_ref, bc_ref, bn_ref,
                        sink_ref, o_ref, qt_scr, ot_scr, *, seq):
    q_pos0 = pl.program_id(1) * WINDOW_TQ
    prev_bias = bp_ref[...] + jnp.where(q_pos0 >= BLOCK, 0.0, NEG_BIG)
    next_bias = bn_ref[...] + jnp.where(q_pos0 + WINDOW_TQ < seq, 0.0, NEG_BIG)
    sources = [(kp_ref, vp_ref, prev_bias), (kc_ref, vc_ref, bc_ref[...]), (kn_ref, vn_ref, next_bias),
               (ck_ref, cv_ref, None)]
    _attention_tile(q_ref, sources, sink_ref, o_ref, qt_scr, ot_scr, tq=WINDOW_TQ, key_chunk=WINDOW_TQ)


def _band_bias(first_key, n_keys):
    d = (first_key + np.arange(n_keys))[:, None] - np.arange(WINDOW_TQ)[None, :]
    return np.where(np.abs(d) <= WINDOW, 0.0, NEG_BIG).astype(np.float32)


def _window_attention(q, k, v, ck, cv, sink, *, n_req, off):
    b, s = n_req, q.shape[1]
    nb = s // BLOCK
    per_tile = WINDOW_TQ // BLOCK
    edge = lambda f: pl.BlockSpec((1, BLOCK, KV_W), lambda i, j: (off + i, f(j), 0))
    prev = lambda j: jnp.maximum(j * per_tile - 1, 0)
    nxt = lambda j: jnp.minimum((j + 1) * per_tile, nb - 1)
    cur = pl.BlockSpec((1, WINDOW_TQ, KV_W), lambda i, j: (off + i, j, 0))
    ctx = pl.BlockSpec((1, PAST_LEN, KV_W), lambda i, j: (i, 0, 0))
    biases = [_band_bias(-BLOCK, BLOCK), _band_bias(0, WINDOW_TQ), _band_bias(WINDOW_TQ, BLOCK)]
    table = lambda a: pl.BlockSpec(a.shape, lambda i, j: (0, 0))
    return pl.pallas_call(
        functools.partial(_window_attn_kernel, seq=s),
        grid=(b, s // WINDOW_TQ),
        in_specs=[pl.BlockSpec((1, WINDOW_TQ, Q_W), lambda i, j: (off + i, j, 0)),
                  edge(prev), cur, edge(nxt), edge(prev), cur, edge(nxt), ctx, ctx,
                  table(biases[0]), table(biases[1]), table(biases[2]),
                  pl.BlockSpec(memory_space=pltpu.SMEM)],
        out_specs=pl.BlockSpec((1, WINDOW_TQ, Q_W), lambda i, j: (i, j, 0)),
        out_shape=jax.ShapeDtypeStruct((b, s, Q_W), BF16),
        scratch_shapes=_attention_scratch(WINDOW_TQ),
        compiler_params=_params(("arbitrary", "arbitrary")),
        name="window_attention",
    )(q, k, k, k, v, v, v, ck, cv, *biases, sink)


def _pack_halves(x):
    half = x.shape[1] // 2
    return pltpu.pack_elementwise([x[:, :half], x[:, half:]], packed_dtype=BF16)


def _unpack_halves(words):
    return tuple(pltpu.unpack_elementwise(words, index=i, packed_dtype=BF16, unpacked_dtype=F32).astype(BF16)
                 for i in range(2))


def _merge_kernel(xc_ref, xl_ref, oac_ref, oal_ref, bu_ref, bv_ref, occ_ref, ocl_ref, gt_ref, wa_ref, wb_ref, wc_ref,
                  wo_ref, ws_ref, bs_ref, g1_ref, sc2_ref, sh2_ref, n2_ref, wr_ref, br_ref, x1_ref, h2p_ref, afft_ref):
    i = pl.program_id(0)
    tm = xc_ref.shape[0]
    group = lax.broadcasted_iota(I32, (CHUNK, B_WIDTH), 1) // B_GROUP_CH
    obs = []
    for c in range(tm // CHUNK):
        v = bv_ref[c * CHUNK:(c + 1) * CHUNK, :]
        sv = jnp.zeros((CHUNK, B_WIDTH), F32)
        for g in range(B_GROUPS):
            sv = jnp.where(group == g, jnp.dot(ws_ref[g], v, preferred_element_type=F32), sv)
        u = bu_ref[c * CHUNK:(c + 1) * CHUNK, :].astype(F32)
        obs.append((u * (sv + bs_ref[...])).astype(BF16))
    ob = jnp.concatenate(obs, axis=0)

    oa = _pick_pass(i, oac_ref, oal_ref)
    oc = _pick_pass(i, occ_ref, ocl_ref)
    merged = gt_ref[:, 0:D_MODEL].astype(F32) * jnp.dot(oa, wa_ref[...], preferred_element_type=F32)
    merged += gt_ref[:, D_MODEL:2 * D_MODEL].astype(F32) * jnp.dot(ob, wb_ref[...], preferred_element_type=F32)
    merged += gt_ref[:, 2 * D_MODEL:3 * D_MODEL].astype(F32) * jnp.dot(oc, wc_ref[...], preferred_element_type=F32)
    y = jnp.dot(merged.astype(BF16), wo_ref[...], preferred_element_type=F32)
    x1 = _pick_pass(i, xc_ref, xl_ref) + g1_ref[...] * y
    x1_ref[...] = x1

    ms = jnp.mean(x1 * x1, axis=-1, keepdims=True)
    h2 = x1 * lax.rsqrt(ms + EPS) * n2_ref[...]
    h2 = h2 * (1.0 + sc2_ref[...]) + sh2_ref[...]
    h2p_ref[...] = _pack_halves(h2)

    logits = jnp.dot(h2.astype(BF16), wr_ref[...], preferred_element_type=F32) + br_ref[...]
    e = jnp.exp(logits - jnp.max(logits, axis=-1, keepdims=True))
    aff = e / jnp.sum(e, axis=-1, keepdims=True)
    afft_ref[...] = aff.T[:N_EXPERTS, :]


def _merge(x_ctx, x_lat, oa_ctx, oa_lat, bu, bv, oc_ctx, oc_lat, gt, wa, wb, wc, wo, ws, bs, mods, layer, n2, wr, br):
    tm = ROW_TILE
    row = lambda w: pl.BlockSpec((tm, w), lambda i: (i, 0))
    full = lambda a: pl.BlockSpec(a.shape, lambda i: (0,) * a.ndim)
    mod = lambda chunk: _mod_spec(layer, chunk, _req_of_tile)
    stack = lambda a: _layer_spec(a, layer)
    return pl.pallas_call(
        _merge_kernel,
        grid=(T_ALL // tm,),
        in_specs=[_ctx_rows(D_MODEL), _lat_rows(D_MODEL), _ctx_rows(Q_W), _lat_rows(Q_W), row(B_WIDTH), row(B_WIDTH),
                  _ctx_rows(Q_W), _lat_rows(Q_W), row(N_BRANCH * D_MODEL),
                  stack(wa), stack(wb), stack(wc), stack(wo), stack(ws), full(bs),
                  mod(MOD_G1), mod(MOD_SC2), mod(MOD_SH2), full(n2), stack(wr), full(br)],
        out_specs=[row(D_MODEL), row(D_MODEL // 2), pl.BlockSpec((N_EXPERTS, tm), lambda i: (0, i))],
        out_shape=[jax.ShapeDtypeStruct((T_ALL, D_MODEL), F32), jax.ShapeDtypeStruct((T_ALL, D_MODEL // 2), jnp.uint32),
                   jax.ShapeDtypeStruct((N_EXPERTS, T_ALL), F32)],
        compiler_params=_params(("arbitrary",)),
        name="merge_router",
    )(x_ctx, x_lat, oa_ctx, oa_lat, bu, bv, oc_ctx, oc_lat, gt, wa, wb, wc, wo, ws, bs, mods, mods, mods, n2, wr, br)


def _select_kernel(aff_ref, idx_ref, val_ref, *rest, n, cap, row_chunk):
    idx_row_ref = rest[0] if len(rest) == 4 else None
    possel_ref, idx_scr, val_scr = rest[-3:]
    a = aff_ref[...]
    rows = a.shape[0]
    tok = lax.broadcasted_iota(I32, (rows, n), 1)

    def count(ones):
        return jnp.sum(ones, axis=1, keepdims=True)

    def at_least(word):
        return jnp.where(a >= pltpu.bitcast(word, F32), 1, 0)

    thr = jnp.zeros((rows, 1), I32)
    for bit in range(30, -1, -1):
        cand = thr | (1 << bit)
        thr = jnp.where(count(at_least(cand)) >= cap, cand, thr)
    above = at_least(thr + 1)
    tied = at_least(thr) - above
    need = cap - count(above)
    last = jnp.zeros((rows, 1), I32)
    for bit in range(n.bit_length() - 2, -1, -1):
        cand = last | (1 << bit)
        last = jnp.where(count(jnp.where(tok < cand, tied, 0)) < need, cand, last)
    sel = above + jnp.where(tok <= last, tied, 0)

    blk = min(n, 256)
    tri = jnp.where(lax.broadcasted_iota(I32, (blk, blk), 0) <= lax.broadcasted_iota(I32, (blk, blk), 1),
                    1.0, 0.0).astype(BF16)
    sel_f = sel.astype(F32)
    offset = jnp.zeros((rows, 1), F32)
    for j in range(n // blk):
        s_blk = sel_f[:, j * blk:(j + 1) * blk]
        incl = jnp.dot(s_blk.astype(BF16), tri, preferred_element_type=F32)
        pos = (incl - s_blk + offset).astype(I32)
        possel_ref[:, j * blk:(j + 1) * blk] = jnp.where(sel[:, j * blk:(j + 1) * blk] > 0, pos, -1)
        offset = offset + incl[:, blk - 1:blk]

    tb = min(n, TOKEN_BLOCK)
    n_blk = n // tb

    def fold_lanes(x):
        acc = x[:, :LANE]
        for k in range(1, tb // LANE):
            acc = acc + x[:, k * LANE:(k + 1) * LANE]
        return acc

    def match(e, slot, t0):
        hit = possel_ref[pl.ds(e, 1), pl.ds(t0, tb)] == slot
        tok = t0 + lax.broadcasted_iota(I32, (1, tb), 1)
        return (fold_lanes(jnp.where(hit, tok, 0)),
                fold_lanes(jnp.where(hit, aff_ref[pl.ds(e, 1), pl.ds(t0, tb)], 0.0)))

    def per_row(e, _):
        ends, run = [], 0
        for j in range(n_blk - 1):
            run = run + jnp.sum(jnp.where(possel_ref[pl.ds(e, 1), j * tb:(j + 1) * tb] >= 0, 1, 0))
            ends.append(run)

        def per_chunk(c, _):
            r0 = pl.multiple_of(c * row_chunk, row_chunk)
            slot = lax.broadcasted_iota(I32, (row_chunk, 1), 0) + r0
            if n_blk == 1:
                idx, val = match(e, slot, 0)
            else:
                first = sum(jnp.where(end <= r0, 1, 0) for end in ends)
                last = 1 + sum(jnp.where(end < r0 + row_chunk, 1, 0) for end in ends)

                def per_block(j, acc):
                    i, v = match(e, slot, pl.multiple_of(j * tb, tb))
                    return acc[0] + i, acc[1] + v

                idx, val = lax.fori_loop(first, last, per_block,
                                         (jnp.zeros((row_chunk, LANE), I32), jnp.zeros((row_chunk, LANE), F32)))
            idx_scr[pl.ds(r0, row_chunk), :] = idx
            val_scr[pl.ds(r0, row_chunk), :] = val
            return 0

        lax.fori_loop(0, cap // row_chunk, per_chunk, 0)
        idx = jnp.sum(idx_scr[...], axis=1, keepdims=True)
        idx_ref[e] = idx
        val_ref[e] = jnp.sum(val_scr[...], axis=1, keepdims=True)
        if idx_row_ref is not None:
            idx_row_ref[pl.ds(e, 1), :] = jnp.broadcast_to(idx.astype(F32), (cap, LANE)).T[0:1, :].astype(I32)
        return 0

    def per_small_row(e, _):
        idx, val = match(e, lax.broadcasted_iota(I32, (cap, 1), 0), 0)
        idx_ref[e] = jnp.sum(idx, axis=1, keepdims=True)
        val_ref[e] = jnp.sum(val, axis=1, keepdims=True)
        return 0

    if n_blk == 1 and cap == row_chunk:
        lax.fori_loop(0, rows, per_small_row, 0, unroll=4)
    else:
        lax.fori_loop(0, rows, per_row, 0)


def _select(aff_rows, rows_per_step, cap):
    r, n = aff_rows.shape
    row_chunk = min(cap, 64)
    out_specs = [pl.BlockSpec((rows_per_step, cap, 1), lambda s: (s, 0, 0))] * 2
    out_shape = [jax.ShapeDtypeStruct((r, cap, 1), I32), jax.ShapeDtypeStruct((r, cap, 1), F32)]
    if cap % LANE == 0:
        out_specs.append(pl.BlockSpec((rows_per_step, cap), lambda s: (s, 0)))
        out_shape.append(jax.ShapeDtypeStruct((r, cap), I32))
    return pl.pallas_call(
        functools.partial(_select_kernel, n=n, cap=cap, row_chunk=row_chunk),
        grid=(r // rows_per_step,),
        in_specs=[pl.BlockSpec((rows_per_step, n), lambda s: (s, 0))],
        out_specs=out_specs,
        out_shape=out_shape,
        scratch_shapes=[pltpu.VMEM((rows_per_step, n), I32), pltpu.VMEM((cap, LANE), I32), pltpu.VMEM((cap, LANE), F32)],
        compiler_params=_params(("arbitrary",)),
        name="expert_select",
    )(aff_rows)


CTX_SLOTS = N_EXPERTS * CAP_CTX
TOKEN_BLOCK = 512
SLOT_GROUP = 16


CTX_PER_STEP = 4


def _ctx_slot_onehot(idx, slots_on_rows):
    idx = idx.reshape(CTX_SLOTS, 1)
    if slots_on_rows:
        hit = idx == lax.broadcasted_iota(I32, (CTX_SLOTS, SEQ), 1)
    else:
        idx_lane = jnp.broadcast_to(idx.astype(F32), (CTX_SLOTS, LANE)).T[0:1, :]
        hit = idx_lane == lax.broadcasted_iota(I32, (SEQ, CTX_SLOTS), 0).astype(F32)
    return jnp.where(hit, 1.0, 0.0).astype(BF16)


def _gather_ctx_kernel(idx_ref, h_ref, out_ref):
    for r in range(CTX_PER_STEP):
        onehot = _ctx_slot_onehot(idx_ref[r * N_EXPERTS:(r + 1) * N_EXPERTS], True)
        lo, hi = _unpack_halves(h_ref[r * SEQ:(r + 1) * SEQ, :])
        g_lo = jnp.dot(onehot, lo, preferred_element_type=F32)
        g_hi = jnp.dot(onehot, hi, preferred_element_type=F32)
        packed = pltpu.pack_elementwise([g_lo, g_hi], packed_dtype=BF16)
        out_ref[:, r * CAP_CTX:(r + 1) * CAP_CTX, :] = packed.reshape(N_EXPERTS, CAP_CTX, D_MODEL // 2)


def _gather_ctx(idx_c, h2p):
    n = CTX_PER_STEP
    return pl.pallas_call(
        _gather_ctx_kernel,
        grid=(BATCH // n,),
        in_specs=[pl.BlockSpec((n * N_EXPERTS, CAP_CTX, 1), lambda b: (b, 0, 0)),
                  pl.BlockSpec((n * SEQ, D_MODEL // 2), lambda b: (b, 0))],
        out_specs=pl.BlockSpec((N_EXPERTS, n * CAP_CTX, D_MODEL // 2), lambda b: (0, b, 0)),
        out_shape=jax.ShapeDtypeStruct((N_EXPERTS, BATCH * CAP_CTX, D_MODEL // 2), jnp.uint32),
        compiler_params=_params(("arbitrary",)),
        name="gather_ctx",
    )(idx_c, h2p)


def _gather_lat_kernel(idx_ref, src_ref, out_ref):
    base = (pl.program_id(0) * N_EXPERTS + pl.program_id(1)) * CAP_LAT

    def body(it, _):
        r0 = pl.multiple_of(it * SLOT_GROUP, SLOT_GROUP)
        picked = [src_ref[0, pl.ds(idx_ref[base + r0 + k], 1), :] for k in range(SLOT_GROUP)]
        dst = out_ref.at[0, pl.ds(r0, SLOT_GROUP)]
        for k in range(SLOT_GROUP):
            dst[k:k + 1, :] = picked[k]
        return 0

    lax.fori_loop(0, CAP_LAT // SLOT_GROUP, body, 0)


def _gather_lat(idx_flat, h2p3, off):
    return pl.pallas_call(
        _gather_lat_kernel,
        grid_spec=pltpu.PrefetchScalarGridSpec(
            num_scalar_prefetch=1,
            grid=(DEC_BATCH, N_EXPERTS),
            in_specs=[pl.BlockSpec((1, DEC_SEQ, D_MODEL // 2), lambda b, e, idx: (off + b, 0, 0))],
            out_specs=pl.BlockSpec((1, CAP_LAT, D_MODEL // 2), lambda b, e, idx: (e, b, 0)),
        ),
        out_shape=jax.ShapeDtypeStruct((N_EXPERTS, DEC_BATCH * CAP_LAT, D_MODEL // 2), jnp.uint32),
        compiler_params=_params(("arbitrary", "arbitrary")),
        name="gather_lat",
    )(idx_flat, h2p3)


N_CTX_FFN_TILES = BATCH * CAP_CTX // FFN_ROW_TILE


def _ffn_kernel(xc_ref, xl_ref, vc_ref, vl_ref, g2_ref, wg_ref, wu_ref, wd_ref, o_ref, wg_b, wu_b, wd_b):
    j = pl.program_id(1)

    @pl.when(j == 0)
    def _():
        wg_b[...] = wg_ref[0].astype(BF16)
        wu_b[...] = wu_ref[0].astype(BF16)
        wd_b[...] = wd_ref[0].astype(BF16)

    is_ctx = j < N_CTX_FFN_TILES
    x = jnp.where(is_ctx, jnp.concatenate(_unpack_halves(xc_ref[0]), axis=1),
                  jnp.concatenate(_unpack_halves(xl_ref[0]), axis=1))
    g = jnp.dot(x, wg_b[...], preferred_element_type=F32)
    u = jnp.dot(x, wu_b[...], preferred_element_type=F32)
    hh = (g * _sigmoid(g)) * u
    y = jnp.dot(hh.astype(BF16), wd_b[...], preferred_element_type=F32)
    o_ref[0] = (y * jnp.where(is_ctx, vc_ref[...].reshape(FFN_ROW_TILE, 1), vl_ref[...])) * g2_ref[...]


def _expert_ffn(xg_ctx, xg_lat, val_ctx, val_lat, mods, w_gate, w_up, w_down, layer):
    tr = FFN_ROW_TILE
    assert tr == BATCH * CAP_CTX == CAP_LAT
    n_tiles = ROWS_PER_EXPERT // tr
    def wspec(k, n, tiles_held):
        ahead = lambda e, j: jnp.minimum(e + jnp.where(j >= tiles_held, 1, 0), N_EXPERTS - 1)
        return pl.BlockSpec((None, 1, k, n), lambda e, j: (layer, ahead(e, j), 0, 0))

    ctx_tile = lambda j: jnp.minimum(j, N_CTX_FFN_TILES - 1)
    lat_tile = lambda j: jnp.maximum(j - N_CTX_FFN_TILES, 0)
    return pl.pallas_call(
        _ffn_kernel,
        grid=(N_EXPERTS, n_tiles),
        in_specs=[pl.BlockSpec((1, tr, D_MODEL // 2), lambda e, j: (e, ctx_tile(j), 0)),
                  pl.BlockSpec((1, tr, D_MODEL // 2), lambda e, j: (e, lat_tile(j), 0)),
                  pl.BlockSpec((BATCH, None, CAP_CTX, 1), lambda e, j: (0, e, 0, 0)),
                  pl.BlockSpec((None, None, CAP_LAT, 1), lambda e, j: (lat_tile(j), e, 0, 0)),
                  _mod_spec(layer, MOD_G2, lambda e, j: j),
                  wspec(D_MODEL, EXPERT_FF, 1), wspec(D_MODEL, EXPERT_FF, n_tiles - 1),
                  wspec(EXPERT_FF, D_MODEL, n_tiles)],
        out_specs=pl.BlockSpec((1, tr, D_MODEL), lambda e, j: (e, j, 0)),
        out_shape=jax.ShapeDtypeStruct((N_EXPERTS, ROWS_PER_EXPERT, D_MODEL), F32),
        scratch_shapes=[pltpu.VMEM((D_MODEL, EXPERT_FF), BF16), pltpu.VMEM((D_MODEL, EXPERT_FF), BF16),
                        pltpu.VMEM((EXPERT_FF, D_MODEL), BF16)],
        compiler_params=_params(("arbitrary", "arbitrary")),
        name="expert_ffn",
    )(xg_ctx, xg_lat, val_ctx, val_lat, mods, w_gate, w_up, w_down)


def _scatter_ctx_kernel(idx_ref, y_ref, x1_ref, out_ref):
    for r in range(CTX_PER_STEP):
        onehot = _ctx_slot_onehot(idx_ref[r * N_EXPERTS:(r + 1) * N_EXPERTS], False)
        y_hi, y_lo = _split_bf16(y_ref[:, r * CAP_CTX:(r + 1) * CAP_CTX, :].reshape(CTX_SLOTS, D_MODEL))
        moe = jnp.dot(onehot, y_hi, preferred_element_type=F32) + jnp.dot(onehot, y_lo, preferred_element_type=F32)
        out_ref[r * SEQ:(r + 1) * SEQ, :] = x1_ref[r * SEQ:(r + 1) * SEQ, :] + moe


def _scatter_ctx(idx_c, yg, x1):
    n = CTX_PER_STEP
    return pl.pallas_call(
        _scatter_ctx_kernel,
        grid=(BATCH // n,),
        in_specs=[pl.BlockSpec((n * N_EXPERTS, CAP_CTX, 1), lambda b: (b, 0, 0)),
                  pl.BlockSpec((N_EXPERTS, n * CAP_CTX, D_MODEL), lambda b: (0, b, 0)),
                  pl.BlockSpec((n * SEQ, D_MODEL), lambda b: (b, 0))],
        out_specs=pl.BlockSpec((n * SEQ, D_MODEL), lambda b: (b, 0)),
        out_shape=jax.ShapeDtypeStruct((T_CTX, D_MODEL), F32),
        compiler_params=_params(("arbitrary",)),
        name="scatter_ctx",
    )(idx_c, yg, x1)


def _scatter_lat_kernel(idx_ref, y_ref, x1_ref, out_ref):
    e = pl.program_id(2)

    @pl.when(e == 0)
    def _():
        out_ref[...] = x1_ref[...]

    base = (pl.program_id(0) * N_EXPERTS + e) * CAP_LAT

    def body(it, _):
        r0 = pl.multiple_of(it * SLOT_GROUP, SLOT_GROUP)
        rows = [idx_ref[base + r0 + k] for k in range(SLOT_GROUP)]
        old = [out_ref[0, pl.ds(rows[k], 1), :] for k in range(SLOT_GROUP)]
        y = y_ref[0, pl.ds(r0, SLOT_GROUP), :]
        for k in range(SLOT_GROUP):
            out_ref[0, pl.ds(rows[k], 1), :] = old[k] + y[k:k + 1, :]
        return 0

    lax.fori_loop(0, CAP_LAT // SLOT_GROUP, body, 0)


def _scatter_lat(idx_flat, yg, x1_3, off):
    blk0 = BATCH * CAP_CTX // CAP_LAT
    return pl.pallas_call(
        _scatter_lat_kernel,
        grid_spec=pltpu.PrefetchScalarGridSpec(
            num_scalar_prefetch=1,
            grid=(DEC_BATCH, 1, N_EXPERTS),
            in_specs=[pl.BlockSpec((1, CAP_LAT, D_MODEL), lambda b, h, e, idx: (e, blk0 + b, 0)),
                      pl.BlockSpec((1, DEC_SEQ, D_MODEL), lambda b, h, e, idx: (off + b, 0, 0),
                                   pipeline_mode=pl.Buffered(1))],
            out_specs=pl.BlockSpec((1, DEC_SEQ, D_MODEL), lambda b, h, e, idx: (b, 0, 0),
                                   pipeline_mode=pl.Buffered(1)),
        ),
        out_shape=jax.ShapeDtypeStruct((DEC_BATCH, DEC_SEQ, D_MODEL), F32),
        compiler_params=_params(("arbitrary", "arbitrary", "arbitrary")),
        name="scatter_lat",
    )(idx_flat, yg, x1_3)


def _rope_tables():
    pos = np.arange(DEC_SEQ)
    freq = (np.float32(ROPE_THETA) ** (-np.arange(ROPE_FREQS, dtype=np.float32) / np.float32(ROPE_FREQS)))
    ang_r = (pos // GRID_W).astype(np.float32)[:, None] * freq.astype(np.float32)
    ang_c = (pos % GRID_W).astype(np.float32)[:, None] * freq.astype(np.float32)
    cos = np.concatenate([np.cos(ang_r)] * 2 + [np.cos(ang_c)] * 2, axis=-1)
    sin = np.concatenate([-np.sin(ang_r), np.sin(ang_r), -np.sin(ang_c), np.sin(ang_c)], axis=-1)
    reps = LANE // HEAD_DIM
    cs = np.concatenate([np.ones((ROW_TILE, LANE)), np.tile(cos, (1, reps))], axis=0).astype(np.float32)
    sn = np.concatenate([np.zeros((ROW_TILE, LANE)), np.tile(sin, (1, reps))], axis=0).astype(np.float32)
    return cs, sn


def _rope_tile(i):
    lat = jnp.maximum(i - N_CTX_TILES, 0) % (DEC_SEQ // ROW_TILE)
    return jnp.where(i < N_CTX_TILES, 0, 1 + lat)


def _qk_gain(q_norm, k_norm):
    q = jnp.tile(q_norm, N_HEADS) * (HEAD_DIM ** -0.5 * LOG2_E)
    return jnp.concatenate([q, jnp.tile(k_norm, N_KV)])[None, :]


def kernel(x_prompt, x_sample, cache_a_k, cache_a_v, cache_c_k, cache_c_v, c, c_ctx, norm1_g, w_mod, b_mod, w_in,
           a_q_norm, a_k_norm, a_sink, b_v_norm, b_ws, b_bs, c_q_norm, c_k_norm, w_a_o, w_b_o, w_c_o, w_out, norm2_g,
           w_router, b_router, w_gate, w_up, w_down):
    cond8 = jnp.concatenate([c_ctx[None, :], c, jnp.zeros((8 - N_REQ, D_MODEL), F32)], axis=0)
    mods = _modulation(cond8, w_mod, b_mod).reshape(DEPTH, 8, 1, 6 * D_MODEL)

    cs, sn = _rope_tables()
    w_in_b = w_in.astype(BF16)
    wa_b, wb_b, wc_b, wo_b = w_a_o.astype(BF16), w_b_o.astype(BF16), w_c_o.astype(BF16), w_out.astype(BF16)
    ws_b = b_ws.astype(BF16)
    wr_pad = jnp.pad(w_router, ((0, 0), (0, 0), (0, LANE - N_EXPERTS))).astype(BF16)
    br_pad = jnp.pad(b_router, ((0, 0), (0, LANE - N_EXPERTS)), constant_values=NEG_BIG)

    by_seq = lambda a: a.reshape(T_ALL // SEQ, SEQ, a.shape[-1])
    by_dec = lambda a: a.reshape(T_ALL // DEC_SEQ, DEC_SEQ, a.shape[-1])
    lat_off = T_CTX // DEC_SEQ

    caches = [a.reshape(DEC_BATCH, DEPTH, PAST_LEN, KV_W).astype(BF16)
              for a in (cache_a_k, cache_a_v, cache_c_k, cache_c_v)]

    x_ctx = x_prompt.reshape(T_CTX, D_MODEL)
    x_lat = x_sample.reshape(T_LAT, D_MODEL)
    new_kv = [[], [], [], []]
    for l in range(DEPTH):
        qa, ka_b, va_b, nka, nva, bu, bv, qc, kc_b, vc_b, nkc, nvc, gt = _input_projection(
            x_ctx, x_lat, mods, l, norm1_g[l][None, :], w_in_b, cs, sn,
            _qk_gain(a_q_norm[l], a_k_norm[l]), _qk_gain(c_q_norm[l], c_k_norm[l]), b_v_norm[l][None, :])
        for lst, arr in zip(new_kv, (nka, nva, nkc, nvc)):
            lst.append(arr[:T_CTX].reshape(BATCH, SEQ, N_KV, HEAD_DIM))

        sink = a_sink[l]
        oa_ctx, oc_ctx = _context_attention(by_seq(qa), by_seq(ka_b), by_seq(va_b),
                                            by_seq(qc), by_seq(kc_b), by_seq(vc_b), sink)
        cak, cav, cck, ccv = (a[:, l] for a in caches)
        oa_lat = _window_attention(by_dec(qa), by_dec(ka_b), by_dec(va_b), cak, cav, sink,
                                   n_req=DEC_BATCH, off=lat_off)
        oc_lat = _dense_attention(by_dec(qc), by_dec(kc_b), by_dec(vc_b), (cck, ccv), None,
                                  n_req=DEC_BATCH, off=lat_off, tq=1024, key_chunk=1024)

        bs_full = jnp.repeat(b_bs[l].T, B_GROUP_CH, axis=1)
        x1, h2p, afft = _merge(x_ctx, x_lat, oa_ctx.reshape(T_CTX, Q_W), oa_lat.reshape(T_LAT, Q_W), bu, bv,
                               oc_ctx.reshape(T_CTX, Q_W), oc_lat.reshape(T_LAT, Q_W), gt,
                               wa_b, wb_b, wc_b, wo_b, ws_b, bs_full,
                               mods, l, norm2_g[l][None, :], wr_pad, br_pad[l][None, :])

        aff_rows = lambda a, n_req, n: a.reshape(N_EXPERTS, n_req, n).transpose(1, 0, 2).reshape(n_req * N_EXPERTS, n)
        idx_c, val_c = _select(aff_rows(afft[:, :T_CTX], BATCH, SEQ), BATCH * N_EXPERTS, CAP_CTX)
        _, val_l, idx_l_rows = _select(aff_rows(afft[:, T_CTX:], DEC_BATCH, DEC_SEQ), N_EXPERTS, CAP_LAT)
        idx_l_flat = idx_l_rows.reshape(-1)
        xg_ctx = _gather_ctx(idx_c, h2p)
        xg_lat = _gather_lat(idx_l_flat, by_dec(h2p), lat_off)
        yg = _expert_ffn(xg_ctx, xg_lat, val_c.reshape(BATCH, N_EXPERTS, CAP_CTX, 1),
                         val_l.reshape(DEC_BATCH, N_EXPERTS, CAP_LAT, 1), mods, w_gate, w_up, w_down, l)

        x_ctx = _scatter_ctx(idx_c, yg, x1)
        x_lat = _scatter_lat(idx_l_flat, yg, by_dec(x1), lat_off).reshape(T_LAT, D_MODEL)

    y_prompt = x_ctx.reshape(BATCH, SEQ, D_MODEL)
    y_sample = x_lat.reshape(DEC_BATCH, DEC_SEQ, D_MODEL)
    return (y_prompt, y_sample) + tuple(jnp.stack(lst, axis=1) for lst in new_kv)
```

```python
import functools

import jax
import numpy as np
import jax.numpy as jnp
from jax import lax
from jax.experimental import pallas as pl
from jax.experimental.pallas import tpu as pltpu

F32 = jnp.float32
BF16 = jnp.bfloat16
I32 = jnp.int32

D_MODEL = 1024
BATCH = 16
SEQ = 256
DEPTH = 2
DEC_BATCH = 2
DEC_SEQ = 4096
PAST_LEN = 256
GRID_W = 64
HEAD_DIM = 64
N_HEADS = 6
N_KV = 2
N_GRP = N_HEADS // N_KV
B_GROUPS = 4
B_GROUP_CH = 64
B_WIDTH = B_GROUPS * B_GROUP_CH
Q_W = N_HEADS * HEAD_DIM
KV_W = N_KV * HEAD_DIM
QK_W = Q_W + KV_W
N_BRANCH = 3
WINDOW = 128
BLOCK = 128
CHUNK = 128
N_EXPERTS = 16
EXPERT_FF = 1024
CAP_FACTOR = 2
ROPE_THETA = 10000.0
ROPE_FREQS = HEAD_DIM // 4
EPS = 1e-6
IN_WIDTH = 2 * (QK_W + KV_W) + 2 * B_WIDTH + N_BRANCH * D_MODEL

T_CTX = BATCH * SEQ
T_LAT = DEC_BATCH * DEC_SEQ
T_ALL = T_CTX + T_LAT
N_REQ = 1 + DEC_BATCH
CAP_CTX = CAP_FACTOR * SEQ // N_EXPERTS
CAP_LAT = CAP_FACTOR * DEC_SEQ // N_EXPERTS
ROWS_PER_EXPERT = BATCH * CAP_CTX + DEC_BATCH * CAP_LAT

LANE = 128
ROW_TILE = 512
N_CTX_TILES = T_CTX // ROW_TILE
FFN_ROW_TILE = 512
VMEM_LIMIT = 56 * 1024 * 1024
NEG_BIG = -1e30
LOG2_E = 1.4426950408889634

OFF_A = 0
OFF_AV = OFF_A + QK_W
OFF_BU = OFF_AV + KV_W
OFF_BV = OFF_BU + B_WIDTH
OFF_C = OFF_BV + B_WIDTH
OFF_CV = OFF_C + QK_W
OFF_G = OFF_CV + KV_W


def _params(sem, vmem=VMEM_LIMIT):
    return pltpu.CompilerParams(dimension_semantics=sem, vmem_limit_bytes=vmem)


def _sigmoid(x):
    return 1.0 / (1.0 + jnp.exp(-x))


def _gelu_tanh(x):
    return 0.5 * x * (1.0 + jnp.tanh(0.7978845608028654 * (x + 0.044715 * (x * x * x))))


def _split_bf16(x):
    hi = x.astype(BF16)
    lo = (x - hi.astype(F32)).astype(BF16)
    return hi, lo


def _mod_kernel(c_ref, w_ref, b_ref, o_ref):
    c = c_ref[...]
    s_hi, s_lo = _split_bf16(c * _sigmoid(c))
    w_hi, w_lo = _split_bf16(w_ref[0])
    acc = jnp.dot(s_hi, w_hi, preferred_element_type=F32)
    acc += jnp.dot(s_lo, w_hi, preferred_element_type=F32)
    acc += jnp.dot(s_hi, w_lo, preferred_element_type=F32)
    o_ref[0] = acc + b_ref[0]


def _modulation(cond8, w_mod, b_mod):
    n_col = 6 * D_MODEL // D_MODEL
    return pl.pallas_call(
        _mod_kernel,
        grid=(DEPTH, n_col),
        in_specs=[
            pl.BlockSpec((8, D_MODEL), lambda l, j: (0, 0)),
            pl.BlockSpec((1, D_MODEL, D_MODEL), lambda l, j: (l, 0, j)),
            pl.BlockSpec((1, 1, D_MODEL), lambda l, j: (l, 0, j)),
        ],
        out_specs=pl.BlockSpec((1, 8, D_MODEL), lambda l, j: (l, 0, j)),
        out_shape=jax.ShapeDtypeStruct((DEPTH, 8, 6 * D_MODEL), F32),
        compiler_params=_params(("arbitrary", "arbitrary")),
        name="modulation",
    )(cond8, w_mod, b_mod.reshape(DEPTH, 1, 6 * D_MODEL))


def _group_sumsq(y, bd_ref):
    return jnp.dot((y * y).astype(BF16), bd_ref[...], preferred_element_type=F32)


def _pick_pass(i, ctx_ref, lat_ref):
    return jnp.where(i < N_CTX_TILES, ctx_ref[...], lat_ref[...])


def _in_kernel(xc_ref, xl_ref, sc_ref, sh_ref, n1_ref, w_ref, cs_ref, sn_ref, ga_ref, gc_ref, gbv_ref, bd_qk_ref,
               bd_b_ref, qa_ref, ka_ref, va_ref, nka_ref, nva_ref, bu_ref, bv_ref, qc_ref, kc_ref, vc_ref, nkc_ref,
               nvc_ref, gt_ref):
    x = _pick_pass(pl.program_id(0), xc_ref, xl_ref)
    ms = jnp.mean(x * x, axis=-1, keepdims=True)
    h = x * lax.rsqrt(ms + EPS) * n1_ref[...]
    h = h * (1.0 + sc_ref[...]) + sh_ref[...]
    hb = h.astype(BF16)
    tm = x.shape[0]

    def proj(c0, width):
        return jnp.dot(hb, w_ref[:, c0:c0 + width], preferred_element_type=F32)

    cs = jnp.concatenate([cs_ref[...]] * (QK_W // LANE), axis=1)
    sn = jnp.concatenate([sn_ref[...]] * (QK_W // LANE), axis=1)
    lane = lax.broadcasted_iota(I32, (tm, QK_W), 1)
    first_half = (lane & ROPE_FREQS) == 0

    def qk_post(y, gain_ref):
        yn = y * lax.rsqrt(_group_sumsq(y, bd_qk_ref) * (1.0 / HEAD_DIM) + EPS) * gain_ref[...]
        partner = jnp.where(first_half, pltpu.roll(yn, QK_W - ROPE_FREQS, 1), pltpu.roll(yn, ROPE_FREQS, 1))
        return yn * cs + partner * sn

    def mixer(off_qk, off_v, gain_ref, q_ref, k_ref, v_ref, nk_ref, nv_ref):
        y = qk_post(proj(off_qk, QK_W), gain_ref)
        v = proj(off_v, KV_W)
        q_ref[...] = y[:, :Q_W].astype(BF16)
        k_ref[...] = y[:, Q_W:].astype(BF16)
        v_ref[...] = v.astype(BF16)

        nk_ref[...] = y[:, Q_W:]
        nv_ref[...] = v

    mixer(OFF_A, OFF_AV, ga_ref, qa_ref, ka_ref, va_ref, nka_ref, nva_ref)

    bu_ref[...] = _gelu_tanh(proj(OFF_BU, B_WIDTH)).astype(BF16)
    gv = _gelu_tanh(proj(OFF_BV, B_WIDTH))
    gvn = gv * lax.rsqrt(_group_sumsq(gv, bd_b_ref) * (1.0 / B_GROUP_CH) + EPS) * gbv_ref[...]
    bv_ref[...] = gvn.astype(BF16)

    mixer(OFF_C, OFF_CV, gc_ref, qc_ref, kc_ref, vc_ref, nkc_ref, nvc_ref)

    gate_chunk = 512
    for j in range(N_BRANCH * D_MODEL // gate_chunk):
        g = proj(OFF_G + j * gate_chunk, gate_chunk)
        gt_ref[:, j * gate_chunk:(j + 1) * gate_chunk] = _sigmoid(g).astype(BF16)


def _req_of_tile(i):
    return i // N_CTX_TILES


def _ctx_rows(w):
    return pl.BlockSpec((ROW_TILE, w), lambda i: (jnp.minimum(i, N_CTX_TILES - 1), 0))


def _lat_rows(w):
    return pl.BlockSpec((ROW_TILE, w), lambda i: (jnp.maximum(i - N_CTX_TILES, 0), 0))


MOD_SH1, MOD_SC1, MOD_G1, MOD_SH2, MOD_SC2, MOD_G2 = range(6)


def _layer_spec(stacked, layer):
    rest = stacked.shape[1:]
    return pl.BlockSpec((None,) + rest, lambda *g: (layer,) + (0,) * len(rest))


def _mod_spec(layer, chunk, req):
    return pl.BlockSpec((None, None, 1, D_MODEL), lambda *g: (layer, req(*g), 0, chunk))


def _block_diag_ones(width, group):
    g = np.arange(width) // group
    return (g[:, None] == g[None, :]).astype(np.float32)


def _input_projection(x_ctx, x_lat, mods, layer, n1, w_in_b, cs, sn, gain_a, gain_c, gain_bv):
    bd_qk = jnp.asarray(_block_diag_ones(QK_W, HEAD_DIM), BF16)
    bd_b = jnp.asarray(_block_diag_ones(B_WIDTH, B_GROUP_CH), BF16)
    tm = ROW_TILE
    row = lambda w: pl.BlockSpec((tm, w), lambda i: (i, 0))
    full = lambda a: pl.BlockSpec(a.shape, lambda i: (0,) * a.ndim)
    rope = pl.BlockSpec((tm, LANE), lambda i: (_rope_tile(i), 0))
    cache_rows = T_CTX + tm
    spare = lambda w: pl.BlockSpec((tm, w), lambda i: (jnp.minimum(i, N_CTX_TILES), 0))
    mixer_outs = [(Q_W, BF16, T_ALL), (KV_W, BF16, T_ALL), (KV_W, BF16, T_ALL), (KV_W, F32, cache_rows),
                  (KV_W, F32, cache_rows)]
    outs = mixer_outs + [(B_WIDTH, BF16, T_ALL), (B_WIDTH, BF16, T_ALL)] + mixer_outs + [(N_BRANCH * D_MODEL, BF16, T_ALL)]
    return pl.pallas_call(
        _in_kernel,
        grid=(T_ALL // tm,),
        in_specs=[_ctx_rows(D_MODEL), _lat_rows(D_MODEL), _mod_spec(layer, MOD_SC1, _req_of_tile),
                  _mod_spec(layer, MOD_SH1, _req_of_tile), full(n1), _layer_spec(w_in_b, layer), rope, rope,
                  full(gain_a), full(gain_c), full(gain_bv), full(bd_qk), full(bd_b)],
        out_specs=[row(w) if rows == T_ALL else spare(w) for w, _, rows in outs],
        out_shape=[jax.ShapeDtypeStruct((rows, w), dt) for w, dt, rows in outs],
        compiler_params=_params(("arbitrary",)),
        name="input_projection",
    )(x_ctx, x_lat, mods, mods, n1, w_in_b, cs, sn, gain_a, gain_c, gain_bv, bd_qk, bd_b)


def _attention_tile(q_ref, sources, sink_ref, o_ref, qt_scr, ot_scr, *, tq, key_chunk):
    width = N_GRP * tq
    for j in range(Q_W // LANE):
        qt_scr[j * LANE:(j + 1) * LANE, :] = q_ref[0, :, j * LANE:(j + 1) * LANE].astype(F32).T.astype(BF16)
    for kv in range(N_KV):
        lo, hi = kv * HEAD_DIM, (kv + 1) * HEAD_DIM
        heads = [kv * N_GRP + g for g in range(N_GRP)]
        qt = jnp.concatenate([qt_scr[h * HEAD_DIM:(h + 1) * HEAD_DIM, :] for h in heads], axis=1)

        def step(carry, kref, vref, c0, size, bias):
            m, acc = carry
            s = jnp.dot(kref[0, pl.ds(c0, size), lo:hi], qt, preferred_element_type=F32)
            if bias is not None:
                s = s + jnp.concatenate([bias] * N_GRP, axis=1)
            vt = vref[0, pl.ds(c0, size), :].astype(F32).T[lo:hi, :].astype(BF16)
            vt = jnp.concatenate([vt, jnp.ones((DEN_ROWS, size), BF16)], axis=0)
            m_new = jnp.maximum(m, jnp.max(s, axis=0, keepdims=True))
            p = jnp.exp2(s - m_new).astype(BF16)
            acc = acc * jnp.exp2(m - m_new) + jnp.dot(vt, p, preferred_element_type=F32)
            return m_new, acc

        if sink_ref is not None:
            m0 = jnp.concatenate([jnp.full((1, tq), sink_ref[h] * LOG2_E, F32) for h in heads], axis=1)
            den0 = jnp.ones((DEN_ROWS, width), F32)
        else:
            m0 = jnp.full((1, width), NEG_BIG, F32)
            den0 = jnp.zeros((DEN_ROWS, width), F32)
        carry = (m0, jnp.concatenate([jnp.zeros((HEAD_DIM, width), F32), den0], axis=0))
        for kref, vref, bias in sources:
            n_rows = kref.shape[1]
            n_full = n_rows // key_chunk
            if bias is not None:
                carry = step(carry, kref, vref, 0, n_rows, bias)
                continue
            if n_full > 1:
                carry = lax.fori_loop(
                    0, n_full,
                    lambda c, cr: step(cr, kref, vref, pl.multiple_of(c * key_chunk, key_chunk), key_chunk, None), carry)
            elif n_full == 1:
                carry = step(carry, kref, vref, 0, key_chunk, None)
            if n_rows - n_full * key_chunk:
                carry = step(carry, kref, vref, n_full * key_chunk, n_rows - n_full * key_chunk, None)
        _, acc = carry
        o = acc[:HEAD_DIM] / acc[HEAD_DIM:HEAD_DIM + 1]
        for g, h in enumerate(heads):
            ot_scr[h * HEAD_DIM:(h + 1) * HEAD_DIM, :] = o[:, g * tq:(g + 1) * tq]
    for j in range(Q_W // LANE):
        o_ref[0, :, j * LANE:(j + 1) * LANE] = ot_scr[j * LANE:(j + 1) * LANE, :].T.astype(o_ref.dtype)


def _dense_attn_kernel(*refs, tq, key_chunk, has_extra, has_sink):
    refs = list(refs)
    q_ref, k_ref, v_ref = refs[:3]
    del refs[:3]
    sources = [(k_ref, v_ref, None)]
    if has_extra:
        sources.append((refs.pop(0), refs.pop(0), None))
    sink_ref = refs.pop(0) if has_sink else None
    o_ref, qt_scr, ot_scr = refs
    _attention_tile(q_ref, sources, sink_ref, o_ref, qt_scr, ot_scr, tq=tq, key_chunk=key_chunk)


def _attention_scratch(tq):
    return [pltpu.VMEM((Q_W, tq), BF16), pltpu.VMEM((Q_W, tq), F32)]


def _dense_attention(q, k, v, extra, sink, *, n_req, off, tq, key_chunk):
    s = q.shape[1]
    kv_spec = pl.BlockSpec((1, s, KV_W), lambda i, j: (off + i, 0, 0))
    in_specs = [pl.BlockSpec((1, tq, Q_W), lambda i, j: (off + i, j, 0)), kv_spec, kv_spec]
    args = [q, k, v]
    if extra is not None:
        in_specs += [pl.BlockSpec((1, extra[0].shape[1], KV_W), lambda i, j: (i, 0, 0))] * 2
        args += list(extra)
    if sink is not None:
        in_specs.append(pl.BlockSpec(memory_space=pltpu.SMEM))
        args.append(sink)
    return pl.pallas_call(
        functools.partial(_dense_attn_kernel, tq=tq, key_chunk=key_chunk, has_extra=extra is not None,
                          has_sink=sink is not None),
        grid=(n_req, s // tq),
        in_specs=in_specs,
        out_specs=pl.BlockSpec((1, tq, Q_W), lambda i, j: (i, j, 0)),
        out_shape=jax.ShapeDtypeStruct((n_req, s, Q_W), BF16),
        scratch_shapes=_attention_scratch(tq),
        compiler_params=_params(("arbitrary", "arbitrary")),
        name="dense_attention",
    )(*args)


def _ctx_attn_kernel(qa_ref, ka_ref, va_ref, qc_ref, kc_ref, vc_ref, sink_ref, oa_ref, oc_ref, qta, ota, qtc, otc):
    _attention_tile(qa_ref, [(ka_ref, va_ref, None)], sink_ref, oa_ref, qta, ota, tq=SEQ, key_chunk=SEQ)
    _attention_tile(qc_ref, [(kc_ref, vc_ref, None)], None, oc_ref, qtc, otc, tq=SEQ, key_chunk=SEQ)


def _context_attention(qa, ka, va, qc, kc, vc, sink):
    q_spec = pl.BlockSpec((1, SEQ, Q_W), lambda i: (i, 0, 0))
    kv_spec = pl.BlockSpec((1, SEQ, KV_W), lambda i: (i, 0, 0))
    return pl.pallas_call(
        _ctx_attn_kernel,
        grid=(BATCH,),
        in_specs=[q_spec, kv_spec, kv_spec, q_spec, kv_spec, kv_spec, pl.BlockSpec(memory_space=pltpu.SMEM)],
        out_specs=[q_spec, q_spec],
        out_shape=[jax.ShapeDtypeStruct((BATCH, SEQ, Q_W), BF16)] * 2,
        scratch_shapes=_attention_scratch(SEQ) * 2,
        compiler_params=_params(("arbitrary",)),
        name="context_attention",
    )(qa, ka, va, qc, kc, vc, sink)


DEN_ROWS = 16
WINDOW_TQ = 512


def _window_attn_kernel(q_ref, kp_ref, kc_ref, kn_ref, vp_ref, vc_ref, vn_ref, ck_ref, cv_ref, bp_ref, bc_ref, bn_ref,
                        sink_ref, o_ref, qt_scr, ot_scr, *, seq):
    q_pos0 = pl.program_id(1) * WINDOW_TQ
    prev_bias = bp_ref[...] + jnp.where(q_pos0 >= BLOCK, 0.0, NEG_BIG)
    next_bias = bn_ref[...] + jnp.where(q_pos0 + WINDOW_TQ < seq, 0.0, NEG_BIG)
    sources = [(kp_ref, vp_ref, prev_bias), (kc_ref, vc_ref, bc_ref[...]), (kn_ref, vn_ref, next_bias),
               (ck_ref, cv_ref, None)]
    _attention_tile(q_ref, sources, sink_ref, o_ref, qt_scr, ot_scr, tq=WINDOW_TQ, key_chunk=WINDOW_TQ)


def _band_bias(first_key, n_keys):
    d = (first_key + np.arange(n_keys))[:, None] - np.arange(WINDOW_TQ)[None, :]
    return np.where(np.abs(d) <= WINDOW, 0.0, NEG_BIG).astype(np.float32)


def _window_attention(q, k, v, ck, cv, sink, *, n_req, off):
    b, s = n_req, q.shape[1]
    nb = s // BLOCK
    per_tile = WINDOW_TQ // BLOCK
    edge = lambda f: pl.BlockSpec((1, BLOCK, KV_W), lambda i, j: (off + i, f(j), 0))
    prev = lambda j: jnp.maximum(j * per_tile - 1, 0)
    nxt = lambda j: jnp.minimum((j + 1) * per_tile, nb - 1)
    cur = pl.BlockSpec((1, WINDOW_TQ, KV_W), lambda i, j: (off + i, j, 0))
    ctx = pl.BlockSpec((1, PAST_LEN, KV_W), lambda i, j: (i, 0, 0))
    biases = [_band_bias(-BLOCK, BLOCK), _band_bias(0, WINDOW_TQ), _band_bias(WINDOW_TQ, BLOCK)]
    table = lambda a: pl.BlockSpec(a.shape, lambda i, j: (0, 0))
    return pl.pallas_call(
        functools.partial(_window_attn_kernel, seq=s),
        grid=(b, s // WINDOW_TQ),
        in_specs=[pl.BlockSpec((1, WINDOW_TQ, Q_W), lambda i, j: (off + i, j, 0)),
                  edge(prev), cur, edge(nxt), edge(prev), cur, edge(nxt), ctx, ctx,
                  table(biases[0]), table(biases[1]), table(biases[2]),
                  pl.BlockSpec(memory_space=pltpu.SMEM)],
        out_specs=pl.BlockSpec((1, WINDOW_TQ, Q_W), lambda i, j: (i, j, 0)),
        out_shape=jax.ShapeDtypeStruct((b, s, Q_W), BF16),
        scratch_shapes=_attention_scratch(WINDOW_TQ),
        compiler_params=_params(("arbitrary", "arbitrary")),
        name="window_attention",
    )(q, k, k, k, v, v, v, ck, cv, *biases, sink)


def _pack_halves(x):
    half = x.shape[1] // 2
    return pltpu.pack_elementwise([x[:, :half], x[:, half:]], packed_dtype=BF16)


def _unpack_halves(words):
    return tuple(pltpu.unpack_elementwise(words, index=i, packed_dtype=BF16, unpacked_dtype=F32).astype(BF16)
                 for i in range(2))


def _merge_kernel(xc_ref, xl_ref, oac_ref, oal_ref, bu_ref, bv_ref, occ_ref, ocl_ref, gt_ref, wa_ref, wb_ref, wc_ref,
                  wo_ref, ws_ref, bs_ref, g1_ref, sc2_ref, sh2_ref, n2_ref, wr_ref, br_ref, x1_ref, h2p_ref, afft_ref):
    i = pl.program_id(0)
    tm = xc_ref.shape[0]
    group = lax.broadcasted_iota(I32, (CHUNK, B_WIDTH), 1) // B_GROUP_CH
    obs = []
    for c in range(tm // CHUNK):
        v = bv_ref[c * CHUNK:(c + 1) * CHUNK, :]
        sv = jnp.zeros((CHUNK, B_WIDTH), F32)
        for g in range(B_GROUPS):
            sv = jnp.where(group == g, jnp.dot(ws_ref[g], v, preferred_element_type=F32), sv)
        u = bu_ref[c * CHUNK:(c + 1) * CHUNK, :].astype(F32)
        obs.append((u * (sv + bs_ref[...])).astype(BF16))
    ob = jnp.concatenate(obs, axis=0)

    oa = _pick_pass(i, oac_ref, oal_ref)
    oc = _pick_pass(i, occ_ref, ocl_ref)
    merged = gt_ref[:, 0:D_MODEL].astype(F32) * jnp.dot(oa, wa_ref[...], preferred_element_type=F32)
    merged += gt_ref[:, D_MODEL:2 * D_MODEL].astype(F32) * jnp.dot(ob, wb_ref[...], preferred_element_type=F32)
    merged += gt_ref[:, 2 * D_MODEL:3 * D_MODEL].astype(F32) * jnp.dot(oc, wc_ref[...], preferred_element_type=F32)
    y = jnp.dot(merged.astype(BF16), wo_ref[...], preferred_element_type=F32)
    x1 = _pick_pass(i, xc_ref, xl_ref) + g1_ref[...] * y
    x1_ref[...] = x1

    ms = jnp.mean(x1 * x1, axis=-1, keepdims=True)
    h2 = x1 * lax.rsqrt(ms + EPS) * n2_ref[...]
    h2 = h2 * (1.0 + sc2_ref[...]) + sh2_ref[...]
    h2p_ref[...] = _pack_halves(h2)

    logits = jnp.dot(h2.astype(BF16), wr_ref[...], preferred_element_type=F32) + br_ref[...]
    e = jnp.exp(logits - jnp.max(logits, axis=-1, keepdims=True))
    aff = e / jnp.sum(e, axis=-1, keepdims=True)
    afft_ref[...] = aff.T[:N_EXPERTS, :]


def _merge(x_ctx, x_lat, oa_ctx, oa_lat, bu, bv, oc_ctx, oc_lat, gt, wa, wb, wc, wo, ws, bs, mods, layer, n2, wr, br):
    tm = ROW_TILE
    row = lambda w: pl.BlockSpec((tm, w), lambda i: (i, 0))
    full = lambda a: pl.BlockSpec(a.shape, lambda i: (0,) * a.ndim)
    mod = lambda chunk: _mod_spec(layer, chunk, _req_of_tile)
    stack = lambda a: _layer_spec(a, layer)
    return pl.pallas_call(
        _merge_kernel,
        grid=(T_ALL // tm,),
        in_specs=[_ctx_rows(D_MODEL), _lat_rows(D_MODEL), _ctx_rows(Q_W), _lat_rows(Q_W), row(B_WIDTH), row(B_WIDTH),
                  _ctx_rows(Q_W), _lat_rows(Q_W), row(N_BRANCH * D_MODEL),
                  stack(wa), stack(wb), stack(wc), stack(wo), stack(ws), full(bs),
                  mod(MOD_G1), mod(MOD_SC2), mod(MOD_SH2), full(n2), stack(wr), full(br)],
        out_specs=[row(D_MODEL), row(D_MODEL // 2), pl.BlockSpec((N_EXPERTS, tm), lambda i: (0, i))],
        out_shape=[jax.ShapeDtypeStruct((T_ALL, D_MODEL), F32), jax.ShapeDtypeStruct((T_ALL, D_MODEL // 2), jnp.uint32),
                   jax.ShapeDtypeStruct((N_EXPERTS, T_ALL), F32)],
        compiler_params=_params(("arbitrary",)),
        name="merge_router",
    )(x_ctx, x_lat, oa_ctx, oa_lat, bu, bv, oc_ctx, oc_lat, gt, wa, wb, wc, wo, ws, bs, mods, mods, mods, n2, wr, br)


def _select_kernel(aff_ref, idx_ref, val_ref, *rest, n, cap, row_chunk):
    idx_row_ref = rest[0] if len(rest) == 4 else None
    possel_ref, idx_scr, val_scr = rest[-3:]
    a = aff_ref[...]
    rows = a.shape[0]
    tok = lax.broadcasted_iota(I32, (rows, n), 1)

    def count(ones):
        return jnp.sum(ones, axis=1, keepdims=True)

    def at_least(word):
        return jnp.where(a >= pltpu.bitcast(word, F32), 1, 0)

    thr = jnp.zeros((rows, 1), I32)
    for bit in range(30, -1, -1):
        cand = thr | (1 << bit)
        thr = jnp.where(count(at_least(cand)) >= cap, cand, thr)
    above = at_least(thr + 1)
    tied = at_least(thr) - above
    need = cap - count(above)
    last = jnp.zeros((rows, 1), I32)
    for bit in range(n.bit_length() - 2, -1, -1):
        cand = last | (1 << bit)
        last = jnp.where(count(jnp.where(tok < cand, tied, 0)) < need, cand, last)
    sel = above + jnp.where(tok <= last, tied, 0)

    blk = min(n, 256)
    tri = jnp.where(lax.broadcasted_iota(I32, (blk, blk), 0) <= lax.broadcasted_iota(I32, (blk, blk), 1),
                    1.0, 0.0).astype(BF16)
    sel_f = sel.astype(F32)
    offset = jnp.zeros((rows, 1), F32)
    for j in range(n // blk):
        s_blk = sel_f[:, j * blk:(j + 1) * blk]
        incl = jnp.dot(s_blk.astype(BF16), tri, preferred_element_type=F32)
        pos = (incl - s_blk + offset).astype(I32)
        possel_ref[:, j * blk:(j + 1) * blk] = jnp.where(sel[:, j * blk:(j + 1) * blk] > 0, pos, -1)
        offset = offset + incl[:, blk - 1:blk]

    tb = min(n, TOKEN_BLOCK)
    n_blk = n // tb

    def fold_lanes(x):
        acc = x[:, :LANE]
        for k in range(1, tb // LANE):
            acc = acc + x[:, k * LANE:(k + 1) * LANE]
        return acc

    def match(e, slot, t0):
        hit = possel_ref[pl.ds(e, 1), pl.ds(t0, tb)] == slot
        tok = t0 + lax.broadcasted_iota(I32, (1, tb), 1)
        return (fold_lanes(jnp.where(hit, tok, 0)),
                fold_lanes(jnp.where(hit, aff_ref[pl.ds(e, 1), pl.ds(t0, tb)], 0.0)))

    def per_row(e, _):
        ends, run = [], 0
        for j in range(n_blk - 1):
            run = run + jnp.sum(jnp.where(possel_ref[pl.ds(e, 1), j * tb:(j + 1) * tb] >= 0, 1, 0))
            ends.append(run)

        def per_chunk(c, _):
            r0 = pl.multiple_of(c * row_chunk, row_chunk)
            slot = lax.broadcasted_iota(I32, (row_chunk, 1), 0) + r0
            if n_blk == 1:
                idx, val = match(e, slot, 0)
            else:
                first = sum(jnp.where(end <= r0, 1, 0) for end in ends)
                last = 1 + sum(jnp.where(end < r0 + row_chunk, 1, 0) for end in ends)

                def per_block(j, acc):
                    i, v = match(e, slot, pl.multiple_of(j * tb, tb))
                    return acc[0] + i, acc[1] + v

                idx, val = lax.fori_loop(first, last, per_block,
                                         (jnp.zeros((row_chunk, LANE), I32), jnp.zeros((row_chunk, LANE), F32)))
            idx_scr[pl.ds(r0, row_chunk), :] = idx
            val_scr[pl.ds(r0, row_chunk), :] = val
            return 0

        lax.fori_loop(0, cap // row_chunk, per_chunk, 0)
        idx = jnp.sum(idx_scr[...], axis=1, keepdims=True)
        idx_ref[e] = idx
        val_ref[e] = jnp.sum(val_scr[...], axis=1, keepdims=True)
        if idx_row_ref is not None:
            idx_row_ref[pl.ds(e, 1), :] = jnp.broadcast_to(idx.astype(F32), (cap, LANE)).T[0:1, :].astype(I32)
        return 0

    def per_small_row(e, _):
        idx, val = match(e, lax.broadcasted_iota(I32, (cap, 1), 0), 0)
        idx_ref[e] = jnp.sum(idx, axis=1, keepdims=True)
        val_ref[e] = jnp.sum(val, axis=1, keepdims=True)
        return 0

    if n_blk == 1 and cap == row_chunk:
        lax.fori_loop(0, rows, per_small_row, 0, unroll=4)
    else:
        lax.fori_loop(0, rows, per_row, 0)


def _select(aff_rows, rows_per_step, cap):
    r, n = aff_rows.shape
    row_chunk = min(cap, 64)
    out_specs = [pl.BlockSpec((rows_per_step, cap, 1), lambda s: (s, 0, 0))] * 2
    out_shape = [jax.ShapeDtypeStruct((r, cap, 1), I32), jax.ShapeDtypeStruct((r, cap, 1), F32)]
    if cap % LANE == 0:
        out_specs.append(pl.BlockSpec((rows_per_step, cap), lambda s: (s, 0)))
        out_shape.append(jax.ShapeDtypeStruct((r, cap), I32))
    return pl.pallas_call(
        functools.partial(_select_kernel, n=n, cap=cap, row_chunk=row_chunk),
        grid=(r // rows_per_step,),
        in_specs=[pl.BlockSpec((rows_per_step, n), lambda s: (s, 0))],
        out_specs=out_specs,
        out_shape=out_shape,
        scratch_shapes=[pltpu.VMEM((rows_per_step, n), I32), pltpu.VMEM((cap, LANE), I32), pltpu.VMEM((cap, LANE), F32)],
        compiler_params=_params(("arbitrary",)),
        name="expert_select",
    )(aff_rows)


CTX_SLOTS = N_EXPERTS * CAP_CTX
TOKEN_BLOCK = 512
SLOT_GROUP = 16


CTX_PER_STEP = 4


def _ctx_slot_onehot(idx, slots_on_rows):
    idx = idx.reshape(CTX_SLOTS, 1)
    if slots_on_rows:
        hit = idx == lax.broadcasted_iota(I32, (CTX_SLOTS, SEQ), 1)
    else:
        idx_lane = jnp.broadcast_to(idx.astype(F32), (CTX_SLOTS, LANE)).T[0:1, :]
        hit = idx_lane == lax.broadcasted_iota(I32, (SEQ, CTX_SLOTS), 0).astype(F32)
    return jnp.where(hit, 1.0, 0.0).astype(BF16)


def _gather_ctx_kernel(idx_ref, h_ref, out_ref):
    for r in range(CTX_PER_STEP):
        onehot = _ctx_slot_onehot(idx_ref[r * N_EXPERTS:(r + 1) * N_EXPERTS], True)
        lo, hi = _unpack_halves(h_ref[r * SEQ:(r + 1) * SEQ, :])
        g_lo = jnp.dot(onehot, lo, preferred_element_type=F32)
        g_hi = jnp.dot(onehot, hi, preferred_element_type=F32)
        packed = pltpu.pack_elementwise([g_lo, g_hi], packed_dtype=BF16)
        out_ref[:, r * CAP_CTX:(r + 1) * CAP_CTX, :] = packed.reshape(N_EXPERTS, CAP_CTX, D_MODEL // 2)


def _gather_ctx(idx_c, h2p):
    n = CTX_PER_STEP
    return pl.pallas_call(
        _gather_ctx_kernel,
        grid=(BATCH // n,),
        in_specs=[pl.BlockSpec((n * N_EXPERTS, CAP_CTX, 1), lambda b: (b, 0, 0)),
                  pl.BlockSpec((n * SEQ, D_MODEL // 2), lambda b: (b, 0))],
        out_specs=pl.BlockSpec((N_EXPERTS, n * CAP_CTX, D_MODEL // 2), lambda b: (0, b, 0)),
        out_shape=jax.ShapeDtypeStruct((N_EXPERTS, BATCH * CAP_CTX, D_MODEL // 2), jnp.uint32),
        compiler_params=_params(("arbitrary",)),
        name="gather_ctx",
    )(idx_c, h2p)


def _gather_lat_kernel(idx_ref, src_ref, out_ref):
    base = (pl.program_id(0) * N_EXPERTS + pl.program_id(1)) * CAP_LAT

    def body(it, _):
        r0 = pl.multiple_of(it * SLOT_GROUP, SLOT_GROUP)
        picked = [src_ref[0, pl.ds(idx_ref[base + r0 + k], 1), :] for k in range(SLOT_GROUP)]
        dst = out_ref.at[0, pl.ds(r0, SLOT_GROUP)]
        for k in range(SLOT_GROUP):
            dst[k:k + 1, :] = picked[k]
        return 0

    lax.fori_loop(0, CAP_LAT // SLOT_GROUP, body, 0)


def _gather_lat(idx_flat, h2p3, off):
    return pl.pallas_call(
        _gather_lat_kernel,
        grid_spec=pltpu.PrefetchScalarGridSpec(
            num_scalar_prefetch=1,
            grid=(DEC_BATCH, N_EXPERTS),
            in_specs=[pl.BlockSpec((1, DEC_SEQ, D_MODEL // 2), lambda b, e, idx: (off + b, 0, 0))],
            out_specs=pl.BlockSpec((1, CAP_LAT, D_MODEL // 2), lambda b, e, idx: (e, b, 0)),
        ),
        out_shape=jax.ShapeDtypeStruct((N_EXPERTS, DEC_BATCH * CAP_LAT, D_MODEL // 2), jnp.uint32),
        compiler_params=_params(("arbitrary", "arbitrary")),
        name="gather_lat",
    )(idx_flat, h2p3)


N_CTX_FFN_TILES = BATCH * CAP_CTX // FFN_ROW_TILE


def _ffn_kernel(xc_ref, xl_ref, vc_ref, vl_ref, g2_ref, wg_ref, wu_ref, wd_ref, o_ref, wg_b, wu_b, wd_b):
    j = pl.program_id(1)

    @pl.when(j == 0)
    def _():
        wg_b[...] = wg_ref[0].astype(BF16)
        wu_b[...] = wu_ref[0].astype(BF16)
        wd_b[...] = wd_ref[0].astype(BF16)

    is_ctx = j < N_CTX_FFN_TILES
    x = jnp.where(is_ctx, jnp.concatenate(_unpack_halves(xc_ref[0]), axis=1),
                  jnp.concatenate(_unpack_halves(xl_ref[0]), axis=1))
    g = jnp.dot(x, wg_b[...], preferred_element_type=F32)
    u = jnp.dot(x, wu_b[...], preferred_element_type=F32)
    hh = (g * _sigmoid(g)) * u
    y = jnp.dot(hh.astype(BF16), wd_b[...], preferred_element_type=F32)
    o_ref[0] = (y * jnp.where(is_ctx, vc_ref[...].reshape(FFN_ROW_TILE, 1), vl_ref[...])) * g2_ref[...]


def _expert_ffn(xg_ctx, xg_lat, val_ctx, val_lat, mods, w_gate, w_up, w_down, layer):
    tr = FFN_ROW_TILE
    assert tr == BATCH * CAP_CTX == CAP_LAT
    n_tiles = ROWS_PER_EXPERT // tr
    def wspec(k, n, tiles_held):
        ahead = lambda e, j: jnp.minimum(e + jnp.where(j >= tiles_held, 1, 0), N_EXPERTS - 1)
        return pl.BlockSpec((None, 1, k, n), lambda e, j: (layer, ahead(e, j), 0, 0))

    ctx_tile = lambda j: jnp.minimum(j, N_CTX_FFN_TILES - 1)
    lat_tile = lambda j: jnp.maximum(j - N_CTX_FFN_TILES, 0)
    return pl.pallas_call(
        _ffn_kernel,
        grid=(N_EXPERTS, n_tiles),
        in_specs=[pl.BlockSpec((1, tr, D_MODEL // 2), lambda e, j: (e, ctx_tile(j), 0)),
                  pl.BlockSpec((1, tr, D_MODEL // 2), lambda e, j: (e, lat_tile(j), 0)),
                  pl.BlockSpec((BATCH, None, CAP_CTX, 1), lambda e, j: (0, e, 0, 0)),
                  pl.BlockSpec((None, None, CAP_LAT, 1), lambda e, j: (lat_tile(j), e, 0, 0)),
                  _mod_spec(layer, MOD_G2, lambda e, j: j),
                  wspec(D_MODEL, EXPERT_FF, 1), wspec(D_MODEL, EXPERT_FF, n_tiles - 1),
                  wspec(EXPERT_FF, D_MODEL, n_tiles)],
        out_specs=pl.BlockSpec((1, tr, D_MODEL), lambda e, j: (e, j, 0)),
        out_shape=jax.ShapeDtypeStruct((N_EXPERTS, ROWS_PER_EXPERT, D_MODEL), F32),
        scratch_shapes=[pltpu.VMEM((D_MODEL, EXPERT_FF), BF16), pltpu.VMEM((D_MODEL, EXPERT_FF), BF16),
                        pltpu.VMEM((EXPERT_FF, D_MODEL), BF16)],
        compiler_params=_params(("arbitrary", "arbitrary")),
        name="expert_ffn",
    )(xg_ctx, xg_lat, val_ctx, val_lat, mods, w_gate, w_up, w_down)


def _scatter_ctx_kernel(idx_ref, y_ref, x1_ref, out_ref):
    for r in range(CTX_PER_STEP):
        onehot = _ctx_slot_onehot(idx_ref[r * N_EXPERTS:(r + 1) * N_EXPERTS], False)
        y_hi, y_lo = _split_bf16(y_ref[:, r * CAP_CTX:(r + 1) * CAP_CTX, :].reshape(CTX_SLOTS, D_MODEL))
        moe = jnp.dot(onehot, y_hi, preferred_element_type=F32) + jnp.dot(onehot, y_lo, preferred_element_type=F32)
        out_ref[r * SEQ:(r + 1) * SEQ, :] = x1_ref[r * SEQ:(r + 1) * SEQ, :] + moe


def _scatter_ctx(idx_c, yg, x1):
    n = CTX_PER_STEP
    return pl.pallas_call(
        _scatter_ctx_kernel,
        grid=(BATCH // n,),
        in_specs=[pl.BlockSpec((n * N_EXPERTS, CAP_CTX, 1), lambda b: (b, 0, 0)),
                  pl.BlockSpec((N_EXPERTS, n * CAP_CTX, D_MODEL), lambda b: (0, b, 0)),
                  pl.BlockSpec((n * SEQ, D_MODEL), lambda b: (b, 0))],
        out_specs=pl.BlockSpec((n * SEQ, D_MODEL), lambda b: (b, 0)),
        out_shape=jax.ShapeDtypeStruct((T_CTX, D_MODEL), F32),
        compiler_params=_params(("arbitrary",)),
        name="scatter_ctx",
    )(idx_c, yg, x1)


def _scatter_lat_kernel(idx_ref, y_ref, x1_hbm, out_ref, sem, *, off):
    b, e = pl.program_id(0), pl.program_id(2)

    @pl.when(e == 0)
    def _():
        load = pltpu.make_async_copy(x1_hbm.at[pl.ds(off + b, 1)], out_ref, sem)
        load.start()
        load.wait()

    base = (b * N_EXPERTS + e) * CAP_LAT

    def body(it, _):
        r0 = pl.multiple_of(it * SLOT_GROUP, SLOT_GROUP)
        rows = [idx_ref[base + r0 + k] for k in range(SLOT_GROUP)]
        old = [out_ref[0, pl.ds(rows[k], 1), :] for k in range(SLOT_GROUP)]
        y = y_ref[0, pl.ds(r0, SLOT_GROUP), :]
        for k in range(SLOT_GROUP):
            out_ref[0, pl.ds(rows[k], 1), :] = old[k] + y[k:k + 1, :]
        return 0

    lax.fori_loop(0, CAP_LAT // SLOT_GROUP, body, 0)


def _scatter_lat(idx_flat, yg, x1_3, off):
    blk0 = BATCH * CAP_CTX // CAP_LAT
    return pl.pallas_call(
        functools.partial(_scatter_lat_kernel, off=off),
        grid_spec=pltpu.PrefetchScalarGridSpec(
            num_scalar_prefetch=1,
            grid=(DEC_BATCH, 1, N_EXPERTS),
            in_specs=[pl.BlockSpec((1, CAP_LAT, D_MODEL), lambda b, h, e, idx: (e, blk0 + b, 0)),
                      pl.BlockSpec(memory_space=pl.ANY)],
            out_specs=pl.BlockSpec((1, DEC_SEQ, D_MODEL), lambda b, h, e, idx: (b, 0, 0)),
            scratch_shapes=[pltpu.SemaphoreType.DMA(())],
        ),
        out_shape=jax.ShapeDtypeStruct((DEC_BATCH, DEC_SEQ, D_MODEL), F32),
        compiler_params=_params(("arbitrary", "arbitrary", "arbitrary")),
        name="scatter_lat",
    )(idx_flat, yg, x1_3)


def _rope_tables():
    pos = np.arange(DEC_SEQ)
    freq = (np.float32(ROPE_THETA) ** (-np.arange(ROPE_FREQS, dtype=np.float32) / np.float32(ROPE_FREQS)))
    ang_r = (pos // GRID_W).astype(np.float32)[:, None] * freq.astype(np.float32)
    ang_c = (pos % GRID_W).astype(np.float32)[:, None] * freq.astype(np.float32)
    cos = np.concatenate([np.cos(ang_r)] * 2 + [np.cos(ang_c)] * 2, axis=-1)
    sin = np.concatenate([-np.sin(ang_r), np.sin(ang_r), -np.sin(ang_c), np.sin(ang_c)], axis=-1)
    reps = LANE // HEAD_DIM
    cs = np.concatenate([np.ones((ROW_TILE, LANE)), np.tile(cos, (1, reps))], axis=0).astype(np.float32)
    sn = np.concatenate([np.zeros((ROW_TILE, LANE)), np.tile(sin, (1, reps))], axis=0).astype(np.float32)
    return cs, sn


def _rope_tile(i):
    lat = jnp.maximum(i - N_CTX_TILES, 0) % (DEC_SEQ // ROW_TILE)
    return jnp.where(i < N_CTX_TILES, 0, 1 + lat)


def _qk_gain(q_norm, k_norm):
    q = jnp.tile(q_norm, N_HEADS) * (HEAD_DIM ** -0.5 * LOG2_E)
    return jnp.concatenate([q, jnp.tile(k_norm, N_KV)])[None, :]


def kernel(x_prompt, x_sample, cache_a_k, cache_a_v, cache_c_k, cache_c_v, c, c_ctx, norm1_g, w_mod, b_mod, w_in,
           a_q_norm, a_k_norm, a_sink, b_v_norm, b_ws, b_bs, c_q_norm, c_k_norm, w_a_o, w_b_o, w_c_o, w_out, norm2_g,
           w_router, b_router, w_gate, w_up, w_down):
    cond8 = jnp.concatenate([c_ctx[None, :], c, jnp.zeros((8 - N_REQ, D_MODEL), F32)], axis=0)
    mods = _modulation(cond8, w_mod, b_mod).reshape(DEPTH, 8, 1, 6 * D_MODEL)

    cs, sn = _rope_tables()
    w_in_b = w_in.astype(BF16)
    wa_b, wb_b, wc_b, wo_b = w_a_o.astype(BF16), w_b_o.astype(BF16), w_c_o.astype(BF16), w_out.astype(BF16)
    ws_b = b_ws.astype(BF16)
    wr_pad = jnp.pad(w_router, ((0, 0), (0, 0), (0, LANE - N_EXPERTS))).astype(BF16)
    br_pad = jnp.pad(b_router, ((0, 0), (0, LANE - N_EXPERTS)), constant_values=NEG_BIG)

    by_seq = lambda a: a.reshape(T_ALL // SEQ, SEQ, a.shape[-1])
    by_dec = lambda a: a.reshape(T_ALL // DEC_SEQ, DEC_SEQ, a.shape[-1])
    lat_off = T_CTX // DEC_SEQ

    caches = [a.reshape(DEC_BATCH, DEPTH, PAST_LEN, KV_W).astype(BF16)
              for a in (cache_a_k, cache_a_v, cache_c_k, cache_c_v)]

    x_ctx = x_prompt.reshape(T_CTX, D_MODEL)
    x_lat = x_sample.reshape(T_LAT, D_MODEL)
    new_kv = [[], [], [], []]
    for l in range(DEPTH):
        qa, ka_b, va_b, nka, nva, bu, bv, qc, kc_b, vc_b, nkc, nvc, gt = _input_projection(
            x_ctx, x_lat, mods, l, norm1_g[l][None, :], w_in_b, cs, sn,
            _qk_gain(a_q_norm[l], a_k_norm[l]), _qk_gain(c_q_norm[l], c_k_norm[l]), b_v_norm[l][None, :])
        for lst, arr in zip(new_kv, (nka, nva, nkc, nvc)):
            lst.append(arr[:T_CTX].reshape(BATCH, SEQ, N_KV, HEAD_DIM))

        sink = a_sink[l]
        oa_ctx, oc_ctx = _context_attention(by_seq(qa), by_seq(ka_b), by_seq(va_b),
                                            by_seq(qc), by_seq(kc_b), by_seq(vc_b), sink)
        cak, cav, cck, ccv = (a[:, l] for a in caches)
        oa_lat = _window_attention(by_dec(qa), by_dec(ka_b), by_dec(va_b), cak, cav, sink,
                                   n_req=DEC_BATCH, off=lat_off)
        oc_lat = _dense_attention(by_dec(qc), by_dec(kc_b), by_dec(vc_b), (cck, ccv), None,
                                  n_req=DEC_BATCH, off=lat_off, tq=1024, key_chunk=1024)

        bs_full = jnp.repeat(b_bs[l].T, B_GROUP_CH, axis=1)
        x1, h2p, afft = _merge(x_ctx, x_lat, oa_ctx.reshape(T_CTX, Q_W), oa_lat.reshape(T_LAT, Q_W), bu, bv,
                               oc_ctx.reshape(T_CTX, Q_W), oc_lat.reshape(T_LAT, Q_W), gt,
                               wa_b, wb_b, wc_b, wo_b, ws_b, bs_full,
                               mods, l, norm2_g[l][None, :], wr_pad, br_pad[l][None, :])

        aff_rows = lambda a, n_req, n: a.reshape(N_EXPERTS, n_req, n).transpose(1, 0, 2).reshape(n_req * N_EXPERTS, n)
        idx_c, val_c = _select(aff_rows(afft[:, :T_CTX], BATCH, SEQ), BATCH * N_EXPERTS, CAP_CTX)
        _, val_l, idx_l_rows = _select(aff_rows(afft[:, T_CTX:], DEC_BATCH, DEC_SEQ), N_EXPERTS, CAP_LAT)
        idx_l_flat = idx_l_rows.reshape(-1)
        xg_ctx = _gather_ctx(idx_c, h2p)
        xg_lat = _gather_lat(idx_l_flat, by_dec(h2p), lat_off)
        yg = _expert_ffn(xg_ctx, xg_lat, val_c.reshape(BATCH, N_EXPERTS, CAP_CTX, 1),
                         val_l.reshape(DEC_BATCH, N_EXPERTS, CAP_LAT, 1), mods, w_gate, w_up, w_down, l)

        x_ctx = _scatter_ctx(idx_c, yg, x1)
        x_lat = _scatter_lat(idx_l_flat, yg, by_dec(x1), lat_off).reshape(T_LAT, D_MODEL)

    y_prompt = x_ctx.reshape(BATCH, SEQ, D_MODEL)
    y_sample = x_lat.reshape(DEC_BATCH, DEC_SEQ, D_MODEL)
    return (y_prompt, y_sample) + tuple(jnp.stack(lst, axis=1) for lst in new_kv)
```

```python
import functools

import jax
import numpy as np
import jax.numpy as jnp
from jax import lax
from jax.experimental import pallas as pl
from jax.experimental.pallas import tpu as pltpu

F32 = jnp.float32
BF16 = jnp.bfloat16
I32 = jnp.int32

D_MODEL = 1024
BATCH = 16
SEQ = 256
DEPTH = 2
DEC_BATCH = 2
DEC_SEQ = 4096
PAST_LEN = 256
GRID_W = 64
HEAD_DIM = 64
N_HEADS = 6
N_KV = 2
N_GRP = N_HEADS // N_KV
B_GROUPS = 4
B_GROUP_CH = 64
B_WIDTH = B_GROUPS * B_GROUP_CH
Q_W = N_HEADS * HEAD_DIM
KV_W = N_KV * HEAD_DIM
QK_W = Q_W + KV_W
N_BRANCH = 3
WINDOW = 128
BLOCK = 128
CHUNK = 128
N_EXPERTS = 16
EXPERT_FF = 1024
CAP_FACTOR = 2
ROPE_THETA = 10000.0
ROPE_FREQS = HEAD_DIM // 4
EPS = 1e-6
IN_WIDTH = 2 * (QK_W + KV_W) + 2 * B_WIDTH + N_BRANCH * D_MODEL

T_CTX = BATCH * SEQ
T_LAT = DEC_BATCH * DEC_SEQ
T_ALL = T_CTX + T_LAT
N_REQ = 1 + DEC_BATCH
CAP_CTX = CAP_FACTOR * SEQ // N_EXPERTS
CAP_LAT = CAP_FACTOR * DEC_SEQ // N_EXPERTS
ROWS_PER_EXPERT = BATCH * CAP_CTX + DEC_BATCH * CAP_LAT

LANE = 128
ROW_TILE = 512
N_CTX_TILES = T_CTX // ROW_TILE
FFN_ROW_TILE = 512
VMEM_LIMIT = 56 * 1024 * 1024
NEG_BIG = -1e30
LOG2_E = 1.4426950408889634

OFF_A = 0
OFF_AV = OFF_A + QK_W
OFF_BU = OFF_AV + KV_W
OFF_BV = OFF_BU + B_WIDTH
OFF_C = OFF_BV + B_WIDTH
OFF_CV = OFF_C + QK_W
OFF_G = OFF_CV + KV_W


def _params(sem, vmem=VMEM_LIMIT):
    return pltpu.CompilerParams(dimension_semantics=sem, vmem_limit_bytes=vmem)


def _sigmoid(x):
    return 1.0 / (1.0 + jnp.exp(-x))


def _gelu_tanh(x):
    return 0.5 * x * (1.0 + jnp.tanh(0.7978845608028654 * (x + 0.044715 * (x * x * x))))


def _split_bf16(x):
    hi = x.astype(BF16)
    lo = (x - hi.astype(F32)).astype(BF16)
    return hi, lo


def _mod_kernel(c_ref, w_ref, b_ref, o_ref):
    c = c_ref[...]
    s_hi, s_lo = _split_bf16(c * _sigmoid(c))
    w_hi, w_lo = _split_bf16(w_ref[0])
    acc = jnp.dot(s_hi, w_hi, preferred_element_type=F32)
    acc += jnp.dot(s_lo, w_hi, preferred_element_type=F32)
    acc += jnp.dot(s_hi, w_lo, preferred_element_type=F32)
    o_ref[0] = acc + b_ref[0]


def _modulation(cond8, w_mod, b_mod):
    n_col = 6 * D_MODEL // D_MODEL
    return pl.pallas_call(
        _mod_kernel,
        grid=(DEPTH, n_col),
        in_specs=[
            pl.BlockSpec((8, D_MODEL), lambda l, j: (0, 0)),
            pl.BlockSpec((1, D_MODEL, D_MODEL), lambda l, j: (l, 0, j)),
            pl.BlockSpec((1, 1, D_MODEL), lambda l, j: (l, 0, j)),
        ],
        out_specs=pl.BlockSpec((1, 8, D_MODEL), lambda l, j: (l, 0, j)),
        out_shape=jax.ShapeDtypeStruct((DEPTH, 8, 6 * D_MODEL), F32),
        compiler_params=_params(("arbitrary", "arbitrary")),
        name="modulation",
    )(cond8, w_mod, b_mod.reshape(DEPTH, 1, 6 * D_MODEL))


def _group_sumsq(y, bd_ref):
    return jnp.dot((y * y).astype(BF16), bd_ref[...], preferred_element_type=F32)


def _pick_pass(i, ctx_ref, lat_ref):
    return jnp.where(i < N_CTX_TILES, ctx_ref[...], lat_ref[...])


def _in_kernel(xc_ref, xl_ref, sc_ref, sh_ref, n1_ref, w_ref, cs_ref, sn_ref, ga_ref, gc_ref, gbv_ref, bd_qk_ref,
               bd_b_ref, qa_ref, ka_ref, va_ref, nka_ref, nva_ref, bu_ref, bv_ref, qc_ref, kc_ref, vc_ref, nkc_ref,
               nvc_ref, gt_ref):
    x = _pick_pass(pl.program_id(0), xc_ref, xl_ref)
    ms = jnp.mean(x * x, axis=-1, keepdims=True)
    h = x * lax.rsqrt(ms + EPS) * n1_ref[...]
    h = h * (1.0 + sc_ref[...]) + sh_ref[...]
    hb = h.astype(BF16)
    tm = x.shape[0]

    def proj(c0, width):
        return jnp.dot(hb, w_ref[:, c0:c0 + width], preferred_element_type=F32)

    cs = jnp.concatenate([cs_ref[...]] * (QK_W // LANE), axis=1)
    sn = jnp.concatenate([sn_ref[...]] * (QK_W // LANE), axis=1)
    lane = lax.broadcasted_iota(I32, (tm, QK_W), 1)
    first_half = (lane & ROPE_FREQS) == 0

    def qk_post(y, gain_ref):
        yn = y * lax.rsqrt(_group_sumsq(y, bd_qk_ref) * (1.0 / HEAD_DIM) + EPS) * gain_ref[...]
        partner = jnp.where(first_half, pltpu.roll(yn, QK_W - ROPE_FREQS, 1), pltpu.roll(yn, ROPE_FREQS, 1))
        return yn * cs + partner * sn

    def mixer(off_qk, off_v, gain_ref, q_ref, k_ref, v_ref, nk_ref, nv_ref):
        y = qk_post(proj(off_qk, QK_W), gain_ref)
        v = proj(off_v, KV_W)
        q_ref[...] = y[:, :Q_W].astype(BF16)
        k_ref[...] = y[:, Q_W:].astype(BF16)
        v_ref[...] = v.astype(BF16)

        nk_ref[...] = y[:, Q_W:]
        nv_ref[...] = v

    mixer(OFF_A, OFF_AV, ga_ref, qa_ref, ka_ref, va_ref, nka_ref, nva_ref)

    bu_ref[...] = _gelu_tanh(proj(OFF_BU, B_WIDTH)).astype(BF16)
    gv = _gelu_tanh(proj(OFF_BV, B_WIDTH))
    gvn = gv * lax.rsqrt(_group_sumsq(gv, bd_b_ref) * (1.0 / B_GROUP_CH) + EPS) * gbv_ref[...]
    bv_ref[...] = gvn.astype(BF16)

    mixer(OFF_C, OFF_CV, gc_ref, qc_ref, kc_ref, vc_ref, nkc_ref, nvc_ref)

    gate_chunk = 512
    for j in range(N_BRANCH * D_MODEL // gate_chunk):
        g = proj(OFF_G + j * gate_chunk, gate_chunk)
        gt_ref[:, j * gate_chunk:(j + 1) * gate_chunk] = _sigmoid(g).astype(BF16)


def _req_of_tile(i):
    return i // N_CTX_TILES


def _ctx_rows(w):
    return pl.BlockSpec((ROW_TILE, w), lambda i: (jnp.minimum(i, N_CTX_TILES - 1), 0))


def _lat_rows(w):
    return pl.BlockSpec((ROW_TILE, w), lambda i: (jnp.maximum(i - N_CTX_TILES, 0), 0))


MOD_SH1, MOD_SC1, MOD_G1, MOD_SH2, MOD_SC2, MOD_G2 = range(6)


def _layer_spec(stacked, layer):
    rest = stacked.shape[1:]
    return pl.BlockSpec((None,) + rest, lambda *g: (layer,) + (0,) * len(rest))


def _mod_spec(layer, chunk, req):
    return pl.BlockSpec((None, None, 1, D_MODEL), lambda *g: (layer, req(*g), 0, chunk))


def _block_diag_ones(width, group):
    g = np.arange(width) // group
    return (g[:, None] == g[None, :]).astype(np.float32)


def _input_projection(x_ctx, x_lat, mods, layer, n1, w_in_b, cs, sn, gain_a, gain_c, gain_bv):
    bd_qk = jnp.asarray(_block_diag_ones(QK_W, HEAD_DIM), BF16)
    bd_b = jnp.asarray(_block_diag_ones(B_WIDTH, B_GROUP_CH), BF16)
    tm = ROW_TILE
    row = lambda w: pl.BlockSpec((tm, w), lambda i: (i, 0))
    full = lambda a: pl.BlockSpec(a.shape, lambda i: (0,) * a.ndim)
    rope = pl.BlockSpec((tm, LANE), lambda i: (_rope_tile(i), 0))
    cache_rows = T_CTX + tm
    spare = lambda w: pl.BlockSpec((tm, w), lambda i: (jnp.minimum(i, N_CTX_TILES), 0))
    mixer_outs = [(Q_W, BF16, T_ALL), (KV_W, BF16, T_ALL), (KV_W, BF16, T_ALL), (KV_W, F32, cache_rows),
                  (KV_W, F32, cache_rows)]
    outs = mixer_outs + [(B_WIDTH, BF16, T_ALL), (B_WIDTH, BF16, T_ALL)] + mixer_outs + [(N_BRANCH * D_MODEL, BF16, T_ALL)]
    return pl.pallas_call(
        _in_kernel,
        grid=(T_ALL // tm,),
        in_specs=[_ctx_rows(D_MODEL), _lat_rows(D_MODEL), _mod_spec(layer, MOD_SC1, _req_of_tile),
                  _mod_spec(layer, MOD_SH1, _req_of_tile), full(n1), _layer_spec(w_in_b, layer), rope, rope,
                  full(gain_a), full(gain_c), full(gain_bv), full(bd_qk), full(bd_b)],
        out_specs=[row(w) if rows == T_ALL else spare(w) for w, _, rows in outs],
        out_shape=[jax.ShapeDtypeStruct((rows, w), dt) for w, dt, rows in outs],
        compiler_params=_params(("arbitrary",)),
        name="input_projection",
    )(x_ctx, x_lat, mods, mods, n1, w_in_b, cs, sn, gain_a, gain_c, gain_bv, bd_qk, bd_b)


def _banded_start(carry, banded, qt_scr, heads, lo, hi, tq):
    k_refs, v_refs, biases = banded
    m0, acc0 = carry
    k_loc = jnp.concatenate([r[0, :, lo:hi] for r in k_refs], axis=0)
    vt = jnp.concatenate([r[0].astype(F32).T[lo:hi, :].astype(BF16) for r in v_refs], axis=1)
    vt = jnp.concatenate([vt, jnp.ones((DEN_ROWS, vt.shape[1]), BF16)], axis=0)
    pick = lambda x, j: jnp.concatenate([x[:, g * tq + j * BAND_Q:g * tq + (j + 1) * BAND_Q] for g in range(N_GRP)], axis=1)
    ms, accs = [], []
    for j in range(tq // BAND_Q):
        keys = slice(j * BAND_Q, (j + 1) * BAND_Q + 2 * BLOCK)
        qt_j = jnp.concatenate([qt_scr[h * HEAD_DIM:(h + 1) * HEAD_DIM, j * BAND_Q:(j + 1) * BAND_Q] for h in heads], axis=1)
        s = jnp.dot(k_loc[keys], qt_j, preferred_element_type=F32) + jnp.concatenate([biases[j]] * N_GRP, axis=1)
        m_old = pick(m0, j)
        m_new = jnp.maximum(m_old, jnp.max(s, axis=0, keepdims=True))
        p = jnp.exp2(s - m_new).astype(BF16)
        accs.append(pick(acc0, j) * jnp.exp2(m_old - m_new) + jnp.dot(vt[:, keys], p, preferred_element_type=F32))
        ms.append(m_new)
    gather = lambda parts: jnp.concatenate(
        [parts[j][:, g * BAND_Q:(g + 1) * BAND_Q] for g in range(N_GRP) for j in range(tq // BAND_Q)], axis=1)
    return gather(ms), gather(accs)


def _attention_tile(q_ref, sources, sink_ref, o_ref, qt_scr, ot_scr, *, tq, key_chunk, banded=None):
    width = N_GRP * tq
    for j in range(Q_W // LANE):
        qt_scr[j * LANE:(j + 1) * LANE, :] = q_ref[0, :, j * LANE:(j + 1) * LANE].astype(F32).T.astype(BF16)
    for kv in range(N_KV):
        lo, hi = kv * HEAD_DIM, (kv + 1) * HEAD_DIM
        heads = [kv * N_GRP + g for g in range(N_GRP)]
        qt = jnp.concatenate([qt_scr[h * HEAD_DIM:(h + 1) * HEAD_DIM, :] for h in heads], axis=1)

        def step(carry, kref, vref, c0, size, bias):
            m, acc = carry
            s = jnp.dot(kref[0, pl.ds(c0, size), lo:hi], qt, preferred_element_type=F32)
            if bias is not None:
                s = s + jnp.concatenate([bias] * N_GRP, axis=1)
            vt = vref[0, pl.ds(c0, size), :].astype(F32).T[lo:hi, :].astype(BF16)
            vt = jnp.concatenate([vt, jnp.ones((DEN_ROWS, size), BF16)], axis=0)
            m_new = jnp.maximum(m, jnp.max(s, axis=0, keepdims=True))
            p = jnp.exp2(s - m_new).astype(BF16)
            acc = acc * jnp.exp2(m - m_new) + jnp.dot(vt, p, preferred_element_type=F32)
            return m_new, acc

        if sink_ref is not None:
            m0 = jnp.concatenate([jnp.full((1, tq), sink_ref[h] * LOG2_E, F32) for h in heads], axis=1)
            den0 = jnp.ones((DEN_ROWS, width), F32)
        else:
            m0 = jnp.full((1, width), NEG_BIG, F32)
            den0 = jnp.zeros((DEN_ROWS, width), F32)
        carry = (m0, jnp.concatenate([jnp.zeros((HEAD_DIM, width), F32), den0], axis=0))
        if banded is not None:
            carry = _banded_start(carry, banded, qt_scr, heads, lo, hi, tq)
        for kref, vref, bias in sources:
            n_rows = kref.shape[1]
            n_full = n_rows // key_chunk
            if bias is not None:
                carry = step(carry, kref, vref, 0, n_rows, bias)
                continue
            if n_full > 1:
                carry = lax.fori_loop(
                    0, n_full,
                    lambda c, cr: step(cr, kref, vref, pl.multiple_of(c * key_chunk, key_chunk), key_chunk, None), carry)
            elif n_full == 1:
                carry = step(carry, kref, vref, 0, key_chunk, None)
            if n_rows - n_full * key_chunk:
                carry = step(carry, kref, vref, n_full * key_chunk, n_rows - n_full * key_chunk, None)
        _, acc = carry
        o = acc[:HEAD_DIM] / acc[HEAD_DIM:HEAD_DIM + 1]
        for g, h in enumerate(heads):
            ot_scr[h * HEAD_DIM:(h + 1) * HEAD_DIM, :] = o[:, g * tq:(g + 1) * tq]
    for j in range(Q_W // LANE):
        o_ref[0, :, j * LANE:(j + 1) * LANE] = ot_scr[j * LANE:(j + 1) * LANE, :].T.astype(o_ref.dtype)


def _dense_attn_kernel(*refs, tq, key_chunk, has_extra, has_sink):
    refs = list(refs)
    q_ref, k_ref, v_ref = refs[:3]
    del refs[:3]
    sources = [(k_ref, v_ref, None)]
    if has_extra:
        sources.append((refs.pop(0), refs.pop(0), None))
    sink_ref = refs.pop(0) if has_sink else None
    o_ref, qt_scr, ot_scr = refs
    _attention_tile(q_ref, sources, sink_ref, o_ref, qt_scr, ot_scr, tq=tq, key_chunk=key_chunk)


def _attention_scratch(tq):
    return [pltpu.VMEM((Q_W, tq), BF16), pltpu.VMEM((Q_W, tq), F32)]


def _dense_attention(q, k, v, extra, sink, *, n_req, off, tq, key_chunk):
    s = q.shape[1]
    kv_spec = pl.BlockSpec((1, s, KV_W), lambda i, j: (off + i, 0, 0))
    in_specs = [pl.BlockSpec((1, tq, Q_W), lambda i, j: (off + i, j, 0)), kv_spec, kv_spec]
    args = [q, k, v]
    if extra is not None:
        in_specs += [pl.BlockSpec((1, extra[0].shape[1], KV_W), lambda i, j: (i, 0, 0))] * 2
        args += list(extra)
    if sink is not None:
        in_specs.append(pl.BlockSpec(memory_space=pltpu.SMEM))
        args.append(sink)
    return pl.pallas_call(
        functools.partial(_dense_attn_kernel, tq=tq, key_chunk=key_chunk, has_extra=extra is not None,
                          has_sink=sink is not None),
        grid=(n_req, s // tq),
        in_specs=in_specs,
        out_specs=pl.BlockSpec((1, tq, Q_W), lambda i, j: (i, j, 0)),
        out_shape=jax.ShapeDtypeStruct((n_req, s, Q_W), BF16),
        scratch_shapes=_attention_scratch(tq),
        compiler_params=_params(("arbitrary", "arbitrary")),
        name="dense_attention",
    )(*args)


def _ctx_attn_kernel(qa_ref, ka_ref, va_ref, qc_ref, kc_ref, vc_ref, sink_ref, oa_ref, oc_ref, qta, ota, qtc, otc):
    _attention_tile(qa_ref, [(ka_ref, va_ref, None)], sink_ref, oa_ref, qta, ota, tq=SEQ, key_chunk=SEQ)
    _attention_tile(qc_ref, [(kc_ref, vc_ref, None)], None, oc_ref, qtc, otc, tq=SEQ, key_chunk=SEQ)


def _context_attention(qa, ka, va, qc, kc, vc, sink):
    q_spec = pl.BlockSpec((1, SEQ, Q_W), lambda i: (i, 0, 0))
    kv_spec = pl.BlockSpec((1, SEQ, KV_W), lambda i: (i, 0, 0))
    return pl.pallas_call(
        _ctx_attn_kernel,
        grid=(BATCH,),
        in_specs=[q_spec, kv_spec, kv_spec, q_spec, kv_spec, kv_spec, pl.BlockSpec(memory_space=pltpu.SMEM)],
        out_specs=[q_spec, q_spec],
        out_shape=[jax.ShapeDtypeStruct((BATCH, SEQ, Q_W), BF16)] * 2,
        scratch_shapes=_attention_scratch(SEQ) * 2,
        compiler_params=_params(("arbitrary",)),
        name="context_attention",
    )(qa, ka, va, qc, kc, vc, sink)


DEN_ROWS = 16
WINDOW_TQ = 512
BAND_Q = 2 * BLOCK


def _window_attn_kernel(q_ref, kp_ref, kc_ref, kn_ref, vp_ref, vc_ref, vn_ref, ck_ref, cv_ref, band_ref, sink_ref, o_ref,
                        qt_scr, ot_scr, *, seq):
    q_pos0 = pl.program_id(1) * WINDOW_TQ
    n_grp = WINDOW_TQ // BAND_Q
    band = band_ref[...]
    hide_prev = jnp.where(q_pos0 >= BLOCK, 0.0, NEG_BIG)
    hide_next = jnp.where(q_pos0 + WINDOW_TQ < seq, 0.0, NEG_BIG)
    first = jnp.concatenate([band[:BLOCK] + hide_prev, band[BLOCK:]], axis=0)
    last = jnp.concatenate([band[:BAND_Q + BLOCK], band[BAND_Q + BLOCK:] + hide_next], axis=0)
    biases = [first] + [band] * (n_grp - 2) + [last]
    banded = ((kp_ref, kc_ref, kn_ref), (vp_ref, vc_ref, vn_ref), biases)
    _attention_tile(q_ref, [(ck_ref, cv_ref, None)], sink_ref, o_ref, qt_scr, ot_scr, tq=WINDOW_TQ,
                    key_chunk=PAST_LEN, banded=banded)


def _band_bias():
    d = (np.arange(BAND_Q + 2 * BLOCK) - BLOCK)[:, None] - np.arange(BAND_Q)[None, :]
    return np.where(np.abs(d) <= WINDOW, 0.0, NEG_BIG).astype(np.float32)


def _window_attention(q, k, v, ck, cv, sink, *, n_req, off):
    b, s = n_req, q.shape[1]
    nb = s // BLOCK
    per_tile = WINDOW_TQ // BLOCK
    edge = lambda f: pl.BlockSpec((1, BLOCK, KV_W), lambda i, j: (off + i, f(j), 0))
    prev = lambda j: jnp.maximum(j * per_tile - 1, 0)
    nxt = lambda j: jnp.minimum((j + 1) * per_tile, nb - 1)
    cur = pl.BlockSpec((1, WINDOW_TQ, KV_W), lambda i, j: (off + i, j, 0))
    ctx = pl.BlockSpec((1, PAST_LEN, KV_W), lambda i, j: (i, 0, 0))
    assert WINDOW_TQ // BAND_Q >= 2
    band = _band_bias()
    return pl.pallas_call(
        functools.partial(_window_attn_kernel, seq=s),
        grid=(b, s // WINDOW_TQ),
        in_specs=[pl.BlockSpec((1, WINDOW_TQ, Q_W), lambda i, j: (off + i, j, 0)),
                  edge(prev), cur, edge(nxt), edge(prev), cur, edge(nxt), ctx, ctx,
                  pl.BlockSpec(band.shape, lambda i, j: (0, 0)),
                  pl.BlockSpec(memory_space=pltpu.SMEM)],
        out_specs=pl.BlockSpec((1, WINDOW_TQ, Q_W), lambda i, j: (i, j, 0)),
        out_shape=jax.ShapeDtypeStruct((b, s, Q_W), BF16),
        scratch_shapes=_attention_scratch(WINDOW_TQ),
        compiler_params=_params(("arbitrary", "arbitrary")),
        name="window_attention",
    )(q, k, k, k, v, v, v, ck, cv, band, sink)


def _pack_halves(x):
    half = x.shape[1] // 2
    return pltpu.pack_elementwise([x[:, :half], x[:, half:]], packed_dtype=BF16)


def _unpack_halves(words):
    return tuple(pltpu.unpack_elementwise(words, index=i, packed_dtype=BF16, unpacked_dtype=F32).astype(BF16)
                 for i in range(2))


def _merge_kernel(xc_ref, xl_ref, oac_ref, oal_ref, bu_ref, bv_ref, occ_ref, ocl_ref, gt_ref, wa_ref, wb_ref, wc_ref,
                  wo_ref, ws_ref, bs_ref, g1_ref, sc2_ref, sh2_ref, n2_ref, wr_ref, br_ref, x1_ref, h2p_ref, afft_ref):
    i = pl.program_id(0)
    tm = xc_ref.shape[0]
    group = lax.broadcasted_iota(I32, (CHUNK, B_WIDTH), 1) // B_GROUP_CH
    obs = []
    for c in range(tm // CHUNK):
        v = bv_ref[c * CHUNK:(c + 1) * CHUNK, :]
        sv = jnp.zeros((CHUNK, B_WIDTH), F32)
        for g in range(B_GROUPS):
            sv = jnp.where(group == g, jnp.dot(ws_ref[g], v, preferred_element_type=F32), sv)
        u = bu_ref[c * CHUNK:(c + 1) * CHUNK, :].astype(F32)
        obs.append((u * (sv + bs_ref[...])).astype(BF16))
    ob = jnp.concatenate(obs, axis=0)

    oa = _pick_pass(i, oac_ref, oal_ref)
    oc = _pick_pass(i, occ_ref, ocl_ref)
    merged = gt_ref[:, 0:D_MODEL].astype(F32) * jnp.dot(oa, wa_ref[...], preferred_element_type=F32)
    merged += gt_ref[:, D_MODEL:2 * D_MODEL].astype(F32) * jnp.dot(ob, wb_ref[...], preferred_element_type=F32)
    merged += gt_ref[:, 2 * D_MODEL:3 * D_MODEL].astype(F32) * jnp.dot(oc, wc_ref[...], preferred_element_type=F32)
    y = jnp.dot(merged.astype(BF16), wo_ref[...], preferred_element_type=F32)
    x1 = _pick_pass(i, xc_ref, xl_ref) + g1_ref[...] * y
    x1_ref[...] = x1

    ms = jnp.mean(x1 * x1, axis=-1, keepdims=True)
    h2 = x1 * lax.rsqrt(ms + EPS) * n2_ref[...]
    h2 = h2 * (1.0 + sc2_ref[...]) + sh2_ref[...]
    h2p_ref[...] = _pack_halves(h2)

    logits = jnp.dot(h2.astype(BF16), wr_ref[...], preferred_element_type=F32) + br_ref[...]
    e = jnp.exp(logits - jnp.max(logits, axis=-1, keepdims=True))
    aff = e / jnp.sum(e, axis=-1, keepdims=True)
    afft_ref[...] = aff.T[:N_EXPERTS, :]


def _merge(x_ctx, x_lat, oa_ctx, oa_lat, bu, bv, oc_ctx, oc_lat, gt, wa, wb, wc, wo, ws, bs, mods, layer, n2, wr, br):
    tm = ROW_TILE
    row = lambda w: pl.BlockSpec((tm, w), lambda i: (i, 0))
    full = lambda a: pl.BlockSpec(a.shape, lambda i: (0,) * a.ndim)
    mod = lambda chunk: _mod_spec(layer, chunk, _req_of_tile)
    stack = lambda a: _layer_spec(a, layer)
    return pl.pallas_call(
        _merge_kernel,
        grid=(T_ALL // tm,),
        in_specs=[_ctx_rows(D_MODEL), _lat_rows(D_MODEL), _ctx_rows(Q_W), _lat_rows(Q_W), row(B_WIDTH), row(B_WIDTH),
                  _ctx_rows(Q_W), _lat_rows(Q_W), row(N_BRANCH * D_MODEL),
                  stack(wa), stack(wb), stack(wc), stack(wo), stack(ws), full(bs),
                  mod(MOD_G1), mod(MOD_SC2), mod(MOD_SH2), full(n2), stack(wr), full(br)],
        out_specs=[row(D_MODEL), row(D_MODEL // 2), pl.BlockSpec((N_EXPERTS, tm), lambda i: (0, i))],
        out_shape=[jax.ShapeDtypeStruct((T_ALL, D_MODEL), F32), jax.ShapeDtypeStruct((T_ALL, D_MODEL // 2), jnp.uint32),
                   jax.ShapeDtypeStruct((N_EXPERTS, T_ALL), F32)],
        compiler_params=_params(("arbitrary",)),
        name="merge_router",
    )(x_ctx, x_lat, oa_ctx, oa_lat, bu, bv, oc_ctx, oc_lat, gt, wa, wb, wc, wo, ws, bs, mods, mods, mods, n2, wr, br)


def _select_kernel(aff_ref, idx_ref, val_ref, *rest, n, cap, row_chunk):
    idx_row_ref = rest[0] if len(rest) == 4 else None
    possel_ref, idx_scr, val_scr = rest[-3:]
    a = aff_ref[...]
    rows = a.shape[0]
    tok = lax.broadcasted_iota(I32, (rows, n), 1)

    def count(ones):
        return jnp.sum(ones, axis=1, keepdims=True)

    def at_least(word):
        return jnp.where(a >= pltpu.bitcast(word, F32), 1, 0)

    thr = jnp.zeros((rows, 1), I32)
    for bit in range(30, -1, -1):
        cand = thr | (1 << bit)
        thr = jnp.where(count(at_least(cand)) >= cap, cand, thr)
    above = at_least(thr + 1)
    tied = at_least(thr) - above
    need = cap - count(above)
    last = jnp.zeros((rows, 1), I32)
    for bit in range(n.bit_length() - 2, -1, -1):
        cand = last | (1 << bit)
        last = jnp.where(count(jnp.where(tok < cand, tied, 0)) < need, cand, last)
    sel = above + jnp.where(tok <= last, tied, 0)

    blk = min(n, 256)
    tri = jnp.where(lax.broadcasted_iota(I32, (blk, blk), 0) <= lax.broadcasted_iota(I32, (blk, blk), 1),
                    1.0, 0.0).astype(BF16)
    sel_f = sel.astype(F32)
    offset = jnp.zeros((rows, 1), F32)
    for j in range(n // blk):
        s_blk = sel_f[:, j * blk:(j + 1) * blk]
        incl = jnp.dot(s_blk.astype(BF16), tri, preferred_element_type=F32)
        pos = (incl - s_blk + offset).astype(I32)
        possel_ref[:, j * blk:(j + 1) * blk] = jnp.where(sel[:, j * blk:(j + 1) * blk] > 0, pos, -1)
        offset = offset + incl[:, blk - 1:blk]

    tb = min(n, TOKEN_BLOCK)
    n_blk = n // tb

    def fold_lanes(x):
        acc = x[:, :LANE]
        for k in range(1, tb // LANE):
            acc = acc + x[:, k * LANE:(k + 1) * LANE]
        return acc

    def match(e, slot, t0):
        hit = possel_ref[pl.ds(e, 1), pl.ds(t0, tb)] == slot
        tok = t0 + lax.broadcasted_iota(I32, (1, tb), 1)
        return (fold_lanes(jnp.where(hit, tok, 0)),
                fold_lanes(jnp.where(hit, aff_ref[pl.ds(e, 1), pl.ds(t0, tb)], 0.0)))

    def per_row(e, _):
        ends, run = [], 0
        for j in range(n_blk - 1):
            run = run + jnp.sum(jnp.where(possel_ref[pl.ds(e, 1), j * tb:(j + 1) * tb] >= 0, 1, 0))
            ends.append(run)

        def per_chunk(c, _):
            r0 = pl.multiple_of(c * row_chunk, row_chunk)
            slot = lax.broadcasted_iota(I32, (row_chunk, 1), 0) + r0
            if n_blk == 1:
                idx, val = match(e, slot, 0)
            else:
                first = sum(jnp.where(end <= r0, 1, 0) for end in ends)
                last = 1 + sum(jnp.where(end < r0 + row_chunk, 1, 0) for end in ends)

                def per_block(j, acc):
                    i, v = match(e, slot, pl.multiple_of(j * tb, tb))
                    return acc[0] + i, acc[1] + v

                idx, val = lax.fori_loop(first, last, per_block,
                                         (jnp.zeros((row_chunk, LANE), I32), jnp.zeros((row_chunk, LANE), F32)))
            idx_scr[pl.ds(r0, row_chunk), :] = idx
            val_scr[pl.ds(r0, row_chunk), :] = val
            return 0

        lax.fori_loop(0, cap // row_chunk, per_chunk, 0)
        idx = jnp.sum(idx_scr[...], axis=1, keepdims=True)
        idx_ref[e] = idx
        val_ref[e] = jnp.sum(val_scr[...], axis=1, keepdims=True)
        if idx_row_ref is not None:
            idx_row_ref[pl.ds(e, 1), :] = jnp.broadcast_to(idx.astype(F32), (cap, LANE)).T[0:1, :].astype(I32)
        return 0

    def per_small_row(e, _):
        idx, val = match(e, lax.broadcasted_iota(I32, (cap, 1), 0), 0)
        idx_ref[e] = jnp.sum(idx, axis=1, keepdims=True)
        val_ref[e] = jnp.sum(val, axis=1, keepdims=True)
        return 0

    if n_blk == 1 and cap == row_chunk:
        lax.fori_loop(0, rows, per_small_row, 0, unroll=4)
    else:
        lax.fori_loop(0, rows, per_row, 0)


def _select(aff_rows, rows_per_step, cap):
    r, n = aff_rows.shape
    row_chunk = min(cap, 64)
    out_specs = [pl.BlockSpec((rows_per_step, cap, 1), lambda s: (s, 0, 0))] * 2
    out_shape = [jax.ShapeDtypeStruct((r, cap, 1), I32), jax.ShapeDtypeStruct((r, cap, 1), F32)]
    if cap % LANE == 0:
        out_specs.append(pl.BlockSpec((rows_per_step, cap), lambda s: (s, 0)))
        out_shape.append(jax.ShapeDtypeStruct((r, cap), I32))
    return pl.pallas_call(
        functools.partial(_select_kernel, n=n, cap=cap, row_chunk=row_chunk),
        grid=(r // rows_per_step,),
        in_specs=[pl.BlockSpec((rows_per_step, n), lambda s: (s, 0))],
        out_specs=out_specs,
        out_shape=out_shape,
        scratch_shapes=[pltpu.VMEM((rows_per_step, n), I32), pltpu.VMEM((cap, LANE), I32), pltpu.VMEM((cap, LANE), F32)],
        compiler_params=_params(("arbitrary",)),
        name="expert_select",
    )(aff_rows)


CTX_SLOTS = N_EXPERTS * CAP_CTX
TOKEN_BLOCK = 512
SLOT_GROUP = 16


CTX_PER_STEP = 4


def _ctx_slot_onehot(idx, slots_on_rows):
    idx = idx.reshape(CTX_SLOTS, 1)
    if slots_on_rows:
        hit = idx == lax.broadcasted_iota(I32, (CTX_SLOTS, SEQ), 1)
    else:
        idx_lane = jnp.broadcast_to(idx.astype(F32), (CTX_SLOTS, LANE)).T[0:1, :]
        hit = idx_lane == lax.broadcasted_iota(I32, (SEQ, CTX_SLOTS), 0).astype(F32)
    return jnp.where(hit, 1.0, 0.0).astype(BF16)


def _gather_ctx_kernel(idx_ref, h_ref, out_ref):
    for r in range(CTX_PER_STEP):
        onehot = _ctx_slot_onehot(idx_ref[r * N_EXPERTS:(r + 1) * N_EXPERTS], True)
        lo, hi = _unpack_halves(h_ref[r * SEQ:(r + 1) * SEQ, :])
        g_lo = jnp.dot(onehot, lo, preferred_element_type=F32)
        g_hi = jnp.dot(onehot, hi, preferred_element_type=F32)
        packed = pltpu.pack_elementwise([g_lo, g_hi], packed_dtype=BF16)
        out_ref[:, r * CAP_CTX:(r + 1) * CAP_CTX, :] = packed.reshape(N_EXPERTS, CAP_CTX, D_MODEL // 2)


def _gather_ctx(idx_c, h2p):
    n = CTX_PER_STEP
    return pl.pallas_call(
        _gather_ctx_kernel,
        grid=(BATCH // n,),
        in_specs=[pl.BlockSpec((n * N_EXPERTS, CAP_CTX, 1), lambda b: (b, 0, 0)),
                  pl.BlockSpec((n * SEQ, D_MODEL // 2), lambda b: (b, 0))],
        out_specs=pl.BlockSpec((N_EXPERTS, n * CAP_CTX, D_MODEL // 2), lambda b: (0, b, 0)),
        out_shape=jax.ShapeDtypeStruct((N_EXPERTS, BATCH * CAP_CTX, D_MODEL // 2), jnp.uint32),
        compiler_params=_params(("arbitrary",)),
        name="gather_ctx",
    )(idx_c, h2p)


def _gather_lat_kernel(idx_ref, src_ref, out_ref):
    base = (pl.program_id(0) * N_EXPERTS + pl.program_id(1)) * CAP_LAT

    def body(it, _):
        r0 = pl.multiple_of(it * SLOT_GROUP, SLOT_GROUP)
        picked = [src_ref[0, pl.ds(idx_ref[base + r0 + k], 1), :] for k in range(SLOT_GROUP)]
        dst = out_ref.at[0, pl.ds(r0, SLOT_GROUP)]
        for k in range(SLOT_GROUP):
            dst[k:k + 1, :] = picked[k]
        return 0

    lax.fori_loop(0, CAP_LAT // SLOT_GROUP, body, 0)


def _gather_lat(idx_flat, h2p3, off):
    return pl.pallas_call(
        _gather_lat_kernel,
        grid_spec=pltpu.PrefetchScalarGridSpec(
            num_scalar_prefetch=1,
            grid=(DEC_BATCH, N_EXPERTS),
            in_specs=[pl.BlockSpec((1, DEC_SEQ, D_MODEL // 2), lambda b, e, idx: (off + b, 0, 0))],
            out_specs=pl.BlockSpec((1, CAP_LAT, D_MODEL // 2), lambda b, e, idx: (e, b, 0)),
        ),
        out_shape=jax.ShapeDtypeStruct((N_EXPERTS, DEC_BATCH * CAP_LAT, D_MODEL // 2), jnp.uint32),
        compiler_params=_params(("arbitrary", "arbitrary")),
        name="gather_lat",
    )(idx_flat, h2p3)


N_CTX_FFN_TILES = BATCH * CAP_CTX // FFN_ROW_TILE


def _ffn_kernel(xc_ref, xl_ref, vc_ref, vl_ref, g2_ref, wg_ref, wu_ref, wd_ref, o_ref, wg_b, wu_b, wd_b):
    j = pl.program_id(1)

    @pl.when(j == 0)
    def _():
        wg_b[...] = wg_ref[0].astype(BF16)
        wu_b[...] = wu_ref[0].astype(BF16)
        wd_b[...] = wd_ref[0].astype(BF16)

    is_ctx = j < N_CTX_FFN_TILES
    x = jnp.where(is_ctx, jnp.concatenate(_unpack_halves(xc_ref[0]), axis=1),
                  jnp.concatenate(_unpack_halves(xl_ref[0]), axis=1))
    g = jnp.dot(x, wg_b[...], preferred_element_type=F32)
    u = jnp.dot(x, wu_b[...], preferred_element_type=F32)
    hh = (g * _sigmoid(g)) * u
    y = jnp.dot(hh.astype(BF16), wd_b[...], preferred_element_type=F32)
    o_ref[0] = (y * jnp.where(is_ctx, vc_ref[...].reshape(FFN_ROW_TILE, 1), vl_ref[...])) * g2_ref[...]


def _expert_ffn(xg_ctx, xg_lat, val_ctx, val_lat, mods, w_gate, w_up, w_down, layer):
    tr = FFN_ROW_TILE
    assert tr == BATCH * CAP_CTX == CAP_LAT
    n_tiles = ROWS_PER_EXPERT // tr
    def wspec(k, n, tiles_held):
        ahead = lambda e, j: jnp.minimum(e + jnp.where(j >= tiles_held, 1, 0), N_EXPERTS - 1)
        return pl.BlockSpec((None, 1, k, n), lambda e, j: (layer, ahead(e, j), 0, 0))

    ctx_tile = lambda j: jnp.minimum(j, N_CTX_FFN_TILES - 1)
    lat_tile = lambda j: jnp.maximum(j - N_CTX_FFN_TILES, 0)
    return pl.pallas_call(
        _ffn_kernel,
        grid=(N_EXPERTS, n_tiles),
        in_specs=[pl.BlockSpec((1, tr, D_MODEL // 2), lambda e, j: (e, ctx_tile(j), 0)),
                  pl.BlockSpec((1, tr, D_MODEL // 2), lambda e, j: (e, lat_tile(j), 0)),
                  pl.BlockSpec((BATCH, None, CAP_CTX, 1), lambda e, j: (0, e, 0, 0)),
                  pl.BlockSpec((None, None, CAP_LAT, 1), lambda e, j: (lat_tile(j), e, 0, 0)),
                  _mod_spec(layer, MOD_G2, lambda e, j: j),
                  wspec(D_MODEL, EXPERT_FF, 1), wspec(D_MODEL, EXPERT_FF, n_tiles - 1),
                  wspec(EXPERT_FF, D_MODEL, n_tiles)],
        out_specs=pl.BlockSpec((1, tr, D_MODEL), lambda e, j: (e, j, 0)),
        out_shape=jax.ShapeDtypeStruct((N_EXPERTS, ROWS_PER_EXPERT, D_MODEL), F32),
        scratch_shapes=[pltpu.VMEM((D_MODEL, EXPERT_FF), BF16), pltpu.VMEM((D_MODEL, EXPERT_FF), BF16),
                        pltpu.VMEM((EXPERT_FF, D_MODEL), BF16)],
        compiler_params=_params(("arbitrary", "arbitrary")),
        name="expert_ffn",
    )(xg_ctx, xg_lat, val_ctx, val_lat, mods, w_gate, w_up, w_down)


def _scatter_ctx_kernel(idx_ref, y_ref, x1_ref, out_ref):
    for r in range(CTX_PER_STEP):
        onehot = _ctx_slot_onehot(idx_ref[r * N_EXPERTS:(r + 1) * N_EXPERTS], False)
        y_hi, y_lo = _split_bf16(y_ref[:, r * CAP_CTX:(r + 1) * CAP_CTX, :].reshape(CTX_SLOTS, D_MODEL))
        moe = jnp.dot(onehot, y_hi, preferred_element_type=F32) + jnp.dot(onehot, y_lo, preferred_element_type=F32)
        out_ref[r * SEQ:(r + 1) * SEQ, :] = x1_ref[r * SEQ:(r + 1) * SEQ, :] + moe


def _scatter_ctx(idx_c, yg, x1):
    n = CTX_PER_STEP
    return pl.pallas_call(
        _scatter_ctx_kernel,
        grid=(BATCH // n,),
        in_specs=[pl.BlockSpec((n * N_EXPERTS, CAP_CTX, 1), lambda b: (b, 0, 0)),
                  pl.BlockSpec((N_EXPERTS, n * CAP_CTX, D_MODEL), lambda b: (0, b, 0)),
                  pl.BlockSpec((n * SEQ, D_MODEL), lambda b: (b, 0))],
        out_specs=pl.BlockSpec((n * SEQ, D_MODEL), lambda b: (b, 0)),
        out_shape=jax.ShapeDtypeStruct((T_CTX, D_MODEL), F32),
        compiler_params=_params(("arbitrary",)),
        name="scatter_ctx",
    )(idx_c, yg, x1)


def _scatter_lat_kernel(idx_ref, y_ref, x1_hbm, out_ref, sem, *, off):
    b, e = pl.program_id(0), pl.program_id(2)

    @pl.when(e == 0)
    def _():
        load = pltpu.make_async_copy(x1_hbm.at[pl.ds(off + b, 1)], out_ref, sem)
        load.start()
        load.wait()

    base = (b * N_EXPERTS + e) * CAP_LAT

    def body(it, _):
        r0 = pl.multiple_of(it * SLOT_GROUP, SLOT_GROUP)
        rows = [idx_ref[base + r0 + k] for k in range(SLOT_GROUP)]
        old = [out_ref[0, pl.ds(rows[k], 1), :] for k in range(SLOT_GROUP)]
        y = y_ref[0, pl.ds(r0, SLOT_GROUP), :]
        for k in range(SLOT_GROUP):
            out_ref[0, pl.ds(rows[k], 1), :] = old[k] + y[k:k + 1, :]
        return 0

    lax.fori_loop(0, CAP_LAT // SLOT_GROUP, body, 0)


def _scatter_lat(idx_flat, yg, x1_3, off):
    blk0 = BATCH * CAP_CTX // CAP_LAT
    return pl.pallas_call(
        functools.partial(_scatter_lat_kernel, off=off),
        grid_spec=pltpu.PrefetchScalarGridSpec(
            num_scalar_prefetch=1,
            grid=(DEC_BATCH, 1, N_EXPERTS),
            in_specs=[pl.BlockSpec((1, CAP_LAT, D_MODEL), lambda b, h, e, idx: (e, blk0 + b, 0)),
                      pl.BlockSpec(memory_space=pl.ANY)],
            out_specs=pl.BlockSpec((1, DEC_SEQ, D_MODEL), lambda b, h, e, idx: (b, 0, 0)),
            scratch_shapes=[pltpu.SemaphoreType.DMA(())],
        ),
        out_shape=jax.ShapeDtypeStruct((DEC_BATCH, DEC_SEQ, D_MODEL), F32),
        compiler_params=_params(("arbitrary", "arbitrary", "arbitrary")),
        name="scatter_lat",
    )(idx_flat, yg, x1_3)


def _rope_tables():
    pos = np.arange(DEC_SEQ)
    freq = (np.float32(ROPE_THETA) ** (-np.arange(ROPE_FREQS, dtype=np.float32) / np.float32(ROPE_FREQS)))
    ang_r = (pos // GRID_W).astype(np.float32)[:, None] * freq.astype(np.float32)
    ang_c = (pos % GRID_W).astype(np.float32)[:, None] * freq.astype(np.float32)
    cos = np.concatenate([np.cos(ang_r)] * 2 + [np.cos(ang_c)] * 2, axis=-1)
    sin = np.concatenate([-np.sin(ang_r), np.sin(ang_r), -np.sin(ang_c), np.sin(ang_c)], axis=-1)
    reps = LANE // HEAD_DIM
    cs = np.concatenate([np.ones((ROW_TILE, LANE)), np.tile(cos, (1, reps))], axis=0).astype(np.float32)
    sn = np.concatenate([np.zeros((ROW_TILE, LANE)), np.tile(sin, (1, reps))], axis=0).astype(np.float32)
    return cs, sn


def _rope_tile(i):
    lat = jnp.maximum(i - N_CTX_TILES, 0) % (DEC_SEQ // ROW_TILE)
    return jnp.where(i < N_CTX_TILES, 0, 1 + lat)


def _qk_gain(q_norm, k_norm):
    q = jnp.tile(q_norm, N_HEADS) * (HEAD_DIM ** -0.5 * LOG2_E)
    return jnp.concatenate([q, jnp.tile(k_norm, N_KV)])[None, :]


def kernel(x_prompt, x_sample, cache_a_k, cache_a_v, cache_c_k, cache_c_v, c, c_ctx, norm1_g, w_mod, b_mod, w_in,
           a_q_norm, a_k_norm, a_sink, b_v_norm, b_ws, b_bs, c_q_norm, c_k_norm, w_a_o, w_b_o, w_c_o, w_out, norm2_g,
           w_router, b_router, w_gate, w_up, w_down):
    cond8 = jnp.concatenate([c_ctx[None, :], c, jnp.zeros((8 - N_REQ, D_MODEL), F32)], axis=0)
    mods = _modulation(cond8, w_mod, b_mod).reshape(DEPTH, 8, 1, 6 * D_MODEL)

    cs, sn = _rope_tables()
    w_in_b = w_in.astype(BF16)
    wa_b, wb_b, wc_b, wo_b = w_a_o.astype(BF16), w_b_o.astype(BF16), w_c_o.astype(BF16), w_out.astype(BF16)
    ws_b = b_ws.astype(BF16)
    wr_pad = jnp.pad(w_router, ((0, 0), (0, 0), (0, LANE - N_EXPERTS))).astype(BF16)
    br_pad = jnp.pad(b_router, ((0, 0), (0, LANE - N_EXPERTS)), constant_values=NEG_BIG)

    by_seq = lambda a: a.reshape(T_ALL // SEQ, SEQ, a.shape[-1])
    by_dec = lambda a: a.reshape(T_ALL // DEC_SEQ, DEC_SEQ, a.shape[-1])
    lat_off = T_CTX // DEC_SEQ

    caches = [a.reshape(DEC_BATCH, DEPTH, PAST_LEN, KV_W).astype(BF16)
              for a in (cache_a_k, cache_a_v, cache_c_k, cache_c_v)]

    x_ctx = x_prompt.reshape(T_CTX, D_MODEL)
    x_lat = x_sample.reshape(T_LAT, D_MODEL)
    new_kv = [[], [], [], []]
    for l in range(DEPTH):
        qa, ka_b, va_b, nka, nva, bu, bv, qc, kc_b, vc_b, nkc, nvc, gt = _input_projection(
            x_ctx, x_lat, mods, l, norm1_g[l][None, :], w_in_b, cs, sn,
            _qk_gain(a_q_norm[l], a_k_norm[l]), _qk_gain(c_q_norm[l], c_k_norm[l]), b_v_norm[l][None, :])
        for lst, arr in zip(new_kv, (nka, nva, nkc, nvc)):
            lst.append(arr[:T_CTX].reshape(BATCH, SEQ, N_KV, HEAD_DIM))

        sink = a_sink[l]
        oa_ctx, oc_ctx = _context_attention(by_seq(qa), by_seq(ka_b), by_seq(va_b),
                                            by_seq(qc), by_seq(kc_b), by_seq(vc_b), sink)
        cak, cav, cck, ccv = (a[:, l] for a in caches)
        oa_lat = _window_attention(by_dec(qa), by_dec(ka_b), by_dec(va_b), cak, cav, sink,
                                   n_req=DEC_BATCH, off=lat_off)
        oc_lat = _dense_attention(by_dec(qc), by_dec(kc_b), by_dec(vc_b), (cck, ccv), None,
                                  n_req=DEC_BATCH, off=lat_off, tq=1024, key_chunk=1024)

        bs_full = jnp.repeat(b_bs[l].T, B_GROUP_CH, axis=1)
        x1, h2p, afft = _merge(x_ctx, x_lat, oa_ctx.reshape(T_CTX, Q_W), oa_lat.reshape(T_LAT, Q_W), bu, bv,
                               oc_ctx.reshape(T_CTX, Q_W), oc_lat.reshape(T_LAT, Q_W), gt,
                               wa_b, wb_b, wc_b, wo_b, ws_b, bs_full,
                               mods, l, norm2_g[l][None, :], wr_pad, br_pad[l][None, :])

        aff_rows = lambda a, n_req, n: a.reshape(N_EXPERTS, n_req, n).transpose(1, 0, 2).reshape(n_req * N_EXPERTS, n)
        idx_c, val_c = _select(aff_rows(afft[:, :T_CTX], BATCH, SEQ), BATCH * N_EXPERTS, CAP_CTX)
        _, val_l, idx_l_rows = _select(aff_rows(afft[:, T_CTX:], DEC_BATCH, DEC_SEQ), N_EXPERTS, CAP_LAT)
        idx_l_flat = idx_l_rows.reshape(-1)
        xg_ctx = _gather_ctx(idx_c, h2p)
        xg_lat = _gather_lat(idx_l_flat, by_dec(h2p), lat_off)
        yg = _expert_ffn(xg_ctx, xg_lat, val_c.reshape(BATCH, N_EXPERTS, CAP_CTX, 1),
                         val_l.reshape(DEC_BATCH, N_EXPERTS, CAP_LAT, 1), mods, w_gate, w_up, w_down, l)

        x_ctx = _scatter_ctx(idx_c, yg, x1)
        x_lat = _scatter_lat(idx_l_flat, yg, by_dec(x1), lat_off).reshape(T_LAT, D_MODEL)

    y_prompt = x_ctx.reshape(BATCH, SEQ, D_MODEL)
    y_sample = x_lat.reshape(DEC_BATCH, DEC_SEQ, D_MODEL)
    return (y_prompt, y_sample) + tuple(jnp.stack(lst, axis=1) for lst in new_kv)
```

```python
import functools

import jax
import numpy as np
import jax.numpy as jnp
from jax import lax
from jax.experimental import pallas as pl
from jax.experimental.pallas import tpu as pltpu

F32 = jnp.float32
BF16 = jnp.bfloat16
I32 = jnp.int32

D_MODEL = 1024
BATCH = 16
SEQ = 256
DEPTH = 2
DEC_BATCH = 2
DEC_SEQ = 4096
PAST_LEN = 256
GRID_W = 64
HEAD_DIM = 64
N_HEADS = 6
N_KV = 2
N_GRP = N_HEADS // N_KV
B_GROUPS = 4
B_GROUP_CH = 64
B_WIDTH = B_GROUPS * B_GROUP_CH
Q_W = N_HEADS * HEAD_DIM
KV_W = N_KV * HEAD_DIM
QK_W = Q_W + KV_W
N_BRANCH = 3
WINDOW = 128
BLOCK = 128
CHUNK = 128
N_EXPERTS = 16
EXPERT_FF = 1024
CAP_FACTOR = 2
ROPE_THETA = 10000.0
ROPE_FREQS = HEAD_DIM // 4
EPS = 1e-6
IN_WIDTH = 2 * (QK_W + KV_W) + 2 * B_WIDTH + N_BRANCH * D_MODEL

T_CTX = BATCH * SEQ
T_LAT = DEC_BATCH * DEC_SEQ
T_ALL = T_CTX + T_LAT
N_REQ = 1 + DEC_BATCH
CAP_CTX = CAP_FACTOR * SEQ // N_EXPERTS
CAP_LAT = CAP_FACTOR * DEC_SEQ // N_EXPERTS
ROWS_PER_EXPERT = BATCH * CAP_CTX + DEC_BATCH * CAP_LAT

LANE = 128
ROW_TILE = 512
N_CTX_TILES = T_CTX // ROW_TILE
FFN_ROW_TILE = 512
VMEM_LIMIT = 56 * 1024 * 1024
NEG_BIG = -1e30
LOG2_E = 1.4426950408889634

OFF_A = 0
OFF_AV = OFF_A + QK_W
OFF_BU = OFF_AV + KV_W
OFF_BV = OFF_BU + B_WIDTH
OFF_C = OFF_BV + B_WIDTH
OFF_CV = OFF_C + QK_W
OFF_G = OFF_CV + KV_W


def _params(sem, vmem=VMEM_LIMIT):
    return pltpu.CompilerParams(dimension_semantics=sem, vmem_limit_bytes=vmem)


def _sigmoid(x):
    return 1.0 / (1.0 + jnp.exp(-x))


def _gelu_tanh(x):
    return 0.5 * x * (1.0 + jnp.tanh(0.7978845608028654 * (x + 0.044715 * (x * x * x))))


def _split_bf16(x):
    hi = x.astype(BF16)
    lo = (x - hi.astype(F32)).astype(BF16)
    return hi, lo


def _mod_kernel(c_ref, w_ref, b_ref, o_ref):
    c = c_ref[...]
    s_hi, s_lo = _split_bf16(c * _sigmoid(c))
    w_hi, w_lo = _split_bf16(w_ref[0])
    acc = jnp.dot(s_hi, w_hi, preferred_element_type=F32)
    acc += jnp.dot(s_lo, w_hi, preferred_element_type=F32)
    acc += jnp.dot(s_hi, w_lo, preferred_element_type=F32)
    o_ref[0] = acc + b_ref[0]


def _modulation(cond8, w_mod, b_mod):
    n_col = 6 * D_MODEL // D_MODEL
    return pl.pallas_call(
        _mod_kernel,
        grid=(DEPTH, n_col),
        in_specs=[
            pl.BlockSpec((8, D_MODEL), lambda l, j: (0, 0)),
            pl.BlockSpec((1, D_MODEL, D_MODEL), lambda l, j: (l, 0, j)),
            pl.BlockSpec((1, 1, D_MODEL), lambda l, j: (l, 0, j)),
        ],
        out_specs=pl.BlockSpec((1, 8, D_MODEL), lambda l, j: (l, 0, j)),
        out_shape=jax.ShapeDtypeStruct((DEPTH, 8, 6 * D_MODEL), F32),
        compiler_params=_params(("arbitrary", "arbitrary")),
        name="modulation",
    )(cond8, w_mod, b_mod.reshape(DEPTH, 1, 6 * D_MODEL))


def _group_sumsq(y, bd_ref):
    return jnp.dot((y * y).astype(BF16), bd_ref[...], preferred_element_type=F32)


def _pick_pass(i, ctx_ref, lat_ref):
    return jnp.where(i < N_CTX_TILES, ctx_ref[...], lat_ref[...])


def _in_kernel(xc_ref, xl_ref, sc_ref, sh_ref, n1_ref, w_ref, cs_ref, sn_ref, ga_ref, gc_ref, gbv_ref, bd_qk_ref,
               bd_b_ref, qa_ref, ka_ref, va_ref, nka_ref, nva_ref, bu_ref, bv_ref, qc_ref, kc_ref, vc_ref, nkc_ref,
               nvc_ref, gt_ref):
    x = _pick_pass(pl.program_id(0), xc_ref, xl_ref)
    ms = jnp.mean(x * x, axis=-1, keepdims=True)
    h = x * lax.rsqrt(ms + EPS) * n1_ref[...]
    h = h * (1.0 + sc_ref[...]) + sh_ref[...]
    hb = h.astype(BF16)
    tm = x.shape[0]

    def proj(c0, width):
        return jnp.dot(hb, w_ref[:, c0:c0 + width], preferred_element_type=F32)

    cs = jnp.concatenate([cs_ref[...]] * (QK_W // LANE), axis=1)
    sn = jnp.concatenate([sn_ref[...]] * (QK_W // LANE), axis=1)
    lane = lax.broadcasted_iota(I32, (tm, QK_W), 1)
    first_half = (lane & ROPE_FREQS) == 0

    def qk_post(y, gain_ref):
        yn = y * lax.rsqrt(_group_sumsq(y, bd_qk_ref) * (1.0 / HEAD_DIM) + EPS) * gain_ref[...]
        partner = jnp.where(first_half, pltpu.roll(yn, QK_W - ROPE_FREQS, 1), pltpu.roll(yn, ROPE_FREQS, 1))
        return yn * cs + partner * sn

    def mixer(off_qk, off_v, gain_ref, q_ref, k_ref, v_ref, nk_ref, nv_ref):
        y = qk_post(proj(off_qk, QK_W), gain_ref)
        v = proj(off_v, KV_W)
        q_ref[...] = y[:, :Q_W].astype(BF16)
        k_ref[...] = y[:, Q_W:].astype(BF16)
        v_ref[...] = v.astype(BF16)

        nk_ref[...] = y[:, Q_W:]
        nv_ref[...] = v

    mixer(OFF_A, OFF_AV, ga_ref, qa_ref, ka_ref, va_ref, nka_ref, nva_ref)

    bu_ref[...] = _gelu_tanh(proj(OFF_BU, B_WIDTH)).astype(BF16)
    gv = _gelu_tanh(proj(OFF_BV, B_WIDTH))
    gvn = gv * lax.rsqrt(_group_sumsq(gv, bd_b_ref) * (1.0 / B_GROUP_CH) + EPS) * gbv_ref[...]
    bv_ref[...] = gvn.astype(BF16)

    mixer(OFF_C, OFF_CV, gc_ref, qc_ref, kc_ref, vc_ref, nkc_ref, nvc_ref)

    gate_chunk = 512
    for j in range(N_BRANCH * D_MODEL // gate_chunk):
        g = proj(OFF_G + j * gate_chunk, gate_chunk)
        gt_ref[:, j * gate_chunk:(j + 1) * gate_chunk] = _sigmoid(g).astype(BF16)


def _req_of_tile(i):
    return i // N_CTX_TILES


def _ctx_rows(w):
    return pl.BlockSpec((ROW_TILE, w), lambda i: (jnp.minimum(i, N_CTX_TILES - 1), 0))


def _lat_rows(w):
    return pl.BlockSpec((ROW_TILE, w), lambda i: (jnp.maximum(i - N_CTX_TILES, 0), 0))


MOD_SH1, MOD_SC1, MOD_G1, MOD_SH2, MOD_SC2, MOD_G2 = range(6)


def _layer_spec(stacked, layer):
    rest = stacked.shape[1:]
    return pl.BlockSpec((None,) + rest, lambda *g: (layer,) + (0,) * len(rest))


def _mod_spec(layer, chunk, req):
    return pl.BlockSpec((None, None, 1, D_MODEL), lambda *g: (layer, req(*g), 0, chunk))


def _block_diag_ones(width, group):
    g = np.arange(width) // group
    return (g[:, None] == g[None, :]).astype(np.float32)


def _input_projection(x_ctx, x_lat, mods, layer, n1, w_in_b, cs, sn, gain_a, gain_c, gain_bv):
    bd_qk = jnp.asarray(_block_diag_ones(QK_W, HEAD_DIM), BF16)
    bd_b = jnp.asarray(_block_diag_ones(B_WIDTH, B_GROUP_CH), BF16)
    tm = ROW_TILE
    row = lambda w: pl.BlockSpec((tm, w), lambda i: (i, 0))
    full = lambda a: pl.BlockSpec(a.shape, lambda i: (0,) * a.ndim)
    rope = pl.BlockSpec((tm, LANE), lambda i: (_rope_tile(i), 0))
    cache_rows = T_CTX + tm
    spare = lambda w: pl.BlockSpec((tm, w), lambda i: (jnp.minimum(i, N_CTX_TILES), 0))
    mixer_outs = [(Q_W, BF16, T_ALL), (KV_W, BF16, T_ALL), (KV_W, BF16, T_ALL), (KV_W, F32, cache_rows),
                  (KV_W, F32, cache_rows)]
    outs = mixer_outs + [(B_WIDTH, BF16, T_ALL), (B_WIDTH, BF16, T_ALL)] + mixer_outs + [(N_BRANCH * D_MODEL, BF16, T_ALL)]
    return pl.pallas_call(
        _in_kernel,
        grid=(T_ALL // tm,),
        in_specs=[_ctx_rows(D_MODEL), _lat_rows(D_MODEL), _mod_spec(layer, MOD_SC1, _req_of_tile),
                  _mod_spec(layer, MOD_SH1, _req_of_tile), full(n1), _layer_spec(w_in_b, layer), rope, rope,
                  full(gain_a), full(gain_c), full(gain_bv), full(bd_qk), full(bd_b)],
        out_specs=[row(w) if rows == T_ALL else spare(w) for w, _, rows in outs],
        out_shape=[jax.ShapeDtypeStruct((rows, w), dt) for w, dt, rows in outs],
        compiler_params=_params(("arbitrary",)),
        name="input_projection",
    )(x_ctx, x_lat, mods, mods, n1, w_in_b, cs, sn, gain_a, gain_c, gain_bv, bd_qk, bd_b)


def _banded_start(carry, banded, qt_scr, heads, lo, hi, tq):
    k_refs, v_refs, biases = banded
    m0, acc0 = carry
    k_loc = jnp.concatenate([r[0, :, lo:hi] for r in k_refs], axis=0)
    vt = jnp.concatenate([r[0].astype(F32).T[lo:hi, :].astype(BF16) for r in v_refs], axis=1)
    vt = jnp.concatenate([vt, jnp.ones((DEN_ROWS, vt.shape[1]), BF16)], axis=0)
    pick = lambda x, j: jnp.concatenate([x[:, g * tq + j * BAND_Q:g * tq + (j + 1) * BAND_Q] for g in range(N_GRP)], axis=1)
    ms, accs = [], []
    for j in range(tq // BAND_Q):
        keys = slice(j * BAND_Q, (j + 1) * BAND_Q + 2 * BLOCK)
        qt_j = jnp.concatenate([qt_scr[h * HEAD_DIM:(h + 1) * HEAD_DIM, j * BAND_Q:(j + 1) * BAND_Q] for h in heads], axis=1)
        s = jnp.dot(k_loc[keys], qt_j, preferred_element_type=F32) + jnp.concatenate([biases[j]] * N_GRP, axis=1)
        m_old = pick(m0, j)
        m_new = jnp.maximum(m_old, jnp.max(s, axis=0, keepdims=True))
        p = jnp.exp2(s - m_new).astype(BF16)
        accs.append(pick(acc0, j) * jnp.exp2(m_old - m_new) + jnp.dot(vt[:, keys], p, preferred_element_type=F32))
        ms.append(m_new)
    gather = lambda parts: jnp.concatenate(
        [parts[j][:, g * BAND_Q:(g + 1) * BAND_Q] for g in range(N_GRP) for j in range(tq // BAND_Q)], axis=1)
    return gather(ms), gather(accs)


def _attention_tile(q_ref, sources, sink_ref, o_ref, qt_scr, ot_scr, *, tq, key_chunk, banded=None):
    width = N_GRP * tq
    for j in range(Q_W // LANE):
        qt_scr[j * LANE:(j + 1) * LANE, :] = q_ref[0, :, j * LANE:(j + 1) * LANE].astype(F32).T.astype(BF16)
    for kv in range(N_KV):
        lo, hi = kv * HEAD_DIM, (kv + 1) * HEAD_DIM
        heads = [kv * N_GRP + g for g in range(N_GRP)]
        qt = jnp.concatenate([qt_scr[h * HEAD_DIM:(h + 1) * HEAD_DIM, :] for h in heads], axis=1)

        def step(carry, kref, vref, c0, size):
            m, acc = carry
            s = jnp.dot(kref[0, pl.ds(c0, size), lo:hi], qt, preferred_element_type=F32)
            vt =vref[0, pl.ds(c0, size), :].astype(F32).T[lo:hi, :].astype(BF16)
            vt = jnp.concatenate([vt, jnp.ones((DEN_ROWS, size), BF16)], axis=0)
            m_new = jnp.maximum(m, jnp.max(s, axis=0, keepdims=True))
            p = jnp.exp2(s - m_new).astype(BF16)
            acc = acc * jnp.exp2(m - m_new) + jnp.dot(vt, p, preferred_element_type=F32)
            return m_new, acc

        if sink_ref is not None:
            m0 = jnp.concatenate([jnp.full((1, tq), sink_ref[h] * LOG2_E, F32) for h in heads], axis=1)
            den0 = jnp.ones((DEN_ROWS, width), F32)
        else:
            m0 = jnp.full((1, width), NEG_BIG, F32)
            den0 = jnp.zeros((DEN_ROWS, width), F32)
        carry = (m0, jnp.concatenate([jnp.zeros((HEAD_DIM, width), F32), den0], axis=0))
        if banded is not None:
            carry = _banded_start(carry, banded, qt_scr, heads, lo, hi, tq)
        for kref, vref in sources:
            n_rows = kref.shape[1]
            n_full = n_rows // key_chunk
            if n_full > 1:
                carry = lax.fori_loop(
                    0, n_full,
                    lambda c, cr: step(cr, kref, vref, pl.multiple_of(c * key_chunk, key_chunk), key_chunk), carry)
            elif n_full == 1:
                carry = step(carry, kref, vref, 0, key_chunk)
            if n_rows - n_full * key_chunk:
                carry = step(carry, kref, vref, n_full * key_chunk, n_rows - n_full * key_chunk)
        _, acc = carry
        o = acc[:HEAD_DIM] / acc[HEAD_DIM:HEAD_DIM + 1]
        for g, h in enumerate(heads):
            ot_scr[h * HEAD_DIM:(h + 1) * HEAD_DIM, :] = o[:, g * tq:(g + 1) * tq]
    for j in range(Q_W // LANE):
        o_ref[0, :, j * LANE:(j + 1) * LANE] = ot_scr[j * LANE:(j + 1) * LANE, :].T.astype(o_ref.dtype)


def _dense_attn_kernel(*refs, tq, key_chunk, has_extra, has_sink):
    refs = list(refs)
    q_ref, k_ref, v_ref = refs[:3]
    del refs[:3]
    sources = [(k_ref, v_ref)]
    if has_extra:
        sources.append((refs.pop(0), refs.pop(0)))
    sink_ref = refs.pop(0) if has_sink else None
    o_ref, qt_scr, ot_scr = refs
    _attention_tile(q_ref, sources, sink_ref, o_ref, qt_scr, ot_scr, tq=tq, key_chunk=key_chunk)


def _attention_scratch(tq):
    return [pltpu.VMEM((Q_W, tq), BF16), pltpu.VMEM((Q_W, tq), F32)]


def _dense_attention(q, k, v, extra, sink, *, n_req, off, tq, key_chunk):
    s = q.shape[1]
    kv_spec = pl.BlockSpec((1, s, KV_W), lambda i, j: (off + i, 0, 0))
    in_specs = [pl.BlockSpec((1, tq, Q_W), lambda i, j: (off + i, j, 0)), kv_spec, kv_spec]
    args = [q, k, v]
    if extra is not None:
        in_specs += [pl.BlockSpec((1, extra[0].shape[1], KV_W), lambda i, j: (i, 0, 0))] * 2
        args += list(extra)
    if sink is not None:
        in_specs.append(pl.BlockSpec(memory_space=pltpu.SMEM))
        args.append(sink)
    return pl.pallas_call(
        functools.partial(_dense_attn_kernel, tq=tq, key_chunk=key_chunk, has_extra=extra is not None,
                          has_sink=sink is not None),
        grid=(n_req, s // tq),
        in_specs=in_specs,
        out_specs=pl.BlockSpec((1, tq, Q_W), lambda i, j: (i, j, 0)),
        out_shape=jax.ShapeDtypeStruct((n_req, s, Q_W), BF16),
        scratch_shapes=_attention_scratch(tq),
        compiler_params=_params(("arbitrary", "arbitrary")),
        name="dense_attention",
    )(*args)


CTX_ATTN_PER_STEP = 2


def _ctx_attn_kernel(qa_ref, ka_ref, va_ref, qc_ref, kc_ref, vc_ref, sink_ref, oa_ref, oc_ref, *scratch):
    for r in range(CTX_ATTN_PER_STEP):
        one = lambda ref: ref.at[pl.ds(r, 1)]
        qta, ota, qtc, otc = scratch[4 * r:4 * r + 4]
        _attention_tile(one(qa_ref), [(one(ka_ref), one(va_ref))], sink_ref, one(oa_ref), qta, ota,
                        tq=SEQ, key_chunk=SEQ)
        _attention_tile(one(qc_ref), [(one(kc_ref), one(vc_ref))], None, one(oc_ref), qtc, otc,
                        tq=SEQ, key_chunk=SEQ)


def _context_attention(qa, ka, va, qc, kc, vc, sink):
    n = CTX_ATTN_PER_STEP
    q_spec = pl.BlockSpec((n, SEQ, Q_W), lambda i: (i, 0, 0))
    kv_spec = pl.BlockSpec((n, SEQ, KV_W), lambda i: (i, 0, 0))
    return pl.pallas_call(
        _ctx_attn_kernel,
        grid=(BATCH // n,),
        in_specs=[q_spec, kv_spec, kv_spec, q_spec, kv_spec, kv_spec, pl.BlockSpec(memory_space=pltpu.SMEM)],
        out_specs=[q_spec, q_spec],
        out_shape=[jax.ShapeDtypeStruct((BATCH, SEQ, Q_W), BF16)] * 2,
        scratch_shapes=_attention_scratch(SEQ) * (2 * n),
        compiler_params=_params(("arbitrary",)),
        name="context_attention",
    )(qa, ka, va, qc, kc, vc, sink)


DEN_ROWS = 16
WINDOW_TQ = 512
BAND_Q = 2 * BLOCK


def _window_attn_kernel(q_ref, kp_ref, kc_ref, kn_ref, vp_ref, vc_ref, vn_ref, ck_ref, cv_ref, band_ref, sink_ref, o_ref,
                        qt_scr, ot_scr, *, seq):
    q_pos0 = pl.program_id(1) * WINDOW_TQ
    n_grp = WINDOW_TQ // BAND_Q
    band = band_ref[...]
    hide_prev = jnp.where(q_pos0 >= BLOCK, 0.0, NEG_BIG)
    hide_next = jnp.where(q_pos0 + WINDOW_TQ < seq, 0.0, NEG_BIG)
    first = jnp.concatenate([band[:BLOCK] + hide_prev, band[BLOCK:]], axis=0)
    last = jnp.concatenate([band[:BAND_Q + BLOCK], band[BAND_Q + BLOCK:] + hide_next], axis=0)
    biases = [first] + [band] * (n_grp - 2) + [last]
    banded = ((kp_ref, kc_ref, kn_ref), (vp_ref, vc_ref, vn_ref), biases)
    _attention_tile(q_ref, [(ck_ref, cv_ref)], sink_ref, o_ref, qt_scr, ot_scr, tq=WINDOW_TQ,
                    key_chunk=PAST_LEN, banded=banded)


def _band_bias():
    d = (np.arange(BAND_Q + 2 * BLOCK) - BLOCK)[:, None] - np.arange(BAND_Q)[None, :]
    return np.where(np.abs(d) <= WINDOW, 0.0, NEG_BIG).astype(np.float32)


def _window_attention(q, k, v, ck, cv, sink, *, n_req, off):
    b, s = n_req, q.shape[1]
    nb = s // BLOCK
    per_tile = WINDOW_TQ // BLOCK
    edge = lambda f: pl.BlockSpec((1, BLOCK, KV_W), lambda i, j: (off + i, f(j), 0))
    prev = lambda j: jnp.maximum(j * per_tile - 1, 0)
    nxt = lambda j: jnp.minimum((j + 1) * per_tile, nb - 1)
    cur = pl.BlockSpec((1, WINDOW_TQ, KV_W), lambda i, j: (off + i, j, 0))
    ctx = pl.BlockSpec((1, PAST_LEN, KV_W), lambda i, j: (i, 0, 0))
    assert WINDOW_TQ // BAND_Q >= 2
    band = _band_bias()
    return pl.pallas_call(
        functools.partial(_window_attn_kernel, seq=s),
        grid=(b, s // WINDOW_TQ),
        in_specs=[pl.BlockSpec((1, WINDOW_TQ, Q_W), lambda i, j: (off + i, j, 0)),
                  edge(prev), cur, edge(nxt), edge(prev), cur, edge(nxt), ctx, ctx,
                  pl.BlockSpec(band.shape, lambda i, j: (0, 0)),
                  pl.BlockSpec(memory_space=pltpu.SMEM)],
        out_specs=pl.BlockSpec((1, WINDOW_TQ, Q_W), lambda i, j: (i, j, 0)),
        out_shape=jax.ShapeDtypeStruct((b, s, Q_W), BF16),
        scratch_shapes=_attention_scratch(WINDOW_TQ),
        compiler_params=_params(("arbitrary", "arbitrary")),
        name="window_attention",
    )(q, k, k, k, v, v, v, ck, cv, band, sink)


def _pack_halves(x):
    half = x.shape[1] // 2
    return pltpu.pack_elementwise([x[:, :half], x[:, half:]], packed_dtype=BF16)


def _unpack_halves(words):
    return tuple(pltpu.unpack_elementwise(words, index=i, packed_dtype=BF16, unpacked_dtype=F32).astype(BF16)
                 for i in range(2))


def _merge_kernel(xc_ref, xl_ref, oac_ref, oal_ref, bu_ref, bv_ref, occ_ref, ocl_ref, gt_ref, wa_ref, wb_ref, wc_ref,
                  wo_ref, ws_ref, bs_ref, g1_ref, sc2_ref, sh2_ref, n2_ref, wr_ref, br_ref, x1_ref, h2p_ref, afft_ref):
    i = pl.program_id(0)
    tm = xc_ref.shape[0]
    group = lax.broadcasted_iota(I32, (CHUNK, B_WIDTH), 1) // B_GROUP_CH
    obs = []
    for c in range(tm // CHUNK):
        v = bv_ref[c * CHUNK:(c + 1) * CHUNK, :]
        sv = jnp.zeros((CHUNK, B_WIDTH), F32)
        for g in range(B_GROUPS):
            sv = jnp.where(group == g, jnp.dot(ws_ref[g], v, preferred_element_type=F32), sv)
        u = bu_ref[c * CHUNK:(c + 1) * CHUNK, :].astype(F32)
        obs.append((u * (sv + bs_ref[...])).astype(BF16))
    ob = jnp.concatenate(obs, axis=0)

    oa = _pick_pass(i, oac_ref, oal_ref)
    oc = _pick_pass(i, occ_ref, ocl_ref)
    merged = gt_ref[:, 0:D_MODEL].astype(F32) * jnp.dot(oa, wa_ref[...], preferred_element_type=F32)
    merged += gt_ref[:, D_MODEL:2 * D_MODEL].astype(F32) * jnp.dot(ob, wb_ref[...], preferred_element_type=F32)
    merged += gt_ref[:, 2 * D_MODEL:3 * D_MODEL].astype(F32) * jnp.dot(oc, wc_ref[...], preferred_element_type=F32)
    y = jnp.dot(merged.astype(BF16), wo_ref[...], preferred_element_type=F32)
    x1 = _pick_pass(i, xc_ref, xl_ref) + g1_ref[...] * y
    x1_ref[...] = x1

    ms = jnp.mean(x1 * x1, axis=-1, keepdims=True)
    h2 = x1 * lax.rsqrt(ms + EPS) * n2_ref[...]
    h2 = h2 * (1.0 + sc2_ref[...]) + sh2_ref[...]
    h2p_ref[...] = _pack_halves(h2)

    logits = jnp.dot(h2.astype(BF16), wr_ref[...], preferred_element_type=F32) + br_ref[...]
    e = jnp.exp(logits - jnp.max(logits, axis=-1, keepdims=True))
    aff = e / jnp.sum(e, axis=-1, keepdims=True)
    afft_ref[...] = aff.T[:N_EXPERTS, :]


def _merge(x_ctx, x_lat, oa_ctx, oa_lat, bu, bv, oc_ctx, oc_lat, gt, wa, wb, wc, wo, ws, bs, mods, layer, n2, wr, br):
    tm = ROW_TILE
    row = lambda w: pl.BlockSpec((tm, w), lambda i: (i, 0))
    full = lambda a: pl.BlockSpec(a.shape, lambda i: (0,) * a.ndim)
    mod = lambda chunk: _mod_spec(layer, chunk, _req_of_tile)
    stack = lambda a: _layer_spec(a, layer)
    return pl.pallas_call(
        _merge_kernel,
        grid=(T_ALL // tm,),
        in_specs=[_ctx_rows(D_MODEL), _lat_rows(D_MODEL), _ctx_rows(Q_W), _lat_rows(Q_W), row(B_WIDTH), row(B_WIDTH),
                  _ctx_rows(Q_W), _lat_rows(Q_W), row(N_BRANCH * D_MODEL),
                  stack(wa), stack(wb), stack(wc), stack(wo), stack(ws), full(bs),
                  mod(MOD_G1), mod(MOD_SC2), mod(MOD_SH2), full(n2), stack(wr), full(br)],
        out_specs=[row(D_MODEL), row(D_MODEL // 2), pl.BlockSpec((N_EXPERTS, tm), lambda i: (0, i))],
        out_shape=[jax.ShapeDtypeStruct((T_ALL, D_MODEL), F32), jax.ShapeDtypeStruct((T_ALL, D_MODEL // 2), jnp.uint32),
                   jax.ShapeDtypeStruct((N_EXPERTS, T_ALL), F32)],
        compiler_params=_params(("arbitrary",)),
        name="merge_router",
    )(x_ctx, x_lat, oa_ctx, oa_lat, bu, bv, oc_ctx, oc_lat, gt, wa, wb, wc, wo, ws, bs, mods, mods, mods, n2, wr, br)


def _select_kernel(aff_ref, idx_ref, val_ref, *rest, n, cap, row_chunk):
    idx_row_ref = rest[0] if len(rest) == 4 else None
    possel_ref, idx_scr, val_scr = rest[-3:]
    a = aff_ref[...]
    rows = a.shape[0]
    tok = lax.broadcasted_iota(I32, (rows, n), 1)

    def count(ones):
        return jnp.sum(ones, axis=1, keepdims=True)

    def at_least(word):
        return jnp.where(a >= pltpu.bitcast(word, F32), 1, 0)

    def greedy_bits(start, top_bit, keep):
        word = start
        bits = list(range(top_bit, -1, -1))
        if rows > SEARCH_PAIR_MAX_ROWS:
            for bit in bits:
                cand = word | (1 << bit)
                word = jnp.where(keep(cand), cand, word)
            return word
        if len(bits) % 2:
            cand = word | (1 << bits[0])
            word = jnp.where(keep(cand), cand, word)
            bits = bits[1:]
        for hi, lo in zip(bits[0::2], bits[1::2]):
            c_lo, c_hi, c_both = word | (1 << lo), word | (1 << hi), word | (1 << hi) | (1 << lo)
            word = jnp.where(keep(c_both), c_both, jnp.where(keep(c_hi), c_hi, jnp.where(keep(c_lo), c_lo, word)))
        return word

    thr = greedy_bits(jnp.zeros((rows, 1), I32), 30, lambda w: count(at_least(w)) >= cap)
    above = at_least(thr + 1)
    tied = at_least(thr) - above
    need = cap - count(above)
    last = greedy_bits(jnp.zeros((rows, 1), I32), n.bit_length() - 2,
                       lambda w: count(jnp.where(tok < w, tied, 0)) < need)
    sel = above + jnp.where(tok <= last, tied, 0)

    blk = min(n, 256)
    tri = jnp.where(lax.broadcasted_iota(I32, (blk, blk), 0) <= lax.broadcasted_iota(I32, (blk, blk), 1),
                    1.0, 0.0).astype(BF16)
    sel_f = sel.astype(F32)
    offset = jnp.zeros((rows, 1), F32)
    for j in range(n // blk):
        s_blk = sel_f[:, j * blk:(j + 1) * blk]
        incl = jnp.dot(s_blk.astype(BF16), tri, preferred_element_type=F32)
        pos = (incl - s_blk + offset).astype(I32)
        possel_ref[:, j * blk:(j + 1) * blk] = jnp.where(sel[:, j * blk:(j + 1) * blk] > 0, pos, -1)
        offset = offset + incl[:, blk - 1:blk]

    tb = min(n, TOKEN_BLOCK)
    n_blk = n // tb

    def fold_lanes(x):
        acc = x[:, :LANE]
        for k in range(1, tb // LANE):
            acc = acc + x[:, k * LANE:(k + 1) * LANE]
        return acc

    def match(e, slot, t0):
        hit = possel_ref[pl.ds(e, 1), pl.ds(t0, tb)] == slot
        tok = t0 + lax.broadcasted_iota(I32, (1, tb), 1)
        return (fold_lanes(jnp.where(hit, tok, 0)),
                fold_lanes(jnp.where(hit, aff_ref[pl.ds(e, 1), pl.ds(t0, tb)], 0.0)))

    def per_row(e, _):
        ends, run = [], 0
        for j in range(n_blk - 1):
            run = run + jnp.sum(jnp.where(possel_ref[pl.ds(e, 1), j * tb:(j + 1) * tb] >= 0, 1, 0))
            ends.append(run)

        def per_chunk(c, _):
            r0 = pl.multiple_of(c * row_chunk, row_chunk)
            slot = lax.broadcasted_iota(I32, (row_chunk, 1), 0) + r0
            if n_blk == 1:
                idx, val = match(e, slot, 0)
            else:
                first = sum(jnp.where(end <= r0, 1, 0) for end in ends)
                last = 1 + sum(jnp.where(end < r0 + row_chunk, 1, 0) for end in ends)

                def per_block(j, acc):
                    i, v = match(e, slot, pl.multiple_of(j * tb, tb))
                    return acc[0] + i, acc[1] + v

                idx, val = lax.fori_loop(first, last, per_block,
                                         (jnp.zeros((row_chunk, LANE), I32), jnp.zeros((row_chunk, LANE), F32)))
            idx_scr[pl.ds(r0, row_chunk), :] = idx
            val_scr[pl.ds(r0, row_chunk), :] = val
            return 0

        lax.fori_loop(0, cap // row_chunk, per_chunk, 0)
        idx = jnp.sum(idx_scr[...], axis=1, keepdims=True)
        idx_ref[e] = idx
        val_ref[e] = jnp.sum(val_scr[...], axis=1, keepdims=True)
        if idx_row_ref is not None:
            idx_row_ref[pl.ds(e, 1), :] = jnp.broadcast_to(idx.astype(F32), (cap, LANE)).T[0:1, :].astype(I32)
        return 0

    def per_small_row(e, _):
        idx, val = match(e, lax.broadcasted_iota(I32, (cap, 1), 0), 0)
        idx_ref[e] = jnp.sum(idx, axis=1, keepdims=True)
        val_ref[e] = jnp.sum(val, axis=1, keepdims=True)
        return 0

    if n_blk == 1 and cap == row_chunk:
        lax.fori_loop(0, rows, per_small_row, 0, unroll=4)
    else:
        lax.fori_loop(0, rows, per_row, 0)


def _select(aff_rows, rows_per_step, cap):
    r, n = aff_rows.shape
    row_chunk = min(cap, 64)
    out_specs = [pl.BlockSpec((rows_per_step, cap, 1), lambda s: (s, 0, 0))] * 2
    out_shape = [jax.ShapeDtypeStruct((r, cap, 1), I32), jax.ShapeDtypeStruct((r, cap, 1), F32)]
    if cap % LANE == 0:
        out_specs.append(pl.BlockSpec((rows_per_step, cap), lambda s: (s, 0)))
        out_shape.append(jax.ShapeDtypeStruct((r, cap), I32))
    return pl.pallas_call(
        functools.partial(_select_kernel, n=n, cap=cap, row_chunk=row_chunk),
        grid=(r // rows_per_step,),
        in_specs=[pl.BlockSpec((rows_per_step, n), lambda s: (s, 0))],
        out_specs=out_specs,
        out_shape=out_shape,
        scratch_shapes=[pltpu.VMEM((rows_per_step, n), I32), pltpu.VMEM((cap, LANE), I32), pltpu.VMEM((cap, LANE), F32)],
        compiler_params=_params(("arbitrary",)),
        name="expert_select",
    )(aff_rows)


CTX_SLOTS = N_EXPERTS * CAP_CTX
TOKEN_BLOCK = 512
SEARCH_PAIR_MAX_ROWS = 32
SLOT_GROUP = 16


CTX_PER_STEP = 4


def _ctx_slot_onehot(idx, slots_on_rows):
    idx = idx.reshape(CTX_SLOTS, 1)
    if slots_on_rows:
        hit = idx == lax.broadcasted_iota(I32, (CTX_SLOTS, SEQ), 1)
    else:
        idx_lane = jnp.broadcast_to(idx.astype(F32), (CTX_SLOTS, LANE)).T[0:1, :]
        hit = idx_lane == lax.broadcasted_iota(I32, (SEQ, CTX_SLOTS), 0).astype(F32)
    return jnp.where(hit, 1.0, 0.0).astype(BF16)


def _gather_ctx_kernel(idx_ref, h_ref, out_ref):
    for r in range(CTX_PER_STEP):
        onehot = _ctx_slot_onehot(idx_ref[r * N_EXPERTS:(r + 1) * N_EXPERTS], True)
        lo, hi = _unpack_halves(h_ref[r * SEQ:(r + 1) * SEQ, :])
        g_lo = jnp.dot(onehot, lo, preferred_element_type=F32)
        g_hi = jnp.dot(onehot, hi, preferred_element_type=F32)
        packed = pltpu.pack_elementwise([g_lo, g_hi], packed_dtype=BF16)
        out_ref[:, r * CAP_CTX:(r + 1) * CAP_CTX, :] = packed.reshape(N_EXPERTS, CAP_CTX, D_MODEL // 2)


def _gather_ctx(idx_c, h2p):
    n = CTX_PER_STEP
    return pl.pallas_call(
        _gather_ctx_kernel,
        grid=(BATCH // n,),
        in_specs=[pl.BlockSpec((n * N_EXPERTS, CAP_CTX, 1), lambda b: (b, 0, 0)),
                  pl.BlockSpec((n * SEQ, D_MODEL // 2), lambda b: (b, 0))],
        out_specs=pl.BlockSpec((N_EXPERTS, n * CAP_CTX, D_MODEL // 2), lambda b: (0, b, 0)),
        out_shape=jax.ShapeDtypeStruct((N_EXPERTS, BATCH * CAP_CTX, D_MODEL // 2), jnp.uint32),
        compiler_params=_params(("arbitrary",)),
        name="gather_ctx",
    )(idx_c, h2p)


def _gather_lat_kernel(idx_ref, src_ref, out_ref):
    base = (pl.program_id(0) * N_EXPERTS + pl.program_id(1)) * CAP_LAT

    def body(it, _):
        r0 = pl.multiple_of(it * SLOT_GROUP, SLOT_GROUP)
        picked = [src_ref[0, pl.ds(idx_ref[base + r0 + k], 1), :] for k in range(SLOT_GROUP)]
        dst = out_ref.at[0, pl.ds(r0, SLOT_GROUP)]
        for k in range(SLOT_GROUP):
            dst[k:k + 1, :] = picked[k]
        return 0

    lax.fori_loop(0, CAP_LAT // SLOT_GROUP, body, 0)


def _gather_lat(idx_flat, h2p3, off):
    return pl.pallas_call(
        _gather_lat_kernel,
        grid_spec=pltpu.PrefetchScalarGridSpec(
            num_scalar_prefetch=1,
            grid=(DEC_BATCH, N_EXPERTS),
            in_specs=[pl.BlockSpec((1, DEC_SEQ, D_MODEL // 2), lambda b, e, idx: (off + b, 0, 0))],
            out_specs=pl.BlockSpec((1, CAP_LAT, D_MODEL // 2), lambda b, e, idx: (e, b, 0)),
        ),
        out_shape=jax.ShapeDtypeStruct((N_EXPERTS, DEC_BATCH * CAP_LAT, D_MODEL // 2), jnp.uint32),
        compiler_params=_params(("arbitrary", "arbitrary")),
        name="gather_lat",
    )(idx_flat, h2p3)


N_CTX_FFN_TILES = BATCH * CAP_CTX // FFN_ROW_TILE


def _ffn_kernel(xc_ref, xl_ref, vc_ref, vl_ref, g2_ref, wg_ref, wu_ref, wd_ref, o_ref, wg_b, wu_b, wd_b):
    j = pl.program_id(1)

    @pl.when(j == 0)
    def _():
        wg_b[...] = wg_ref[0].astype(BF16)
        wu_b[...] = wu_ref[0].astype(BF16)
        wd_b[...] = wd_ref[0].astype(BF16)

    is_ctx = j < N_CTX_FFN_TILES
    x = jnp.where(is_ctx, jnp.concatenate(_unpack_halves(xc_ref[0]), axis=1),
                  jnp.concatenate(_unpack_halves(xl_ref[0]), axis=1))
    g = jnp.dot(x, wg_b[...], preferred_element_type=F32)
    u = jnp.dot(x, wu_b[...], preferred_element_type=F32)
    hh = (g * _sigmoid(g)) * u
    y = jnp.dot(hh.astype(BF16), wd_b[...], preferred_element_type=F32)
    o_ref[0] = (y * jnp.where(is_ctx, vc_ref[...].reshape(FFN_ROW_TILE, 1), vl_ref[...])) * g2_ref[...]


def _expert_ffn(xg_ctx, xg_lat, val_ctx, val_lat, mods, w_gate, w_up, w_down, layer):
    tr = FFN_ROW_TILE
    assert tr == BATCH * CAP_CTX == CAP_LAT
    n_tiles = ROWS_PER_EXPERT // tr
    def wspec(k, n, tiles_held):
        ahead = lambda e, j: jnp.minimum(e + jnp.where(j >= tiles_held, 1, 0), N_EXPERTS - 1)
        return pl.BlockSpec((None, 1, k, n), lambda e, j: (layer, ahead(e, j), 0, 0))

    ctx_tile = lambda j: jnp.minimum(j, N_CTX_FFN_TILES - 1)
    lat_tile = lambda j: jnp.maximum(j - N_CTX_FFN_TILES, 0)
    return pl.pallas_call(
        _ffn_kernel,
        grid=(N_EXPERTS, n_tiles),
        in_specs=[pl.BlockSpec((1, tr, D_MODEL // 2), lambda e, j: (e, ctx_tile(j), 0)),
                  pl.BlockSpec((1, tr, D_MODEL // 2), lambda e, j: (e, lat_tile(j), 0)),
                  pl.BlockSpec((BATCH, None, CAP_CTX, 1), lambda e, j: (0, e, 0, 0)),
                  pl.BlockSpec((None, None, CAP_LAT, 1), lambda e, j: (lat_tile(j), e, 0, 0)),
                  _mod_spec(layer, MOD_G2, lambda e, j: j),
                  wspec(D_MODEL, EXPERT_FF, 1), wspec(D_MODEL, EXPERT_FF, n_tiles - 1),
                  wspec(EXPERT_FF, D_MODEL, n_tiles)],
        out_specs=pl.BlockSpec((1, tr, D_MODEL), lambda e, j: (e, j, 0)),
        out_shape=jax.ShapeDtypeStruct((N_EXPERTS, ROWS_PER_EXPERT, D_MODEL), F32),
        scratch_shapes=[pltpu.VMEM((D_MODEL, EXPERT_FF), BF16), pltpu.VMEM((D_MODEL, EXPERT_FF), BF16),
                        pltpu.VMEM((EXPERT_FF, D_MODEL), BF16)],
        compiler_params=_params(("arbitrary", "arbitrary")),
        name="expert_ffn",
    )(xg_ctx, xg_lat, val_ctx, val_lat, mods, w_gate, w_up, w_down)


def _scatter_ctx_kernel(idx_ref, y_ref, x1_ref, out_ref):
    for r in range(CTX_PER_STEP):
        onehot = _ctx_slot_onehot(idx_ref[r * N_EXPERTS:(r + 1) * N_EXPERTS], False)
        y_hi, y_lo = _split_bf16(y_ref[:, r * CAP_CTX:(r + 1) * CAP_CTX, :].reshape(CTX_SLOTS, D_MODEL))
        moe = jnp.dot(onehot, y_hi, preferred_element_type=F32) + jnp.dot(onehot, y_lo, preferred_element_type=F32)
        out_ref[r * SEQ:(r + 1) * SEQ, :] = x1_ref[r * SEQ:(r + 1) * SEQ, :] + moe


def _scatter_ctx(idx_c, yg, x1):
    n = CTX_PER_STEP
    return pl.pallas_call(
        _scatter_ctx_kernel,
        grid=(BATCH // n,),
        in_specs=[pl.BlockSpec((n * N_EXPERTS, CAP_CTX, 1), lambda b: (b, 0, 0)),
                  pl.BlockSpec((N_EXPERTS, n * CAP_CTX, D_MODEL), lambda b: (0, b, 0)),
                  pl.BlockSpec((n * SEQ, D_MODEL), lambda b: (b, 0))],
        out_specs=pl.BlockSpec((n * SEQ, D_MODEL), lambda b: (b, 0)),
        out_shape=jax.ShapeDtypeStruct((T_CTX, D_MODEL), F32),
        compiler_params=_params(("arbitrary",)),
        name="scatter_ctx",
    )(idx_c, yg, x1)


def _scatter_lat_kernel(idx_ref, y_ref, x1_hbm, out_ref, sem, *, off):
    b, e = pl.program_id(0), pl.program_id(2)

    @pl.when(e == 0)
    def _():
        load = pltpu.make_async_copy(x1_hbm.at[pl.ds(off + b, 1)], out_ref, sem)
        load.start()
        load.wait()

    base = (b * N_EXPERTS + e) * CAP_LAT

    def body(it, _):
        r0 = pl.multiple_of(it * SLOT_GROUP, SLOT_GROUP)
        rows = [idx_ref[base + r0 + k] for k in range(SLOT_GROUP)]
        old = [out_ref[0, pl.ds(rows[k], 1), :] for k in range(SLOT_GROUP)]
        y = y_ref[0, pl.ds(r0, SLOT_GROUP), :]
        for k in range(SLOT_GROUP):
            out_ref[0, pl.ds(rows[k], 1), :] = old[k] + y[k:k + 1, :]
        return 0

    lax.fori_loop(0, CAP_LAT // SLOT_GROUP, body, 0)


def _scatter_lat(idx_flat, yg, x1_3, off):
    blk0 = BATCH * CAP_CTX // CAP_LAT
    return pl.pallas_call(
        functools.partial(_scatter_lat_kernel, off=off),
        grid_spec=pltpu.PrefetchScalarGridSpec(
            num_scalar_prefetch=1,
            grid=(DEC_BATCH, 1, N_EXPERTS),
            in_specs=[pl.BlockSpec((1, CAP_LAT, D_MODEL), lambda b, h, e, idx: (e, blk0 + b, 0)),
                      pl.BlockSpec(memory_space=pl.ANY)],
            out_specs=pl.BlockSpec((1, DEC_SEQ, D_MODEL), lambda b, h, e, idx: (b, 0, 0)),
            scratch_shapes=[pltpu.SemaphoreType.DMA(())],
        ),
        out_shape=jax.ShapeDtypeStruct((DEC_BATCH, DEC_SEQ, D_MODEL), F32),
        compiler_params=_params(("arbitrary", "arbitrary", "arbitrary")),
        name="scatter_lat",
    )(idx_flat, yg, x1_3)


def _rope_tables():
    pos = np.arange(DEC_SEQ)
    freq = (np.float32(ROPE_THETA) ** (-np.arange(ROPE_FREQS, dtype=np.float32) / np.float32(ROPE_FREQS)))
    ang_r = (pos // GRID_W).astype(np.float32)[:, None] * freq.astype(np.float32)
    ang_c = (pos % GRID_W).astype(np.float32)[:, None] * freq.astype(np.float32)
    cos = np.concatenate([np.cos(ang_r)] * 2 + [np.cos(ang_c)] * 2, axis=-1)
    sin = np.concatenate([-np.sin(ang_r), np.sin(ang_r), -np.sin(ang_c), np.sin(ang_c)], axis=-1)
    reps = LANE // HEAD_DIM
    cs = np.concatenate([np.ones((ROW_TILE, LANE)), np.tile(cos, (1, reps))], axis=0).astype(np.float32)
    sn = np.concatenate([np.zeros((ROW_TILE, LANE)), np.tile(sin, (1, reps))], axis=0).astype(np.float32)
    return cs, sn


def _rope_tile(i):
    lat = jnp.maximum(i - N_CTX_TILES, 0) % (DEC_SEQ // ROW_TILE)
    return jnp.where(i < N_CTX_TILES, 0, 1 + lat)


def _qk_gain(q_norm, k_norm):
    q = jnp.tile(q_norm, N_HEADS) * (HEAD_DIM ** -0.5 * LOG2_E)
    return jnp.concatenate([q, jnp.tile(k_norm, N_KV)])[None, :]


def kernel(x_prompt, x_sample, cache_a_k, cache_a_v, cache_c_k, cache_c_v, c, c_ctx, norm1_g, w_mod, b_mod, w_in,
           a_q_norm, a_k_norm, a_sink, b_v_norm, b_ws, b_bs, c_q_norm, c_k_norm, w_a_o, w_b_o, w_c_o, w_out, norm2_g,
           w_router, b_router, w_gate, w_up, w_down):
    cond8 = jnp.concatenate([c_ctx[None, :], c, jnp.zeros((8 - N_REQ, D_MODEL), F32)], axis=0)
    mods = _modulation(cond8, w_mod, b_mod).reshape(DEPTH, 8, 1, 6 * D_MODEL)

    cs, sn = _rope_tables()
    w_in_b = w_in.astype(BF16)
    wa_b, wb_b, wc_b, wo_b = w_a_o.astype(BF16), w_b_o.astype(BF16), w_c_o.astype(BF16), w_out.astype(BF16)
    ws_b = b_ws.astype(BF16)
    wr_pad = jnp.pad(w_router, ((0, 0), (0, 0), (0, LANE - N_EXPERTS))).astype(BF16)
    br_pad = jnp.pad(b_router, ((0, 0), (0, LANE - N_EXPERTS)), constant_values=NEG_BIG)

    by_seq = lambda a: a.reshape(T_ALL // SEQ, SEQ, a.shape[-1])
    by_dec = lambda a: a.reshape(T_ALL // DEC_SEQ, DEC_SEQ, a.shape[-1])
    lat_off = T_CTX // DEC_SEQ

    caches = [a.reshape(DEC_BATCH, DEPTH, PAST_LEN, KV_W).astype(BF16)
              for a in (cache_a_k, cache_a_v, cache_c_k, cache_c_v)]

    x_ctx = x_prompt.reshape(T_CTX, D_MODEL)
    x_lat = x_sample.reshape(T_LAT, D_MODEL)
    new_kv = [[], [], [], []]
    for l in range(DEPTH):
        qa, ka_b, va_b, nka, nva, bu, bv, qc, kc_b, vc_b, nkc, nvc, gt = _input_projection(
            x_ctx, x_lat, mods, l, norm1_g[l][None, :], w_in_b, cs, sn,
            _qk_gain(a_q_norm[l], a_k_norm[l]), _qk_gain(c_q_norm[l], c_k_norm[l]), b_v_norm[l][None, :])
        for lst, arr in zip(new_kv, (nka, nva, nkc, nvc)):
            lst.append(arr[:T_CTX].reshape(BATCH, SEQ, N_KV, HEAD_DIM))

        sink = a_sink[l]
        oa_ctx, oc_ctx = _context_attention(by_seq(qa), by_seq(ka_b), by_seq(va_b),
                                            by_seq(qc), by_seq(kc_b), by_seq(vc_b), sink)
        cak, cav, cck, ccv = (a[:, l] for a in caches)
        oa_lat = _window_attention(by_dec(qa), by_dec(ka_b), by_dec(va_b), cak, cav, sink,
                                   n_req=DEC_BATCH, off=lat_off)
        oc_lat = _dense_attention(by_dec(qc), by_dec(kc_b), by_dec(vc_b), (cck, ccv), None,
                                  n_req=DEC_BATCH, off=lat_off, tq=1024, key_chunk=1024)

        bs_full = jnp.repeat(b_bs[l].T, B_GROUP_CH, axis=1)
        x1, h2p, afft = _merge(x_ctx, x_lat, oa_ctx.reshape(T_CTX, Q_W), oa_lat.reshape(T_LAT, Q_W), bu, bv,
                               oc_ctx.reshape(T_CTX, Q_W), oc_lat.reshape(T_LAT, Q_W), gt,
                               wa_b, wb_b, wc_b, wo_b, ws_b, bs_full,
                               mods, l, norm2_g[l][None, :], wr_pad, br_pad[l][None, :])

        aff_rows = lambda a, n_req, n: a.reshape(N_EXPERTS, n_req, n).transpose(1, 0, 2).reshape(n_req * N_EXPERTS, n)
        idx_c, val_c = _select(aff_rows(afft[:, :T_CTX], BATCH, SEQ), BATCH * N_EXPERTS, CAP_CTX)
        _, val_l, idx_l_rows = _select(aff_rows(afft[:, T_CTX:], DEC_BATCH, DEC_SEQ), N_EXPERTS, CAP_LAT)
        idx_l_flat = idx_l_rows.reshape(-1)
        xg_ctx = _gather_ctx(idx_c, h2p)
        xg_lat = _gather_lat(idx_l_flat, by_dec(h2p), lat_off)
        yg = _expert_ffn(xg_ctx, xg_lat, val_c.reshape(BATCH, N_EXPERTS, CAP_CTX, 1),
                         val_l.reshape(DEC_BATCH, N_EXPERTS, CAP_LAT, 1), mods, w_gate, w_up, w_down, l)

        x_ctx = _scatter_ctx(idx_c, yg, x1)
        x_lat = _scatter_lat(idx_l_flat, yg, by_dec(x1), lat_off).reshape(T_LAT, D_MODEL)

    y_prompt = x_ctx.reshape(BATCH, SEQ, D_MODEL)
    y_sample = x_lat.reshape(DEC_BATCH, DEC_SEQ, D_MODEL)
    return (y_prompt, y_sample) + tuple(jnp.stack(lst, axis=1) for lst in new_kv)
```

```python
import functools

import jax
import numpy as np
import jax.numpy as jnp
from jax import lax
from jax.experimental import pallas as pl
from jax.experimental.pallas import tpu as pltpu

F32 = jnp.float32
BF16 = jnp.bfloat16
I32 = jnp.int32

D_MODEL = 1024
BATCH = 16
SEQ = 256
DEPTH = 2
DEC_BATCH = 2
DEC_SEQ = 4096
PAST_LEN = 256
GRID_W = 64
HEAD_DIM = 64
N_HEADS = 6
N_KV = 2
N_GRP = N_HEADS // N_KV
B_GROUPS = 4
B_GROUP_CH = 64
B_WIDTH = B_GROUPS * B_GROUP_CH
Q_W = N_HEADS * HEAD_DIM
KV_W = N_KV * HEAD_DIM
QK_W = Q_W + KV_W
N_BRANCH = 3
WINDOW = 128
BLOCK = 128
CHUNK = 128
N_EXPERTS = 16
EXPERT_FF = 1024
CAP_FACTOR = 2
ROPE_THETA = 10000.0
ROPE_FREQS = HEAD_DIM // 4
EPS = 1e-6
IN_WIDTH = 2 * (QK_W + KV_W) + 2 * B_WIDTH + N_BRANCH * D_MODEL

T_CTX = BATCH * SEQ
T_LAT = DEC_BATCH * DEC_SEQ
T_ALL = T_CTX + T_LAT
N_REQ = 1 + DEC_BATCH
CAP_CTX = CAP_FACTOR * SEQ // N_EXPERTS
CAP_LAT = CAP_FACTOR * DEC_SEQ // N_EXPERTS
ROWS_PER_EXPERT = BATCH * CAP_CTX + DEC_BATCH * CAP_LAT

LANE = 128
ROW_TILE = 512
N_CTX_TILES = T_CTX // ROW_TILE
FFN_ROW_TILE = 512
VMEM_LIMIT = 56 * 1024 * 1024
NEG_BIG = -1e30
LOG2_E = 1.4426950408889634

OFF_A = 0
OFF_AV = OFF_A + QK_W
OFF_BU = OFF_AV + KV_W
OFF_BV = OFF_BU + B_WIDTH
OFF_C = OFF_BV + B_WIDTH
OFF_CV = OFF_C + QK_W
OFF_G = OFF_CV + KV_W


def _params(sem, vmem=VMEM_LIMIT):
    return pltpu.CompilerParams(dimension_semantics=sem, vmem_limit_bytes=vmem)


def _sigmoid(x):
    return 1.0 / (1.0 + jnp.exp(-x))


def _gelu_tanh(x):
    return 0.5 * x * (1.0 + jnp.tanh(0.7978845608028654 * (x + 0.044715 * (x * x * x))))


def _split_bf16(x):
    hi = x.astype(BF16)
    lo = (x - hi.astype(F32)).astype(BF16)
    return hi, lo


def _mod_kernel(c_ref, w_ref, b_ref, o_ref):
    c = c_ref[...]
    s_hi, s_lo = _split_bf16(c * _sigmoid(c))
    w_hi, w_lo = _split_bf16(w_ref[0])
    acc = jnp.dot(s_hi, w_hi, preferred_element_type=F32)
    acc += jnp.dot(s_lo, w_hi, preferred_element_type=F32)
    acc += jnp.dot(s_hi, w_lo, preferred_element_type=F32)
    o_ref[0] = acc + b_ref[0]


def _modulation(cond8, w_mod, b_mod):
    n_col = 6 * D_MODEL // D_MODEL
    return pl.pallas_call(
        _mod_kernel,
        grid=(DEPTH, n_col),
        in_specs=[
            pl.BlockSpec((8, D_MODEL), lambda l, j: (0, 0)),
            pl.BlockSpec((1, D_MODEL, D_MODEL), lambda l, j: (l, 0, j)),
            pl.BlockSpec((1, 1, D_MODEL), lambda l, j: (l, 0, j)),
        ],
        out_specs=pl.BlockSpec((1, 8, D_MODEL), lambda l, j: (l, 0, j)),
        out_shape=jax.ShapeDtypeStruct((DEPTH, 8, 6 * D_MODEL), F32),
        compiler_params=_params(("arbitrary", "arbitrary")),
        name="modulation",
    )(cond8, w_mod, b_mod.reshape(DEPTH, 1, 6 * D_MODEL))


def _group_sumsq(y, bd_ref):
    return jnp.dot((y * y).astype(BF16), bd_ref[...], preferred_element_type=F32)


def _pick_pass(i, ctx_ref, lat_ref):
    return jnp.where(i < N_CTX_TILES, ctx_ref[...], lat_ref[...])


def _in_kernel(xc_ref, xl_ref, sc_ref, sh_ref, n1_ref, w_ref, cs_ref, sn_ref, ga_ref, gc_ref, gbv_ref, bd_qk_ref,
               bd_b_ref, qa_ref, ka_ref, va_ref, nka_ref, nva_ref, bu_ref, bv_ref, qc_ref, kc_ref, vc_ref, nkc_ref,
               nvc_ref, gt_ref):
    x = _pick_pass(pl.program_id(0), xc_ref, xl_ref)
    ms = jnp.mean(x * x, axis=-1, keepdims=True)
    h = x * lax.rsqrt(ms + EPS) * n1_ref[...]
    h = h * (1.0 + sc_ref[...]) + sh_ref[...]
    hb = h.astype(BF16)
    tm = x.shape[0]

    def proj(c0, width):
        return jnp.dot(hb, w_ref[:, c0:c0 + width], preferred_element_type=F32)

    cs = jnp.concatenate([cs_ref[...]] * (QK_W // LANE), axis=1)
    sn = jnp.concatenate([sn_ref[...]] * (QK_W // LANE), axis=1)
    lane = lax.broadcasted_iota(I32, (tm, QK_W), 1)
    first_half = (lane & ROPE_FREQS) == 0

    def qk_post(y, gain_ref):
        yn = y * lax.rsqrt(_group_sumsq(y, bd_qk_ref) * (1.0 / HEAD_DIM) + EPS) * gain_ref[...]
        partner = jnp.where(first_half, pltpu.roll(yn, QK_W - ROPE_FREQS, 1), pltpu.roll(yn, ROPE_FREQS, 1))
        return yn * cs + partner * sn

    def mixer(off_qk, off_v, gain_ref, q_ref, k_ref, v_ref, nk_ref, nv_ref):
        y = qk_post(proj(off_qk, QK_W), gain_ref)
        v = proj(off_v, KV_W)
        q_ref[...] = y[:, :Q_W].astype(BF16)
        k_ref[...] = y[:, Q_W:].astype(BF16)
        v_ref[...] = v.astype(BF16)

        nk_ref[...] = y[:, Q_W:]
        nv_ref[...] = v

    mixer(OFF_A, OFF_AV, ga_ref, qa_ref, ka_ref, va_ref, nka_ref, nva_ref)

    bu_ref[...] = _gelu_tanh(proj(OFF_BU, B_WIDTH)).astype(BF16)
    gv = _gelu_tanh(proj(OFF_BV, B_WIDTH))
    gvn = gv * lax.rsqrt(_group_sumsq(gv, bd_b_ref) * (1.0 / B_GROUP_CH) + EPS) * gbv_ref[...]
    bv_ref[...] = gvn.astype(BF16)

    mixer(OFF_C, OFF_CV, gc_ref, qc_ref, kc_ref, vc_ref, nkc_ref, nvc_ref)

    gate_chunk = 512
    for j in range(N_BRANCH * D_MODEL // gate_chunk):
        g = proj(OFF_G + j * gate_chunk, gate_chunk)
        gt_ref[:, j * gate_chunk:(j + 1) * gate_chunk] = _sigmoid(g).astype(BF16)


def _req_of_tile(i):
    return i // N_CTX_TILES


def _ctx_rows(w):
    return pl.BlockSpec((ROW_TILE, w), lambda i: (jnp.minimum(i, N_CTX_TILES - 1), 0))


def _lat_rows(w):
    return pl.BlockSpec((ROW_TILE, w), lambda i: (jnp.maximum(i - N_CTX_TILES, 0), 0))


MOD_SH1, MOD_SC1, MOD_G1, MOD_SH2, MOD_SC2, MOD_G2 = range(6)


def _layer_spec(stacked, layer):
    rest = stacked.shape[1:]
    return pl.BlockSpec((None,) + rest, lambda *g: (layer,) + (0,) * len(rest))


def _mod_spec(layer, chunk, req):
    return pl.BlockSpec((None, None, 1, D_MODEL), lambda *g: (layer, req(*g), 0, chunk))


def _block_diag_ones(width, group):
    g = np.arange(width) // group
    return (g[:, None] == g[None, :]).astype(np.float32)


def _input_projection(x_ctx, x_lat, mods, layer, n1, w_in_b, cs, sn, gain_a, gain_c, gain_bv):
    bd_qk = jnp.asarray(_block_diag_ones(QK_W, HEAD_DIM), BF16)
    bd_b = jnp.asarray(_block_diag_ones(B_WIDTH, B_GROUP_CH), BF16)
    tm = ROW_TILE
    row = lambda w: pl.BlockSpec((tm, w), lambda i: (i, 0))
    full = lambda a: pl.BlockSpec(a.shape, lambda i: (0,) * a.ndim)
    rope = pl.BlockSpec((tm, LANE), lambda i: (_rope_tile(i), 0))
    cache_rows = T_CTX + tm
    spare = lambda w: pl.BlockSpec((tm, w), lambda i: (jnp.minimum(i, N_CTX_TILES), 0))
    mixer_outs = [(Q_W, BF16, T_ALL), (KV_W, BF16, T_ALL), (KV_W, BF16, T_ALL), (KV_W, F32, cache_rows),
                  (KV_W, F32, cache_rows)]
    outs = mixer_outs + [(B_WIDTH, BF16, T_ALL), (B_WIDTH, BF16, T_ALL)] + mixer_outs + [(N_BRANCH * D_MODEL, BF16, T_ALL)]
    return pl.pallas_call(
        _in_kernel,
        grid=(T_ALL // tm,),
        in_specs=[_ctx_rows(D_MODEL), _lat_rows(D_MODEL), _mod_spec(layer, MOD_SC1, _req_of_tile),
                  _mod_spec(layer, MOD_SH1, _req_of_tile), full(n1), _layer_spec(w_in_b, layer), rope, rope,
                  full(gain_a), full(gain_c), full(gain_bv), full(bd_qk), full(bd_b)],
        out_specs=[row(w) if rows == T_ALL else spare(w) for w, _, rows in outs],
        out_shape=[jax.ShapeDtypeStruct((rows, w), dt) for w, dt, rows in outs],
        compiler_params=_params(("arbitrary",)),
        name="input_projection",
    )(x_ctx, x_lat, mods, mods, n1, w_in_b, cs, sn, gain_a, gain_c, gain_bv, bd_qk, bd_b)


def _banded_start(carry, banded, qt_scr, heads, lo, hi, tq):
    k_refs, v_refs, biases = banded
    m0, acc0 = carry
    k_loc = jnp.concatenate([r[0, :, lo:hi] for r in k_refs], axis=0)
    vt = jnp.concatenate([r[0].astype(F32).T[lo:hi, :].astype(BF16) for r in v_refs], axis=1)
    vt = jnp.concatenate([vt, jnp.ones((DEN_ROWS, vt.shape[1]), BF16)], axis=0)
    pick = lambda x, j: jnp.concatenate([x[:, g * tq + j * BAND_Q:g * tq + (j + 1) * BAND_Q] for g in range(N_GRP)], axis=1)
    ms, accs = [], []
    for j in range(tq // BAND_Q):
        keys = slice(j * BAND_Q, (j + 1) * BAND_Q + 2 * BLOCK)
        qt_j = jnp.concatenate([qt_scr[h * HEAD_DIM:(h + 1) * HEAD_DIM, j * BAND_Q:(j + 1) * BAND_Q] for h in heads], axis=1)
        s = jnp.dot(k_loc[keys], qt_j, preferred_element_type=F32) + jnp.concatenate([biases[j]] * N_GRP, axis=1)
        m_old = pick(m0, j)
        m_new = jnp.maximum(m_old, jnp.max(s, axis=0, keepdims=True))
        p = jnp.exp2(s - m_new).astype(BF16)
        accs.append(pick(acc0, j) * jnp.exp2(m_old - m_new) + jnp.dot(vt[:, keys], p, preferred_element_type=F32))
        ms.append(m_new)
    gather = lambda parts: jnp.concatenate(
        [parts[j][:, g * BAND_Q:(g + 1) * BAND_Q] for g in range(N_GRP) for j in range(tq // BAND_Q)], axis=1)
    return gather(ms), gather(accs)


def _attention_tile(q_ref, sources, sink_ref, o_ref, qt_scr, ot_scr, *, tq, key_chunk, banded=None):
    width = N_GRP * tq
    for j in range(Q_W // LANE):
        qt_scr[j * LANE:(j + 1) * LANE, :] = q_ref[0, :, j * LANE:(j + 1) * LANE].astype(F32).T.astype(BF16)
    for kv in range(N_KV):
        lo, hi = kv * HEAD_DIM, (kv + 1) * HEAD_DIM
        heads = [kv * N_GRP + g for g in range(N_GRP)]
        qt = jnp.concatenate([qt_scr[h * HEAD_DIM:(h + 1) * HEAD_DIM, :] for h in heads], axis=1)

        def step(carry, kref, vref, c0, size):
            m, acc = carry
            s = jnp.dot(kref[0, pl.ds(c0, size), lo:hi], qt, preferred_element_type=F32)
            vt =vref[0, pl.ds(c0, size), :].astype(F32).T[lo:hi, :].astype(BF16)
            vt = jnp.concatenate([vt, jnp.ones((DEN_ROWS, size), BF16)], axis=0)
            m_new = jnp.maximum(m, jnp.max(s, axis=0, keepdims=True))
            p = jnp.exp2(s - m_new).astype(BF16)
            acc = acc * jnp.exp2(m - m_new) + jnp.dot(vt, p, preferred_element_type=F32)
            return m_new, acc

        if sink_ref is not None:
            m0 = jnp.concatenate([jnp.full((1, tq), sink_ref[h] * LOG2_E, F32) for h in heads], axis=1)
            den0 = jnp.ones((DEN_ROWS, width), F32)
        else:
            m0 = jnp.full((1, width), NEG_BIG, F32)
            den0 = jnp.zeros((DEN_ROWS, width), F32)
        carry = (m0, jnp.concatenate([jnp.zeros((HEAD_DIM, width), F32), den0], axis=0))
        if banded is not None:
            carry = _banded_start(carry, banded, qt_scr, heads, lo, hi, tq)
        for kref, vref in sources:
            n_rows = kref.shape[1]
            n_full = n_rows // key_chunk
            if n_full > 1:
                carry = lax.fori_loop(
                    0, n_full,
                    lambda c, cr: step(cr, kref, vref, pl.multiple_of(c * key_chunk, key_chunk), key_chunk), carry)
            elif n_full == 1:
                carry = step(carry, kref, vref, 0, key_chunk)
            if n_rows - n_full * key_chunk:
                carry = step(carry, kref, vref, n_full * key_chunk, n_rows - n_full * key_chunk)
        _, acc = carry
        o = acc[:HEAD_DIM] / acc[HEAD_DIM:HEAD_DIM + 1]
        for g, h in enumerate(heads):
            ot_scr[h * HEAD_DIM:(h + 1) * HEAD_DIM, :] = o[:, g * tq:(g + 1) * tq]
    for j in range(Q_W // LANE):
        o_ref[0, :, j * LANE:(j + 1) * LANE] = ot_scr[j * LANE:(j + 1) * LANE, :].T.astype(o_ref.dtype)


def _dense_attn_kernel(*refs, tq, key_chunk, has_extra, has_sink):
    refs = list(refs)
    q_ref, k_ref, v_ref = refs[:3]
    del refs[:3]
    sources = [(k_ref, v_ref)]
    if has_extra:
        sources.append((refs.pop(0), refs.pop(0)))
    sink_ref = refs.pop(0) if has_sink else None
    o_ref, qt_scr, ot_scr = refs
    _attention_tile(q_ref, sources, sink_ref, o_ref, qt_scr, ot_scr, tq=tq, key_chunk=key_chunk)


def _attention_scratch(tq):
    return [pltpu.VMEM((Q_W, tq), BF16), pltpu.VMEM((Q_W, tq), F32)]


def _dense_attention(q, k, v, extra, sink, *, n_req, off, tq, key_chunk):
    s = q.shape[1]
    kv_spec = pl.BlockSpec((1, s, KV_W), lambda i, j: (off + i, 0, 0))
    in_specs = [pl.BlockSpec((1, tq, Q_W), lambda i, j: (off + i, j, 0)), kv_spec, kv_spec]
    args = [q, k, v]
    if extra is not None:
        in_specs += [pl.BlockSpec((1, extra[0].shape[1], KV_W), lambda i, j: (i, 0, 0))] * 2
        args += list(extra)
    if sink is not None:
        in_specs.append(pl.BlockSpec(memory_space=pltpu.SMEM))
        args.append(sink)
    return pl.pallas_call(
        functools.partial(_dense_attn_kernel, tq=tq, key_chunk=key_chunk, has_extra=extra is not None,
                          has_sink=sink is not None),
        grid=(n_req, s // tq),
        in_specs=in_specs,
        out_specs=pl.BlockSpec((1, tq, Q_W), lambda i, j: (i, j, 0)),
        out_shape=jax.ShapeDtypeStruct((n_req, s, Q_W), BF16),
        scratch_shapes=_attention_scratch(tq),
        compiler_params=_params(("arbitrary", "arbitrary")),
        name="dense_attention",
    )(*args)


CTX_ATTN_PER_STEP = 2


def _ctx_attn_kernel(qa_ref, ka_ref, va_ref, qc_ref, kc_ref, vc_ref, sink_ref, oa_ref, oc_ref, *scratch):
    for r in range(CTX_ATTN_PER_STEP):
        one = lambda ref: ref.at[pl.ds(r, 1)]
        qta, ota, qtc, otc = scratch[4 * r:4 * r + 4]
        _attention_tile(one(qa_ref), [(one(ka_ref), one(va_ref))], sink_ref, one(oa_ref), qta, ota,
                        tq=SEQ, key_chunk=SEQ)
        _attention_tile(one(qc_ref), [(one(kc_ref), one(vc_ref))], None, one(oc_ref), qtc, otc,
                        tq=SEQ, key_chunk=SEQ)


def _context_attention(qa, ka, va, qc, kc, vc, sink):
    n = CTX_ATTN_PER_STEP
    q_spec = pl.BlockSpec((n, SEQ, Q_W), lambda i: (i, 0, 0))
    kv_spec = pl.BlockSpec((n, SEQ, KV_W), lambda i: (i, 0, 0))
    return pl.pallas_call(
        _ctx_attn_kernel,
        grid=(BATCH // n,),
        in_specs=[q_spec, kv_spec, kv_spec, q_spec, kv_spec, kv_spec, pl.BlockSpec(memory_space=pltpu.SMEM)],
        out_specs=[q_spec, q_spec],
        out_shape=[jax.ShapeDtypeStruct((BATCH, SEQ, Q_W), BF16)] * 2,
        scratch_shapes=_attention_scratch(SEQ) * (2 * n),
        compiler_params=_params(("arbitrary",)),
        name="context_attention",
    )(qa, ka, va, qc, kc, vc, sink)


DEN_ROWS = 16
WINDOW_TQ = 512
BAND_Q = 2 * BLOCK


def _window_attn_kernel(q_ref, kp_ref, kc_ref, kn_ref, vp_ref, vc_ref, vn_ref, ck_ref, cv_ref, band_ref, sink_ref, o_ref,
                        qt_scr, ot_scr, *, seq):
    q_pos0 = pl.program_id(1) * WINDOW_TQ
    n_grp = WINDOW_TQ // BAND_Q
    band = band_ref[...]
    hide_prev = jnp.where(q_pos0 >= BLOCK, 0.0, NEG_BIG)
    hide_next = jnp.where(q_pos0 + WINDOW_TQ < seq, 0.0, NEG_BIG)
    first = jnp.concatenate([band[:BLOCK] + hide_prev, band[BLOCK:]], axis=0)
    last = jnp.concatenate([band[:BAND_Q + BLOCK], band[BAND_Q + BLOCK:] + hide_next], axis=0)
    biases = [first] + [band] * (n_grp - 2) + [last]
    banded = ((kp_ref, kc_ref, kn_ref), (vp_ref, vc_ref, vn_ref), biases)
    _attention_tile(q_ref, [(ck_ref, cv_ref)], sink_ref, o_ref, qt_scr, ot_scr, tq=WINDOW_TQ,
                    key_chunk=PAST_LEN, banded=banded)


def _band_bias():
    d = (np.arange(BAND_Q + 2 * BLOCK) - BLOCK)[:, None] - np.arange(BAND_Q)[None, :]
    return np.where(np.abs(d) <= WINDOW, 0.0, NEG_BIG).astype(np.float32)


def _window_attention(q, k, v, ck, cv, sink, *, n_req, off):
    b, s = n_req, q.shape[1]
    nb = s // BLOCK
    per_tile = WINDOW_TQ // BLOCK
    edge = lambda f: pl.BlockSpec((1, BLOCK, KV_W), lambda i, j: (off + i, f(j), 0))
    prev = lambda j: jnp.maximum(j * per_tile - 1, 0)
    nxt = lambda j: jnp.minimum((j + 1) * per_tile, nb - 1)
    cur = pl.BlockSpec((1, WINDOW_TQ, KV_W), lambda i, j: (off + i, j, 0))
    ctx = pl.BlockSpec((1, PAST_LEN, KV_W), lambda i, j: (i, 0, 0))
    assert WINDOW_TQ // BAND_Q >= 2
    band = _band_bias()
    return pl.pallas_call(
        functools.partial(_window_attn_kernel, seq=s),
        grid=(b, s // WINDOW_TQ),
        in_specs=[pl.BlockSpec((1, WINDOW_TQ, Q_W), lambda i, j: (off + i, j, 0)),
                  edge(prev), cur, edge(nxt), edge(prev), cur, edge(nxt), ctx, ctx,
                  pl.BlockSpec(band.shape, lambda i, j: (0, 0)),
                  pl.BlockSpec(memory_space=pltpu.SMEM)],
        out_specs=pl.BlockSpec((1, WINDOW_TQ, Q_W), lambda i, j: (i, j, 0)),
        out_shape=jax.ShapeDtypeStruct((b, s, Q_W), BF16),
        scratch_shapes=_attention_scratch(WINDOW_TQ),
        compiler_params=_params(("arbitrary", "arbitrary")),
        name="window_attention",
    )(q, k, k, k, v, v, v, ck, cv, band, sink)


def _pack_halves(x):
    half = x.shape[1] // 2
    return pltpu.pack_elementwise([x[:, :half], x[:, half:]], packed_dtype=BF16)


def _unpack_halves(words):
    return tuple(pltpu.unpack_elementwise(words, index=i, packed_dtype=BF16, unpacked_dtype=F32).astype(BF16)
                 for i in range(2))


def _merge_kernel(xc_ref, xl_ref, oac_ref, oal_ref, bu_ref, bv_ref, occ_ref, ocl_ref, gt_ref, wa_ref, wb_ref, wc_ref,
                  wo_ref, ws_ref, bs_ref, g1_ref, sc2_ref, sh2_ref, n2_ref, wr_ref, br_ref, x1_ref, h2p_ref, afft_ref):
    i = pl.program_id(0)
    tm = xc_ref.shape[0]
    group = lax.broadcasted_iota(I32, (CHUNK, B_WIDTH), 1) // B_GROUP_CH
    obs = []
    for c in range(tm // CHUNK):
        v = bv_ref[c * CHUNK:(c + 1) * CHUNK, :]
        sv = jnp.zeros((CHUNK, B_WIDTH), F32)
        for g in range(B_GROUPS):
            sv = jnp.where(group == g, jnp.dot(ws_ref[g], v, preferred_element_type=F32), sv)
        u = bu_ref[c * CHUNK:(c + 1) * CHUNK, :].astype(F32)
        obs.append((u * (sv + bs_ref[...])).astype(BF16))
    ob = jnp.concatenate(obs, axis=0)

    oa = _pick_pass(i, oac_ref, oal_ref)
    oc = _pick_pass(i, occ_ref, ocl_ref)
    merged = gt_ref[:, 0:D_MODEL].astype(F32) * jnp.dot(oa, wa_ref[...], preferred_element_type=F32)
    merged += gt_ref[:, D_MODEL:2 * D_MODEL].astype(F32) * jnp.dot(ob, wb_ref[...], preferred_element_type=F32)
    merged += gt_ref[:, 2 * D_MODEL:3 * D_MODEL].astype(F32) * jnp.dot(oc, wc_ref[...], preferred_element_type=F32)
    y = jnp.dot(merged.astype(BF16), wo_ref[...], preferred_element_type=F32)
    x1 = _pick_pass(i, xc_ref, xl_ref) + g1_ref[...] * y
    x1_ref[...] = x1

    ms = jnp.mean(x1 * x1, axis=-1, keepdims=True)
    h2 = x1 * lax.rsqrt(ms + EPS) * n2_ref[...]
    h2 = h2 * (1.0 + sc2_ref[...]) + sh2_ref[...]
    h2p_ref[...] = _pack_halves(h2)

    logits = jnp.dot(h2.astype(BF16), wr_ref[...], preferred_element_type=F32) + br_ref[...]
    e = jnp.exp(logits - jnp.max(logits, axis=-1, keepdims=True))
    aff = e / jnp.sum(e, axis=-1, keepdims=True)
    afft_ref[...] = aff.T[:N_EXPERTS, :]


def _merge(x_ctx, x_lat, oa_ctx, oa_lat, bu, bv, oc_ctx, oc_lat, gt, wa, wb, wc, wo, ws, bs, mods, layer, n2, wr, br):
    tm = ROW_TILE
    row = lambda w: pl.BlockSpec((tm, w), lambda i: (i, 0))
    full = lambda a: pl.BlockSpec(a.shape, lambda i: (0,) * a.ndim)
    mod = lambda chunk: _mod_spec(layer, chunk, _req_of_tile)
    stack = lambda a: _layer_spec(a, layer)
    return pl.pallas_call(
        _merge_kernel,
        grid=(T_ALL // tm,),
        in_specs=[_ctx_rows(D_MODEL), _lat_rows(D_MODEL), _ctx_rows(Q_W), _lat_rows(Q_W), row(B_WIDTH), row(B_WIDTH),
                  _ctx_rows(Q_W), _lat_rows(Q_W), row(N_BRANCH * D_MODEL),
                  stack(wa), stack(wb), stack(wc), stack(wo), stack(ws), full(bs),
                  mod(MOD_G1), mod(MOD_SC2), mod(MOD_SH2), full(n2), stack(wr), full(br)],
        out_specs=[row(D_MODEL), row(D_MODEL // 2), pl.BlockSpec((N_EXPERTS, tm), lambda i: (0, i))],
        out_shape=[jax.ShapeDtypeStruct((T_ALL, D_MODEL), F32), jax.ShapeDtypeStruct((T_ALL, D_MODEL // 2), jnp.uint32),
                   jax.ShapeDtypeStruct((N_EXPERTS, T_ALL), F32)],
        compiler_params=_params(("arbitrary",)),
        name="merge_router",
    )(x_ctx, x_lat, oa_ctx, oa_lat, bu, bv, oc_ctx, oc_lat, gt, wa, wb, wc, wo, ws, bs, mods, mods, mods, n2, wr, br)


def _select_kernel(aff_ref, idx_ref, val_ref, *rest, n, cap, row_chunk):
    idx_row_ref = rest[0] if len(rest) == 4 else None
    possel_ref, idx_scr, val_scr = rest[-3:]
    a = aff_ref[...]
    rows = a.shape[0]
    tok = lax.broadcasted_iota(I32, (rows, n), 1)

    def count(ones):
        return jnp.sum(ones, axis=1, keepdims=True)

    def at_least(word):
        return jnp.where(a >= pltpu.bitcast(word, F32), 1, 0)

    def greedy_bits(start, top_bit, keep):
        word = start
        bits = list(range(top_bit, -1, -1))
        if rows > SEARCH_PAIR_MAX_ROWS:
            for bit in bits:
                cand = word | (1 << bit)
                word = jnp.where(keep(cand), cand, word)
            return word
        if len(bits) % 2:
            cand = word | (1 << bits[0])
            word = jnp.where(keep(cand), cand, word)
            bits = bits[1:]
        for hi, lo in zip(bits[0::2], bits[1::2]):
            c_lo, c_hi, c_both = word | (1 << lo), word | (1 << hi), word | (1 << hi) | (1 << lo)
            word = jnp.where(keep(c_both), c_both, jnp.where(keep(c_hi), c_hi, jnp.where(keep(c_lo), c_lo, word)))
        return word

    thr = greedy_bits(jnp.zeros((rows, 1), I32), 30, lambda w: count(at_least(w)) >= cap)
    above = at_least(thr + 1)
    tied = at_least(thr) - above
    need = cap - count(above)
    last = greedy_bits(jnp.zeros((rows, 1), I32), n.bit_length() - 2,
                       lambda w: count(jnp.where(tok < w, tied, 0)) < need)
    sel = above + jnp.where(tok <= last, tied, 0)

    blk = min(n, 256)
    tri = jnp.where(lax.broadcasted_iota(I32, (blk, blk), 0) <= lax.broadcasted_iota(I32, (blk, blk), 1),
                    1.0, 0.0).astype(BF16)
    sel_f = sel.astype(F32)
    offset = jnp.zeros((rows, 1), F32)
    for j in range(n // blk):
        s_blk = sel_f[:, j * blk:(j + 1) * blk]
        incl = jnp.dot(s_blk.astype(BF16), tri, preferred_element_type=F32)
        pos = (incl - s_blk + offset).astype(I32)
        possel_ref[:, j * blk:(j + 1) * blk] = jnp.where(sel[:, j * blk:(j + 1) * blk] > 0, pos, -1)
        offset = offset + incl[:, blk - 1:blk]

    tb = min(n, TOKEN_BLOCK)
    n_blk = n // tb

    def fold_lanes(x):
        acc = x[:, :LANE]
        for k in range(1, tb // LANE):
            acc = acc + x[:, k * LANE:(k + 1) * LANE]
        return acc

    def match(e, slot, t0):
        hit = possel_ref[pl.ds(e, 1), pl.ds(t0, tb)] == slot
        tok = t0 + lax.broadcasted_iota(I32, (1, tb), 1)
        return (fold_lanes(jnp.where(hit, tok, 0)),
                fold_lanes(jnp.where(hit, aff_ref[pl.ds(e, 1), pl.ds(t0, tb)], 0.0)))

    def per_row(e, _):
        ends, run = [], 0
        for j in range(n_blk - 1):
            run = run + jnp.sum(jnp.where(possel_ref[pl.ds(e, 1), j * tb:(j + 1) * tb] >= 0, 1, 0))
            ends.append(run)

        def per_chunk(c, _):
            r0 = pl.multiple_of(c * row_chunk, row_chunk)
            slot = lax.broadcasted_iota(I32, (row_chunk, 1), 0) + r0
            if n_blk == 1:
                idx, val = match(e, slot, 0)
            else:
                first = sum(jnp.where(end <= r0, 1, 0) for end in ends)
                last = 1 + sum(jnp.where(end < r0 + row_chunk, 1, 0) for end in ends)

                def per_block(j, acc):
                    i, v = match(e, slot, pl.multiple_of(j * tb, tb))
                    return acc[0] + i, acc[1] + v

                idx, val = lax.fori_loop(first, last, per_block,
                                         (jnp.zeros((row_chunk, LANE), I32), jnp.zeros((row_chunk, LANE), F32)))
            idx_scr[pl.ds(r0, row_chunk), :] = idx
            val_scr[pl.ds(r0, row_chunk), :] = val
            return 0

        lax.fori_loop(0, cap // row_chunk, per_chunk, 0)
        idx = jnp.sum(idx_scr[...], axis=1, keepdims=True)
        idx_ref[e] = idx
        val_ref[e] = jnp.sum(val_scr[...], axis=1, keepdims=True)
        if idx_row_ref is not None:
            idx_row_ref[pl.ds(e, 1), :] = jnp.broadcast_to(idx.astype(F32), (cap, LANE)).T[0:1, :].astype(I32)
        return 0

    def per_small_row(e, _):
        idx, val = match(e, lax.broadcasted_iota(I32, (cap, 1), 0), 0)
        idx_ref[e] = jnp.sum(idx, axis=1, keepdims=True)
        val_ref[e] = jnp.sum(val, axis=1, keepdims=True)
        return 0

    if n_blk == 1 and cap == row_chunk:
        lax.fori_loop(0, rows, per_small_row, 0, unroll=4)
    else:
        lax.fori_loop(0, rows, per_row, 0)


def _select(aff_rows, rows_per_step, cap):
    r, n = aff_rows.shape
    row_chunk = min(cap, 64)
    out_specs = [pl.BlockSpec((rows_per_step, cap, 1), lambda s: (s, 0, 0))] * 2
    out_shape = [jax.ShapeDtypeStruct((r, cap, 1), I32), jax.ShapeDtypeStruct((r, cap, 1), F32)]
    if cap % LANE == 0:
        out_specs.append(pl.BlockSpec((rows_per_step, cap), lambda s: (s, 0)))
        out_shape.append(jax.ShapeDtypeStruct((r, cap), I32))
    return pl.pallas_call(
        functools.partial(_select_kernel, n=n, cap=cap, row_chunk=row_chunk),
        grid=(r // rows_per_step,),
        in_specs=[pl.BlockSpec((rows_per_step, n), lambda s: (s, 0))],
        out_specs=out_specs,
        out_shape=out_shape,
        scratch_shapes=[pltpu.VMEM((rows_per_step, n), I32), pltpu.VMEM((cap, LANE), I32), pltpu.VMEM((cap, LANE), F32)],
        compiler_params=_params(("arbitrary",)),
        name="expert_select",
    )(aff_rows)


CTX_SLOTS = N_EXPERTS * CAP_CTX
TOKEN_BLOCK = 512
SEARCH_PAIR_MAX_ROWS = 32
SLOT_GROUP = 16


CTX_PER_STEP = 4


def _ctx_slot_onehot(idx, slots_on_rows):
    idx = idx.reshape(CTX_SLOTS, 1)
    if slots_on_rows:
        hit = idx == lax.broadcasted_iota(I32, (CTX_SLOTS, SEQ), 1)
    else:
        idx_lane = jnp.broadcast_to(idx.astype(F32), (CTX_SLOTS, LANE)).T[0:1, :]
        hit = idx_lane == lax.broadcasted_iota(I32, (SEQ, CTX_SLOTS), 0).astype(F32)
    return jnp.where(hit, 1.0, 0.0).astype(BF16)


def _gather_ctx_kernel(idx_ref, h_ref, out_ref):
    for r in range(CTX_PER_STEP):
        onehot = _ctx_slot_onehot(idx_ref[r * N_EXPERTS:(r + 1) * N_EXPERTS], True)
        lo, hi = _unpack_halves(h_ref[r * SEQ:(r + 1) * SEQ, :])
        g_lo = jnp.dot(onehot, lo, preferred_element_type=F32)
        g_hi = jnp.dot(onehot, hi, preferred_element_type=F32)
        packed = pltpu.pack_elementwise([g_lo, g_hi], packed_dtype=BF16)
        out_ref[:, r * CAP_CTX:(r + 1) * CAP_CTX, :] = packed.reshape(N_EXPERTS, CAP_CTX, D_MODEL // 2)


def _gather_ctx(idx_c, h2p):
    n = CTX_PER_STEP
    return pl.pallas_call(
        _gather_ctx_kernel,
        grid=(BATCH // n,),
        in_specs=[pl.BlockSpec((n * N_EXPERTS, CAP_CTX, 1), lambda b: (b, 0, 0)),
                  pl.BlockSpec((n * SEQ, D_MODEL // 2), lambda b: (b, 0))],
        out_specs=pl.BlockSpec((N_EXPERTS, n * CAP_CTX, D_MODEL // 2), lambda b: (0, b, 0)),
        out_shape=jax.ShapeDtypeStruct((N_EXPERTS, BATCH * CAP_CTX, D_MODEL // 2), jnp.uint32),
        compiler_params=_params(("arbitrary",)),
        name="gather_ctx",
    )(idx_c, h2p)


def _gather_lat_kernel(idx_ref, src_ref, out_ref):
    base = (pl.program_id(0) * N_EXPERTS + pl.program_id(1)) * CAP_LAT

    def body(it, _):
        r0 = pl.multiple_of(it * SLOT_GROUP, SLOT_GROUP)
        picked = [src_ref[0, pl.ds(idx_ref[base + r0 + k], 1), :] for k in range(SLOT_GROUP)]
        dst = out_ref.at[0, pl.ds(r0, SLOT_GROUP)]
        for k in range(SLOT_GROUP):
            dst[k:k + 1, :] = picked[k]
        return 0

    lax.fori_loop(0, CAP_LAT // SLOT_GROUP, body, 0)


def _gather_lat(idx_flat, h2p3, off):
    return pl.pallas_call(
        _gather_lat_kernel,
        grid_spec=pltpu.PrefetchScalarGridSpec(
            num_scalar_prefetch=1,
            grid=(DEC_BATCH, N_EXPERTS),
            in_specs=[pl.BlockSpec((1, DEC_SEQ, D_MODEL // 2), lambda b, e, idx: (off + b, 0, 0))],
            out_specs=pl.BlockSpec((1, CAP_LAT, D_MODEL // 2), lambda b, e, idx: (e, b, 0)),
        ),
        out_shape=jax.ShapeDtypeStruct((N_EXPERTS, DEC_BATCH * CAP_LAT, D_MODEL // 2), jnp.uint32),
        compiler_params=_params(("arbitrary", "arbitrary")),
        name="gather_lat",
    )(idx_flat, h2p3)


N_CTX_FFN_TILES = BATCH * CAP_CTX // FFN_ROW_TILE


def _ffn_kernel(xc_ref, xl_ref, vc_ref, vl_ref, g2_ref, wg_ref, wu_ref, wd_ref, o_ref):
    is_ctx = pl.program_id(1) < N_CTX_FFN_TILES
    x = jnp.where(is_ctx, jnp.concatenate(_unpack_halves(xc_ref[0]), axis=1),
                  jnp.concatenate(_unpack_halves(xl_ref[0]), axis=1)).astype(F32)
    g = jnp.dot(x, wg_ref[0], preferred_element_type=F32)
    u = jnp.dot(x, wu_ref[0], preferred_element_type=F32)
    hh = (g * _sigmoid(g)) * u
    y = jnp.dot(hh.astype(BF16).astype(F32), wd_ref[0], preferred_element_type=F32)
    o_ref[0] = (y * jnp.where(is_ctx, vc_ref[...].reshape(FFN_ROW_TILE, 1), vl_ref[...])) * g2_ref[...]


def _expert_ffn(xg_ctx, xg_lat, val_ctx, val_lat, mods, w_gate, w_up, w_down, layer):
    tr = FFN_ROW_TILE
    assert tr == BATCH * CAP_CTX == CAP_LAT
    n_tiles = ROWS_PER_EXPERT // tr
    wspec = lambda k, n: pl.BlockSpec((None, 1, k, n), lambda e, j: (layer, e, 0, 0))

    ctx_tile = lambda j: jnp.minimum(j, N_CTX_FFN_TILES - 1)
    lat_tile = lambda j: jnp.maximum(j - N_CTX_FFN_TILES, 0)
    return pl.pallas_call(
        _ffn_kernel,
        grid=(N_EXPERTS, n_tiles),
        in_specs=[pl.BlockSpec((1, tr, D_MODEL // 2), lambda e, j: (e, ctx_tile(j), 0)),
                  pl.BlockSpec((1, tr, D_MODEL // 2), lambda e, j: (e, lat_tile(j), 0)),
                  pl.BlockSpec((BATCH, None, CAP_CTX, 1), lambda e, j: (0, e, 0, 0)),
                  pl.BlockSpec((None, None, CAP_LAT, 1), lambda e, j: (lat_tile(j), e, 0, 0)),
                  _mod_spec(layer, MOD_G2, lambda e, j: j),
                  wspec(D_MODEL, EXPERT_FF), wspec(D_MODEL, EXPERT_FF), wspec(EXPERT_FF, D_MODEL)],
        out_specs=pl.BlockSpec((1, tr, D_MODEL), lambda e, j: (e, j, 0)),
        out_shape=jax.ShapeDtypeStruct((N_EXPERTS, ROWS_PER_EXPERT, D_MODEL), F32),
        compiler_params=_params(("arbitrary", "arbitrary")),
        name="expert_ffn",
    )(xg_ctx, xg_lat, val_ctx, val_lat, mods, w_gate, w_up, w_down)


def _scatter_ctx_kernel(idx_ref, y_ref, x1_ref, out_ref):
    for r in range(CTX_PER_STEP):
        onehot = _ctx_slot_onehot(idx_ref[r * N_EXPERTS:(r + 1) * N_EXPERTS], False)
        y_hi, y_lo = _split_bf16(y_ref[:, r * CAP_CTX:(r + 1) * CAP_CTX, :].reshape(CTX_SLOTS, D_MODEL))
        moe = jnp.dot(onehot, y_hi, preferred_element_type=F32) + jnp.dot(onehot, y_lo, preferred_element_type=F32)
        out_ref[r * SEQ:(r + 1) * SEQ, :] = x1_ref[r * SEQ:(r + 1) * SEQ, :] + moe


def _scatter_ctx(idx_c, yg, x1):
    n = CTX_PER_STEP
    return pl.pallas_call(
        _scatter_ctx_kernel,
        grid=(BATCH // n,),
        in_specs=[pl.BlockSpec((n * N_EXPERTS, CAP_CTX, 1), lambda b: (b, 0, 0)),
                  pl.BlockSpec((N_EXPERTS, n * CAP_CTX, D_MODEL), lambda b: (0, b, 0)),
                  pl.BlockSpec((n * SEQ, D_MODEL), lambda b: (b, 0))],
        out_specs=pl.BlockSpec((n * SEQ, D_MODEL), lambda b: (b, 0)),
        out_shape=jax.ShapeDtypeStruct((T_CTX, D_MODEL), F32),
        compiler_params=_params(("arbitrary",)),
        name="scatter_ctx",
    )(idx_c, yg, x1)


def _scatter_lat_kernel(idx_ref, y_ref, x1_hbm, out_ref, sem, *, off):
    b, e = pl.program_id(0), pl.program_id(2)

    @pl.when(e == 0)
    def _():
        load = pltpu.make_async_copy(x1_hbm.at[pl.ds(off + b, 1)], out_ref, sem)
        load.start()
        load.wait()

    base = (b * N_EXPERTS + e) * CAP_LAT

    def body(it, _):
        r0 = pl.multiple_of(it * SLOT_GROUP, SLOT_GROUP)
        rows = [idx_ref[base + r0 + k] for k in range(SLOT_GROUP)]
        old = [out_ref[0, pl.ds(rows[k], 1), :] for k in range(SLOT_GROUP)]
        y = y_ref[0, pl.ds(r0, SLOT_GROUP), :]
        for k in range(SLOT_GROUP):
            out_ref[0, pl.ds(rows[k], 1), :] = old[k] + y[k:k + 1, :]
        return 0

    lax.fori_loop(0, CAP_LAT // SLOT_GROUP, body, 0)


def _scatter_lat(idx_flat, yg, x1_3, off):
    blk0 = BATCH * CAP_CTX // CAP_LAT
    return pl.pallas_call(
        functools.partial(_scatter_lat_kernel, off=off),
        grid_spec=pltpu.PrefetchScalarGridSpec(
            num_scalar_prefetch=1,
            grid=(DEC_BATCH, 1, N_EXPERTS),
            in_specs=[pl.BlockSpec((1, CAP_LAT, D_MODEL), lambda b, h, e, idx: (e, blk0 + b, 0)),
                      pl.BlockSpec(memory_space=pl.ANY)],
            out_specs=pl.BlockSpec((1, DEC_SEQ, D_MODEL), lambda b, h, e, idx: (b, 0, 0)),
            scratch_shapes=[pltpu.SemaphoreType.DMA(())],
        ),
        out_shape=jax.ShapeDtypeStruct((DEC_BATCH, DEC_SEQ, D_MODEL), F32),
        compiler_params=_params(("arbitrary", "arbitrary", "arbitrary")),
        name="scatter_lat",
    )(idx_flat, yg, x1_3)


def _rope_tables():
    pos = np.arange(DEC_SEQ)
    freq = (np.float32(ROPE_THETA) ** (-np.arange(ROPE_FREQS, dtype=np.float32) / np.float32(ROPE_FREQS)))
    ang_r = (pos // GRID_W).astype(np.float32)[:, None] * freq.astype(np.float32)
    ang_c = (pos % GRID_W).astype(np.float32)[:, None] * freq.astype(np.float32)
    cos = np.concatenate([np.cos(ang_r)] * 2 + [np.cos(ang_c)] * 2, axis=-1)
    sin = np.concatenate([-np.sin(ang_r), np.sin(ang_r), -np.sin(ang_c), np.sin(ang_c)], axis=-1)
    reps = LANE // HEAD_DIM
    cs = np.concatenate([np.ones((ROW_TILE, LANE)), np.tile(cos, (1, reps))], axis=0).astype(np.float32)
    sn = np.concatenate([np.zeros((ROW_TILE, LANE)), np.tile(sin, (1, reps))], axis=0).astype(np.float32)
    return cs, sn


def _rope_tile(i):
    lat = jnp.maximum(i - N_CTX_TILES, 0) % (DEC_SEQ // ROW_TILE)
    return jnp.where(i < N_CTX_TILES, 0, 1 + lat)


def _qk_gain(q_norm, k_norm):
    q = jnp.tile(q_norm, N_HEADS) * (HEAD_DIM ** -0.5 * LOG2_E)
    return jnp.concatenate([q, jnp.tile(k_norm, N_KV)])[None, :]


def kernel(x_prompt, x_sample, cache_a_k, cache_a_v, cache_c_k, cache_c_v, c, c_ctx, norm1_g, w_mod, b_mod, w_in,
           a_q_norm, a_k_norm, a_sink, b_v_norm, b_ws, b_bs, c_q_norm, c_k_norm, w_a_o, w_b_o, w_c_o, w_out, norm2_g,
           w_router, b_router, w_gate, w_up, w_down):
    cond8 = jnp.concatenate([c_ctx[None, :], c, jnp.zeros((8 - N_REQ, D_MODEL), F32)], axis=0)
    mods = _modulation(cond8, w_mod, b_mod).reshape(DEPTH, 8, 1, 6 * D_MODEL)

    cs, sn = _rope_tables()
    w_in_b = w_in.astype(BF16)
    wa_b, wb_b, wc_b, wo_b = w_a_o.astype(BF16), w_b_o.astype(BF16), w_c_o.astype(BF16), w_out.astype(BF16)
    ws_b = b_ws.astype(BF16)
    wr_pad = jnp.pad(w_router, ((0, 0), (0, 0), (0, LANE - N_EXPERTS))).astype(BF16)
    br_pad = jnp.pad(b_router, ((0, 0), (0, LANE - N_EXPERTS)), constant_values=NEG_BIG)

    by_seq = lambda a: a.reshape(T_ALL // SEQ, SEQ, a.shape[-1])
    by_dec = lambda a: a.reshape(T_ALL // DEC_SEQ, DEC_SEQ, a.shape[-1])
    lat_off = T_CTX // DEC_SEQ

    caches = [a.reshape(DEC_BATCH, DEPTH, PAST_LEN, KV_W).astype(BF16)
              for a in (cache_a_k, cache_a_v, cache_c_k, cache_c_v)]

    x_ctx = x_prompt.reshape(T_CTX, D_MODEL)
    x_lat = x_sample.reshape(T_LAT, D_MODEL)
    new_kv = [[], [], [], []]
    for l in range(DEPTH):
        qa, ka_b, va_b, nka, nva, bu, bv, qc, kc_b, vc_b, nkc, nvc, gt = _input_projection(
            x_ctx, x_lat, mods, l, norm1_g[l][None, :], w_in_b, cs, sn,
            _qk_gain(a_q_norm[l], a_k_norm[l]), _qk_gain(c_q_norm[l], c_k_norm[l]), b_v_norm[l][None, :])
        for lst, arr in zip(new_kv, (nka, nva, nkc, nvc)):
            lst.append(arr[:T_CTX].reshape(BATCH, SEQ, N_KV, HEAD_DIM))

        sink = a_sink[l]
        oa_ctx, oc_ctx = _context_attention(by_seq(qa), by_seq(ka_b), by_seq(va_b),
                                            by_seq(qc), by_seq(kc_b), by_seq(vc_b), sink)
        cak, cav, cck, ccv = (a[:, l] for a in caches)
        oa_lat = _window_attention(by_dec(qa), by_dec(ka_b), by_dec(va_b), cak, cav, sink,
                                   n_req=DEC_BATCH, off=lat_off)
        oc_lat = _dense_attention(by_dec(qc), by_dec(kc_b), by_dec(vc_b), (cck, ccv), None,
                                  n_req=DEC_BATCH, off=lat_off, tq=1024, key_chunk=1024)

        bs_full = jnp.repeat(b_bs[l].T, B_GROUP_CH, axis=1)
        x1, h2p, afft = _merge(x_ctx, x_lat, oa_ctx.reshape(T_CTX, Q_W), oa_lat.reshape(T_LAT, Q_W), bu, bv,
                               oc_ctx.reshape(T_CTX, Q_W), oc_lat.reshape(T_LAT, Q_W), gt,
                               wa_b, wb_b, wc_b, wo_b, ws_b, bs_full,
                               mods, l, norm2_g[l][None, :], wr_pad, br_pad[l][None, :])

        aff_rows = lambda a, n_req, n: a.reshape(N_EXPERTS, n_req, n).transpose(1, 0, 2).reshape(n_req * N_EXPERTS, n)
        idx_c, val_c = _select(aff_rows(afft[:, :T_CTX], BATCH, SEQ), BATCH * N_EXPERTS, CAP_CTX)
        _, val_l, idx_l_rows = _select(aff_rows(afft[:, T_CTX:], DEC_BATCH, DEC_SEQ), N_EXPERTS, CAP_LAT)
        idx_l_flat = idx_l_rows.reshape(-1)
        xg_ctx = _gather_ctx(idx_c, h2p)
        xg_lat = _gather_lat(idx_l_flat, by_dec(h2p), lat_off)
        yg = _expert_ffn(xg_ctx, xg_lat, val_c.reshape(BATCH, N_EXPERTS, CAP_CTX, 1),
                         val_l.reshape(DEC_BATCH, N_EXPERTS, CAP_LAT, 1), mods, w_gate, w_up, w_down, l)

        x_ctx = _scatter_ctx(idx_c, yg, x1)
        x_lat = _scatter_lat(idx_l_flat, yg, by_dec(x1), lat_off).reshape(T_LAT, D_MODEL)

    y_prompt = x_ctx.reshape(BATCH, SEQ, D_MODEL)
    y_sample = x_lat.reshape(DEC_BATCH, DEC_SEQ, D_MODEL)
    return (y_prompt, y_sample) + tuple(jnp.stack(lst, axis=1) for lst in new_kv)
```

```python
import functools

import jax
import numpy as np
import jax.numpy as jnp
from jax import lax
from jax.experimental import pallas as pl
from jax.experimental.pallas import tpu as pltpu

F32 = jnp.float32
BF16 = jnp.bfloat16
I32 = jnp.int32

D_MODEL = 1024
BATCH = 16
SEQ = 256
DEPTH = 2
DEC_BATCH = 2
DEC_SEQ = 4096
PAST_LEN = 256
GRID_W = 64
HEAD_DIM = 64
N_HEADS = 6
N_KV = 2
N_GRP = N_HEADS // N_KV
B_GROUPS = 4
B_GROUP_CH = 64
B_WIDTH = B_GROUPS * B_GROUP_CH
Q_W = N_HEADS * HEAD_DIM
KV_W = N_KV * HEAD_DIM
QK_W = Q_W + KV_W
N_BRANCH = 3
WINDOW = 128
BLOCK = 128
CHUNK = 128
N_EXPERTS = 16
EXPERT_FF = 1024
CAP_FACTOR = 2
ROPE_THETA = 10000.0
ROPE_FREQS = HEAD_DIM // 4
EPS = 1e-6
IN_WIDTH = 2 * (QK_W + KV_W) + 2 * B_WIDTH + N_BRANCH * D_MODEL

T_CTX = BATCH * SEQ
T_LAT = DEC_BATCH * DEC_SEQ
T_ALL = T_CTX + T_LAT
N_REQ = 1 + DEC_BATCH
CAP_CTX = CAP_FACTOR * SEQ // N_EXPERTS
CAP_LAT = CAP_FACTOR * DEC_SEQ // N_EXPERTS
ROWS_PER_EXPERT = BATCH * CAP_CTX + DEC_BATCH * CAP_LAT

LANE = 128
ROW_TILE = 512
N_CTX_TILES = T_CTX // ROW_TILE
FFN_ROW_TILE = 512
VMEM_LIMIT = 56 * 1024 * 1024
NEG_BIG = -1e30
LOG2_E = 1.4426950408889634

OFF_A = 0
OFF_AV = OFF_A + QK_W
OFF_BU = OFF_AV + KV_W
OFF_BV = OFF_BU + B_WIDTH
OFF_C = OFF_BV + B_WIDTH
OFF_CV = OFF_C + QK_W
OFF_G = OFF_CV + KV_W


def _params(sem, vmem=VMEM_LIMIT):
    return pltpu.CompilerParams(dimension_semantics=sem, vmem_limit_bytes=vmem)


def _sigmoid(x):
    return 1.0 / (1.0 + jnp.exp(-x))


def _gelu_tanh(x):
    return 0.5 * x * (1.0 + jnp.tanh(0.7978845608028654 * (x + 0.044715 * (x * x * x))))


def _split_bf16(x):
    hi = x.astype(BF16)
    lo = (x - hi.astype(F32)).astype(BF16)
    return hi, lo


def _mod_kernel(c_ref, w_ref, b_ref, o_ref):
    c = c_ref[...]
    s_hi, s_lo = _split_bf16(c * _sigmoid(c))
    w_hi, w_lo = _split_bf16(w_ref[0])
    acc = jnp.dot(s_hi, w_hi, preferred_element_type=F32)
    acc += jnp.dot(s_lo, w_hi, preferred_element_type=F32)
    acc += jnp.dot(s_hi, w_lo, preferred_element_type=F32)
    o_ref[0] = acc + b_ref[0]


def _modulation(cond8, w_mod, b_mod):
    n_col = 6 * D_MODEL // D_MODEL
    return pl.pallas_call(
        _mod_kernel,
        grid=(DEPTH, n_col),
        in_specs=[
            pl.BlockSpec((8, D_MODEL), lambda l, j: (0, 0)),
            pl.BlockSpec((1, D_MODEL, D_MODEL), lambda l, j: (l, 0, j)),
            pl.BlockSpec((1, 1, D_MODEL), lambda l, j: (l, 0, j)),
        ],
        out_specs=pl.BlockSpec((1, 8, D_MODEL), lambda l, j: (l, 0, j)),
        out_shape=jax.ShapeDtypeStruct((DEPTH, 8, 6 * D_MODEL), F32),
        compiler_params=_params(("arbitrary", "arbitrary")),
        name="modulation",
    )(cond8, w_mod, b_mod.reshape(DEPTH, 1, 6 * D_MODEL))


def _group_sumsq(y, bd_ref):
    return jnp.dot((y * y).astype(BF16), bd_ref[...], preferred_element_type=F32)


def _pick_pass(i, ctx_ref, lat_ref):
    return jnp.where(i < N_CTX_TILES, ctx_ref[...], lat_ref[...])


def _in_kernel(xc_ref, xl_ref, sc_ref, sh_ref, n1_ref, w_ref, cs_ref, sn_ref, ga_ref, gc_ref, gbv_ref, bd_qk_ref,
               bd_b_ref, qa_ref, ka_ref, va_ref, nka_ref, nva_ref, bu_ref, bv_ref, qc_ref, kc_ref, vc_ref, nkc_ref,
               nvc_ref, gt_ref):
    x = _pick_pass(pl.program_id(0), xc_ref, xl_ref)
    ms = jnp.mean(x * x, axis=-1, keepdims=True)
    h = x * lax.rsqrt(ms + EPS) * n1_ref[...]
    h = h * (1.0 + sc_ref[...]) + sh_ref[...]
    hb = h.astype(BF16)
    tm = x.shape[0]

    def proj(c0, width):
        return jnp.dot(hb, w_ref[:, c0:c0 + width], preferred_element_type=F32)

    cs = jnp.concatenate([cs_ref[...]] * (QK_W // LANE), axis=1)
    sn = jnp.concatenate([sn_ref[...]] * (QK_W // LANE), axis=1)
    lane = lax.broadcasted_iota(I32, (tm, QK_W), 1)
    first_half = (lane & ROPE_FREQS) == 0

    def qk_post(y, gain_ref):
        yn = y * lax.rsqrt(_group_sumsq(y, bd_qk_ref) * (1.0 / HEAD_DIM) + EPS) * gain_ref[...]
        partner = jnp.where(first_half, pltpu.roll(yn, QK_W - ROPE_FREQS, 1), pltpu.roll(yn, ROPE_FREQS, 1))
        return yn * cs + partner * sn

    def mixer(off_qk, off_v, gain_ref, q_ref, k_ref, v_ref, nk_ref, nv_ref):
        y = qk_post(proj(off_qk, QK_W), gain_ref)
        v = proj(off_v, KV_W)
        q_ref[...] = y[:, :Q_W].astype(BF16)
        k_ref[...] = y[:, Q_W:].astype(BF16)
        v_ref[...] = v.astype(BF16)

        nk_ref[...] = y[:, Q_W:]
        nv_ref[...] = v

    mixer(OFF_A, OFF_AV, ga_ref, qa_ref, ka_ref, va_ref, nka_ref, nva_ref)

    bu_ref[...] = _gelu_tanh(proj(OFF_BU, B_WIDTH)).astype(BF16)
    gv = _gelu_tanh(proj(OFF_BV, B_WIDTH))
    gvn = gv * lax.rsqrt(_group_sumsq(gv, bd_b_ref) * (1.0 / B_GROUP_CH) + EPS) * gbv_ref[...]
    bv_ref[...] = gvn.astype(BF16)

    mixer(OFF_C, OFF_CV, gc_ref, qc_ref, kc_ref, vc_ref, nkc_ref, nvc_ref)

    gate_chunk = 512
    for j in range(N_BRANCH * D_MODEL // gate_chunk):
        g = proj(OFF_G + j * gate_chunk, gate_chunk)
        gt_ref[:, j * gate_chunk:(j + 1) * gate_chunk] = _sigmoid(g).astype(BF16)


def _req_of_tile(i):
    return i // N_CTX_TILES


def _ctx_rows(w):
    return pl.BlockSpec((ROW_TILE, w), lambda i: (jnp.minimum(i, N_CTX_TILES - 1), 0))


def _lat_rows(w):
    return pl.BlockSpec((ROW_TILE, w), lambda i: (jnp.maximum(i - N_CTX_TILES, 0), 0))


MOD_SH1, MOD_SC1, MOD_G1, MOD_SH2, MOD_SC2, MOD_G2 = range(6)


def _layer_spec(stacked, layer):
    rest = stacked.shape[1:]
    return pl.BlockSpec((None,) + rest, lambda *g: (layer,) + (0,) * len(rest))


def _mod_spec(layer, chunk, req):
    return pl.BlockSpec((None, None, 1, D_MODEL), lambda *g: (layer, req(*g), 0, chunk))


def _block_diag_ones(width, group):
    g = np.arange(width) // group
    return (g[:, None] == g[None, :]).astype(np.float32)


def _input_projection(x_ctx, x_lat, mods, layer, n1, w_in_b, cs, sn, gain_a, gain_c, gain_bv):
    bd_qk = jnp.asarray(_block_diag_ones(QK_W, HEAD_DIM), BF16)
    bd_b = jnp.asarray(_block_diag_ones(B_WIDTH, B_GROUP_CH), BF16)
    tm = ROW_TILE
    row = lambda w: pl.BlockSpec((tm, w), lambda i: (i, 0))
    full = lambda a: pl.BlockSpec(a.shape, lambda i: (0,) * a.ndim)
    rope = pl.BlockSpec((tm, LANE), lambda i: (_rope_tile(i), 0))
    cache_rows = T_CTX + tm
    spare = lambda w: pl.BlockSpec((tm, w), lambda i: (jnp.minimum(i, N_CTX_TILES), 0))
    mixer_outs = [(Q_W, BF16, T_ALL), (KV_W, BF16, T_ALL), (KV_W, BF16, T_ALL), (KV_W, F32, cache_rows),
                  (KV_W, F32, cache_rows)]
    outs = mixer_outs + [(B_WIDTH, BF16, T_ALL), (B_WIDTH, BF16, T_ALL)] + mixer_outs + [(N_BRANCH * D_MODEL, BF16, T_ALL)]
    return pl.pallas_call(
        _in_kernel,
        grid=(T_ALL // tm,),
        in_specs=[_ctx_rows(D_MODEL), _lat_rows(D_MODEL), _mod_spec(layer, MOD_SC1, _req_of_tile),
                  _mod_spec(layer, MOD_SH1, _req_of_tile), full(n1), _layer_spec(w_in_b, layer), rope, rope,
                  full(gain_a), full(gain_c), full(gain_bv), full(bd_qk), full(bd_b)],
        out_specs=[row(w) if rows == T_ALL else spare(w) for w, _, rows in outs],
        out_shape=[jax.ShapeDtypeStruct((rows, w), dt) for w, dt, rows in outs],
        compiler_params=_params(("arbitrary",)),
        name="input_projection",
    )(x_ctx, x_lat, mods, mods, n1, w_in_b, cs, sn, gain_a, gain_c, gain_bv, bd_qk, bd_b)


def _banded_start(carry, banded, qt_scr, heads, lo, hi, tq):
    k_refs, v_refs, biases = banded
    m0, acc0 = carry
    k_loc = jnp.concatenate([r[0, :, lo:hi] for r in k_refs], axis=0)
    vt = jnp.concatenate([r[0].astype(F32).T[lo:hi, :].astype(BF16) for r in v_refs], axis=1)
    vt = jnp.concatenate([vt, jnp.ones((DEN_ROWS, vt.shape[1]), BF16)], axis=0)
    pick = lambda x, j: jnp.concatenate([x[:, g * tq + j * BAND_Q:g * tq + (j + 1) * BAND_Q] for g in range(N_GRP)], axis=1)
    ms, accs = [], []
    for j in range(tq // BAND_Q):
        keys = slice(j * BAND_Q, (j + 1) * BAND_Q + 2 * BLOCK)
        qt_j = jnp.concatenate([qt_scr[h * HEAD_DIM:(h + 1) * HEAD_DIM, j * BAND_Q:(j + 1) * BAND_Q] for h in heads], axis=1)
        s = jnp.dot(k_loc[keys], qt_j, preferred_element_type=F32) + jnp.concatenate([biases[j]] * N_GRP, axis=1)
        m_old = pick(m0, j)
        m_new = jnp.maximum(m_old, jnp.max(s, axis=0, keepdims=True))
        p = jnp.exp2(s - m_new).astype(BF16)
        accs.append(pick(acc0, j) * jnp.exp2(m_old - m_new) + jnp.dot(vt[:, keys], p, preferred_element_type=F32))
        ms.append(m_new)
    gather = lambda parts: jnp.concatenate(
        [parts[j][:, g * BAND_Q:(g + 1) * BAND_Q] for g in range(N_GRP) for j in range(tq // BAND_Q)], axis=1)
    return gather(ms), gather(accs)


def _attention_tile(q_ref, sources, sink_ref, o_ref, qt_scr, ot_scr, *, tq, key_chunk, banded=None):
    width = N_GRP * tq
    for j in range(Q_W // LANE):
        qt_scr[j * LANE:(j + 1) * LANE, :] = q_ref[0, :, j * LANE:(j + 1) * LANE].astype(F32).T.astype(BF16)
    for kv in range(N_KV):
        lo, hi = kv * HEAD_DIM, (kv + 1) * HEAD_DIM
        heads = [kv * N_GRP + g for g in range(N_GRP)]
        qt = jnp.concatenate([qt_scr[h * HEAD_DIM:(h + 1) * HEAD_DIM, :] for h in heads], axis=1)

        def step(carry, kref, vref, c0, size):
            m, acc = carry
            s = jnp.dot(kref[0, pl.ds(c0, size), lo:hi], qt, preferred_element_type=F32)
            vt =vref[0, pl.ds(c0, size), :].astype(F32).T[lo:hi, :].astype(BF16)
            vt = jnp.concatenate([vt, jnp.ones((DEN_ROWS, size), BF16)], axis=0)
            m_new = jnp.maximum(m, jnp.max(s, axis=0, keepdims=True))
            p = jnp.exp2(s - m_new).astype(BF16)
            acc = acc * jnp.exp2(m - m_new) + jnp.dot(vt, p, preferred_element_type=F32)
            return m_new, acc

        if sink_ref is not None:
            m0 = jnp.concatenate([jnp.full((1, tq), sink_ref[h] * LOG2_E, F32) for h in heads], axis=1)
            den0 = jnp.ones((DEN_ROWS, width), F32)
        else:
            m0 = jnp.full((1, width), NEG_BIG, F32)
            den0 = jnp.zeros((DEN_ROWS, width), F32)
        carry = (m0, jnp.concatenate([jnp.zeros((HEAD_DIM, width), F32), den0], axis=0))
        if banded is not None:
            carry = _banded_start(carry, banded, qt_scr, heads, lo, hi, tq)
        for kref, vref in sources:
            n_rows = kref.shape[1]
            n_full = n_rows // key_chunk
            if n_full > 1:
                carry = lax.fori_loop(
                    0, n_full,
                    lambda c, cr: step(cr, kref, vref, pl.multiple_of(c * key_chunk, key_chunk), key_chunk), carry)
            elif n_full == 1:
                carry = step(carry, kref, vref, 0, key_chunk)
            if n_rows - n_full * key_chunk:
                carry = step(carry, kref, vref, n_full * key_chunk, n_rows - n_full * key_chunk)
        _, acc = carry
        o = acc[:HEAD_DIM] / acc[HEAD_DIM:HEAD_DIM + 1]
        for g, h in enumerate(heads):
            ot_scr[h * HEAD_DIM:(h + 1) * HEAD_DIM, :] = o[:, g * tq:(g + 1) * tq]
    for j in range(Q_W // LANE):
        o_ref[0, :, j * LANE:(j + 1) * LANE] = ot_scr[j * LANE:(j + 1) * LANE, :].T.astype(o_ref.dtype)


def _dense_attn_kernel(*refs, tq, key_chunk, has_extra, has_sink):
    refs = list(refs)
    q_ref, k_ref, v_ref = refs[:3]
    del refs[:3]
    sources = [(k_ref, v_ref)]
    if has_extra:
        sources.append((refs.pop(0), refs.pop(0)))
    sink_ref = refs.pop(0) if has_sink else None
    o_ref, qt_scr, ot_scr = refs
    _attention_tile(q_ref, sources, sink_ref, o_ref, qt_scr, ot_scr, tq=tq, key_chunk=key_chunk)


def _attention_scratch(tq):
    return [pltpu.VMEM((Q_W, tq), BF16), pltpu.VMEM((Q_W, tq), F32)]


def _dense_attention(q, k, v, extra, sink, *, n_req, off, tq, key_chunk):
    s = q.shape[1]
    kv_spec = pl.BlockSpec((1, s, KV_W), lambda i, j: (off + i, 0, 0))
    in_specs = [pl.BlockSpec((1, tq, Q_W), lambda i, j: (off + i, j, 0)), kv_spec, kv_spec]
    args = [q, k, v]
    if extra is not None:
        in_specs += [pl.BlockSpec((1, extra[0].shape[1], KV_W), lambda i, j: (i, 0, 0))] * 2
        args += list(extra)
    if sink is not None:
        in_specs.append(pl.BlockSpec(memory_space=pltpu.SMEM))
        args.append(sink)
    return pl.pallas_call(
        functools.partial(_dense_attn_kernel, tq=tq, key_chunk=key_chunk, has_extra=extra is not None,
                          has_sink=sink is not None),
        grid=(n_req, s // tq),
        in_specs=in_specs,
        out_specs=pl.BlockSpec((1, tq, Q_W), lambda i, j: (i, j, 0)),
        out_shape=jax.ShapeDtypeStruct((n_req, s, Q_W), BF16),
        scratch_shapes=_attention_scratch(tq),
        compiler_params=_params(("arbitrary", "arbitrary")),
        name="dense_attention",
    )(*args)


CTX_ATTN_PER_STEP = 2


def _ctx_attn_kernel(qa_ref, ka_ref, va_ref, qc_ref, kc_ref, vc_ref, sink_ref, oa_ref, oc_ref, *scratch):
    for r in range(CTX_ATTN_PER_STEP):
        one = lambda ref: ref.at[pl.ds(r, 1)]
        qta, ota, qtc, otc = scratch[4 * r:4 * r + 4]
        _attention_tile(one(qa_ref), [(one(ka_ref), one(va_ref))], sink_ref, one(oa_ref), qta, ota,
                        tq=SEQ, key_chunk=SEQ)
        _attention_tile(one(qc_ref), [(one(kc_ref), one(vc_ref))], None, one(oc_ref), qtc, otc,
                        tq=SEQ, key_chunk=SEQ)


def _context_attention(qa, ka, va, qc, kc, vc, sink):
    n = CTX_ATTN_PER_STEP
    q_spec = pl.BlockSpec((n, SEQ, Q_W), lambda i: (i, 0, 0))
    kv_spec = pl.BlockSpec((n, SEQ, KV_W), lambda i: (i, 0, 0))
    return pl.pallas_call(
        _ctx_attn_kernel,
        grid=(BATCH // n,),
        in_specs=[q_spec, kv_spec, kv_spec, q_spec, kv_spec, kv_spec, pl.BlockSpec(memory_space=pltpu.SMEM)],
        out_specs=[q_spec, q_spec],
        out_shape=[jax.ShapeDtypeStruct((BATCH, SEQ, Q_W), BF16)] * 2,
        scratch_shapes=_attention_scratch(SEQ) * (2 * n),
        compiler_params=_params(("arbitrary",)),
        name="context_attention",
    )(qa, ka, va, qc, kc, vc, sink)


DEN_ROWS = 16
WINDOW_TQ = 512
BAND_Q = 2 * BLOCK


def _window_attn_kernel(q_ref, kp_ref, kc_ref, kn_ref, vp_ref, vc_ref, vn_ref, ck_ref, cv_ref, band_ref, sink_ref, o_ref,
                        qt_scr, ot_scr, *, seq):
    q_pos0 = pl.program_id(1) * WINDOW_TQ
    n_grp = WINDOW_TQ // BAND_Q
    band = band_ref[...]
    hide_prev = jnp.where(q_pos0 >= BLOCK, 0.0, NEG_BIG)
    hide_next = jnp.where(q_pos0 + WINDOW_TQ < seq, 0.0, NEG_BIG)
    first = jnp.concatenate([band[:BLOCK] + hide_prev, band[BLOCK:]], axis=0)
    last = jnp.concatenate([band[:BAND_Q + BLOCK], band[BAND_Q + BLOCK:] + hide_next], axis=0)
    biases = [first] + [band] * (n_grp - 2) + [last]
    banded = ((kp_ref, kc_ref, kn_ref), (vp_ref, vc_ref, vn_ref), biases)
    _attention_tile(q_ref, [(ck_ref, cv_ref)], sink_ref, o_ref, qt_scr, ot_scr, tq=WINDOW_TQ,
                    key_chunk=PAST_LEN, banded=banded)


def _band_bias():
    d = (np.arange(BAND_Q + 2 * BLOCK) - BLOCK)[:, None] - np.arange(BAND_Q)[None, :]
    return np.where(np.abs(d) <= WINDOW, 0.0, NEG_BIG).astype(np.float32)


def _window_attention(q, k, v, ck, cv, sink, *, n_req, off):
    b, s = n_req, q.shape[1]
    nb = s // BLOCK
    per_tile = WINDOW_TQ // BLOCK
    edge = lambda f: pl.BlockSpec((1, BLOCK, KV_W), lambda i, j: (off + i, f(j), 0))
    prev = lambda j: jnp.maximum(j * per_tile - 1, 0)
    nxt = lambda j: jnp.minimum((j + 1) * per_tile, nb - 1)
    cur = pl.BlockSpec((1, WINDOW_TQ, KV_W), lambda i, j: (off + i, j, 0))
    ctx = pl.BlockSpec((1, PAST_LEN, KV_W), lambda i, j: (i, 0, 0))
    assert WINDOW_TQ // BAND_Q >= 2
    band = _band_bias()
    return pl.pallas_call(
        functools.partial(_window_attn_kernel, seq=s),
        grid=(b, s // WINDOW_TQ),
        in_specs=[pl.BlockSpec((1, WINDOW_TQ, Q_W), lambda i, j: (off + i, j, 0)),
                  edge(prev), cur, edge(nxt), edge(prev), cur, edge(nxt), ctx, ctx,
                  pl.BlockSpec(band.shape, lambda i, j: (0, 0)),
                  pl.BlockSpec(memory_space=pltpu.SMEM)],
        out_specs=pl.BlockSpec((1, WINDOW_TQ, Q_W), lambda i, j: (i, j, 0)),
        out_shape=jax.ShapeDtypeStruct((b, s, Q_W), BF16),
        scratch_shapes=_attention_scratch(WINDOW_TQ),
        compiler_params=_params(("arbitrary", "arbitrary")),
        name="window_attention",
    )(q, k, k, k, v, v, v, ck, cv, band, sink)


def _pack_halves(x):
    half = x.shape[1] // 2
    return pltpu.pack_elementwise([x[:, :half], x[:, half:]], packed_dtype=BF16)


def _unpack_halves(words):
    return tuple(pltpu.unpack_elementwise(words, index=i, packed_dtype=BF16, unpacked_dtype=F32).astype(BF16)
                 for i in range(2))


def _merge_kernel(xc_ref, xl_ref, oac_ref, oal_ref, bu_ref, bv_ref, occ_ref, ocl_ref, gt_ref, wa_ref, wb_ref, wc_ref,
                  wo_ref, ws_ref, bs_ref, g1_ref, sc2_ref, sh2_ref, n2_ref, wr_ref, br_ref, x1_ref, h2p_ref, afft_ref):
    i = pl.program_id(0)
    tm = xc_ref.shape[0]
    group = lax.broadcasted_iota(I32, (CHUNK, B_WIDTH), 1) // B_GROUP_CH
    obs = []
    for c in range(tm // CHUNK):
        v = bv_ref[c * CHUNK:(c + 1) * CHUNK, :]
        sv = jnp.zeros((CHUNK, B_WIDTH), F32)
        for g in range(B_GROUPS):
            sv = jnp.where(group == g, jnp.dot(ws_ref[g], v, preferred_element_type=F32), sv)
        u = bu_ref[c * CHUNK:(c + 1) * CHUNK, :].astype(F32)
        obs.append((u * (sv + bs_ref[...])).astype(BF16))
    ob = jnp.concatenate(obs, axis=0)

    oa = _pick_pass(i, oac_ref, oal_ref)
    oc = _pick_pass(i, occ_ref, ocl_ref)
    merged = gt_ref[:, 0:D_MODEL].astype(F32) * jnp.dot(oa, wa_ref[...], preferred_element_type=F32)
    merged += gt_ref[:, D_MODEL:2 * D_MODEL].astype(F32) * jnp.dot(ob, wb_ref[...], preferred_element_type=F32)
    merged += gt_ref[:, 2 * D_MODEL:3 * D_MODEL].astype(F32) * jnp.dot(oc, wc_ref[...], preferred_element_type=F32)
    y = jnp.dot(merged.astype(BF16), wo_ref[...], preferred_element_type=F32)
    x1 = _pick_pass(i, xc_ref, xl_ref) + g1_ref[...] * y
    x1_ref[...] = x1

    ms = jnp.mean(x1 * x1, axis=-1, keepdims=True)
    h2 = x1 * lax.rsqrt(ms + EPS) * n2_ref[...]
    h2 = h2 * (1.0 + sc2_ref[...]) + sh2_ref[...]
    h2p_ref[...] = _pack_halves(h2)

    logits = jnp.dot(h2.astype(BF16), wr_ref[...], preferred_element_type=F32) + br_ref[...]
    e = jnp.exp(logits - jnp.max(logits, axis=-1, keepdims=True))
    aff = e / jnp.sum(e, axis=-1, keepdims=True)
    afft_ref[...] = aff.T[:N_EXPERTS, :]


def _merge(x_ctx, x_lat, oa_ctx, oa_lat, bu, bv, oc_ctx, oc_lat, gt, wa, wb, wc, wo, ws, bs, mods, layer, n2, wr, br):
    tm = ROW_TILE
    row = lambda w: pl.BlockSpec((tm, w), lambda i: (i, 0))
    full = lambda a: pl.BlockSpec(a.shape, lambda i: (0,) * a.ndim)
    mod = lambda chunk: _mod_spec(layer, chunk, _req_of_tile)
    stack = lambda a: _layer_spec(a, layer)
    return pl.pallas_call(
        _merge_kernel,
        grid=(T_ALL // tm,),
        in_specs=[_ctx_rows(D_MODEL), _lat_rows(D_MODEL), _ctx_rows(Q_W), _lat_rows(Q_W), row(B_WIDTH), row(B_WIDTH),
                  _ctx_rows(Q_W), _lat_rows(Q_W), row(N_BRANCH * D_MODEL),
                  stack(wa), stack(wb), stack(wc), stack(wo), stack(ws), full(bs),
                  mod(MOD_G1), mod(MOD_SC2), mod(MOD_SH2), full(n2), stack(wr), full(br)],
        out_specs=[row(D_MODEL), row(D_MODEL // 2), pl.BlockSpec((N_EXPERTS, tm), lambda i: (0, i))],
        out_shape=[jax.ShapeDtypeStruct((T_ALL, D_MODEL), F32), jax.ShapeDtypeStruct((T_ALL, D_MODEL // 2), jnp.uint32),
                   jax.ShapeDtypeStruct((N_EXPERTS, T_ALL), F32)],
        compiler_params=_params(("arbitrary",)),
        name="merge_router",
    )(x_ctx, x_lat, oa_ctx, oa_lat, bu, bv, oc_ctx, oc_lat, gt, wa, wb, wc, wo, ws, bs, mods, mods, mods, n2, wr, br)


def _select_kernel(aff_ref, idx_ref, val_ref, *rest, n, cap, row_chunk):
    idx_row_ref = rest[0] if len(rest) == 4 else None
    possel_ref, idx_scr, val_scr = rest[-3:]
    a = aff_ref[...]
    rows = a.shape[0]
    tok = lax.broadcasted_iota(I32, (rows, n), 1)

    def count(ones):
        return jnp.sum(ones, axis=1, keepdims=True)

    def at_least(word):
        return jnp.where(a >= pltpu.bitcast(word, F32), 1, 0)

    def greedy_bits(start, top_bit, keep):
        word = start
        bits = list(range(top_bit, -1, -1))
        if rows > SEARCH_PAIR_MAX_ROWS:
            for bit in bits:
                cand = word | (1 << bit)
                word = jnp.where(keep(cand), cand, word)
            return word
        if len(bits) % 2:
            cand = word | (1 << bits[0])
            word = jnp.where(keep(cand), cand, word)
            bits = bits[1:]
        for hi, lo in zip(bits[0::2], bits[1::2]):
            c_lo, c_hi, c_both = word | (1 << lo), word | (1 << hi), word | (1 << hi) | (1 << lo)
            word = jnp.where(keep(c_both), c_both, jnp.where(keep(c_hi), c_hi, jnp.where(keep(c_lo), c_lo, word)))
        return word

    thr = greedy_bits(jnp.zeros((rows, 1), I32), 30, lambda w: count(at_least(w)) >= cap)
    above = at_least(thr + 1)
    tied = at_least(thr) - above
    need = cap - count(above)
    last = greedy_bits(jnp.zeros((rows, 1), I32), n.bit_length() - 2,
                       lambda w: count(jnp.where(tok < w, tied, 0)) < need)
    sel = above + jnp.where(tok <= last, tied, 0)

    blk = min(n, 256)
    tri = jnp.where(lax.broadcasted_iota(I32, (blk, blk), 0) <= lax.broadcasted_iota(I32, (blk, blk), 1),
                    1.0, 0.0).astype(BF16)
    sel_f = sel.astype(F32)
    offset = jnp.zeros((rows, 1), F32)
    for j in range(n // blk):
        s_blk = sel_f[:, j * blk:(j + 1) * blk]
        incl = jnp.dot(s_blk.astype(BF16), tri, preferred_element_type=F32)
        pos = (incl - s_blk + offset).astype(I32)
        possel_ref[:, j * blk:(j + 1) * blk] = jnp.where(sel[:, j * blk:(j + 1) * blk] > 0, pos, -1)
        offset = offset + incl[:, blk - 1:blk]

    tb = min(n, TOKEN_BLOCK)
    n_blk = n // tb

    def fold_lanes(x):
        acc = x[:, :LANE]
        for k in range(1, tb // LANE):
            acc = acc + x[:, k * LANE:(k + 1) * LANE]
        return acc

    def match(e, slot, t0):
        hit = possel_ref[pl.ds(e, 1), pl.ds(t0, tb)] == slot
        tok = t0 + lax.broadcasted_iota(I32, (1, tb), 1)
        return (fold_lanes(jnp.where(hit, tok, 0)),
                fold_lanes(jnp.where(hit, aff_ref[pl.ds(e, 1), pl.ds(t0, tb)], 0.0)))

    def per_row(e, _):
        ends, run = [], 0
        for j in range(n_blk - 1):
            run = run + jnp.sum(jnp.where(possel_ref[pl.ds(e, 1), j * tb:(j + 1) * tb] >= 0, 1, 0))
            ends.append(run)

        def per_chunk(c, _):
            r0 = pl.multiple_of(c * row_chunk, row_chunk)
            slot = lax.broadcasted_iota(I32, (row_chunk, 1), 0) + r0
            if n_blk == 1:
                idx, val = match(e, slot, 0)
            else:
                first = sum(jnp.where(end <= r0, 1, 0) for end in ends)
                last = 1 + sum(jnp.where(end < r0 + row_chunk, 1, 0) for end in ends)

                def per_block(j, acc):
                    i, v = match(e, slot, pl.multiple_of(j * tb, tb))
                    return acc[0] + i, acc[1] + v

                idx, val = lax.fori_loop(first, last, per_block,
                                         (jnp.zeros((row_chunk, LANE), I32), jnp.zeros((row_chunk, LANE), F32)))
            idx_scr[pl.ds(r0, row_chunk), :] = idx
            val_scr[pl.ds(r0, row_chunk), :] = val
            return 0

        lax.fori_loop(0, cap // row_chunk, per_chunk, 0)
        idx = jnp.sum(idx_scr[...], axis=1, keepdims=True)
        idx_ref[e] = idx
        val_ref[e] = jnp.sum(val_scr[...], axis=1, keepdims=True)
        if idx_row_ref is not None:
            idx_row_ref[pl.ds(e, 1), :] = jnp.broadcast_to(idx.astype(F32), (cap, LANE)).T[0:1, :].astype(I32)
        return 0

    def per_small_row(e, _):
        idx, val = match(e, lax.broadcasted_iota(I32, (cap, 1), 0), 0)
        idx_ref[e] = jnp.sum(idx, axis=1, keepdims=True)
        val_ref[e] = jnp.sum(val, axis=1, keepdims=True)
        return 0

    if n_blk == 1 and cap == row_chunk:
        lax.fori_loop(0, rows, per_small_row, 0, unroll=4)
    else:
        lax.fori_loop(0, rows, per_row, 0)


def _select(aff_rows, rows_per_step, cap):
    r, n = aff_rows.shape
    row_chunk = min(cap, 64)
    out_specs = [pl.BlockSpec((rows_per_step, cap, 1), lambda s: (s, 0, 0))] * 2
    out_shape = [jax.ShapeDtypeStruct((r, cap, 1), I32), jax.ShapeDtypeStruct((r, cap, 1), F32)]
    if cap % LANE == 0:
        out_specs.append(pl.BlockSpec((rows_per_step, cap), lambda s: (s, 0)))
        out_shape.append(jax.ShapeDtypeStruct((r, cap), I32))
    return pl.pallas_call(
        functools.partial(_select_kernel, n=n, cap=cap, row_chunk=row_chunk),
        grid=(r // rows_per_step,),
        in_specs=[pl.BlockSpec((rows_per_step, n), lambda s: (s, 0))],
        out_specs=out_specs,
        out_shape=out_shape,
        scratch_shapes=[pltpu.VMEM((rows_per_step, n), I32), pltpu.VMEM((cap, LANE), I32), pltpu.VMEM((cap, LANE), F32)],
        compiler_params=_params(("arbitrary",)),
        name="expert_select",
    )(aff_rows)


CTX_SLOTS = N_EXPERTS * CAP_CTX
TOKEN_BLOCK = 512
SEARCH_PAIR_MAX_ROWS = 32
SLOT_GROUP = 16


CTX_PER_STEP = 4


def _ctx_slot_onehot(idx, slots_on_rows):
    idx = idx.reshape(CTX_SLOTS, 1)
    if slots_on_rows:
        hit = idx == lax.broadcasted_iota(I32, (CTX_SLOTS, SEQ), 1)
    else:
        idx_lane = jnp.broadcast_to(idx.astype(F32), (CTX_SLOTS, LANE)).T[0:1, :]
        hit = idx_lane == lax.broadcasted_iota(I32, (SEQ, CTX_SLOTS), 0).astype(F32)
    return jnp.where(hit, 1.0, 0.0).astype(BF16)


def _gather_ctx_kernel(idx_ref, h_ref, out_ref):
    for r in range(CTX_PER_STEP):
        onehot = _ctx_slot_onehot(idx_ref[r * N_EXPERTS:(r + 1) * N_EXPERTS], True)
        lo, hi = _unpack_halves(h_ref[r * SEQ:(r + 1) * SEQ, :])
        g_lo = jnp.dot(onehot, lo, preferred_element_type=F32)
        g_hi = jnp.dot(onehot, hi, preferred_element_type=F32)
        packed = pltpu.pack_elementwise([g_lo, g_hi], packed_dtype=BF16)
        out_ref[:, r * CAP_CTX:(r + 1) * CAP_CTX, :] = packed.reshape(N_EXPERTS, CAP_CTX, D_MODEL // 2)


def _gather_ctx(idx_c, h2p):
    n = CTX_PER_STEP
    return pl.pallas_call(
        _gather_ctx_kernel,
        grid=(BATCH // n,),
        in_specs=[pl.BlockSpec((n * N_EXPERTS, CAP_CTX, 1), lambda b: (b, 0, 0)),
                  pl.BlockSpec((n * SEQ, D_MODEL // 2), lambda b: (b, 0))],
        out_specs=pl.BlockSpec((N_EXPERTS, n * CAP_CTX, D_MODEL // 2), lambda b: (0, b, 0)),
        out_shape=jax.ShapeDtypeStruct((N_EXPERTS, BATCH * CAP_CTX, D_MODEL // 2), jnp.uint32),
        compiler_params=_params(("arbitrary",)),
        name="gather_ctx",
    )(idx_c, h2p)


def _gather_lat_kernel(idx_ref, src_ref, out_ref):
    base = (pl.program_id(0) * N_EXPERTS + pl.program_id(1)) * CAP_LAT

    def body(it, _):
        r0 = pl.multiple_of(it * SLOT_GROUP, SLOT_GROUP)
        picked = [src_ref[0, pl.ds(idx_ref[base + r0 + k], 1), :] for k in range(SLOT_GROUP)]
        dst = out_ref.at[0, pl.ds(r0, SLOT_GROUP)]
        for k in range(SLOT_GROUP):
            dst[k:k + 1, :] = picked[k]
        return 0

    lax.fori_loop(0, CAP_LAT // SLOT_GROUP, body, 0)


def _gather_lat(idx_flat, h2p3, off):
    return pl.pallas_call(
        _gather_lat_kernel,
        grid_spec=pltpu.PrefetchScalarGridSpec(
            num_scalar_prefetch=1,
            grid=(DEC_BATCH, N_EXPERTS),
            in_specs=[pl.BlockSpec((1, DEC_SEQ, D_MODEL // 2), lambda b, e, idx: (off + b, 0, 0))],
            out_specs=pl.BlockSpec((1, CAP_LAT, D_MODEL // 2), lambda b, e, idx: (e, b, 0)),
        ),
        out_shape=jax.ShapeDtypeStruct((N_EXPERTS, DEC_BATCH * CAP_LAT, D_MODEL // 2), jnp.uint32),
        compiler_params=_params(("arbitrary", "arbitrary")),
        name="gather_lat",
    )(idx_flat, h2p3)


N_CTX_FFN_TILES = BATCH * CAP_CTX // FFN_ROW_TILE


def _ffn_kernel(xc_ref, xl_ref, vc_ref, vl_ref, g2_ref, wg_hbm, wu_hbm, wd_hbm, o_ref, wbuf, sems, *, layer):
    e, j = pl.program_id(0), pl.program_id(1)
    slot = e % 2
    weights = (wg_hbm, wu_hbm, wd_hbm)

    def fetch(m, expert, into):
        return pltpu.make_async_copy(weights[m].at[layer, expert], wbuf.at[into, m], sems.at[into, m])

    @pl.when(jnp.logical_and(e == 0, j == 0))
    def _():
        for m in range(3):
            fetch(m, 0, 0).start()
        for m in range(3):
            fetch(m, 0, 0).wait()

    for m in range(3):
        @pl.when(jnp.logical_and(j == m, e + 1 < N_EXPERTS))
        def _(m=m):
            fetch(m, e + 1, 1 - slot).start()

    @pl.when(jnp.logical_and(j == 0, e > 0))
    def _():
        for m in range(3):
            fetch(m, e, slot).wait()

    is_ctx = j < N_CTX_FFN_TILES
    x = jnp.where(is_ctx, jnp.concatenate(_unpack_halves(xc_ref[0]), axis=1),
                  jnp.concatenate(_unpack_halves(xl_ref[0]), axis=1)).astype(F32)
    g = jnp.dot(x, wbuf[slot, 0], preferred_element_type=F32)
    u = jnp.dot(x, wbuf[slot, 1], preferred_element_type=F32)
    hh = (g * _sigmoid(g)) * u
    y = jnp.dot(hh.astype(BF16).astype(F32), wbuf[slot, 2], preferred_element_type=F32)
    o_ref[0] = (y * jnp.where(is_ctx, vc_ref[...].reshape(FFN_ROW_TILE, 1), vl_ref[...])) * g2_ref[...]


def _expert_ffn(xg_ctx, xg_lat, val_ctx, val_lat, mods, w_gate, w_up, w_down, layer):
    tr = FFN_ROW_TILE
    assert tr == BATCH * CAP_CTX == CAP_LAT
    n_tiles = ROWS_PER_EXPERT // tr
    assert n_tiles >= 3 and D_MODEL == EXPERT_FF
    in_hbm = pl.BlockSpec(memory_space=pl.ANY)

    ctx_tile = lambda j: jnp.minimum(j, N_CTX_FFN_TILES - 1)
    lat_tile = lambda j: jnp.maximum(j - N_CTX_FFN_TILES, 0)
    return pl.pallas_call(
        functools.partial(_ffn_kernel, layer=layer),
        grid=(N_EXPERTS, n_tiles),
        in_specs=[pl.BlockSpec((1, tr, D_MODEL // 2), lambda e, j: (e, ctx_tile(j), 0)),
                  pl.BlockSpec((1, tr, D_MODEL // 2), lambda e, j: (e, lat_tile(j), 0)),
                  pl.BlockSpec((BATCH, None, CAP_CTX, 1), lambda e, j: (0, e, 0, 0)),
                  pl.BlockSpec((None, None, CAP_LAT, 1), lambda e, j: (lat_tile(j), e, 0, 0)),
                  _mod_spec(layer, MOD_G2, lambda e, j: j),
                  in_hbm, in_hbm, in_hbm],
        out_specs=pl.BlockSpec((1, tr, D_MODEL), lambda e, j: (e, j, 0)),
        out_shape=jax.ShapeDtypeStruct((N_EXPERTS, ROWS_PER_EXPERT, D_MODEL), F32),
        scratch_shapes=[pltpu.VMEM((2, 3, D_MODEL, EXPERT_FF), F32), pltpu.SemaphoreType.DMA((2, 3))],
        compiler_params=_params(("arbitrary", "arbitrary")),
        name="expert_ffn",
    )(xg_ctx, xg_lat, val_ctx, val_lat, mods, w_gate, w_up, w_down)


def _scatter_ctx_kernel(idx_ref, y_ref, x1_ref, out_ref):
    for r in range(CTX_PER_STEP):
        onehot = _ctx_slot_onehot(idx_ref[r * N_EXPERTS:(r + 1) * N_EXPERTS], False)
        y_hi, y_lo = _split_bf16(y_ref[:, r * CAP_CTX:(r + 1) * CAP_CTX, :].reshape(CTX_SLOTS, D_MODEL))
        moe = jnp.dot(onehot, y_hi, preferred_element_type=F32) + jnp.dot(onehot, y_lo, preferred_element_type=F32)
        out_ref[r * SEQ:(r + 1) * SEQ, :] = x1_ref[r * SEQ:(r + 1) * SEQ, :] + moe


def _scatter_ctx(idx_c, yg, x1):
    n = CTX_PER_STEP
    return pl.pallas_call(
        _scatter_ctx_kernel,
        grid=(BATCH // n,),
        in_specs=[pl.BlockSpec((n * N_EXPERTS, CAP_CTX, 1), lambda b: (b, 0, 0)),
                  pl.BlockSpec((N_EXPERTS, n * CAP_CTX, D_MODEL), lambda b: (0, b, 0)),
                  pl.BlockSpec((n * SEQ, D_MODEL), lambda b: (b, 0))],
        out_specs=pl.BlockSpec((n * SEQ, D_MODEL), lambda b: (b, 0)),
        out_shape=jax.ShapeDtypeStruct((T_CTX, D_MODEL), F32),
        compiler_params=_params(("arbitrary",)),
        name="scatter_ctx",
    )(idx_c, yg, x1)


def _scatter_lat_kernel(idx_ref, y_ref, x1_hbm, out_ref, sem, *, off):
    b, e = pl.program_id(0), pl.program_id(2)

    @pl.when(e == 0)
    def _():
        load = pltpu.make_async_copy(x1_hbm.at[pl.ds(off + b, 1)], out_ref, sem)
        load.start()
        load.wait()

    base = (b * N_EXPERTS + e) * CAP_LAT

    def body(it, _):
        r0 = pl.multiple_of(it * SLOT_GROUP, SLOT_GROUP)
        rows = [idx_ref[base + r0 + k] for k in range(SLOT_GROUP)]
        old = [out_ref[0, pl.ds(rows[k], 1), :] for k in range(SLOT_GROUP)]
        y = y_ref[0, pl.ds(r0, SLOT_GROUP), :]
        for k in range(SLOT_GROUP):
            out_ref[0, pl.ds(rows[k], 1), :] = old[k] + y[k:k + 1, :]
        return 0

    lax.fori_loop(0, CAP_LAT // SLOT_GROUP, body, 0)


def _scatter_lat(idx_flat, yg, x1_3, off):
    blk0 = BATCH * CAP_CTX // CAP_LAT
    return pl.pallas_call(
        functools.partial(_scatter_lat_kernel, off=off),
        grid_spec=pltpu.PrefetchScalarGridSpec(
            num_scalar_prefetch=1,
            grid=(DEC_BATCH, 1, N_EXPERTS),
            in_specs=[pl.BlockSpec((1, CAP_LAT, D_MODEL), lambda b, h, e, idx: (e, blk0 + b, 0)),
                      pl.BlockSpec(memory_space=pl.ANY)],
            out_specs=pl.BlockSpec((1, DEC_SEQ, D_MODEL), lambda b, h, e, idx: (b, 0, 0)),
            scratch_shapes=[pltpu.SemaphoreType.DMA(())],
        ),
        out_shape=jax.ShapeDtypeStruct((DEC_BATCH, DEC_SEQ, D_MODEL), F32),
        compiler_params=_params(("arbitrary", "arbitrary", "arbitrary")),
        name="scatter_lat",
    )(idx_flat, yg, x1_3)


def _rope_tables():
    pos = np.arange(DEC_SEQ)
    freq = (np.float32(ROPE_THETA) ** (-np.arange(ROPE_FREQS, dtype=np.float32) / np.float32(ROPE_FREQS)))
    ang_r = (pos // GRID_W).astype(np.float32)[:, None] * freq.astype(np.float32)
    ang_c = (pos % GRID_W).astype(np.float32)[:, None] * freq.astype(np.float32)
    cos = np.concatenate([np.cos(ang_r)] * 2 + [np.cos(ang_c)] * 2, axis=-1)
    sin = np.concatenate([-np.sin(ang_r), np.sin(ang_r), -np.sin(ang_c), np.sin(ang_c)], axis=-1)
    reps = LANE // HEAD_DIM
    cs = np.concatenate([np.ones((ROW_TILE, LANE)), np.tile(cos, (1, reps))], axis=0).astype(np.float32)
    sn = np.concatenate([np.zeros((ROW_TILE, LANE)), np.tile(sin, (1, reps))], axis=0).astype(np.float32)
    return cs, sn


def _rope_tile(i):
    lat = jnp.maximum(i - N_CTX_TILES, 0) % (DEC_SEQ // ROW_TILE)
    return jnp.where(i < N_CTX_TILES, 0, 1 + lat)


def _qk_gain(q_norm, k_norm):
    q = jnp.tile(q_norm, N_HEADS) * (HEAD_DIM ** -0.5 * LOG2_E)
    return jnp.concatenate([q, jnp.tile(k_norm, N_KV)])[None, :]


def kernel(x_prompt, x_sample, cache_a_k, cache_a_v, cache_c_k, cache_c_v, c, c_ctx, norm1_g, w_mod, b_mod, w_in,
           a_q_norm, a_k_norm, a_sink, b_v_norm, b_ws, b_bs, c_q_norm, c_k_norm, w_a_o, w_b_o, w_c_o, w_out, norm2_g,
           w_router, b_router, w_gate, w_up, w_down):
    cond8 = jnp.concatenate([c_ctx[None, :], c, jnp.zeros((8 - N_REQ, D_MODEL), F32)], axis=0)
    mods = _modulation(cond8, w_mod, b_mod).reshape(DEPTH, 8, 1, 6 * D_MODEL)

    cs, sn = _rope_tables()
    w_in_b = w_in.astype(BF16)
    wa_b, wb_b, wc_b, wo_b = w_a_o.astype(BF16), w_b_o.astype(BF16), w_c_o.astype(BF16), w_out.astype(BF16)
    ws_b = b_ws.astype(BF16)
    wr_pad = jnp.pad(w_router, ((0, 0), (0, 0), (0, LANE - N_EXPERTS))).astype(BF16)
    br_pad = jnp.pad(b_router, ((0, 0), (0, LANE - N_EXPERTS)), constant_values=NEG_BIG)

    by_seq = lambda a: a.reshape(T_ALL // SEQ, SEQ, a.shape[-1])
    by_dec = lambda a: a.reshape(T_ALL // DEC_SEQ, DEC_SEQ, a.shape[-1])
    lat_off = T_CTX // DEC_SEQ

    caches = [a.reshape(DEC_BATCH, DEPTH, PAST_LEN, KV_W).astype(BF16)
              for a in (cache_a_k, cache_a_v, cache_c_k, cache_c_v)]

    x_ctx = x_prompt.reshape(T_CTX, D_MODEL)
    x_lat = x_sample.reshape(T_LAT, D_MODEL)
    new_kv = [[], [], [], []]
    for l in range(DEPTH):
        qa, ka_b, va_b, nka, nva, bu, bv, qc, kc_b, vc_b, nkc, nvc, gt = _input_projection(
            x_ctx, x_lat, mods, l, norm1_g[l][None, :], w_in_b, cs, sn,
            _qk_gain(a_q_norm[l], a_k_norm[l]), _qk_gain(c_q_norm[l], c_k_norm[l]), b_v_norm[l][None, :])
        for lst, arr in zip(new_kv, (nka, nva, nkc, nvc)):
            lst.append(arr[:T_CTX].reshape(BATCH, SEQ, N_KV, HEAD_DIM))

        sink = a_sink[l]
        oa_ctx, oc_ctx = _context_attention(by_seq(qa), by_seq(ka_b), by_seq(va_b),
                                            by_seq(qc), by_seq(kc_b), by_seq(vc_b), sink)
        cak, cav, cck, ccv = (a[:, l] for a in caches)
        oa_lat = _window_attention(by_dec(qa), by_dec(ka_b), by_dec(va_b), cak, cav, sink,
                                   n_req=DEC_BATCH, off=lat_off)
        oc_lat = _dense_attention(by_dec(qc), by_dec(kc_b), by_dec(vc_b), (cck, ccv), None,
                                  n_req=DEC_BATCH, off=lat_off, tq=1024, key_chunk=1024)

        bs_full = jnp.repeat(b_bs[l].T, B_GROUP_CH, axis=1)
        x1, h2p, afft = _merge(x_ctx, x_lat, oa_ctx.reshape(T_CTX, Q_W), oa_lat.reshape(T_LAT, Q_W), bu, bv,
                               oc_ctx.reshape(T_CTX, Q_W), oc_lat.reshape(T_LAT, Q_W), gt,
                               wa_b, wb_b, wc_b, wo_b, ws_b, bs_full,
                               mods, l, norm2_g[l][None, :], wr_pad, br_pad[l][None, :])

        aff_rows = lambda a, n_req, n: a.reshape(N_EXPERTS, n_req, n).transpose(1, 0, 2).reshape(n_req * N_EXPERTS, n)
        idx_c, val_c = _select(aff_rows(afft[:, :T_CTX], BATCH, SEQ), BATCH * N_EXPERTS, CAP_CTX)
        _, val_l, idx_l_rows = _select(aff_rows(afft[:, T_CTX:], DEC_BATCH, DEC_SEQ), N_EXPERTS, CAP_LAT)
        idx_l_flat = idx_l_rows.reshape(-1)
        xg_ctx = _gather_ctx(idx_c, h2p)
        xg_lat = _gather_lat(idx_l_flat, by_dec(h2p), lat_off)
        yg = _expert_ffn(xg_ctx, xg_lat, val_c.reshape(BATCH, N_EXPERTS, CAP_CTX, 1),
                         val_l.reshape(DEC_BATCH, N_EXPERTS, CAP_LAT, 1), mods, w_gate, w_up, w_down, l)

        x_ctx = _scatter_ctx(idx_c, yg, x1)
        x_lat = _scatter_lat(idx_l_flat, yg, by_dec(x1), lat_off).reshape(T_LAT, D_MODEL)

    y_prompt = x_ctx.reshape(BATCH, SEQ, D_MODEL)
    y_sample = x_lat.reshape(DEC_BATCH, DEC_SEQ, D_MODEL)
    return (y_prompt, y_sample) + tuple(jnp.stack(lst, axis=1) for lst in new_kv)
```

```python
import functools

import jax
import numpy as np
import jax.numpy as jnp
from jax import lax
from jax.experimental import pallas as pl
from jax.experimental.pallas import tpu as pltpu

F32 = jnp.float32
BF16 = jnp.bfloat16
I32 = jnp.int32

D_MODEL = 1024
BATCH = 16
SEQ = 256
DEPTH = 2
DEC_BATCH = 2
DEC_SEQ = 4096
PAST_LEN = 256
GRID_W = 64
HEAD_DIM = 64
N_HEADS = 6
N_KV = 2
N_GRP = N_HEADS // N_KV
B_GROUPS = 4
B_GROUP_CH = 64
B_WIDTH = B_GROUPS * B_GROUP_CH
Q_W = N_HEADS * HEAD_DIM
KV_W = N_KV * HEAD_DIM
QK_W = Q_W + KV_W
N_BRANCH = 3
WINDOW = 128
BLOCK = 128
CHUNK = 128
N_EXPERTS = 16
EXPERT_FF = 1024
CAP_FACTOR = 2
ROPE_THETA = 10000.0
ROPE_FREQS = HEAD_DIM // 4
EPS = 1e-6
IN_WIDTH = 2 * (QK_W + KV_W) + 2 * B_WIDTH + N_BRANCH * D_MODEL

T_CTX = BATCH * SEQ
T_LAT = DEC_BATCH * DEC_SEQ
T_ALL = T_CTX + T_LAT
N_REQ = 1 + DEC_BATCH
CAP_CTX = CAP_FACTOR * SEQ // N_EXPERTS
CAP_LAT = CAP_FACTOR * DEC_SEQ // N_EXPERTS
ROWS_PER_EXPERT = BATCH * CAP_CTX + DEC_BATCH * CAP_LAT

LANE = 128
ROW_TILE = 512
N_CTX_TILES = T_CTX // ROW_TILE
FFN_ROW_TILE = 512
VMEM_LIMIT = 56 * 1024 * 1024
NEG_BIG = -1e30
LOG2_E = 1.4426950408889634

OFF_A = 0
OFF_AV = OFF_A + QK_W
OFF_BU = OFF_AV + KV_W
OFF_BV = OFF_BU + B_WIDTH
OFF_C = OFF_BV + B_WIDTH
OFF_CV = OFF_C + QK_W
OFF_G = OFF_CV + KV_W


def _params(sem, vmem=VMEM_LIMIT):
    return pltpu.CompilerParams(dimension_semantics=sem, vmem_limit_bytes=vmem)


def _sigmoid(x):
    return 1.0 / (1.0 + jnp.exp(-x))


def _gelu_tanh(x):
    return 0.5 * x * (1.0 + jnp.tanh(0.7978845608028654 * (x + 0.044715 * (x * x * x))))


def _split_bf16(x):
    hi = x.astype(BF16)
    lo = (x - hi.astype(F32)).astype(BF16)
    return hi, lo


def _mod_kernel(c_ref, w_ref, b_ref, o_ref):
    c = c_ref[...]
    s_hi, s_lo = _split_bf16(c * _sigmoid(c))
    w_hi, w_lo = _split_bf16(w_ref[0])
    acc = jnp.dot(s_hi, w_hi, preferred_element_type=F32)
    acc += jnp.dot(s_lo, w_hi, preferred_element_type=F32)
    acc += jnp.dot(s_hi, w_lo, preferred_element_type=F32)
    o_ref[0] = acc + b_ref[0]


def _modulation(cond8, w_mod, b_mod):
    n_col = 6 * D_MODEL // D_MODEL
    return pl.pallas_call(
        _mod_kernel,
        grid=(DEPTH, n_col),
        in_specs=[
            pl.BlockSpec((8, D_MODEL), lambda l, j: (0, 0)),
            pl.BlockSpec((1, D_MODEL, D_MODEL), lambda l, j: (l, 0, j)),
            pl.BlockSpec((1, 1, D_MODEL), lambda l, j: (l, 0, j)),
        ],
        out_specs=pl.BlockSpec((1, 8, D_MODEL), lambda l, j: (l, 0, j)),
        out_shape=jax.ShapeDtypeStruct((DEPTH, 8, 6 * D_MODEL), F32),
        compiler_params=_params(("arbitrary", "arbitrary")),
        name="modulation",
    )(cond8, w_mod, b_mod.reshape(DEPTH, 1, 6 * D_MODEL))


def _group_sumsq(y, bd_ref):
    return jnp.dot((y * y).astype(BF16), bd_ref[...], preferred_element_type=F32)


def _pick_pass(i, ctx_ref, lat_ref):
    return jnp.where(i < N_CTX_TILES, ctx_ref[...], lat_ref[...])


def _in_kernel(xc_ref, xl_ref, sc_ref, sh_ref, n1_ref, w_ref, cs_ref, sn_ref, ga_ref, gc_ref, gbv_ref, bd_qk_ref,
               bd_b_ref, qa_ref, ka_ref, va_ref, nka_ref, nva_ref, bu_ref, bv_ref, qc_ref, kc_ref, vc_ref, nkc_ref,
               nvc_ref, gt_ref):
    x = _pick_pass(pl.program_id(0), xc_ref, xl_ref)
    ms = jnp.mean(x * x, axis=-1, keepdims=True)
    h = x * lax.rsqrt(ms + EPS) * n1_ref[...]
    h = h * (1.0 + sc_ref[...]) + sh_ref[...]
    hb = h.astype(BF16).astype(F32)
    tm = x.shape[0]

    def proj(c0, width):
        return jnp.dot(hb, w_ref[:, c0:c0 + width], preferred_element_type=F32)

    cs = jnp.concatenate([cs_ref[...]] * (QK_W // LANE), axis=1)
    sn = jnp.concatenate([sn_ref[...]] * (QK_W // LANE), axis=1)
    lane = lax.broadcasted_iota(I32, (tm, QK_W), 1)
    first_half = (lane & ROPE_FREQS) == 0

    def qk_post(y, gain_ref):
        yn = y * lax.rsqrt(_group_sumsq(y, bd_qk_ref) * (1.0 / HEAD_DIM) + EPS) * gain_ref[...]
        partner = jnp.where(first_half, pltpu.roll(yn, QK_W - ROPE_FREQS, 1), pltpu.roll(yn, ROPE_FREQS, 1))
        return yn * cs + partner * sn

    def mixer(off_qk, off_v, gain_ref, q_ref, k_ref, v_ref, nk_ref, nv_ref):
        y = qk_post(proj(off_qk, QK_W), gain_ref)
        v = proj(off_v, KV_W)
        q_ref[...] = y[:, :Q_W].astype(BF16)
        k_ref[...] = y[:, Q_W:].astype(BF16)
        v_ref[...] = v.astype(BF16)

        nk_ref[...] = y[:, Q_W:]
        nv_ref[...] = v

    mixer(OFF_A, OFF_AV, ga_ref, qa_ref, ka_ref, va_ref, nka_ref, nva_ref)

    bu_ref[...] = _gelu_tanh(proj(OFF_BU, B_WIDTH)).astype(BF16)
    gv = _gelu_tanh(proj(OFF_BV, B_WIDTH))
    gvn = gv * lax.rsqrt(_group_sumsq(gv, bd_b_ref) * (1.0 / B_GROUP_CH) + EPS) * gbv_ref[...]
    bv_ref[...] = gvn.astype(BF16)

    mixer(OFF_C, OFF_CV, gc_ref, qc_ref, kc_ref, vc_ref, nkc_ref, nvc_ref)

    gate_chunk = 512
    for j in range(N_BRANCH * D_MODEL // gate_chunk):
        g = proj(OFF_G + j * gate_chunk, gate_chunk)
        gt_ref[:, j * gate_chunk:(j + 1) * gate_chunk] = _sigmoid(g).astype(BF16)


def _req_of_tile(i):
    return i // N_CTX_TILES


def _ctx_rows(w):
    return pl.BlockSpec((ROW_TILE, w), lambda i: (jnp.minimum(i, N_CTX_TILES - 1), 0))


def _lat_rows(w):
    return pl.BlockSpec((ROW_TILE, w), lambda i: (jnp.maximum(i - N_CTX_TILES, 0), 0))


MOD_SH1, MOD_SC1, MOD_G1, MOD_SH2, MOD_SC2, MOD_G2 = range(6)


def _layer_spec(stacked, layer, buffers=None):
    rest = stacked.shape[1:]
    mode = {} if buffers is None else {"pipeline_mode": pl.Buffered(buffers)}
    return pl.BlockSpec((None,) + rest, lambda *g: (layer,) + (0,) * len(rest), **mode)


def _mod_spec(layer, chunk, req):
    return pl.BlockSpec((None, None, 1, D_MODEL), lambda *g: (layer, req(*g), 0, chunk))


def _block_diag_ones(width, group):
    g = np.arange(width) // group
    return (g[:, None] == g[None, :]).astype(np.float32)


def _input_projection(x_ctx, x_lat, mods, layer, n1, w_in, cs, sn, gain_a, gain_c, gain_bv):
    bd_qk = jnp.asarray(_block_diag_ones(QK_W, HEAD_DIM), BF16)
    bd_b = jnp.asarray(_block_diag_ones(B_WIDTH, B_GROUP_CH), BF16)
    tm = ROW_TILE
    row = lambda w: pl.BlockSpec((tm, w), lambda i: (i, 0))
    full = lambda a: pl.BlockSpec(a.shape, lambda i: (0,) * a.ndim)
    rope = pl.BlockSpec((tm, LANE), lambda i: (_rope_tile(i), 0))
    cache_rows = T_CTX + tm
    spare = lambda w: pl.BlockSpec((tm, w), lambda i: (jnp.minimum(i, N_CTX_TILES), 0))
    mixer_outs = [(Q_W, BF16, T_ALL), (KV_W, BF16, T_ALL), (KV_W, BF16, T_ALL), (KV_W, F32, cache_rows),
                  (KV_W, F32, cache_rows)]
    outs = mixer_outs + [(B_WIDTH, BF16, T_ALL), (B_WIDTH, BF16, T_ALL)] + mixer_outs + [(N_BRANCH * D_MODEL, BF16, T_ALL)]
    return pl.pallas_call(
        _in_kernel,
        grid=(T_ALL // tm,),
        in_specs=[_ctx_rows(D_MODEL), _lat_rows(D_MODEL), _mod_spec(layer, MOD_SC1, _req_of_tile),
                  _mod_spec(layer, MOD_SH1, _req_of_tile), full(n1), _layer_spec(w_in, layer, buffers=1), rope, rope,
                  full(gain_a), full(gain_c), full(gain_bv), full(bd_qk), full(bd_b)],
        out_specs=[row(w) if rows == T_ALL else spare(w) for w, _, rows in outs],
        out_shape=[jax.ShapeDtypeStruct((rows, w), dt) for w, dt, rows in outs],
        compiler_params=_params(("arbitrary",)),
        name="input_projection",
    )(x_ctx, x_lat, mods, mods, n1, w_in, cs, sn, gain_a, gain_c, gain_bv, bd_qk, bd_b)


def _banded_start(carry, banded, qt_scr, heads, lo, hi, tq):
    k_refs, v_refs, biases = banded
    m0, acc0 = carry
    k_loc = jnp.concatenate([r[0, :, lo:hi] for r in k_refs], axis=0)
    vt = jnp.concatenate([r[0].astype(F32).T[lo:hi, :].astype(BF16) for r in v_refs], axis=1)
    vt = jnp.concatenate([vt, jnp.ones((DEN_ROWS, vt.shape[1]), BF16)], axis=0)
    pick = lambda x, j: jnp.concatenate([x[:, g * tq + j * BAND_Q:g * tq + (j + 1) * BAND_Q] for g in range(N_GRP)], axis=1)
    ms, accs = [], []
    for j in range(tq // BAND_Q):
        keys = slice(j * BAND_Q, (j + 1) * BAND_Q + 2 * BLOCK)
        qt_j = jnp.concatenate([qt_scr[h * HEAD_DIM:(h + 1) * HEAD_DIM, j * BAND_Q:(j + 1) * BAND_Q] for h in heads], axis=1)
        s = jnp.dot(k_loc[keys], qt_j, preferred_element_type=F32) + jnp.concatenate([biases[j]] * N_GRP, axis=1)
        m_old = pick(m0, j)
        m_new = jnp.maximum(m_old, jnp.max(s, axis=0, keepdims=True))
        p = jnp.exp2(s - m_new).astype(BF16)
        accs.append(pick(acc0, j) * jnp.exp2(m_old - m_new) + jnp.dot(vt[:, keys], p, preferred_element_type=F32))
        ms.append(m_new)
    gather = lambda parts: jnp.concatenate(
        [parts[j][:, g * BAND_Q:(g + 1) * BAND_Q] for g in range(N_GRP) for j in range(tq // BAND_Q)], axis=1)
    return gather(ms), gather(accs)


def _attention_tile(q_ref, sources, sink_ref, o_ref, qt_scr, ot_scr, *, tq, key_chunk, banded=None):
    width = N_GRP * tq
    for j in range(Q_W // LANE):
        qt_scr[j * LANE:(j + 1) * LANE, :] = q_ref[0, :, j * LANE:(j + 1) * LANE].astype(F32).T.astype(BF16)
    for kv in range(N_KV):
        lo, hi = kv * HEAD_DIM, (kv + 1) * HEAD_DIM
        heads = [kv * N_GRP + g for g in range(N_GRP)]
        qt = jnp.concatenate([qt_scr[h * HEAD_DIM:(h + 1) * HEAD_DIM, :] for h in heads], axis=1)

        def step(carry, kref, vref, c0, size):
            m, acc = carry
            s = jnp.dot(kref[0, pl.ds(c0, size), lo:hi], qt, preferred_element_type=F32)
            vt =vref[0, pl.ds(c0, size), :].astype(F32).T[lo:hi, :].astype(BF16)
            vt = jnp.concatenate([vt, jnp.ones((DEN_ROWS, size), BF16)], axis=0)
            m_new = jnp.maximum(m, jnp.max(s, axis=0, keepdims=True))
            p = jnp.exp2(s - m_new).astype(BF16)
            acc = acc * jnp.exp2(m - m_new) + jnp.dot(vt, p, preferred_element_type=F32)
            return m_new, acc

        if sink_ref is not None:
            m0 = jnp.concatenate([jnp.full((1, tq), sink_ref[h] * LOG2_E, F32) for h in heads], axis=1)
            den0 = jnp.ones((DEN_ROWS, width), F32)
        else:
            m0 = jnp.full((1, width), NEG_BIG, F32)
            den0 = jnp.zeros((DEN_ROWS, width), F32)
        carry = (m0, jnp.concatenate([jnp.zeros((HEAD_DIM, width), F32), den0], axis=0))
        if banded is not None:
            carry = _banded_start(carry, banded, qt_scr, heads, lo, hi, tq)
        for kref, vref in sources:
            n_rows = kref.shape[1]
            n_full = n_rows // key_chunk
            if n_full > 1:
                carry = lax.fori_loop(
                    0, n_full,
                    lambda c, cr: step(cr, kref, vref, pl.multiple_of(c * key_chunk, key_chunk), key_chunk), carry)
            elif n_full == 1:
                carry = step(carry, kref, vref, 0, key_chunk)
            if n_rows - n_full * key_chunk:
                carry = step(carry, kref, vref, n_full * key_chunk, n_rows - n_full * key_chunk)
        _, acc = carry
        o = acc[:HEAD_DIM] / acc[HEAD_DIM:HEAD_DIM + 1]
        for g, h in enumerate(heads):
            ot_scr[h * HEAD_DIM:(h + 1) * HEAD_DIM, :] = o[:, g * tq:(g + 1) * tq]
    for j in range(Q_W // LANE):
        o_ref[0, :, j * LANE:(j + 1) * LANE] = ot_scr[j * LANE:(j + 1) * LANE, :].T.astype(o_ref.dtype)


def _dense_attn_kernel(*refs, tq, key_chunk, has_extra, has_sink):
    refs = list(refs)
    q_ref, k_ref, v_ref = refs[:3]
    del refs[:3]
    sources = [(k_ref, v_ref)]
    if has_extra:
        sources.append((refs.pop(0), refs.pop(0)))
    sink_ref = refs.pop(0) if has_sink else None
    o_ref, qt_scr, ot_scr = refs
    _attention_tile(q_ref, sources, sink_ref, o_ref, qt_scr, ot_scr, tq=tq, key_chunk=key_chunk)


def _attention_scratch(tq):
    return [pltpu.VMEM((Q_W, tq), BF16), pltpu.VMEM((Q_W, tq), F32)]


def _dense_attention(q, k, v, extra, sink, *, n_req, off, tq, key_chunk):
    s = q.shape[1]
    kv_spec = pl.BlockSpec((1, s, KV_W), lambda i, j: (off + i, 0, 0))
    in_specs = [pl.BlockSpec((1, tq, Q_W), lambda i, j: (off + i, j, 0)), kv_spec, kv_spec]
    args = [q, k, v]
    if extra is not None:
        in_specs += [pl.BlockSpec((1, extra[0].shape[1], KV_W), lambda i, j: (i, 0, 0))] * 2
        args += list(extra)
    if sink is not None:
        in_specs.append(pl.BlockSpec(memory_space=pltpu.SMEM))
        args.append(sink)
    return pl.pallas_call(
        functools.partial(_dense_attn_kernel, tq=tq, key_chunk=key_chunk, has_extra=extra is not None,
                          has_sink=sink is not None),
        grid=(n_req, s // tq),
        in_specs=in_specs,
        out_specs=pl.BlockSpec((1, tq, Q_W), lambda i, j: (i, j, 0)),
        out_shape=jax.ShapeDtypeStruct((n_req, s, Q_W), BF16),
        scratch_shapes=_attention_scratch(tq),
        compiler_params=_params(("arbitrary", "arbitrary")),
        name="dense_attention",
    )(*args)


CTX_ATTN_PER_STEP = 2


def _ctx_attn_kernel(qa_ref, ka_ref, va_ref, qc_ref, kc_ref, vc_ref, sink_ref, oa_ref, oc_ref, *scratch):
    for r in range(CTX_ATTN_PER_STEP):
        one = lambda ref: ref.at[pl.ds(r, 1)]
        qta, ota, qtc, otc = scratch[4 * r:4 * r + 4]
        _attention_tile(one(qa_ref), [(one(ka_ref), one(va_ref))], sink_ref, one(oa_ref), qta, ota,
                        tq=SEQ, key_chunk=SEQ)
        _attention_tile(one(qc_ref), [(one(kc_ref), one(vc_ref))], None, one(oc_ref), qtc, otc,
                        tq=SEQ, key_chunk=SEQ)


def _context_attention(qa, ka, va, qc, kc, vc, sink):
    n = CTX_ATTN_PER_STEP
    q_spec = pl.BlockSpec((n, SEQ, Q_W), lambda i: (i, 0, 0))
    kv_spec = pl.BlockSpec((n, SEQ, KV_W), lambda i: (i, 0, 0))
    return pl.pallas_call(
        _ctx_attn_kernel,
        grid=(BATCH // n,),
        in_specs=[q_spec, kv_spec, kv_spec, q_spec, kv_spec, kv_spec, pl.BlockSpec(memory_space=pltpu.SMEM)],
        out_specs=[q_spec, q_spec],
        out_shape=[jax.ShapeDtypeStruct((BATCH, SEQ, Q_W), BF16)] * 2,
        scratch_shapes=_attention_scratch(SEQ) * (2 * n),
        compiler_params=_params(("arbitrary",)),
        name="context_attention",
    )(qa, ka, va, qc, kc, vc, sink)


DEN_ROWS = 16
WINDOW_TQ = 512
BAND_Q = 2 * BLOCK


def _window_attn_kernel(q_ref, kp_ref, kc_ref, kn_ref, vp_ref, vc_ref, vn_ref, ck_ref, cv_ref, band_ref, sink_ref, o_ref,
                        qt_scr, ot_scr, *, seq):
    q_pos0 = pl.program_id(1) * WINDOW_TQ
    n_grp = WINDOW_TQ // BAND_Q
    band = band_ref[...]
    hide_prev = jnp.where(q_pos0 >= BLOCK, 0.0, NEG_BIG)
    hide_next = jnp.where(q_pos0 + WINDOW_TQ < seq, 0.0, NEG_BIG)
    first = jnp.concatenate([band[:BLOCK] + hide_prev, band[BLOCK:]], axis=0)
    last = jnp.concatenate([band[:BAND_Q + BLOCK], band[BAND_Q + BLOCK:] + hide_next], axis=0)
    biases = [first] + [band] * (n_grp - 2) + [last]
    banded = ((kp_ref, kc_ref, kn_ref), (vp_ref, vc_ref, vn_ref), biases)
    _attention_tile(q_ref, [(ck_ref, cv_ref)], sink_ref, o_ref, qt_scr, ot_scr, tq=WINDOW_TQ,
                    key_chunk=PAST_LEN, banded=banded)


def _band_bias():
    d = (np.arange(BAND_Q + 2 * BLOCK) - BLOCK)[:, None] - np.arange(BAND_Q)[None, :]
    return np.where(np.abs(d) <= WINDOW, 0.0, NEG_BIG).astype(np.float32)


def _window_attention(q, k, v, ck, cv, sink, *, n_req, off):
    b, s = n_req, q.shape[1]
    nb = s // BLOCK
    per_tile = WINDOW_TQ // BLOCK
    edge = lambda f: pl.BlockSpec((1, BLOCK, KV_W), lambda i, j: (off + i, f(j), 0))
    prev = lambda j: jnp.maximum(j * per_tile - 1, 0)
    nxt = lambda j: jnp.minimum((j + 1) * per_tile, nb - 1)
    cur = pl.BlockSpec((1, WINDOW_TQ, KV_W), lambda i, j: (off + i, j, 0))
    ctx = pl.BlockSpec((1, PAST_LEN, KV_W), lambda i, j: (i, 0, 0))
    assert WINDOW_TQ // BAND_Q >= 2
    band = _band_bias()
    return pl.pallas_call(
        functools.partial(_window_attn_kernel, seq=s),
        grid=(b, s // WINDOW_TQ),
        in_specs=[pl.BlockSpec((1, WINDOW_TQ, Q_W), lambda i, j: (off + i, j, 0)),
                  edge(prev), cur, edge(nxt), edge(prev), cur, edge(nxt), ctx, ctx,
                  pl.BlockSpec(band.shape, lambda i, j: (0, 0)),
                  pl.BlockSpec(memory_space=pltpu.SMEM)],
        out_specs=pl.BlockSpec((1, WINDOW_TQ, Q_W), lambda i, j: (i, j, 0)),
        out_shape=jax.ShapeDtypeStruct((b, s, Q_W), BF16),
        scratch_shapes=_attention_scratch(WINDOW_TQ),
        compiler_params=_params(("arbitrary", "arbitrary")),
        name="window_attention",
    )(q, k, k, k, v, v, v, ck, cv, band, sink)


def _pack_halves(x):
    half = x.shape[1] // 2
    return pltpu.pack_elementwise([x[:, :half], x[:, half:]], packed_dtype=BF16)


def _unpack_halves(words):
    return tuple(pltpu.unpack_elementwise(words, index=i, packed_dtype=BF16, unpacked_dtype=F32).astype(BF16)
                 for i in range(2))


def _merge_kernel(xc_ref, xl_ref, oac_ref, oal_ref, bu_ref, bv_ref, occ_ref, ocl_ref, gt_ref, wa_ref, wb_ref, wc_ref,
                  wo_ref, ws_ref, bs_ref, g1_ref, sc2_ref, sh2_ref, n2_ref, wr_ref, br_ref, x1_ref, h2p_ref, afft_ref):
    i = pl.program_id(0)
    tm = xc_ref.shape[0]
    group = lax.broadcasted_iota(I32, (CHUNK, B_WIDTH), 1) // B_GROUP_CH
    obs = []
    for c in range(tm // CHUNK):
        v = bv_ref[c * CHUNK:(c + 1) * CHUNK, :]
        sv = jnp.zeros((CHUNK, B_WIDTH), F32)
        for g in range(B_GROUPS):
            sv = jnp.where(group == g, jnp.dot(ws_ref[g], v, preferred_element_type=F32), sv)
        u = bu_ref[c * CHUNK:(c + 1) * CHUNK, :].astype(F32)
        obs.append((u * (sv + bs_ref[...])).astype(BF16))
    ob = jnp.concatenate(obs, axis=0)

    oa = _pick_pass(i, oac_ref, oal_ref)
    oc = _pick_pass(i, occ_ref, ocl_ref)
    merged = gt_ref[:, 0:D_MODEL].astype(F32) * jnp.dot(oa, wa_ref[...], preferred_element_type=F32)
    merged += gt_ref[:, D_MODEL:2 * D_MODEL].astype(F32) * jnp.dot(ob, wb_ref[...], preferred_element_type=F32)
    merged += gt_ref[:, 2 * D_MODEL:3 * D_MODEL].astype(F32) * jnp.dot(oc, wc_ref[...], preferred_element_type=F32)
    y = jnp.dot(merged.astype(BF16), wo_ref[...], preferred_element_type=F32)
    x1 = _pick_pass(i, xc_ref, xl_ref) + g1_ref[...] * y
    x1_ref[...] = x1

    ms = jnp.mean(x1 * x1, axis=-1, keepdims=True)
    h2 = x1 * lax.rsqrt(ms + EPS) * n2_ref[...]
    h2 = h2 * (1.0 + sc2_ref[...]) + sh2_ref[...]
    h2p_ref[...] = _pack_halves(h2)

    logits = jnp.dot(h2.astype(BF16), wr_ref[...], preferred_element_type=F32) + br_ref[...]
    e = jnp.exp(logits - jnp.max(logits, axis=-1, keepdims=True))
    aff = e / jnp.sum(e, axis=-1, keepdims=True)
    afft_ref[...] = aff.T[:N_EXPERTS, :]


def _merge(x_ctx, x_lat, oa_ctx, oa_lat, bu, bv, oc_ctx, oc_lat, gt, wa, wb, wc, wo, ws, bs, mods, layer, n2, wr, br):
    tm = ROW_TILE
    row = lambda w: pl.BlockSpec((tm, w), lambda i: (i, 0))
    full = lambda a: pl.BlockSpec(a.shape, lambda i: (0,) * a.ndim)
    mod = lambda chunk: _mod_spec(layer, chunk, _req_of_tile)
    stack = lambda a: _layer_spec(a, layer)
    return pl.pallas_call(
        _merge_kernel,
        grid=(T_ALL // tm,),
        in_specs=[_ctx_rows(D_MODEL), _lat_rows(D_MODEL), _ctx_rows(Q_W), _lat_rows(Q_W), row(B_WIDTH), row(B_WIDTH),
                  _ctx_rows(Q_W), _lat_rows(Q_W), row(N_BRANCH * D_MODEL),
                  stack(wa), stack(wb), stack(wc), stack(wo), stack(ws), full(bs),
                  mod(MOD_G1), mod(MOD_SC2), mod(MOD_SH2), full(n2), stack(wr), full(br)],
        out_specs=[row(D_MODEL), row(D_MODEL // 2), pl.BlockSpec((N_EXPERTS, tm), lambda i: (0, i))],
        out_shape=[jax.ShapeDtypeStruct((T_ALL, D_MODEL), F32), jax.ShapeDtypeStruct((T_ALL, D_MODEL // 2), jnp.uint32),
                   jax.ShapeDtypeStruct((N_EXPERTS, T_ALL), F32)],
        compiler_params=_params(("arbitrary",)),
        name="merge_router",
    )(x_ctx, x_lat, oa_ctx, oa_lat, bu, bv, oc_ctx, oc_lat, gt, wa, wb, wc, wo, ws, bs, mods, mods, mods, n2, wr, br)


def _select_kernel(aff_ref, idx_ref, val_ref, *rest, n, cap, row_chunk):
    idx_row_ref = rest[0] if len(rest) == 4 else None
    possel_ref, idx_scr, val_scr = rest[-3:]
    a = aff_ref[...]
    rows = a.shape[0]
    tok = lax.broadcasted_iota(I32, (rows, n), 1)

    def count(ones):
        return jnp.sum(ones, axis=1, keepdims=True)

    def at_least(word):
        return jnp.where(a >= pltpu.bitcast(word, F32), 1, 0)

    def greedy_bits(start, top_bit, keep):
        word = start
        bits = list(range(top_bit, -1, -1))
        if rows > SEARCH_PAIR_MAX_ROWS:
            for bit in bits:
                cand = word | (1 << bit)
                word = jnp.where(keep(cand), cand, word)
            return word
        if len(bits) % 2:
            cand = word | (1 << bits[0])
            word = jnp.where(keep(cand), cand, word)
            bits = bits[1:]
        for hi, lo in zip(bits[0::2], bits[1::2]):
            c_lo, c_hi, c_both = word | (1 << lo), word | (1 << hi), word | (1 << hi) | (1 << lo)
            word = jnp.where(keep(c_both), c_both, jnp.where(keep(c_hi), c_hi, jnp.where(keep(c_lo), c_lo, word)))
        return word

    thr = greedy_bits(jnp.zeros((rows, 1), I32), 30, lambda w: count(at_least(w)) >= cap)
    above = at_least(thr + 1)
    tied = at_least(thr) - above
    need = cap - count(above)
    last = greedy_bits(jnp.zeros((rows, 1), I32), n.bit_length() - 2,
                       lambda w: count(jnp.where(tok < w, tied, 0)) < need)
    sel = above + jnp.where(tok <= last, tied, 0)

    blk = min(n, 256)
    tri = jnp.where(lax.broadcasted_iota(I32, (blk, blk), 0) <= lax.broadcasted_iota(I32, (blk, blk), 1),
                    1.0, 0.0).astype(BF16)
    sel_f = sel.astype(F32)
    offset = jnp.zeros((rows, 1), F32)
    for j in range(n // blk):
        s_blk = sel_f[:, j * blk:(j + 1) * blk]
        incl = jnp.dot(s_blk.astype(BF16), tri, preferred_element_type=F32)
        pos = (incl - s_blk + offset).astype(I32)
        possel_ref[:, j * blk:(j + 1) * blk] = jnp.where(sel[:, j * blk:(j + 1) * blk] > 0, pos, -1)
        offset = offset + incl[:, blk - 1:blk]

    tb = min(n, TOKEN_BLOCK)
    n_blk = n // tb

    def fold_lanes(x):
        acc = x[:, :LANE]
        for k in range(1, tb // LANE):
            acc = acc + x[:, k * LANE:(k + 1) * LANE]
        return acc

    def match(e, slot, t0):
        hit = possel_ref[pl.ds(e, 1), pl.ds(t0, tb)] == slot
        tok = t0 + lax.broadcasted_iota(I32, (1, tb), 1)
        return (fold_lanes(jnp.where(hit, tok, 0)),
                fold_lanes(jnp.where(hit, aff_ref[pl.ds(e, 1), pl.ds(t0, tb)], 0.0)))

    def per_row(e, _):
        ends, run = [], 0
        for j in range(n_blk - 1):
            run = run + jnp.sum(jnp.where(possel_ref[pl.ds(e, 1), j * tb:(j + 1) * tb] >= 0, 1, 0))
            ends.append(run)

        def per_chunk(c, _):
            r0 = pl.multiple_of(c * row_chunk, row_chunk)
            slot = lax.broadcasted_iota(I32, (row_chunk, 1), 0) + r0
            if n_blk == 1:
                idx, val = match(e, slot, 0)
            else:
                first = sum(jnp.where(end <= r0, 1, 0) for end in ends)
                last = 1 + sum(jnp.where(end < r0 + row_chunk, 1, 0) for end in ends)

                def per_block(j, acc):
                    i, v = match(e, slot, pl.multiple_of(j * tb, tb))
                    return acc[0] + i, acc[1] + v

                idx, val = lax.fori_loop(first, last, per_block,
                                         (jnp.zeros((row_chunk, LANE), I32), jnp.zeros((row_chunk, LANE), F32)))
            idx_scr[pl.ds(r0, row_chunk), :] = idx
            val_scr[pl.ds(r0, row_chunk), :] = val
            return 0

        lax.fori_loop(0, cap // row_chunk, per_chunk, 0)
        idx = jnp.sum(idx_scr[...], axis=1, keepdims=True)
        idx_ref[e] = idx
        val_ref[e] = jnp.sum(val_scr[...], axis=1, keepdims=True)
        if idx_row_ref is not None:
            idx_row_ref[pl.ds(e, 1), :] = jnp.broadcast_to(idx.astype(F32), (cap, LANE)).T[0:1, :].astype(I32)
        return 0

    def per_small_row(e, _):
        idx, val = match(e, lax.broadcasted_iota(I32, (cap, 1), 0), 0)
        idx_ref[e] = jnp.sum(idx, axis=1, keepdims=True)
        val_ref[e] = jnp.sum(val, axis=1, keepdims=True)
        return 0

    if n_blk == 1 and cap == row_chunk:
        lax.fori_loop(0, rows, per_small_row, 0, unroll=4)
    else:
        lax.fori_loop(0, rows, per_row, 0)


def _select(aff_rows, rows_per_step, cap):
    r, n = aff_rows.shape
    row_chunk = min(cap, 64)
    out_specs = [pl.BlockSpec((rows_per_step, cap, 1), lambda s: (s, 0, 0))] * 2
    out_shape = [jax.ShapeDtypeStruct((r, cap, 1), I32), jax.ShapeDtypeStruct((r, cap, 1), F32)]
    if cap % LANE == 0:
        out_specs.append(pl.BlockSpec((rows_per_step, cap), lambda s: (s, 0)))
        out_shape.append(jax.ShapeDtypeStruct((r, cap), I32))
    return pl.pallas_call(
        functools.partial(_select_kernel, n=n, cap=cap, row_chunk=row_chunk),
        grid=(r // rows_per_step,),
        in_specs=[pl.BlockSpec((rows_per_step, n), lambda s: (s, 0))],
        out_specs=out_specs,
        out_shape=out_shape,
        scratch_shapes=[pltpu.VMEM((rows_per_step, n), I32), pltpu.VMEM((cap, LANE), I32), pltpu.VMEM((cap, LANE), F32)],
        compiler_params=_params(("arbitrary",)),
        name="expert_select",
    )(aff_rows)


CTX_SLOTS = N_EXPERTS * CAP_CTX
TOKEN_BLOCK = 512
SEARCH_PAIR_MAX_ROWS = 32
SLOT_GROUP = 16


CTX_PER_STEP = 4


def _ctx_slot_onehot(idx, slots_on_rows):
    idx = idx.reshape(CTX_SLOTS, 1)
    if slots_on_rows:
        hit = idx == lax.broadcasted_iota(I32, (CTX_SLOTS, SEQ), 1)
    else:
        idx_lane = jnp.broadcast_to(idx.astype(F32), (CTX_SLOTS, LANE)).T[0:1, :]
        hit = idx_lane == lax.broadcasted_iota(I32, (SEQ, CTX_SLOTS), 0).astype(F32)
    return jnp.where(hit, 1.0, 0.0).astype(BF16)


def _gather_ctx_kernel(idx_ref, h_ref, out_ref):
    for r in range(CTX_PER_STEP):
        onehot = _ctx_slot_onehot(idx_ref[r * N_EXPERTS:(r + 1) * N_EXPERTS], True)
        lo, hi = _unpack_halves(h_ref[r * SEQ:(r + 1) * SEQ, :])
        g_lo = jnp.dot(onehot, lo, preferred_element_type=F32)
        g_hi = jnp.dot(onehot, hi, preferred_element_type=F32)
        packed = pltpu.pack_elementwise([g_lo, g_hi], packed_dtype=BF16)
        out_ref[:, r * CAP_CTX:(r + 1) * CAP_CTX, :] = packed.reshape(N_EXPERTS, CAP_CTX, D_MODEL // 2)


def _gather_ctx(idx_c, h2p):
    n = CTX_PER_STEP
    return pl.pallas_call(
        _gather_ctx_kernel,
        grid=(BATCH // n,),
        in_specs=[pl.BlockSpec((n * N_EXPERTS, CAP_CTX, 1), lambda b: (b, 0, 0)),
                  pl.BlockSpec((n * SEQ, D_MODEL // 2), lambda b: (b, 0))],
        out_specs=pl.BlockSpec((N_EXPERTS, n * CAP_CTX, D_MODEL // 2), lambda b: (0, b, 0)),
        out_shape=jax.ShapeDtypeStruct((N_EXPERTS, BATCH * CAP_CTX, D_MODEL // 2), jnp.uint32),
        compiler_params=_params(("arbitrary",)),
        name="gather_ctx",
    )(idx_c, h2p)


def _gather_lat_kernel(idx_ref, src_ref, out_ref):
    base = (pl.program_id(0) * N_EXPERTS + pl.program_id(1)) * CAP_LAT

    def body(it, _):
        r0 = pl.multiple_of(it * SLOT_GROUP, SLOT_GROUP)
        picked = [src_ref[0, pl.ds(idx_ref[base + r0 + k], 1), :] for k in range(SLOT_GROUP)]
        dst = out_ref.at[0, pl.ds(r0, SLOT_GROUP)]
        for k in range(SLOT_GROUP):
            dst[k:k + 1, :] = picked[k]
        return 0

    lax.fori_loop(0, CAP_LAT // SLOT_GROUP, body, 0)


def _gather_lat(idx_flat, h2p3, off):
    return pl.pallas_call(
        _gather_lat_kernel,
        grid_spec=pltpu.PrefetchScalarGridSpec(
            num_scalar_prefetch=1,
            grid=(DEC_BATCH, N_EXPERTS),
            in_specs=[pl.BlockSpec((1, DEC_SEQ, D_MODEL // 2), lambda b, e, idx: (off + b, 0, 0))],
            out_specs=pl.BlockSpec((1, CAP_LAT, D_MODEL // 2), lambda b, e, idx: (e, b, 0)),
        ),
        out_shape=jax.ShapeDtypeStruct((N_EXPERTS, DEC_BATCH * CAP_LAT, D_MODEL // 2), jnp.uint32),
        compiler_params=_params(("arbitrary", "arbitrary")),
        name="gather_lat",
    )(idx_flat, h2p3)


N_CTX_FFN_TILES = BATCH * CAP_CTX // FFN_ROW_TILE


def _ffn_kernel(xc_ref, xl_ref, vc_ref, vl_ref, g2_ref, wg_hbm, wu_hbm, wd_hbm, o_ref, wbuf, sems, *, layer):
    e, j = pl.program_id(0), pl.program_id(1)
    slot = e % 2
    weights = (wg_hbm, wu_hbm, wd_hbm)

    def fetch(m, expert, into):
        return pltpu.make_async_copy(weights[m].at[layer, expert], wbuf.at[into, m], sems.at[into, m])

    @pl.when(jnp.logical_and(e == 0, j == 0))
    def _():
        for m in range(3):
            fetch(m, 0, 0).start()
        for m in range(3):
            fetch(m, 0, 0).wait()

    for m in range(3):
        @pl.when(jnp.logical_and(j == m, e + 1 < N_EXPERTS))
        def _(m=m):
            fetch(m, e + 1, 1 - slot).start()

    @pl.when(jnp.logical_and(j == 0, e > 0))
    def _():
        for m in range(3):
            fetch(m, e, slot).wait()

    is_ctx = j < N_CTX_FFN_TILES
    x = jnp.where(is_ctx, jnp.concatenate(_unpack_halves(xc_ref[0]), axis=1),
                  jnp.concatenate(_unpack_halves(xl_ref[0]), axis=1)).astype(F32)
    g = jnp.dot(x, wbuf[slot, 0], preferred_element_type=F32)
    u = jnp.dot(x, wbuf[slot, 1], preferred_element_type=F32)
    hh = (g * _sigmoid(g)) * u
    y = jnp.dot(hh.astype(BF16).astype(F32), wbuf[slot, 2], preferred_element_type=F32)
    o_ref[0] = (y * jnp.where(is_ctx, vc_ref[...].reshape(FFN_ROW_TILE, 1), vl_ref[...])) * g2_ref[...]


def _expert_ffn(xg_ctx, xg_lat, val_ctx, val_lat, mods, w_gate, w_up, w_down, layer):
    tr = FFN_ROW_TILE
    assert tr == BATCH * CAP_CTX == CAP_LAT
    n_tiles = ROWS_PER_EXPERT // tr
    assert n_tiles >= 3 and D_MODEL == EXPERT_FF
    in_hbm = pl.BlockSpec(memory_space=pl.ANY)

    ctx_tile = lambda j: jnp.minimum(j, N_CTX_FFN_TILES - 1)
    lat_tile = lambda j: jnp.maximum(j - N_CTX_FFN_TILES, 0)
    return pl.pallas_call(
        functools.partial(_ffn_kernel, layer=layer),
        grid=(N_EXPERTS, n_tiles),
        in_specs=[pl.BlockSpec((1, tr, D_MODEL // 2), lambda e, j: (e, ctx_tile(j), 0)),
                  pl.BlockSpec((1, tr, D_MODEL // 2), lambda e, j: (e, lat_tile(j), 0)),
                  pl.BlockSpec((BATCH, None, CAP_CTX, 1), lambda e, j: (0, e, 0, 0)),
                  pl.BlockSpec((None, None, CAP_LAT, 1), lambda e, j: (lat_tile(j), e, 0, 0)),
                  _mod_spec(layer, MOD_G2, lambda e, j: j),
                  in_hbm, in_hbm, in_hbm],
        out_specs=pl.BlockSpec((1, tr, D_MODEL), lambda e, j: (e, j, 0)),
        out_shape=jax.ShapeDtypeStruct((N_EXPERTS, ROWS_PER_EXPERT, D_MODEL), F32),
        scratch_shapes=[pltpu.VMEM((2, 3, D_MODEL, EXPERT_FF), F32), pltpu.SemaphoreType.DMA((2, 3))],
        compiler_params=_params(("arbitrary", "arbitrary")),
        name="expert_ffn",
    )(xg_ctx, xg_lat, val_ctx, val_lat, mods, w_gate, w_up, w_down)


def _scatter_ctx_kernel(idx_ref, y_ref, x1_ref, out_ref):
    for r in range(CTX_PER_STEP):
        onehot = _ctx_slot_onehot(idx_ref[r * N_EXPERTS:(r + 1) * N_EXPERTS], False)
        y_hi, y_lo = _split_bf16(y_ref[:, r * CAP_CTX:(r + 1) * CAP_CTX, :].reshape(CTX_SLOTS, D_MODEL))
        moe = jnp.dot(onehot, y_hi, preferred_element_type=F32) + jnp.dot(onehot, y_lo, preferred_element_type=F32)
        out_ref[r * SEQ:(r + 1) * SEQ, :] = x1_ref[r * SEQ:(r + 1) * SEQ, :] + moe


def _scatter_ctx(idx_c, yg, x1):
    n = CTX_PER_STEP
    return pl.pallas_call(
        _scatter_ctx_kernel,
        grid=(BATCH // n,),
        in_specs=[pl.BlockSpec((n * N_EXPERTS, CAP_CTX, 1), lambda b: (b, 0, 0)),
                  pl.BlockSpec((N_EXPERTS, n * CAP_CTX, D_MODEL), lambda b: (0, b, 0)),
                  pl.BlockSpec((n * SEQ, D_MODEL), lambda b: (b, 0))],
        out_specs=pl.BlockSpec((n * SEQ, D_MODEL), lambda b: (b, 0)),
        out_shape=jax.ShapeDtypeStruct((T_CTX, D_MODEL), F32),
        compiler_params=_params(("arbitrary",)),
        name="scatter_ctx",
    )(idx_c, yg, x1)


def _scatter_lat_kernel(idx_ref, y_ref, x1_hbm, out_ref, sem, *, off):
    b, e = pl.program_id(0), pl.program_id(2)

    @pl.when(e == 0)
    def _():
        load = pltpu.make_async_copy(x1_hbm.at[pl.ds(off + b, 1)], out_ref, sem)
        load.start()
        load.wait()

    base = (b * N_EXPERTS + e) * CAP_LAT

    def body(it, _):
        r0 = pl.multiple_of(it * SLOT_GROUP, SLOT_GROUP)
        rows = [idx_ref[base + r0 + k] for k in range(SLOT_GROUP)]
        old = [out_ref[0, pl.ds(rows[k], 1), :] for k in range(SLOT_GROUP)]
        y = y_ref[0, pl.ds(r0, SLOT_GROUP), :]
        for k in range(SLOT_GROUP):
            out_ref[0, pl.ds(rows[k], 1), :] = old[k] + y[k:k + 1, :]
        return 0

    lax.fori_loop(0, CAP_LAT // SLOT_GROUP, body, 0)


def _scatter_lat(idx_flat, yg, x1_3, off):
    blk0 = BATCH * CAP_CTX // CAP_LAT
    return pl.pallas_call(
        functools.partial(_scatter_lat_kernel, off=off),
        grid_spec=pltpu.PrefetchScalarGridSpec(
            num_scalar_prefetch=1,
            grid=(DEC_BATCH, 1, N_EXPERTS),
            in_specs=[pl.BlockSpec((1, CAP_LAT, D_MODEL), lambda b, h, e, idx: (e, blk0 + b, 0)),
                      pl.BlockSpec(memory_space=pl.ANY)],
            out_specs=pl.BlockSpec((1, DEC_SEQ, D_MODEL), lambda b, h, e, idx: (b, 0, 0)),
            scratch_shapes=[pltpu.SemaphoreType.DMA(())],
        ),
        out_shape=jax.ShapeDtypeStruct((DEC_BATCH, DEC_SEQ, D_MODEL), F32),
        compiler_params=_params(("arbitrary", "arbitrary", "arbitrary")),
        name="scatter_lat",
    )(idx_flat, yg, x1_3)


def _rope_tables():
    pos = np.arange(DEC_SEQ)
    freq = (np.float32(ROPE_THETA) ** (-np.arange(ROPE_FREQS, dtype=np.float32) / np.float32(ROPE_FREQS)))
    ang_r = (pos // GRID_W).astype(np.float32)[:, None] * freq.astype(np.float32)
    ang_c = (pos % GRID_W).astype(np.float32)[:, None] * freq.astype(np.float32)
    cos = np.concatenate([np.cos(ang_r)] * 2 + [np.cos(ang_c)] * 2, axis=-1)
    sin = np.concatenate([-np.sin(ang_r), np.sin(ang_r), -np.sin(ang_c), np.sin(ang_c)], axis=-1)
    reps = LANE // HEAD_DIM
    cs = np.concatenate([np.ones((ROW_TILE, LANE)), np.tile(cos, (1, reps))], axis=0).astype(np.float32)
    sn = np.concatenate([np.zeros((ROW_TILE, LANE)), np.tile(sin, (1, reps))], axis=0).astype(np.float32)
    return cs, sn


def _rope_tile(i):
    lat = jnp.maximum(i - N_CTX_TILES, 0) % (DEC_SEQ // ROW_TILE)
    return jnp.where(i < N_CTX_TILES, 0, 1 + lat)


def _qk_gain(q_norm, k_norm):
    q = jnp.tile(q_norm, N_HEADS) * (HEAD_DIM ** -0.5 * LOG2_E)
    return jnp.concatenate([q, jnp.tile(k_norm, N_KV)])[None, :]


def kernel(x_prompt, x_sample, cache_a_k, cache_a_v, cache_c_k, cache_c_v, c, c_ctx, norm1_g, w_mod, b_mod, w_in,
           a_q_norm, a_k_norm, a_sink, b_v_norm, b_ws, b_bs, c_q_norm, c_k_norm, w_a_o, w_b_o, w_c_o, w_out, norm2_g,
           w_router, b_router, w_gate, w_up, w_down):
    cond8 = jnp.concatenate([c_ctx[None, :], c, jnp.zeros((8 - N_REQ, D_MODEL), F32)], axis=0)
    mods = _modulation(cond8, w_mod, b_mod).reshape(DEPTH, 8, 1, 6 * D_MODEL)

    cs, sn = _rope_tables()
    wa_b, wb_b, wc_b, wo_b = w_a_o.astype(BF16), w_b_o.astype(BF16), w_c_o.astype(BF16), w_out.astype(BF16)
    ws_b = b_ws.astype(BF16)
    wr_pad = jnp.pad(w_router, ((0, 0), (0, 0), (0, LANE - N_EXPERTS))).astype(BF16)
    br_pad = jnp.pad(b_router, ((0, 0), (0, LANE - N_EXPERTS)), constant_values=NEG_BIG)

    by_seq = lambda a: a.reshape(T_ALL // SEQ, SEQ, a.shape[-1])
    by_dec = lambda a: a.reshape(T_ALL // DEC_SEQ, DEC_SEQ, a.shape[-1])
    lat_off = T_CTX // DEC_SEQ

    caches = [a.reshape(DEC_BATCH, DEPTH, PAST_LEN, KV_W).astype(BF16)
              for a in (cache_a_k, cache_a_v, cache_c_k, cache_c_v)]

    x_ctx = x_prompt.reshape(T_CTX, D_MODEL)
    x_lat = x_sample.reshape(T_LAT, D_MODEL)
    new_kv = [[], [], [], []]
    for l in range(DEPTH):
        qa, ka_b, va_b, nka, nva, bu, bv, qc, kc_b, vc_b, nkc, nvc, gt = _input_projection(
            x_ctx, x_lat, mods, l, norm1_g[l][None, :], w_in, cs, sn,
            _qk_gain(a_q_norm[l], a_k_norm[l]), _qk_gain(c_q_norm[l], c_k_norm[l]), b_v_norm[l][None, :])
        for lst, arr in zip(new_kv, (nka, nva, nkc, nvc)):
            lst.append(arr[:T_CTX].reshape(BATCH, SEQ, N_KV, HEAD_DIM))

        sink = a_sink[l]
        oa_ctx, oc_ctx = _context_attention(by_seq(qa), by_seq(ka_b), by_seq(va_b),
                                            by_seq(qc), by_seq(kc_b), by_seq(vc_b), sink)
        cak, cav, cck, ccv = (a[:, l] for a in caches)
        oa_lat = _window_attention(by_dec(qa), by_dec(ka_b), by_dec(va_b), cak, cav, sink,
                                   n_req=DEC_BATCH, off=lat_off)
        oc_lat = _dense_attention(by_dec(qc), by_dec(kc_b), by_dec(vc_b), (cck, ccv), None,
                                  n_req=DEC_BATCH, off=lat_off, tq=1024, key_chunk=1024)

        bs_full = jnp.repeat(b_bs[l].T, B_GROUP_CH, axis=1)
        x1, h2p, afft = _merge(x_ctx, x_lat, oa_ctx.reshape(T_CTX, Q_W), oa_lat.reshape(T_LAT, Q_W), bu, bv,
                               oc_ctx.reshape(T_CTX, Q_W), oc_lat.reshape(T_LAT, Q_W), gt,
                               wa_b, wb_b, wc_b, wo_b, ws_b, bs_full,
                               mods, l, norm2_g[l][None, :], wr_pad, br_pad[l][None, :])

        aff_rows = lambda a, n_req, n: a.reshape(N_EXPERTS, n_req, n).transpose(1, 0, 2).reshape(n_req * N_EXPERTS, n)
        idx_c, val_c = _select(aff_rows(afft[:, :T_CTX], BATCH, SEQ), BATCH * N_EXPERTS, CAP_CTX)
        _, val_l, idx_l_rows = _select(aff_rows(afft[:, T_CTX:], DEC_BATCH, DEC_SEQ), N_EXPERTS, CAP_LAT)
        idx_l_flat = idx_l_rows.reshape(-1)
        xg_ctx = _gather_ctx(idx_c, h2p)
        xg_lat = _gather_lat(idx_l_flat, by_dec(h2p), lat_off)
        yg = _expert_ffn(xg_ctx, xg_lat, val_c.reshape(BATCH, N_EXPERTS, CAP_CTX, 1),
                         val_l.reshape(DEC_BATCH, N_EXPERTS, CAP_LAT, 1), mods, w_gate, w_up, w_down, l)

        x_ctx = _scatter_ctx(idx_c, yg, x1)
        x_lat = _scatter_lat(idx_l_flat, yg, by_dec(x1), lat_off).reshape(T_LAT, D_MODEL)

    y_prompt = x_ctx.reshape(BATCH, SEQ, D_MODEL)
    y_sample = x_lat.reshape(DEC_BATCH, DEC_SEQ, D_MODEL)
    return (y_prompt, y_sample) + tuple(jnp.stack(lst, axis=1) for lst in new_kv)
```

```python
import functools

import jax
import numpy as np
import jax.numpy as jnp
from jax import lax
from jax.experimental import pallas as pl
from jax.experimental.pallas import tpu as pltpu

F32 = jnp.float32
BF16 = jnp.bfloat16
I32 = jnp.int32

D_MODEL = 1024
BATCH = 16
SEQ = 256
DEPTH = 2
DEC_BATCH = 2
DEC_SEQ = 4096
PAST_LEN = 256
GRID_W = 64
HEAD_DIM = 64
N_HEADS = 6
N_KV = 2
N_GRP = N_HEADS // N_KV
B_GROUPS = 4
B_GROUP_CH = 64
B_WIDTH = B_GROUPS * B_GROUP_CH
Q_W = N_HEADS * HEAD_DIM
KV_W = N_KV * HEAD_DIM
QK_W = Q_W + KV_W
N_BRANCH = 3
WINDOW = 128
BLOCK = 128
CHUNK = 128
N_EXPERTS = 16
EXPERT_FF = 1024
CAP_FACTOR = 2
ROPE_THETA = 10000.0
ROPE_FREQS = HEAD_DIM // 4
EPS = 1e-6
IN_WIDTH = 2 * (QK_W + KV_W) + 2 * B_WIDTH + N_BRANCH * D_MODEL

T_CTX = BATCH * SEQ
T_LAT = DEC_BATCH * DEC_SEQ
T_ALL = T_CTX + T_LAT
N_REQ = 1 + DEC_BATCH
CAP_CTX = CAP_FACTOR * SEQ // N_EXPERTS
CAP_LAT = CAP_FACTOR * DEC_SEQ // N_EXPERTS
ROWS_PER_EXPERT = BATCH * CAP_CTX + DEC_BATCH * CAP_LAT

LANE = 128
ROW_TILE = 512
N_CTX_TILES = T_CTX // ROW_TILE
FFN_ROW_TILE = 512
VMEM_LIMIT = 56 * 1024 * 1024
NEG_BIG = -1e30
LOG2_E = 1.4426950408889634

OFF_A = 0
OFF_AV = OFF_A + QK_W
OFF_BU = OFF_AV + KV_W
OFF_BV = OFF_BU + B_WIDTH
OFF_C = OFF_BV + B_WIDTH
OFF_CV = OFF_C + QK_W
OFF_G = OFF_CV + KV_W


def _params(sem, vmem=VMEM_LIMIT):
    return pltpu.CompilerParams(dimension_semantics=sem, vmem_limit_bytes=vmem)


def _sigmoid(x):
    return 1.0 / (1.0 + jnp.exp(-x))


def _gelu_tanh(x):
    return 0.5 * x * (1.0 + jnp.tanh(0.7978845608028654 * (x + 0.044715 * (x * x * x))))


def _split_bf16(x):
    hi = x.astype(BF16)
    lo = (x - hi.astype(F32)).astype(BF16)
    return hi, lo


def _mod_kernel(c_ref, w_ref, b_ref, o_ref):
    c = c_ref[...]
    s_hi, s_lo = _split_bf16(c * _sigmoid(c))
    w_hi, w_lo = _split_bf16(w_ref[0])
    acc = jnp.dot(s_hi, w_hi, preferred_element_type=F32)
    acc += jnp.dot(s_lo, w_hi, preferred_element_type=F32)
    acc += jnp.dot(s_hi, w_lo, preferred_element_type=F32)
    o_ref[0] = acc + b_ref[0]


def _modulation(cond8, w_mod, b_mod):
    n_col = 6 * D_MODEL // D_MODEL
    return pl.pallas_call(
        _mod_kernel,
        grid=(DEPTH, n_col),
        in_specs=[
            pl.BlockSpec((8, D_MODEL), lambda l, j: (0, 0)),
            pl.BlockSpec((1, D_MODEL, D_MODEL), lambda l, j: (l, 0, j)),
            pl.BlockSpec((1, 1, D_MODEL), lambda l, j: (l, 0, j)),
        ],
        out_specs=pl.BlockSpec((1, 8, D_MODEL), lambda l, j: (l, 0, j)),
        out_shape=jax.ShapeDtypeStruct((DEPTH, 8, 6 * D_MODEL), F32),
        compiler_params=_params(("arbitrary", "arbitrary")),
        name="modulation",
    )(cond8, w_mod, b_mod.reshape(DEPTH, 1, 6 * D_MODEL))


def _group_sumsq(y, bd_ref):
    return jnp.dot((y * y).astype(BF16), bd_ref[...], preferred_element_type=F32)


def _pick_pass(i, ctx_ref, lat_ref):
    return jnp.where(i < N_CTX_TILES, ctx_ref[...], lat_ref[...])


def _in_kernel(xc_ref, xl_ref, sc_ref, sh_ref, n1_ref, w_ref, cs_ref, sn_ref, ga_ref, gc_ref, gbv_ref, bd_qk_ref,
               bd_b_ref, qa_ref, ka_ref, va_ref, nka_ref, nva_ref, bu_ref, bv_ref, qc_ref, kc_ref, vc_ref, nkc_ref,
               nvc_ref, gt_ref):
    x = _pick_pass(pl.program_id(0), xc_ref, xl_ref)
    ms = jnp.mean(x * x, axis=-1, keepdims=True)
    h = x * lax.rsqrt(ms + EPS) * n1_ref[...]
    h = h * (1.0 + sc_ref[...]) + sh_ref[...]
    hb = h.astype(BF16).astype(F32)
    tm = x.shape[0]

    def proj(c0, width):
        return jnp.dot(hb, w_ref[:, c0:c0 + width], preferred_element_type=F32)

    cs = jnp.concatenate([cs_ref[...]] * (QK_W // LANE), axis=1)
    sn = jnp.concatenate([sn_ref[...]] * (QK_W // LANE), axis=1)
    lane = lax.broadcasted_iota(I32, (tm, QK_W), 1)
    first_half = (lane & ROPE_FREQS) == 0

    def qk_post(y, gain_ref):
        yn = y * lax.rsqrt(_group_sumsq(y, bd_qk_ref) * (1.0 / HEAD_DIM) + EPS) * gain_ref[...]
        partner = jnp.where(first_half, pltpu.roll(yn, QK_W - ROPE_FREQS, 1), pltpu.roll(yn, ROPE_FREQS, 1))
        return yn * cs + partner * sn

    def mixer(off_qk, off_v, gain_ref, q_ref, k_ref, v_ref, nk_ref, nv_ref):
        y = qk_post(proj(off_qk, QK_W), gain_ref)
        v = proj(off_v, KV_W)
        q_ref[...] = y[:, :Q_W].astype(BF16)
        k_ref[...] = y[:, Q_W:].astype(BF16)
        v_ref[...] = v.astype(BF16)

        nk_ref[...] = y[:, Q_W:]
        nv_ref[...] = v

    mixer(OFF_A, OFF_AV, ga_ref, qa_ref, ka_ref, va_ref, nka_ref, nva_ref)

    bu_ref[...] = _gelu_tanh(proj(OFF_BU, B_WIDTH)).astype(BF16)
    gv = _gelu_tanh(proj(OFF_BV, B_WIDTH))
    gvn = gv * lax.rsqrt(_group_sumsq(gv, bd_b_ref) * (1.0 / B_GROUP_CH) + EPS) * gbv_ref[...]
    bv_ref[...] = gvn.astype(BF16)

    mixer(OFF_C, OFF_CV, gc_ref, qc_ref, kc_ref, vc_ref, nkc_ref, nvc_ref)

    gate_chunk = 512
    for j in range(N_BRANCH * D_MODEL // gate_chunk):
        g = proj(OFF_G + j * gate_chunk, gate_chunk)
        gt_ref[:, j * gate_chunk:(j + 1) * gate_chunk] = _sigmoid(g).astype(BF16)


def _req_of_tile(i):
    return i // N_CTX_TILES


def _ctx_rows(w):
    return pl.BlockSpec((ROW_TILE, w), lambda i: (jnp.minimum(i, N_CTX_TILES - 1), 0))


def _lat_rows(w):
    return pl.BlockSpec((ROW_TILE, w), lambda i: (jnp.maximum(i - N_CTX_TILES, 0), 0))


MOD_SH1, MOD_SC1, MOD_G1, MOD_SH2, MOD_SC2, MOD_G2 = range(6)


def _layer_spec(stacked, layer, buffers=None):
    rest = stacked.shape[1:]
    mode = {} if buffers is None else {"pipeline_mode": pl.Buffered(buffers)}
    return pl.BlockSpec((None,) + rest, lambda *g: (layer,) + (0,) * len(rest), **mode)


def _mod_spec(layer, chunk, req):
    return pl.BlockSpec((None, None, 1, D_MODEL), lambda *g: (layer, req(*g), 0, chunk))


def _block_diag_ones(width, group):
    g = np.arange(width) // group
    return (g[:, None] == g[None, :]).astype(np.float32)


def _input_projection(x_ctx, x_lat, mods, layer, n1, w_in, cs, sn, gain_a, gain_c, gain_bv):
    bd_qk = jnp.asarray(_block_diag_ones(QK_W, HEAD_DIM), BF16)
    bd_b = jnp.asarray(_block_diag_ones(B_WIDTH, B_GROUP_CH), BF16)
    tm = ROW_TILE
    row = lambda w: pl.BlockSpec((tm, w), lambda i: (i, 0))
    full = lambda a: pl.BlockSpec(a.shape, lambda i: (0,) * a.ndim)
    rope = pl.BlockSpec((tm, LANE), lambda i: (_rope_tile(i), 0))
    cache_rows = T_CTX + tm
    spare = lambda w: pl.BlockSpec((tm, w), lambda i: (jnp.minimum(i, N_CTX_TILES), 0))
    mixer_outs = [(Q_W, BF16, T_ALL), (KV_W, BF16, T_ALL), (KV_W, BF16, T_ALL), (KV_W, F32, cache_rows),
                  (KV_W, F32, cache_rows)]
    outs = mixer_outs + [(B_WIDTH, BF16, T_ALL), (B_WIDTH, BF16, T_ALL)] + mixer_outs + [(N_BRANCH * D_MODEL, BF16, T_ALL)]
    return pl.pallas_call(
        _in_kernel,
        grid=(T_ALL // tm,),
        in_specs=[_ctx_rows(D_MODEL), _lat_rows(D_MODEL), _mod_spec(layer, MOD_SC1, _req_of_tile),
                  _mod_spec(layer, MOD_SH1, _req_of_tile), full(n1), _layer_spec(w_in, layer, buffers=1), rope, rope,
                  full(gain_a), full(gain_c), full(gain_bv), full(bd_qk), full(bd_b)],
        out_specs=[row(w) if rows == T_ALL else spare(w) for w, _, rows in outs],
        out_shape=[jax.ShapeDtypeStruct((rows, w), dt) for w, dt, rows in outs],
        compiler_params=_params(("arbitrary",)),
        name="input_projection",
    )(x_ctx, x_lat, mods, mods, n1, w_in, cs, sn, gain_a, gain_c, gain_bv, bd_qk, bd_b)


def _banded_start(carry, banded, qt_scr, heads, lo, hi, tq):
    k_refs, v_refs, biases = banded
    m0, acc0 = carry
    k_loc = jnp.concatenate([r[0, :, lo:hi] for r in k_refs], axis=0)
    vt = jnp.concatenate([r[0].astype(F32).T[lo:hi, :].astype(BF16) for r in v_refs], axis=1)
    vt = jnp.concatenate([vt, jnp.ones((DEN_ROWS, vt.shape[1]), BF16)], axis=0)
    pick = lambda x, j: jnp.concatenate([x[:, g * tq + j * BAND_Q:g * tq + (j + 1) * BAND_Q] for g in range(N_GRP)], axis=1)
    ms, accs = [], []
    for j in range(tq // BAND_Q):
        keys = slice(j * BAND_Q, (j + 1) * BAND_Q + 2 * BLOCK)
        qt_j = jnp.concatenate([qt_scr[h * HEAD_DIM:(h + 1) * HEAD_DIM, j * BAND_Q:(j + 1) * BAND_Q] for h in heads], axis=1)
        s = jnp.dot(k_loc[keys], qt_j, preferred_element_type=F32) + jnp.concatenate([biases[j]] * N_GRP, axis=1)
        m_old = pick(m0, j)
        m_new = jnp.maximum(m_old, jnp.max(s, axis=0, keepdims=True))
        p = jnp.exp2(s - m_new).astype(BF16)
        accs.append(pick(acc0, j) * jnp.exp2(m_old - m_new) + jnp.dot(vt[:, keys], p, preferred_element_type=F32))
        ms.append(m_new)
    gather = lambda parts: jnp.concatenate(
        [parts[j][:, g * BAND_Q:(g + 1) * BAND_Q] for g in range(N_GRP) for j in range(tq // BAND_Q)], axis=1)
    return gather(ms), gather(accs)


def _attention_tile(q_ref, sources, sink_ref, o_ref, qt_scr, ot_scr, *, tq, key_chunk, banded=None):
    width = N_GRP * tq
    for j in range(Q_W // LANE):
        qt_scr[j * LANE:(j + 1) * LANE, :] = q_ref[0, :, j * LANE:(j + 1) * LANE].astype(F32).T.astype(BF16)
    for kv in range(N_KV):
        lo, hi = kv * HEAD_DIM, (kv + 1) * HEAD_DIM
        heads = [kv * N_GRP + g for g in range(N_GRP)]
        qt = jnp.concatenate([qt_scr[h * HEAD_DIM:(h + 1) * HEAD_DIM, :] for h in heads], axis=1)

        def step(carry, kref, vref, c0, size):
            m, acc = carry
            s = jnp.dot(kref[0, pl.ds(c0, size), lo:hi], qt, preferred_element_type=F32)
            vt =vref[0, pl.ds(c0, size), :].astype(F32).T[lo:hi, :].astype(BF16)
            vt = jnp.concatenate([vt, jnp.ones((DEN_ROWS, size), BF16)], axis=0)
            m_new = jnp.maximum(m, jnp.max(s, axis=0, keepdims=True))
            p = jnp.exp2(s - m_new).astype(BF16)
            acc = acc * jnp.exp2(m - m_new) + jnp.dot(vt, p, preferred_element_type=F32)
            return m_new, acc

        if sink_ref is not None:
            m0 = jnp.concatenate([jnp.full((1, tq), sink_ref[h] * LOG2_E, F32) for h in heads], axis=1)
            den0 = jnp.ones((DEN_ROWS, width), F32)
        else:
            m0 = jnp.full((1, width), NEG_BIG, F32)
            den0 = jnp.zeros((DEN_ROWS, width), F32)
        carry = (m0, jnp.concatenate([jnp.zeros((HEAD_DIM, width), F32), den0], axis=0))
        if banded is not None:
            carry = _banded_start(carry, banded, qt_scr, heads, lo, hi, tq)
        for kref, vref in sources:
            n_rows = kref.shape[1]
            n_full = n_rows // key_chunk
            if n_full > 1:
                carry = lax.fori_loop(
                    0, n_full,
                    lambda c, cr: step(cr, kref, vref, pl.multiple_of(c * key_chunk, key_chunk), key_chunk), carry)
            elif n_full == 1:
                carry = step(carry, kref, vref, 0, key_chunk)
            if n_rows - n_full * key_chunk:
                carry = step(carry, kref, vref, n_full * key_chunk, n_rows - n_full * key_chunk)
        _, acc = carry
        o = acc[:HEAD_DIM] / acc[HEAD_DIM:HEAD_DIM + 1]
        for g, h in enumerate(heads):
            ot_scr[h * HEAD_DIM:(h + 1) * HEAD_DIM, :] = o[:, g * tq:(g + 1) * tq]
    for j in range(Q_W // LANE):
        o_ref[0, :, j * LANE:(j + 1) * LANE] = ot_scr[j * LANE:(j + 1) * LANE, :].T.astype(o_ref.dtype)


def _dense_attn_kernel(*refs, tq, key_chunk, has_extra, has_sink):
    refs = list(refs)
    q_ref, k_ref, v_ref = refs[:3]
    del refs[:3]
    sources = [(k_ref, v_ref)]
    if has_extra:
        sources.append((refs.pop(0), refs.pop(0)))
    sink_ref = refs.pop(0) if has_sink else None
    o_ref, qt_scr, ot_scr = refs
    _attention_tile(q_ref, sources, sink_ref, o_ref, qt_scr, ot_scr, tq=tq, key_chunk=key_chunk)


def _attention_scratch(tq):
    return [pltpu.VMEM((Q_W, tq), BF16), pltpu.VMEM((Q_W, tq), F32)]


def _dense_attention(q, k, v, extra, sink, *, n_req, off, tq, key_chunk):
    s = q.shape[1]
    kv_spec = pl.BlockSpec((1, s, KV_W), lambda i, j: (off + i, 0, 0))
    in_specs = [pl.BlockSpec((1, tq, Q_W), lambda i, j: (off + i, j, 0)), kv_spec, kv_spec]
    args = [q, k, v]
    if extra is not None:
        in_specs += [pl.BlockSpec((1, extra[0].shape[1], KV_W), lambda i, j: (i, 0, 0))] * 2
        args += list(extra)
    if sink is not None:
        in_specs.append(pl.BlockSpec(memory_space=pltpu.SMEM))
        args.append(sink)
    return pl.pallas_call(
        functools.partial(_dense_attn_kernel, tq=tq, key_chunk=key_chunk, has_extra=extra is not None,
                          has_sink=sink is not None),
        grid=(n_req, s // tq),
        in_specs=in_specs,
        out_specs=pl.BlockSpec((1, tq, Q_W), lambda i, j: (i, j, 0)),
        out_shape=jax.ShapeDtypeStruct((n_req, s, Q_W), BF16),
        scratch_shapes=_attention_scratch(tq),
        compiler_params=_params(("arbitrary", "arbitrary")),
        name="dense_attention",
    )(*args)


CTX_ATTN_PER_STEP = 4


def _ctx_attn_kernel(qa_ref, ka_ref, va_ref, qc_ref, kc_ref, vc_ref, sink_ref, oa_ref, oc_ref, *scratch):
    for r in range(CTX_ATTN_PER_STEP):
        one = lambda ref: ref.at[pl.ds(r, 1)]
        qta, ota, qtc, otc = scratch[4 * r:4 * r + 4]
        _attention_tile(one(qa_ref), [(one(ka_ref), one(va_ref))], sink_ref, one(oa_ref), qta, ota,
                        tq=SEQ, key_chunk=SEQ)
        _attention_tile(one(qc_ref), [(one(kc_ref), one(vc_ref))], None, one(oc_ref), qtc, otc,
                        tq=SEQ, key_chunk=SEQ)


def _context_attention(qa, ka, va, qc, kc, vc, sink):
    n = CTX_ATTN_PER_STEP
    q_spec = pl.BlockSpec((n, SEQ, Q_W), lambda i: (i, 0, 0))
    kv_spec = pl.BlockSpec((n, SEQ, KV_W), lambda i: (i, 0, 0))
    return pl.pallas_call(
        _ctx_attn_kernel,
        grid=(BATCH // n,),
        in_specs=[q_spec, kv_spec, kv_spec, q_spec, kv_spec, kv_spec, pl.BlockSpec(memory_space=pltpu.SMEM)],
        out_specs=[q_spec, q_spec],
        out_shape=[jax.ShapeDtypeStruct((BATCH, SEQ, Q_W), BF16)] * 2,
        scratch_shapes=_attention_scratch(SEQ) * (2 * n),
        compiler_params=_params(("arbitrary",)),
        name="context_attention",
    )(qa, ka, va, qc, kc, vc, sink)


DEN_ROWS = 16
WINDOW_TQ = 512
BAND_Q = 2 * BLOCK


def _window_attn_kernel(q_ref, kp_ref, kc_ref, kn_ref, vp_ref, vc_ref, vn_ref, ck_ref, cv_ref, band_ref, sink_ref, o_ref,
                        qt_scr, ot_scr, *, seq):
    q_pos0 = pl.program_id(1) * WINDOW_TQ
    n_grp = WINDOW_TQ // BAND_Q
    band = band_ref[...]
    hide_prev = jnp.where(q_pos0 >= BLOCK, 0.0, NEG_BIG)
    hide_next = jnp.where(q_pos0 + WINDOW_TQ < seq, 0.0, NEG_BIG)
    first = jnp.concatenate([band[:BLOCK] + hide_prev, band[BLOCK:]], axis=0)
    last = jnp.concatenate([band[:BAND_Q + BLOCK], band[BAND_Q + BLOCK:] + hide_next], axis=0)
    biases = [first] + [band] * (n_grp - 2) + [last]
    banded = ((kp_ref, kc_ref, kn_ref), (vp_ref, vc_ref, vn_ref), biases)
    _attention_tile(q_ref, [(ck_ref, cv_ref)], sink_ref, o_ref, qt_scr, ot_scr, tq=WINDOW_TQ,
                    key_chunk=PAST_LEN, banded=banded)


def _band_bias():
    d = (np.arange(BAND_Q + 2 * BLOCK) - BLOCK)[:, None] - np.arange(BAND_Q)[None, :]
    return np.where(np.abs(d) <= WINDOW, 0.0, NEG_BIG).astype(np.float32)


def _window_attention(q, k, v, ck, cv, sink, *, n_req, off):
    b, s = n_req, q.shape[1]
    nb = s // BLOCK
    per_tile = WINDOW_TQ // BLOCK
    edge = lambda f: pl.BlockSpec((1, BLOCK, KV_W), lambda i, j: (off + i, f(j), 0))
    prev = lambda j: jnp.maximum(j * per_tile - 1, 0)
    nxt = lambda j: jnp.minimum((j + 1) * per_tile, nb - 1)
    cur = pl.BlockSpec((1, WINDOW_TQ, KV_W), lambda i, j: (off + i, j, 0))
    ctx = pl.BlockSpec((1, PAST_LEN, KV_W), lambda i, j: (i, 0, 0))
    assert WINDOW_TQ // BAND_Q >= 2
    band = _band_bias()
    return pl.pallas_call(
        functools.partial(_window_attn_kernel, seq=s),
        grid=(b, s // WINDOW_TQ),
        in_specs=[pl.BlockSpec((1, WINDOW_TQ, Q_W), lambda i, j: (off + i, j, 0)),
                  edge(prev), cur, edge(nxt), edge(prev), cur, edge(nxt), ctx, ctx,
                  pl.BlockSpec(band.shape, lambda i, j: (0, 0)),
                  pl.BlockSpec(memory_space=pltpu.SMEM)],
        out_specs=pl.BlockSpec((1, WINDOW_TQ, Q_W), lambda i, j: (i, j, 0)),
        out_shape=jax.ShapeDtypeStruct((b, s, Q_W), BF16),
        scratch_shapes=_attention_scratch(WINDOW_TQ),
        compiler_params=_params(("arbitrary", "arbitrary")),
        name="window_attention",
    )(q, k, k, k, v, v, v, ck, cv, band, sink)


def _pack_halves(x):
    half = x.shape[1] // 2
    return pltpu.pack_elementwise([x[:, :half], x[:, half:]], packed_dtype=BF16)


def _unpack_halves(words):
    return tuple(pltpu.unpack_elementwise(words, index=i, packed_dtype=BF16, unpacked_dtype=F32).astype(BF16)
                 for i in range(2))


def _merge_kernel(xc_ref, xl_ref, oac_ref, oal_ref, bu_ref, bv_ref, occ_ref, ocl_ref, gt_ref, wa_ref, wb_ref, wc_ref,
                  wo_ref, ws_ref, bs_ref, g1_ref, sc2_ref, sh2_ref, n2_ref, wr_ref, br_ref, x1_ref, h2p_ref, afft_ref):
    i = pl.program_id(0)
    tm = xc_ref.shape[0]
    group = lax.broadcasted_iota(I32, (CHUNK, B_WIDTH), 1) // B_GROUP_CH
    obs = []
    for c in range(tm // CHUNK):
        v = bv_ref[c * CHUNK:(c + 1) * CHUNK, :]
        sv = jnp.zeros((CHUNK, B_WIDTH), F32)
        for g in range(B_GROUPS):
            sv = jnp.where(group == g, jnp.dot(ws_ref[g], v, preferred_element_type=F32), sv)
        u = bu_ref[c * CHUNK:(c + 1) * CHUNK, :].astype(F32)
        obs.append((u * (sv + bs_ref[...])).astype(BF16))
    ob = jnp.concatenate(obs, axis=0)

    oa = _pick_pass(i, oac_ref, oal_ref)
    oc = _pick_pass(i, occ_ref, ocl_ref)
    merged = gt_ref[:, 0:D_MODEL].astype(F32) * jnp.dot(oa, wa_ref[...], preferred_element_type=F32)
    merged += gt_ref[:, D_MODEL:2 * D_MODEL].astype(F32) * jnp.dot(ob, wb_ref[...], preferred_element_type=F32)
    merged += gt_ref[:, 2 * D_MODEL:3 * D_MODEL].astype(F32) * jnp.dot(oc, wc_ref[...], preferred_element_type=F32)
    y = jnp.dot(merged.astype(BF16).astype(F32), wo_ref[...], preferred_element_type=F32)
    x1 = _pick_pass(i, xc_ref, xl_ref) + g1_ref[...] * y
    x1_ref[...] = x1

    ms = jnp.mean(x1 * x1, axis=-1, keepdims=True)
    h2 = x1 * lax.rsqrt(ms + EPS) * n2_ref[...]
    h2 = h2 * (1.0 + sc2_ref[...]) + sh2_ref[...]
    h2p_ref[...] = _pack_halves(h2)

    logits = jnp.dot(h2.astype(BF16), wr_ref[...], preferred_element_type=F32) + br_ref[...]
    e = jnp.exp(logits - jnp.max(logits, axis=-1, keepdims=True))
    aff = e / jnp.sum(e, axis=-1, keepdims=True)
    afft_ref[...] = aff.T[:N_EXPERTS, :]


def _merge(x_ctx, x_lat, oa_ctx, oa_lat, bu, bv, oc_ctx, oc_lat, gt, wa, wb, wc, wo, ws, bs, mods, layer, n2, wr, br):
    tm = ROW_TILE
    row = lambda w: pl.BlockSpec((tm, w), lambda i: (i, 0))
    full = lambda a: pl.BlockSpec(a.shape, lambda i: (0,) * a.ndim)
    mod = lambda chunk: _mod_spec(layer, chunk, _req_of_tile)
    stack = lambda a: _layer_spec(a, layer)
    return pl.pallas_call(
        _merge_kernel,
        grid=(T_ALL // tm,),
        in_specs=[_ctx_rows(D_MODEL), _lat_rows(D_MODEL), _ctx_rows(Q_W), _lat_rows(Q_W), row(B_WIDTH), row(B_WIDTH),
                  _ctx_rows(Q_W), _lat_rows(Q_W), row(N_BRANCH * D_MODEL),
                  stack(wa), stack(wb), stack(wc), stack(wo), stack(ws), full(bs),
                  mod(MOD_G1), mod(MOD_SC2), mod(MOD_SH2), full(n2), stack(wr), full(br)],
        out_specs=[row(D_MODEL), row(D_MODEL // 2), pl.BlockSpec((N_EXPERTS, tm), lambda i: (0, i))],
        out_shape=[jax.ShapeDtypeStruct((T_ALL, D_MODEL), F32), jax.ShapeDtypeStruct((T_ALL, D_MODEL // 2), jnp.uint32),
                   jax.ShapeDtypeStruct((N_EXPERTS, T_ALL), F32)],
        compiler_params=_params(("arbitrary",)),
        name="merge_router",
    )(x_ctx, x_lat, oa_ctx, oa_lat, bu, bv, oc_ctx, oc_lat, gt, wa, wb, wc, wo, ws, bs, mods, mods, mods, n2, wr, br)


def _select_kernel(aff_ref, idx_ref, val_ref, *rest, n, cap, row_chunk):
    idx_row_ref = rest[0] if len(rest) == 4 else None
    possel_ref, idx_scr, val_scr = rest[-3:]
    a = aff_ref[...]
    rows = a.shape[0]
    tok = lax.broadcasted_iota(I32, (rows, n), 1)

    def count(ones):
        return jnp.sum(ones, axis=1, keepdims=True)

    def at_least(word):
        return jnp.where(a >= pltpu.bitcast(word, F32), 1, 0)

    def greedy_bits(start, top_bit, keep):
        word = start
        bits = list(range(top_bit, -1, -1))
        if rows > SEARCH_PAIR_MAX_ROWS:
            for bit in bits:
                cand = word | (1 << bit)
                word = jnp.where(keep(cand), cand, word)
            return word
        if len(bits) % 2:
            cand = word | (1 << bits[0])
            word = jnp.where(keep(cand), cand, word)
            bits = bits[1:]
        for hi, lo in zip(bits[0::2], bits[1::2]):
            c_lo, c_hi, c_both = word | (1 << lo), word | (1 << hi), word | (1 << hi) | (1 << lo)
            word = jnp.where(keep(c_both), c_both, jnp.where(keep(c_hi), c_hi, jnp.where(keep(c_lo), c_lo, word)))
        return word

    thr = greedy_bits(jnp.zeros((rows, 1), I32), 30, lambda w: count(at_least(w)) >= cap)
    above = at_least(thr + 1)
    tied = at_least(thr) - above
    need = cap - count(above)
    last = greedy_bits(jnp.zeros((rows, 1), I32), n.bit_length() - 2,
                       lambda w: count(jnp.where(tok < w, tied, 0)) < need)
    sel = above + jnp.where(tok <= last, tied, 0)

    blk = min(n, 256)
    tri = jnp.where(lax.broadcasted_iota(I32, (blk, blk), 0) <= lax.broadcasted_iota(I32, (blk, blk), 1),
                    1.0, 0.0).astype(BF16)
    sel_f = sel.astype(F32)
    offset = jnp.zeros((rows, 1), F32)
    for j in range(n // blk):
        s_blk = sel_f[:, j * blk:(j + 1) * blk]
        incl = jnp.dot(s_blk.astype(BF16), tri, preferred_element_type=F32)
        pos = (incl - s_blk + offset).astype(I32)
        possel_ref[:, j * blk:(j + 1) * blk] = jnp.where(sel[:, j * blk:(j + 1) * blk] > 0, pos, -1)
        offset = offset + incl[:, blk - 1:blk]

    tb = min(n, TOKEN_BLOCK)
    n_blk = n // tb

    def fold_lanes(x):
        acc = x[:, :LANE]
        for k in range(1, tb // LANE):
            acc = acc + x[:, k * LANE:(k + 1) * LANE]
        return acc

    def match(e, slot, t0):
        hit = possel_ref[pl.ds(e, 1), pl.ds(t0, tb)] == slot
        tok = t0 + lax.broadcasted_iota(I32, (1, tb), 1)
        return (fold_lanes(jnp.where(hit, tok, 0)),
                fold_lanes(jnp.where(hit, aff_ref[pl.ds(e, 1), pl.ds(t0, tb)], 0.0)))

    def per_row(e, _):
        ends, run = [], 0
        for j in range(n_blk - 1):
            run = run + jnp.sum(jnp.where(possel_ref[pl.ds(e, 1), j * tb:(j + 1) * tb] >= 0, 1, 0))
            ends.append(run)

        def per_chunk(c, _):
            r0 = pl.multiple_of(c * row_chunk, row_chunk)
            slot = lax.broadcasted_iota(I32, (row_chunk, 1), 0) + r0
            if n_blk == 1:
                idx, val = match(e, slot, 0)
            else:
                first = sum(jnp.where(end <= r0, 1, 0) for end in ends)
                last = 1 + sum(jnp.where(end < r0 + row_chunk, 1, 0) for end in ends)

                def per_block(j, acc):
                    i, v = match(e, slot, pl.multiple_of(j * tb, tb))
                    return acc[0] + i, acc[1] + v

                idx, val = lax.fori_loop(first, last, per_block,
                                         (jnp.zeros((row_chunk, LANE), I32), jnp.zeros((row_chunk, LANE), F32)))
            idx_scr[pl.ds(r0, row_chunk), :] = idx
            val_scr[pl.ds(r0, row_chunk), :] = val
            return 0

        lax.fori_loop(0, cap // row_chunk, per_chunk, 0)
        idx = jnp.sum(idx_scr[...], axis=1, keepdims=True)
        idx_ref[e] = idx
        val_ref[e] = jnp.sum(val_scr[...], axis=1, keepdims=True)
        if idx_row_ref is not None:
            idx_row_ref[pl.ds(e, 1), :] = jnp.broadcast_to(idx.astype(F32), (cap, LANE)).T[0:1, :].astype(I32)
        return 0

    def per_small_row(e, _):
        idx, val = match(e, lax.broadcasted_iota(I32, (cap, 1), 0), 0)
        idx_ref[e] = jnp.sum(idx, axis=1, keepdims=True)
        val_ref[e] = jnp.sum(val, axis=1, keepdims=True)
        return 0

    if n_blk == 1 and cap == row_chunk:
        lax.fori_loop(0, rows, per_small_row, 0, unroll=4)
    else:
        lax.fori_loop(0, rows, per_row, 0)


def _select(aff_rows, rows_per_step, cap):
    r, n = aff_rows.shape
    row_chunk = min(cap, 64)
    out_specs = [pl.BlockSpec((rows_per_step, cap, 1), lambda s: (s, 0, 0))] * 2
    out_shape = [jax.ShapeDtypeStruct((r, cap, 1), I32), jax.ShapeDtypeStruct((r, cap, 1), F32)]
    if cap % LANE == 0:
        out_specs.append(pl.BlockSpec((rows_per_step, cap), lambda s: (s, 0)))
        out_shape.append(jax.ShapeDtypeStruct((r, cap), I32))
    return pl.pallas_call(
        functools.partial(_select_kernel, n=n, cap=cap, row_chunk=row_chunk),
        grid=(r // rows_per_step,),
        in_specs=[pl.BlockSpec((rows_per_step, n), lambda s: (s, 0))],
        out_specs=out_specs,
        out_shape=out_shape,
        scratch_shapes=[pltpu.VMEM((rows_per_step, n), I32), pltpu.VMEM((cap, LANE), I32), pltpu.VMEM((cap, LANE), F32)],
        compiler_params=_params(("arbitrary",)),
        name="expert_select",
    )(aff_rows)


CTX_SLOTS = N_EXPERTS * CAP_CTX
TOKEN_BLOCK = 512
SEARCH_PAIR_MAX_ROWS = 32
SLOT_GROUP = 16


CTX_PER_STEP = 4


def _ctx_slot_onehot(idx, slots_on_rows):
    idx = idx.reshape(CTX_SLOTS, 1)
    if slots_on_rows:
        hit = idx == lax.broadcasted_iota(I32, (CTX_SLOTS, SEQ), 1)
    else:
        idx_lane = jnp.broadcast_to(idx.astype(F32), (CTX_SLOTS, LANE)).T[0:1, :]
        hit = idx_lane == lax.broadcasted_iota(I32, (SEQ, CTX_SLOTS), 0).astype(F32)
    return jnp.where(hit, 1.0, 0.0).astype(BF16)


def _gather_ctx_kernel(idx_ref, h_ref, out_ref):
    for r in range(CTX_PER_STEP):
        onehot = _ctx_slot_onehot(idx_ref[r * N_EXPERTS:(r + 1) * N_EXPERTS], True)
        lo, hi = _unpack_halves(h_ref[r * SEQ:(r + 1) * SEQ, :])
        g_lo = jnp.dot(onehot, lo, preferred_element_type=F32)
        g_hi = jnp.dot(onehot, hi, preferred_element_type=F32)
        packed = pltpu.pack_elementwise([g_lo, g_hi], packed_dtype=BF16)
        out_ref[:, r * CAP_CTX:(r + 1) * CAP_CTX, :] = packed.reshape(N_EXPERTS, CAP_CTX, D_MODEL // 2)


def _gather_ctx(idx_c, h2p):
    n = CTX_PER_STEP
    return pl.pallas_call(
        _gather_ctx_kernel,
        grid=(BATCH // n,),
        in_specs=[pl.BlockSpec((n * N_EXPERTS, CAP_CTX, 1), lambda b: (b, 0, 0)),
                  pl.BlockSpec((n * SEQ, D_MODEL // 2), lambda b: (b, 0))],
        out_specs=pl.BlockSpec((N_EXPERTS, n * CAP_CTX, D_MODEL // 2), lambda b: (0, b, 0)),
        out_shape=jax.ShapeDtypeStruct((N_EXPERTS, BATCH * CAP_CTX, D_MODEL // 2), jnp.uint32),
        compiler_params=_params(("arbitrary",)),
        name="gather_ctx",
    )(idx_c, h2p)


def _gather_lat_kernel(idx_ref, src_ref, out_ref):
    base = (pl.program_id(0) * N_EXPERTS + pl.program_id(1)) * CAP_LAT

    def body(it, _):
        r0 = pl.multiple_of(it * SLOT_GROUP, SLOT_GROUP)
        picked = [src_ref[0, pl.ds(idx_ref[base + r0 + k], 1), :] for k in range(SLOT_GROUP)]
        dst = out_ref.at[0, pl.ds(r0, SLOT_GROUP)]
        for k in range(SLOT_GROUP):
            dst[k:k + 1, :] = picked[k]
        return 0

    lax.fori_loop(0, CAP_LAT // SLOT_GROUP, body, 0)


def _gather_lat(idx_flat, h2p3, off):
    return pl.pallas_call(
        _gather_lat_kernel,
        grid_spec=pltpu.PrefetchScalarGridSpec(
            num_scalar_prefetch=1,
            grid=(DEC_BATCH, N_EXPERTS),
            in_specs=[pl.BlockSpec((1, DEC_SEQ, D_MODEL // 2), lambda b, e, idx: (off + b, 0, 0))],
            out_specs=pl.BlockSpec((1, CAP_LAT, D_MODEL // 2), lambda b, e, idx: (e, b, 0)),
        ),
        out_shape=jax.ShapeDtypeStruct((N_EXPERTS, DEC_BATCH * CAP_LAT, D_MODEL // 2), jnp.uint32),
        compiler_params=_params(("arbitrary", "arbitrary")),
        name="gather_lat",
    )(idx_flat, h2p3)


N_CTX_FFN_TILES = BATCH * CAP_CTX // FFN_ROW_TILE


def _ffn_kernel(xc_ref, xl_ref, vc_ref, vl_ref, g2_ref, wg_hbm, wu_hbm, wd_hbm, o_ref, wbuf, sems, *, layer):
    e, j = pl.program_id(0), pl.program_id(1)
    slot = e % 2
    weights = (wg_hbm, wu_hbm, wd_hbm)

    def fetch(m, expert, into):
        return pltpu.make_async_copy(weights[m].at[layer, expert], wbuf.at[into, m], sems.at[into, m])

    @pl.when(jnp.logical_and(e == 0, j == 0))
    def _():
        for m in range(3):
            fetch(m, 0, 0).start()
        for m in range(3):
            fetch(m, 0, 0).wait()

    for m in range(3):
        @pl.when(jnp.logical_and(j == m, e + 1 < N_EXPERTS))
        def _(m=m):
            fetch(m, e + 1, 1 - slot).start()

    @pl.when(jnp.logical_and(j == 0, e > 0))
    def _():
        for m in range(3):
            fetch(m, e, slot).wait()

    is_ctx = j < N_CTX_FFN_TILES
    x = jnp.where(is_ctx, jnp.concatenate(_unpack_halves(xc_ref[0]), axis=1),
                  jnp.concatenate(_unpack_halves(xl_ref[0]), axis=1)).astype(F32)
    g = jnp.dot(x, wbuf[slot, 0], preferred_element_type=F32)
    u = jnp.dot(x, wbuf[slot, 1], preferred_element_type=F32)
    hh = (g * _sigmoid(g)) * u
    y = jnp.dot(hh.astype(BF16).astype(F32), wbuf[slot, 2], preferred_element_type=F32)
    o_ref[0] = (y * jnp.where(is_ctx, vc_ref[...].reshape(FFN_ROW_TILE, 1), vl_ref[...])) * g2_ref[...]


def _expert_ffn(xg_ctx, xg_lat, val_ctx, val_lat, mods, w_gate, w_up, w_down, layer):
    tr = FFN_ROW_TILE
    assert tr == BATCH * CAP_CTX == CAP_LAT
    n_tiles = ROWS_PER_EXPERT // tr
    assert n_tiles >= 3 and D_MODEL == EXPERT_FF
    in_hbm = pl.BlockSpec(memory_space=pl.ANY)

    ctx_tile = lambda j: jnp.minimum(j, N_CTX_FFN_TILES - 1)
    lat_tile = lambda j: jnp.maximum(j - N_CTX_FFN_TILES, 0)
    return pl.pallas_call(
        functools.partial(_ffn_kernel, layer=layer),
        grid=(N_EXPERTS, n_tiles),
        in_specs=[pl.BlockSpec((1, tr, D_MODEL // 2), lambda e, j: (e, ctx_tile(j), 0)),
                  pl.BlockSpec((1, tr, D_MODEL // 2), lambda e, j: (e, lat_tile(j), 0)),
                  pl.BlockSpec((BATCH, None, CAP_CTX, 1), lambda e, j: (0, e, 0, 0)),
                  pl.BlockSpec((None, None, CAP_LAT, 1), lambda e, j: (lat_tile(j), e, 0, 0)),
                  _mod_spec(layer, MOD_G2, lambda e, j: j),
                  in_hbm, in_hbm, in_hbm],
        out_specs=pl.BlockSpec((1, tr, D_MODEL), lambda e, j: (e, j, 0)),
        out_shape=jax.ShapeDtypeStruct((N_EXPERTS, ROWS_PER_EXPERT, D_MODEL), F32),
        scratch_shapes=[pltpu.VMEM((2, 3, D_MODEL, EXPERT_FF), F32), pltpu.SemaphoreType.DMA((2, 3))],
        compiler_params=_params(("arbitrary", "arbitrary")),
        name="expert_ffn",
    )(xg_ctx, xg_lat, val_ctx, val_lat, mods, w_gate, w_up, w_down)


def _scatter_ctx_kernel(idx_ref, y_ref, x1_ref, out_ref):
    for r in range(CTX_PER_STEP):
        onehot = _ctx_slot_onehot(idx_ref[r * N_EXPERTS:(r + 1) * N_EXPERTS], False)
        y_hi, y_lo = _split_bf16(y_ref[:, r * CAP_CTX:(r + 1) * CAP_CTX, :].reshape(CTX_SLOTS, D_MODEL))
        moe = jnp.dot(onehot, y_hi, preferred_element_type=F32) + jnp.dot(onehot, y_lo, preferred_element_type=F32)
        out_ref[r * SEQ:(r + 1) * SEQ, :] = x1_ref[r * SEQ:(r + 1) * SEQ, :] + moe


def _scatter_ctx(idx_c, yg, x1):
    n = CTX_PER_STEP
    return pl.pallas_call(
        _scatter_ctx_kernel,
        grid=(BATCH // n,),
        in_specs=[pl.BlockSpec((n * N_EXPERTS, CAP_CTX, 1), lambda b: (b, 0, 0)),
                  pl.BlockSpec((N_EXPERTS, n * CAP_CTX, D_MODEL), lambda b: (0, b, 0)),
                  pl.BlockSpec((n * SEQ, D_MODEL), lambda b: (b, 0))],
        out_specs=pl.BlockSpec((n * SEQ, D_MODEL), lambda b: (b, 0)),
        out_shape=jax.ShapeDtypeStruct((T_CTX, D_MODEL), F32),
        compiler_params=_params(("arbitrary",)),
        name="scatter_ctx",
    )(idx_c, yg, x1)


def _scatter_lat_kernel(idx_ref, y_ref, x1_hbm, out_ref, sem, *, off):
    b, e = pl.program_id(0), pl.program_id(2)

    @pl.when(e == 0)
    def _():
        load = pltpu.make_async_copy(x1_hbm.at[pl.ds(off + b, 1)], out_ref, sem)
        load.start()
        load.wait()

    base = (b * N_EXPERTS + e) * CAP_LAT

    def body(it, _):
        r0 = pl.multiple_of(it * SLOT_GROUP, SLOT_GROUP)
        rows = [idx_ref[base + r0 + k] for k in range(SLOT_GROUP)]
        old = [out_ref[0, pl.ds(rows[k], 1), :] for k in range(SLOT_GROUP)]
        y = y_ref[0, pl.ds(r0, SLOT_GROUP), :]
        for k in range(SLOT_GROUP):
            out_ref[0, pl.ds(rows[k], 1), :] = old[k] + y[k:k + 1, :]
        return 0

    lax.fori_loop(0, CAP_LAT // SLOT_GROUP, body, 0)


def _scatter_lat(idx_flat, yg, x1_3, off):
    blk0 = BATCH * CAP_CTX // CAP_LAT
    return pl.pallas_call(
        functools.partial(_scatter_lat_kernel, off=off),
        grid_spec=pltpu.PrefetchScalarGridSpec(
            num_scalar_prefetch=1,
            grid=(DEC_BATCH, 1, N_EXPERTS),
            in_specs=[pl.BlockSpec((1, CAP_LAT, D_MODEL), lambda b, h, e, idx: (e, blk0 + b, 0)),
                      pl.BlockSpec(memory_space=pl.ANY)],
            out_specs=pl.BlockSpec((1, DEC_SEQ, D_MODEL), lambda b, h, e, idx: (b, 0, 0)),
            scratch_shapes=[pltpu.SemaphoreType.DMA(())],
        ),
        out_shape=jax.ShapeDtypeStruct((DEC_BATCH, DEC_SEQ, D_MODEL), F32),
        compiler_params=_params(("arbitrary", "arbitrary", "arbitrary")),
        name="scatter_lat",
    )(idx_flat, yg, x1_3)


def _rope_tables():
    pos = np.arange(DEC_SEQ)
    freq = (np.float32(ROPE_THETA) ** (-np.arange(ROPE_FREQS, dtype=np.float32) / np.float32(ROPE_FREQS)))
    ang_r = (pos // GRID_W).astype(np.float32)[:, None] * freq.astype(np.float32)
    ang_c = (pos % GRID_W).astype(np.float32)[:, None] * freq.astype(np.float32)
    cos = np.concatenate([np.cos(ang_r)] * 2 + [np.cos(ang_c)] * 2, axis=-1)
    sin = np.concatenate([-np.sin(ang_r), np.sin(ang_r), -np.sin(ang_c), np.sin(ang_c)], axis=-1)
    reps = LANE // HEAD_DIM
    cs = np.concatenate([np.ones((ROW_TILE, LANE)), np.tile(cos, (1, reps))], axis=0).astype(np.float32)
    sn = np.concatenate([np.zeros((ROW_TILE, LANE)), np.tile(sin, (1, reps))], axis=0).astype(np.float32)
    return cs, sn


def _rope_tile(i):
    lat = jnp.maximum(i - N_CTX_TILES, 0) % (DEC_SEQ // ROW_TILE)
    return jnp.where(i < N_CTX_TILES, 0, 1 + lat)


def _qk_gain(q_norm, k_norm):
    q = jnp.tile(q_norm, N_HEADS) * (HEAD_DIM ** -0.5 * LOG2_E)
    return jnp.concatenate([q, jnp.tile(k_norm, N_KV)])[None, :]


def kernel(x_prompt, x_sample, cache_a_k, cache_a_v, cache_c_k, cache_c_v, c, c_ctx, norm1_g, w_mod, b_mod, w_in,
           a_q_norm, a_k_norm, a_sink, b_v_norm, b_ws, b_bs, c_q_norm, c_k_norm, w_a_o, w_b_o, w_c_o, w_out, norm2_g,
           w_router, b_router, w_gate, w_up, w_down):
    cond8 = jnp.concatenate([c_ctx[None, :], c, jnp.zeros((8 - N_REQ, D_MODEL), F32)], axis=0)
    mods = _modulation(cond8, w_mod, b_mod).reshape(DEPTH, 8, 1, 6 * D_MODEL)

    cs, sn = _rope_tables()
    wa_b, wb_b, wc_b, wo_b = w_a_o.astype(BF16), w_b_o.astype(BF16), w_c_o.astype(BF16), w_out
    ws_b = b_ws.astype(BF16)
    wr_pad = jnp.pad(w_router, ((0, 0), (0, 0), (0, LANE - N_EXPERTS))).astype(BF16)
    br_pad = jnp.pad(b_router, ((0, 0), (0, LANE - N_EXPERTS)), constant_values=NEG_BIG)

    by_seq = lambda a: a.reshape(T_ALL // SEQ, SEQ, a.shape[-1])
    by_dec = lambda a: a.reshape(T_ALL // DEC_SEQ, DEC_SEQ, a.shape[-1])
    lat_off = T_CTX // DEC_SEQ

    caches = [a.reshape(DEC_BATCH, DEPTH, PAST_LEN, KV_W).astype(BF16)
              for a in (cache_a_k, cache_a_v, cache_c_k, cache_c_v)]

    x_ctx = x_prompt.reshape(T_CTX, D_MODEL)
    x_lat = x_sample.reshape(T_LAT, D_MODEL)
    new_kv = [[], [], [], []]
    for l in range(DEPTH):
        qa, ka_b, va_b, nka, nva, bu, bv, qc, kc_b, vc_b, nkc, nvc, gt = _input_projection(
            x_ctx, x_lat, mods, l, norm1_g[l][None, :], w_in, cs, sn,
            _qk_gain(a_q_norm[l], a_k_norm[l]), _qk_gain(c_q_norm[l], c_k_norm[l]), b_v_norm[l][None, :])
        for lst, arr in zip(new_kv, (nka, nva, nkc, nvc)):
            lst.append(arr[:T_CTX].reshape(BATCH, SEQ, N_KV, HEAD_DIM))

        sink = a_sink[l]
        oa_ctx, oc_ctx = _context_attention(by_seq(qa), by_seq(ka_b), by_seq(va_b),
                                            by_seq(qc), by_seq(kc_b), by_seq(vc_b), sink)
        cak, cav, cck, ccv = (a[:, l] for a in caches)
        oa_lat = _window_attention(by_dec(qa), by_dec(ka_b), by_dec(va_b), cak, cav, sink,
                                   n_req=DEC_BATCH, off=lat_off)
        oc_lat = _dense_attention(by_dec(qc), by_dec(kc_b), by_dec(vc_b), (cck, ccv), None,
                                  n_req=DEC_BATCH, off=lat_off, tq=1024, key_chunk=1024)

        bs_full = jnp.repeat(b_bs[l].T, B_GROUP_CH, axis=1)
        x1, h2p, afft = _merge(x_ctx, x_lat, oa_ctx.reshape(T_CTX, Q_W), oa_lat.reshape(T_LAT, Q_W), bu, bv,
                               oc_ctx.reshape(T_CTX, Q_W), oc_lat.reshape(T_LAT, Q_W), gt,
                               wa_b, wb_b, wc_b, wo_b, ws_b, bs_full,
                               mods, l, norm2_g[l][None, :], wr_pad, br_pad[l][None, :])

        aff_rows = lambda a, n_req, n: a.reshape(N_EXPERTS, n_req, n).transpose(1, 0, 2).reshape(n_req * N_EXPERTS, n)
        idx_c, val_c = _select(aff_rows(afft[:, :T_CTX], BATCH, SEQ), BATCH * N_EXPERTS, CAP_CTX)
        _, val_l, idx_l_rows = _select(aff_rows(afft[:, T_CTX:], DEC_BATCH, DEC_SEQ), N_EXPERTS, CAP_LAT)
        idx_l_flat = idx_l_rows.reshape(-1)
        xg_ctx = _gather_ctx(idx_c, h2p)
        xg_lat = _gather_lat(idx_l_flat, by_dec(h2p), lat_off)
        yg = _expert_ffn(xg_ctx, xg_lat, val_c.reshape(BATCH, N_EXPERTS, CAP_CTX, 1),
                         val_l.reshape(DEC_BATCH, N_EXPERTS, CAP_LAT, 1), mods, w_gate, w_up, w_down, l)

        x_ctx = _scatter_ctx(idx_c, yg, x1)
        x_lat = _scatter_lat(idx_l_flat, yg, by_dec(x1), lat_off).reshape(T_LAT, D_MODEL)

    y_prompt = x_ctx.reshape(BATCH, SEQ, D_MODEL)
    y_sample = x_lat.reshape(DEC_BATCH, DEC_SEQ, D_MODEL)
    return (y_prompt, y_sample) + tuple(jnp.stack(lst, axis=1) for lst in new_kv)
```

```python
import functools

import jax
import numpy as np
import jax.numpy as jnp
from jax import lax
from jax.experimental import pallas as pl
from jax.experimental.pallas import tpu as pltpu

F32 = jnp.float32
BF16 = jnp.bfloat16
I32 = jnp.int32

D_MODEL = 1024
BATCH = 16
SEQ = 256
DEPTH = 2
DEC_BATCH = 2
DEC_SEQ = 4096
PAST_LEN = 256
GRID_W = 64
HEAD_DIM = 64
N_HEADS = 6
N_KV = 2
N_GRP = N_HEADS // N_KV
B_GROUPS = 4
B_GROUP_CH = 64
B_WIDTH = B_GROUPS * B_GROUP_CH
Q_W = N_HEADS * HEAD_DIM
KV_W = N_KV * HEAD_DIM
QK_W = Q_W + KV_W
N_BRANCH = 3
WINDOW = 128
BLOCK = 128
CHUNK = 128
N_EXPERTS = 16
EXPERT_FF = 1024
CAP_FACTOR = 2
ROPE_THETA = 10000.0
ROPE_FREQS = HEAD_DIM // 4
EPS = 1e-6
IN_WIDTH = 2 * (QK_W + KV_W) + 2 * B_WIDTH + N_BRANCH * D_MODEL

T_CTX = BATCH * SEQ
T_LAT = DEC_BATCH * DEC_SEQ
T_ALL = T_CTX + T_LAT
N_REQ = 1 + DEC_BATCH
CAP_CTX = CAP_FACTOR * SEQ // N_EXPERTS
CAP_LAT = CAP_FACTOR * DEC_SEQ // N_EXPERTS
ROWS_PER_EXPERT = BATCH * CAP_CTX + DEC_BATCH * CAP_LAT

LANE = 128
ROW_TILE = 512
N_CTX_TILES = T_CTX // ROW_TILE
FFN_ROW_TILE = 512
VMEM_LIMIT = 56 * 1024 * 1024
NEG_BIG = -1e30
LOG2_E = 1.4426950408889634

OFF_A = 0
OFF_AV = OFF_A + QK_W
OFF_BU = OFF_AV + KV_W
OFF_BV = OFF_BU + B_WIDTH
OFF_C = OFF_BV + B_WIDTH
OFF_CV = OFF_C + QK_W
OFF_G = OFF_CV + KV_W


def _params(sem, vmem=VMEM_LIMIT):
    return pltpu.CompilerParams(dimension_semantics=sem, vmem_limit_bytes=vmem)


def _sigmoid(x):
    return 1.0 / (1.0 + jnp.exp(-x))


def _gelu_tanh(x):
    return 0.5 * x * (1.0 + jnp.tanh(0.7978845608028654 * (x + 0.044715 * (x * x * x))))


def _split_bf16(x):
    hi = x.astype(BF16)
    lo = (x - hi.astype(F32)).astype(BF16)
    return hi, lo


def _mod_kernel(c_ref, w_ref, b_ref, o_ref):
    c = c_ref[...]
    s_hi, s_lo = _split_bf16(c * _sigmoid(c))
    w_hi, w_lo = _split_bf16(w_ref[0])
    acc = jnp.dot(s_hi, w_hi, preferred_element_type=F32)
    acc += jnp.dot(s_lo, w_hi, preferred_element_type=F32)
    acc += jnp.dot(s_hi, w_lo, preferred_element_type=F32)
    o_ref[0] = acc + b_ref[0]


def _modulation(cond8, w_mod, b_mod):
    n_col = 6 * D_MODEL // D_MODEL
    return pl.pallas_call(
        _mod_kernel,
        grid=(DEPTH, n_col),
        in_specs=[
            pl.BlockSpec((8, D_MODEL), lambda l, j: (0, 0)),
            pl.BlockSpec((1, D_MODEL, D_MODEL), lambda l, j: (l, 0, j)),
            pl.BlockSpec((1, 1, D_MODEL), lambda l, j: (l, 0, j)),
        ],
        out_specs=pl.BlockSpec((1, 8, D_MODEL), lambda l, j: (l, 0, j)),
        out_shape=jax.ShapeDtypeStruct((DEPTH, 8, 6 * D_MODEL), F32),
        compiler_params=_params(("arbitrary", "arbitrary")),
        name="modulation",
    )(cond8, w_mod, b_mod.reshape(DEPTH, 1, 6 * D_MODEL))


def _group_sumsq(y, bd_ref):
    return jnp.dot((y * y).astype(BF16), bd_ref[...], preferred_element_type=F32)


def _pick_pass(i, ctx_ref, lat_ref):
    return jnp.where(i < N_CTX_TILES, ctx_ref[...], lat_ref[...])


def _in_kernel(xc_ref, xl_ref, sc_ref, sh_ref, n1_ref, w_ref, cs_ref, sn_ref, ga_ref, gc_ref, gbv_ref, bd_qk_ref,
               bd_b_ref, qa_ref, ka_ref, va_ref, nka_ref, nva_ref, bu_ref, bv_ref, qc_ref, kc_ref, vc_ref, nkc_ref,
               nvc_ref, gt_ref):
    x = _pick_pass(pl.program_id(0), xc_ref, xl_ref)
    ms = jnp.mean(x * x, axis=-1, keepdims=True)
    h = x * lax.rsqrt(ms + EPS) * n1_ref[...]
    h = h * (1.0 + sc_ref[...]) + sh_ref[...]
    hb = h.astype(BF16).astype(F32)
    tm = x.shape[0]

    def proj(c0, width):
        return jnp.dot(hb, w_ref[:, c0:c0 + width], preferred_element_type=F32)

    cs = jnp.concatenate([cs_ref[...]] * (QK_W // LANE), axis=1)
    sn = jnp.concatenate([sn_ref[...]] * (QK_W // LANE), axis=1)
    lane = lax.broadcasted_iota(I32, (tm, QK_W), 1)
    first_half = (lane & ROPE_FREQS) == 0

    def qk_post(y, gain_ref):
        yn = y * lax.rsqrt(_group_sumsq(y, bd_qk_ref) * (1.0 / HEAD_DIM) + EPS) * gain_ref[...]
        partner = jnp.where(first_half, pltpu.roll(yn, QK_W - ROPE_FREQS, 1), pltpu.roll(yn, ROPE_FREQS, 1))
        return yn * cs + partner * sn

    def mixer(off_qk, off_v, gain_ref, q_ref, k_ref, v_ref, nk_ref, nv_ref):
        y = qk_post(proj(off_qk, QK_W), gain_ref)
        v = proj(off_v, KV_W)
        q_ref[...] = y[:, :Q_W].astype(BF16)
        k_ref[...] = y[:, Q_W:].astype(BF16)
        v_ref[...] = v.astype(BF16)

        nk_ref[...] = y[:, Q_W:]
        nv_ref[...] = v

    mixer(OFF_A, OFF_AV, ga_ref, qa_ref, ka_ref, va_ref, nka_ref, nva_ref)

    bu_ref[...] = _gelu_tanh(proj(OFF_BU, B_WIDTH)).astype(BF16)
    gv = _gelu_tanh(proj(OFF_BV, B_WIDTH))
    gvn = gv * lax.rsqrt(_group_sumsq(gv, bd_b_ref) * (1.0 / B_GROUP_CH) + EPS) * gbv_ref[...]
    bv_ref[...] = gvn.astype(BF16)

    mixer(OFF_C, OFF_CV, gc_ref, qc_ref, kc_ref, vc_ref, nkc_ref, nvc_ref)

    gate_chunk = 512
    for j in range(N_BRANCH * D_MODEL // gate_chunk):
        g = proj(OFF_G + j * gate_chunk, gate_chunk)
        gt_ref[:, j * gate_chunk:(j + 1) * gate_chunk] = _sigmoid(g).astype(BF16)


def _req_of_tile(i):
    return i // N_CTX_TILES


def _ctx_rows(w):
    return pl.BlockSpec((ROW_TILE, w), lambda i: (jnp.minimum(i, N_CTX_TILES - 1), 0))


def _lat_rows(w):
    return pl.BlockSpec((ROW_TILE, w), lambda i: (jnp.maximum(i - N_CTX_TILES, 0), 0))


MOD_SH1, MOD_SC1, MOD_G1, MOD_SH2, MOD_SC2, MOD_G2 = range(6)


def _layer_spec(stacked, layer, buffers=None):
    rest = stacked.shape[1:]
    mode = {} if buffers is None else {"pipeline_mode": pl.Buffered(buffers)}
    return pl.BlockSpec((None,) + rest, lambda *g: (layer,) + (0,) * len(rest), **mode)


def _mod_spec(layer, chunk, req):
    return pl.BlockSpec((None, None, 1, D_MODEL), lambda *g: (layer, req(*g), 0, chunk))


def _block_diag_ones(width, group):
    g = np.arange(width) // group
    return (g[:, None] == g[None, :]).astype(np.float32)


def _input_projection(x_ctx, x_lat, mods, layer, n1, w_in, cs, sn, gain_a, gain_c, gain_bv):
    bd_qk = jnp.asarray(_block_diag_ones(QK_W, HEAD_DIM), BF16)
    bd_b = jnp.asarray(_block_diag_ones(B_WIDTH, B_GROUP_CH), BF16)
    tm = ROW_TILE
    row = lambda w: pl.BlockSpec((tm, w), lambda i: (i, 0))
    full = lambda a: pl.BlockSpec(a.shape, lambda i: (0,) * a.ndim)
    rope = pl.BlockSpec((tm, LANE), lambda i: (_rope_tile(i), 0))
    cache_rows = T_CTX + tm
    spare = lambda w: pl.BlockSpec((tm, w), lambda i: (jnp.minimum(i, N_CTX_TILES), 0))
    mixer_outs = [(Q_W, BF16, T_ALL), (KV_W, BF16, T_ALL), (KV_W, BF16, T_ALL), (KV_W, F32, cache_rows),
                  (KV_W, F32, cache_rows)]
    outs = mixer_outs + [(B_WIDTH, BF16, T_ALL), (B_WIDTH, BF16, T_ALL)] + mixer_outs + [(N_BRANCH * D_MODEL, BF16, T_ALL)]
    return pl.pallas_call(
        _in_kernel,
        grid=(T_ALL // tm,),
        in_specs=[_ctx_rows(D_MODEL), _lat_rows(D_MODEL), _mod_spec(layer, MOD_SC1, _req_of_tile),
                  _mod_spec(layer, MOD_SH1, _req_of_tile), full(n1), _layer_spec(w_in, layer, buffers=1), rope, rope,
                  full(gain_a), full(gain_c), full(gain_bv), full(bd_qk), full(bd_b)],
        out_specs=[row(w) if rows == T_ALL else spare(w) for w, _, rows in outs],
        out_shape=[jax.ShapeDtypeStruct((rows, w), dt) for w, dt, rows in outs],
        compiler_params=_params(("arbitrary",)),
        name="input_projection",
    )(x_ctx, x_lat, mods, mods, n1, w_in, cs, sn, gain_a, gain_c, gain_bv, bd_qk, bd_b)


def _banded_start(carry, banded, qt_scr, heads, lo, hi, tq):
    k_refs, v_refs, biases = banded
    m0, acc0 = carry
    k_loc = jnp.concatenate([r[0, :, lo:hi] for r in k_refs], axis=0)
    vt = jnp.concatenate([r[0].astype(F32).T[lo:hi, :].astype(BF16) for r in v_refs], axis=1)
    vt = jnp.concatenate([vt, jnp.ones((DEN_ROWS, vt.shape[1]), BF16)], axis=0)
    pick = lambda x, j: jnp.concatenate([x[:, g * tq + j * BAND_Q:g * tq + (j + 1) * BAND_Q] for g in range(N_GRP)], axis=1)
    ms, accs = [], []
    for j in range(tq // BAND_Q):
        keys = slice(j * BAND_Q, (j + 1) * BAND_Q + 2 * BLOCK)
        qt_j = jnp.concatenate([qt_scr[h * HEAD_DIM:(h + 1) * HEAD_DIM, j * BAND_Q:(j + 1) * BAND_Q] for h in heads], axis=1)
        s = jnp.dot(k_loc[keys], qt_j, preferred_element_type=F32) + jnp.concatenate([biases[j]] * N_GRP, axis=1)
        m_old = pick(m0, j)
        m_new = jnp.maximum(m_old, jnp.max(s, axis=0, keepdims=True))
        p = jnp.exp2(s - m_new).astype(BF16)
        accs.append(pick(acc0, j) * jnp.exp2(m_old - m_new) + jnp.dot(vt[:, keys], p, preferred_element_type=F32))
        ms.append(m_new)
    gather = lambda parts: jnp.concatenate(
        [parts[j][:, g * BAND_Q:(g + 1) * BAND_Q] for g in range(N_GRP) for j in range(tq // BAND_Q)], axis=1)
    return gather(ms), gather(accs)


def _attention_tile(q_ref, sources, sink_ref, o_ref, qt_scr, ot_scr, *, tq, key_chunk, banded=None):
    width = N_GRP * tq
    for j in range(Q_W // LANE):
        qt_scr[j * LANE:(j + 1) * LANE, :] = q_ref[0, :, j * LANE:(j + 1) * LANE].astype(F32).T.astype(BF16)
    for kv in range(N_KV):
        lo, hi = kv * HEAD_DIM, (kv + 1) * HEAD_DIM
        heads = [kv * N_GRP + g for g in range(N_GRP)]
        qt = jnp.concatenate([qt_scr[h * HEAD_DIM:(h + 1) * HEAD_DIM, :] for h in heads], axis=1)

        def step(carry, kref, vref, c0, size):
            m, acc = carry
            s = jnp.dot(kref[0, pl.ds(c0, size), lo:hi], qt, preferred_element_type=F32)
            vt =vref[0, pl.ds(c0, size), :].astype(F32).T[lo:hi, :].astype(BF16)
            vt = jnp.concatenate([vt, jnp.ones((DEN_ROWS, size), BF16)], axis=0)
            m_new = jnp.maximum(m, jnp.max(s, axis=0, keepdims=True))
            p = jnp.exp2(s - m_new).astype(BF16)
            acc = acc * jnp.exp2(m - m_new) + jnp.dot(vt, p, preferred_element_type=F32)
            return m_new, acc

        if sink_ref is not None:
            m0 = jnp.concatenate([jnp.full((1, tq), sink_ref[h] * LOG2_E, F32) for h in heads], axis=1)
            den0 = jnp.ones((DEN_ROWS, width), F32)
        else:
            m0 = jnp.full((1, width), NEG_BIG, F32)
            den0 = jnp.zeros((DEN_ROWS, width), F32)
        carry = (m0, jnp.concatenate([jnp.zeros((HEAD_DIM, width), F32), den0], axis=0))
        if banded is not None:
            carry = _banded_start(carry, banded, qt_scr, heads, lo, hi, tq)
        for kref, vref in sources:
            n_rows = kref.shape[1]
            n_full = n_rows // key_chunk
            if n_full > 1:
                carry = lax.fori_loop(
                    0, n_full,
                    lambda c, cr: step(cr, kref, vref, pl.multiple_of(c * key_chunk, key_chunk), key_chunk), carry)
            elif n_full == 1:
                carry = step(carry, kref, vref, 0, key_chunk)
            if n_rows - n_full * key_chunk:
                carry = step(carry, kref, vref, n_full * key_chunk, n_rows - n_full * key_chunk)
        _, acc = carry
        o = acc[:HEAD_DIM] / acc[HEAD_DIM:HEAD_DIM + 1]
        for g, h in enumerate(heads):
            ot_scr[h * HEAD_DIM:(h + 1) * HEAD_DIM, :] = o[:, g * tq:(g + 1) * tq]
    for j in range(Q_W // LANE):
        o_ref[0, :, j * LANE:(j + 1) * LANE] = ot_scr[j * LANE:(j + 1) * LANE, :].T.astype(o_ref.dtype)


def _dense_attn_kernel(*refs, tq, key_chunk, has_extra, has_sink):
    refs = list(refs)
    q_ref, k_ref, v_ref = refs[:3]
    del refs[:3]
    sources = [(k_ref, v_ref)]
    if has_extra:
        sources.append((refs.pop(0), refs.pop(0)))
    sink_ref = refs.pop(0) if has_sink else None
    o_ref, qt_scr, ot_scr = refs
    _attention_tile(q_ref, sources, sink_ref, o_ref, qt_scr, ot_scr, tq=tq, key_chunk=key_chunk)


def _attention_scratch(tq):
    return [pltpu.VMEM((Q_W, tq), BF16), pltpu.VMEM((Q_W, tq), F32)]


def _dense_attention(q, k, v, extra, sink, *, n_req, off, tq, key_chunk):
    s = q.shape[1]
    kv_spec = pl.BlockSpec((1, s, KV_W), lambda i, j: (off + i, 0, 0))
    in_specs = [pl.BlockSpec((1, tq, Q_W), lambda i, j: (off + i, j, 0)), kv_spec, kv_spec]
    args = [q, k, v]
    if extra is not None:
        in_specs += [pl.BlockSpec((1, extra[0].shape[1], KV_W), lambda i, j: (i, 0, 0))] * 2
        args += list(extra)
    if sink is not None:
        in_specs.append(pl.BlockSpec(memory_space=pltpu.SMEM))
        args.append(sink)
    return pl.pallas_call(
        functools.partial(_dense_attn_kernel, tq=tq, key_chunk=key_chunk, has_extra=extra is not None,
                          has_sink=sink is not None),
        grid=(n_req, s // tq),
        in_specs=in_specs,
        out_specs=pl.BlockSpec((1, tq, Q_W), lambda i, j: (i, j, 0)),
        out_shape=jax.ShapeDtypeStruct((n_req, s, Q_W), BF16),
        scratch_shapes=_attention_scratch(tq),
        compiler_params=_params(("arbitrary", "arbitrary")),
        name="dense_attention",
    )(*args)


CTX_ATTN_PER_STEP = 2


def _ctx_attn_kernel(qa_ref, ka_ref, va_ref, qc_ref, kc_ref, vc_ref, sink_ref, oa_ref, oc_ref, *scratch):
    for r in range(CTX_ATTN_PER_STEP):
        one = lambda ref: ref.at[pl.ds(r, 1)]
        qta, ota, qtc, otc = scratch[4 * r:4 * r + 4]
        _attention_tile(one(qa_ref), [(one(ka_ref), one(va_ref))], sink_ref, one(oa_ref), qta, ota,
                        tq=SEQ, key_chunk=SEQ)
        _attention_tile(one(qc_ref), [(one(kc_ref), one(vc_ref))], None, one(oc_ref), qtc, otc,
                        tq=SEQ, key_chunk=SEQ)


def _context_attention(qa, ka, va, qc, kc, vc, sink):
    n = CTX_ATTN_PER_STEP
    q_spec = pl.BlockSpec((n, SEQ, Q_W), lambda i: (i, 0, 0))
    kv_spec = pl.BlockSpec((n, SEQ, KV_W), lambda i: (i, 0, 0))
    return pl.pallas_call(
        _ctx_attn_kernel,
        grid=(BATCH // n,),
        in_specs=[q_spec, kv_spec, kv_spec, q_spec, kv_spec, kv_spec, pl.BlockSpec(memory_space=pltpu.SMEM)],
        out_specs=[q_spec, q_spec],
        out_shape=[jax.ShapeDtypeStruct((BATCH, SEQ, Q_W), BF16)] * 2,
        scratch_shapes=_attention_scratch(SEQ) * (2 * n),
        compiler_params=_params(("arbitrary",)),
        name="context_attention",
    )(qa, ka, va, qc, kc, vc, sink)


DEN_ROWS = 16
WINDOW_TQ = 512
BAND_Q = 2 * BLOCK


def _window_attn_kernel(q_ref, kp_ref, kc_ref, kn_ref, vp_ref, vc_ref, vn_ref, ck_ref, cv_ref, band_ref, sink_ref, o_ref,
                        qt_scr, ot_scr, *, seq):
    q_pos0 = pl.program_id(1) * WINDOW_TQ
    n_grp = WINDOW_TQ // BAND_Q
    band = band_ref[...]
    hide_prev = jnp.where(q_pos0 >= BLOCK, 0.0, NEG_BIG)
    hide_next = jnp.where(q_pos0 + WINDOW_TQ < seq, 0.0, NEG_BIG)
    first = jnp.concatenate([band[:BLOCK] + hide_prev, band[BLOCK:]], axis=0)
    last = jnp.concatenate([band[:BAND_Q + BLOCK], band[BAND_Q + BLOCK:] + hide_next], axis=0)
    biases = [first] + [band] * (n_grp - 2) + [last]
    banded = ((kp_ref, kc_ref, kn_ref), (vp_ref, vc_ref, vn_ref), biases)
    _attention_tile(q_ref, [(ck_ref, cv_ref)], sink_ref, o_ref, qt_scr, ot_scr, tq=WINDOW_TQ,
                    key_chunk=PAST_LEN, banded=banded)


def _band_bias():
    d = (np.arange(BAND_Q + 2 * BLOCK) - BLOCK)[:, None] - np.arange(BAND_Q)[None, :]
    return np.where(np.abs(d) <= WINDOW, 0.0, NEG_BIG).astype(np.float32)


def _window_attention(q, k, v, ck, cv, sink, *, n_req, off):
    b, s = n_req, q.shape[1]
    nb = s // BLOCK
    per_tile = WINDOW_TQ // BLOCK
    edge = lambda f: pl.BlockSpec((1, BLOCK, KV_W), lambda i, j: (off + i, f(j), 0))
    prev = lambda j: jnp.maximum(j * per_tile - 1, 0)
    nxt = lambda j: jnp.minimum((j + 1) * per_tile, nb - 1)
    cur = pl.BlockSpec((1, WINDOW_TQ, KV_W), lambda i, j: (off + i, j, 0))
    ctx = pl.BlockSpec((1, PAST_LEN, KV_W), lambda i, j: (i, 0, 0))
    assert WINDOW_TQ // BAND_Q >= 2
    band = _band_bias()
    return pl.pallas_call(
        functools.partial(_window_attn_kernel, seq=s),
        grid=(b, s // WINDOW_TQ),
        in_specs=[pl.BlockSpec((1, WINDOW_TQ, Q_W), lambda i, j: (off + i, j, 0)),
                  edge(prev), cur, edge(nxt), edge(prev), cur, edge(nxt), ctx, ctx,
                  pl.BlockSpec(band.shape, lambda i, j: (0, 0)),
                  pl.BlockSpec(memory_space=pltpu.SMEM)],
        out_specs=pl.BlockSpec((1, WINDOW_TQ, Q_W), lambda i, j: (i, j, 0)),
        out_shape=jax.ShapeDtypeStruct((b, s, Q_W), BF16),
        scratch_shapes=_attention_scratch(WINDOW_TQ),
        compiler_params=_params(("arbitrary", "arbitrary")),
        name="window_attention",
    )(q, k, k, k, v, v, v, ck, cv, band, sink)


def _pack_halves(x):
    half = x.shape[1] // 2
    return pltpu.pack_elementwise([x[:, :half], x[:, half:]], packed_dtype=BF16)


def _unpack_halves(words):
    return tuple(pltpu.unpack_elementwise(words, index=i, packed_dtype=BF16, unpacked_dtype=F32).astype(BF16)
                 for i in range(2))


def _merge_kernel(xc_ref, xl_ref, oac_ref, oal_ref, bu_ref, bv_ref, occ_ref, ocl_ref, gt_ref, wa_ref, wb_ref, wc_ref,
                  wo_ref, ws_ref, bs_ref, g1_ref, sc2_ref, sh2_ref, n2_ref, wr_ref, br_ref, x1_ref, h2p_ref, afft_ref):
    i = pl.program_id(0)
    tm = xc_ref.shape[0]
    group = lax.broadcasted_iota(I32, (CHUNK, B_WIDTH), 1) // B_GROUP_CH
    obs = []
    for c in range(tm // CHUNK):
        v = bv_ref[c * CHUNK:(c + 1) * CHUNK, :]
        sv = jnp.zeros((CHUNK, B_WIDTH), F32)
        for g in range(B_GROUPS):
            sv = jnp.where(group == g, jnp.dot(ws_ref[g], v, preferred_element_type=F32), sv)
        u = bu_ref[c * CHUNK:(c + 1) * CHUNK, :].astype(F32)
        obs.append((u * (sv + bs_ref[...])).astype(BF16))
    ob = jnp.concatenate(obs, axis=0)

    oa = _pick_pass(i, oac_ref, oal_ref)
    oc = _pick_pass(i, occ_ref, ocl_ref)
    merged = gt_ref[:, 0:D_MODEL].astype(F32) * jnp.dot(oa, wa_ref[...], preferred_element_type=F32)
    merged += gt_ref[:, D_MODEL:2 * D_MODEL].astype(F32) * jnp.dot(ob, wb_ref[...], preferred_element_type=F32)
    merged += gt_ref[:, 2 * D_MODEL:3 * D_MODEL].astype(F32) * jnp.dot(oc, wc_ref[...], preferred_element_type=F32)
    y = jnp.dot(merged.astype(BF16), wo_ref[...], preferred_element_type=F32)
    x1 = _pick_pass(i, xc_ref, xl_ref) + g1_ref[...] * y
    x1_ref[...] = x1

    ms = jnp.mean(x1 * x1, axis=-1, keepdims=True)
    h2 = x1 * lax.rsqrt(ms + EPS) * n2_ref[...]
    h2 = h2 * (1.0 + sc2_ref[...]) + sh2_ref[...]
    h2p_ref[...] = _pack_halves(h2)

    logits = jnp.dot(h2.astype(BF16), wr_ref[...], preferred_element_type=F32) + br_ref[...]
    e = jnp.exp(logits - jnp.max(logits, axis=-1, keepdims=True))
    aff = e / jnp.sum(e, axis=-1, keepdims=True)
    afft_ref[...] = aff.T[:N_EXPERTS, :]


def _merge(x_ctx, x_lat, oa_ctx, oa_lat, bu, bv, oc_ctx, oc_lat, gt, wa, wb, wc, wo, ws, bs, mods, layer, n2, wr, br):
    tm = ROW_TILE
    row = lambda w: pl.BlockSpec((tm, w), lambda i: (i, 0))
    full = lambda a: pl.BlockSpec(a.shape, lambda i: (0,) * a.ndim)
    mod = lambda chunk: _mod_spec(layer, chunk, _req_of_tile)
    stack = lambda a: _layer_spec(a, layer)
    return pl.pallas_call(
        _merge_kernel,
        grid=(T_ALL // tm,),
        in_specs=[_ctx_rows(D_MODEL), _lat_rows(D_MODEL), _ctx_rows(Q_W), _lat_rows(Q_W), row(B_WIDTH), row(B_WIDTH),
                  _ctx_rows(Q_W), _lat_rows(Q_W), row(N_BRANCH * D_MODEL),
                  stack(wa), stack(wb), stack(wc), stack(wo), stack(ws), full(bs),
                  mod(MOD_G1), mod(MOD_SC2), mod(MOD_SH2), full(n2), stack(wr), full(br)],
        out_specs=[row(D_MODEL), row(D_MODEL // 2), pl.BlockSpec((N_EXPERTS, tm), lambda i: (0, i))],
        out_shape=[jax.ShapeDtypeStruct((T_ALL, D_MODEL), F32), jax.ShapeDtypeStruct((T_ALL, D_MODEL // 2), jnp.uint32),
                   jax.ShapeDtypeStruct((N_EXPERTS, T_ALL), F32)],
        compiler_params=_params(("arbitrary",)),
        name="merge_router",
    )(x_ctx, x_lat, oa_ctx, oa_lat, bu, bv, oc_ctx, oc_lat, gt, wa, wb, wc, wo, ws, bs, mods, mods, mods, n2, wr, br)


def _select_kernel(aff_ref, idx_ref, val_ref, *rest, n, cap, row_chunk):
    idx_row_ref = rest[0] if len(rest) == 4 else None
    possel_ref, idx_scr, val_scr = rest[-3:]
    a = aff_ref[...]
    rows = a.shape[0]
    tok = lax.broadcasted_iota(I32, (rows, n), 1)

    def count(ones):
        return jnp.sum(ones, axis=1, keepdims=True)

    def at_least(word):
        return jnp.where(a >= pltpu.bitcast(word, F32), 1, 0)

    def greedy_bits(start, top_bit, keep):
        word = start
        bits = list(range(top_bit, -1, -1))
        if rows > SEARCH_PAIR_MAX_ROWS:
            for bit in bits:
                cand = word | (1 << bit)
                word = jnp.where(keep(cand), cand, word)
            return word
        if len(bits) % 2:
            cand = word | (1 << bits[0])
            word = jnp.where(keep(cand), cand, word)
            bits = bits[1:]
        for hi, lo in zip(bits[0::2], bits[1::2]):
            c_lo, c_hi, c_both = word | (1 << lo), word | (1 << hi), word | (1 << hi) | (1 << lo)
            word = jnp.where(keep(c_both), c_both, jnp.where(keep(c_hi), c_hi, jnp.where(keep(c_lo), c_lo, word)))
        return word

    thr = greedy_bits(jnp.zeros((rows, 1), I32), 30, lambda w: count(at_least(w)) >= cap)
    above = at_least(thr + 1)
    tied = at_least(thr) - above
    need = cap - count(above)
    last = greedy_bits(jnp.zeros((rows, 1), I32), n.bit_length() - 2,
                       lambda w: count(jnp.where(tok < w, tied, 0)) < need)
    sel = above + jnp.where(tok <= last, tied, 0)

    blk = min(n, 256)
    tri = jnp.where(lax.broadcasted_iota(I32, (blk, blk), 0) <= lax.broadcasted_iota(I32, (blk, blk), 1),
                    1.0, 0.0).astype(BF16)
    sel_f = sel.astype(F32)
    offset = jnp.zeros((rows, 1), F32)
    for j in range(n // blk):
        s_blk = sel_f[:, j * blk:(j + 1) * blk]
        incl = jnp.dot(s_blk.astype(BF16), tri, preferred_element_type=F32)
        pos = (incl - s_blk + offset).astype(I32)
        possel_ref[:, j * blk:(j + 1) * blk] = jnp.where(sel[:, j * blk:(j + 1) * blk] > 0, pos, -1)
        offset = offset + incl[:, blk - 1:blk]

    tb = min(n, TOKEN_BLOCK)
    n_blk = n // tb

    def fold_lanes(x):
        acc = x[:, :LANE]
        for k in range(1, tb // LANE):
            acc = acc + x[:, k * LANE:(k + 1) * LANE]
        return acc

    def match(e, slot, t0):
        hit = possel_ref[pl.ds(e, 1), pl.ds(t0, tb)] == slot
        tok = t0 + lax.broadcasted_iota(I32, (1, tb), 1)
        return (fold_lanes(jnp.where(hit, tok, 0)),
                fold_lanes(jnp.where(hit, aff_ref[pl.ds(e, 1), pl.ds(t0, tb)], 0.0)))

    def per_row(e, _):
        ends, run = [], 0
        for j in range(n_blk - 1):
            run = run + jnp.sum(jnp.where(possel_ref[pl.ds(e, 1), j * tb:(j + 1) * tb] >= 0, 1, 0))
            ends.append(run)

        def per_chunk(c, _):
            r0 = pl.multiple_of(c * row_chunk, row_chunk)
            slot = lax.broadcasted_iota(I32, (row_chunk, 1), 0) + r0
            if n_blk == 1:
                idx, val = match(e, slot, 0)
            else:
                first = sum(jnp.where(end <= r0, 1, 0) for end in ends)
                last = 1 + sum(jnp.where(end < r0 + row_chunk, 1, 0) for end in ends)

                def per_block(j, acc):
                    i, v = match(e, slot, pl.multiple_of(j * tb, tb))
                    return acc[0] + i, acc[1] + v

                idx, val = lax.fori_loop(first, last, per_block,
                                         (jnp.zeros((row_chunk, LANE), I32), jnp.zeros((row_chunk, LANE), F32)))
            idx_scr[pl.ds(r0, row_chunk), :] = idx
            val_scr[pl.ds(r0, row_chunk), :] = val
            return 0

        lax.fori_loop(0, cap // row_chunk, per_chunk, 0)
        idx = jnp.sum(idx_scr[...], axis=1, keepdims=True)
        idx_ref[e] = idx
        val_ref[e] = jnp.sum(val_scr[...], axis=1, keepdims=True)
        if idx_row_ref is not None:
            idx_row_ref[pl.ds(e, 1), :] = jnp.broadcast_to(idx.astype(F32), (cap, LANE)).T[0:1, :].astype(I32)
        return 0

    def per_small_row(e, _):
        idx, val = match(e, lax.broadcasted_iota(I32, (cap, 1), 0), 0)
        idx_ref[e] = jnp.sum(idx, axis=1, keepdims=True)
        val_ref[e] = jnp.sum(val, axis=1, keepdims=True)
        return 0

    if n_blk == 1 and cap == row_chunk:
        lax.fori_loop(0, rows, per_small_row, 0, unroll=4)
    else:
        lax.fori_loop(0, rows, per_row, 0)


def _select(aff_rows, rows_per_step, cap):
    r, n = aff_rows.shape
    row_chunk = min(cap, 64)
    out_specs = [pl.BlockSpec((rows_per_step, cap, 1), lambda s: (s, 0, 0))] * 2
    out_shape = [jax.ShapeDtypeStruct((r, cap, 1), I32), jax.ShapeDtypeStruct((r, cap, 1), F32)]
    if cap % LANE == 0:
        out_specs.append(pl.BlockSpec((rows_per_step, cap), lambda s: (s, 0)))
        out_shape.append(jax.ShapeDtypeStruct((r, cap), I32))
    return pl.pallas_call(
        functools.partial(_select_kernel, n=n, cap=cap, row_chunk=row_chunk),
        grid=(r // rows_per_step,),
        in_specs=[pl.BlockSpec((rows_per_step, n), lambda s: (s, 0))],
        out_specs=out_specs,
        out_shape=out_shape,
        scratch_shapes=[pltpu.VMEM((rows_per_step, n), I32), pltpu.VMEM((cap, LANE), I32), pltpu.VMEM((cap, LANE), F32)],
        compiler_params=_params(("arbitrary",)),
        name="expert_select",
    )(aff_rows)


CTX_SLOTS = N_EXPERTS * CAP_CTX
TOKEN_BLOCK = 512
SEARCH_PAIR_MAX_ROWS = 32
SLOT_GROUP = 16


CTX_PER_STEP = 4


def _ctx_slot_onehot(idx, slots_on_rows):
    idx = idx.reshape(CTX_SLOTS, 1)
    if slots_on_rows:
        hit = idx == lax.broadcasted_iota(I32, (CTX_SLOTS, SEQ), 1)
    else:
        idx_lane = jnp.broadcast_to(idx.astype(F32), (CTX_SLOTS, LANE)).T[0:1, :]
        hit = idx_lane == lax.broadcasted_iota(I32, (SEQ, CTX_SLOTS), 0).astype(F32)
    return jnp.where(hit, 1.0, 0.0).astype(BF16)


def _gather_ctx_kernel(idx_ref, h_ref, out_ref):
    for r in range(CTX_PER_STEP):
        onehot = _ctx_slot_onehot(idx_ref[r * N_EXPERTS:(r + 1) * N_EXPERTS], True)
        lo, hi = _unpack_halves(h_ref[r * SEQ:(r + 1) * SEQ, :])
        g_lo = jnp.dot(onehot, lo, preferred_element_type=F32)
        g_hi = jnp.dot(onehot, hi, preferred_element_type=F32)
        packed = pltpu.pack_elementwise([g_lo, g_hi], packed_dtype=BF16)
        out_ref[:, r * CAP_CTX:(r + 1) * CAP_CTX, :] = packed.reshape(N_EXPERTS, CAP_CTX, D_MODEL // 2)


def _gather_ctx(idx_c, h2p):
    n = CTX_PER_STEP
    return pl.pallas_call(
        _gather_ctx_kernel,
        grid=(BATCH // n,),
        in_specs=[pl.BlockSpec((n * N_EXPERTS, CAP_CTX, 1), lambda b: (b, 0, 0)),
                  pl.BlockSpec((n * SEQ, D_MODEL // 2), lambda b: (b, 0))],
        out_specs=pl.BlockSpec((N_EXPERTS, n * CAP_CTX, D_MODEL // 2), lambda b: (0, b, 0)),
        out_shape=jax.ShapeDtypeStruct((N_EXPERTS, BATCH * CAP_CTX, D_MODEL // 2), jnp.uint32),
        compiler_params=_params(("arbitrary",)),
        name="gather_ctx",
    )(idx_c, h2p)


def _gather_lat_kernel(idx_ref, src_hbm, out_ref, sem, *, off):
    b = pl.program_id(0)
    base = (b * N_EXPERTS + pl.program_id(1)) * CAP_LAT

    def row_copy(row, slot):
        return pltpu.make_async_copy(src_hbm.at[off + b, pl.ds(row, 1)], out_ref.at[0, pl.ds(slot, 1)], sem)

    def start(it, _):
        r0 = pl.multiple_of(it * SLOT_GROUP, SLOT_GROUP)
        for k in range(SLOT_GROUP):
            row_copy(idx_ref[base + r0 + k], r0 + k).start(priority=k % 2)
        return 0

    lax.fori_loop(0, CAP_LAT // SLOT_GROUP, start, 0)

    def wait(it, _):
        r0 = pl.multiple_of(it * SLOT_GROUP, SLOT_GROUP)
        for k in range(SLOT_GROUP):
            row_copy(0, r0 + k).wait()
        return 0

    lax.fori_loop(0, CAP_LAT // SLOT_GROUP, wait, 0)


def _gather_lat(idx_flat, h2p3, off):
    return pl.pallas_call(
        functools.partial(_gather_lat_kernel, off=off),
        grid_spec=pltpu.PrefetchScalarGridSpec(
            num_scalar_prefetch=1,
            grid=(DEC_BATCH, N_EXPERTS),
            in_specs=[pl.BlockSpec(memory_space=pl.ANY)],
            out_specs=pl.BlockSpec((1, CAP_LAT, D_MODEL // 2), lambda b, e, idx: (e, b, 0)),
            scratch_shapes=[pltpu.SemaphoreType.DMA(())],
        ),
        out_shape=jax.ShapeDtypeStruct((N_EXPERTS, DEC_BATCH * CAP_LAT, D_MODEL // 2), jnp.uint32),
        compiler_params=_params(("arbitrary", "arbitrary")),
        name="gather_lat",
    )(idx_flat, h2p3)


N_CTX_FFN_TILES = BATCH * CAP_CTX // FFN_ROW_TILE


def _ffn_kernel(xc_ref, xl_ref, vc_ref, vl_ref, g2_ref, wg_hbm, wu_hbm, wd_hbm, o_ref, wbuf, sems, *, layer):
    e, j = pl.program_id(0), pl.program_id(1)
    slot = e % 2
    weights = (wg_hbm, wu_hbm, wd_hbm)

    def fetch(m, expert, into):
        return pltpu.make_async_copy(weights[m].at[layer, expert], wbuf.at[into, m], sems.at[into, m])

    @pl.when(jnp.logical_and(e == 0, j == 0))
    def _():
        for m in range(3):
            fetch(m, 0, 0).start()
        for m in range(3):
            fetch(m, 0, 0).wait()

    for m in range(3):
        @pl.when(jnp.logical_and(j == m, e + 1 < N_EXPERTS))
        def _(m=m):
            fetch(m, e + 1, 1 - slot).start()

    @pl.when(jnp.logical_and(j == 0, e > 0))
    def _():
        for m in range(3):
            fetch(m, e, slot).wait()

    is_ctx = j < N_CTX_FFN_TILES
    x = jnp.where(is_ctx, jnp.concatenate(_unpack_halves(xc_ref[0]), axis=1),
                  jnp.concatenate(_unpack_halves(xl_ref[0]), axis=1)).astype(F32)
    g = jnp.dot(x, wbuf[slot, 0], preferred_element_type=F32)
    u = jnp.dot(x, wbuf[slot, 1], preferred_element_type=F32)
    hh = (g * _sigmoid(g)) * u
    y = jnp.dot(hh.astype(BF16).astype(F32), wbuf[slot, 2], preferred_element_type=F32)
    o_ref[0] = (y * jnp.where(is_ctx, vc_ref[...].reshape(FFN_ROW_TILE, 1), vl_ref[...])) * g2_ref[...]


def _expert_ffn(xg_ctx, xg_lat, val_ctx, val_lat, mods, w_gate, w_up, w_down, layer):
    tr = FFN_ROW_TILE
    assert tr == BATCH * CAP_CTX == CAP_LAT
    n_tiles = ROWS_PER_EXPERT // tr
    assert n_tiles >= 3 and D_MODEL == EXPERT_FF
    in_hbm = pl.BlockSpec(memory_space=pl.ANY)

    ctx_tile = lambda j: jnp.minimum(j, N_CTX_FFN_TILES - 1)
    lat_tile = lambda j: jnp.maximum(j - N_CTX_FFN_TILES, 0)
    return pl.pallas_call(
        functools.partial(_ffn_kernel, layer=layer),
        grid=(N_EXPERTS, n_tiles),
        in_specs=[pl.BlockSpec((1, tr, D_MODEL // 2), lambda e, j: (e, ctx_tile(j), 0)),
                  pl.BlockSpec((1, tr, D_MODEL // 2), lambda e, j: (e, lat_tile(j), 0)),
                  pl.BlockSpec((BATCH, None, CAP_CTX, 1), lambda e, j: (0, e, 0, 0)),
                  pl.BlockSpec((None, None, CAP_LAT, 1), lambda e, j: (lat_tile(j), e, 0, 0)),
                  _mod_spec(layer, MOD_G2, lambda e, j: j),
                  in_hbm, in_hbm, in_hbm],
        out_specs=pl.BlockSpec((1, tr, D_MODEL), lambda e, j: (e, j, 0)),
        out_shape=jax.ShapeDtypeStruct((N_EXPERTS, ROWS_PER_EXPERT, D_MODEL), F32),
        scratch_shapes=[pltpu.VMEM((2, 3, D_MODEL, EXPERT_FF), F32), pltpu.SemaphoreType.DMA((2, 3))],
        compiler_params=_params(("arbitrary", "arbitrary")),
        name="expert_ffn",
    )(xg_ctx, xg_lat, val_ctx, val_lat, mods, w_gate, w_up, w_down)


def _scatter_ctx_kernel(idx_ref, y_ref, x1_ref, out_ref):
    for r in range(CTX_PER_STEP):
        onehot = _ctx_slot_onehot(idx_ref[r * N_EXPERTS:(r + 1) * N_EXPERTS], False)
        y_hi, y_lo = _split_bf16(y_ref[:, r * CAP_CTX:(r + 1) * CAP_CTX, :].reshape(CTX_SLOTS, D_MODEL))
        moe = jnp.dot(onehot, y_hi, preferred_element_type=F32) + jnp.dot(onehot, y_lo, preferred_element_type=F32)
        out_ref[r * SEQ:(r + 1) * SEQ, :] = x1_ref[r * SEQ:(r + 1) * SEQ, :] + moe


def _scatter_ctx(idx_c, yg, x1):
    n = CTX_PER_STEP
    return pl.pallas_call(
        _scatter_ctx_kernel,
        grid=(BATCH // n,),
        in_specs=[pl.BlockSpec((n * N_EXPERTS, CAP_CTX, 1), lambda b: (b, 0, 0)),
                  pl.BlockSpec((N_EXPERTS, n * CAP_CTX, D_MODEL), lambda b: (0, b, 0)),
                  pl.BlockSpec((n * SEQ, D_MODEL), lambda b: (b, 0))],
        out_specs=pl.BlockSpec((n * SEQ, D_MODEL), lambda b: (b, 0)),
        out_shape=jax.ShapeDtypeStruct((T_CTX, D_MODEL), F32),
        compiler_params=_params(("arbitrary",)),
        name="scatter_ctx",
    )(idx_c, yg, x1)


def _scatter_lat_kernel(idx_ref, y_ref, x1_hbm, out_ref, sem, *, off):
    b, e = pl.program_id(0), pl.program_id(2)

    @pl.when(e == 0)
    def _():
        load = pltpu.make_async_copy(x1_hbm.at[pl.ds(off + b, 1)], out_ref, sem)
        load.start()
        load.wait()

    base = (b * N_EXPERTS + e) * CAP_LAT

    def body(it, _):
        r0 = pl.multiple_of(it * SLOT_GROUP, SLOT_GROUP)
        rows = [idx_ref[base + r0 + k] for k in range(SLOT_GROUP)]
        old = [out_ref[0, pl.ds(rows[k], 1), :] for k in range(SLOT_GROUP)]
        y = y_ref[0, pl.ds(r0, SLOT_GROUP), :]
        for k in range(SLOT_GROUP):
            out_ref[0, pl.ds(rows[k], 1), :] = old[k] + y[k:k + 1, :]
        return 0

    lax.fori_loop(0, CAP_LAT // SLOT_GROUP, body, 0)


def _scatter_lat(idx_flat, yg, x1_3, off):
    blk0 = BATCH * CAP_CTX // CAP_LAT
    return pl.pallas_call(
        functools.partial(_scatter_lat_kernel, off=off),
        grid_spec=pltpu.PrefetchScalarGridSpec(
            num_scalar_prefetch=1,
            grid=(DEC_BATCH, 1, N_EXPERTS),
            in_specs=[pl.BlockSpec((1, CAP_LAT, D_MODEL), lambda b, h, e, idx: (e, blk0 + b, 0)),
                      pl.BlockSpec(memory_space=pl.ANY)],
            out_specs=pl.BlockSpec((1, DEC_SEQ, D_MODEL), lambda b, h, e, idx: (b, 0, 0)),
            scratch_shapes=[pltpu.SemaphoreType.DMA(())],
        ),
        out_shape=jax.ShapeDtypeStruct((DEC_BATCH, DEC_SEQ, D_MODEL), F32),
        compiler_params=_params(("arbitrary", "arbitrary", "arbitrary")),
        name="scatter_lat",
    )(idx_flat, yg, x1_3)


def _rope_tables():
    pos = np.arange(DEC_SEQ)
    freq = (np.float32(ROPE_THETA) ** (-np.arange(ROPE_FREQS, dtype=np.float32) / np.float32(ROPE_FREQS)))
    ang_r = (pos // GRID_W).astype(np.float32)[:, None] * freq.astype(np.float32)
    ang_c = (pos % GRID_W).astype(np.float32)[:, None] * freq.astype(np.float32)
    cos = np.concatenate([np.cos(ang_r)] * 2 + [np.cos(ang_c)] * 2, axis=-1)
    sin = np.concatenate([-np.sin(ang_r), np.sin(ang_r), -np.sin(ang_c), np.sin(ang_c)], axis=-1)
    reps = LANE // HEAD_DIM
    cs = np.concatenate([np.ones((ROW_TILE, LANE)), np.tile(cos, (1, reps))], axis=0).astype(np.float32)
    sn = np.concatenate([np.zeros((ROW_TILE, LANE)), np.tile(sin, (1, reps))], axis=0).astype(np.float32)
    return cs, sn


def _rope_tile(i):
    lat = jnp.maximum(i - N_CTX_TILES, 0) % (DEC_SEQ // ROW_TILE)
    return jnp.where(i < N_CTX_TILES, 0, 1 + lat)


def _qk_gain(q_norm, k_norm):
    q = jnp.tile(q_norm, N_HEADS) * (HEAD_DIM ** -0.5 * LOG2_E)
    return jnp.concatenate([q, jnp.tile(k_norm, N_KV)])[None, :]


def kernel(x_prompt, x_sample, cache_a_k, cache_a_v, cache_c_k, cache_c_v, c, c_ctx, norm1_g, w_mod, b_mod, w_in,
           a_q_norm, a_k_norm, a_sink, b_v_norm, b_ws, b_bs, c_q_norm, c_k_norm, w_a_o, w_b_o, w_c_o, w_out, norm2_g,
           w_router, b_router, w_gate, w_up, w_down):
    cond8 = jnp.concatenate([c_ctx[None, :], c, jnp.zeros((8 - N_REQ, D_MODEL), F32)], axis=0)
    mods = _modulation(cond8, w_mod, b_mod).reshape(DEPTH, 8, 1, 6 * D_MODEL)

    cs, sn = _rope_tables()
    wa_b, wb_b, wc_b, wo_b = w_a_o.astype(BF16), w_b_o.astype(BF16), w_c_o.astype(BF16), w_out.astype(BF16)
    ws_b = b_ws.astype(BF16)
    wr_pad = jnp.pad(w_router, ((0, 0), (0, 0), (0, LANE - N_EXPERTS))).astype(BF16)
    br_pad = jnp.pad(b_router, ((0, 0), (0, LANE - N_EXPERTS)), constant_values=NEG_BIG)

    by_seq = lambda a: a.reshape(T_ALL // SEQ, SEQ, a.shape[-1])
    by_dec = lambda a: a.reshape(T_ALL // DEC_SEQ, DEC_SEQ, a.shape[-1])
    lat_off = T_CTX // DEC_SEQ

    caches = [a.reshape(DEC_BATCH, DEPTH, PAST_LEN, KV_W).astype(BF16)
              for a in (cache_a_k, cache_a_v, cache_c_k, cache_c_v)]

    x_ctx = x_prompt.reshape(T_CTX, D_MODEL)
    x_lat = x_sample.reshape(T_LAT, D_MODEL)
    new_kv = [[], [], [], []]
    for l in range(DEPTH):
        qa, ka_b, va_b, nka, nva, bu, bv, qc, kc_b, vc_b, nkc, nvc, gt = _input_projection(
            x_ctx, x_lat, mods, l, norm1_g[l][None, :], w_in, cs, sn,
            _qk_gain(a_q_norm[l], a_k_norm[l]), _qk_gain(c_q_norm[l], c_k_norm[l]), b_v_norm[l][None, :])
        for lst, arr in zip(new_kv, (nka, nva, nkc, nvc)):
            lst.append(arr[:T_CTX].reshape(BATCH, SEQ, N_KV, HEAD_DIM))

        sink = a_sink[l]
        oa_ctx, oc_ctx = _context_attention(by_seq(qa), by_seq(ka_b), by_seq(va_b),
                                            by_seq(qc), by_seq(kc_b), by_seq(vc_b), sink)
        cak, cav, cck, ccv = (a[:, l] for a in caches)
        oa_lat = _window_attention(by_dec(qa), by_dec(ka_b), by_dec(va_b), cak, cav, sink,
                                   n_req=DEC_BATCH, off=lat_off)
        oc_lat = _dense_attention(by_dec(qc), by_dec(kc_b), by_dec(vc_b), (cck, ccv), None,
                                  n_req=DEC_BATCH, off=lat_off, tq=1024, key_chunk=1024)

        bs_full = jnp.repeat(b_bs[l].T, B_GROUP_CH, axis=1)
        x1, h2p, afft = _merge(x_ctx, x_lat, oa_ctx.reshape(T_CTX, Q_W), oa_lat.reshape(T_LAT, Q_W), bu, bv,
                               oc_ctx.reshape(T_CTX, Q_W), oc_lat.reshape(T_LAT, Q_W), gt,
                               wa_b, wb_b, wc_b, wo_b, ws_b, bs_full,
                               mods, l, norm2_g[l][None, :], wr_pad, br_pad[l][None, :])

        aff_rows = lambda a, n_req, n: a.reshape(N_EXPERTS, n_req, n).transpose(1, 0, 2).reshape(n_req * N_EXPERTS, n)
        idx_c, val_c = _select(aff_rows(afft[:, :T_CTX], BATCH, SEQ), BATCH * N_EXPERTS, CAP_CTX)
        _, val_l, idx_l_rows = _select(aff_rows(afft[:, T_CTX:], DEC_BATCH, DEC_SEQ), N_EXPERTS, CAP_LAT)
        idx_l_flat = idx_l_rows.reshape(-1)
        xg_ctx = _gather_ctx(idx_c, h2p)
        xg_lat = _gather_lat(idx_l_flat, by_dec(h2p), lat_off)
        yg = _expert_ffn(xg_ctx, xg_lat, val_c.reshape(BATCH, N_EXPERTS, CAP_CTX, 1),
                         val_l.reshape(DEC_BATCH, N_EXPERTS, CAP_LAT, 1), mods, w_gate, w_up, w_down, l)

        x_ctx = _scatter_ctx(idx_c, yg, x1)
        x_lat = _scatter_lat(idx_l_flat, yg, by_dec(x1), lat_off).reshape(T_LAT, D_MODEL)

    y_prompt = x_ctx.reshape(BATCH, SEQ, D_MODEL)
    y_sample = x_lat.reshape(DEC_BATCH, DEC_SEQ, D_MODEL)
    return (y_prompt, y_sample) + tuple(jnp.stack(lst, axis=1) for lst in new_kv)
```
